```python
import jax, jax.numpy as jnp
from jax import lax
import numpy as np

D_MODEL = 2048
BATCH = 8
SEQ = 8192
DEPTH = 1

N_SUB = 3
D_FF = 5632
POOL_WINDOWS = (2, 4, 8, 16)
POOL_GROUPS = len(POOL_WINDOWS)
POOL_GROUP_W = D_MODEL // 8
POOL_W = POOL_GROUPS * POOL_GROUP_W
HEAD_DIM = 64
N_HEADS = 16
N_KV_HEADS = 2
GQA_GROUP = N_HEADS // N_KV_HEADS
WINDOW = 128
BLK = 128
NUM_BUCKETS = 32
MAX_EXACT = NUM_BUCKETS // 2
REL_MAX_DIST = 128
EPS = 1e-6
NEG_INF = -1e30
IN_SPLITS = (POOL_W, N_HEADS * HEAD_DIM, N_KV_HEADS * HEAD_DIM, N_KV_HEADS * HEAD_DIM, D_MODEL, D_MODEL)
IN_W = sum(IN_SPLITS)

kernel_name = "hybrid_pool_swa_gated_macaron_block"


def rms_norm(x, g):
    xf = x.astype(jnp.float32)
    y = xf * lax.rsqrt(jnp.mean(xf * xf, axis=-1, keepdims=True) + EPS)
    return (y * g.astype(jnp.float32)).astype(x.dtype)


def modulate(h, shift, scale):
    return h * (1 + scale) + shift


def swiglu(h, w_gu, w_down):
    g, u = jnp.split(h @ w_gu, 2, axis=-1)
    return (jax.nn.silu(g) * u) @ w_down


def multiscale_pool(u, pool_mix, pool_scale):
    B, S, _ = u.shape
    uf = u.astype(jnp.float32).reshape(B, S, POOL_GROUPS, POOL_GROUP_W)
    cs = jnp.pad(jnp.cumsum(uf, axis=1), ((0, 0), (1, 0), (0, 0), (0, 0)))
    t1 = np.arange(1, S + 1)
    outs = []
    for gi, w in enumerate(POOL_WINDOWS):
        lo = np.maximum(t1 - w, 0)
        cnt = np.minimum(t1, w).astype(np.float32)[None, :, None]
        win_sum = cs[:, 1:, gi] - cs[:, lo, gi]
        outs.append(win_sum / cnt - uf[:, :, gi])
    pooled = jnp.stack(outs, axis=2).astype(u.dtype)
    mixed = jnp.einsum('bsgc,gcd->bsgd', pooled, pool_mix)
    return mixed.reshape(B, S, POOL_W) * pool_scale


def rel_bucket_band():
    ql = np.arange(BLK)[:, None]
    j = np.arange(2 * BLK)[None, :]
    n = np.clip(BLK + ql - j, 0, None)
    nf = np.maximum(n, 1).astype(np.float32)
    large = MAX_EXACT + (np.log(nf / MAX_EXACT) / np.log(REL_MAX_DIST / MAX_EXACT)
                         * (NUM_BUCKETS - MAX_EXACT)).astype(np.int32)
    large = np.minimum(large, NUM_BUCKETS - 1)
    return np.where(n < MAX_EXACT, n, large).astype(np.int32)


def band_mask(nb):
    qpos = (np.arange(nb)[:, None, None] * BLK + np.arange(BLK)[None, :, None])
    kpos = ((np.arange(nb)[:, None, None] - 1) * BLK + np.arange(2 * BLK)[None, None, :])
    dist = qpos - kpos
    return (dist >= 0) & (dist < WINDOW) & (kpos >= 0)


def swa_sink_attention(q, k, v, q_gain, k_gain, sinks, rel_bias):
    B, S = q.shape[:2]
    nb = S // BLK
    q = rms_norm(q, q_gain)
    k = rms_norm(k, k_gain)
    qb = q.reshape(B, nb, BLK, N_KV_HEADS, GQA_GROUP, HEAD_DIM)

    def band(t):
        tb = t.reshape(B, nb, BLK, N_KV_HEADS, HEAD_DIM)
        prev = jnp.pad(tb, ((0, 0), (1, 0), (0, 0), (0, 0), (0, 0)))[:, :nb]
        return jnp.concatenate([prev, tb], axis=2)

    kband, vband = band(k), band(v)
    logits = jnp.einsum('bnqkgd,bnjkd->bnkgqj', qb, kband).astype(jnp.float32) * (HEAD_DIM ** -0.5)
    bias = rel_bias.astype(jnp.float32)[rel_bucket_band()]
    bias = jnp.transpose(bias, (2, 0, 1)).reshape(N_KV_HEADS, GQA_GROUP, BLK, 2 * BLK)
    logits = logits + bias
    mask = band_mask(nb)[None, :, None, None]
    logits = jnp.where(mask, logits, NEG_INF)
    sink = jnp.broadcast_to(sinks.astype(jnp.float32).reshape(1, 1, N_KV_HEADS, GQA_GROUP, 1, 1),
                            logits.shape[:-1] + (1,))
    p = jax.nn.softmax(jnp.concatenate([logits, sink], axis=-1), axis=-1)[..., :-1]
    out = jnp.einsum('bnkgqj,bnjkd->bnqkgd', p.astype(v.dtype), vband)
    return out.reshape(B, S, N_HEADS * HEAD_DIM)


def _fwd_setup_inputs(seed: int = 0) -> dict:
    key = jax.random.key(seed)
    ks = jax.random.split(key, 24)
    L = DEPTH

    def dense(k, shape, fan_in):
        return jax.random.normal(k, shape, jnp.float32) * (fan_in ** -0.5)

    def gain(k, shape, s=0.05):
        return 1.0 + s * jax.random.normal(k, shape, jnp.float32)

    return {
        "x": jax.random.normal(ks[0], (BATCH, SEQ, D_MODEL), jnp.float32),
        "c": jax.random.normal(ks[1], (BATCH, D_MODEL), jnp.float32),
        "w_ada": dense(ks[2], (L, D_MODEL, 3 * N_SUB * D_MODEL), D_MODEL),
        "b_ada": 0.02 * jax.random.normal(ks[3], (L, 3 * N_SUB * D_MODEL), jnp.float32),
        "g_ffn1": gain(ks[4], (L, D_MODEL)),
        "w_ffn1_gu": dense(ks[5], (L, D_MODEL, 2 * D_FF), D_MODEL),
        "w_ffn1_down": dense(ks[6], (L, D_FF, D_MODEL), D_FF),
        "g_mix": gain(ks[7], (L, D_MODEL)),
        "w_in": dense(ks[8], (L, D_MODEL, IN_W), D_MODEL),
        "pool_mix": dense(ks[9], (L, POOL_GROUPS, POOL_GROUP_W, POOL_GROUP_W), POOL_GROUP_W),
        "pool_scale": gain(ks[10], (L, POOL_W), 0.1),
        "w_pool_up": dense(ks[11], (L, POOL_W, D_MODEL), POOL_W),
        "q_gain": gain(ks[12], (L, HEAD_DIM)),
        "k_gain": gain(ks[13], (L, HEAD_DIM)),
        "sinks": jax.random.normal(ks[14], (L, N_HEADS), jnp.float32),
        "rel_bias": 0.5 * jax.random.normal(ks[15], (NUM_BUCKETS, N_HEADS), jnp.float32),
        "w_attn_up": dense(ks[16], (L, N_HEADS * HEAD_DIM, D_MODEL), N_HEADS * HEAD_DIM),
        "w_o": dense(ks[17], (L, D_MODEL, D_MODEL), D_MODEL),
        "g_ffn2": gain(ks[18], (L, D_MODEL)),
        "w_ffn2_gu": dense(ks[19], (L, D_MODEL, 2 * D_FF), D_MODEL),
        "w_ffn2_down": dense(ks[20], (L, D_FF, D_MODEL), D_FF),
    }


def _fwd_reference(x, c, w_ada, b_ada, g_ffn1, w_ffn1_gu, w_ffn1_down, g_mix, w_in, pool_mix,
              pool_scale, w_pool_up, q_gain, k_gain, sinks, rel_bias, w_attn_up, w_o,
              g_ffn2, w_ffn2_gu, w_ffn2_down):
    B, S, D = x.shape
    split_idx = [int(s) for s in np.cumsum(IN_SPLITS)[:-1]]
    for l in range(DEPTH):
        mod = (jax.nn.silu(c) @ w_ada[l] + b_ada[l]).reshape(B, 3 * N_SUB, 1, D)

        h = modulate(rms_norm(x, g_ffn1[l]), mod[:, 0], mod[:, 1])
        x = x + 0.5 * mod[:, 2] * swiglu(h, w_ffn1_gu[l], w_ffn1_down[l])

        h = modulate(rms_norm(x, g_mix[l]), mod[:, 3], mod[:, 4])
        z = h @ w_in[l]
        u_pool, q, k, v, ga, gb = jnp.split(z, split_idx, axis=-1)
        y_pool = multiscale_pool(u_pool, pool_mix[l], pool_scale[l]) @ w_pool_up[l]
        y_attn = swa_sink_attention(
            q.reshape(B, S, N_HEADS, HEAD_DIM),
            k.reshape(B, S, N_KV_HEADS, HEAD_DIM),
            v.reshape(B, S, N_KV_HEADS, HEAD_DIM),
            q_gain[l], k_gain[l], sinks[l], rel_bias) @ w_attn_up[l]
        merged = jax.nn.sigmoid(ga) * y_pool + jax.nn.sigmoid(gb) * y_attn
        x = x + mod[:, 5] * (merged @ w_o[l])

        h = modulate(rms_norm(x, g_ffn2[l]), mod[:, 6], mod[:, 7])
        x = x + 0.5 * mod[:, 8] * swiglu(h, w_ffn2_gu[l], w_ffn2_down[l])
    return x


import jax as _jax
import jax.numpy as _jnp

TWIN_FORMAT = 'train_step'
FWD_PARAMS = ['x', 'c', 'w_ada', 'b_ada', 'g_ffn1', 'w_ffn1_gu', 'w_ffn1_down', 'g_mix', 'w_in', 'pool_mix', 'pool_scale', 'w_pool_up', 'q_gain', 'k_gain', 'sinks', 'rel_bias', 'w_attn_up', 'w_o', 'g_ffn2', 'w_ffn2_gu', 'w_ffn2_down']
TWIN_WEIGHTS = ['w_ada', 'b_ada', 'g_ffn1', 'w_ffn1_gu', 'w_ffn1_down', 'g_mix', 'w_in', 'pool_mix', 'pool_scale', 'w_pool_up', 'q_gain', 'k_gain', 'sinks', 'rel_bias', 'w_attn_up', 'w_o', 'g_ffn2', 'w_ffn2_gu', 'w_ffn2_down']
TWIN_DIFF_INPUT = 'x'
TWIN_INPUTS = ['x', 'c', 'w_ada', 'b_ada', 'g_ffn1', 'w_ffn1_gu', 'w_ffn1_down', 'g_mix', 'w_in', 'pool_mix', 'pool_scale', 'w_pool_up', 'q_gain', 'k_gain', 'sinks', 'rel_bias', 'w_attn_up', 'w_o', 'g_ffn2', 'w_ffn2_gu', 'w_ffn2_down', 'loss_target', 'm_w_ada', 'm_b_ada', 'm_g_ffn1', 'm_w_ffn1_gu', 'm_w_ffn1_down', 'm_g_mix', 'm_w_in', 'm_pool_mix', 'm_pool_scale', 'm_w_pool_up', 'm_q_gain', 'm_k_gain', 'm_sinks', 'm_rel_bias', 'm_w_attn_up', 'm_w_o', 'm_g_ffn2', 'm_w_ffn2_gu', 'm_w_ffn2_down', 'v_w_ada', 'v_b_ada', 'v_g_ffn1', 'v_w_ffn1_gu', 'v_w_ffn1_down', 'v_g_mix', 'v_w_in', 'v_pool_mix', 'v_pool_scale', 'v_w_pool_up', 'v_q_gain', 'v_k_gain', 'v_sinks', 'v_rel_bias', 'v_w_attn_up', 'v_w_o', 'v_g_ffn2', 'v_w_ffn2_gu', 'v_w_ffn2_down']
TWIN_OUTPUTS = ['loss', 'grad_x', 'grad_w_ada', 'grad_b_ada', 'grad_g_ffn1', 'grad_w_ffn1_gu', 'grad_w_ffn1_down', 'grad_g_mix', 'grad_w_in', 'grad_pool_mix', 'grad_pool_scale', 'grad_w_pool_up', 'grad_q_gain', 'grad_k_gain', 'grad_sinks', 'grad_rel_bias', 'grad_w_attn_up', 'grad_w_o', 'grad_g_ffn2', 'grad_w_ffn2_gu', 'grad_w_ffn2_down', 'delta_w_ada', 'delta_b_ada', 'delta_g_ffn1', 'delta_w_ffn1_gu', 'delta_w_ffn1_down', 'delta_g_mix', 'delta_w_in', 'delta_pool_mix', 'delta_pool_scale', 'delta_w_pool_up', 'delta_q_gain', 'delta_k_gain', 'delta_sinks', 'delta_rel_bias', 'delta_w_attn_up', 'delta_w_o', 'delta_g_ffn2', 'delta_w_ffn2_gu', 'delta_w_ffn2_down', 'new_m_w_ada', 'new_m_b_ada', 'new_m_g_ffn1', 'new_m_w_ffn1_gu', 'new_m_w_ffn1_down', 'new_m_g_mix', 'new_m_w_in', 'new_m_pool_mix', 'new_m_pool_scale', 'new_m_w_pool_up', 'new_m_q_gain', 'new_m_k_gain', 'new_m_sinks', 'new_m_rel_bias', 'new_m_w_attn_up', 'new_m_w_o', 'new_m_g_ffn2', 'new_m_w_ffn2_gu', 'new_m_w_ffn2_down', 'new_v_w_ada', 'new_v_b_ada', 'new_v_g_ffn1', 'new_v_w_ffn1_gu', 'new_v_w_ffn1_down', 'new_v_g_mix', 'new_v_w_in', 'new_v_pool_mix', 'new_v_pool_scale', 'new_v_w_pool_up', 'new_v_q_gain', 'new_v_k_gain', 'new_v_sinks', 'new_v_rel_bias', 'new_v_w_attn_up', 'new_v_w_o', 'new_v_g_ffn2', 'new_v_w_ffn2_gu', 'new_v_w_ffn2_down']
TWIN_LEAF_KINDS = {'loss': 'loss', 'grad_x': 'grad_x', 'grad_w_ada': 'grad_w', 'grad_b_ada': 'grad_w', 'grad_g_ffn1': 'grad_w', 'grad_w_ffn1_gu': 'grad_w', 'grad_w_ffn1_down': 'grad_w', 'grad_g_mix': 'grad_w', 'grad_w_in': 'grad_w', 'grad_pool_mix': 'grad_w', 'grad_pool_scale': 'grad_w', 'grad_w_pool_up': 'grad_w', 'grad_q_gain': 'grad_w', 'grad_k_gain': 'grad_w', 'grad_sinks': 'grad_w', 'grad_rel_bias': 'grad_w', 'grad_w_attn_up': 'grad_w', 'grad_w_o': 'grad_w', 'grad_g_ffn2': 'grad_w', 'grad_w_ffn2_gu': 'grad_w', 'grad_w_ffn2_down': 'grad_w', 'delta_w_ada': 'delta_w', 'delta_b_ada': 'delta_w', 'delta_g_ffn1': 'delta_w', 'delta_w_ffn1_gu': 'delta_w', 'delta_w_ffn1_down': 'delta_w', 'delta_g_mix': 'delta_w', 'delta_w_in': 'delta_w', 'delta_pool_mix': 'delta_w', 'delta_pool_scale': 'delta_w', 'delta_w_pool_up': 'delta_w', 'delta_q_gain': 'delta_w', 'delta_k_gain': 'delta_w', 'delta_sinks': 'delta_w', 'delta_rel_bias': 'delta_w', 'delta_w_attn_up': 'delta_w', 'delta_w_o': 'delta_w', 'delta_g_ffn2': 'delta_w', 'delta_w_ffn2_gu': 'delta_w', 'delta_w_ffn2_down': 'delta_w', 'new_m_w_ada': 'new_m', 'new_m_b_ada': 'new_m', 'new_m_g_ffn1': 'new_m', 'new_m_w_ffn1_gu': 'new_m', 'new_m_w_ffn1_down': 'new_m', 'new_m_g_mix': 'new_m', 'new_m_w_in': 'new_m', 'new_m_pool_mix': 'new_m', 'new_m_pool_scale': 'new_m', 'new_m_w_pool_up': 'new_m', 'new_m_q_gain': 'new_m', 'new_m_k_gain': 'new_m', 'new_m_sinks': 'new_m', 'new_m_rel_bias': 'new_m', 'new_m_w_attn_up': 'new_m', 'new_m_w_o': 'new_m', 'new_m_g_ffn2': 'new_m', 'new_m_w_ffn2_gu': 'new_m', 'new_m_w_ffn2_down': 'new_m', 'new_v_w_ada': 'new_v', 'new_v_b_ada': 'new_v', 'new_v_g_ffn1': 'new_v', 'new_v_w_ffn1_gu': 'new_v', 'new_v_w_ffn1_down': 'new_v', 'new_v_g_mix': 'new_v', 'new_v_w_in': 'new_v', 'new_v_pool_mix': 'new_v', 'new_v_pool_scale': 'new_v', 'new_v_w_pool_up': 'new_v', 'new_v_q_gain': 'new_v', 'new_v_k_gain': 'new_v', 'new_v_sinks': 'new_v', 'new_v_rel_bias': 'new_v', 'new_v_w_attn_up': 'new_v', 'new_v_w_o': 'new_v', 'new_v_g_ffn2': 'new_v', 'new_v_w_ffn2_gu': 'new_v', 'new_v_w_ffn2_down': 'new_v'}


def _forward(args):
    return _fwd_reference(*[args[k] for k in FWD_PARAMS])


def _output_shape():
    def fwd():
        inp = _fwd_setup_inputs(0)
        return _fwd_reference(*[inp[k] for k in FWD_PARAMS])
    out = _jax.eval_shape(fwd)
    return out.shape, out.dtype

N_MICROBATCH = 1
ADAM_LR = 0.001
ADAM_B1 = 0.9
ADAM_B2 = 0.999
ADAM_EPS = 1e-08
ADAM_WD = 0.01
ADAM_STEP = 10
PER_EXAMPLE_BATCH_AXIS = {'x': 0, 'c': 0, 'loss_target': 0}
SHARED_INPUTS = []
_WEIGHT_DTYPES = {'w_ada': _jnp.float32, 'b_ada': _jnp.float32, 'g_ffn1': _jnp.float32, 'w_ffn1_gu': _jnp.float32, 'w_ffn1_down': _jnp.float32, 'g_mix': _jnp.float32, 'w_in': _jnp.float32, 'pool_mix': _jnp.float32, 'pool_scale': _jnp.float32, 'w_pool_up': _jnp.float32, 'q_gain': _jnp.float32, 'k_gain': _jnp.float32, 'sinks': _jnp.float32, 'rel_bias': _jnp.float32, 'w_attn_up': _jnp.float32, 'w_o': _jnp.float32, 'g_ffn2': _jnp.float32, 'w_ffn2_gu': _jnp.float32, 'w_ffn2_down': _jnp.float32}
MOMENT_SCALE = {'w_ada': 1.453100e+00, 'b_ada': 3.641262e+00, 'g_ffn1': 6.255108e+00, 'w_ffn1_gu': 2.500244e-01, 'w_ffn1_down': 3.530399e-01, 'g_mix': 4.818767e+00, 'w_in': 1.264271e+00, 'pool_mix': 4.943694e-01, 'pool_scale': 7.799063e+00, 'w_pool_up': 2.099525e-01, 'q_gain': 1.803775e+00, 'k_gain': 1.783675e+00, 'sinks': 1.678473e+00, 'rel_bias': 1.978492e-01, 'w_attn_up': 9.977437e-01, 'w_o': 7.861021e-01, 'g_ffn2': 6.339514e+00, 'w_ffn2_gu': 2.134967e-01, 'w_ffn2_down': 2.099547e-01}


def _to_microbatches(a, axis):
    t = _jnp.moveaxis(a, axis, 0)
    t = t.reshape((N_MICROBATCH, t.shape[0] // N_MICROBATCH) + t.shape[1:])
    return _jnp.moveaxis(t, 1, axis + 1)


def setup_inputs(seed: int = 0) -> dict:
    inp = _fwd_setup_inputs(seed)
    key = _jax.random.fold_in(_jax.random.key(seed), 7919)
    shape, _ = _output_shape()
    out = dict(inp)
    out["loss_target"] = _jax.random.normal(_jax.random.fold_in(key, 0), shape, _jnp.float32)
    for i, name in enumerate(TWIN_WEIGHTS):
        w = inp[name].astype(_jnp.float32)
        if MOMENT_SCALE is None:
            s = _jnp.sqrt(_jnp.mean(_jnp.square(w)) + 1e-30)
        else:
            s = MOMENT_SCALE[name]
        km, kv = _jax.random.split(_jax.random.fold_in(key, i + 1))
        out[name] = w
        out["m_" + name] = s * _jax.random.normal(km, w.shape, _jnp.float32)
        out["v_" + name] = (s * s) * _jax.random.uniform(kv, w.shape, _jnp.float32, 0.5, 1.5)
    if N_MICROBATCH > 1:
        for name, axis in PER_EXAMPLE_BATCH_AXIS.items():
            out[name] = _to_microbatches(out[name], axis)
    return {'x': out['x'], 'c': out['c'], 'w_ada': out['w_ada'], 'b_ada': out['b_ada'], 'g_ffn1': out['g_ffn1'], 'w_ffn1_gu': out['w_ffn1_gu'], 'w_ffn1_down': out['w_ffn1_down'], 'g_mix': out['g_mix'], 'w_in': out['w_in'], 'pool_mix': out['pool_mix'], 'pool_scale': out['pool_scale'], 'w_pool_up': out['w_pool_up'], 'q_gain': out['q_gain'], 'k_gain': out['k_gain'], 'sinks': out['sinks'], 'rel_bias': out['rel_bias'], 'w_attn_up': out['w_attn_up'], 'w_o': out['w_o'], 'g_ffn2': out['g_ffn2'], 'w_ffn2_gu': out['w_ffn2_gu'], 'w_ffn2_down': out['w_ffn2_down'], 'loss_target': out['loss_target'], 'm_w_ada': out['m_w_ada'], 'm_b_ada': out['m_b_ada'], 'm_g_ffn1': out['m_g_ffn1'], 'm_w_ffn1_gu': out['m_w_ffn1_gu'], 'm_w_ffn1_down': out['m_w_ffn1_down'], 'm_g_mix': out['m_g_mix'], 'm_w_in': out['m_w_in'], 'm_pool_mix': out['m_pool_mix'], 'm_pool_scale': out['m_pool_scale'], 'm_w_pool_up': out['m_w_pool_up'], 'm_q_gain': out['m_q_gain'], 'm_k_gain': out['m_k_gain'], 'm_sinks': out['m_sinks'], 'm_rel_bias': out['m_rel_bias'], 'm_w_attn_up': out['m_w_attn_up'], 'm_w_o': out['m_w_o'], 'm_g_ffn2': out['m_g_ffn2'], 'm_w_ffn2_gu': out['m_w_ffn2_gu'], 'm_w_ffn2_down': out['m_w_ffn2_down'], 'v_w_ada': out['v_w_ada'], 'v_b_ada': out['v_b_ada'], 'v_g_ffn1': out['v_g_ffn1'], 'v_w_ffn1_gu': out['v_w_ffn1_gu'], 'v_w_ffn1_down': out['v_w_ffn1_down'], 'v_g_mix': out['v_g_mix'], 'v_w_in': out['v_w_in'], 'v_pool_mix': out['v_pool_mix'], 'v_pool_scale': out['v_pool_scale'], 'v_w_pool_up': out['v_w_pool_up'], 'v_q_gain': out['v_q_gain'], 'v_k_gain': out['v_k_gain'], 'v_sinks': out['v_sinks'], 'v_rel_bias': out['v_rel_bias'], 'v_w_attn_up': out['v_w_attn_up'], 'v_w_o': out['v_w_o'], 'v_g_ffn2': out['v_g_ffn2'], 'v_w_ffn2_gu': out['v_w_ffn2_gu'], 'v_w_ffn2_down': out['v_w_ffn2_down']}


def _loss(weights, diff, rest, loss_target):
    with _jax.named_scope("forward"):
        args = {**rest, TWIN_DIFF_INPUT: diff, **{k: w.astype(_WEIGHT_DTYPES[k]) for k, w in weights.items()}}
        y = _forward(args)
    with _jax.named_scope("loss_head"):
        err = _jnp.square(y.astype(_jnp.float32) - loss_target)
        return 0.5 * _jnp.sum(_jnp.mean(err, axis=-1)) if err.ndim else 0.5 * err


def _adamw(w, g, m, v):
    m = ADAM_B1 * m + (1.0 - ADAM_B1) * g
    v = ADAM_B2 * v + (1.0 - ADAM_B2) * _jnp.square(g)
    m_hat = m / (1.0 - ADAM_B1 ** ADAM_STEP)
    v_hat = v / (1.0 - ADAM_B2 ** ADAM_STEP)
    delta = -ADAM_LR * (m_hat / (_jnp.sqrt(v_hat) + ADAM_EPS) + ADAM_WD * w)
    return delta, m, v


def reference(x, c, w_ada, b_ada, g_ffn1, w_ffn1_gu, w_ffn1_down, g_mix, w_in, pool_mix, pool_scale, w_pool_up, q_gain, k_gain, sinks, rel_bias, w_attn_up, w_o, g_ffn2, w_ffn2_gu, w_ffn2_down, loss_target, m_w_ada, m_b_ada, m_g_ffn1, m_w_ffn1_gu, m_w_ffn1_down, m_g_mix, m_w_in, m_pool_mix, m_pool_scale, m_w_pool_up, m_q_gain, m_k_gain, m_sinks, m_rel_bias, m_w_attn_up, m_w_o, m_g_ffn2, m_w_ffn2_gu, m_w_ffn2_down, v_w_ada, v_b_ada, v_g_ffn1, v_w_ffn1_gu, v_w_ffn1_down, v_g_mix, v_w_in, v_pool_mix, v_pool_scale, v_w_pool_up, v_q_gain, v_k_gain, v_sinks, v_rel_bias, v_w_attn_up, v_w_o, v_g_ffn2, v_w_ffn2_gu, v_w_ffn2_down):
    given = dict(x=x, c=c, w_ada=w_ada, b_ada=b_ada, g_ffn1=g_ffn1, w_ffn1_gu=w_ffn1_gu, w_ffn1_down=w_ffn1_down, g_mix=g_mix, w_in=w_in, pool_mix=pool_mix, pool_scale=pool_scale, w_pool_up=w_pool_up, q_gain=q_gain, k_gain=k_gain, sinks=sinks, rel_bias=rel_bias, w_attn_up=w_attn_up, w_o=w_o, g_ffn2=g_ffn2, w_ffn2_gu=w_ffn2_gu, w_ffn2_down=w_ffn2_down, loss_target=loss_target, m_w_ada=m_w_ada, m_b_ada=m_b_ada, m_g_ffn1=m_g_ffn1, m_w_ffn1_gu=m_w_ffn1_gu, m_w_ffn1_down=m_w_ffn1_down, m_g_mix=m_g_mix, m_w_in=m_w_in, m_pool_mix=m_pool_mix, m_pool_scale=m_pool_scale, m_w_pool_up=m_w_pool_up, m_q_gain=m_q_gain, m_k_gain=m_k_gain, m_sinks=m_sinks, m_rel_bias=m_rel_bias, m_w_attn_up=m_w_attn_up, m_w_o=m_w_o, m_g_ffn2=m_g_ffn2, m_w_ffn2_gu=m_w_ffn2_gu, m_w_ffn2_down=m_w_ffn2_down, v_w_ada=v_w_ada, v_b_ada=v_b_ada, v_g_ffn1=v_g_ffn1, v_w_ffn1_gu=v_w_ffn1_gu, v_w_ffn1_down=v_w_ffn1_down, v_g_mix=v_g_mix, v_w_in=v_w_in, v_pool_mix=v_pool_mix, v_pool_scale=v_pool_scale, v_w_pool_up=v_w_pool_up, v_q_gain=v_q_gain, v_k_gain=v_k_gain, v_sinks=v_sinks, v_rel_bias=v_rel_bias, v_w_attn_up=v_w_attn_up, v_w_o=v_w_o, v_g_ffn2=v_g_ffn2, v_w_ffn2_gu=v_w_ffn2_gu, v_w_ffn2_down=v_w_ffn2_down)
    weights = {n: given[n] for n in TWIN_WEIGHTS}
    shared = {n: given[n] for n in SHARED_INPUTS}
    per_example = {n: given[n] for n in ['x', 'c']}
    grad_fn = _jax.value_and_grad(_loss, argnums=(0, 1))

    def one_microbatch(ex, loss_target):
        ex = dict(ex)
        diff = ex.pop(TWIN_DIFF_INPUT)
        return grad_fn(weights, diff, {**shared, **ex}, loss_target)

    if N_MICROBATCH == 1:
        loss, (grad_w, grad_x) = one_microbatch(per_example, given["loss_target"])
    else:
        def body(carry, xs):
            loss_sum, grad_sum = carry
            l_k, (gw_k, gx_k) = one_microbatch(xs[0], xs[1])
            with _jax.named_scope("update"):
                return (loss_sum + l_k, _jax.tree.map(_jnp.add, grad_sum, gw_k)), gx_k

        init = (_jnp.zeros((), _jnp.float32), _jax.tree.map(_jnp.zeros_like, weights))
        (loss, grad_w), grad_x = _jax.lax.scan(body, init, (per_example, given["loss_target"]))
    with _jax.named_scope("update"):
        delta_w, new_m, new_v = {}, {}, {}
        for n in TWIN_WEIGHTS:
            delta_w[n], new_m[n], new_v[n] = _adamw(weights[n], grad_w[n], given["m_" + n], given["v_" + n])
    return (loss, grad_x, *[grad_w[n] for n in TWIN_WEIGHTS], *[delta_w[n] for n in TWIN_WEIGHTS],
            *[new_m[n] for n in TWIN_WEIGHTS], *[new_v[n] for n in TWIN_WEIGHTS])
```

```python
import numpy as np
import jax
import jax.numpy as jnp
from jax import lax
from jax.experimental import pallas as pl
from jax.experimental.pallas import tpu as pltpu

BF = jnp.bfloat16
F32 = jnp.float32
MESH = pl.DeviceIdType.MESH

EPS = 1e-6
NEG_INF = -1e30
HEAD_DIM = 64
N_HEADS = 16
N_KV = 2
ATT_W = N_HEADS * HEAD_DIM
KV_W = N_KV * HEAD_DIM
BLK = 128
NUM_BUCKETS = 32
POOL_MAX_W = 16
N_CHIPS = 4
N_DEV = 8
LANES = 128
ADAM_LR, ADAM_B1, ADAM_B2, ADAM_EPS, ADAM_WD, ADAM_STEP = 0.001, 0.9, 0.999, 1e-08, 0.01, 10
VMEM_LIMIT = 52 * 1024 * 1024


def _cp(sem=None):
    return pltpu.CompilerParams(dimension_semantics=sem, vmem_limit_bytes=VMEM_LIMIT)


def _pick(dim, prefs):
    for p in prefs:
        if p <= dim and dim % p == 0:
            return p
    return dim


def _sds(shape, dtype):
    return jax.ShapeDtypeStruct(tuple(shape), dtype)


NN = (((1,), (0,)), ((), ()))
NT = (((1,), (1,)), ((), ()))
TN = (((0,), (0,)), ((), ()))


def _mm(name, grid, dims, a, a_spec, b, b_spec, extras, extra_specs, out_shapes, out_specs, acc_shape, epilogue):
    n_k = grid[2]
    n_e = len(extras)
    n_o = len(out_shapes)

    def body(*refs):
        a_ref, b_ref = refs[0], refs[1]
        e_refs = refs[2:2 + n_e]
        o_refs = refs[2 + n_e:2 + n_e + n_o]
        p = lax.dot_general(a_ref[...].astype(BF), b_ref[...].astype(BF), dims, preferred_element_type=F32)
        if n_k == 1:
            epilogue(p, e_refs, o_refs)
        else:
            acc = refs[-1]
            k = pl.program_id(2)

            @pl.when(k == 0)
            def _():
                acc[...] = p

            @pl.when(k > 0)
            def _():
                acc[...] += p

            @pl.when(k == n_k - 1)
            def _():
                epilogue(acc[...], e_refs, o_refs)

    scratch = [] if n_k == 1 else [pltpu.VMEM(acc_shape, F32)]
    return pl.pallas_call(
        body, grid=grid, in_specs=[a_spec, b_spec, *extra_specs], out_specs=list(out_specs),
        out_shape=list(out_shapes), scratch_shapes=scratch, name=name,
        compiler_params=_cp(("parallel", "parallel", "arbitrary")),
    )(a, b, *extras)


def _store(p, e, o):
    o[0][...] = p.astype(o[0].dtype)


def _rms_mod_fwd(name, x, gain, shift, scale):
    S, D = x.shape
    ts = _pick(S, (512,))

    def body(x_ref, g_ref, sh_ref, sc_ref, h_ref):
        xv = x_ref[...]
        r = lax.rsqrt(jnp.mean(xv * xv, axis=-1, keepdims=True) + EPS)
        n = xv * r * g_ref[...]
        h_ref[...] = (n * (1.0 + sc_ref[...]) + sh_ref[...]).astype(BF)

    row = pl.BlockSpec((ts, D), lambda i: (i, 0))
    vec = pl.BlockSpec((1, D), lambda i: (0, 0))
    return pl.pallas_call(body, grid=(S // ts,), in_specs=[row, vec, vec, vec], out_specs=row,
                          out_shape=_sds((S, D), BF), name=name, compiler_params=_cp(("parallel",)))(x, gain, shift, scale)


def _rms_mod_bwd(name, dh, x, dres, gain, scale):
    S, D = x.shape
    ts = _pick(S, (256,))

    def body(dh_ref, x_ref, dr_ref, g_ref, sc_ref, dx_ref, acc_ref):
        i = pl.program_id(0)
        xv = x_ref[...]
        dhv = dh_ref[...]
        g = g_ref[...]
        r = lax.rsqrt(jnp.mean(xv * xv, axis=-1, keepdims=True) + EPS)
        xhat = xv * r
        dn = dhv * (1.0 + sc_ref[...])
        dxhat = dn * g
        dx_ref[...] = dr_ref[...] + r * (dxhat - xhat * jnp.mean(dxhat * xhat, axis=-1, keepdims=True))
        part = jnp.concatenate([
            jnp.sum(dhv, axis=0, keepdims=True),
            jnp.sum(dhv * (xhat * g), axis=0, keepdims=True),
            jnp.sum(dn * xhat, axis=0, keepdims=True),
            jnp.zeros((5, D), F32)], axis=0)

        @pl.when(i == 0)
        def _():
            acc_ref[...] = part

        @pl.when(i > 0)
        def _():
            acc_ref[...] += part

    row = pl.BlockSpec((ts, D), lambda i: (i, 0))
    vec = pl.BlockSpec((1, D), lambda i: (0, 0))
    return pl.pallas_call(body, grid=(S // ts,), in_specs=[row, row, row, vec, vec],
                          out_specs=[row, pl.BlockSpec((8, D), lambda i: (0, 0))],
                          out_shape=[_sds((S, D), F32), _sds((8, D), F32)], name=name,
                          compiler_params=_cp(("arbitrary",)))(dh, x, dres, gain, scale)


def _gate_bwd(name, dx, f, coef):
    S, D = dx.shape
    ts = _pick(S, (512,))

    def body(dx_ref, f_ref, c_ref, df_ref, acc_ref):
        i = pl.program_id(0)
        dxv = dx_ref[...]
        df_ref[...] = (dxv * c_ref[...]).astype(BF)
        part = jnp.concatenate([jnp.sum(dxv * f_ref[...].astype(F32), axis=0, keepdims=True), jnp.zeros((7, D), F32)], axis=0)

        @pl.when(i == 0)
        def _():
            acc_ref[...] = part

        @pl.when(i > 0)
        def _():
            acc_ref[...] += part

    row = pl.BlockSpec((ts, D), lambda i: (i, 0))
    return pl.pallas_call(body, grid=(S // ts,), in_specs=[row, row, pl.BlockSpec((1, D), lambda i: (0, 0))],
                          out_specs=[row, pl.BlockSpec((8, D), lambda i: (0, 0))],
                          out_shape=[_sds((S, D), BF), _sds((8, D), F32)], name=name,
                          compiler_params=_cp(("arbitrary",)))(dx, f, coef)


def _loss_bwd(x3, target):
    S, D = x3.shape
    ts = _pick(S, (512,))

    def body(x_ref, t_ref, dx_ref, acc_ref):
        i = pl.program_id(0)
        e = x_ref[...] - t_ref[...]
        dx_ref[...] = e * (1.0 / D)
        part = jnp.concatenate([jnp.sum(e * e, axis=0, keepdims=True), jnp.zeros((7, D), F32)], axis=0)

        @pl.when(i == 0)
        def _():
            acc_ref[...] = part

        @pl.when(i > 0)
        def _():
            acc_ref[...] += part

    row = pl.BlockSpec((ts, D), lambda i: (i, 0))
    return pl.pallas_call(body, grid=(S // ts,), in_specs=[row, row], out_specs=[row, pl.BlockSpec((8, D), lambda i: (0, 0))],
                          out_shape=[_sds((S, D), F32), _sds((8, D), F32)], name="loss_bwd",
                          compiler_params=_cp(("arbitrary",)))(x3, target)


def _silu_parts(g):
    s = jax.nn.sigmoid(g)
    return s, g * s


def _ffn_up(name, h, wgu4):
    S, D = h.shape
    SH = wgu4.shape[2]
    F = 2 * SH
    tm = _pick(S, (1024,))
    tn = _pick(SH, (256,))
    nts = SH // tn

    def body(h_ref, wg_ref, wu_ref, gu_ref, act_ref):
        hv = h_ref[...]
        g = jnp.dot(hv, wg_ref[...], preferred_element_type=F32)
        u = jnp.dot(hv, wu_ref[...], preferred_element_type=F32)
        gu_ref[0] = g.astype(BF)
        gu_ref[1] = u.astype(BF)
        act_ref[...] = (_silu_parts(g)[1] * u).astype(BF)

    return pl.pallas_call(
        body, grid=(S // tm, F // tn),
        in_specs=[pl.BlockSpec((tm, D), lambda i, j: (i, 0)),
                  pl.BlockSpec((None, D, tn), lambda i, j: (j // nts, 0, j % nts)),
                  pl.BlockSpec((None, D, tn), lambda i, j: (2 + j // nts, 0, j % nts))],
        out_specs=[pl.BlockSpec((2, tm, tn), lambda i, j: (0, i, j)), pl.BlockSpec((tm, tn), lambda i, j: (i, j))],
        out_shape=[_sds((2, S, F), BF), _sds((S, F), BF)], name=name,
        compiler_params=_cp(("parallel", "parallel")))(h, wgu4, wgu4)


def _mm_residual(name, a, w, x_in, coef):
    S, K = a.shape
    D = w.shape[1]
    tm = _pick(S, (1024,))
    tn = _pick(D, (1024,))
    tk = _pick(K, (1408, 2048, 1024, 512))

    def epi(p, e, o):
        o[0][...] = e[0][...] + e[1][...] * p
        o[1][...] = p.astype(BF)

    tile = pl.BlockSpec((tm, tn), lambda i, j, k: (i, j))
    return _mm(name, (S // tm, D // tn, K // tk), NN,
               a, pl.BlockSpec((tm, tk), lambda i, j, k: (i, k)),
               w, pl.BlockSpec((tk, tn), lambda i, j, k: (k, j)),
               [x_in, coef], [tile, pl.BlockSpec((1, tn), lambda i, j, k: (0, j))],
               [_sds((S, D), F32), _sds((S, D), BF)], [tile, tile], (tm, tn), epi)


def _ffn_dact(name, df, wd, gu):
    S, D = df.shape
    F = wd.shape[0]
    tm = _pick(S, (1024,))
    tn = _pick(F, (256,))

    def epi(p, e, o):
        g = e[0][0].astype(F32)
        u = e[0][1].astype(F32)
        s, sg = _silu_parts(g)
        o[0][0] = (p * u * (s * (1.0 + g * (1.0 - s)))).astype(BF)
        o[0][1] = (p * sg).astype(BF)

    pair = pl.BlockSpec((2, tm, tn), lambda i, j, k: (0, i, j))
    return _mm(name, (S // tm, F // tn, 1), NT,
               df, pl.BlockSpec((tm, D), lambda i, j, k: (i, 0)),
               wd, pl.BlockSpec((tn, D), lambda i, j, k: (j, 0)),
               [gu], [pair], [_sds((2, S, F), BF)], [pair], None, epi)[0]


def _ffn_dh(name, dgu, wgu4):
    _, S, F = dgu.shape
    _, D, SH = wgu4.shape
    tm = _pick(S, (1024,))
    tn = _pick(D, (1024,))
    tk = _pick(SH, (1408, 256))
    nkp = F // tk
    nks = SH // tk
    return _mm(name, (S // tm, D // tn, 2 * nkp), NT,
               dgu, pl.BlockSpec((None, tm, tk), lambda i, j, k: (k // nkp, i, k % nkp)),
               wgu4, pl.BlockSpec((None, tn, tk), lambda i, j, k: (k // nks, j, k % nks)),
               [], [], [_sds((S, D), F32)], [pl.BlockSpec((tm, tn), lambda i, j, k: (i, j))], (tm, tn), _store)[0]


def _ffn_dwgu(name, h, dgu):
    _, S, F = dgu.shape
    D = h.shape[1]
    SH = F // 2
    tk1 = _pick(D, (1024,))
    tn = _pick(SH, (1408, 256))
    ts = _pick(S, (1024,))
    npj = F // tn
    nsj = SH // tn
    return _mm(name, (D // tk1, 2 * npj, S // ts), TN,
               h, pl.BlockSpec((ts, tk1), lambda i, j, k: (k, i)),
               dgu, pl.BlockSpec((None, ts, tn), lambda i, j, k: (j // npj, k, j % npj)),
               [], [], [_sds((4, D, SH), BF)],
               [pl.BlockSpec((None, tk1, tn), lambda i, j, k: (j // nsj, i, j % nsj))], (tk1, tn), _store)[0]


def _mm_tn(name, a, b, tk1_prefs, tn_prefs):
    S, K1 = a.shape
    N = b.shape[1]
    tk1 = _pick(K1, tk1_prefs)
    tn = _pick(N, tn_prefs)
    ts = _pick(S, (1024,))
    return _mm(name, (K1 // tk1, N // tn, S // ts), TN,
               a, pl.BlockSpec((ts, tk1), lambda i, j, k: (k, i)),
               b, pl.BlockSpec((ts, tn), lambda i, j, k: (k, j)),
               [], [], [_sds((K1, N), BF)], [pl.BlockSpec((tk1, tn), lambda i, j, k: (i, j))], (tk1, tn), _store)[0]


def _pool_window(ext, w, back):
    n = ext.shape[0]
    s = ext
    for step in (1, 2, 4, 8):
        sh = pltpu.roll(s, (n - step) if back else step, axis=0)
        s = jnp.where(w > step, s + sh, s)
    return s


def _pool_fwd(z, PW):
    S = z.shape[0]
    tc = _pick(S, (1024,))
    bpg = (PW // 4) // LANES
    H = POOL_MAX_W

    def body(prev_ref, u_ref, o_ref):
        i = pl.program_id(0)
        j = pl.program_id(1)
        w = lax.shift_left(jnp.int32(2), j // bpg)
        u = u_ref[...]
        prev = jnp.where(i > 0, prev_ref[...], 0.0)
        s = _pool_window(jnp.concatenate([prev, u], axis=0), w, False)[H:]
        t = i * tc + lax.broadcasted_iota(jnp.int32, (tc, LANES), 0)
        cnt = jnp.minimum(t + 1, w).astype(F32)
        o_ref[...] = (s / cnt - u).astype(BF)

    r = tc // H
    return pl.pallas_call(
        body, grid=(S // tc, PW // LANES),
        in_specs=[pl.BlockSpec((H, LANES), lambda i, j: (jnp.maximum(i * r - 1, 0), j)),
                  pl.BlockSpec((tc, LANES), lambda i, j: (i, j))],
        out_specs=pl.BlockSpec((tc, LANES), lambda i, j: (i, j)),
        out_shape=_sds((S, PW), BF), name="pool_fwd", compiler_params=_cp(("parallel", "parallel")))(z, z)


def _pool_bwd(dpooled):
    S, PW = dpooled.shape
    tc = _pick(S, (1024,))
    bpg = (PW // 4) // LANES
    H = POOL_MAX_W
    last = S // tc - 1

    def body(dp_ref, nxt_ref, o_ref):
        i = pl.program_id(0)
        j = pl.program_id(1)
        w = lax.shift_left(jnp.int32(2), j // bpg)
        dp = dp_ref[...]
        nxt = jnp.where(i < last, nxt_ref[...], 0.0)
        ext = jnp.concatenate([dp, nxt], axis=0)
        t = i * tc + lax.broadcasted_iota(jnp.int32, (tc + H, LANES), 0)
        cnt = jnp.minimum(t + 1, w).astype(F32)
        s = _pool_window(ext / cnt, w, True)[:tc]
        o_ref[...] = (s - dp).astype(BF)

    r = tc // H
    nh = S // H - 1
    return pl.pallas_call(
        body, grid=(S // tc, PW // LANES),
        in_specs=[pl.BlockSpec((tc, LANES), lambda i, j: (i, j)),
                  pl.BlockSpec((H, LANES), lambda i, j: (jnp.minimum((i + 1) * r, nh), j))],
        out_specs=pl.BlockSpec((tc, LANES), lambda i, j: (i, j)),
        out_shape=_sds((S, PW), BF), name="pool_bwd", compiler_params=_cp(("parallel", "parallel")))(dpooled, dpooled)


def _pool_mix(pooled, pm, scale):
    S, PW = pooled.shape
    gw = PW // 4
    ts = _pick(S, (1024,))

    def epi(p, e, o):
        o[0][...] = (p * e[0][...]).astype(BF)

    tile = pl.BlockSpec((ts, gw), lambda i, j, k: (i, j))
    return _mm("pool_mix", (S // ts, 4, 1), NN, pooled, tile,
               pm, pl.BlockSpec((None, gw, gw), lambda i, j, k: (j, 0, 0)),
               [scale], [pl.BlockSpec((1, gw), lambda i, j, k: (0, j))], [_sds((S, PW), BF)], [tile], None, epi)[0]


def _pool_mix_bwd(pooled, pm, scale, dmixed):
    S, PW = pooled.shape
    gw = PW // 4
    ts = _pick(S, (1024,))

    def body(p_ref, pm_ref, sc_ref, dm_ref, dp_ref, dpm_ref, dsc_ref):
        i = pl.program_id(1)
        p = p_ref[...]
        w = pm_ref[...]
        dm = dm_ref[...]
        pre = jnp.dot(p, w, preferred_element_type=F32)
        dmp = (dm * sc_ref[...]).astype(BF)
        dp_ref[...] = lax.dot_general(dmp, w, NT, preferred_element_type=F32)
        dw = lax.dot_general(p, dmp, TN, preferred_element_type=F32)
        ds = jnp.concatenate([jnp.sum(dm * pre, axis=0, keepdims=True), jnp.zeros((7, gw), F32)], axis=0)

        @pl.when(i == 0)
        def _():
            dpm_ref[...] = dw
            dsc_ref[...] = ds

        @pl.when(i > 0)
        def _():
            dpm_ref[...] += dw
            dsc_ref[...] += ds

    tile = pl.BlockSpec((ts, gw), lambda g, i: (i, g))
    return pl.pallas_call(
        body, grid=(4, S // ts),
        in_specs=[tile, pl.BlockSpec((None, gw, gw), lambda g, i: (g, 0, 0)), pl.BlockSpec((1, gw), lambda g, i: (0, g)), tile],
        out_specs=[tile, pl.BlockSpec((None, gw, gw), lambda g, i: (g, 0, 0)), pl.BlockSpec((8, gw), lambda g, i: (0, g))],
        out_shape=[_sds((S, PW), F32), _sds((4, gw, gw), F32), _sds((8, PW), F32)], name="pool_mix_bwd",
        compiler_params=_cp(("parallel", "arbitrary")))(pooled, pm, scale, dmixed)


def _bucket_onehot():
    ql = np.arange(BLK)[:, None]
    j = np.arange(2 * BLK)[None, :]
    d = BLK + ql - j
    n = np.clip(d, 0, None)
    nf = np.maximum(n, 1).astype(np.float32)
    max_exact = NUM_BUCKETS // 2
    large = max_exact + (np.log(nf / max_exact) / np.log(BLK / max_exact) * (NUM_BUCKETS - max_exact)).astype(np.int32)
    large = np.minimum(large, NUM_BUCKETS - 1)
    bucket = np.where(n < max_exact, n, large).astype(np.int32)
    valid = (d >= 0) & (d < BLK)
    oh = (bucket[None] == np.arange(NUM_BUCKETS)[:, None, None]) & valid[None]
    return oh.reshape(NUM_BUCKETS, BLK * 2 * BLK)


def _bias_table(rel_bias):
    oh = jnp.asarray(_bucket_onehot(), BF)
    tn = 4096

    def body(rb_ref, oh_ref, o_ref):
        rb = rb_ref[...]
        o = oh_ref[...]
        hi = rb.astype(BF)
        r1 = rb - hi.astype(F32)
        mid = r1.astype(BF)
        lo = (r1 - mid.astype(F32)).astype(BF)
        acc = lax.dot_general(hi, o, TN, preferred_element_type=F32)
        acc = acc + lax.dot_general(mid, o, TN, preferred_element_type=F32)
        acc = acc + lax.dot_general(lo, o, TN, preferred_element_type=F32)
        on_band = jnp.sum(o.astype(F32), axis=0, keepdims=True) > 0.5
        o_ref[...] = jnp.where(on_band, acc, NEG_INF)

    n = oh.shape[1]
    return pl.pallas_call(body, grid=(n // tn,),
                          in_specs=[pl.BlockSpec((NUM_BUCKETS, N_HEADS), lambda i: (0, 0)), pl.BlockSpec((NUM_BUCKETS, tn), lambda i: (0, i))],
                          out_specs=pl.BlockSpec((N_HEADS, tn), lambda i: (0, i)), out_shape=_sds((N_HEADS, n), F32),
                          name="bias_table", compiler_params=_cp(("parallel",)))(rel_bias, oh)


def _rel_bias_grad(dl):
    oh = jnp.asarray(_bucket_onehot(), BF)
    n = oh.shape[1]
    tk = 4096

    def body(dl_ref, oh_ref, o_ref):
        i = pl.program_id(0)
        d = dl_ref[...]
        o = oh_ref[...]
        hi = d.astype(BF)
        r1 = d - hi.astype(F32)
        mid = r1.astype(BF)
        lo = (r1 - mid.astype(F32)).astype(BF)
        acc = lax.dot_general(o, hi, NT, preferred_element_type=F32)
        acc = acc + lax.dot_general(o, mid, NT, preferred_element_type=F32)
        acc = acc + lax.dot_general(o, lo, NT, preferred_element_type=F32)

        @pl.when(i == 0)
        def _():
            o_ref[...] = acc

        @pl.when(i > 0)
        def _():
            o_ref[...] += acc

    return pl.pallas_call(body, grid=(n // tk,),
                          in_specs=[pl.BlockSpec((N_HEADS, tk), lambda i: (0, i)), pl.BlockSpec((NUM_BUCKETS, tk), lambda i: (0, i))],
                          out_specs=pl.BlockSpec((NUM_BUCKETS, N_HEADS), lambda i: (0, 0)), out_shape=_sds((NUM_BUCKETS, N_HEADS), F32),
                          name="rel_bias_grad", compiler_params=_cp(("arbitrary",)))(dl, oh)


def _lo_half(shape):
    return lax.broadcasted_iota(jnp.int32, shape, 1) < HEAD_DIM


def _half_sum(x, lo):
    s_lo = jnp.sum(jnp.where(lo, x, 0.0), axis=-1, keepdims=True)
    s_hi = jnp.sum(jnp.where(lo, 0.0, x), axis=-1, keepdims=True)
    return jnp.where(lo, s_lo, s_hi)


def _norm2(x, lo):
    r = lax.rsqrt(_half_sum(x * x, lo) * (1.0 / HEAD_DIM) + EPS)
    return x * r, r


def _norm2_bwd(dy, xhat, r, gain, lo):
    dxhat = dy * gain
    dx = r * (dxhat - xhat * (_half_sum(dxhat * xhat, lo) * (1.0 / HEAD_DIM)))
    return dx, dy * xhat


def _swap(x):
    return pltpu.roll(x, HEAD_DIM, axis=1)


def _attn_logits(n, kk, zq_ref, kn, qg, bias_ref, sink_ref, lo_k):
    lo_q = _lo_half((BLK, LANES))
    half_k = lo_k if kk == 0 else jnp.logical_not(lo_k)
    K = jnp.where(half_k, kn, 0.0).astype(BF)
    rows, qhats, qrs = [], [], []
    for jp in range(4):
        xq = zq_ref[:, jp * LANES:(jp + 1) * LANES]
        qhat, qr = _norm2(xq, lo_q)
        qn = qhat * qg * (HEAD_DIM ** -0.5)
        qs = _swap(qn)
        rows += [qn, qs] if kk == 0 else [qs, qn]
        qhats.append(qhat)
        qrs.append(qr)
    Q = jnp.concatenate(rows, axis=0).astype(BF)
    qk = lax.dot_general(Q, K, NT, preferred_element_type=F32)
    b = bias_ref[8 * kk:8 * kk + 8].reshape(8 * BLK, 2 * BLK)
    col = lax.broadcasted_iota(jnp.int32, qk.shape, 1)
    ok = (b > -1e29) & ((n > 0) | (col >= BLK))
    l = jnp.where(ok, qk + b, NEG_INF)
    sink = sink_ref[kk]
    m = jnp.maximum(jnp.max(l, axis=-1, keepdims=True), sink)
    e = jnp.exp(l - m)
    es = jnp.exp(sink - m)
    den = jnp.sum(e, axis=-1, keepdims=True) + es
    return Q, K, e / den, es / den, qhats, qrs


def _attn_specs(o_q, o_k):
    nq = o_q // 512
    nk = o_k // LANES
    prev = lambda n: (jnp.maximum(n - 1, 0), nk)
    prev_v = lambda n: (jnp.maximum(n - 1, 0), nk + 1)
    return [pl.BlockSpec((BLK, 512), lambda n: (n, nq)), pl.BlockSpec((BLK, 512), lambda n: (n, nq + 1)),
            pl.BlockSpec((BLK, LANES), prev), pl.BlockSpec((BLK, LANES), lambda n: (n, nk)),
            pl.BlockSpec((BLK, LANES), prev_v), pl.BlockSpec((BLK, LANES), lambda n: (n, nk + 1)),
            pl.BlockSpec((1, LANES), lambda n: (0, 0)), pl.BlockSpec((1, LANES), lambda n: (0, 0)),
            pl.BlockSpec((N_KV, 8 * BLK, 1), lambda n: (0, 0, 0)),
            pl.BlockSpec((N_HEADS, BLK, 2 * BLK), lambda n: (0, 0, 0))]


def _attn_fwd(z, o_q, o_k, qg2, kg2, sink_rows, bias):
    S = z.shape[0]

    def body(zq0, zq1, zkp, zkc, zvp, zvc, qg_ref, kg_ref, sink_ref, bias_ref, o_ref):
        n = pl.program_id(0)
        lo_k = _lo_half((2 * BLK, LANES))
        lo_q = _lo_half((BLK, LANES))
        khat, _ = _norm2(jnp.concatenate([zkp[...], zkc[...]], axis=0), lo_k)
        kn = khat * kg_ref[...]
        vb = jnp.concatenate([zvp[...], zvc[...]], axis=0).astype(BF)
        for kk, zq in enumerate((zq0, zq1)):
            _, _, p, _, _, _ = _attn_logits(n, kk, zq, kn, qg_ref[...], bias_ref, sink_ref, lo_k)
            r = jnp.dot(p.astype(BF), vb, preferred_element_type=F32)
            for jp in range(4):
                ev = r[(2 * jp) * BLK:(2 * jp + 1) * BLK]
                od = r[(2 * jp + 1) * BLK:(2 * jp + 2) * BLK]
                pair = jnp.where(lo_q, ev, _swap(od)) if kk == 0 else jnp.where(lo_q, _swap(ev), od)
                c0 = (4 * kk + jp) * LANES
                o_ref[:, c0:c0 + LANES] = pair.astype(BF)

    return pl.pallas_call(body, grid=(S // BLK,), in_specs=_attn_specs(o_q, o_k),
                          out_specs=pl.BlockSpec((BLK, ATT_W), lambda n: (n, 0)), out_shape=_sds((S, ATT_W), BF),
                          name="attn_fwd", compiler_params=_cp(("parallel",)))(z, z, z, z, z, z, qg2, kg2, sink_rows, bias)


def _attn_bwd(z, o_q, o_k, qg2, kg2, sink_rows, bias, dout):
    S = z.shape[0]

    def body(zq0, zq1, zkp, zkc, zvp, zvc, qg_ref, kg_ref, sink_ref, bias_ref, do_ref,
             dq_ref, dkp_ref, dkc_ref, dvp_ref, dvc_ref, dl_ref, dsink_ref, dgain_ref):
        n = pl.program_id(0)
        lo_k = _lo_half((2 * BLK, LANES))
        lo_q = _lo_half((BLK, LANES))
        qg = qg_ref[...]
        kg = kg_ref[...]
        khat, kr = _norm2(jnp.concatenate([zkp[...], zkc[...]], axis=0), lo_k)
        kn = khat * kg
        vf = jnp.concatenate([zvp[...], zvc[...]], axis=0)

        @pl.when(n == 0)
        def _():
            dl_ref[...] = jnp.zeros_like(dl_ref)
            dsink_ref[...] = jnp.zeros_like(dsink_ref)
            dgain_ref[...] = jnp.zeros_like(dgain_ref)

        dkn = jnp.zeros((2 * BLK, LANES), F32)
        dvb = jnp.zeros((2 * BLK, LANES), F32)
        dqg = jnp.zeros((1, LANES), F32)
        for kk, zq in enumerate((zq0, zq1)):
            half_k = lo_k if kk == 0 else jnp.logical_not(lo_k)
            Q, K, p, ps, qhats, qrs = _attn_logits(n, kk, zq, kn, qg, bias_ref, sink_ref, lo_k)
            rows = []
            for jp in range(4):
                c0 = (4 * kk + jp) * LANES
                x = do_ref[:, c0:c0 + LANES]
                rows += [x, _swap(x)] if kk == 0 else [_swap(x), x]
            dO = jnp.concatenate(rows, axis=0).astype(BF)
            V = jnp.where(half_k, vf, 0.0).astype(BF)
            dP = lax.dot_general(dO, V, NT, preferred_element_type=F32)
            delta = jnp.sum(p * dP, axis=-1, keepdims=True)
            dS = p * (dP - delta)
            dsink_ref[kk] += -ps * delta
            dl_ref[8 * kk:8 * kk + 8] += dS.reshape(8, BLK, 2 * BLK)
            dSb = dS.astype(BF)
            dvb = dvb + jnp.where(half_k, lax.dot_general(p.astype(BF), dO, TN, preferred_element_type=F32), 0.0)
            dkn = dkn + jnp.where(half_k, lax.dot_general(dSb, Q, TN, preferred_element_type=F32), 0.0)
            dQ = jnp.dot(dSb, K, preferred_element_type=F32) * (HEAD_DIM ** -0.5)
            for jp in range(4):
                ev = dQ[(2 * jp) * BLK:(2 * jp + 1) * BLK]
                od = dQ[(2 * jp + 1) * BLK:(2 * jp + 2) * BLK]
                dy = (ev + _swap(od)) if kk == 0 else (_swap(ev) + od)
                dx, gq = _norm2_bwd(dy, qhats[jp], qrs[jp], qg, lo_q)
                dqg = dqg + jnp.sum(gq, axis=0, keepdims=True)
                c0 = (4 * kk + jp) * LANES
                dq_ref[:, c0:c0 + LANES] = dx.astype(BF)
        dk, gk = _norm2_bwd(dkn, khat, kr, kg, lo_k)
        dkp_ref[...] = dk[:BLK]
        dkc_ref[...] = dk[BLK:]
        dvp_ref[...] = dvb[:BLK]
        dvc_ref[...] = dvb[BLK:]
        dgain_ref[...] += jnp.concatenate([dqg, jnp.sum(gk, axis=0, keepdims=True), jnp.zeros((6, LANES), F32)], axis=0)

    blk = pl.BlockSpec((BLK, LANES), lambda n: (n, 0))
    wide = pl.BlockSpec((BLK, ATT_W), lambda n: (n, 0))
    return pl.pallas_call(
        body, grid=(S // BLK,), in_specs=_attn_specs(o_q, o_k) + [wide],
        out_specs=[wide, blk, blk, blk, blk, pl.BlockSpec((N_HEADS, BLK, 2 * BLK), lambda n: (0, 0, 0)),
                   pl.BlockSpec((N_KV, 8 * BLK, 1), lambda n: (0, 0, 0)), pl.BlockSpec((8, LANES), lambda n: (0, 0))],
        out_shape=[_sds((S, ATT_W), BF), _sds((S, LANES), F32), _sds((S, LANES), F32), _sds((S, LANES), F32), _sds((S, LANES), F32),
                   _sds((N_HEADS, BLK, 2 * BLK), F32), _sds((N_KV, 8 * BLK, 1), F32), _sds((8, LANES), F32)],
        name="attn_bwd", compiler_params=_cp(("arbitrary",)))(z, z, z, z, z, z, qg2, kg2, sink_rows, bias, dout)


def _kv_combine(dkp, dkc, dvp, dvc):
    S = dkc.shape[0]
    last = S // BLK - 1

    def body(kp_ref, kc_ref, vp_ref, vc_ref, dk_ref, dv_ref):
        more = pl.program_id(0) < last
        dk_ref[...] = (kc_ref[...] + jnp.where(more, kp_ref[...], 0.0)).astype(BF)
        dv_ref[...] = (vc_ref[...] + jnp.where(more, vp_ref[...], 0.0)).astype(BF)

    cur = pl.BlockSpec((BLK, LANES), lambda n: (n, 0))
    nxt = pl.BlockSpec((BLK, LANES), lambda n: (jnp.minimum(n + 1, last), 0))
    return pl.pallas_call(body, grid=(S // BLK,), in_specs=[nxt, cur, nxt, cur], out_specs=[cur, cur],
                          out_shape=[_sds((S, LANES), BF), _sds((S, LANES), BF)], name="kv_combine",
                          compiler_params=_cp(("parallel",)))(dkp, dkc, dvp, dvc)


def _merge_fwd(mixed, attn, wpu4, wau4, z, o_ga):
    S, PW = mixed.shape
    _, _, CS = wpu4.shape
    D = 4 * CS
    tm = _pick(S, (1024,))
    tn = 256
    nsj = CS // tn
    na = o_ga // tn
    nb = (o_ga + D) // tn

    def body(m_ref, a_ref, wp_ref, wa_ref, ga_ref, gb_ref, mg_ref, yy_ref):
        yp = jnp.dot(m_ref[...], wp_ref[...], preferred_element_type=F32)
        ya = jnp.dot(a_ref[...], wa_ref[...], preferred_element_type=F32)
        mg_ref[...] = (jax.nn.sigmoid(ga_ref[...]) * yp + jax.nn.sigmoid(gb_ref[...]) * ya).astype(BF)
        yy_ref[0] = yp.astype(BF)
        yy_ref[1] = ya.astype(BF)

    return pl.pallas_call(
        body, grid=(S // tm, D // tn),
        in_specs=[pl.BlockSpec((tm, PW), lambda i, j: (i, 0)), pl.BlockSpec((tm, ATT_W), lambda i, j: (i, 0)),
                  pl.BlockSpec((None, PW, tn), lambda i, j: (j // nsj, 0, j % nsj)),
                  pl.BlockSpec((None, ATT_W, tn), lambda i, j: (j // nsj, 0, j % nsj)),
                  pl.BlockSpec((tm, tn), lambda i, j: (i, na + j)), pl.BlockSpec((tm, tn), lambda i, j: (i, nb + j))],
        out_specs=[pl.BlockSpec((tm, tn), lambda i, j: (i, j)), pl.BlockSpec((2, tm, tn), lambda i, j: (0, i, j))],
        out_shape=[_sds((S, D), BF), _sds((2, S, D), BF)], name="merge_fwd",
        compiler_params=_cp(("parallel", "parallel")))(mixed, attn, wpu4, wau4, z, z)


def _merge_bwd(do, wo, z, o_ga, yy):
    S, D = do.shape
    tm = _pick(S, (1024,))
    tn = 256
    na = o_ga // tn
    nb = (o_ga + D) // tn

    def epi(p, e, o):
        sa = jax.nn.sigmoid(e[0][...])
        sb = jax.nn.sigmoid(e[1][...])
        yp = e[2][0].astype(F32)
        ya = e[2][1].astype(F32)
        o[0][0] = (p * yp * sa * (1.0 - sa)).astype(BF)
        o[0][1] = (p * ya * sb * (1.0 - sb)).astype(BF)
        o[1][0] = (p * sa).astype(BF)
        o[1][1] = (p * sb).astype(BF)

    pair = pl.BlockSpec((2, tm, tn), lambda i, j, k: (0, i, j))
    return _mm("merge_bwd", (S // tm, D // tn, 1), NT,
               do, pl.BlockSpec((tm, D), lambda i, j, k: (i, 0)),
               wo, pl.BlockSpec((tn, D), lambda i, j, k: (j, 0)),
               [z, z, yy], [pl.BlockSpec((tm, tn), lambda i, j, k: (i, na + j)), pl.BlockSpec((tm, tn), lambda i, j, k: (i, nb + j)), pair],
               [_sds((2, S, D), BF), _sds((2, S, D), BF)], [pair, pair], None, epi)


def _mm_up_t(name, dyy, which, w4):
    _, S, D = dyy.shape
    _, K, CS = w4.shape
    tm = _pick(S, (1024,))
    return _mm(name, (S // tm, 1, N_CHIPS), NT,
               dyy, pl.BlockSpec((None, tm, CS), lambda i, j, k: (which, i, k)),
               w4, pl.BlockSpec((None, K, CS), lambda i, j, k: (k, 0, 0)),
               [], [], [_sds((S, K), F32)], [pl.BlockSpec((tm, K), lambda i, j, k: (i, 0))], (tm, K), _store)[0]


def _mm_up_dw(name, a, dyy, which):
    _, S, D = dyy.shape
    K = a.shape[1]
    CS = D // N_CHIPS
    ts = _pick(S, (1024,))
    return _mm(name, (1, N_CHIPS, S // ts), TN,
               a, pl.BlockSpec((ts, K), lambda i, j, k: (k, 0)),
               dyy, pl.BlockSpec((None, ts, CS), lambda i, j, k: (which, k, j)),
               [], [], [_sds((N_CHIPS, K, CS), BF)], [pl.BlockSpec((None, K, CS), lambda i, j, k: (j, 0, 0))], (K, CS), _store)[0]


def _adamw(w, g, m, v):
    m = ADAM_B1 * m + (1.0 - ADAM_B1) * g
    v = ADAM_B2 * v + (1.0 - ADAM_B2) * (g * g)
    m_hat = m / (1.0 - ADAM_B1 ** ADAM_STEP)
    v_hat = v / (1.0 - ADAM_B2 ** ADAM_STEP)
    delta = -ADAM_LR * (m_hat / (jnp.sqrt(v_hat) + ADAM_EPS) + ADAM_WD * w)
    return delta, m, v


def _mod_fwd(c_all, w_ada, b_sh):
    D, cols = w_ada.shape
    tn = cols // 9

    def body(c_ref, w_ref, b_ref, o_ref):
        cv = c_ref[...]
        sc = (cv * jax.nn.sigmoid(cv)).astype(BF)
        o_ref[...] = jnp.dot(sc, w_ref[...].astype(BF), preferred_element_type=F32) + b_ref[...]

    return pl.pallas_call(body, grid=(9,),
                          in_specs=[pl.BlockSpec((N_DEV, D), lambda j: (0, 0)), pl.BlockSpec((D, tn), lambda j: (0, j)),
                                    pl.BlockSpec((1, tn), lambda j: (0, j))],
                          out_specs=pl.BlockSpec((N_DEV, tn), lambda j: (0, j)), out_shape=_sds((N_DEV, cols), F32),
                          name="mod_fwd", compiler_params=_cp(("parallel",)))(c_all, w_ada, b_sh)


def _wada_bwd(c_all, dmod_sh, w, m, v):
    D, cols = w.shape
    tn = cols // 18

    def body(c_ref, d_ref, w_ref, m_ref, v_ref, g_ref, dl_ref, nm_ref, nv_ref):
        cv = c_ref[...]
        sc = (cv * jax.nn.sigmoid(cv)).astype(BF)
        g = lax.dot_general(sc, d_ref[...].astype(BF), TN, preferred_element_type=F32)
        g_ref[...] = g
        dl_ref[...], nm_ref[...], nv_ref[...] = _adamw(w_ref[...], g, m_ref[...], v_ref[...])

    tile = pl.BlockSpec((D, tn), lambda j: (0, j))
    out = _sds((D, cols), F32)
    return pl.pallas_call(body, grid=(18,),
                          in_specs=[pl.BlockSpec((N_DEV, D), lambda j: (0, 0)), pl.BlockSpec((N_DEV, tn), lambda j: (0, j)), tile, tile, tile],
                          out_specs=[tile] * 4, out_shape=[out] * 4, name="wada_bwd",
                          compiler_params=_cp(("parallel",)))(c_all, dmod_sh, w, m, v)


def _adam_2d(name, w, g, m, v):
    R, C = w.shape
    tr = _pick(R, (256, 128, 64, 8))

    def body(w_ref, g_ref, m_ref, v_ref, dl_ref, nm_ref, nv_ref):
        dl_ref[...], nm_ref[...], nv_ref[...] = _adamw(w_ref[...], g_ref[...], m_ref[...], v_ref[...])

    tile = pl.BlockSpec((tr, C), lambda i: (i, 0))
    out = _sds((R, C), F32)
    return pl.pallas_call(body, grid=(R // tr,), in_specs=[tile] * 4, out_specs=[tile] * 3, out_shape=[out] * 3,
                          name=name, compiler_params=_cp(("parallel",)))(w, g, m, v)


def _small_finish(parts, w, m, v):
    _, R, C = parts.shape

    def body(p_ref, w_ref, m_ref, v_ref, g_ref, dl_ref, nm_ref, nv_ref):
        g = p_ref[0]
        for d in range(1, N_DEV):
            g = g + p_ref[d]
        g_ref[...] = g
        dl_ref[...], nm_ref[...], nv_ref[...] = _adamw(w_ref[...], g, m_ref[...], v_ref[...])

    out = _sds((R, C), F32)
    return pl.pallas_call(body, out_shape=[out] * 4, name="small_finish", compiler_params=_cp())(parts, w, m, v)


ANY = pl.BlockSpec(memory_space=pl.ANY)


def _place():
    x, y, c = lax.axis_index("x"), lax.axis_index("y"), lax.axis_index("c")
    return x, y, c


def _other_chips(x, y):
    return [(1 - x, y), (x, 1 - y), (1 - x, 1 - y)]


def _gather_weights(shards, c8):
    n = len(shards)

    def body(*refs):
        src = refs[:n]
        c_src = refs[n]
        dst = refs[n + 1:2 * n + 1]
        c_dst = refs[2 * n + 1]
        send, recv, loc, c_send, c_recv = refs[2 * n + 2:]
        x, y, c = _place()
        r = 2 * x + y
        chips = _other_chips(x, y)
        sib = (x, y, 1 - c)

        def half(i, chip_r, cc):
            h = shards[i].shape[0] // 2
            return dst[i].at[chip_r, pl.ds(pl.multiple_of(cc * h, 16), h), :]

        def src_half(i, cc):
            h = shards[i].shape[0] // 2
            return src[i].at[pl.ds(pl.multiple_of(cc * h, 16), h), :]

        def ici(i, j):
            return pltpu.make_async_remote_copy(src_half(i, c), half(i, r, c), send.at[i, j], recv.at[i, j],
                                                device_id=(*chips[j], c), device_id_type=MESH)

        def landed(i, j):
            rj = 2 * chips[j][0] + chips[j][1]
            return pltpu.make_async_remote_copy(src_half(i, c), half(i, rj, c), send.at[i, j], recv.at[i, j],
                                                device_id=(*chips[j], c), device_id_type=MESH)

        def d2d(i, j, cc):
            rj = 2 * chips[j][0] + chips[j][1]
            return pltpu.make_async_remote_copy(half(i, rj, cc), half(i, rj, cc), send.at[i, 3 + j], recv.at[i, 3 + j],
                                                device_id=sib, device_id_type=MESH)

        def c_copy(k):
            peer = (x ^ ((k >> 2) & 1), y ^ ((k >> 1) & 1), c ^ (k & 1))
            return pltpu.make_async_remote_copy(c_src, c_dst.at[4 * x + 2 * y + c], c_send.at[k - 1], c_recv.at[k - 1],
                                                device_id=peer, device_id_type=MESH)

        def c_landed(k):
            px, py, pc = x ^ ((k >> 2) & 1), y ^ ((k >> 1) & 1), c ^ (k & 1)
            return pltpu.make_async_remote_copy(c_src, c_dst.at[4 * px + 2 * py + pc], c_send.at[k - 1], c_recv.at[k - 1],
                                                device_id=(px, py, pc), device_id_type=MESH)

        mine = [pltpu.make_async_copy(src[i], dst[i].at[r], loc.at[i]) for i in range(n)]
        mine.append(pltpu.make_async_copy(c_src, c_dst.at[4 * x + 2 * y + c], loc.at[n]))
        for cp in mine:
            cp.start()
        for k in range(1, N_DEV):
            c_copy(k).start()
        for i in range(n):
            for j in range(3):
                ici(i, j).start()
        for i in range(n):
            for j in range(3):
                landed(i, j).wait_recv()
                d2d(i, j, c).start()
        for i in range(n):
            for j in range(3):
                d2d(i, j, 1 - c).wait_recv()
        for k in range(1, N_DEV):
            c_landed(k).wait_recv()
        for i in range(n):
            for j in range(3):
                ici(i, j).wait_send()
                d2d(i, j, c).wait_send()
        for k in range(1, N_DEV):
            c_copy(k).wait_send()
        for cp in mine:
            cp.wait()

    out_shape = [_sds((N_CHIPS, *s.shape), BF) for s in shards] + [_sds((N_DEV, *c8.shape), F32)]
    outs = pl.pallas_call(
        body, in_specs=[ANY] * (n + 1), out_specs=[ANY] * (n + 1), out_shape=out_shape,
        scratch_shapes=[pltpu.SemaphoreType.DMA((n, 6)), pltpu.SemaphoreType.DMA((n, 6)), pltpu.SemaphoreType.DMA((n + 1,)),
                        pltpu.SemaphoreType.DMA((N_DEV - 1,)), pltpu.SemaphoreType.DMA((N_DEV - 1,))],
        name="gather_weights")(*shards, c8)
    return outs[:n], outs[n]


def _chip_exchange(name, arr):
    def body(src, dst, send, recv, loc):
        x, y, c = _place()
        r = 2 * x + y
        chips = _other_chips(x, y)

        def cp(j, slot):
            return pltpu.make_async_remote_copy(src, dst.at[slot], send.at[j], recv.at[j], device_id=(*chips[j], c), device_id_type=MESH)

        mine = pltpu.make_async_copy(src, dst.at[r], loc)
        mine.start()
        for j in range(3):
            cp(j, r).start()
        for j in range(3):
            cp(j, 2 * chips[j][0] + chips[j][1]).wait_recv()
        for j in range(3):
            cp(j, r).wait_send()
        mine.wait()

    return pl.pallas_call(body, in_specs=[ANY], out_specs=ANY, out_shape=_sds((N_CHIPS, *arr.shape), arr.dtype),
                          scratch_shapes=[pltpu.SemaphoreType.DMA((3,)), pltpu.SemaphoreType.DMA((3,)), pltpu.SemaphoreType.DMA],
                          name=name)(arr)


def _gather_all(name, arr):
    def body(src, dst, send, recv, loc):
        x, y, c = _place()

        def cp(k, slot_of_me):
            px, py, pc = x ^ ((k >> 2) & 1), y ^ ((k >> 1) & 1), c ^ (k & 1)
            slot = (4 * x + 2 * y + c) if slot_of_me else (4 * px + 2 * py + pc)
            return pltpu.make_async_remote_copy(src, dst.at[slot], send.at[k - 1], recv.at[k - 1],
                                                device_id=(px, py, pc), device_id_type=MESH)

        mine = pltpu.make_async_copy(src, dst.at[4 * x + 2 * y + c], loc)
        mine.start()
        for k in range(1, N_DEV):
            cp(k, True).start()
        for k in range(1, N_DEV):
            cp(k, False).wait_recv()
        for k in range(1, N_DEV):
            cp(k, True).wait_send()
        mine.wait()

    return pl.pallas_call(body, in_specs=[ANY], out_specs=ANY, out_shape=_sds((N_DEV, *arr.shape), arr.dtype),
                          scratch_shapes=[pltpu.SemaphoreType.DMA((N_DEV - 1,)), pltpu.SemaphoreType.DMA((N_DEV - 1,)), pltpu.SemaphoreType.DMA],
                          name=name)(arr)


def _pair_split(parts):
    n = len(parts)

    def body(*refs):
        src = refs[:n]
        own = refs[n:2 * n]
        got = refs[2 * n:3 * n]
        send, recv, loc = refs[3 * n:]
        x, y, c = _place()

        def rows(i, cc):
            h = parts[i].shape[1] // 2
            return src[i].at[:, pl.ds(pl.multiple_of(cc * h, 16), h), :]

        def cp(i):
            return pltpu.make_async_remote_copy(rows(i, 1 - c), got[i], send.at[i], recv.at[i], device_id=(x, y, 1 - c), device_id_type=MESH)

        keep = [pltpu.make_async_copy(rows(i, c), own[i], loc.at[i]) for i in range(n)]
        for i in range(n):
            keep[i].start()
            cp(i).start()
        for i in range(n):
            cp(i).wait_recv()
        for i in range(n):
            cp(i).wait_send()
            keep[i].wait()

    halves = [_sds((N_CHIPS, p.shape[1] // 2, p.shape[2]), BF) for p in parts]
    outs = pl.pallas_call(body, in_specs=[ANY] * n, out_specs=[ANY] * (2 * n), out_shape=halves + halves,
                          scratch_shapes=[pltpu.SemaphoreType.DMA((n,)), pltpu.SemaphoreType.DMA((n,)), pltpu.SemaphoreType.DMA((n,))],
                          name="pair_split")(*parts)
    return outs[:n], outs[n:]


def _add_pair(name, a, b):
    _, H, C = a.shape
    tr = _pick(H, (512, 256, 128, 64, 16))

    def body(a_ref, b_ref, o_ref):
        o_ref[...] = (a_ref[...].astype(F32) + b_ref[...].astype(F32)).astype(BF)

    tile = pl.BlockSpec((None, tr, C), lambda k, i: (k, i, 0))
    return pl.pallas_call(body, grid=(N_CHIPS, H // tr), in_specs=[tile, tile], out_specs=tile, out_shape=_sds(a.shape, BF),
                          name=name, compiler_params=_cp(("parallel", "parallel")))(a, b)


def _to_owners(sums):
    n = len(sums)

    def body(*refs):
        src = refs[:n]
        dst = refs[n:2 * n]
        send, recv, loc = refs[2 * n:]
        x, y, c = _place()
        r = 2 * x + y
        chips = _other_chips(x, y)

        def cp(i, j, slot):
            rj = 2 * chips[j][0] + chips[j][1]
            return pltpu.make_async_remote_copy(src[i].at[rj], dst[i].at[slot], send.at[i, j], recv.at[i, j],
                                                device_id=(*chips[j], c), device_id_type=MESH)

        keep = [pltpu.make_async_copy(src[i].at[r], dst[i].at[r], loc.at[i]) for i in range(n)]
        for i in range(n):
            keep[i].start()
            for j in range(3):
                cp(i, j, r).start()
        for i in range(n):
            for j in range(3):
                cp(i, j, 2 * chips[j][0] + chips[j][1]).wait_recv()
        for i in range(n):
            for j in range(3):
                cp(i, j, r).wait_send()
            keep[i].wait()

    outs = pl.pallas_call(body, in_specs=[ANY] * n, out_specs=[ANY] * n, out_shape=[_sds(s.shape, BF) for s in sums],
                          scratch_shapes=[pltpu.SemaphoreType.DMA((n, 3)), pltpu.SemaphoreType.DMA((n, 3)), pltpu.SemaphoreType.DMA((n,))],
                          name="to_owners")(*sums)
    return outs


def _sum_chips(name, u):
    _, H, C = u.shape
    tr = _pick(H, (256, 128, 64, 16))

    def body(u_ref, o_ref):
        o_ref[...] = ((u_ref[0].astype(F32) + u_ref[1].astype(F32)) + u_ref[2].astype(F32)) + u_ref[3].astype(F32)

    return pl.pallas_call(body, grid=(H // tr,), in_specs=[pl.BlockSpec((N_CHIPS, tr, C), lambda i: (0, i, 0))],
                          out_specs=pl.BlockSpec((tr, C), lambda i: (i, 0)), out_shape=_sds((H, C), F32),
                          name=name, compiler_params=_cp(("parallel",)))(u)


def _pair_join(halves):
    n = len(halves)

    def body(*refs):
        src = refs[:n]
        dst = refs[n:2 * n]
        send, recv, loc = refs[2 * n:]
        x, y, c = _place()

        def cp(i, slot):
            return pltpu.make_async_remote_copy(src[i], dst[i].at[slot], send.at[i], recv.at[i], device_id=(x, y, 1 - c), device_id_type=MESH)

        keep = [pltpu.make_async_copy(src[i], dst[i].at[c], loc.at[i]) for i in range(n)]
        for i in range(n):
            keep[i].start()
            cp(i, c).start()
        for i in range(n):
            cp(i, 1 - c).wait_recv()
        for i in range(n):
            cp(i, c).wait_send()
            keep[i].wait()

    return pl.pallas_call(body, in_specs=[ANY] * n, out_specs=[ANY] * n, out_shape=[_sds((2, *h.shape), F32) for h in halves],
                          scratch_shapes=[pltpu.SemaphoreType.DMA((n,)), pltpu.SemaphoreType.DMA((n,)), pltpu.SemaphoreType.DMA((n,))],
                          name="pair_join")(*halves)


def _row(a, i):
    return a[i:i + 1]


def _local_step(x, target, mod, g_ffn1, g_mix, g_ffn2, pool_scale, q_gain, k_gain, sinks, rel_bias, W):
    S, D = x.shape
    PW = W["pool_up"].shape[1]
    o_q, o_k = PW, PW + ATT_W
    o_ga = o_k + 2 * KV_W
    half = 0.5 * mod
    tile2 = lambda g: jnp.concatenate([g, g], axis=1)
    qg2, kg2 = tile2(q_gain), tile2(k_gain)
    sink_rows = jnp.broadcast_to(sinks.reshape(N_KV, 8, 1, 1), (N_KV, 8, BLK, 1)).reshape(N_KV, 8 * BLK, 1)
    bias = _bias_table(rel_bias).reshape(N_HEADS, BLK, 2 * BLK)

    h1 = _rms_mod_fwd("rms_mod_fwd1", x, g_ffn1, _row(mod, 0), _row(mod, 1))
    gu1, act1 = _ffn_up("ffn1_up", h1, W["gu1"])
    x1, f1 = _mm_residual("ffn1_down", act1, W["down1"], x, _row(half, 2))
    h2 = _rms_mod_fwd("rms_mod_fwd2", x1, g_mix, _row(mod, 3), _row(mod, 4))
    IN_W = W["in"].shape[1]
    tnz = _pick(IN_W, (1280, 256))
    tmz = _pick(S, (1024,))
    z = _mm("mix_in", (S // tmz, IN_W // tnz, 1), NN, h2, pl.BlockSpec((tmz, D), lambda i, j, k: (i, 0)),
            W["in"], pl.BlockSpec((D, tnz), lambda i, j, k: (0, j)), [], [], [_sds((S, IN_W), F32)],
            [pl.BlockSpec((tmz, tnz), lambda i, j, k: (i, j))], None, _store)[0]
    pooled = _pool_fwd(z, PW)
    mixed = _pool_mix(pooled, W["pool_mix"], pool_scale)
    attn = _attn_fwd(z, o_q, o_k, qg2, kg2, sink_rows, bias)
    merged, yy = _merge_fwd(mixed, attn, W["pool_up"], W["attn_up"], z, o_ga)
    x2, fo = _mm_residual("mix_out", merged, W["o"], x1, _row(mod, 5))
    h3 = _rms_mod_fwd("rms_mod_fwd3", x2, g_ffn2, _row(mod, 6), _row(mod, 7))
    gu2, act2 = _ffn_up("ffn2_up", h3, W["gu2"])
    x3, f2 = _mm_residual("ffn2_down", act2, W["down2"], x2, _row(half, 8))
    dx3, loss_acc = _loss_bwd(x3, target)

    df2, dgate8 = _gate_bwd("gate_bwd3", dx3, f2, _row(half, 8))
    dgu2 = _ffn_dact("ffn2_dact", df2, W["down2"], gu2)
    p_down2 = _mm_tn("ffn2_dwd", act2, df2, (1408, 512), (1024,))
    p_gu2 = _ffn_dwgu("ffn2_dwgu", h3, dgu2)
    dh3 = _ffn_dh("ffn2_dh", dgu2, W["gu2"])
    dx2, acc3 = _rms_mod_bwd("rms_mod_bwd3", dh3, x2, dx3, g_ffn2, _row(mod, 7))

    do, dgate5 = _gate_bwd("gate_bwd2", dx2, fo, _row(mod, 5))
    dgab, dyy = _merge_bwd(do, W["o"], z, o_ga, yy)
    p_o = _mm_tn("mix_dwo", merged, do, (1024,), (1024,))
    dmixed = _mm_up_t("pool_up_t", dyy, 0, W["pool_up"])
    dattn = _mm_up_t("attn_up_t", dyy, 1, W["attn_up"])
    p_pool_up = _mm_up_dw("pool_up_dw", mixed, dyy, 0)
    p_attn_up = _mm_up_dw("attn_up_dw", attn, dyy, 1)
    dpooled, dpm, dps = _pool_mix_bwd(pooled, W["pool_mix"], pool_scale, dmixed)
    du_pool = _pool_bwd(dpooled)
    dq, dkp, dkc, dvp, dvc, dl, dsink, dgain = _attn_bwd(z, o_q, o_k, qg2, kg2, sink_rows, bias, dattn)
    dk, dv = _kv_combine(dkp, dkc, dvp, dvc)
    drb = _rel_bias_grad(dl.reshape(N_HEADS, BLK * 2 * BLK))
    dz = jnp.concatenate([du_pool, dq, dk, dv, dgab[0], dgab[1]], axis=1)
    p_in = _mm_tn("mix_dwin", h2, dz, (1024,), (1280, 256))
    tkz = _pick(IN_W, (1280, 256))
    tnd = _pick(D, (1024,))
    dh2 = _mm("mix_dh", (S // tmz, D // tnd, IN_W // tkz), NT, dz, pl.BlockSpec((tmz, tkz), lambda i, j, k: (i, k)),
              W["in"], pl.BlockSpec((tnd, tkz), lambda i, j, k: (j, k)), [], [], [_sds((S, D), F32)],
              [pl.BlockSpec((tmz, tnd), lambda i, j, k: (i, j))], (tmz, tnd), _store)[0]
    dx1, acc2 = _rms_mod_bwd("rms_mod_bwd2", dh2, x1, dx2, g_mix, _row(mod, 4))

    df1, dgate2 = _gate_bwd("gate_bwd1", dx1, f1, _row(half, 2))
    dgu1 = _ffn_dact("ffn1_dact", df1, W["down1"], gu1)
    p_down1 = _mm_tn("ffn1_dwd", act1, df1, (1408, 512), (1024,))
    p_gu1 = _ffn_dwgu("ffn1_dwgu", h1, dgu1)
    dh1 = _ffn_dh("ffn1_dh", dgu1, W["gu1"])
    grad_x, acc1 = _rms_mod_bwd("rms_mod_bwd1", dh1, x, dx1, g_ffn1, _row(mod, 1))

    dmod = jnp.concatenate([_row(acc1, 0), _row(acc1, 1), 0.5 * _row(dgate2, 0),
                            _row(acc2, 0), _row(acc2, 1), _row(dgate5, 0),
                            _row(acc3, 0), _row(acc3, 1), 0.5 * _row(dgate8, 0)], axis=0)
    fold = lambda r: r[:, :HEAD_DIM] + r[:, HEAD_DIM:]
    small = dict(
        dmod=dmod, g_ffn1=_row(acc1, 2), g_mix=_row(acc2, 2), g_ffn2=_row(acc3, 2), pool_scale=_row(dps, 0),
        q_gain=fold(_row(dgain, 0)), k_gain=fold(_row(dgain, 1)),
        sinks=jnp.sum(dsink.reshape(N_HEADS, BLK), axis=1).reshape(1, N_HEADS), rel_bias=drb,
        loss=(0.5 / D) * jnp.sum(_row(loss_acc, 0)).reshape(1, 1))
    by_rows = lambda p: p.reshape(N_CHIPS, p.shape[0] // N_CHIPS, p.shape[1])
    parts = dict(gu1=p_gu1, down1=by_rows(p_down1), w_in=p_in, pool_mix=dpm, pool_up=p_pool_up, attn_up=p_attn_up,
                 o=by_rows(p_o), gu2=p_gu2, down2=by_rows(p_down2))
    return grad_x, small, parts


SMALL_ORDER = ("dmod", "g_ffn1", "g_mix", "g_ffn2", "pool_scale", "q_gain", "k_gain", "sinks", "rel_bias", "loss")


def _pack_small(vals):
    flat = jnp.concatenate([vals[k].reshape(-1) for k in SMALL_ORDER])
    n = flat.shape[0]
    rows = -(-n // (8 * LANES)) * 8
    return jnp.pad(flat, (0, rows * LANES - n)).reshape(rows, LANES)


def _unpack_small(packed, like):
    flat = packed.reshape(-1)
    out, off = {}, 0
    for k in SMALL_ORDER:
        n = int(np.prod(like[k].shape))
        out[k] = flat[off:off + n].reshape(like[k].shape)
        off += n
    return out


BIG = ("gu1", "down1", "w_in", "pool_mix", "pool_up", "attn_up", "o", "gu2", "down2")


def kernel(x, c, w_ada, b_ada, g_ffn1, w_ffn1_gu, w_ffn1_down, g_mix, w_in, pool_mix, pool_scale, w_pool_up, q_gain, k_gain, sinks, rel_bias, w_attn_up, w_o, g_ffn2, w_ffn2_gu, w_ffn2_down, loss_target, m_w_ada, m_b_ada, m_g_ffn1, m_w_ffn1_gu, m_w_ffn1_down, m_g_mix, m_w_in, m_pool_mix, m_pool_scale, m_w_pool_up, m_q_gain, m_k_gain, m_sinks, m_rel_bias, m_w_attn_up, m_w_o, m_g_ffn2, m_w_ffn2_gu, m_w_ffn2_down, v_w_ada, v_b_ada, v_g_ffn1, v_w_ffn1_gu, v_w_ffn1_down, v_g_mix, v_w_in, v_pool_mix, v_pool_scale, v_w_pool_up, v_q_gain, v_k_gain, v_sinks, v_rel_bias, v_w_attn_up, v_w_o, v_g_ffn2, v_w_ffn2_gu, v_w_ffn2_down):
    S, D = x.shape[1], x.shape[2]
    gw = pool_mix.shape[3]
    r = 2 * lax.axis_index("x") + lax.axis_index("y")

    two_d = lambda a: a.reshape(-1, a.shape[-1])
    w_sh = dict(gu1=w_ffn1_gu, down1=w_ffn1_down, w_in=w_in, pool_mix=pool_mix, pool_up=w_pool_up, attn_up=w_attn_up, o=w_o,
                gu2=w_ffn2_gu, down2=w_ffn2_down)
    m_sh = dict(gu1=m_w_ffn1_gu, down1=m_w_ffn1_down, w_in=m_w_in, pool_mix=m_pool_mix, pool_up=m_w_pool_up, attn_up=m_w_attn_up,
                o=m_w_o, gu2=m_w_ffn2_gu, down2=m_w_ffn2_down)
    v_sh = dict(gu1=v_w_ffn1_gu, down1=v_w_ffn1_down, w_in=v_w_in, pool_mix=v_pool_mix, pool_up=v_w_pool_up, attn_up=v_w_attn_up,
                o=v_w_o, gu2=v_w_ffn2_gu, down2=v_w_ffn2_down)
    w2 = {k: two_d(w_sh[k]) for k in BIG}

    full, c_all = _gather_weights([w2[k].astype(BF) for k in BIG], jnp.broadcast_to(c, (8, D)))
    full = dict(zip(BIG, full))
    c_all = c_all[:, 0, :]
    in_cols = w2["w_in"].shape[1]
    W = dict(gu1=full["gu1"], gu2=full["gu2"],
             down1=full["down1"].reshape(-1, D), down2=full["down2"].reshape(-1, D), o=full["o"].reshape(-1, D),
             pool_up=full["pool_up"], attn_up=full["attn_up"],
             pool_mix=full["pool_mix"].reshape(N_CHIPS, 4, gw // N_CHIPS, gw).transpose(1, 0, 2, 3).reshape(4, gw, gw))
    W["in"] = full["w_in"].transpose(1, 0, 2).reshape(D, N_CHIPS * in_cols)

    cols = w_ada.shape[2]
    b_sh = lax.dynamic_slice(b_ada, (0, r * cols), (1, cols))
    mod_cols = _mod_fwd(c_all, w_ada[0], b_sh)
    mod_all = _chip_exchange("mod_exchange", mod_cols)
    me = 4 * lax.axis_index("x") + 2 * lax.axis_index("y") + lax.axis_index("c")
    mod = lax.dynamic_slice(mod_all, (0, me, 0), (N_CHIPS, 1, cols)).reshape(9, D)

    grad_x, small, parts = _local_step(x[0], loss_target[0], mod, g_ffn1, g_mix, g_ffn2, pool_scale, q_gain, k_gain,
                                        sinks, rel_bias, W)

    small_w = dict(dmod=b_ada, g_ffn1=g_ffn1, g_mix=g_mix, g_ffn2=g_ffn2, pool_scale=pool_scale, q_gain=q_gain, k_gain=k_gain,
                   sinks=sinks, rel_bias=rel_bias, loss=jnp.zeros((1, 1), F32))
    small_m = dict(dmod=m_b_ada, g_ffn1=m_g_ffn1, g_mix=m_g_mix, g_ffn2=m_g_ffn2, pool_scale=m_pool_scale, q_gain=m_q_gain,
                   k_gain=m_k_gain, sinks=m_sinks, rel_bias=m_rel_bias, loss=jnp.zeros((1, 1), F32))
    small_v = dict(dmod=v_b_ada, g_ffn1=v_g_ffn1, g_mix=v_g_mix, g_ffn2=v_g_ffn2, pool_scale=v_pool_scale, q_gain=v_q_gain,
                   k_gain=v_k_gain, sinks=v_sinks, rel_bias=v_rel_bias, loss=jnp.ones((1, 1), F32))
    small_all = _gather_all("gather_small", _pack_small(small))
    sg, sd, sm, sv = [_unpack_small(a, small_w) for a in
                      _small_finish(small_all, _pack_small(small_w), _pack_small(small_m), _pack_small(small_v))]
    loss = sg["loss"].reshape(())

    n_mod = 9 * D
    dmod_all = small_all.reshape(N_DEV, -1)[:, :n_mod]
    dmod_sh = lax.dynamic_slice(dmod_all, (0, r * cols), (N_DEV, cols))
    g_ada, d_ada, nm_ada, nv_ada = _wada_bwd(c_all, dmod_sh, w_ada[0], m_w_ada[0], v_w_ada[0])

    parts = dict(parts)
    parts["w_in"] = parts["w_in"].reshape(D, N_CHIPS, in_cols).transpose(1, 0, 2)
    parts["pool_mix"] = parts["pool_mix"].astype(BF).reshape(4, N_CHIPS, gw // N_CHIPS, gw).transpose(1, 0, 2, 3).reshape(N_CHIPS, gw, gw)
    own, got = _pair_split([parts[k] for k in BIG])
    sums = [_add_pair("add_pair_" + k, a, b) for k, a, b in zip(BIG, own, got)]
    landed = _to_owners(sums)
    halves = [_sum_chips("sum_chips_" + k, u) for k, u in zip(BIG, landed)]
    joined = _pair_join(halves)
    grads, deltas, new_m, new_v = {}, {}, {}, {}
    for k, g in zip(BIG, joined):
        shape = w_sh[k].shape
        g2 = g.reshape(w2[k].shape)
        dl, nm, nv = _adam_2d("adam_" + k, w2[k], g2, two_d(m_sh[k]), two_d(v_sh[k]))
        grads[k], deltas[k], new_m[k], new_v[k] = g2.reshape(shape), dl.reshape(shape), nm.reshape(shape), nv.reshape(shape)

    def ordered(big, ada, sm_):
        return (ada[None], sm_["dmod"], sm_["g_ffn1"], big["gu1"], big["down1"], sm_["g_mix"], big["w_in"], big["pool_mix"],
                sm_["pool_scale"], big["pool_up"], sm_["q_gain"], sm_["k_gain"], sm_["sinks"], sm_["rel_bias"], big["attn_up"],
                big["o"], sm_["g_ffn2"], big["gu2"], big["down2"])

    return (loss, grad_x[None], *ordered(grads, g_ada, sg), *ordered(deltas, d_ada, sd), *ordered(new_m, nm_ada, sm),
            *ordered(new_v, nv_ada, sv))
```

```python
import numpy as np
import jax
import jax.numpy as jnp
from jax import lax
from jax.experimental import pallas as pl
from jax.experimental.pallas import tpu as pltpu

BF = jnp.bfloat16
F32 = jnp.float32
MESH = pl.DeviceIdType.MESH

EPS = 1e-6
NEG_INF = -1e30
HEAD_DIM = 64
N_HEADS = 16
N_KV = 2
ATT_W = N_HEADS * HEAD_DIM
KV_W = N_KV * HEAD_DIM
BLK = 128
NUM_BUCKETS = 32
POOL_MAX_W = 16
N_CHIPS = 4
N_DEV = 8
LANES = 128
ADAM_LR, ADAM_B1, ADAM_B2, ADAM_EPS, ADAM_WD, ADAM_STEP = 0.001, 0.9, 0.999, 1e-08, 0.01, 10
VMEM_LIMIT = 52 * 1024 * 1024


def _cp(sem=None):
    return pltpu.CompilerParams(dimension_semantics=sem, vmem_limit_bytes=VMEM_LIMIT)


def _pick(dim, prefs):
    for p in prefs:
        if p <= dim and dim % p == 0:
            return p
    return dim


def _sds(shape, dtype):
    return jax.ShapeDtypeStruct(tuple(shape), dtype)


NN = (((1,), (0,)), ((), ()))
NT = (((1,), (1,)), ((), ()))
TN = (((0,), (0,)), ((), ()))


def _mm(name, grid, dims, a, a_spec, b, b_spec, extras, extra_specs, out_shapes, out_specs, acc_shape, epilogue):
    n_k = grid[2]
    n_e = len(extras)
    n_o = len(out_shapes)

    def body(*refs):
        a_ref, b_ref = refs[0], refs[1]
        e_refs = refs[2:2 + n_e]
        o_refs = refs[2 + n_e:2 + n_e + n_o]
        p = lax.dot_general(a_ref[...].astype(BF), b_ref[...].astype(BF), dims, preferred_element_type=F32)
        if n_k == 1:
            epilogue(p, e_refs, o_refs)
        else:
            acc = refs[-1]
            k = pl.program_id(2)

            @pl.when(k == 0)
            def _():
                acc[...] = p

            @pl.when(k > 0)
            def _():
                acc[...] += p

            @pl.when(k == n_k - 1)
            def _():
                epilogue(acc[...], e_refs, o_refs)

    scratch = [] if n_k == 1 else [pltpu.VMEM(acc_shape, F32)]
    return pl.pallas_call(
        body, grid=grid, in_specs=[a_spec, b_spec, *extra_specs], out_specs=list(out_specs),
        out_shape=list(out_shapes), scratch_shapes=scratch, name=name,
        compiler_params=_cp(("parallel", "parallel", "arbitrary")),
    )(a, b, *extras)


def _store(p, e, o):
    o[0][...] = p.astype(o[0].dtype)


def _rms_mod_fwd(name, x, gain, shift, scale):
    S, D = x.shape
    ts = _pick(S, (512,))

    def body(x_ref, g_ref, sh_ref, sc_ref, h_ref):
        xv = x_ref[...]
        r = lax.rsqrt(jnp.mean(xv * xv, axis=-1, keepdims=True) + EPS)
        n = xv * r * g_ref[...]
        h_ref[...] = (n * (1.0 + sc_ref[...]) + sh_ref[...]).astype(BF)

    row = pl.BlockSpec((ts, D), lambda i: (i, 0))
    vec = pl.BlockSpec((1, D), lambda i: (0, 0))
    return pl.pallas_call(body, grid=(S // ts,), in_specs=[row, vec, vec, vec], out_specs=row,
                          out_shape=_sds((S, D), BF), name=name, compiler_params=_cp(("parallel",)))(x, gain, shift, scale)


def _rms_mod_bwd(name, dh, x, dres, gain, scale):
    S, D = x.shape
    ts = _pick(S, (256,))

    def body(dh_ref, x_ref, dr_ref, g_ref, sc_ref, dx_ref, acc_ref):
        i = pl.program_id(0)
        xv = x_ref[...]
        dhv = dh_ref[...]
        g = g_ref[...]
        r = lax.rsqrt(jnp.mean(xv * xv, axis=-1, keepdims=True) + EPS)
        xhat = xv * r
        dn = dhv * (1.0 + sc_ref[...])
        dxhat = dn * g
        dx_ref[...] = dr_ref[...] + r * (dxhat - xhat * jnp.mean(dxhat * xhat, axis=-1, keepdims=True))
        part = jnp.concatenate([
            jnp.sum(dhv, axis=0, keepdims=True),
            jnp.sum(dhv * (xhat * g), axis=0, keepdims=True),
            jnp.sum(dn * xhat, axis=0, keepdims=True),
            jnp.zeros((5, D), F32)], axis=0)

        @pl.when(i == 0)
        def _():
            acc_ref[...] = part

        @pl.when(i > 0)
        def _():
            acc_ref[...] += part

    row = pl.BlockSpec((ts, D), lambda i: (i, 0))
    vec = pl.BlockSpec((1, D), lambda i: (0, 0))
    return pl.pallas_call(body, grid=(S // ts,), in_specs=[row, row, row, vec, vec],
                          out_specs=[row, pl.BlockSpec((8, D), lambda i: (0, 0))],
                          out_shape=[_sds((S, D), F32), _sds((8, D), F32)], name=name,
                          compiler_params=_cp(("arbitrary",)))(dh, x, dres, gain, scale)


def _gate_bwd(name, dx, f, coef):
    S, D = dx.shape
    ts = _pick(S, (512,))

    def body(dx_ref, f_ref, c_ref, df_ref, acc_ref):
        i = pl.program_id(0)
        dxv = dx_ref[...]
        df_ref[...] = (dxv * c_ref[...]).astype(BF)
        part = jnp.concatenate([jnp.sum(dxv * f_ref[...].astype(F32), axis=0, keepdims=True), jnp.zeros((7, D), F32)], axis=0)

        @pl.when(i == 0)
        def _():
            acc_ref[...] = part

        @pl.when(i > 0)
        def _():
            acc_ref[...] += part

    row = pl.BlockSpec((ts, D), lambda i: (i, 0))
    return pl.pallas_call(body, grid=(S // ts,), in_specs=[row, row, pl.BlockSpec((1, D), lambda i: (0, 0))],
                          out_specs=[row, pl.BlockSpec((8, D), lambda i: (0, 0))],
                          out_shape=[_sds((S, D), BF), _sds((8, D), F32)], name=name,
                          compiler_params=_cp(("arbitrary",)))(dx, f, coef)


def _loss_bwd(x3, target):
    S, D = x3.shape
    ts = _pick(S, (512,))

    def body(x_ref, t_ref, dx_ref, acc_ref):
        i = pl.program_id(0)
        e = x_ref[...] - t_ref[...]
        dx_ref[...] = e * (1.0 / D)
        part = jnp.concatenate([jnp.sum(e * e, axis=0, keepdims=True), jnp.zeros((7, D), F32)], axis=0)

        @pl.when(i == 0)
        def _():
            acc_ref[...] = part

        @pl.when(i > 0)
        def _():
            acc_ref[...] += part

    row = pl.BlockSpec((ts, D), lambda i: (i, 0))
    return pl.pallas_call(body, grid=(S // ts,), in_specs=[row, row], out_specs=[row, pl.BlockSpec((8, D), lambda i: (0, 0))],
                          out_shape=[_sds((S, D), F32), _sds((8, D), F32)], name="loss_bwd",
                          compiler_params=_cp(("arbitrary",)))(x3, target)


def _silu_parts(g):
    s = jax.nn.sigmoid(g)
    return s, g * s


def _ffn_up(name, h, wgu4):
    S, D = h.shape
    SH = wgu4.shape[2]
    F = 2 * SH
    tm = _pick(S, (1024,))
    tn = _pick(SH, (256,))
    nts = SH // tn

    def body(h_ref, wg_ref, wu_ref, gu_ref, act_ref):
        hv = h_ref[...]
        g = jnp.dot(hv, wg_ref[...], preferred_element_type=F32)
        u = jnp.dot(hv, wu_ref[...], preferred_element_type=F32)
        gu_ref[0] = g.astype(BF)
        gu_ref[1] = u.astype(BF)
        act_ref[...] = (_silu_parts(g)[1] * u).astype(BF)

    return pl.pallas_call(
        body, grid=(S // tm, F // tn),
        in_specs=[pl.BlockSpec((tm, D), lambda i, j: (i, 0)),
                  pl.BlockSpec((None, D, tn), lambda i, j: (j // nts, 0, j % nts)),
                  pl.BlockSpec((None, D, tn), lambda i, j: (2 + j // nts, 0, j % nts))],
        out_specs=[pl.BlockSpec((2, tm, tn), lambda i, j: (0, i, j)), pl.BlockSpec((tm, tn), lambda i, j: (i, j))],
        out_shape=[_sds((2, S, F), BF), _sds((S, F), BF)], name=name,
        compiler_params=_cp(("parallel", "parallel")))(h, wgu4, wgu4)


def _mm_residual(name, a, w, x_in, coef):
    S, K = a.shape
    D = w.shape[1]
    tm = _pick(S, (1024,))
    tn = _pick(D, (1024,))
    tk = _pick(K, (1408, 2048, 1024, 512))

    def epi(p, e, o):
        o[0][...] = e[0][...] + e[1][...] * p
        o[1][...] = p.astype(BF)

    tile = pl.BlockSpec((tm, tn), lambda i, j, k: (i, j))
    return _mm(name, (S // tm, D // tn, K // tk), NN,
               a, pl.BlockSpec((tm, tk), lambda i, j, k: (i, k)),
               w, pl.BlockSpec((tk, tn), lambda i, j, k: (k, j)),
               [x_in, coef], [tile, pl.BlockSpec((1, tn), lambda i, j, k: (0, j))],
               [_sds((S, D), F32), _sds((S, D), BF)], [tile, tile], (tm, tn), epi)


def _ffn_dact(name, df, wd, gu):
    S, D = df.shape
    F = wd.shape[0]
    tm = _pick(S, (1024,))
    tn = _pick(F, (256,))

    def epi(p, e, o):
        g = e[0][0].astype(F32)
        u = e[0][1].astype(F32)
        s, sg = _silu_parts(g)
        o[0][0] = (p * u * (s * (1.0 + g * (1.0 - s)))).astype(BF)
        o[0][1] = (p * sg).astype(BF)

    pair = pl.BlockSpec((2, tm, tn), lambda i, j, k: (0, i, j))
    return _mm(name, (S // tm, F // tn, 1), NT,
               df, pl.BlockSpec((tm, D), lambda i, j, k: (i, 0)),
               wd, pl.BlockSpec((tn, D), lambda i, j, k: (j, 0)),
               [gu], [pair], [_sds((2, S, F), BF)], [pair], None, epi)[0]


def _ffn_dh(name, dgu, wgu4):
    _, S, F = dgu.shape
    _, D, SH = wgu4.shape
    tm = _pick(S, (1024,))
    tn = _pick(D, (1024,))
    tk = _pick(SH, (1408, 256))
    nkp = F // tk
    nks = SH // tk
    return _mm(name, (S // tm, D // tn, 2 * nkp), NT,
               dgu, pl.BlockSpec((None, tm, tk), lambda i, j, k: (k // nkp, i, k % nkp)),
               wgu4, pl.BlockSpec((None, tn, tk), lambda i, j, k: (k // nks, j, k % nks)),
               [], [], [_sds((S, D), F32)], [pl.BlockSpec((tm, tn), lambda i, j, k: (i, j))], (tm, tn), _store)[0]


def _ffn_dwgu(name, h, dgu):
    _, S, F = dgu.shape
    D = h.shape[1]
    SH = F // 2
    tk1 = _pick(D, (1024,))
    tn = _pick(SH, (1408, 256))
    ts = _pick(S, (1024,))
    npj = F // tn
    nsj = SH // tn
    return _mm(name, (D // tk1, 2 * npj, S // ts), TN,
               h, pl.BlockSpec((ts, tk1), lambda i, j, k: (k, i)),
               dgu, pl.BlockSpec((None, ts, tn), lambda i, j, k: (j // npj, k, j % npj)),
               [], [], [_sds((4, D, SH), BF)],
               [pl.BlockSpec((None, tk1, tn), lambda i, j, k: (j // nsj, i, j % nsj))], (tk1, tn), _store)[0]


def _mm_tn(name, a, b, tk1_prefs, tn_prefs):
    S, K1 = a.shape
    N = b.shape[1]
    tk1 = _pick(K1, tk1_prefs)
    tn = _pick(N, tn_prefs)
    ts = _pick(S, (1024,))
    return _mm(name, (K1 // tk1, N // tn, S // ts), TN,
               a, pl.BlockSpec((ts, tk1), lambda i, j, k: (k, i)),
               b, pl.BlockSpec((ts, tn), lambda i, j, k: (k, j)),
               [], [], [_sds((K1, N), BF)], [pl.BlockSpec((tk1, tn), lambda i, j, k: (i, j))], (tk1, tn), _store)[0]


def _pool_window(ext, w, back):
    n = ext.shape[0]
    s = ext
    for step in (1, 2, 4, 8):
        sh = pltpu.roll(s, (n - step) if back else step, axis=0)
        s = jnp.where(w > step, s + sh, s)
    return s


def _pool_fwd(z, PW):
    S = z.shape[0]
    tc = _pick(S, (1024,))
    bpg = (PW // 4) // LANES
    H = POOL_MAX_W

    def body(prev_ref, u_ref, o_ref):
        i = pl.program_id(0)
        j = pl.program_id(1)
        w = lax.shift_left(jnp.int32(2), j // bpg)
        u = u_ref[...]
        prev = jnp.where(i > 0, prev_ref[...], 0.0)
        s = _pool_window(jnp.concatenate([prev, u], axis=0), w, False)[H:]
        t = i * tc + lax.broadcasted_iota(jnp.int32, (tc, LANES), 0)
        cnt = jnp.minimum(t + 1, w).astype(F32)
        o_ref[...] = (s / cnt - u).astype(BF)

    r = tc // H
    return pl.pallas_call(
        body, grid=(S // tc, PW // LANES),
        in_specs=[pl.BlockSpec((H, LANES), lambda i, j: (jnp.maximum(i * r - 1, 0), j)),
                  pl.BlockSpec((tc, LANES), lambda i, j: (i, j))],
        out_specs=pl.BlockSpec((tc, LANES), lambda i, j: (i, j)),
        out_shape=_sds((S, PW), BF), name="pool_fwd", compiler_params=_cp(("parallel", "parallel")))(z, z)


def _pool_bwd(dpooled):
    S, PW = dpooled.shape
    tc = _pick(S, (1024,))
    bpg = (PW // 4) // LANES
    H = POOL_MAX_W
    last = S // tc - 1

    def body(dp_ref, nxt_ref, o_ref):
        i = pl.program_id(0)
        j = pl.program_id(1)
        w = lax.shift_left(jnp.int32(2), j // bpg)
        dp = dp_ref[...]
        nxt = jnp.where(i < last, nxt_ref[...], 0.0)
        ext = jnp.concatenate([dp, nxt], axis=0)
        t = i * tc + lax.broadcasted_iota(jnp.int32, (tc + H, LANES), 0)
        cnt = jnp.minimum(t + 1, w).astype(F32)
        s = _pool_window(ext / cnt, w, True)[:tc]
        o_ref[...] = (s - dp).astype(BF)

    r = tc // H
    nh = S // H - 1
    return pl.pallas_call(
        body, grid=(S // tc, PW // LANES),
        in_specs=[pl.BlockSpec((tc, LANES), lambda i, j: (i, j)),
                  pl.BlockSpec((H, LANES), lambda i, j: (jnp.minimum((i + 1) * r, nh), j))],
        out_specs=pl.BlockSpec((tc, LANES), lambda i, j: (i, j)),
        out_shape=_sds((S, PW), BF), name="pool_bwd", compiler_params=_cp(("parallel", "parallel")))(dpooled, dpooled)


def _pool_mix(pooled, pm, scale):
    S, PW = pooled.shape
    gw = PW // 4
    ts = _pick(S, (1024,))

    def epi(p, e, o):
        o[0][...] = (p * e[0][...]).astype(BF)

    tile = pl.BlockSpec((ts, gw), lambda i, j, k: (i, j))
    return _mm("pool_mix", (S // ts, 4, 1), NN, pooled, tile,
               pm, pl.BlockSpec((None, gw, gw), lambda i, j, k: (j, 0, 0)),
               [scale], [pl.BlockSpec((1, gw), lambda i, j, k: (0, j))], [_sds((S, PW), BF)], [tile], None, epi)[0]


def _pool_mix_bwd(pooled, pm, scale, dmixed):
    S, PW = pooled.shape
    gw = PW // 4
    ts = _pick(S, (1024,))

    def body(p_ref, pm_ref, sc_ref, dm_ref, dp_ref, dpm_ref, dsc_ref):
        i = pl.program_id(1)
        p = p_ref[...]
        w = pm_ref[...]
        dm = dm_ref[...]
        pre = jnp.dot(p, w, preferred_element_type=F32)
        dmp = (dm * sc_ref[...]).astype(BF)
        dp_ref[...] = lax.dot_general(dmp, w, NT, preferred_element_type=F32)
        dw = lax.dot_general(p, dmp, TN, preferred_element_type=F32)
        ds = jnp.concatenate([jnp.sum(dm * pre, axis=0, keepdims=True), jnp.zeros((7, gw), F32)], axis=0)

        @pl.when(i == 0)
        def _():
            dpm_ref[...] = dw
            dsc_ref[...] = ds

        @pl.when(i > 0)
        def _():
            dpm_ref[...] += dw
            dsc_ref[...] += ds

    tile = pl.BlockSpec((ts, gw), lambda g, i: (i, g))
    return pl.pallas_call(
        body, grid=(4, S // ts),
        in_specs=[tile, pl.BlockSpec((None, gw, gw), lambda g, i: (g, 0, 0)), pl.BlockSpec((1, gw), lambda g, i: (0, g)), tile],
        out_specs=[tile, pl.BlockSpec((None, gw, gw), lambda g, i: (g, 0, 0)), pl.BlockSpec((8, gw), lambda g, i: (0, g))],
        out_shape=[_sds((S, PW), F32), _sds((4, gw, gw), F32), _sds((8, PW), F32)], name="pool_mix_bwd",
        compiler_params=_cp(("parallel", "arbitrary")))(pooled, pm, scale, dmixed)


def _bucket_onehot():
    ql = np.arange(BLK)[:, None]
    j = np.arange(2 * BLK)[None, :]
    d = BLK + ql - j
    n = np.clip(d, 0, None)
    nf = np.maximum(n, 1).astype(np.float32)
    max_exact = NUM_BUCKETS // 2
    large = max_exact + (np.log(nf / max_exact) / np.log(BLK / max_exact) * (NUM_BUCKETS - max_exact)).astype(np.int32)
    large = np.minimum(large, NUM_BUCKETS - 1)
    bucket = np.where(n < max_exact, n, large).astype(np.int32)
    valid = (d >= 0) & (d < BLK)
    oh = (bucket[None] == np.arange(NUM_BUCKETS)[:, None, None]) & valid[None]
    return oh.reshape(NUM_BUCKETS, BLK * 2 * BLK)


def _bias_table(rel_bias):
    oh = jnp.asarray(_bucket_onehot(), BF)
    tn = 4096

    def body(rb_ref, oh_ref, o_ref):
        rb = rb_ref[...]
        o = oh_ref[...]
        hi = rb.astype(BF)
        r1 = rb - hi.astype(F32)
        mid = r1.astype(BF)
        lo = (r1 - mid.astype(F32)).astype(BF)
        acc = lax.dot_general(hi, o, TN, preferred_element_type=F32)
        acc = acc + lax.dot_general(mid, o, TN, preferred_element_type=F32)
        acc = acc + lax.dot_general(lo, o, TN, preferred_element_type=F32)
        on_band = jnp.sum(o.astype(F32), axis=0, keepdims=True) > 0.5
        o_ref[...] = jnp.where(on_band, acc, NEG_INF)

    n = oh.shape[1]
    return pl.pallas_call(body, grid=(n // tn,),
                          in_specs=[pl.BlockSpec((NUM_BUCKETS, N_HEADS), lambda i: (0, 0)), pl.BlockSpec((NUM_BUCKETS, tn), lambda i: (0, i))],
                          out_specs=pl.BlockSpec((N_HEADS, tn), lambda i: (0, i)), out_shape=_sds((N_HEADS, n), F32),
                          name="bias_table", compiler_params=_cp(("parallel",)))(rel_bias, oh)


def _rel_bias_grad(dl):
    oh = jnp.asarray(_bucket_onehot(), BF)
    n = oh.shape[1]
    tk = 4096

    def body(dl_ref, oh_ref, o_ref):
        i = pl.program_id(0)
        d = dl_ref[...]
        o = oh_ref[...]
        hi = d.astype(BF)
        r1 = d - hi.astype(F32)
        mid = r1.astype(BF)
        lo = (r1 - mid.astype(F32)).astype(BF)
        acc = lax.dot_general(o, hi, NT, preferred_element_type=F32)
        acc = acc + lax.dot_general(o, mid, NT, preferred_element_type=F32)
        acc = acc + lax.dot_general(o, lo, NT, preferred_element_type=F32)

        @pl.when(i == 0)
        def _():
            o_ref[...] = acc

        @pl.when(i > 0)
        def _():
            o_ref[...] += acc

    return pl.pallas_call(body, grid=(n // tk,),
                          in_specs=[pl.BlockSpec((N_HEADS, tk), lambda i: (0, i)), pl.BlockSpec((NUM_BUCKETS, tk), lambda i: (0, i))],
                          out_specs=pl.BlockSpec((NUM_BUCKETS, N_HEADS), lambda i: (0, 0)), out_shape=_sds((NUM_BUCKETS, N_HEADS), F32),
                          name="rel_bias_grad", compiler_params=_cp(("arbitrary",)))(dl, oh)


def _lo_half(shape):
    return lax.broadcasted_iota(jnp.int32, shape, 1) < HEAD_DIM


def _half_sum(x, lo):
    s_lo = jnp.sum(jnp.where(lo, x, 0.0), axis=-1, keepdims=True)
    s_hi = jnp.sum(jnp.where(lo, 0.0, x), axis=-1, keepdims=True)
    return jnp.where(lo, s_lo, s_hi)


def _norm2(x, lo):
    r = lax.rsqrt(_half_sum(x * x, lo) * (1.0 / HEAD_DIM) + EPS)
    return x * r, r


def _norm2_bwd(dy, xhat, r, gain, lo):
    dxhat = dy * gain
    dx = r * (dxhat - xhat * (_half_sum(dxhat * xhat, lo) * (1.0 / HEAD_DIM)))
    return dx, dy * xhat


def _swap(x):
    return pltpu.roll(x, HEAD_DIM, axis=1)


def _attn_logits(n, kk, zq_ref, kn, qg, bias_ref, sink_ref, lo_k):
    lo_q = _lo_half((BLK, LANES))
    half_k = lo_k if kk == 0 else jnp.logical_not(lo_k)
    K = jnp.where(half_k, kn, 0.0).astype(BF)
    rows, qhats, qrs = [], [], []
    for jp in range(4):
        xq = zq_ref[:, jp * LANES:(jp + 1) * LANES]
        qhat, qr = _norm2(xq, lo_q)
        qn = qhat * qg * (HEAD_DIM ** -0.5)
        qs = _swap(qn)
        rows += [qn, qs] if kk == 0 else [qs, qn]
        qhats.append(qhat)
        qrs.append(qr)
    Q = jnp.concatenate(rows, axis=0).astype(BF)
    qk = lax.dot_general(Q, K, NT, preferred_element_type=F32)
    b = bias_ref[8 * kk:8 * kk + 8].reshape(8 * BLK, 2 * BLK)
    col = lax.broadcasted_iota(jnp.int32, qk.shape, 1)
    ok = (b > -1e29) & ((n > 0) | (col >= BLK))
    l = jnp.where(ok, qk + b, NEG_INF)
    sink = sink_ref[kk]
    m = jnp.maximum(jnp.max(l, axis=-1, keepdims=True), sink)
    e = jnp.exp(l - m)
    es = jnp.exp(sink - m)
    den = jnp.sum(e, axis=-1, keepdims=True) + es
    return Q, K, e / den, es / den, qhats, qrs


def _attn_specs(o_q, o_k):
    nq = o_q // 512
    nk = o_k // LANES
    prev = lambda n: (jnp.maximum(n - 1, 0), nk)
    prev_v = lambda n: (jnp.maximum(n - 1, 0), nk + 1)
    return [pl.BlockSpec((BLK, 512), lambda n: (n, nq)), pl.BlockSpec((BLK, 512), lambda n: (n, nq + 1)),
            pl.BlockSpec((BLK, LANES), prev), pl.BlockSpec((BLK, LANES), lambda n: (n, nk)),
            pl.BlockSpec((BLK, LANES), prev_v), pl.BlockSpec((BLK, LANES), lambda n: (n, nk + 1)),
            pl.BlockSpec((1, LANES), lambda n: (0, 0)), pl.BlockSpec((1, LANES), lambda n: (0, 0)),
            pl.BlockSpec((N_KV, 8 * BLK, 1), lambda n: (0, 0, 0)),
            pl.BlockSpec((N_HEADS, BLK, 2 * BLK), lambda n: (0, 0, 0))]


def _attn_fwd(z, o_q, o_k, qg2, kg2, sink_rows, bias):
    S = z.shape[0]

    def body(zq0, zq1, zkp, zkc, zvp, zvc, qg_ref, kg_ref, sink_ref, bias_ref, o_ref):
        n = pl.program_id(0)
        lo_k = _lo_half((2 * BLK, LANES))
        lo_q = _lo_half((BLK, LANES))
        khat, _ = _norm2(jnp.concatenate([zkp[...], zkc[...]], axis=0), lo_k)
        kn = khat * kg_ref[...]
        vb = jnp.concatenate([zvp[...], zvc[...]], axis=0).astype(BF)
        for kk, zq in enumerate((zq0, zq1)):
            _, _, p, _, _, _ = _attn_logits(n, kk, zq, kn, qg_ref[...], bias_ref, sink_ref, lo_k)
            r = jnp.dot(p.astype(BF), vb, preferred_element_type=F32)
            for jp in range(4):
                ev = r[(2 * jp) * BLK:(2 * jp + 1) * BLK]
                od = r[(2 * jp + 1) * BLK:(2 * jp + 2) * BLK]
                pair = jnp.where(lo_q, ev, _swap(od)) if kk == 0 else jnp.where(lo_q, _swap(ev), od)
                c0 = (4 * kk + jp) * LANES
                o_ref[:, c0:c0 + LANES] = pair.astype(BF)

    return pl.pallas_call(body, grid=(S // BLK,), in_specs=_attn_specs(o_q, o_k),
                          out_specs=pl.BlockSpec((BLK, ATT_W), lambda n: (n, 0)), out_shape=_sds((S, ATT_W), BF),
                          name="attn_fwd", compiler_params=_cp(("parallel",)))(z, z, z, z, z, z, qg2, kg2, sink_rows, bias)


def _attn_bwd(z, o_q, o_k, qg2, kg2, sink_rows, bias, dout):
    S = z.shape[0]

    def body(zq0, zq1, zkp, zkc, zvp, zvc, qg_ref, kg_ref, sink_ref, bias_ref, do_ref,
             dq_ref, dkp_ref, dkc_ref, dvp_ref, dvc_ref, dl_ref, dsink_ref, dgain_ref):
        n = pl.program_id(0)
        lo_k = _lo_half((2 * BLK, LANES))
        lo_q = _lo_half((BLK, LANES))
        qg = qg_ref[...]
        kg = kg_ref[...]
        khat, kr = _norm2(jnp.concatenate([zkp[...], zkc[...]], axis=0), lo_k)
        kn = khat * kg
        vf = jnp.concatenate([zvp[...], zvc[...]], axis=0)

        @pl.when(n == 0)
        def _():
            dl_ref[...] = jnp.zeros_like(dl_ref)
            dsink_ref[...] = jnp.zeros_like(dsink_ref)
            dgain_ref[...] = jnp.zeros_like(dgain_ref)

        dkn = jnp.zeros((2 * BLK, LANES), F32)
        dvb = jnp.zeros((2 * BLK, LANES), F32)
        dqg = jnp.zeros((1, LANES), F32)
        for kk, zq in enumerate((zq0, zq1)):
            half_k = lo_k if kk == 0 else jnp.logical_not(lo_k)
            Q, K, p, ps, qhats, qrs = _attn_logits(n, kk, zq, kn, qg, bias_ref, sink_ref, lo_k)
            rows = []
            for jp in range(4):
                c0 = (4 * kk + jp) * LANES
                x = do_ref[:, c0:c0 + LANES]
                rows += [x, _swap(x)] if kk == 0 else [_swap(x), x]
            dO = jnp.concatenate(rows, axis=0).astype(BF)
            V = jnp.where(half_k, vf, 0.0).astype(BF)
            dP = lax.dot_general(dO, V, NT, preferred_element_type=F32)
            delta = jnp.sum(p * dP, axis=-1, keepdims=True)
            dS = p * (dP - delta)
            dsink_ref[kk] += -ps * delta
            dl_ref[8 * kk:8 * kk + 8] += dS.reshape(8, BLK, 2 * BLK)
            dSb = dS.astype(BF)
            dvb = dvb + jnp.where(half_k, lax.dot_general(p.astype(BF), dO, TN, preferred_element_type=F32), 0.0)
            dkn = dkn + jnp.where(half_k, lax.dot_general(dSb, Q, TN, preferred_element_type=F32), 0.0)
            dQ = jnp.dot(dSb, K, preferred_element_type=F32) * (HEAD_DIM ** -0.5)
            for jp in range(4):
                ev = dQ[(2 * jp) * BLK:(2 * jp + 1) * BLK]
                od = dQ[(2 * jp + 1) * BLK:(2 * jp + 2) * BLK]
                dy = (ev + _swap(od)) if kk == 0 else (_swap(ev) + od)
                dx, gq = _norm2_bwd(dy, qhats[jp], qrs[jp], qg, lo_q)
                dqg = dqg + jnp.sum(gq, axis=0, keepdims=True)
                c0 = (4 * kk + jp) * LANES
                dq_ref[:, c0:c0 + LANES] = dx.astype(BF)
        dk, gk = _norm2_bwd(dkn, khat, kr, kg, lo_k)
        dkp_ref[...] = dk[:BLK]
        dkc_ref[...] = dk[BLK:]
        dvp_ref[...] = dvb[:BLK]
        dvc_ref[...] = dvb[BLK:]
        dgain_ref[...] += jnp.concatenate([dqg, jnp.sum(gk, axis=0, keepdims=True), jnp.zeros((6, LANES), F32)], axis=0)

    blk = pl.BlockSpec((BLK, LANES), lambda n: (n, 0))
    wide = pl.BlockSpec((BLK, ATT_W), lambda n: (n, 0))
    return pl.pallas_call(
        body, grid=(S // BLK,), in_specs=_attn_specs(o_q, o_k) + [wide],
        out_specs=[wide, blk, blk, blk, blk, pl.BlockSpec((N_HEADS, BLK, 2 * BLK), lambda n: (0, 0, 0)),
                   pl.BlockSpec((N_KV, 8 * BLK, 1), lambda n: (0, 0, 0)), pl.BlockSpec((8, LANES), lambda n: (0, 0))],
        out_shape=[_sds((S, ATT_W), BF), _sds((S, LANES), F32), _sds((S, LANES), F32), _sds((S, LANES), F32), _sds((S, LANES), F32),
                   _sds((N_HEADS, BLK, 2 * BLK), F32), _sds((N_KV, 8 * BLK, 1), F32), _sds((8, LANES), F32)],
        name="attn_bwd", compiler_params=_cp(("arbitrary",)))(z, z, z, z, z, z, qg2, kg2, sink_rows, bias, dout)


def _kv_combine(dkp, dkc, dvp, dvc):
    S = dkc.shape[0]
    last = S // BLK - 1

    def body(kp_ref, kc_ref, vp_ref, vc_ref, dk_ref, dv_ref):
        more = pl.program_id(0) < last
        dk_ref[...] = (kc_ref[...] + jnp.where(more, kp_ref[...], 0.0)).astype(BF)
        dv_ref[...] = (vc_ref[...] + jnp.where(more, vp_ref[...], 0.0)).astype(BF)

    cur = pl.BlockSpec((BLK, LANES), lambda n: (n, 0))
    nxt = pl.BlockSpec((BLK, LANES), lambda n: (jnp.minimum(n + 1, last), 0))
    return pl.pallas_call(body, grid=(S // BLK,), in_specs=[nxt, cur, nxt, cur], out_specs=[cur, cur],
                          out_shape=[_sds((S, LANES), BF), _sds((S, LANES), BF)], name="kv_combine",
                          compiler_params=_cp(("parallel",)))(dkp, dkc, dvp, dvc)


def _merge_fwd(mixed, attn, wpu4, wau4, z, o_ga):
    S, PW = mixed.shape
    _, _, CS = wpu4.shape
    D = 4 * CS
    tm = _pick(S, (1024,))
    tn = 256
    nsj = CS // tn
    na = o_ga // tn
    nb = (o_ga + D) // tn

    def body(m_ref, a_ref, wp_ref, wa_ref, ga_ref, gb_ref, mg_ref, yy_ref):
        yp = jnp.dot(m_ref[...], wp_ref[...], preferred_element_type=F32)
        ya = jnp.dot(a_ref[...], wa_ref[...], preferred_element_type=F32)
        mg_ref[...] = (jax.nn.sigmoid(ga_ref[...]) * yp + jax.nn.sigmoid(gb_ref[...]) * ya).astype(BF)
        yy_ref[0] = yp.astype(BF)
        yy_ref[1] = ya.astype(BF)

    return pl.pallas_call(
        body, grid=(S // tm, D // tn),
        in_specs=[pl.BlockSpec((tm, PW), lambda i, j: (i, 0)), pl.BlockSpec((tm, ATT_W), lambda i, j: (i, 0)),
                  pl.BlockSpec((None, PW, tn), lambda i, j: (j // nsj, 0, j % nsj)),
                  pl.BlockSpec((None, ATT_W, tn), lambda i, j: (j // nsj, 0, j % nsj)),
                  pl.BlockSpec((tm, tn), lambda i, j: (i, na + j)), pl.BlockSpec((tm, tn), lambda i, j: (i, nb + j))],
        out_specs=[pl.BlockSpec((tm, tn), lambda i, j: (i, j)), pl.BlockSpec((2, tm, tn), lambda i, j: (0, i, j))],
        out_shape=[_sds((S, D), BF), _sds((2, S, D), BF)], name="merge_fwd",
        compiler_params=_cp(("parallel", "parallel")))(mixed, attn, wpu4, wau4, z, z)


def _merge_bwd(do, wo, z, o_ga, yy):
    S, D = do.shape
    tm = _pick(S, (1024,))
    tn = 256
    na = o_ga // tn
    nb = (o_ga + D) // tn

    def epi(p, e, o):
        sa = jax.nn.sigmoid(e[0][...])
        sb = jax.nn.sigmoid(e[1][...])
        yp = e[2][0].astype(F32)
        ya = e[2][1].astype(F32)
        o[0][0] = (p * yp * sa * (1.0 - sa)).astype(BF)
        o[0][1] = (p * ya * sb * (1.0 - sb)).astype(BF)
        o[1][0] = (p * sa).astype(BF)
        o[1][1] = (p * sb).astype(BF)

    pair = pl.BlockSpec((2, tm, tn), lambda i, j, k: (0, i, j))
    return _mm("merge_bwd", (S // tm, D // tn, 1), NT,
               do, pl.BlockSpec((tm, D), lambda i, j, k: (i, 0)),
               wo, pl.BlockSpec((tn, D), lambda i, j, k: (j, 0)),
               [z, z, yy], [pl.BlockSpec((tm, tn), lambda i, j, k: (i, na + j)), pl.BlockSpec((tm, tn), lambda i, j, k: (i, nb + j)), pair],
               [_sds((2, S, D), BF), _sds((2, S, D), BF)], [pair, pair], None, epi)


def _mm_up_t(name, dyy, which, w4):
    _, S, D = dyy.shape
    _, K, CS = w4.shape
    tm = _pick(S, (1024,))
    return _mm(name, (S // tm, 1, N_CHIPS), NT,
               dyy, pl.BlockSpec((None, tm, CS), lambda i, j, k: (which, i, k)),
               w4, pl.BlockSpec((None, K, CS), lambda i, j, k: (k, 0, 0)),
               [], [], [_sds((S, K), F32)], [pl.BlockSpec((tm, K), lambda i, j, k: (i, 0))], (tm, K), _store)[0]


def _mm_up_dw(name, a, dyy, which):
    _, S, D = dyy.shape
    K = a.shape[1]
    CS = D // N_CHIPS
    ts = _pick(S, (1024,))
    return _mm(name, (1, N_CHIPS, S // ts), TN,
               a, pl.BlockSpec((ts, K), lambda i, j, k: (k, 0)),
               dyy, pl.BlockSpec((None, ts, CS), lambda i, j, k: (which, k, j)),
               [], [], [_sds((N_CHIPS, K, CS), BF)], [pl.BlockSpec((None, K, CS), lambda i, j, k: (j, 0, 0))], (K, CS), _store)[0]


def _adamw(w, g, m, v):
    m = ADAM_B1 * m + (1.0 - ADAM_B1) * g
    v = ADAM_B2 * v + (1.0 - ADAM_B2) * (g * g)
    m_hat = m / (1.0 - ADAM_B1 ** ADAM_STEP)
    v_hat = v / (1.0 - ADAM_B2 ** ADAM_STEP)
    delta = -ADAM_LR * (m_hat / (jnp.sqrt(v_hat) + ADAM_EPS) + ADAM_WD * w)
    return delta, m, v


def _mod_fwd(c_all, w_ada, b_sh):
    D, cols = w_ada.shape
    tn = cols // 9

    def body(c_ref, w_ref, b_ref, o_ref):
        cv = c_ref[...]
        sc = (cv * jax.nn.sigmoid(cv)).astype(BF)
        o_ref[...] = jnp.dot(sc, w_ref[...].astype(BF), preferred_element_type=F32) + b_ref[...]

    return pl.pallas_call(body, grid=(9,),
                          in_specs=[pl.BlockSpec((N_DEV, D), lambda j: (0, 0)), pl.BlockSpec((D, tn), lambda j: (0, j)),
                                    pl.BlockSpec((1, tn), lambda j: (0, j))],
                          out_specs=pl.BlockSpec((N_DEV, tn), lambda j: (0, j)), out_shape=_sds((N_DEV, cols), F32),
                          name="mod_fwd", compiler_params=_cp(("parallel",)))(c_all, w_ada, b_sh)


def _wada_bwd(c_all, dmod_sh, w, m, v):
    D, cols = w.shape
    tn = cols // 18

    def body(c_ref, d_ref, w_ref, m_ref, v_ref, g_ref, dl_ref, nm_ref, nv_ref):
        cv = c_ref[...]
        sc = (cv * jax.nn.sigmoid(cv)).astype(BF)
        g = lax.dot_general(sc, d_ref[...].astype(BF), TN, preferred_element_type=F32)
        g_ref[...] = g
        dl_ref[...], nm_ref[...], nv_ref[...] = _adamw(w_ref[...], g, m_ref[...], v_ref[...])

    tile = pl.BlockSpec((D, tn), lambda j: (0, j))
    out = _sds((D, cols), F32)
    return pl.pallas_call(body, grid=(18,),
                          in_specs=[pl.BlockSpec((N_DEV, D), lambda j: (0, 0)), pl.BlockSpec((N_DEV, tn), lambda j: (0, j)), tile, tile, tile],
                          out_specs=[tile] * 4, out_shape=[out] * 4, name="wada_bwd",
                          compiler_params=_cp(("parallel",)))(c_all, dmod_sh, w, m, v)


def _adam_2d(name, w, g, m, v):
    R, C = w.shape
    tr = _pick(R, (256, 128, 64, 8))

    def body(w_ref, g_ref, m_ref, v_ref, dl_ref, nm_ref, nv_ref):
        dl_ref[...], nm_ref[...], nv_ref[...] = _adamw(w_ref[...], g_ref[...], m_ref[...], v_ref[...])

    tile = pl.BlockSpec((tr, C), lambda i: (i, 0))
    out = _sds((R, C), F32)
    return pl.pallas_call(body, grid=(R // tr,), in_specs=[tile] * 4, out_specs=[tile] * 3, out_shape=[out] * 3,
                          name=name, compiler_params=_cp(("parallel",)))(w, g, m, v)


def _small_finish(parts, w, m, v):
    _, R, C = parts.shape

    def body(p_ref, w_ref, m_ref, v_ref, g_ref, dl_ref, nm_ref, nv_ref):
        g = p_ref[0]
        for d in range(1, N_DEV):
            g = g + p_ref[d]
        g_ref[...] = g
        dl_ref[...], nm_ref[...], nv_ref[...] = _adamw(w_ref[...], g, m_ref[...], v_ref[...])

    out = _sds((R, C), F32)
    return pl.pallas_call(body, out_shape=[out] * 4, name="small_finish", compiler_params=_cp())(parts, w, m, v)


ANY = pl.BlockSpec(memory_space=pl.ANY)


def _place():
    x, y, c = lax.axis_index("x"), lax.axis_index("y"), lax.axis_index("c")
    return x, y, c


def _other_chips(x, y):
    return [(1 - x, y), (x, 1 - y), (1 - x, 1 - y)]


def _chip_of(chip):
    return 2 * chip[0] + chip[1]


def _half_rows(ref, lead, cc, h):
    return ref.at[lead, pl.ds(pl.multiple_of(cc * h, 16), h), :]


class _Stage:
    def __init__(self, bufs, outs, alias, n_sem, start, wait):
        self.bufs, self.outs, self.alias, self.n_sem, self.start, self.wait = bufs, outs, alias, n_sem, start, wait


def _stage_plumbing(stages, n_in0, n_out0):
    bufs, outs, aliases, spans, scratch = [], [], {}, [], []
    for st in stages:
        i0, o0 = len(bufs), len(outs)
        bufs += list(st.bufs)
        outs += list(st.outs)
        for a, b in st.alias.items():
            aliases[n_in0 + i0 + a] = n_out0 + o0 + b
        spans.append((i0, len(bufs), o0, len(outs)))
        scratch += [pltpu.SemaphoreType.DMA((st.n_sem,)), pltpu.SemaphoreType.DMA((st.n_sem,))]

    def run(which, in_refs, out_refs, sem_refs):
        for s, st in enumerate(stages):
            i0, i1, o0, o1 = spans[s]
            getattr(st, which)(in_refs[i0:i1], out_refs[o0:o1], sem_refs[2 * s], sem_refs[2 * s + 1])

    def split(flat):
        return [list(flat[o0:o1]) for (_, _, o0, o1) in spans]

    return bufs, outs, aliases, scratch, run, split


def _run_stages(name, stages):
    bufs, outs, aliases, scratch, run, split = _stage_plumbing(stages, 0, 0)
    ni, no = len(bufs), len(outs)

    def body(*refs):
        ins, os_, sems = refs[:ni], refs[ni:ni + no], refs[ni + no:]
        run("start", ins, os_, sems)
        run("wait", ins, os_, sems)

    res = pl.pallas_call(body, in_specs=[ANY] * ni, out_specs=[ANY] * no, out_shape=outs, input_output_aliases=aliases,
                         scratch_shapes=scratch, name=name)(*bufs)
    return split(res)


def _gather_ici_stage(fulls):
    n = len(fulls)

    def copy(i, j, slot, ins, outs, send, recv):
        x, y, c = _place()
        chip = _other_chips(x, y)[j]
        h = fulls[i].shape[1] // 2
        return pltpu.make_async_remote_copy(_half_rows(ins[i], 2 * x + y, c, h), _half_rows(outs[i], slot(x, y, chip), c, h),
                                            send.at[3 * i + j], recv.at[3 * i + j], device_id=(*chip, c), device_id_type=MESH)

    mine = lambda x, y, chip: 2 * x + y
    theirs = lambda x, y, chip: _chip_of(chip)

    def start(ins, outs, send, recv):
        for i in range(n):
            for j in range(3):
                copy(i, j, mine, ins, outs, send, recv).start()

    def wait(ins, outs, send, recv):
        for i in range(n):
            for j in range(3):
                copy(i, j, theirs, ins, outs, send, recv).wait_recv()
        for i in range(n):
            for j in range(3):
                copy(i, j, mine, ins, outs, send, recv).wait_send()

    return _Stage(fulls, [_sds(f.shape, f.dtype) for f in fulls], {i: i for i in range(n)}, 3 * n, start, wait)


def _gather_d2d_stage(fulls):
    n = len(fulls)

    def copy(i, j, cc, ins, outs, send, recv):
        x, y, c = _place()
        rj = _chip_of(_other_chips(x, y)[j])
        h = fulls[i].shape[1] // 2
        half = cc(c)
        return pltpu.make_async_remote_copy(_half_rows(ins[i], rj, half, h), _half_rows(outs[i], rj, half, h),
                                            send.at[3 * i + j], recv.at[3 * i + j], device_id=(x, y, 1 - c), device_id_type=MESH)

    mine = lambda c: c
    theirs = lambda c: 1 - c

    def start(ins, outs, send, recv):
        for i in range(n):
            for j in range(3):
                copy(i, j, mine, ins, outs, send, recv).start()

    def wait(ins, outs, send, recv):
        for i in range(n):
            for j in range(3):
                copy(i, j, theirs, ins, outs, send, recv).wait_recv()
        for i in range(n):
            for j in range(3):
                copy(i, j, mine, ins, outs, send, recv).wait_send()

    return _Stage(fulls, [_sds(f.shape, f.dtype) for f in fulls], {i: i for i in range(n)}, 3 * n, start, wait)


def _split_stage(parts):
    n = len(parts)

    def copy(i, ins, outs, send, recv):
        x, y, c = _place()
        h = parts[i].shape[1] // 2
        return pltpu.make_async_remote_copy(_half_rows(ins[i], slice(None), 1 - c, h), outs[i], send.at[i], recv.at[i],
                                            device_id=(x, y, 1 - c), device_id_type=MESH)

    def start(ins, outs, send, recv):
        for i in range(n):
            copy(i, ins, outs, send, recv).start()

    def wait(ins, outs, send, recv):
        for i in range(n):
            copy(i, ins, outs, send, recv).wait_recv()
        for i in range(n):
            copy(i, ins, outs, send, recv).wait_send()

    return _Stage(parts, [_sds((N_CHIPS, p.shape[1] // 2, p.shape[2]), p.dtype) for p in parts], {}, n, start, wait)


def _owners_stage(sums):
    n = len(sums)

    def copy(i, j, mine, ins, outs, send, recv):
        x, y, c = _place()
        chip = _other_chips(x, y)[j]
        slot = (2 * x + y) if mine else _chip_of(chip)
        return pltpu.make_async_remote_copy(ins[i].at[_chip_of(chip)], outs[i].at[slot], send.at[3 * i + j], recv.at[3 * i + j],
                                            device_id=(*chip, c), device_id_type=MESH)

    def start(ins, outs, send, recv):
        for i in range(n):
            for j in range(3):
                copy(i, j, True, ins, outs, send, recv).start()

    def wait(ins, outs, send, recv):
        for i in range(n):
            for j in range(3):
                copy(i, j, False, ins, outs, send, recv).wait_recv()
        for i in range(n):
            for j in range(3):
                copy(i, j, True, ins, outs, send, recv).wait_send()

    return _Stage(sums, [_sds(s.shape, s.dtype) for s in sums], {}, 3 * n, start, wait)


def _join_stage(gs):
    n = len(gs)

    def copy(i, mine, ins, outs, send, recv):
        x, y, c = _place()
        slot = c if mine else 1 - c
        return pltpu.make_async_remote_copy(ins[i].at[slot], outs[i].at[slot], send.at[i], recv.at[i],
                                            device_id=(x, y, 1 - c), device_id_type=MESH)

    def start(ins, outs, send, recv):
        for i in range(n):
            copy(i, True, ins, outs, send, recv).start()

    def wait(ins, outs, send, recv):
        for i in range(n):
            copy(i, False, ins, outs, send, recv).wait_recv()
        for i in range(n):
            copy(i, True, ins, outs, send, recv).wait_send()

    return _Stage(gs, [_sds(g.shape, g.dtype) for g in gs], {i: i for i in range(n)}, n, start, wait)


def _cast_into_slot(name, w, rc):
    R, C = w.shape
    tr = _pick(R, (256, 128, 64, 16))

    def body(rc_ref, w_ref, o_ref):
        o_ref[...] = w_ref[...].astype(BF)

    grid_spec = pltpu.PrefetchScalarGridSpec(
        num_scalar_prefetch=1, grid=(R // tr,), in_specs=[pl.BlockSpec((tr, C), lambda i, rc_ref: (i, 0))],
        out_specs=pl.BlockSpec((None, tr, C), lambda i, rc_ref: (rc_ref[0], i, 0)))
    return pl.pallas_call(body, grid_spec=grid_spec, out_shape=_sds((N_CHIPS, R, C), BF), name=name,
                          compiler_params=_cp(("parallel",)))(rc, w)


def _chip_exchange(name, arr):
    def body(src, dst, send, recv, loc):
        x, y, c = _place()
        r = 2 * x + y
        chips = _other_chips(x, y)

        def cp(j, slot):
            return pltpu.make_async_remote_copy(src, dst.at[slot], send.at[j], recv.at[j], device_id=(*chips[j], c), device_id_type=MESH)

        mine = pltpu.make_async_copy(src, dst.at[r], loc)
        mine.start()
        for j in range(3):
            cp(j, r).start()
        for j in range(3):
            cp(j, 2 * chips[j][0] + chips[j][1]).wait_recv()
        for j in range(3):
            cp(j, r).wait_send()
        mine.wait()

    return pl.pallas_call(body, in_specs=[ANY], out_specs=ANY, out_shape=_sds((N_CHIPS, *arr.shape), arr.dtype),
                          scratch_shapes=[pltpu.SemaphoreType.DMA((3,)), pltpu.SemaphoreType.DMA((3,)), pltpu.SemaphoreType.DMA],
                          name=name)(arr)


def _gather_all(name, arr):
    def body(src, dst, send, recv, loc):
        x, y, c = _place()

        def cp(k, slot_of_me):
            px, py, pc = x ^ ((k >> 2) & 1), y ^ ((k >> 1) & 1), c ^ (k & 1)
            slot = (4 * x + 2 * y + c) if slot_of_me else (4 * px + 2 * py + pc)
            return pltpu.make_async_remote_copy(src, dst.at[slot], send.at[k - 1], recv.at[k - 1],
                                                device_id=(px, py, pc), device_id_type=MESH)

        mine = pltpu.make_async_copy(src, dst.at[4 * x + 2 * y + c], loc)
        mine.start()
        for k in range(1, N_DEV):
            cp(k, True).start()
        for k in range(1, N_DEV):
            cp(k, False).wait_recv()
        for k in range(1, N_DEV):
            cp(k, True).wait_send()
        mine.wait()

    return pl.pallas_call(body, in_specs=[ANY], out_specs=ANY, out_shape=_sds((N_DEV, *arr.shape), arr.dtype),
                          scratch_shapes=[pltpu.SemaphoreType.DMA((N_DEV - 1,)), pltpu.SemaphoreType.DMA((N_DEV - 1,)), pltpu.SemaphoreType.DMA],
                          name=name)(arr)


def _add_pair(name, p, q, rc):
    _, H, C = q.shape
    tr = _pick(H, (512, 256, 128, 64, 16))
    nt = H // tr

    def body(rc_ref, p_ref, q_ref, o_ref):
        o_ref[...] = (p_ref[...].astype(F32) + q_ref[...].astype(F32)).astype(BF)

    tile = pl.BlockSpec((None, tr, C), lambda k, i, rc_ref: (k, i, 0))
    grid_spec = pltpu.PrefetchScalarGridSpec(
        num_scalar_prefetch=1, grid=(N_CHIPS, nt),
        in_specs=[pl.BlockSpec((None, tr, C), lambda k, i, rc_ref: (k, rc_ref[1] * nt + i, 0)), tile], out_specs=tile)
    return pl.pallas_call(body, grid_spec=grid_spec, out_shape=_sds(q.shape, BF), name=name,
                          compiler_params=_cp(("parallel", "parallel")))(rc, p, q)


def _sum_chips(name, u, t, rc):
    _, H, C = u.shape
    tr = _pick(H, (256, 128, 64, 16))

    def body(rc_ref, u_ref, t_ref, o_ref):
        r = rc_ref[0]
        own = t_ref[...].astype(F32)
        pick = lambda k: jnp.where(r == k, own, u_ref[k].astype(F32))
        o_ref[...] = ((pick(0) + pick(1)) + pick(2)) + pick(3)

    grid_spec = pltpu.PrefetchScalarGridSpec(
        num_scalar_prefetch=1, grid=(H // tr,),
        in_specs=[pl.BlockSpec((N_CHIPS, tr, C), lambda i, rc_ref: (0, i, 0)),
                  pl.BlockSpec((None, tr, C), lambda i, rc_ref: (rc_ref[0], i, 0))],
        out_specs=pl.BlockSpec((None, tr, C), lambda i, rc_ref: (rc_ref[1], i, 0)))
    return pl.pallas_call(body, grid_spec=grid_spec, out_shape=_sds((2, H, C), F32), name=name,
                          compiler_params=_cp(("parallel",)))(rc, u, t)


def _row(a, i):
    return a[i:i + 1]


def _local_step(x, target, mod, g_ffn1, g_mix, g_ffn2, pool_scale, q_gain, k_gain, sinks, rel_bias, W):
    S, D = x.shape
    PW = W["pool_up"].shape[1]
    o_q, o_k = PW, PW + ATT_W
    o_ga = o_k + 2 * KV_W
    half = 0.5 * mod
    tile2 = lambda g: jnp.concatenate([g, g], axis=1)
    qg2, kg2 = tile2(q_gain), tile2(k_gain)
    sink_rows = jnp.broadcast_to(sinks.reshape(N_KV, 8, 1, 1), (N_KV, 8, BLK, 1)).reshape(N_KV, 8 * BLK, 1)
    bias = _bias_table(rel_bias).reshape(N_HEADS, BLK, 2 * BLK)

    h1 = _rms_mod_fwd("rms_mod_fwd1", x, g_ffn1, _row(mod, 0), _row(mod, 1))
    gu1, act1 = _ffn_up("ffn1_up", h1, W["gu1"])
    x1, f1 = _mm_residual("ffn1_down", act1, W["down1"], x, _row(half, 2))
    h2 = _rms_mod_fwd("rms_mod_fwd2", x1, g_mix, _row(mod, 3), _row(mod, 4))
    IN_W = W["in"].shape[1]
    tnz = _pick(IN_W, (1280, 256))
    tmz = _pick(S, (1024,))
    z = _mm("mix_in", (S // tmz, IN_W // tnz, 1), NN, h2, pl.BlockSpec((tmz, D), lambda i, j, k: (i, 0)),
            W["in"], pl.BlockSpec((D, tnz), lambda i, j, k: (0, j)), [], [], [_sds((S, IN_W), F32)],
            [pl.BlockSpec((tmz, tnz), lambda i, j, k: (i, j))], None, _store)[0]
    pooled = _pool_fwd(z, PW)
    mixed = _pool_mix(pooled, W["pool_mix"], pool_scale)
    attn = _attn_fwd(z, o_q, o_k, qg2, kg2, sink_rows, bias)
    merged, yy = _merge_fwd(mixed, attn, W["pool_up"], W["attn_up"], z, o_ga)
    x2, fo = _mm_residual("mix_out", merged, W["o"], x1, _row(mod, 5))
    h3 = _rms_mod_fwd("rms_mod_fwd3", x2, g_ffn2, _row(mod, 6), _row(mod, 7))
    gu2, act2 = _ffn_up("ffn2_up", h3, W["gu2"])
    x3, f2 = _mm_residual("ffn2_down", act2, W["down2"], x2, _row(half, 8))
    dx3, loss_acc = _loss_bwd(x3, target)

    df2, dgate8 = _gate_bwd("gate_bwd3", dx3, f2, _row(half, 8))
    dgu2 = _ffn_dact("ffn2_dact", df2, W["down2"], gu2)
    p_down2 = _mm_tn("ffn2_dwd", act2, df2, (1408, 512), (1024,))
    p_gu2 = _ffn_dwgu("ffn2_dwgu", h3, dgu2)
    dh3 = _ffn_dh("ffn2_dh", dgu2, W["gu2"])
    dx2, acc3 = _rms_mod_bwd("rms_mod_bwd3", dh3, x2, dx3, g_ffn2, _row(mod, 7))

    do, dgate5 = _gate_bwd("gate_bwd2", dx2, fo, _row(mod, 5))
    dgab, dyy = _merge_bwd(do, W["o"], z, o_ga, yy)
    p_o = _mm_tn("mix_dwo", merged, do, (1024,), (1024,))
    dmixed = _mm_up_t("pool_up_t", dyy, 0, W["pool_up"])
    dattn = _mm_up_t("attn_up_t", dyy, 1, W["attn_up"])
    p_pool_up = _mm_up_dw("pool_up_dw", mixed, dyy, 0)
    p_attn_up = _mm_up_dw("attn_up_dw", attn, dyy, 1)
    dpooled, dpm, dps = _pool_mix_bwd(pooled, W["pool_mix"], pool_scale, dmixed)
    du_pool = _pool_bwd(dpooled)
    dq, dkp, dkc, dvp, dvc, dl, dsink, dgain = _attn_bwd(z, o_q, o_k, qg2, kg2, sink_rows, bias, dattn)
    dk, dv = _kv_combine(dkp, dkc, dvp, dvc)
    drb = _rel_bias_grad(dl.reshape(N_HEADS, BLK * 2 * BLK))
    dz = jnp.concatenate([du_pool, dq, dk, dv, dgab[0], dgab[1]], axis=1)
    p_in = _mm_tn("mix_dwin", h2, dz, (1024,), (1280, 256))
    tkz = _pick(IN_W, (1280, 256))
    tnd = _pick(D, (1024,))
    dh2 = _mm("mix_dh", (S // tmz, D // tnd, IN_W // tkz), NT, dz, pl.BlockSpec((tmz, tkz), lambda i, j, k: (i, k)),
              W["in"], pl.BlockSpec((tnd, tkz), lambda i, j, k: (j, k)), [], [], [_sds((S, D), F32)],
              [pl.BlockSpec((tmz, tnd), lambda i, j, k: (i, j))], (tmz, tnd), _store)[0]
    dx1, acc2 = _rms_mod_bwd("rms_mod_bwd2", dh2, x1, dx2, g_mix, _row(mod, 4))

    df1, dgate2 = _gate_bwd("gate_bwd1", dx1, f1, _row(half, 2))
    dgu1 = _ffn_dact("ffn1_dact", df1, W["down1"], gu1)
    p_down1 = _mm_tn("ffn1_dwd", act1, df1, (1408, 512), (1024,))
    p_gu1 = _ffn_dwgu("ffn1_dwgu", h1, dgu1)
    dh1 = _ffn_dh("ffn1_dh", dgu1, W["gu1"])
    grad_x, acc1 = _rms_mod_bwd("rms_mod_bwd1", dh1, x, dx1, g_ffn1, _row(mod, 1))

    dmod = jnp.concatenate([_row(acc1, 0), _row(acc1, 1), 0.5 * _row(dgate2, 0),
                            _row(acc2, 0), _row(acc2, 1), _row(dgate5, 0),
                            _row(acc3, 0), _row(acc3, 1), 0.5 * _row(dgate8, 0)], axis=0)
    fold = lambda r: r[:, :HEAD_DIM] + r[:, HEAD_DIM:]
    small = dict(
        dmod=dmod, g_ffn1=_row(acc1, 2), g_mix=_row(acc2, 2), g_ffn2=_row(acc3, 2), pool_scale=_row(dps, 0),
        q_gain=fold(_row(dgain, 0)), k_gain=fold(_row(dgain, 1)),
        sinks=jnp.sum(dsink.reshape(N_HEADS, BLK), axis=1).reshape(1, N_HEADS), rel_bias=drb,
        loss=(0.5 / D) * jnp.sum(_row(loss_acc, 0)).reshape(1, 1))
    by_rows = lambda p: p.reshape(N_CHIPS, p.shape[0] // N_CHIPS, p.shape[1])
    parts = dict(gu1=p_gu1, down1=by_rows(p_down1), w_in=p_in, pool_mix=dpm, pool_up=p_pool_up, attn_up=p_attn_up,
                 o=by_rows(p_o), gu2=p_gu2, down2=by_rows(p_down2))
    return grad_x, small, parts


SMALL_ORDER = ("dmod", "g_ffn1", "g_mix", "g_ffn2", "pool_scale", "q_gain", "k_gain", "sinks", "rel_bias", "loss")


def _pack_small(vals):
    flat = jnp.concatenate([vals[k].reshape(-1) for k in SMALL_ORDER])
    n = flat.shape[0]
    rows = -(-n // (8 * LANES)) * 8
    return jnp.pad(flat, (0, rows * LANES - n)).reshape(rows, LANES)


def _unpack_small(packed, like):
    flat = packed.reshape(-1)
    out, off = {}, 0
    for k in SMALL_ORDER:
        n = int(np.prod(like[k].shape))
        out[k] = flat[off:off + n].reshape(like[k].shape)
        off += n
    return out


BIG = ("gu1", "down1", "w_in", "pool_mix", "pool_up", "attn_up", "o", "gu2", "down2")


def kernel(x, c, w_ada, b_ada, g_ffn1, w_ffn1_gu, w_ffn1_down, g_mix, w_in, pool_mix, pool_scale, w_pool_up, q_gain, k_gain, sinks, rel_bias, w_attn_up, w_o, g_ffn2, w_ffn2_gu, w_ffn2_down, loss_target, m_w_ada, m_b_ada, m_g_ffn1, m_w_ffn1_gu, m_w_ffn1_down, m_g_mix, m_w_in, m_pool_mix, m_pool_scale, m_w_pool_up, m_q_gain, m_k_gain, m_sinks, m_rel_bias, m_w_attn_up, m_w_o, m_g_ffn2, m_w_ffn2_gu, m_w_ffn2_down, v_w_ada, v_b_ada, v_g_ffn1, v_w_ffn1_gu, v_w_ffn1_down, v_g_mix, v_w_in, v_pool_mix, v_pool_scale, v_w_pool_up, v_q_gain, v_k_gain, v_sinks, v_rel_bias, v_w_attn_up, v_w_o, v_g_ffn2, v_w_ffn2_gu, v_w_ffn2_down):
    S, D = x.shape[1], x.shape[2]
    gw = pool_mix.shape[3]
    r = 2 * lax.axis_index("x") + lax.axis_index("y")

    two_d = lambda a: a.reshape(-1, a.shape[-1])
    w_sh = dict(gu1=w_ffn1_gu, down1=w_ffn1_down, w_in=w_in, pool_mix=pool_mix, pool_up=w_pool_up, attn_up=w_attn_up, o=w_o,
                gu2=w_ffn2_gu, down2=w_ffn2_down)
    m_sh = dict(gu1=m_w_ffn1_gu, down1=m_w_ffn1_down, w_in=m_w_in, pool_mix=m_pool_mix, pool_up=m_w_pool_up, attn_up=m_w_attn_up,
                o=m_w_o, gu2=m_w_ffn2_gu, down2=m_w_ffn2_down)
    v_sh = dict(gu1=v_w_ffn1_gu, down1=v_w_ffn1_down, w_in=v_w_in, pool_mix=v_pool_mix, pool_up=v_w_pool_up, attn_up=v_w_attn_up,
                o=v_w_o, gu2=v_w_ffn2_gu, down2=v_w_ffn2_down)
    w2 = {k: two_d(w_sh[k]) for k in BIG}

    rc = jnp.stack([r, lax.axis_index("c")]).astype(jnp.int32)
    full = [_cast_into_slot("cast_" + k, w2[k], rc) for k in BIG]
    full = _run_stages("gather_ici", [_gather_ici_stage(full)])[0]
    full = _run_stages("gather_d2d", [_gather_d2d_stage(full)])[0]
    full = dict(zip(BIG, full))
    c_all = _gather_all("gather_c", jnp.broadcast_to(c, (8, D)))[:, 0, :]
    in_cols = w2["w_in"].shape[1]
    W = dict(gu1=full["gu1"], gu2=full["gu2"],
             down1=full["down1"].reshape(-1, D), down2=full["down2"].reshape(-1, D), o=full["o"].reshape(-1, D),
             pool_up=full["pool_up"], attn_up=full["attn_up"],
             pool_mix=full["pool_mix"].reshape(N_CHIPS, 4, gw // N_CHIPS, gw).transpose(1, 0, 2, 3).reshape(4, gw, gw))
    W["in"] = full["w_in"].transpose(1, 0, 2).reshape(D, N_CHIPS * in_cols)

    cols = w_ada.shape[2]
    b_sh = lax.dynamic_slice(b_ada, (0, r * cols), (1, cols))
    mod_cols = _mod_fwd(c_all, w_ada[0], b_sh)
    mod_all = _chip_exchange("mod_exchange", mod_cols)
    me = 4 * lax.axis_index("x") + 2 * lax.axis_index("y") + lax.axis_index("c")
    mod = lax.dynamic_slice(mod_all, (0, me, 0), (N_CHIPS, 1, cols)).reshape(9, D)

    grad_x, small, parts = _local_step(x[0], loss_target[0], mod, g_ffn1, g_mix, g_ffn2, pool_scale, q_gain, k_gain,
                                        sinks, rel_bias, W)

    small_w = dict(dmod=b_ada, g_ffn1=g_ffn1, g_mix=g_mix, g_ffn2=g_ffn2, pool_scale=pool_scale, q_gain=q_gain, k_gain=k_gain,
                   sinks=sinks, rel_bias=rel_bias, loss=jnp.zeros((1, 1), F32))
    small_m = dict(dmod=m_b_ada, g_ffn1=m_g_ffn1, g_mix=m_g_mix, g_ffn2=m_g_ffn2, pool_scale=m_pool_scale, q_gain=m_q_gain,
                   k_gain=m_k_gain, sinks=m_sinks, rel_bias=m_rel_bias, loss=jnp.zeros((1, 1), F32))
    small_v = dict(dmod=v_b_ada, g_ffn1=v_g_ffn1, g_mix=v_g_mix, g_ffn2=v_g_ffn2, pool_scale=v_pool_scale, q_gain=v_q_gain,
                   k_gain=v_k_gain, sinks=v_sinks, rel_bias=v_rel_bias, loss=jnp.ones((1, 1), F32))
    small_all = _gather_all("gather_small", _pack_small(small))
    sg, sd, sm, sv = [_unpack_small(a, small_w) for a in
                      _small_finish(small_all, _pack_small(small_w), _pack_small(small_m), _pack_small(small_v))]
    loss = sg["loss"].reshape(())

    n_mod = 9 * D
    dmod_all = small_all.reshape(N_DEV, -1)[:, :n_mod]
    dmod_sh = lax.dynamic_slice(dmod_all, (0, r * cols), (N_DEV, cols))
    g_ada, d_ada, nm_ada, nv_ada = _wada_bwd(c_all, dmod_sh, w_ada[0], m_w_ada[0], v_w_ada[0])

    parts = dict(parts)
    parts["w_in"] = parts["w_in"].reshape(D, N_CHIPS, in_cols).transpose(1, 0, 2)
    parts["pool_mix"] = parts["pool_mix"].astype(BF).reshape(4, N_CHIPS, gw // N_CHIPS, gw).transpose(1, 0, 2, 3).reshape(N_CHIPS, gw, gw)
    got = _run_stages("pair_split", [_split_stage([parts[k] for k in BIG])])[0]
    sums = [_add_pair("add_pair_" + k, parts[k], q, rc) for k, q in zip(BIG, got)]
    landed = _run_stages("to_owners", [_owners_stage(sums)])[0]
    halves = [_sum_chips("sum_chips_" + k, u, t, rc) for k, u, t in zip(BIG, landed, sums)]
    joined = _run_stages("pair_join", [_join_stage(halves)])[0]
    grads, deltas, new_m, new_v = {}, {}, {}, {}
    for k, g in zip(BIG, joined):
        shape = w_sh[k].shape
        g2 = g.reshape(w2[k].shape)
        dl, nm, nv = _adam_2d("adam_" + k, w2[k], g2, two_d(m_sh[k]), two_d(v_sh[k]))
        grads[k], deltas[k], new_m[k], new_v[k] = g2.reshape(shape), dl.reshape(shape), nm.reshape(shape), nv.reshape(shape)

    def ordered(big, ada, sm_):
        return (ada[None], sm_["dmod"], sm_["g_ffn1"], big["gu1"], big["down1"], sm_["g_mix"], big["w_in"], big["pool_mix"],
                sm_["pool_scale"], big["pool_up"], sm_["q_gain"], sm_["k_gain"], sm_["sinks"], sm_["rel_bias"], big["attn_up"],
                big["o"], sm_["g_ffn2"], big["gu2"], big["down2"])

    return (loss, grad_x[None], *ordered(grads, g_ada, sg), *ordered(deltas, d_ada, sd), *ordered(new_m, nm_ada, sm),
            *ordered(new_v, nv_ada, sv))
```

```python
import numpy as np
import jax
import jax.numpy as jnp
from jax import lax
from jax.experimental import pallas as pl
from jax.experimental.pallas import tpu as pltpu

BF = jnp.bfloat16
F32 = jnp.float32
MESH = pl.DeviceIdType.MESH

EPS = 1e-6
NEG_INF = -1e30
HEAD_DIM = 64
N_HEADS = 16
N_KV = 2
ATT_W = N_HEADS * HEAD_DIM
KV_W = N_KV * HEAD_DIM
BLK = 128
NUM_BUCKETS = 32
POOL_MAX_W = 16
N_CHIPS = 4
N_DEV = 8
LANES = 128
ADAM_LR, ADAM_B1, ADAM_B2, ADAM_EPS, ADAM_WD, ADAM_STEP = 0.001, 0.9, 0.999, 1e-08, 0.01, 10
VMEM_LIMIT = 52 * 1024 * 1024
ANY = pl.BlockSpec(memory_space=pl.ANY)


def _pick(dim, prefs):
    for p in prefs:
        if p <= dim and dim % p == 0:
            return p
    return dim


def _sds(shape, dtype):
    return jax.ShapeDtypeStruct(tuple(shape), dtype)


def _place():
    return lax.axis_index("x"), lax.axis_index("y"), lax.axis_index("c")


def _other_chips(x, y):
    return [(1 - x, y), (x, 1 - y), (1 - x, 1 - y)]


def _chip_of(chip):
    return 2 * chip[0] + chip[1]


def _half_rows(ref, lead, cc, h):
    return ref.at[lead, pl.ds(pl.multiple_of(cc * h, 16), h), :]


class _Stage:
    def __init__(self, bufs, outs, alias, n_sem, start, wait):
        self.bufs, self.outs, self.alias, self.n_sem, self.start, self.wait = bufs, outs, alias, n_sem, start, wait


def _stage_plumbing(stages, n_in0, n_out0):
    bufs, outs, aliases, spans, scratch = [], [], {}, [], []
    for st in stages:
        i0, o0 = len(bufs), len(outs)
        bufs += list(st.bufs)
        outs += list(st.outs)
        for a, b in st.alias.items():
            aliases[n_in0 + i0 + a] = n_out0 + o0 + b
        spans.append((i0, len(bufs), o0, len(outs)))
        scratch += [pltpu.SemaphoreType.DMA((st.n_sem,)), pltpu.SemaphoreType.DMA((st.n_sem,))]

    def run(which, in_refs, out_refs, sem_refs):
        for s, st in enumerate(stages):
            i0, i1, o0, o1 = spans[s]
            getattr(st, which)(in_refs[i0:i1], out_refs[o0:o1], sem_refs[2 * s], sem_refs[2 * s + 1])

    def split(flat):
        return [list(flat[o0:o1]) for (_, _, o0, o1) in spans]

    return bufs, outs, aliases, scratch, run, split


def _run_stages(name, stages):
    bufs, outs, aliases, scratch, run, split = _stage_plumbing(stages, 0, 0)
    ni, no = len(bufs), len(outs)

    def body(*refs):
        ins, os_, sems = refs[:ni], refs[ni:ni + no], refs[ni + no:]
        run("start", ins, os_, sems)
        run("wait", ins, os_, sems)

    res = pl.pallas_call(body, in_specs=[ANY] * ni, out_specs=[ANY] * no, out_shape=outs, input_output_aliases=aliases,
                         scratch_shapes=scratch, name=name)(*bufs)
    return split(res)


def _call(name, body, grid, in_specs, out_specs, out_shape, args, scratch=(), sem=None, plan=None):
    stages = plan.stages(name) if plan is not None else []
    n_in, n_out, n_scr = len(args), len(out_shape), len(scratch)
    if not stages:
        return pl.pallas_call(body, grid=grid, in_specs=list(in_specs), out_specs=list(out_specs), out_shape=list(out_shape),
                              scratch_shapes=list(scratch), name=name,
                              compiler_params=pltpu.CompilerParams(dimension_semantics=sem, vmem_limit_bytes=VMEM_LIMIT))(*args)
    bufs, s_outs, aliases, s_scratch, run, split = _stage_plumbing(stages, n_in, n_out)
    nb, nso = len(bufs), len(s_outs)

    def hosted(*refs):
        ins = refs[:n_in]
        s_ins = refs[n_in:n_in + nb]
        outs = refs[n_in + nb:n_in + nb + n_out]
        s_os = refs[n_in + nb + n_out:n_in + nb + n_out + nso]
        scr = refs[n_in + nb + n_out + nso:n_in + nb + n_out + nso + n_scr]
        sems = refs[n_in + nb + n_out + nso + n_scr:]
        first = pl.program_id(0) == 0
        last = pl.program_id(0) == grid[0] - 1
        for d in range(1, len(grid)):
            first = first & (pl.program_id(d) == 0)
            last = last & (pl.program_id(d) == grid[d] - 1)

        @pl.when(first)
        def _():
            run("start", s_ins, s_os, sems)

        body(*ins, *outs, *scr)

        @pl.when(last)
        def _():
            run("wait", s_ins, s_os, sems)

    res = pl.pallas_call(
        hosted, grid=grid, in_specs=list(in_specs) + [ANY] * nb, out_specs=list(out_specs) + [ANY] * nso,
        out_shape=list(out_shape) + s_outs, input_output_aliases=aliases, scratch_shapes=list(scratch) + s_scratch, name=name,
        compiler_params=pltpu.CompilerParams(dimension_semantics=("arbitrary",) * len(grid), vmem_limit_bytes=VMEM_LIMIT))(*args, *bufs)
    plan.done(name, split(res[n_out:]))
    return list(res[:n_out])


def _gather_ici_stage(fulls):
    n = len(fulls)

    def copy(i, j, slot, ins, outs, send, recv):
        x, y, c = _place()
        chip = _other_chips(x, y)[j]
        h = fulls[i].shape[1] // 2
        return pltpu.make_async_remote_copy(_half_rows(ins[i], 2 * x + y, c, h), _half_rows(outs[i], slot(x, y, chip), c, h),
                                            send.at[3 * i + j], recv.at[3 * i + j], device_id=(*chip, c), device_id_type=MESH)

    mine = lambda x, y, chip: 2 * x + y
    theirs = lambda x, y, chip: _chip_of(chip)

    def start(ins, outs, send, recv):
        for i in range(n):
            for j in range(3):
                copy(i, j, mine, ins, outs, send, recv).start()

    def wait(ins, outs, send, recv):
        for i in range(n):
            for j in range(3):
                copy(i, j, theirs, ins, outs, send, recv).wait_recv()
        for i in range(n):
            for j in range(3):
                copy(i, j, mine, ins, outs, send, recv).wait_send()

    return _Stage(fulls, [_sds(f.shape, f.dtype) for f in fulls], {i: i for i in range(n)}, 3 * n, start, wait)


def _gather_d2d_stage(fulls):
    n = len(fulls)

    def copy(i, j, cc, ins, outs, send, recv):
        x, y, c = _place()
        rj = _chip_of(_other_chips(x, y)[j])
        h = fulls[i].shape[1] // 2
        half = cc(c)
        return pltpu.make_async_remote_copy(_half_rows(ins[i], rj, half, h), _half_rows(outs[i], rj, half, h),
                                            send.at[3 * i + j], recv.at[3 * i + j], device_id=(x, y, 1 - c), device_id_type=MESH)

    mine = lambda c: c
    theirs = lambda c: 1 - c

    def start(ins, outs, send, recv):
        for i in range(n):
            for j in range(3):
                copy(i, j, mine, ins, outs, send, recv).start()

    def wait(ins, outs, send, recv):
        for i in range(n):
            for j in range(3):
                copy(i, j, theirs, ins, outs, send, recv).wait_recv()
        for i in range(n):
            for j in range(3):
                copy(i, j, mine, ins, outs, send, recv).wait_send()

    return _Stage(fulls, [_sds(f.shape, f.dtype) for f in fulls], {i: i for i in range(n)}, 3 * n, start, wait)


def _split_stage(parts):
    n = len(parts)

    def copy(i, ins, outs, send, recv):
        x, y, c = _place()
        h = parts[i].shape[1] // 2
        return pltpu.make_async_remote_copy(_half_rows(ins[i], slice(None), 1 - c, h), outs[i], send.at[i], recv.at[i],
                                            device_id=(x, y, 1 - c), device_id_type=MESH)

    def start(ins, outs, send, recv):
        for i in range(n):
            copy(i, ins, outs, send, recv).start()

    def wait(ins, outs, send, recv):
        for i in range(n):
            copy(i, ins, outs, send, recv).wait_recv()
        for i in range(n):
            copy(i, ins, outs, send, recv).wait_send()

    return _Stage(parts, [_sds((N_CHIPS, p.shape[1] // 2, p.shape[2]), p.dtype) for p in parts], {}, n, start, wait)


def _owners_stage(sums):
    n = len(sums)

    def copy(i, j, mine, ins, outs, send, recv):
        x, y, c = _place()
        chip = _other_chips(x, y)[j]
        slot = (2 * x + y) if mine else _chip_of(chip)
        return pltpu.make_async_remote_copy(ins[i].at[_chip_of(chip)], outs[i].at[slot], send.at[3 * i + j], recv.at[3 * i + j],
                                            device_id=(*chip, c), device_id_type=MESH)

    def start(ins, outs, send, recv):
        for i in range(n):
            for j in range(3):
                copy(i, j, True, ins, outs, send, recv).start()

    def wait(ins, outs, send, recv):
        for i in range(n):
            for j in range(3):
                copy(i, j, False, ins, outs, send, recv).wait_recv()
        for i in range(n):
            for j in range(3):
                copy(i, j, True, ins, outs, send, recv).wait_send()

    return _Stage(sums, [_sds(s.shape, s.dtype) for s in sums], {}, 3 * n, start, wait)


def _join_stage(gs):
    n = len(gs)

    def copy(i, mine, ins, outs, send, recv):
        x, y, c = _place()
        slot = c if mine else 1 - c
        return pltpu.make_async_remote_copy(ins[i].at[slot], outs[i].at[slot], send.at[i], recv.at[i],
                                            device_id=(x, y, 1 - c), device_id_type=MESH)

    def start(ins, outs, send, recv):
        for i in range(n):
            copy(i, True, ins, outs, send, recv).start()

    def wait(ins, outs, send, recv):
        for i in range(n):
            copy(i, False, ins, outs, send, recv).wait_recv()
        for i in range(n):
            copy(i, True, ins, outs, send, recv).wait_send()

    return _Stage(gs, [_sds(g.shape, g.dtype) for g in gs], {i: i for i in range(n)}, n, start, wait)


def _chip_exchange(name, arr):
    def body(src, dst, send, recv, loc):
        x, y, c = _place()
        r = 2 * x + y
        chips = _other_chips(x, y)

        def cp(j, slot):
            return pltpu.make_async_remote_copy(src, dst.at[slot], send.at[j], recv.at[j], device_id=(*chips[j], c), device_id_type=MESH)

        mine = pltpu.make_async_copy(src, dst.at[r], loc)
        mine.start()
        for j in range(3):
            cp(j, r).start()
        for j in range(3):
            cp(j, _chip_of(chips[j])).wait_recv()
        for j in range(3):
            cp(j, r).wait_send()
        mine.wait()

    return pl.pallas_call(body, in_specs=[ANY], out_specs=ANY, out_shape=_sds((N_CHIPS, *arr.shape), arr.dtype),
                          scratch_shapes=[pltpu.SemaphoreType.DMA((3,)), pltpu.SemaphoreType.DMA((3,)), pltpu.SemaphoreType.DMA],
                          name=name)(arr)


def _gather_all(name, arr):
    def body(src, dst, send, recv, loc):
        x, y, c = _place()

        def cp(k, slot_of_me):
            px, py, pc = x ^ ((k >> 2) & 1), y ^ ((k >> 1) & 1), c ^ (k & 1)
            slot = (4 * x + 2 * y + c) if slot_of_me else (4 * px + 2 * py + pc)
            return pltpu.make_async_remote_copy(src, dst.at[slot], send.at[k - 1], recv.at[k - 1],
                                                device_id=(px, py, pc), device_id_type=MESH)

        mine = pltpu.make_async_copy(src, dst.at[4 * x + 2 * y + c], loc)
        mine.start()
        for k in range(1, N_DEV):
            cp(k, True).start()
        for k in range(1, N_DEV):
            cp(k, False).wait_recv()
        for k in range(1, N_DEV):
            cp(k, True).wait_send()
        mine.wait()

    return pl.pallas_call(body, in_specs=[ANY], out_specs=ANY, out_shape=_sds((N_DEV, *arr.shape), arr.dtype),
                          scratch_shapes=[pltpu.SemaphoreType.DMA((N_DEV - 1,)), pltpu.SemaphoreType.DMA((N_DEV - 1,)), pltpu.SemaphoreType.DMA],
                          name=name)(arr)


NN = (((1,), (0,)), ((), ()))
NT = (((1,), (1,)), ((), ()))
TN = (((0,), (0,)), ((), ()))


def _mm(name, grid, dims, a, a_spec, b, b_spec, extras, extra_specs, out_shapes, out_specs, acc_shape, epilogue, plan=None):
    n_k = grid[2]
    n_e = len(extras)
    n_o = len(out_shapes)

    def body(*refs):
        a_ref, b_ref = refs[0], refs[1]
        e_refs = refs[2:2 + n_e]
        o_refs = refs[2 + n_e:2 + n_e + n_o]
        p = lax.dot_general(a_ref[...].astype(BF), b_ref[...].astype(BF), dims, preferred_element_type=F32)
        if n_k == 1:
            epilogue(p, e_refs, o_refs)
        else:
            acc = refs[-1]
            k = pl.program_id(2)

            @pl.when(k == 0)
            def _():
                acc[...] = p

            @pl.when(k > 0)
            def _():
                acc[...] += p

            @pl.when(k == n_k - 1)
            def _():
                epilogue(acc[...], e_refs, o_refs)

    scratch = [] if n_k == 1 else [pltpu.VMEM(acc_shape, F32)]
    return _call(name, body, grid, [a_spec, b_spec, *extra_specs], out_specs, out_shapes, [a, b, *extras], scratch,
                 ("parallel", "parallel", "arbitrary"), plan)


def _store(p, e, o):
    o[0][...] = p.astype(o[0].dtype)


def _rms_mod_fwd(name, x, gain, shift, scale):
    S, D = x.shape
    ts = _pick(S, (512,))

    def body(x_ref, g_ref, sh_ref, sc_ref, h_ref):
        xv = x_ref[...]
        r = lax.rsqrt(jnp.mean(xv * xv, axis=-1, keepdims=True) + EPS)
        n = xv * r * g_ref[...]
        h_ref[...] = (n * (1.0 + sc_ref[...]) + sh_ref[...]).astype(BF)

    row = pl.BlockSpec((ts, D), lambda i: (i, 0))
    vec = pl.BlockSpec((1, D), lambda i: (0, 0))
    return _call(name, body, (S // ts,), [row, vec, vec, vec], [row], [_sds((S, D), BF)], [x, gain, shift, scale],
                 sem=("parallel",))[0]


def _acc_rows(acc_ref, first, part):
    @pl.when(first)
    def _():
        acc_ref[...] = part

    @pl.when(jnp.logical_not(first))
    def _():
        acc_ref[...] += part


def _rms_mod_bwd(name, dh, x, dres, gain, scale, plan=None):
    S, D = x.shape
    ts = _pick(S, (256,))

    def body(dh_ref, x_ref, dr_ref, g_ref, sc_ref, dx_ref, acc_ref):
        xv = x_ref[...]
        dhv = dh_ref[...]
        g = g_ref[...]
        r = lax.rsqrt(jnp.mean(xv * xv, axis=-1, keepdims=True) + EPS)
        xhat = xv * r
        dn = dhv * (1.0 + sc_ref[...])
        dxhat = dn * g
        dx_ref[...] = dr_ref[...] + r * (dxhat - xhat * jnp.mean(dxhat * xhat, axis=-1, keepdims=True))
        part = jnp.concatenate([
            jnp.sum(dhv, axis=0, keepdims=True),
            jnp.sum(dhv * (xhat * g), axis=0, keepdims=True),
            jnp.sum(dn * xhat, axis=0, keepdims=True),
            jnp.zeros((5, D), F32)], axis=0)
        _acc_rows(acc_ref, pl.program_id(0) == 0, part)

    row = pl.BlockSpec((ts, D), lambda i: (i, 0))
    vec = pl.BlockSpec((1, D), lambda i: (0, 0))
    return _call(name, body, (S // ts,), [row, row, row, vec, vec], [row, pl.BlockSpec((8, D), lambda i: (0, 0))],
                 [_sds((S, D), F32), _sds((8, D), F32)], [dh, x, dres, gain, scale], sem=("arbitrary",), plan=plan)


def _gate_bwd(name, dx, f, coef):
    S, D = dx.shape
    ts = _pick(S, (512,))

    def body(dx_ref, f_ref, c_ref, df_ref, acc_ref):
        dxv = dx_ref[...]
        df_ref[...] = (dxv * c_ref[...]).astype(BF)
        part = jnp.concatenate([jnp.sum(dxv * f_ref[...].astype(F32), axis=0, keepdims=True), jnp.zeros((7, D), F32)], axis=0)
        _acc_rows(acc_ref, pl.program_id(0) == 0, part)

    row = pl.BlockSpec((ts, D), lambda i: (i, 0))
    return _call(name, body, (S // ts,), [row, row, pl.BlockSpec((1, D), lambda i: (0, 0))],
                 [row, pl.BlockSpec((8, D), lambda i: (0, 0))], [_sds((S, D), BF), _sds((8, D), F32)], [dx, f, coef],
                 sem=("arbitrary",))


def _loss_bwd(x3, target):
    S, D = x3.shape
    ts = _pick(S, (512,))

    def body(x_ref, t_ref, dx_ref, acc_ref):
        e = x_ref[...] - t_ref[...]
        dx_ref[...] = e * (1.0 / D)
        part = jnp.concatenate([jnp.sum(e * e, axis=0, keepdims=True), jnp.zeros((7, D), F32)], axis=0)
        _acc_rows(acc_ref, pl.program_id(0) == 0, part)

    row = pl.BlockSpec((ts, D), lambda i: (i, 0))
    return _call("loss_bwd", body, (S // ts,), [row, row], [row, pl.BlockSpec((8, D), lambda i: (0, 0))],
                 [_sds((S, D), F32), _sds((8, D), F32)], [x3, target], sem=("arbitrary",))


def _silu_parts(g):
    s = jax.nn.sigmoid(g)
    return s, g * s


def _ffn_up(name, h, wgu4, plan=None):
    S, D = h.shape
    SH = wgu4.shape[2]
    F = 2 * SH
    tm = _pick(S, (1024,))
    tn = _pick(SH, (256,))
    nts = SH // tn

    def body(h_ref, wg_ref, wu_ref, gu_ref, act_ref):
        hv = h_ref[...]
        g = jnp.dot(hv, wg_ref[...], preferred_element_type=F32)
        u = jnp.dot(hv, wu_ref[...], preferred_element_type=F32)
        gu_ref[0] = g.astype(BF)
        gu_ref[1] = u.astype(BF)
        act_ref[...] = (_silu_parts(g)[1] * u).astype(BF)

    return _call(name, body, (S // tm, F // tn),
                 [pl.BlockSpec((tm, D), lambda i, j: (i, 0)),
                  pl.BlockSpec((None, D, tn), lambda i, j: (j // nts, 0, j % nts)),
                  pl.BlockSpec((None, D, tn), lambda i, j: (2 + j // nts, 0, j % nts))],
                 [pl.BlockSpec((2, tm, tn), lambda i, j: (0, i, j)), pl.BlockSpec((tm, tn), lambda i, j: (i, j))],
                 [_sds((2, S, F), BF), _sds((S, F), BF)], [h, wgu4, wgu4], sem=("parallel", "parallel"), plan=plan)


def _mm_residual(name, a, w, x_in, coef, plan=None):
    S, K = a.shape
    D = w.shape[1]
    tm = _pick(S, (1024,))
    tn = _pick(D, (1024,))
    tk = _pick(K, (1408, 2048, 1024, 512))

    def epi(p, e, o):
        o[0][...] = e[0][...] + e[1][...] * p
        o[1][...] = p.astype(BF)

    tile = pl.BlockSpec((tm, tn), lambda i, j, k: (i, j))
    return _mm(name, (S // tm, D // tn, K // tk), NN,
               a, pl.BlockSpec((tm, tk), lambda i, j, k: (i, k)),
               w, pl.BlockSpec((tk, tn), lambda i, j, k: (k, j)),
               [x_in, coef], [tile, pl.BlockSpec((1, tn), lambda i, j, k: (0, j))],
               [_sds((S, D), F32), _sds((S, D), BF)], [tile, tile], (tm, tn), epi, plan)


def _ffn_dact(name, df, wd, gu, plan=None):
    S, D = df.shape
    F = wd.shape[0]
    tm = _pick(S, (1024,))
    tn = _pick(F, (256,))

    def epi(p, e, o):
        g = e[0][0].astype(F32)
        u = e[0][1].astype(F32)
        s, sg = _silu_parts(g)
        o[0][0] = (p * u * (s * (1.0 + g * (1.0 - s)))).astype(BF)
        o[0][1] = (p * sg).astype(BF)

    pair = pl.BlockSpec((2, tm, tn), lambda i, j, k: (0, i, j))
    return _mm(name, (S // tm, F // tn, 1), NT,
               df, pl.BlockSpec((tm, D), lambda i, j, k: (i, 0)),
               wd, pl.BlockSpec((tn, D), lambda i, j, k: (j, 0)),
               [gu], [pair], [_sds((2, S, F), BF)], [pair], None, epi, plan)[0]


def _ffn_dh(name, dgu, wgu4, plan=None):
    _, S, F = dgu.shape
    _, D, SH = wgu4.shape
    tm = _pick(S, (1024,))
    tn = _pick(D, (1024,))
    tk = _pick(SH, (1408, 256))
    nkp = F // tk
    nks = SH // tk
    return _mm(name, (S // tm, D // tn, 2 * nkp), NT,
               dgu, pl.BlockSpec((None, tm, tk), lambda i, j, k: (k // nkp, i, k % nkp)),
               wgu4, pl.BlockSpec((None, tn, tk), lambda i, j, k: (k // nks, j, k % nks)),
               [], [], [_sds((S, D), F32)], [pl.BlockSpec((tm, tn), lambda i, j, k: (i, j))], (tm, tn), _store, plan)[0]


def _ffn_dwgu(name, h, dgu, plan=None):
    _, S, F = dgu.shape
    D = h.shape[1]
    SH = F // 2
    tk1 = _pick(D, (1024,))
    tn = _pick(SH, (1408, 256))
    ts = _pick(S, (1024,))
    npj = F // tn
    nsj = SH // tn
    return _mm(name, (D // tk1, 2 * npj, S // ts), TN,
               h, pl.BlockSpec((ts, tk1), lambda i, j, k: (k, i)),
               dgu, pl.BlockSpec((None, ts, tn), lambda i, j, k: (j // npj, k, j % npj)),
               [], [], [_sds((4, D, SH), BF)],
               [pl.BlockSpec((None, tk1, tn), lambda i, j, k: (j // nsj, i, j % nsj))], (tk1, tn), _store, plan)[0]


def _mm_tn(name, a, b, tk1_prefs, tn_prefs, plan=None):
    S, K1 = a.shape
    N = b.shape[1]
    tk1 = _pick(K1, tk1_prefs)
    tn = _pick(N, tn_prefs)
    ts = _pick(S, (1024,))
    return _mm(name, (K1 // tk1, N // tn, S // ts), TN,
               a, pl.BlockSpec((ts, tk1), lambda i, j, k: (k, i)),
               b, pl.BlockSpec((ts, tn), lambda i, j, k: (k, j)),
               [], [], [_sds((K1, N), BF)], [pl.BlockSpec((tk1, tn), lambda i, j, k: (i, j))], (tk1, tn), _store, plan)[0]


def _pool_window(ext, w, back):
    n = ext.shape[0]
    s = ext
    for step in (1, 2, 4, 8):
        sh = pltpu.roll(s, (n - step) if back else step, axis=0)
        s = jnp.where(w > step, s + sh, s)
    return s


def _pool_fwd(z, PW):
    S = z.shape[0]
    tc = _pick(S, (1024,))
    bpg = (PW // 4) // LANES
    H = POOL_MAX_W

    def body(prev_ref, u_ref, o_ref):
        i = pl.program_id(0)
        j = pl.program_id(1)
        w = lax.shift_left(jnp.int32(2), j // bpg)
        u = u_ref[...]
        prev = jnp.where(i > 0, prev_ref[...], 0.0)
        s = _pool_window(jnp.concatenate([prev, u], axis=0), w, False)[H:]
        t = i * tc + lax.broadcasted_iota(jnp.int32, (tc, LANES), 0)
        cnt = jnp.minimum(t + 1, w).astype(F32)
        o_ref[...] = (s / cnt - u).astype(BF)

    r = tc // H
    return _call("pool_fwd", body, (S // tc, PW // LANES),
                 [pl.BlockSpec((H, LANES), lambda i, j: (jnp.maximum(i * r - 1, 0), j)),
                  pl.BlockSpec((tc, LANES), lambda i, j: (i, j))],
                 [pl.BlockSpec((tc, LANES), lambda i, j: (i, j))], [_sds((S, PW), BF)], [z, z], sem=("parallel", "parallel"))[0]


def _pool_bwd(dpooled):
    S, PW = dpooled.shape
    tc = _pick(S, (1024,))
    bpg = (PW // 4) // LANES
    H = POOL_MAX_W
    last = S // tc - 1

    def body(dp_ref, nxt_ref, o_ref):
        i = pl.program_id(0)
        j = pl.program_id(1)
        w = lax.shift_left(jnp.int32(2), j // bpg)
        dp = dp_ref[...]
        nxt = jnp.where(i < last, nxt_ref[...], 0.0)
        ext = jnp.concatenate([dp, nxt], axis=0)
        t = i * tc + lax.broadcasted_iota(jnp.int32, (tc + H, LANES), 0)
        cnt = jnp.minimum(t + 1, w).astype(F32)
        s = _pool_window(ext / cnt, w, True)[:tc]
        o_ref[...] = (s - dp).astype(BF)

    r = tc // H
    nh = S // H - 1
    return _call("pool_bwd", body, (S // tc, PW // LANES),
                 [pl.BlockSpec((tc, LANES), lambda i, j: (i, j)),
                  pl.BlockSpec((H, LANES), lambda i, j: (jnp.minimum((i + 1) * r, nh), j))],
                 [pl.BlockSpec((tc, LANES), lambda i, j: (i, j))], [_sds((S, PW), BF)], [dpooled, dpooled],
                 sem=("parallel", "parallel"))[0]


def _pool_mix(pooled, pm, scale):
    S, PW = pooled.shape
    gw = PW // 4
    ts = _pick(S, (1024,))

    def epi(p, e, o):
        o[0][...] = (p * e[0][...]).astype(BF)

    tile = pl.BlockSpec((ts, gw), lambda i, j, k: (i, j))
    return _mm("pool_mix", (S // ts, 4, 1), NN, pooled, tile,
               pm, pl.BlockSpec((None, gw, gw), lambda i, j, k: (j, 0, 0)),
               [scale], [pl.BlockSpec((1, gw), lambda i, j, k: (0, j))], [_sds((S, PW), BF)], [tile], None, epi)[0]


def _pool_mix_bwd(pooled, pm, scale, dmixed):
    S, PW = pooled.shape
    gw = PW // 4
    ts = _pick(S, (1024,))

    def body(p_ref, pm_ref, sc_ref, dm_ref, dp_ref, dpm_ref, dsc_ref):
        i = pl.program_id(1)
        p = p_ref[...]
        w = pm_ref[...]
        dm = dm_ref[...]
        pre = jnp.dot(p, w, preferred_element_type=F32)
        dmp = (dm * sc_ref[...]).astype(BF)
        dp_ref[...] = lax.dot_general(dmp, w, NT, preferred_element_type=F32)
        dw = lax.dot_general(p, dmp, TN, preferred_element_type=F32)
        ds = jnp.concatenate([jnp.sum(dm * pre, axis=0, keepdims=True), jnp.zeros((7, gw), F32)], axis=0)
        _acc_rows(dpm_ref, i == 0, dw)
        _acc_rows(dsc_ref, i == 0, ds)

    tile = pl.BlockSpec((ts, gw), lambda g, i: (i, g))
    return _call("pool_mix_bwd", body, (4, S // ts),
                 [tile, pl.BlockSpec((None, gw, gw), lambda g, i: (g, 0, 0)), pl.BlockSpec((1, gw), lambda g, i: (0, g)), tile],
                 [tile, pl.BlockSpec((None, gw, gw), lambda g, i: (g, 0, 0)), pl.BlockSpec((8, gw), lambda g, i: (0, g))],
                 [_sds((S, PW), F32), _sds((4, gw, gw), F32), _sds((8, PW), F32)], [pooled, pm, scale, dmixed],
                 sem=("parallel", "arbitrary"))


def _bucket_onehot():
    ql = np.arange(BLK)[:, None]
    j = np.arange(2 * BLK)[None, :]
    d = BLK + ql - j
    n = np.clip(d, 0, None)
    nf = np.maximum(n, 1).astype(np.float32)
    max_exact = NUM_BUCKETS // 2
    large = max_exact + (np.log(nf / max_exact) / np.log(BLK / max_exact) * (NUM_BUCKETS - max_exact)).astype(np.int32)
    large = np.minimum(large, NUM_BUCKETS - 1)
    bucket = np.where(n < max_exact, n, large).astype(np.int32)
    valid = (d >= 0) & (d < BLK)
    oh = (bucket[None] == np.arange(NUM_BUCKETS)[:, None, None]) & valid[None]
    return oh.reshape(NUM_BUCKETS, BLK * 2 * BLK)


def _three_bf16(v):
    hi = v.astype(BF)
    r1 = v - hi.astype(F32)
    mid = r1.astype(BF)
    lo = (r1 - mid.astype(F32)).astype(BF)
    return hi, mid, lo


def _bias_table(rel_bias):
    oh = jnp.asarray(_bucket_onehot(), BF)
    tn = 4096

    def body(rb_ref, oh_ref, o_ref):
        o = oh_ref[...]
        hi, mid, lo = _three_bf16(rb_ref[...])
        acc = lax.dot_general(hi, o, TN, preferred_element_type=F32)
        acc = acc + lax.dot_general(mid, o, TN, preferred_element_type=F32)
        acc = acc + lax.dot_general(lo, o, TN, preferred_element_type=F32)
        on_band = jnp.sum(o.astype(F32), axis=0, keepdims=True) > 0.5
        o_ref[...] = jnp.where(on_band, acc, NEG_INF)

    n = oh.shape[1]
    return _call("bias_table", body, (n // tn,),
                 [pl.BlockSpec((NUM_BUCKETS, N_HEADS), lambda i: (0, 0)), pl.BlockSpec((NUM_BUCKETS, tn), lambda i: (0, i))],
                 [pl.BlockSpec((N_HEADS, tn), lambda i: (0, i))], [_sds((N_HEADS, n), F32)], [rel_bias, oh], sem=("parallel",))[0]


def _rel_bias_grad(dl):
    oh = jnp.asarray(_bucket_onehot(), BF)
    n = oh.shape[1]
    tk = 4096

    def body(dl_ref, oh_ref, o_ref):
        o = oh_ref[...]
        hi, mid, lo = _three_bf16(dl_ref[...])
        acc = lax.dot_general(o, hi, NT, preferred_element_type=F32)
        acc = acc + lax.dot_general(o, mid, NT, preferred_element_type=F32)
        acc = acc + lax.dot_general(o, lo, NT, preferred_element_type=F32)
        _acc_rows(o_ref, pl.program_id(0) == 0, acc)

    return _call("rel_bias_grad", body, (n // tk,),
                 [pl.BlockSpec((N_HEADS, tk), lambda i: (0, i)), pl.BlockSpec((NUM_BUCKETS, tk), lambda i: (0, i))],
                 [pl.BlockSpec((NUM_BUCKETS, N_HEADS), lambda i: (0, 0))], [_sds((NUM_BUCKETS, N_HEADS), F32)], [dl, oh],
                 sem=("arbitrary",))[0]


def _lo_half(shape):
    return lax.broadcasted_iota(jnp.int32, shape, 1) < HEAD_DIM


def _half_sum(x, lo):
    s_lo = jnp.sum(jnp.where(lo, x, 0.0), axis=-1, keepdims=True)
    s_hi = jnp.sum(jnp.where(lo, 0.0, x), axis=-1, keepdims=True)
    return jnp.where(lo, s_lo, s_hi)


def _norm2(x, lo):
    r = lax.rsqrt(_half_sum(x * x, lo) * (1.0 / HEAD_DIM) + EPS)
    return x * r, r


def _norm2_bwd(dy, xhat, r, gain, lo):
    dxhat = dy * gain
    dx = r * (dxhat - xhat * (_half_sum(dxhat * xhat, lo) * (1.0 / HEAD_DIM)))
    return dx, dy * xhat


def _swap(x):
    return pltpu.roll(x, HEAD_DIM, axis=1)


def _attn_logits(n, kk, zq_ref, kn, qg, bias_ref, sink_ref, lo_k):
    lo_q = _lo_half((BLK, LANES))
    half_k = lo_k if kk == 0 else jnp.logical_not(lo_k)
    K = jnp.where(half_k, kn, 0.0).astype(BF)
    rows, qhats, qrs = [], [], []
    for jp in range(4):
        xq = zq_ref[:, jp * LANES:(jp + 1) * LANES]
        qhat, qr = _norm2(xq, lo_q)
        qn = qhat * qg * (HEAD_DIM ** -0.5)
        qs = _swap(qn)
        rows += [qn, qs] if kk == 0 else [qs, qn]
        qhats.append(qhat)
        qrs.append(qr)
    Q = jnp.concatenate(rows, axis=0).astype(BF)
    qk = lax.dot_general(Q, K, NT, preferred_element_type=F32)
    b = bias_ref[8 * kk:8 * kk + 8].reshape(8 * BLK, 2 * BLK)
    col = lax.broadcasted_iota(jnp.int32, qk.shape, 1)
    ok = (b > -1e29) & ((n > 0) | (col >= BLK))
    l = jnp.where(ok, qk + b, NEG_INF)
    sink = sink_ref[kk]
    m = jnp.maximum(jnp.max(l, axis=-1, keepdims=True), sink)
    e = jnp.exp(l - m)
    es = jnp.exp(sink - m)
    den = jnp.sum(e, axis=-1, keepdims=True) + es
    return Q, K, e / den, es / den, qhats, qrs


def _attn_specs(o_q, o_k):
    nq = o_q // 512
    nk = o_k // LANES
    prev = lambda n: (jnp.maximum(n - 1, 0), nk)
    prev_v = lambda n: (jnp.maximum(n - 1, 0), nk + 1)
    return [pl.BlockSpec((BLK, 512), lambda n: (n, nq)), pl.BlockSpec((BLK, 512), lambda n: (n, nq + 1)),
            pl.BlockSpec((BLK, LANES), prev), pl.BlockSpec((BLK, LANES), lambda n: (n, nk)),
            pl.BlockSpec((BLK, LANES), prev_v), pl.BlockSpec((BLK, LANES), lambda n: (n, nk + 1)),
            pl.BlockSpec((1, LANES), lambda n: (0, 0)), pl.BlockSpec((1, LANES), lambda n: (0, 0)),
            pl.BlockSpec((N_KV, 8 * BLK, 1), lambda n: (0, 0, 0)),
            pl.BlockSpec((N_HEADS, BLK, 2 * BLK), lambda n: (0, 0, 0))]


def _attn_fwd(z, o_q, o_k, qg2, kg2, sink_rows, bias, plan=None):
    S = z.shape[0]

    def body(zq0, zq1, zkp, zkc, zvp, zvc, qg_ref, kg_ref, sink_ref, bias_ref, o_ref):
        n = pl.program_id(0)
        lo_k = _lo_half((2 * BLK, LANES))
        lo_q = _lo_half((BLK, LANES))
        khat, _ = _norm2(jnp.concatenate([zkp[...], zkc[...]], axis=0), lo_k)
        kn = khat * kg_ref[...]
        vb = jnp.concatenate([zvp[...], zvc[...]], axis=0).astype(BF)
        for kk, zq in enumerate((zq0, zq1)):
            _, _, p, _, _, _ = _attn_logits(n, kk, zq, kn, qg_ref[...], bias_ref, sink_ref, lo_k)
            r = jnp.dot(p.astype(BF), vb, preferred_element_type=F32)
            for jp in range(4):
                ev = r[(2 * jp) * BLK:(2 * jp + 1) * BLK]
                od = r[(2 * jp + 1) * BLK:(2 * jp + 2) * BLK]
                pair = jnp.where(lo_q, ev, _swap(od)) if kk == 0 else jnp.where(lo_q, _swap(ev), od)
                c0 = (4 * kk + jp) * LANES
                o_ref[:, c0:c0 + LANES] = pair.astype(BF)

    return _call("attn_fwd", body, (S // BLK,), _attn_specs(o_q, o_k), [pl.BlockSpec((BLK, ATT_W), lambda n: (n, 0))],
                 [_sds((S, ATT_W), BF)], [z, z, z, z, z, z, qg2, kg2, sink_rows, bias], sem=("parallel",), plan=plan)[0]


def _attn_bwd(z, o_q, o_k, qg2, kg2, sink_rows, bias, dout, plan=None):
    S = z.shape[0]

    def body(zq0, zq1, zkp, zkc, zvp, zvc, qg_ref, kg_ref, sink_ref, bias_ref, do_ref,
             dq_ref, dkp_ref, dkc_ref, dvp_ref, dvc_ref, dl_ref, dsink_ref, dgain_ref):
        n = pl.program_id(0)
        lo_k = _lo_half((2 * BLK, LANES))
        lo_q = _lo_half((BLK, LANES))
        qg = qg_ref[...]
        kg = kg_ref[...]
        khat, kr = _norm2(jnp.concatenate([zkp[...], zkc[...]], axis=0), lo_k)
        kn = khat * kg
        vf = jnp.concatenate([zvp[...], zvc[...]], axis=0)

        @pl.when(n == 0)
        def _():
            dl_ref[...] = jnp.zeros_like(dl_ref)
            dsink_ref[...] = jnp.zeros_like(dsink_ref)
            dgain_ref[...] = jnp.zeros_like(dgain_ref)

        dkn = jnp.zeros((2 * BLK, LANES), F32)
        dvb = jnp.zeros((2 * BLK, LANES), F32)
        dqg = jnp.zeros((1, LANES), F32)
        for kk, zq in enumerate((zq0, zq1)):
            half_k = lo_k if kk == 0 else jnp.logical_not(lo_k)
            Q, K, p, ps, qhats, qrs = _attn_logits(n, kk, zq, kn, qg, bias_ref, sink_ref, lo_k)
            rows = []
            for jp in range(4):
                c0 = (4 * kk + jp) * LANES
                x = do_ref[:, c0:c0 + LANES]
                rows += [x, _swap(x)] if kk == 0 else [_swap(x), x]
            dO = jnp.concatenate(rows, axis=0).astype(BF)
            V = jnp.where(half_k, vf, 0.0).astype(BF)
            dP = lax.dot_general(dO, V, NT, preferred_element_type=F32)
            delta = jnp.sum(p * dP, axis=-1, keepdims=True)
            dS = p * (dP - delta)
            dsink_ref[kk] += -ps * delta
            dl_ref[8 * kk:8 * kk + 8] += dS.reshape(8, BLK, 2 * BLK)
            dSb = dS.astype(BF)
            dvb = dvb + jnp.where(half_k, lax.dot_general(p.astype(BF), dO, TN, preferred_element_type=F32), 0.0)
            dkn = dkn + jnp.where(half_k, lax.dot_general(dSb, Q, TN, preferred_element_type=F32), 0.0)
            dQ = jnp.dot(dSb, K, preferred_element_type=F32) * (HEAD_DIM ** -0.5)
            for jp in range(4):
                ev = dQ[(2 * jp) * BLK:(2 * jp + 1) * BLK]
                od = dQ[(2 * jp + 1) * BLK:(2 * jp + 2) * BLK]
                dy = (ev + _swap(od)) if kk == 0 else (_swap(ev) + od)
                dx, gq = _norm2_bwd(dy, qhats[jp], qrs[jp], qg, lo_q)
                dqg = dqg + jnp.sum(gq, axis=0, keepdims=True)
                c0 = (4 * kk + jp) * LANES
                dq_ref[:, c0:c0 + LANES] = dx.astype(BF)
        dk, gk = _norm2_bwd(dkn, khat, kr, kg, lo_k)
        dkp_ref[...] = dk[:BLK]
        dkc_ref[...] = dk[BLK:]
        dvp_ref[...] = dvb[:BLK]
        dvc_ref[...] = dvb[BLK:]
        dgain_ref[...] += jnp.concatenate([dqg, jnp.sum(gk, axis=0, keepdims=True), jnp.zeros((6, LANES), F32)], axis=0)

    blk = pl.BlockSpec((BLK, LANES), lambda n: (n, 0))
    wide = pl.BlockSpec((BLK, ATT_W), lambda n: (n, 0))
    return _call(
        "attn_bwd", body, (S // BLK,), _attn_specs(o_q, o_k) + [wide],
        [wide, blk, blk, blk, blk, pl.BlockSpec((N_HEADS, BLK, 2 * BLK), lambda n: (0, 0, 0)),
         pl.BlockSpec((N_KV, 8 * BLK, 1), lambda n: (0, 0, 0)), pl.BlockSpec((8, LANES), lambda n: (0, 0))],
        [_sds((S, ATT_W), BF), _sds((S, LANES), F32), _sds((S, LANES), F32), _sds((S, LANES), F32), _sds((S, LANES), F32),
         _sds((N_HEADS, BLK, 2 * BLK), F32), _sds((N_KV, 8 * BLK, 1), F32), _sds((8, LANES), F32)],
        [z, z, z, z, z, z, qg2, kg2, sink_rows, bias, dout], sem=("arbitrary",), plan=plan)


def _kv_combine(dkp, dkc, dvp, dvc):
    S = dkc.shape[0]
    last = S // BLK - 1

    def body(kp_ref, kc_ref, vp_ref, vc_ref, dk_ref, dv_ref):
        more = pl.program_id(0) < last
        dk_ref[...] = (kc_ref[...] + jnp.where(more, kp_ref[...], 0.0)).astype(BF)
        dv_ref[...] = (vc_ref[...] + jnp.where(more, vp_ref[...], 0.0)).astype(BF)

    cur = pl.BlockSpec((BLK, LANES), lambda n: (n, 0))
    nxt = pl.BlockSpec((BLK, LANES), lambda n: (jnp.minimum(n + 1, last), 0))
    return _call("kv_combine", body, (S // BLK,), [nxt, cur, nxt, cur], [cur, cur],
                 [_sds((S, LANES), BF), _sds((S, LANES), BF)], [dkp, dkc, dvp, dvc], sem=("parallel",))


def _merge_fwd(mixed, attn, wpu4, wau4, z, o_ga, plan=None):
    S, PW = mixed.shape
    _, _, CS = wpu4.shape
    D = 4 * CS
    tm = _pick(S, (1024,))
    tn = 256
    nsj = CS // tn
    na = o_ga // tn
    nb = (o_ga + D) // tn

    def body(m_ref, a_ref, wp_ref, wa_ref, ga_ref, gb_ref, mg_ref, yy_ref):
        yp = jnp.dot(m_ref[...], wp_ref[...], preferred_element_type=F32)
        ya = jnp.dot(a_ref[...], wa_ref[...], preferred_element_type=F32)
        mg_ref[...] = (jax.nn.sigmoid(ga_ref[...]) * yp + jax.nn.sigmoid(gb_ref[...]) * ya).astype(BF)
        yy_ref[0] = yp.astype(BF)
        yy_ref[1] = ya.astype(BF)

    return _call("merge_fwd", body, (S // tm, D // tn),
                 [pl.BlockSpec((tm, PW), lambda i, j: (i, 0)), pl.BlockSpec((tm, ATT_W), lambda i, j: (i, 0)),
                  pl.BlockSpec((None, PW, tn), lambda i, j: (j // nsj, 0, j % nsj)),
                  pl.BlockSpec((None, ATT_W, tn), lambda i, j: (j // nsj, 0, j % nsj)),
                  pl.BlockSpec((tm, tn), lambda i, j: (i, na + j)), pl.BlockSpec((tm, tn), lambda i, j: (i, nb + j))],
                 [pl.BlockSpec((tm, tn), lambda i, j: (i, j)), pl.BlockSpec((2, tm, tn), lambda i, j: (0, i, j))],
                 [_sds((S, D), BF), _sds((2, S, D), BF)], [mixed, attn, wpu4, wau4, z, z], sem=("parallel", "parallel"), plan=plan)


def _merge_bwd(do, wo, z, o_ga, yy, plan=None):
    S, D = do.shape
    tm = _pick(S, (1024,))
    tn = 256
    na = o_ga // tn
    nb = (o_ga + D) // tn

    def epi(p, e, o):
        sa = jax.nn.sigmoid(e[0][...])
        sb = jax.nn.sigmoid(e[1][...])
        yp = e[2][0].astype(F32)
        ya = e[2][1].astype(F32)
        o[0][0] = (p * yp * sa * (1.0 - sa)).astype(BF)
        o[0][1] = (p * ya * sb * (1.0 - sb)).astype(BF)
        o[1][0] = (p * sa).astype(BF)
        o[1][1] = (p * sb).astype(BF)

    pair = pl.BlockSpec((2, tm, tn), lambda i, j, k: (0, i, j))
    return _mm("merge_bwd", (S // tm, D // tn, 1), NT,
               do, pl.BlockSpec((tm, D), lambda i, j, k: (i, 0)),
               wo, pl.BlockSpec((tn, D), lambda i, j, k: (j, 0)),
               [z, z, yy], [pl.BlockSpec((tm, tn), lambda i, j, k: (i, na + j)), pl.BlockSpec((tm, tn), lambda i, j, k: (i, nb + j)), pair],
               [_sds((2, S, D), BF), _sds((2, S, D), BF)], [pair, pair], None, epi, plan)


def _mm_up_t(name, dyy, which, w4):
    _, S, D = dyy.shape
    _, K, CS = w4.shape
    tm = _pick(S, (1024,))
    return _mm(name, (S // tm, 1, N_CHIPS), NT,
               dyy, pl.BlockSpec((None, tm, CS), lambda i, j, k: (which, i, k)),
               w4, pl.BlockSpec((None, K, CS), lambda i, j, k: (k, 0, 0)),
               [], [], [_sds((S, K), F32)], [pl.BlockSpec((tm, K), lambda i, j, k: (i, 0))], (tm, K), _store)[0]


def _mm_up_dw(name, a, dyy, which):
    _, S, D = dyy.shape
    K = a.shape[1]
    CS = D // N_CHIPS
    ts = _pick(S, (1024,))
    return _mm(name, (1, N_CHIPS, S // ts), TN,
               a, pl.BlockSpec((ts, K), lambda i, j, k: (k, 0)),
               dyy, pl.BlockSpec((None, ts, CS), lambda i, j, k: (which, k, j)),
               [], [], [_sds((N_CHIPS, K, CS), BF)], [pl.BlockSpec((None, K, CS), lambda i, j, k: (j, 0, 0))], (K, CS), _store)[0]


def _adamw(w, g, m, v):
    m = ADAM_B1 * m + (1.0 - ADAM_B1) * g
    v = ADAM_B2 * v + (1.0 - ADAM_B2) * (g * g)
    m_hat = m / (1.0 - ADAM_B1 ** ADAM_STEP)
    v_hat = v / (1.0 - ADAM_B2 ** ADAM_STEP)
    delta = -ADAM_LR * (m_hat / (jnp.sqrt(v_hat) + ADAM_EPS) + ADAM_WD * w)
    return delta, m, v


def _mod_fwd(c_all, w_ada, b_sh):
    D, cols = w_ada.shape
    tn = cols // 9

    def body(c_ref, w_ref, b_ref, o_ref):
        cv = c_ref[...]
        sc = (cv * jax.nn.sigmoid(cv)).astype(BF)
        o_ref[...] = jnp.dot(sc, w_ref[...].astype(BF), preferred_element_type=F32) + b_ref[...]

    return _call("mod_fwd", body, (9,),
                 [pl.BlockSpec((N_DEV, D), lambda j: (0, 0)), pl.BlockSpec((D, tn), lambda j: (0, j)), pl.BlockSpec((1, tn), lambda j: (0, j))],
                 [pl.BlockSpec((N_DEV, tn), lambda j: (0, j))], [_sds((N_DEV, cols), F32)], [c_all, w_ada, b_sh], sem=("parallel",))[0]


def _wada_bwd(c_all, dmod_sh, w, m, v, plan=None):
    D, cols = w.shape
    tn = cols // 18

    def body(c_ref, d_ref, w_ref, m_ref, v_ref, g_ref, dl_ref, nm_ref, nv_ref):
        cv = c_ref[...]
        sc = (cv * jax.nn.sigmoid(cv)).astype(BF)
        g = lax.dot_general(sc, d_ref[...].astype(BF), TN, preferred_element_type=F32)
        g_ref[...] = g
        dl_ref[...], nm_ref[...], nv_ref[...] = _adamw(w_ref[...], g, m_ref[...], v_ref[...])

    tile = pl.BlockSpec((D, tn), lambda j: (0, j))
    out = _sds((D, cols), F32)
    return _call("wada_bwd", body, (18,),
                 [pl.BlockSpec((N_DEV, D), lambda j: (0, 0)), pl.BlockSpec((N_DEV, tn), lambda j: (0, j)), tile, tile, tile],
                 [tile] * 4, [out] * 4, [c_all, dmod_sh, w, m, v], sem=("parallel",), plan=plan)


def _adam_2d(name, w, g, m, v):
    R, C = w.shape
    tr = _pick(R, (256, 128, 64, 8))

    def body(w_ref, g_ref, m_ref, v_ref, dl_ref, nm_ref, nv_ref):
        dl_ref[...], nm_ref[...], nv_ref[...] = _adamw(w_ref[...], g_ref[...], m_ref[...], v_ref[...])

    tile = pl.BlockSpec((tr, C), lambda i: (i, 0))
    out = _sds((R, C), F32)
    return _call(name, body, (R // tr,), [tile] * 4, [tile] * 3, [out] * 3, [w, g, m, v], sem=("parallel",))


def _small_finish(parts, w, m, v):
    _, R, C = parts.shape

    def body(p_ref, w_ref, m_ref, v_ref, g_ref, dl_ref, nm_ref, nv_ref):
        g = p_ref[0]
        for d in range(1, N_DEV):
            g = g + p_ref[d]
        g_ref[...] = g
        dl_ref[...], nm_ref[...], nv_ref[...] = _adamw(w_ref[...], g, m_ref[...], v_ref[...])

    out = _sds((R, C), F32)
    return pl.pallas_call(body, out_shape=[out] * 4, name="small_finish",
                          compiler_params=pltpu.CompilerParams(vmem_limit_bytes=VMEM_LIMIT))(parts, w, m, v)


def _cast_into_slot(name, w, rc):
    R, C = w.shape
    tr = _pick(R, (256, 128, 64, 16))

    def body(rc_ref, w_ref, o_ref):
        o_ref[...] = w_ref[...].astype(BF)

    grid_spec = pltpu.PrefetchScalarGridSpec(
        num_scalar_prefetch=1, grid=(R // tr,), in_specs=[pl.BlockSpec((tr, C), lambda i, rc_ref: (i, 0))],
        out_specs=pl.BlockSpec((None, tr, C), lambda i, rc_ref: (rc_ref[0], i, 0)))
    return pl.pallas_call(body, grid_spec=grid_spec, out_shape=_sds((N_CHIPS, R, C), BF), name=name,
                          compiler_params=pltpu.CompilerParams(dimension_semantics=("parallel",), vmem_limit_bytes=VMEM_LIMIT))(rc, w)


def _add_pair(name, p, q, rc):
    _, H, C = q.shape
    tr = _pick(H, (512, 256, 128, 64, 16))
    nt = H // tr

    def body(rc_ref, p_ref, q_ref, o_ref):
        o_ref[...] = (p_ref[...].astype(F32) + q_ref[...].astype(F32)).astype(BF)

    tile = pl.BlockSpec((None, tr, C), lambda k, i, rc_ref: (k, i, 0))
    grid_spec = pltpu.PrefetchScalarGridSpec(
        num_scalar_prefetch=1, grid=(N_CHIPS, nt),
        in_specs=[pl.BlockSpec((None, tr, C), lambda k, i, rc_ref: (k, rc_ref[1] * nt + i, 0)), tile], out_specs=tile)
    return pl.pallas_call(body, grid_spec=grid_spec, out_shape=_sds(q.shape, BF), name=name,
                          compiler_params=pltpu.CompilerParams(dimension_semantics=("parallel", "parallel"),
                                                               vmem_limit_bytes=VMEM_LIMIT))(rc, p, q)


def _sum_chips(name, u, t, rc):
    _, H, C = u.shape
    tr = _pick(H, (256, 128, 64, 16))

    def body(rc_ref, u_ref, t_ref, o_ref):
        r = rc_ref[0]
        own = t_ref[...].astype(F32)
        pick = lambda k: jnp.where(r == k, own, u_ref[k].astype(F32))
        o_ref[...] = ((pick(0) + pick(1)) + pick(2)) + pick(3)

    grid_spec = pltpu.PrefetchScalarGridSpec(
        num_scalar_prefetch=1, grid=(H // tr,),
        in_specs=[pl.BlockSpec((N_CHIPS, tr, C), lambda i, rc_ref: (0, i, 0)),
                  pl.BlockSpec((None, tr, C), lambda i, rc_ref: (rc_ref[0], i, 0))],
        out_specs=pl.BlockSpec((None, tr, C), lambda i, rc_ref: (rc_ref[1], i, 0)))
    return pl.pallas_call(body, grid_spec=grid_spec, out_shape=_sds((2, H, C), F32), name=name,
                          compiler_params=pltpu.CompilerParams(dimension_semantics=("parallel",), vmem_limit_bytes=VMEM_LIMIT))(rc, u, t)


BIG = ("gu1", "down1", "w_in", "pool_mix", "pool_up", "attn_up", "o", "gu2", "down2")
MIX = ("o", "pool_up", "attn_up", "pool_mix")
EARLY = ("down1", "w_in", "pool_mix", "pool_up", "attn_up", "o")

SCHEDULE = {
    "+gather_gu1_ici": ([("ici", ("gu1",))], []),
    "+gather_gu1_d2d": ([("d2d", ("gu1",))], []),
    "ffn1_up": ([("ici", EARLY)], []),
    "+gather_early_d2d": ([("d2d", EARLY)], []),
    "ffn1_down": ([("ici", ("gu2",))], []),
    "mix_in": ([("ici", ("down2",)), ("d2d", ("gu2",))], []),
    "mix_out": ([("d2d", ("down2",))], []),
    "ffn2_dwd": ([("split", ("gu2",))], [("add", ("gu2",))]),
    "ffn2_dh": ([("owners", ("gu2",)), ("split", ("down2",))], [("add", ("down2",)), ("sum", ("gu2",))]),
    "merge_bwd": ([("owners", ("down2",)), ("join", ("gu2",))], [("sum", ("down2",)), ("adam", ("gu2",))]),
    "mix_dwo": ([("join", ("down2",))], [("adam", ("down2",))]),
    "attn_bwd": ([("split", MIX)], [("add", MIX)]),
    "mix_dwin": ([("owners", MIX)], [("sum", MIX)]),
    "mix_dh": ([("split", ("w_in",)), ("join", MIX)], [("add", ("w_in",)), ("adam", MIX)]),
    "ffn1_dact": ([("owners", ("w_in",))], [("sum", ("w_in",))]),
    "ffn1_dwgu": ([("join", ("w_in",))], [("adam", ("w_in",))]),
    "ffn1_dwd": ([("split", ("gu1",))], [("add", ("gu1",))]),
    "ffn1_dh": ([("owners", ("gu1",)), ("split", ("down1",))], [("add", ("down1",)), ("sum", ("gu1",))]),
    "rms_mod_bwd1": ([("owners", ("down1",)), ("join", ("gu1",))], [("sum", ("down1",)), ("adam", ("gu1",))]),
    "wada_bwd": ([("join", ("down1",))], [("adam", ("down1",))]),
}


class _Plan:
    def __init__(self, rc, w2, m2, v2, full, D, gw):
        self.rc, self.w2, self.m2, self.v2, self.full, self.D, self.gw = rc, w2, m2, v2, dict(full), D, gw
        self.part, self.got, self.sums, self.landed, self.g = {}, {}, {}, {}, {}
        self.result = {}
        self.pending = {}

    def _make(self, op, names):
        if op == "ici":
            return _gather_ici_stage([self.full[k] for k in names])
        if op == "d2d":
            return _gather_d2d_stage([self.full[k] for k in names])
        if op == "split":
            return _split_stage([self.part[k] for k in names])
        if op == "owners":
            return _owners_stage([self.sums[k] for k in names])
        return _join_stage([self.g[k] for k in names])

    def stages(self, name):
        ops = SCHEDULE.get(name, ([], []))[0]
        return [self._make(op, names) for op, names in ops]

    def done(self, name, outs):
        ops, local = SCHEDULE[name]
        for (op, names), res in zip(ops, outs):
            store = {"ici": self.full, "d2d": self.full, "split": self.got, "owners": self.landed, "join": self.g}[op]
            store.update(zip(names, res))
        for op, names in local:
            for k in names:
                if op == "add":
                    self.sums[k] = _add_pair("add_pair_" + k, self.part[k], self.got[k], self.rc)
                elif op == "sum":
                    self.g[k] = _sum_chips("sum_chips_" + k, self.landed[k], self.sums[k], self.rc)
                else:
                    g2 = self.g[k].reshape(self.w2[k].shape)
                    self.result[k] = (g2, *_adam_2d("adam_" + k, self.w2[k], g2, self.m2[k], self.v2[k]))

    def alone(self, name):
        self.done(name, _run_stages(name[1:], self.stages(name)))

    def weight(self, k):
        D, gw, f = self.D, self.gw, self.full[k]
        if k in ("down1", "down2", "o"):
            return f.reshape(-1, D)
        if k == "pool_mix":
            return f.reshape(N_CHIPS, 4, gw // N_CHIPS, gw).transpose(1, 0, 2, 3).reshape(4, gw, gw)
        if k == "w_in":
            return f.transpose(1, 0, 2).reshape(D, -1)
        return f

    def partial(self, k, p):
        D, gw = self.D, self.gw
        if k in ("down1", "down2", "o"):
            p = p.reshape(N_CHIPS, p.shape[0] // N_CHIPS, p.shape[1])
        elif k == "pool_mix":
            p = p.astype(BF).reshape(4, N_CHIPS, gw // N_CHIPS, gw).transpose(1, 0, 2, 3).reshape(N_CHIPS, gw, gw)
        elif k == "w_in":
            p = p.reshape(D, N_CHIPS, -1).transpose(1, 0, 2)
        self.part[k] = p


class _NoComm:
    def __init__(self, weights):
        self.w, self.part = weights, {}

    def stages(self, name):
        return []

    def alone(self, name):
        pass

    def weight(self, k):
        return self.w[k]

    def partial(self, k, p):
        self.part[k] = p


def _row(a, i):
    return a[i:i + 1]


def _local_step(x, target, mod, g_ffn1, g_mix, g_ffn2, pool_scale, q_gain, k_gain, sinks, rel_bias, plan):
    S, D = x.shape
    half = 0.5 * mod
    tile2 = lambda g: jnp.concatenate([g, g], axis=1)
    qg2, kg2 = tile2(q_gain), tile2(k_gain)
    sink_rows = jnp.broadcast_to(sinks.reshape(N_KV, 8, 1, 1), (N_KV, 8, BLK, 1)).reshape(N_KV, 8 * BLK, 1)
    bias = _bias_table(rel_bias).reshape(N_HEADS, BLK, 2 * BLK)

    plan.alone("+gather_gu1_ici")
    plan.alone("+gather_gu1_d2d")
    h1 = _rms_mod_fwd("rms_mod_fwd1", x, g_ffn1, _row(mod, 0), _row(mod, 1))
    gu1, act1 = _ffn_up("ffn1_up", h1, plan.weight("gu1"), plan)
    plan.alone("+gather_early_d2d")
    x1, f1 = _mm_residual("ffn1_down", act1, plan.weight("down1"), x, _row(half, 2), plan)
    h2 = _rms_mod_fwd("rms_mod_fwd2", x1, g_mix, _row(mod, 3), _row(mod, 4))
    w_in = plan.weight("w_in")
    IN_W = w_in.shape[1]
    PW = plan.weight("pool_up").shape[1]
    o_q, o_k = PW, PW + ATT_W
    o_ga = o_k + 2 * KV_W
    tnz = _pick(IN_W, (1280, 256))
    tmz = _pick(S, (1024,))
    z = _mm("mix_in", (S // tmz, IN_W // tnz, 1), NN, h2, pl.BlockSpec((tmz, D), lambda i, j, k: (i, 0)),
            w_in, pl.BlockSpec((D, tnz), lambda i, j, k: (0, j)), [], [], [_sds((S, IN_W), F32)],
            [pl.BlockSpec((tmz, tnz), lambda i, j, k: (i, j))], None, _store, plan)[0]
    pooled = _pool_fwd(z, PW)
    mixed = _pool_mix(pooled, plan.weight("pool_mix"), pool_scale)
    attn = _attn_fwd(z, o_q, o_k, qg2, kg2, sink_rows, bias)
    merged, yy = _merge_fwd(mixed, attn, plan.weight("pool_up"), plan.weight("attn_up"), z, o_ga)
    x2, fo = _mm_residual("mix_out", merged, plan.weight("o"), x1, _row(mod, 5), plan)
    h3 = _rms_mod_fwd("rms_mod_fwd3", x2, g_ffn2, _row(mod, 6), _row(mod, 7))
    gu2, act2 = _ffn_up("ffn2_up", h3, plan.weight("gu2"))
    x3, f2 = _mm_residual("ffn2_down", act2, plan.weight("down2"), x2, _row(half, 8))
    dx3, loss_acc = _loss_bwd(x3, target)

    df2, dgate8 = _gate_bwd("gate_bwd3", dx3, f2, _row(half, 8))
    dgu2 = _ffn_dact("ffn2_dact", df2, plan.weight("down2"), gu2)
    plan.partial("gu2", _ffn_dwgu("ffn2_dwgu", h3, dgu2))
    plan.partial("down2", _mm_tn("ffn2_dwd", act2, df2, (1408, 512), (1024,), plan))
    dh3 = _ffn_dh("ffn2_dh", dgu2, plan.weight("gu2"), plan)
    dx2, acc3 = _rms_mod_bwd("rms_mod_bwd3", dh3, x2, dx3, g_ffn2, _row(mod, 7))

    do, dgate5 = _gate_bwd("gate_bwd2", dx2, fo, _row(mod, 5))
    dgab, dyy = _merge_bwd(do, plan.weight("o"), z, o_ga, yy, plan)
    plan.partial("o", _mm_tn("mix_dwo", merged, do, (1024,), (1024,), plan))
    dmixed = _mm_up_t("pool_up_t", dyy, 0, plan.weight("pool_up"))
    dattn = _mm_up_t("attn_up_t", dyy, 1, plan.weight("attn_up"))
    plan.partial("pool_up", _mm_up_dw("pool_up_dw", mixed, dyy, 0))
    plan.partial("attn_up", _mm_up_dw("attn_up_dw", attn, dyy, 1))
    dpooled, dpm, dps = _pool_mix_bwd(pooled, plan.weight("pool_mix"), pool_scale, dmixed)
    plan.partial("pool_mix", dpm)
    du_pool = _pool_bwd(dpooled)
    dq, dkp, dkc, dvp, dvc, dl, dsink, dgain = _attn_bwd(z, o_q, o_k, qg2, kg2, sink_rows, bias, dattn, plan)
    dk, dv = _kv_combine(dkp, dkc, dvp, dvc)
    drb = _rel_bias_grad(dl.reshape(N_HEADS, BLK * 2 * BLK))
    dz = jnp.concatenate([du_pool, dq, dk, dv, dgab[0], dgab[1]], axis=1)
    plan.partial("w_in", _mm_tn("mix_dwin", h2, dz, (1024,), (1280, 256), plan))
    tkz = _pick(IN_W, (1280, 256))
    tnd = _pick(D, (1024,))
    dh2 = _mm("mix_dh", (S // tmz, D // tnd, IN_W // tkz), NT, dz, pl.BlockSpec((tmz, tkz), lambda i, j, k: (i, k)),
              w_in, pl.BlockSpec((tnd, tkz), lambda i, j, k: (j, k)), [], [], [_sds((S, D), F32)],
              [pl.BlockSpec((tmz, tnd), lambda i, j, k: (i, j))], (tmz, tnd), _store, plan)[0]
    dx1, acc2 = _rms_mod_bwd("rms_mod_bwd2", dh2, x1, dx2, g_mix, _row(mod, 4))

    df1, dgate2 = _gate_bwd("gate_bwd1", dx1, f1, _row(half, 2))
    dgu1 = _ffn_dact("ffn1_dact", df1, plan.weight("down1"), gu1, plan)
    plan.partial("gu1", _ffn_dwgu("ffn1_dwgu", h1, dgu1, plan))
    plan.partial("down1", _mm_tn("ffn1_dwd", act1, df1, (1408, 512), (1024,), plan))
    dh1 = _ffn_dh("ffn1_dh", dgu1, plan.weight("gu1"), plan)
    grad_x, acc1 = _rms_mod_bwd("rms_mod_bwd1", dh1, x, dx1, g_ffn1, _row(mod, 1), plan)

    dmod = jnp.concatenate([_row(acc1, 0), _row(acc1, 1), 0.5 * _row(dgate2, 0),
                            _row(acc2, 0), _row(acc2, 1), _row(dgate5, 0),
                            _row(acc3, 0), _row(acc3, 1), 0.5 * _row(dgate8, 0)], axis=0)
    fold = lambda r: r[:, :HEAD_DIM] + r[:, HEAD_DIM:]
    small = dict(
        dmod=dmod, g_ffn1=_row(acc1, 2), g_mix=_row(acc2, 2), g_ffn2=_row(acc3, 2), pool_scale=_row(dps, 0),
        q_gain=fold(_row(dgain, 0)), k_gain=fold(_row(dgain, 1)),
        sinks=jnp.sum(dsink.reshape(N_HEADS, BLK), axis=1).reshape(1, N_HEADS), rel_bias=drb,
        loss=(0.5 / D) * jnp.sum(_row(loss_acc, 0)).reshape(1, 1))
    return grad_x, small


SMALL_ORDER = ("dmod", "g_ffn1", "g_mix", "g_ffn2", "pool_scale", "q_gain", "k_gain", "sinks", "rel_bias", "loss")


def _pack_small(vals):
    flat = jnp.concatenate([vals[k].reshape(-1) for k in SMALL_ORDER])
    n = flat.shape[0]
    rows = -(-n // (8 * LANES)) * 8
    return jnp.pad(flat, (0, rows * LANES - n)).reshape(rows, LANES)


def _unpack_small(packed, like):
    flat = packed.reshape(-1)
    out, off = {}, 0
    for k in SMALL_ORDER:
        n = int(np.prod(like[k].shape))
        out[k] = flat[off:off + n].reshape(like[k].shape)
        off += n
    return out


def kernel(x, c, w_ada, b_ada, g_ffn1, w_ffn1_gu, w_ffn1_down, g_mix, w_in, pool_mix, pool_scale, w_pool_up, q_gain, k_gain, sinks, rel_bias, w_attn_up, w_o, g_ffn2, w_ffn2_gu, w_ffn2_down, loss_target, m_w_ada, m_b_ada, m_g_ffn1, m_w_ffn1_gu, m_w_ffn1_down, m_g_mix, m_w_in, m_pool_mix, m_pool_scale, m_w_pool_up, m_q_gain, m_k_gain, m_sinks, m_rel_bias, m_w_attn_up, m_w_o, m_g_ffn2, m_w_ffn2_gu, m_w_ffn2_down, v_w_ada, v_b_ada, v_g_ffn1, v_w_ffn1_gu, v_w_ffn1_down, v_g_mix, v_w_in, v_pool_mix, v_pool_scale, v_w_pool_up, v_q_gain, v_k_gain, v_sinks, v_rel_bias, v_w_attn_up, v_w_o, v_g_ffn2, v_w_ffn2_gu, v_w_ffn2_down):
    S, D = x.shape[1], x.shape[2]
    gw = pool_mix.shape[3]
    r = 2 * lax.axis_index("x") + lax.axis_index("y")
    rc = jnp.stack([r, lax.axis_index("c")]).astype(jnp.int32)

    two_d = lambda a: a.reshape(-1, a.shape[-1])
    w_sh = dict(gu1=w_ffn1_gu, down1=w_ffn1_down, w_in=w_in, pool_mix=pool_mix, pool_up=w_pool_up, attn_up=w_attn_up, o=w_o,
                gu2=w_ffn2_gu, down2=w_ffn2_down)
    m_sh = dict(gu1=m_w_ffn1_gu, down1=m_w_ffn1_down, w_in=m_w_in, pool_mix=m_pool_mix, pool_up=m_w_pool_up, attn_up=m_w_attn_up,
                o=m_w_o, gu2=m_w_ffn2_gu, down2=m_w_ffn2_down)
    v_sh = dict(gu1=v_w_ffn1_gu, down1=v_w_ffn1_down, w_in=v_w_in, pool_mix=v_pool_mix, pool_up=v_w_pool_up, attn_up=v_w_attn_up,
                o=v_w_o, gu2=v_w_ffn2_gu, down2=v_w_ffn2_down)
    w2 = {k: two_d(w_sh[k]) for k in BIG}
    full = {k: _cast_into_slot("cast_" + k, w2[k], rc) for k in BIG}
    plan = _Plan(rc, w2, {k: two_d(m_sh[k]) for k in BIG}, {k: two_d(v_sh[k]) for k in BIG}, full, D, gw)

    c_all = _gather_all("gather_c", jnp.broadcast_to(c, (8, D)))[:, 0, :]
    cols = w_ada.shape[2]
    b_sh = lax.dynamic_slice(b_ada, (0, r * cols), (1, cols))
    mod_cols = _mod_fwd(c_all, w_ada[0], b_sh)
    mod_all = _chip_exchange("mod_exchange", mod_cols)
    me = 4 * lax.axis_index("x") + 2 * lax.axis_index("y") + lax.axis_index("c")
    mod = lax.dynamic_slice(mod_all, (0, me, 0), (N_CHIPS, 1, cols)).reshape(9, D)

    grad_x, small = _local_step(x[0], loss_target[0], mod, g_ffn1, g_mix, g_ffn2, pool_scale, q_gain, k_gain,
                                sinks, rel_bias, plan)

    small_w = dict(dmod=b_ada, g_ffn1=g_ffn1, g_mix=g_mix, g_ffn2=g_ffn2, pool_scale=pool_scale, q_gain=q_gain, k_gain=k_gain,
                   sinks=sinks, rel_bias=rel_bias, loss=jnp.zeros((1, 1), F32))
    small_m = dict(dmod=m_b_ada, g_ffn1=m_g_ffn1, g_mix=m_g_mix, g_ffn2=m_g_ffn2, pool_scale=m_pool_scale, q_gain=m_q_gain,
                   k_gain=m_k_gain, sinks=m_sinks, rel_bias=m_rel_bias, loss=jnp.zeros((1, 1), F32))
    small_v = dict(dmod=v_b_ada, g_ffn1=v_g_ffn1, g_mix=v_g_mix, g_ffn2=v_g_ffn2, pool_scale=v_pool_scale, q_gain=v_q_gain,
                   k_gain=v_k_gain, sinks=v_sinks, rel_bias=v_rel_bias, loss=jnp.ones((1, 1), F32))
    small_all = _gather_all("gather_small", _pack_small(small))
    sg, sd, sm, sv = [_unpack_small(a, small_w) for a in
                      _small_finish(small_all, _pack_small(small_w), _pack_small(small_m), _pack_small(small_v))]
    loss = sg["loss"].reshape(())

    dmod_all = small_all.reshape(N_DEV, -1)[:, :9 * D]
    dmod_sh = lax.dynamic_slice(dmod_all, (0, r * cols), (N_DEV, cols))
    g_ada, d_ada, nm_ada, nv_ada = _wada_bwd(c_all, dmod_sh, w_ada[0], m_w_ada[0], v_w_ada[0], plan)

    big = [{k: plan.result[k][i].reshape(w_sh[k].shape) for k in BIG} for i in range(4)]

    def ordered(b, ada, sm_):
        return (ada[None], sm_["dmod"], sm_["g_ffn1"], b["gu1"], b["down1"], sm_["g_mix"], b["w_in"], b["pool_mix"],
                sm_["pool_scale"], b["pool_up"], sm_["q_gain"], sm_["k_gain"], sm_["sinks"], sm_["rel_bias"], b["attn_up"],
                b["o"], sm_["g_ffn2"], b["gu2"], b["down2"])

    return (loss, grad_x[None], *ordered(big[0], g_ada, sg), *ordered(big[1], d_ada, sd), *ordered(big[2], nm_ada, sm),
            *ordered(big[3], nv_ada, sv))
```

```python
import numpy as np
import jax
import jax.numpy as jnp
from jax import lax
from jax.experimental import pallas as pl
from jax.experimental.pallas import tpu as pltpu

BF = jnp.bfloat16
F32 = jnp.float32
MESH = pl.DeviceIdType.MESH

EPS = 1e-6
NEG_INF = -1e30
HEAD_DIM = 64
N_HEADS = 16
N_KV = 2
ATT_W = N_HEADS * HEAD_DIM
KV_W = N_KV * HEAD_DIM
BLK = 128
NUM_BUCKETS = 32
POOL_MAX_W = 16
N_CHIPS = 4
N_DEV = 8
LANES = 128
ADAM_LR, ADAM_B1, ADAM_B2, ADAM_EPS, ADAM_WD, ADAM_STEP = 0.001, 0.9, 0.999, 1e-08, 0.01, 10
VMEM_LIMIT = 52 * 1024 * 1024
ANY = pl.BlockSpec(memory_space=pl.ANY)


def _pick(dim, prefs):
    for p in prefs:
        if p <= dim and dim % p == 0:
            return p
    return dim


def _sds(shape, dtype):
    return jax.ShapeDtypeStruct(tuple(shape), dtype)


def _place():
    return lax.axis_index("x"), lax.axis_index("y"), lax.axis_index("c")


def _other_chips(x, y):
    return [(1 - x, y), (x, 1 - y), (1 - x, 1 - y)]


def _chip_of(chip):
    return 2 * chip[0] + chip[1]


def _half_rows(ref, lead, cc, h):
    return ref.at[lead, pl.ds(pl.multiple_of(cc * h, 16), h), :]


class _Stage:
    def __init__(self, bufs, outs, alias, n_sem, start, wait):
        self.bufs, self.outs, self.alias, self.n_sem, self.start, self.wait = bufs, outs, alias, n_sem, start, wait


def _stage_plumbing(stages, n_in0, n_out0):
    bufs, outs, aliases, spans, scratch = [], [], {}, [], []
    for st in stages:
        i0, o0 = len(bufs), len(outs)
        bufs += list(st.bufs)
        outs += list(st.outs)
        for a, b in st.alias.items():
            aliases[n_in0 + i0 + a] = n_out0 + o0 + b
        spans.append((i0, len(bufs), o0, len(outs)))
        scratch += [pltpu.SemaphoreType.DMA((st.n_sem,)), pltpu.SemaphoreType.DMA((st.n_sem,))]

    def run(which, in_refs, out_refs, sem_refs):
        for s, st in enumerate(stages):
            i0, i1, o0, o1 = spans[s]
            getattr(st, which)(in_refs[i0:i1], out_refs[o0:o1], sem_refs[2 * s], sem_refs[2 * s + 1])

    def split(flat):
        return [list(flat[o0:o1]) for (_, _, o0, o1) in spans]

    return bufs, outs, aliases, scratch, run, split


def _run_stages(name, stages):
    bufs, outs, aliases, scratch, run, split = _stage_plumbing(stages, 0, 0)
    ni, no = len(bufs), len(outs)

    def body(*refs):
        ins, os_, sems = refs[:ni], refs[ni:ni + no], refs[ni + no:]
        run("start", ins, os_, sems)
        run("wait", ins, os_, sems)

    res = pl.pallas_call(body, in_specs=[ANY] * ni, out_specs=[ANY] * no, out_shape=outs, input_output_aliases=aliases,
                         scratch_shapes=scratch, name=name)(*bufs)
    return split(res)


def _call(name, body, grid, in_specs, out_specs, out_shape, args, scratch=(), sem=None, plan=None):
    stages = plan.stages(name) if plan is not None else []
    n_in, n_out, n_scr = len(args), len(out_shape), len(scratch)
    if not stages:
        return pl.pallas_call(body, grid=grid, in_specs=list(in_specs), out_specs=list(out_specs), out_shape=list(out_shape),
                              scratch_shapes=list(scratch), name=name,
                              compiler_params=pltpu.CompilerParams(dimension_semantics=sem, vmem_limit_bytes=VMEM_LIMIT))(*args)
    bufs, s_outs, aliases, s_scratch, run, split = _stage_plumbing(stages, n_in, n_out)
    nb, nso = len(bufs), len(s_outs)

    def hosted(*refs):
        ins = refs[:n_in]
        s_ins = refs[n_in:n_in + nb]
        outs = refs[n_in + nb:n_in + nb + n_out]
        s_os = refs[n_in + nb + n_out:n_in + nb + n_out + nso]
        scr = refs[n_in + nb + n_out + nso:n_in + nb + n_out + nso + n_scr]
        sems = refs[n_in + nb + n_out + nso + n_scr:]
        first = pl.program_id(0) == 0
        last = pl.program_id(0) == grid[0] - 1
        for d in range(1, len(grid)):
            first = first & (pl.program_id(d) == 0)
            last = last & (pl.program_id(d) == grid[d] - 1)

        @pl.when(first)
        def _():
            run("start", s_ins, s_os, sems)

        body(*ins, *outs, *scr)

        @pl.when(last)
        def _():
            run("wait", s_ins, s_os, sems)

    res = pl.pallas_call(
        hosted, grid=grid, in_specs=list(in_specs) + [ANY] * nb, out_specs=list(out_specs) + [ANY] * nso,
        out_shape=list(out_shape) + s_outs, input_output_aliases=aliases, scratch_shapes=list(scratch) + s_scratch, name=name,
        compiler_params=pltpu.CompilerParams(dimension_semantics=("arbitrary",) * len(grid), vmem_limit_bytes=VMEM_LIMIT))(*args, *bufs)
    plan.done(name, split(res[n_out:]))
    return list(res[:n_out])


def _gather_ici_stage(fulls):
    n = len(fulls)

    def copy(i, j, slot, ins, outs, send, recv):
        x, y, c = _place()
        chip = _other_chips(x, y)[j]
        h = fulls[i].shape[1] // 2
        return pltpu.make_async_remote_copy(_half_rows(ins[i], 2 * x + y, c, h), _half_rows(outs[i], slot(x, y, chip), c, h),
                                            send.at[3 * i + j], recv.at[3 * i + j], device_id=(*chip, c), device_id_type=MESH)

    mine = lambda x, y, chip: 2 * x + y
    theirs = lambda x, y, chip: _chip_of(chip)

    def start(ins, outs, send, recv):
        for i in range(n):
            for j in range(3):
                copy(i, j, mine, ins, outs, send, recv).start()

    def wait(ins, outs, send, recv):
        for i in range(n):
            for j in range(3):
                copy(i, j, theirs, ins, outs, send, recv).wait_recv()
        for i in range(n):
            for j in range(3):
                copy(i, j, mine, ins, outs, send, recv).wait_send()

    return _Stage(fulls, [_sds(f.shape, f.dtype) for f in fulls], {i: i for i in range(n)}, 3 * n, start, wait)


def _gather_d2d_stage(fulls):
    n = len(fulls)

    def copy(i, j, cc, ins, outs, send, recv):
        x, y, c = _place()
        rj = _chip_of(_other_chips(x, y)[j])
        h = fulls[i].shape[1] // 2
        half = cc(c)
        return pltpu.make_async_remote_copy(_half_rows(ins[i], rj, half, h), _half_rows(outs[i], rj, half, h),
                                            send.at[3 * i + j], recv.at[3 * i + j], device_id=(x, y, 1 - c), device_id_type=MESH)

    mine = lambda c: c
    theirs = lambda c: 1 - c

    def start(ins, outs, send, recv):
        for i in range(n):
            for j in range(3):
                copy(i, j, mine, ins, outs, send, recv).start()

    def wait(ins, outs, send, recv):
        for i in range(n):
            for j in range(3):
                copy(i, j, theirs, ins, outs, send, recv).wait_recv()
        for i in range(n):
            for j in range(3):
                copy(i, j, mine, ins, outs, send, recv).wait_send()

    return _Stage(fulls, [_sds(f.shape, f.dtype) for f in fulls], {i: i for i in range(n)}, 3 * n, start, wait)


def _split_stage(parts):
    n = len(parts)

    def copy(i, ins, outs, send, recv):
        x, y, c = _place()
        h = parts[i].shape[1] // 2
        return pltpu.make_async_remote_copy(_half_rows(ins[i], slice(None), 1 - c, h), outs[i], send.at[i], recv.at[i],
                                            device_id=(x, y, 1 - c), device_id_type=MESH)

    def start(ins, outs, send, recv):
        for i in range(n):
            copy(i, ins, outs, send, recv).start()

    def wait(ins, outs, send, recv):
        for i in range(n):
            copy(i, ins, outs, send, recv).wait_recv()
        for i in range(n):
            copy(i, ins, outs, send, recv).wait_send()

    return _Stage(parts, [_sds((N_CHIPS, p.shape[1] // 2, p.shape[2]), p.dtype) for p in parts], {}, n, start, wait)


def _owners_stage(sums):
    n = len(sums)

    def copy(i, j, mine, ins, outs, send, recv):
        x, y, c = _place()
        chip = _other_chips(x, y)[j]
        slot = (2 * x + y) if mine else _chip_of(chip)
        return pltpu.make_async_remote_copy(ins[i].at[_chip_of(chip)], outs[i].at[slot], send.at[3 * i + j], recv.at[3 * i + j],
                                            device_id=(*chip, c), device_id_type=MESH)

    def start(ins, outs, send, recv):
        for i in range(n):
            for j in range(3):
                copy(i, j, True, ins, outs, send, recv).start()

    def wait(ins, outs, send, recv):
        for i in range(n):
            for j in range(3):
                copy(i, j, False, ins, outs, send, recv).wait_recv()
        for i in range(n):
            for j in range(3):
                copy(i, j, True, ins, outs, send, recv).wait_send()

    return _Stage(sums, [_sds(s.shape, s.dtype) for s in sums], {}, 3 * n, start, wait)


def _join_stage(gs):
    n = len(gs)

    def copy(i, mine, ins, outs, send, recv):
        x, y, c = _place()
        slot = c if mine else 1 - c
        return pltpu.make_async_remote_copy(ins[i].at[slot], outs[i].at[slot], send.at[i], recv.at[i],
                                            device_id=(x, y, 1 - c), device_id_type=MESH)

    def start(ins, outs, send, recv):
        for i in range(n):
            copy(i, True, ins, outs, send, recv).start()

    def wait(ins, outs, send, recv):
        for i in range(n):
            copy(i, False, ins, outs, send, recv).wait_recv()
        for i in range(n):
            copy(i, True, ins, outs, send, recv).wait_send()

    return _Stage(gs, [_sds(g.shape, g.dtype) for g in gs], {i: i for i in range(n)}, n, start, wait)


def _chip_exchange(name, arr):
    def body(src, dst, send, recv, loc):
        x, y, c = _place()
        r = 2 * x + y
        chips = _other_chips(x, y)

        def cp(j, slot):
            return pltpu.make_async_remote_copy(src, dst.at[slot], send.at[j], recv.at[j], device_id=(*chips[j], c), device_id_type=MESH)

        mine = pltpu.make_async_copy(src, dst.at[r], loc)
        mine.start()
        for j in range(3):
            cp(j, r).start()
        for j in range(3):
            cp(j, _chip_of(chips[j])).wait_recv()
        for j in range(3):
            cp(j, r).wait_send()
        mine.wait()

    return pl.pallas_call(body, in_specs=[ANY], out_specs=ANY, out_shape=_sds((N_CHIPS, *arr.shape), arr.dtype),
                          scratch_shapes=[pltpu.SemaphoreType.DMA((3,)), pltpu.SemaphoreType.DMA((3,)), pltpu.SemaphoreType.DMA],
                          name=name)(arr)


def _gather_all(name, arr):
    def body(src, dst, send, recv, loc):
        x, y, c = _place()

        def cp(k, slot_of_me):
            px, py, pc = x ^ ((k >> 2) & 1), y ^ ((k >> 1) & 1), c ^ (k & 1)
            slot = (4 * x + 2 * y + c) if slot_of_me else (4 * px + 2 * py + pc)
            return pltpu.make_async_remote_copy(src, dst.at[slot], send.at[k - 1], recv.at[k - 1],
                                                device_id=(px, py, pc), device_id_type=MESH)

        mine = pltpu.make_async_copy(src, dst.at[4 * x + 2 * y + c], loc)
        mine.start()
        for k in range(1, N_DEV):
            cp(k, True).start()
        for k in range(1, N_DEV):
            cp(k, False).wait_recv()
        for k in range(1, N_DEV):
            cp(k, True).wait_send()
        mine.wait()

    return pl.pallas_call(body, in_specs=[ANY], out_specs=ANY, out_shape=_sds((N_DEV, *arr.shape), arr.dtype),
                          scratch_shapes=[pltpu.SemaphoreType.DMA((N_DEV - 1,)), pltpu.SemaphoreType.DMA((N_DEV - 1,)), pltpu.SemaphoreType.DMA],
                          name=name)(arr)


NN = (((1,), (0,)), ((), ()))
NT = (((1,), (1,)), ((), ()))
TN = (((0,), (0,)), ((), ()))


def _mm(name, grid, dims, a, a_spec, b, b_spec, extras, extra_specs, out_shapes, out_specs, acc_shape, epilogue, plan=None):
    n_k = grid[2]
    n_e = len(extras)
    n_o = len(out_shapes)

    def body(*refs):
        a_ref, b_ref = refs[0], refs[1]
        e_refs = refs[2:2 + n_e]
        o_refs = refs[2 + n_e:2 + n_e + n_o]
        p = lax.dot_general(a_ref[...].astype(BF), b_ref[...].astype(BF), dims, preferred_element_type=F32)
        if n_k == 1:
            epilogue(p, e_refs, o_refs)
        else:
            acc = refs[-1]
            k = pl.program_id(2)

            @pl.when(k == 0)
            def _():
                acc[...] = p

            @pl.when(k > 0)
            def _():
                acc[...] += p

            @pl.when(k == n_k - 1)
            def _():
                epilogue(acc[...], e_refs, o_refs)

    scratch = [] if n_k == 1 else [pltpu.VMEM(acc_shape, F32)]
    return _call(name, body, grid, [a_spec, b_spec, *extra_specs], out_specs, out_shapes, [a, b, *extras], scratch,
                 ("parallel", "parallel", "arbitrary"), plan)


def _store(p, e, o):
    o[0][...] = p.astype(o[0].dtype)


def _rms_mod_fwd(name, x, gain, shift, scale):
    S, D = x.shape
    ts = _pick(S, (512,))

    def body(x_ref, g_ref, sh_ref, sc_ref, h_ref):
        xv = x_ref[...]
        r = lax.rsqrt(jnp.mean(xv * xv, axis=-1, keepdims=True) + EPS)
        n = xv * r * g_ref[...]
        h_ref[...] = (n * (1.0 + sc_ref[...]) + sh_ref[...]).astype(BF)

    row = pl.BlockSpec((ts, D), lambda i: (i, 0))
    vec = pl.BlockSpec((1, D), lambda i: (0, 0))
    return _call(name, body, (S // ts,), [row, vec, vec, vec], [row], [_sds((S, D), BF)], [x, gain, shift, scale],
                 sem=("parallel",))[0]


def _acc_rows(acc_ref, first, part):
    @pl.when(first)
    def _():
        acc_ref[...] = part

    @pl.when(jnp.logical_not(first))
    def _():
        acc_ref[...] += part


def _rms_mod_bwd(name, dh, x, dres, gain, scale, plan=None):
    S, D = x.shape
    ts = _pick(S, (256,))

    def body(dh_ref, x_ref, dr_ref, g_ref, sc_ref, dx_ref, acc_ref):
        xv = x_ref[...]
        dhv = dh_ref[...]
        g = g_ref[...]
        r = lax.rsqrt(jnp.mean(xv * xv, axis=-1, keepdims=True) + EPS)
        xhat = xv * r
        dn = dhv * (1.0 + sc_ref[...])
        dxhat = dn * g
        dx_ref[...] = dr_ref[...] + r * (dxhat - xhat * jnp.mean(dxhat * xhat, axis=-1, keepdims=True))
        part = jnp.concatenate([
            jnp.sum(dhv, axis=0, keepdims=True),
            jnp.sum(dhv * (xhat * g), axis=0, keepdims=True),
            jnp.sum(dn * xhat, axis=0, keepdims=True),
            jnp.zeros((5, D), F32)], axis=0)
        _acc_rows(acc_ref, pl.program_id(0) == 0, part)

    row = pl.BlockSpec((ts, D), lambda i: (i, 0))
    vec = pl.BlockSpec((1, D), lambda i: (0, 0))
    return _call(name, body, (S // ts,), [row, row, row, vec, vec], [row, pl.BlockSpec((8, D), lambda i: (0, 0))],
                 [_sds((S, D), F32), _sds((8, D), F32)], [dh, x, dres, gain, scale], sem=("arbitrary",), plan=plan)


def _gate_bwd(name, dx, f, coef):
    S, D = dx.shape
    ts = _pick(S, (512,))

    def body(dx_ref, f_ref, c_ref, df_ref, acc_ref):
        dxv = dx_ref[...]
        df_ref[...] = (dxv * c_ref[...]).astype(BF)
        part = jnp.concatenate([jnp.sum(dxv * f_ref[...].astype(F32), axis=0, keepdims=True), jnp.zeros((7, D), F32)], axis=0)
        _acc_rows(acc_ref, pl.program_id(0) == 0, part)

    row = pl.BlockSpec((ts, D), lambda i: (i, 0))
    return _call(name, body, (S // ts,), [row, row, pl.BlockSpec((1, D), lambda i: (0, 0))],
                 [row, pl.BlockSpec((8, D), lambda i: (0, 0))], [_sds((S, D), BF), _sds((8, D), F32)], [dx, f, coef],
                 sem=("arbitrary",))


def _loss_bwd(x3, target):
    S, D = x3.shape
    ts = _pick(S, (512,))

    def body(x_ref, t_ref, dx_ref, acc_ref):
        e = x_ref[...] - t_ref[...]
        dx_ref[...] = e * (1.0 / D)
        part = jnp.concatenate([jnp.sum(e * e, axis=0, keepdims=True), jnp.zeros((7, D), F32)], axis=0)
        _acc_rows(acc_ref, pl.program_id(0) == 0, part)

    row = pl.BlockSpec((ts, D), lambda i: (i, 0))
    return _call("loss_bwd", body, (S // ts,), [row, row], [row, pl.BlockSpec((8, D), lambda i: (0, 0))],
                 [_sds((S, D), F32), _sds((8, D), F32)], [x3, target], sem=("arbitrary",))


def _silu_parts(g):
    s = jax.nn.sigmoid(g)
    return s, g * s


def _ffn_up(name, h, wgu4, plan=None):
    S, D = h.shape
    SH = wgu4.shape[2]
    F = 2 * SH
    tm = _pick(S, (1024,))
    tn = _pick(SH, (256,))
    nts = SH // tn

    def body(h_ref, wg_ref, wu_ref, gu_ref, act_ref):
        hv = h_ref[...]
        g = jnp.dot(hv, wg_ref[...], preferred_element_type=F32)
        u = jnp.dot(hv, wu_ref[...], preferred_element_type=F32)
        gu_ref[0] = g.astype(BF)
        gu_ref[1] = u.astype(BF)
        act_ref[...] = (_silu_parts(g)[1] * u).astype(BF)

    return _call(name, body, (S // tm, F // tn),
                 [pl.BlockSpec((tm, D), lambda i, j: (i, 0)),
                  pl.BlockSpec((None, D, tn), lambda i, j: (j // nts, 0, j % nts)),
                  pl.BlockSpec((None, D, tn), lambda i, j: (2 + j // nts, 0, j % nts))],
                 [pl.BlockSpec((2, tm, tn), lambda i, j: (0, i, j)), pl.BlockSpec((tm, tn), lambda i, j: (i, j))],
                 [_sds((2, S, F), BF), _sds((S, F), BF)], [h, wgu4, wgu4], sem=("parallel", "parallel"), plan=plan)


def _mm_residual(name, a, w, x_in, coef, plan=None):
    S, K = a.shape
    D = w.shape[1]
    tm = _pick(S, (1024,))
    tn = _pick(D, (512,))
    tk = K

    def epi(p, e, o):
        o[0][...] = e[0][...] + e[1][...] * p
        o[1][...] = p.astype(BF)

    tile = pl.BlockSpec((tm, tn), lambda i, j, k: (i, j))
    return _mm(name, (S // tm, D // tn, K // tk), NN,
               a, pl.BlockSpec((tm, tk), lambda i, j, k: (i, k)),
               w, pl.BlockSpec((tk, tn), lambda i, j, k: (k, j)),
               [x_in, coef], [tile, pl.BlockSpec((1, tn), lambda i, j, k: (0, j))],
               [_sds((S, D), F32), _sds((S, D), BF)], [tile, tile], (tm, tn), epi, plan)


def _ffn_dact(name, df, wd, gu, plan=None):
    S, D = df.shape
    F = wd.shape[0]
    tm = _pick(S, (1024,))
    tn = _pick(F, (256,))

    def epi(p, e, o):
        g = e[0][0].astype(F32)
        u = e[0][1].astype(F32)
        s, sg = _silu_parts(g)
        o[0][0] = (p * u * (s * (1.0 + g * (1.0 - s)))).astype(BF)
        o[0][1] = (p * sg).astype(BF)

    pair = pl.BlockSpec((2, tm, tn), lambda i, j, k: (0, i, j))
    return _mm(name, (S // tm, F // tn, 1), NT,
               df, pl.BlockSpec((tm, D), lambda i, j, k: (i, 0)),
               wd, pl.BlockSpec((tn, D), lambda i, j, k: (j, 0)),
               [gu], [pair], [_sds((2, S, F), BF)], [pair], None, epi, plan)[0]


def _ffn_dh(name, dgu, wgu4, plan=None):
    _, S, F = dgu.shape
    _, D, SH = wgu4.shape
    tm = _pick(S, (1024,))
    tn = _pick(D, (1024,))
    tk = SH
    nkp = F // tk
    nks = SH // tk
    return _mm(name, (S // tm, D // tn, 2 * nkp), NT,
               dgu, pl.BlockSpec((None, tm, tk), lambda i, j, k: (k // nkp, i, k % nkp)),
               wgu4, pl.BlockSpec((None, tn, tk), lambda i, j, k: (k // nks, j, k % nks)),
               [], [], [_sds((S, D), F32)], [pl.BlockSpec((tm, tn), lambda i, j, k: (i, j))], (tm, tn), _store, plan)[0]


def _ffn_dwgu(name, h, dgu, plan=None):
    _, S, F = dgu.shape
    D = h.shape[1]
    SH = F // 2
    tk1 = _pick(D, (1024,))
    tn = _pick(SH, (1408, 256))
    ts = _pick(S, (2048, 1024))
    npj = F // tn
    nsj = SH // tn
    return _mm(name, (D // tk1, 2 * npj, S // ts), TN,
               h, pl.BlockSpec((ts, tk1), lambda i, j, k: (k, i)),
               dgu, pl.BlockSpec((None, ts, tn), lambda i, j, k: (j // npj, k, j % npj)),
               [], [], [_sds((4, D, SH), BF)],
               [pl.BlockSpec((None, tk1, tn), lambda i, j, k: (j // nsj, i, j % nsj))], (tk1, tn), _store, plan)[0]


def _mm_tn(name, a, b, tk1_prefs, tn_prefs, plan=None):
    S, K1 = a.shape
    N = b.shape[1]
    tk1 = _pick(K1, tk1_prefs)
    tn = _pick(N, tn_prefs)
    ts = _pick(S, (2048, 1024))
    return _mm(name, (K1 // tk1, N // tn, S // ts), TN,
               a, pl.BlockSpec((ts, tk1), lambda i, j, k: (k, i)),
               b, pl.BlockSpec((ts, tn), lambda i, j, k: (k, j)),
               [], [], [_sds((K1, N), BF)], [pl.BlockSpec((tk1, tn), lambda i, j, k: (i, j))], (tk1, tn), _store, plan)[0]


def _pool_window(ext, w, back):
    n = ext.shape[0]
    s = ext
    for step in (1, 2, 4, 8):
        sh = pltpu.roll(s, (n - step) if back else step, axis=0)
        s = jnp.where(w > step, s + sh, s)
    return s


def _pool_fwd(z, PW):
    S = z.shape[0]
    tc = _pick(S, (1024,))
    bpg = (PW // 4) // LANES
    H = POOL_MAX_W

    def body(prev_ref, u_ref, o_ref):
        i = pl.program_id(0)
        j = pl.program_id(1)
        w = lax.shift_left(jnp.int32(2), j // bpg)
        u = u_ref[...]
        prev = jnp.where(i > 0, prev_ref[...], 0.0)
        s = _pool_window(jnp.concatenate([prev, u], axis=0), w, False)[H:]
        t = i * tc + lax.broadcasted_iota(jnp.int32, (tc, LANES), 0)
        cnt = jnp.minimum(t + 1, w).astype(F32)
        o_ref[...] = (s / cnt - u).astype(BF)

    r = tc // H
    return _call("pool_fwd", body, (S // tc, PW // LANES),
                 [pl.BlockSpec((H, LANES), lambda i, j: (jnp.maximum(i * r - 1, 0), j)),
                  pl.BlockSpec((tc, LANES), lambda i, j: (i, j))],
                 [pl.BlockSpec((tc, LANES), lambda i, j: (i, j))], [_sds((S, PW), BF)], [z, z], sem=("parallel", "parallel"))[0]


def _pool_bwd(dpooled):
    S, PW = dpooled.shape
    tc = _pick(S, (1024,))
    bpg = (PW // 4) // LANES
    H = POOL_MAX_W
    last = S // tc - 1

    def body(dp_ref, nxt_ref, o_ref):
        i = pl.program_id(0)
        j = pl.program_id(1)
        w = lax.shift_left(jnp.int32(2), j // bpg)
        dp = dp_ref[...]
        nxt = jnp.where(i < last, nxt_ref[...], 0.0)
        ext = jnp.concatenate([dp, nxt], axis=0)
        t = i * tc + lax.broadcasted_iota(jnp.int32, (tc + H, LANES), 0)
        cnt = jnp.minimum(t + 1, w).astype(F32)
        s = _pool_window(ext / cnt, w, True)[:tc]
        o_ref[...] = (s - dp).astype(BF)

    r = tc // H
    nh = S // H - 1
    return _call("pool_bwd", body, (S // tc, PW // LANES),
                 [pl.BlockSpec((tc, LANES), lambda i, j: (i, j)),
                  pl.BlockSpec((H, LANES), lambda i, j: (jnp.minimum((i + 1) * r, nh), j))],
                 [pl.BlockSpec((tc, LANES), lambda i, j: (i, j))], [_sds((S, PW), BF)], [dpooled, dpooled],
                 sem=("parallel", "parallel"))[0]


def _pool_mix(pooled, pm, scale):
    S, PW = pooled.shape
    gw = PW // 4
    ts = _pick(S, (1024,))

    def epi(p, e, o):
        o[0][...] = (p * e[0][...]).astype(BF)

    tile = pl.BlockSpec((ts, gw), lambda i, j, k: (i, j))
    return _mm("pool_mix", (S // ts, 4, 1), NN, pooled, tile,
               pm, pl.BlockSpec((None, gw, gw), lambda i, j, k: (j, 0, 0)),
               [scale], [pl.BlockSpec((1, gw), lambda i, j, k: (0, j))], [_sds((S, PW), BF)], [tile], None, epi)[0]


def _pool_mix_bwd(pooled, pm, scale, dmixed):
    S, PW = pooled.shape
    gw = PW // 4
    ts = _pick(S, (1024,))

    def body(p_ref, pm_ref, sc_ref, dm_ref, dp_ref, dpm_ref, dsc_ref):
        i = pl.program_id(1)
        p = p_ref[...]
        w = pm_ref[...]
        dm = dm_ref[...]
        pre = jnp.dot(p, w, preferred_element_type=F32)
        dmp = (dm * sc_ref[...]).astype(BF)
        dp_ref[...] = lax.dot_general(dmp, w, NT, preferred_element_type=F32)
        dw = lax.dot_general(p, dmp, TN, preferred_element_type=F32)
        ds = jnp.concatenate([jnp.sum(dm * pre, axis=0, keepdims=True), jnp.zeros((7, gw), F32)], axis=0)
        _acc_rows(dpm_ref, i == 0, dw)
        _acc_rows(dsc_ref, i == 0, ds)

    tile = pl.BlockSpec((ts, gw), lambda g, i: (i, g))
    return _call("pool_mix_bwd", body, (4, S // ts),
                 [tile, pl.BlockSpec((None, gw, gw), lambda g, i: (g, 0, 0)), pl.BlockSpec((1, gw), lambda g, i: (0, g)), tile],
                 [tile, pl.BlockSpec((None, gw, gw), lambda g, i: (g, 0, 0)), pl.BlockSpec((8, gw), lambda g, i: (0, g))],
                 [_sds((S, PW), F32), _sds((4, gw, gw), F32), _sds((8, PW), F32)], [pooled, pm, scale, dmixed],
                 sem=("parallel", "arbitrary"))


def _bucket_onehot():
    ql = np.arange(BLK)[:, None]
    j = np.arange(2 * BLK)[None, :]
    d = BLK + ql - j
    n = np.clip(d, 0, None)
    nf = np.maximum(n, 1).astype(np.float32)
    max_exact = NUM_BUCKETS // 2
    large = max_exact + (np.log(nf / max_exact) / np.log(BLK / max_exact) * (NUM_BUCKETS - max_exact)).astype(np.int32)
    large = np.minimum(large, NUM_BUCKETS - 1)
    bucket = np.where(n < max_exact, n, large).astype(np.int32)
    valid = (d >= 0) & (d < BLK)
    oh = (bucket[None] == np.arange(NUM_BUCKETS)[:, None, None]) & valid[None]
    return oh.reshape(NUM_BUCKETS, BLK * 2 * BLK)


def _three_bf16(v):
    hi = v.astype(BF)
    r1 = v - hi.astype(F32)
    mid = r1.astype(BF)
    lo = (r1 - mid.astype(F32)).astype(BF)
    return hi, mid, lo


def _bias_table(rel_bias):
    oh = jnp.asarray(_bucket_onehot(), BF)
    tn = 4096

    def body(rb_ref, oh_ref, o_ref):
        o = oh_ref[...]
        hi, mid, lo = _three_bf16(rb_ref[...])
        acc = lax.dot_general(hi, o, TN, preferred_element_type=F32)
        acc = acc + lax.dot_general(mid, o, TN, preferred_element_type=F32)
        acc = acc + lax.dot_general(lo, o, TN, preferred_element_type=F32)
        on_band = jnp.sum(o.astype(F32), axis=0, keepdims=True) > 0.5
        o_ref[...] = jnp.where(on_band, acc, NEG_INF)

    n = oh.shape[1]
    return _call("bias_table", body, (n // tn,),
                 [pl.BlockSpec((NUM_BUCKETS, N_HEADS), lambda i: (0, 0)), pl.BlockSpec((NUM_BUCKETS, tn), lambda i: (0, i))],
                 [pl.BlockSpec((N_HEADS, tn), lambda i: (0, i))], [_sds((N_HEADS, n), F32)], [rel_bias, oh], sem=("parallel",))[0]


def _rel_bias_grad(dl):
    oh = jnp.asarray(_bucket_onehot(), BF)
    n = oh.shape[1]
    tk = 4096

    def body(dl_ref, oh_ref, o_ref):
        o = oh_ref[...]
        hi, mid, lo = _three_bf16(dl_ref[...])
        acc = lax.dot_general(o, hi, NT, preferred_element_type=F32)
        acc = acc + lax.dot_general(o, mid, NT, preferred_element_type=F32)
        acc = acc + lax.dot_general(o, lo, NT, preferred_element_type=F32)
        _acc_rows(o_ref, pl.program_id(0) == 0, acc)

    return _call("rel_bias_grad", body, (n // tk,),
                 [pl.BlockSpec((N_HEADS, tk), lambda i: (0, i)), pl.BlockSpec((NUM_BUCKETS, tk), lambda i: (0, i))],
                 [pl.BlockSpec((NUM_BUCKETS, N_HEADS), lambda i: (0, 0))], [_sds((NUM_BUCKETS, N_HEADS), F32)], [dl, oh],
                 sem=("arbitrary",))[0]


def _lo_half(shape):
    return lax.broadcasted_iota(jnp.int32, shape, 1) < HEAD_DIM


def _half_sum(x, lo):
    s_lo = jnp.sum(jnp.where(lo, x, 0.0), axis=-1, keepdims=True)
    s_hi = jnp.sum(jnp.where(lo, 0.0, x), axis=-1, keepdims=True)
    return jnp.where(lo, s_lo, s_hi)


def _norm2(x, lo):
    r = lax.rsqrt(_half_sum(x * x, lo) * (1.0 / HEAD_DIM) + EPS)
    return x * r, r


def _norm2_bwd(dy, xhat, r, gain, lo):
    dxhat = dy * gain
    dx = r * (dxhat - xhat * (_half_sum(dxhat * xhat, lo) * (1.0 / HEAD_DIM)))
    return dx, dy * xhat


def _swap(x):
    return pltpu.roll(x, HEAD_DIM, axis=1)


def _attn_logits(n, kk, zq_ref, kn, qg, bias_ref, sink_ref, lo_k):
    lo_q = _lo_half((BLK, LANES))
    half_k = lo_k if kk == 0 else jnp.logical_not(lo_k)
    K = jnp.where(half_k, kn, 0.0).astype(BF)
    rows, qhats, qrs = [], [], []
    for jp in range(4):
        xq = zq_ref[:, jp * LANES:(jp + 1) * LANES]
        qhat, qr = _norm2(xq, lo_q)
        qn = qhat * qg * (HEAD_DIM ** -0.5)
        qs = _swap(qn)
        rows += [qn, qs] if kk == 0 else [qs, qn]
        qhats.append(qhat)
        qrs.append(qr)
    Q = jnp.concatenate(rows, axis=0).astype(BF)
    qk = lax.dot_general(Q, K, NT, preferred_element_type=F32)
    b = bias_ref[8 * kk:8 * kk + 8].reshape(8 * BLK, 2 * BLK)
    col = lax.broadcasted_iota(jnp.int32, qk.shape, 1)
    ok = (b > -1e29) & ((n > 0) | (col >= BLK))
    l = jnp.where(ok, qk + b, NEG_INF)
    sink = sink_ref[kk]
    m = jnp.maximum(jnp.max(l, axis=-1, keepdims=True), sink)
    e = jnp.exp(l - m)
    es = jnp.exp(sink - m)
    den = jnp.sum(e, axis=-1, keepdims=True) + es
    return Q, K, e / den, es / den, qhats, qrs


def _attn_specs(o_q, o_k):
    nq = o_q // 512
    nk = o_k // LANES
    prev = lambda n: (jnp.maximum(n - 1, 0), nk)
    prev_v = lambda n: (jnp.maximum(n - 1, 0), nk + 1)
    return [pl.BlockSpec((BLK, 512), lambda n: (n, nq)), pl.BlockSpec((BLK, 512), lambda n: (n, nq + 1)),
            pl.BlockSpec((BLK, LANES), prev), pl.BlockSpec((BLK, LANES), lambda n: (n, nk)),
            pl.BlockSpec((BLK, LANES), prev_v), pl.BlockSpec((BLK, LANES), lambda n: (n, nk + 1)),
            pl.BlockSpec((1, LANES), lambda n: (0, 0)), pl.BlockSpec((1, LANES), lambda n: (0, 0)),
            pl.BlockSpec((N_KV, 8 * BLK, 1), lambda n: (0, 0, 0)),
            pl.BlockSpec((N_HEADS, BLK, 2 * BLK), lambda n: (0, 0, 0))]


def _attn_fwd(z, o_q, o_k, qg2, kg2, sink_rows, bias, plan=None):
    S = z.shape[0]

    def body(zq0, zq1, zkp, zkc, zvp, zvc, qg_ref, kg_ref, sink_ref, bias_ref, o_ref):
        n = pl.program_id(0)
        lo_k = _lo_half((2 * BLK, LANES))
        lo_q = _lo_half((BLK, LANES))
        khat, _ = _norm2(jnp.concatenate([zkp[...], zkc[...]], axis=0), lo_k)
        kn = khat * kg_ref[...]
        vb = jnp.concatenate([zvp[...], zvc[...]], axis=0).astype(BF)
        for kk, zq in enumerate((zq0, zq1)):
            _, _, p, _, _, _ = _attn_logits(n, kk, zq, kn, qg_ref[...], bias_ref, sink_ref, lo_k)
            r = jnp.dot(p.astype(BF), vb, preferred_element_type=F32)
            for jp in range(4):
                ev = r[(2 * jp) * BLK:(2 * jp + 1) * BLK]
                od = r[(2 * jp + 1) * BLK:(2 * jp + 2) * BLK]
                pair = jnp.where(lo_q, ev, _swap(od)) if kk == 0 else jnp.where(lo_q, _swap(ev), od)
                c0 = (4 * kk + jp) * LANES
                o_ref[:, c0:c0 + LANES] = pair.astype(BF)

    return _call("attn_fwd", body, (S // BLK,), _attn_specs(o_q, o_k), [pl.BlockSpec((BLK, ATT_W), lambda n: (n, 0))],
                 [_sds((S, ATT_W), BF)], [z, z, z, z, z, z, qg2, kg2, sink_rows, bias], sem=("parallel",), plan=plan)[0]


def _attn_bwd(z, o_q, o_k, qg2, kg2, sink_rows, bias, dout, plan=None):
    S = z.shape[0]

    def body(zq0, zq1, zkp, zkc, zvp, zvc, qg_ref, kg_ref, sink_ref, bias_ref, do_ref,
             dq_ref, dkp_ref, dkc_ref, dvp_ref, dvc_ref, dl_ref, dsink_ref, dgain_ref):
        n = pl.program_id(0)
        lo_k = _lo_half((2 * BLK, LANES))
        lo_q = _lo_half((BLK, LANES))
        qg = qg_ref[...]
        kg = kg_ref[...]
        khat, kr = _norm2(jnp.concatenate([zkp[...], zkc[...]], axis=0), lo_k)
        kn = khat * kg
        vf = jnp.concatenate([zvp[...], zvc[...]], axis=0)

        @pl.when(n == 0)
        def _():
            dl_ref[...] = jnp.zeros_like(dl_ref)
            dsink_ref[...] = jnp.zeros_like(dsink_ref)
            dgain_ref[...] = jnp.zeros_like(dgain_ref)

        dkn = jnp.zeros((2 * BLK, LANES), F32)
        dvb = jnp.zeros((2 * BLK, LANES), F32)
        dqg = jnp.zeros((1, LANES), F32)
        for kk, zq in enumerate((zq0, zq1)):
            half_k = lo_k if kk == 0 else jnp.logical_not(lo_k)
            Q, K, p, ps, qhats, qrs = _attn_logits(n, kk, zq, kn, qg, bias_ref, sink_ref, lo_k)
            rows = []
            for jp in range(4):
                c0 = (4 * kk + jp) * LANES
                x = do_ref[:, c0:c0 + LANES]
                rows += [x, _swap(x)] if kk == 0 else [_swap(x), x]
            dO = jnp.concatenate(rows, axis=0).astype(BF)
            V = jnp.where(half_k, vf, 0.0).astype(BF)
            dP = lax.dot_general(dO, V, NT, preferred_element_type=F32)
            delta = jnp.sum(p * dP, axis=-1, keepdims=True)
            dS = p * (dP - delta)
            dsink_ref[kk] += -ps * delta
            dl_ref[8 * kk:8 * kk + 8] += dS.reshape(8, BLK, 2 * BLK)
            dSb = dS.astype(BF)
            dvb = dvb + jnp.where(half_k, lax.dot_general(p.astype(BF), dO, TN, preferred_element_type=F32), 0.0)
            dkn = dkn + jnp.where(half_k, lax.dot_general(dSb, Q, TN, preferred_element_type=F32), 0.0)
            dQ = jnp.dot(dSb, K, preferred_element_type=F32) * (HEAD_DIM ** -0.5)
            for jp in range(4):
                ev = dQ[(2 * jp) * BLK:(2 * jp + 1) * BLK]
                od = dQ[(2 * jp + 1) * BLK:(2 * jp + 2) * BLK]
                dy = (ev + _swap(od)) if kk == 0 else (_swap(ev) + od)
                dx, gq = _norm2_bwd(dy, qhats[jp], qrs[jp], qg, lo_q)
                dqg = dqg + jnp.sum(gq, axis=0, keepdims=True)
                c0 = (4 * kk + jp) * LANES
                dq_ref[:, c0:c0 + LANES] = dx.astype(BF)
        dk, gk = _norm2_bwd(dkn, khat, kr, kg, lo_k)
        dkp_ref[...] = dk[:BLK]
        dkc_ref[...] = dk[BLK:]
        dvp_ref[...] = dvb[:BLK]
        dvc_ref[...] = dvb[BLK:]
        dgain_ref[...] += jnp.concatenate([dqg, jnp.sum(gk, axis=0, keepdims=True), jnp.zeros((6, LANES), F32)], axis=0)

    blk = pl.BlockSpec((BLK, LANES), lambda n: (n, 0))
    wide = pl.BlockSpec((BLK, ATT_W), lambda n: (n, 0))
    return _call(
        "attn_bwd", body, (S // BLK,), _attn_specs(o_q, o_k) + [wide],
        [wide, blk, blk, blk, blk, pl.BlockSpec((N_HEADS, BLK, 2 * BLK), lambda n: (0, 0, 0)),
         pl.BlockSpec((N_KV, 8 * BLK, 1), lambda n: (0, 0, 0)), pl.BlockSpec((8, LANES), lambda n: (0, 0))],
        [_sds((S, ATT_W), BF), _sds((S, LANES), F32), _sds((S, LANES), F32), _sds((S, LANES), F32), _sds((S, LANES), F32),
         _sds((N_HEADS, BLK, 2 * BLK), F32), _sds((N_KV, 8 * BLK, 1), F32), _sds((8, LANES), F32)],
        [z, z, z, z, z, z, qg2, kg2, sink_rows, bias, dout], sem=("arbitrary",), plan=plan)


def _kv_combine(dkp, dkc, dvp, dvc):
    S = dkc.shape[0]
    last = S // BLK - 1

    def body(kp_ref, kc_ref, vp_ref, vc_ref, dk_ref, dv_ref):
        more = pl.program_id(0) < last
        dk_ref[...] = (kc_ref[...] + jnp.where(more, kp_ref[...], 0.0)).astype(BF)
        dv_ref[...] = (vc_ref[...] + jnp.where(more, vp_ref[...], 0.0)).astype(BF)

    cur = pl.BlockSpec((BLK, LANES), lambda n: (n, 0))
    nxt = pl.BlockSpec((BLK, LANES), lambda n: (jnp.minimum(n + 1, last), 0))
    return _call("kv_combine", body, (S // BLK,), [nxt, cur, nxt, cur], [cur, cur],
                 [_sds((S, LANES), BF), _sds((S, LANES), BF)], [dkp, dkc, dvp, dvc], sem=("parallel",))


def _merge_fwd(mixed, attn, wpu4, wau4, z, o_ga, plan=None):
    S, PW = mixed.shape
    _, _, CS = wpu4.shape
    D = 4 * CS
    tm = _pick(S, (1024,))
    tn = 256
    nsj = CS // tn
    na = o_ga // tn
    nb = (o_ga + D) // tn

    def body(m_ref, a_ref, wp_ref, wa_ref, ga_ref, gb_ref, mg_ref, yy_ref):
        yp = jnp.dot(m_ref[...], wp_ref[...], preferred_element_type=F32)
        ya = jnp.dot(a_ref[...], wa_ref[...], preferred_element_type=F32)
        mg_ref[...] = (jax.nn.sigmoid(ga_ref[...]) * yp + jax.nn.sigmoid(gb_ref[...]) * ya).astype(BF)
        yy_ref[0] = yp.astype(BF)
        yy_ref[1] = ya.astype(BF)

    return _call("merge_fwd", body, (S // tm, D // tn),
                 [pl.BlockSpec((tm, PW), lambda i, j: (i, 0)), pl.BlockSpec((tm, ATT_W), lambda i, j: (i, 0)),
                  pl.BlockSpec((None, PW, tn), lambda i, j: (j // nsj, 0, j % nsj)),
                  pl.BlockSpec((None, ATT_W, tn), lambda i, j: (j // nsj, 0, j % nsj)),
                  pl.BlockSpec((tm, tn), lambda i, j: (i, na + j)), pl.BlockSpec((tm, tn), lambda i, j: (i, nb + j))],
                 [pl.BlockSpec((tm, tn), lambda i, j: (i, j)), pl.BlockSpec((2, tm, tn), lambda i, j: (0, i, j))],
                 [_sds((S, D), BF), _sds((2, S, D), BF)], [mixed, attn, wpu4, wau4, z, z], sem=("parallel", "parallel"), plan=plan)


def _merge_bwd(do, wo, z, o_ga, yy, plan=None):
    S, D = do.shape
    tm = _pick(S, (1024,))
    tn = 256
    na = o_ga // tn
    nb = (o_ga + D) // tn

    def epi(p, e, o):
        sa = jax.nn.sigmoid(e[0][...])
        sb = jax.nn.sigmoid(e[1][...])
        yp = e[2][0].astype(F32)
        ya = e[2][1].astype(F32)
        o[0][0] = (p * yp * sa * (1.0 - sa)).astype(BF)
        o[0][1] = (p * ya * sb * (1.0 - sb)).astype(BF)
        o[1][0] = (p * sa).astype(BF)
        o[1][1] = (p * sb).astype(BF)

    pair = pl.BlockSpec((2, tm, tn), lambda i, j, k: (0, i, j))
    return _mm("merge_bwd", (S // tm, D // tn, 1), NT,
               do, pl.BlockSpec((tm, D), lambda i, j, k: (i, 0)),
               wo, pl.BlockSpec((tn, D), lambda i, j, k: (j, 0)),
               [z, z, yy], [pl.BlockSpec((tm, tn), lambda i, j, k: (i, na + j)), pl.BlockSpec((tm, tn), lambda i, j, k: (i, nb + j)), pair],
               [_sds((2, S, D), BF), _sds((2, S, D), BF)], [pair, pair], None, epi, plan)


def _mm_up_t(name, dyy, which, w4):
    _, S, D = dyy.shape
    _, K, CS = w4.shape
    tm = _pick(S, (1024,))
    return _mm(name, (S // tm, 1, N_CHIPS), NT,
               dyy, pl.BlockSpec((None, tm, CS), lambda i, j, k: (which, i, k)),
               w4, pl.BlockSpec((None, K, CS), lambda i, j, k: (k, 0, 0)),
               [], [], [_sds((S, K), F32)], [pl.BlockSpec((tm, K), lambda i, j, k: (i, 0))], (tm, K), _store)[0]


def _mm_up_dw(name, a, dyy, which):
    _, S, D = dyy.shape
    K = a.shape[1]
    CS = D // N_CHIPS
    ts = _pick(S, (1024,))
    return _mm(name, (1, N_CHIPS, S // ts), TN,
               a, pl.BlockSpec((ts, K), lambda i, j, k: (k, 0)),
               dyy, pl.BlockSpec((None, ts, CS), lambda i, j, k: (which, k, j)),
               [], [], [_sds((N_CHIPS, K, CS), BF)], [pl.BlockSpec((None, K, CS), lambda i, j, k: (j, 0, 0))], (K, CS), _store)[0]


def _adamw(w, g, m, v):
    m = ADAM_B1 * m + (1.0 - ADAM_B1) * g
    v = ADAM_B2 * v + (1.0 - ADAM_B2) * (g * g)
    m_hat = m / (1.0 - ADAM_B1 ** ADAM_STEP)
    v_hat = v / (1.0 - ADAM_B2 ** ADAM_STEP)
    delta = -ADAM_LR * (m_hat / (jnp.sqrt(v_hat) + ADAM_EPS) + ADAM_WD * w)
    return delta, m, v


def _mod_fwd(c_all, w_ada, b_sh):
    D, cols = w_ada.shape
    tn = cols // 9

    def body(c_ref, w_ref, b_ref, o_ref):
        cv = c_ref[...]
        sc = (cv * jax.nn.sigmoid(cv)).astype(BF)
        o_ref[...] = jnp.dot(sc, w_ref[...].astype(BF), preferred_element_type=F32) + b_ref[...]

    return _call("mod_fwd", body, (9,),
                 [pl.BlockSpec((N_DEV, D), lambda j: (0, 0)), pl.BlockSpec((D, tn), lambda j: (0, j)), pl.BlockSpec((1, tn), lambda j: (0, j))],
                 [pl.BlockSpec((N_DEV, tn), lambda j: (0, j))], [_sds((N_DEV, cols), F32)], [c_all, w_ada, b_sh], sem=("parallel",))[0]


def _wada_bwd(c_all, dmod_sh, w, m, v, plan=None):
    D, cols = w.shape
    tn = cols // 18

    def body(c_ref, d_ref, w_ref, m_ref, v_ref, g_ref, dl_ref, nm_ref, nv_ref):
        cv = c_ref[...]
        sc = (cv * jax.nn.sigmoid(cv)).astype(BF)
        g = lax.dot_general(sc, d_ref[...].astype(BF), TN, preferred_element_type=F32)
        g_ref[...] = g
        dl_ref[...], nm_ref[...], nv_ref[...] = _adamw(w_ref[...], g, m_ref[...], v_ref[...])

    tile = pl.BlockSpec((D, tn), lambda j: (0, j))
    out = _sds((D, cols), F32)
    return _call("wada_bwd", body, (18,),
                 [pl.BlockSpec((N_DEV, D), lambda j: (0, 0)), pl.BlockSpec((N_DEV, tn), lambda j: (0, j)), tile, tile, tile],
                 [tile] * 4, [out] * 4, [c_all, dmod_sh, w, m, v], sem=("parallel",), plan=plan)


def _adam_2d(name, w, g, m, v):
    R, C = w.shape
    tr = _pick(R, (256, 128, 64, 8))

    def body(w_ref, g_ref, m_ref, v_ref, dl_ref, nm_ref, nv_ref):
        dl_ref[...], nm_ref[...], nv_ref[...] = _adamw(w_ref[...], g_ref[...], m_ref[...], v_ref[...])

    tile = pl.BlockSpec((tr, C), lambda i: (i, 0))
    out = _sds((R, C), F32)
    return _call(name, body, (R // tr,), [tile] * 4, [tile] * 3, [out] * 3, [w, g, m, v], sem=("parallel",))


def _small_finish(parts, w, m, v):
    _, R, C = parts.shape

    def body(p_ref, w_ref, m_ref, v_ref, g_ref, dl_ref, nm_ref, nv_ref):
        g = p_ref[0]
        for d in range(1, N_DEV):
            g = g + p_ref[d]
        g_ref[...] = g
        dl_ref[...], nm_ref[...], nv_ref[...] = _adamw(w_ref[...], g, m_ref[...], v_ref[...])

    out = _sds((R, C), F32)
    return pl.pallas_call(body, out_shape=[out] * 4, name="small_finish",
                          compiler_params=pltpu.CompilerParams(vmem_limit_bytes=VMEM_LIMIT))(parts, w, m, v)


def _my_chip():
    return 2 * lax.axis_index("x") + lax.axis_index("y")


def _cast_into_slot(name, w):
    R, C = w.shape
    tr = _pick(R, (256, 128, 64, 16))

    def body(w_ref, o_ref):
        o_ref[...] = w_ref[...].astype(BF)

    return _call(name, body, (R // tr,), [pl.BlockSpec((tr, C), lambda i: (i, 0))],
                 [pl.BlockSpec((None, tr, C), lambda i: (_my_chip(), i, 0))], [_sds((N_CHIPS, R, C), BF)], [w], sem=("parallel",))[0]


def _add_pair(name, p, q):
    _, H, C = q.shape
    tr = _pick(H, (512, 256, 128, 64, 16))
    nt = H // tr

    def body(p_ref, q_ref, o_ref):
        o_ref[...] = (p_ref[...].astype(F32) + q_ref[...].astype(F32)).astype(BF)

    tile = pl.BlockSpec((None, tr, C), lambda k, i: (k, i, 0))
    return _call(name, body, (N_CHIPS, nt), [pl.BlockSpec((None, tr, C), lambda k, i: (k, lax.axis_index("c") * nt + i, 0)), tile],
                 [tile], [_sds(q.shape, BF)], [p, q], sem=("parallel", "parallel"))[0]


def _sum_chips(name, u, t):
    _, H, C = u.shape
    tr = _pick(H, (256, 128, 64, 16))

    def body(u_ref, t_ref, o_ref):
        r = _my_chip()
        own = t_ref[...].astype(F32)
        pick = lambda k: jnp.where(r == k, own, u_ref[k].astype(F32))
        o_ref[...] = ((pick(0) + pick(1)) + pick(2)) + pick(3)

    return _call(name, body, (H // tr,),
                 [pl.BlockSpec((N_CHIPS, tr, C), lambda i: (0, i, 0)), pl.BlockSpec((None, tr, C), lambda i: (_my_chip(), i, 0))],
                 [pl.BlockSpec((None, tr, C), lambda i: (lax.axis_index("c"), i, 0))], [_sds((2, H, C), F32)], [u, t],
                 sem=("parallel",))[0]


BIG = ("gu1", "down1", "w_in", "pool_mix", "pool_up", "attn_up", "o", "gu2", "down2")
MIX = ("o", "pool_up", "attn_up", "pool_mix")
EARLY = ("down1", "w_in", "pool_mix", "pool_up", "attn_up", "o")

SCHEDULE = {
    "+gather_gu1_ici": ([("ici", ("gu1",))], []),
    "+gather_gu1_d2d": ([("d2d", ("gu1",))], []),
    "ffn1_up": ([("ici", EARLY)], []),
    "+gather_early_d2d": ([("d2d", EARLY)], []),
    "ffn1_down": ([("ici", ("gu2",))], []),
    "mix_in": ([("ici", ("down2",)), ("d2d", ("gu2",))], []),
    "mix_out": ([("d2d", ("down2",))], []),
    "ffn2_dwd": ([("split", ("gu2",))], [("add", ("gu2",))]),
    "ffn2_dh": ([("owners", ("gu2",)), ("split", ("down2",))], [("add", ("down2",)), ("sum", ("gu2",))]),
    "merge_bwd": ([("owners", ("down2",)), ("join", ("gu2",))], [("sum", ("down2",)), ("adam", ("gu2",))]),
    "mix_dwo": ([("join", ("down2",))], [("adam", ("down2",))]),
    "attn_bwd": ([("split", MIX)], [("add", MIX)]),
    "mix_dwin": ([("owners", MIX)], [("sum", MIX)]),
    "mix_dh": ([("split", ("w_in",)), ("join", MIX)], [("add", ("w_in",)), ("adam", MIX)]),
    "ffn1_dact": ([("owners", ("w_in",))], [("sum", ("w_in",))]),
    "ffn1_dwgu": ([("join", ("w_in",))], [("adam", ("w_in",))]),
    "ffn1_dwd": ([("split", ("gu1",))], [("add", ("gu1",))]),
    "ffn1_dh": ([("owners", ("gu1",)), ("split", ("down1",))], [("add", ("down1",)), ("sum", ("gu1",))]),
    "rms_mod_bwd1": ([("owners", ("down1",)), ("join", ("gu1",))], [("sum", ("down1",)), ("adam", ("gu1",))]),
    "wada_bwd": ([("join", ("down1",))], [("adam", ("down1",))]),
}


class _Plan:
    def __init__(self, w2, m2, v2, full, D, gw):
        self.w2, self.m2, self.v2, self.full, self.D, self.gw = w2, m2, v2, dict(full), D, gw
        self.part, self.got, self.sums, self.landed, self.g = {}, {}, {}, {}, {}
        self.result = {}
        self.pending = {}

    def _make(self, op, names):
        if op == "ici":
            return _gather_ici_stage([self.full[k] for k in names])
        if op == "d2d":
            return _gather_d2d_stage([self.full[k] for k in names])
        if op == "split":
            return _split_stage([self.part[k] for k in names])
        if op == "owners":
            return _owners_stage([self.sums[k] for k in names])
        return _join_stage([self.g[k] for k in names])

    def stages(self, name):
        ops = SCHEDULE.get(name, ([], []))[0]
        return [self._make(op, names) for op, names in ops]

    def done(self, name, outs):
        ops, local = SCHEDULE[name]
        for (op, names), res in zip(ops, outs):
            store = {"ici": self.full, "d2d": self.full, "split": self.got, "owners": self.landed, "join": self.g}[op]
            store.update(zip(names, res))
        for op, names in local:
            for k in names:
                if op == "add":
                    self.sums[k] = _add_pair("add_pair_" + k, self.part[k], self.got[k])
                elif op == "sum":
                    self.g[k] = _sum_chips("sum_chips_" + k, self.landed[k], self.sums[k])
                else:
                    g2 = self.g[k].reshape(self.w2[k].shape)
                    self.result[k] = (g2, *_adam_2d("adam_" + k, self.w2[k], g2, self.m2[k], self.v2[k]))

    def alone(self, name):
        self.done(name, _run_stages(name[1:], self.stages(name)))

    def weight(self, k):
        D, gw, f = self.D, self.gw, self.full[k]
        if k in ("down1", "down2", "o"):
            return f.reshape(-1, D)
        if k == "pool_mix":
            return f.reshape(N_CHIPS, 4, gw // N_CHIPS, gw).transpose(1, 0, 2, 3).reshape(4, gw, gw)
        if k == "w_in":
            return f.transpose(1, 0, 2).reshape(D, -1)
        return f

    def partial(self, k, p):
        D, gw = self.D, self.gw
        if k in ("down1", "down2", "o"):
            p = p.reshape(N_CHIPS, p.shape[0] // N_CHIPS, p.shape[1])
        elif k == "pool_mix":
            p = p.astype(BF).reshape(4, N_CHIPS, gw // N_CHIPS, gw).transpose(1, 0, 2, 3).reshape(N_CHIPS, gw, gw)
        elif k == "w_in":
            p = p.reshape(D, N_CHIPS, -1).transpose(1, 0, 2)
        self.part[k] = p


class _NoComm:
    def __init__(self, weights):
        self.w, self.part = weights, {}

    def stages(self, name):
        return []

    def alone(self, name):
        pass

    def weight(self, k):
        return self.w[k]

    def partial(self, k, p):
        self.part[k] = p


def _row(a, i):
    return a[i:i + 1]


def _local_step(x, target, mod, g_ffn1, g_mix, g_ffn2, pool_scale, q_gain, k_gain, sinks, rel_bias, plan):
    S, D = x.shape
    half = 0.5 * mod
    tile2 = lambda g: jnp.concatenate([g, g], axis=1)
    qg2, kg2 = tile2(q_gain), tile2(k_gain)
    sink_rows = jnp.broadcast_to(sinks.reshape(N_KV, 8, 1, 1), (N_KV, 8, BLK, 1)).reshape(N_KV, 8 * BLK, 1)
    bias = _bias_table(rel_bias).reshape(N_HEADS, BLK, 2 * BLK)

    plan.alone("+gather_gu1_ici")
    plan.alone("+gather_gu1_d2d")
    h1 = _rms_mod_fwd("rms_mod_fwd1", x, g_ffn1, _row(mod, 0), _row(mod, 1))
    gu1, act1 = _ffn_up("ffn1_up", h1, plan.weight("gu1"), plan)
    plan.alone("+gather_early_d2d")
    x1, f1 = _mm_residual("ffn1_down", act1, plan.weight("down1"), x, _row(half, 2), plan)
    h2 = _rms_mod_fwd("rms_mod_fwd2", x1, g_mix, _row(mod, 3), _row(mod, 4))
    w_in = plan.weight("w_in")
    IN_W = w_in.shape[1]
    PW = plan.weight("pool_up").shape[1]
    o_q, o_k = PW, PW + ATT_W
    o_ga = o_k + 2 * KV_W
    tnz = _pick(IN_W, (1280, 256))
    tmz = _pick(S, (1024,))
    z = _mm("mix_in", (S // tmz, IN_W // tnz, 1), NN, h2, pl.BlockSpec((tmz, D), lambda i, j, k: (i, 0)),
            w_in, pl.BlockSpec((D, tnz), lambda i, j, k: (0, j)), [], [], [_sds((S, IN_W), F32)],
            [pl.BlockSpec((tmz, tnz), lambda i, j, k: (i, j))], None, _store, plan)[0]
    pooled = _pool_fwd(z, PW)
    mixed = _pool_mix(pooled, plan.weight("pool_mix"), pool_scale)
    attn = _attn_fwd(z, o_q, o_k, qg2, kg2, sink_rows, bias)
    merged, yy = _merge_fwd(mixed, attn, plan.weight("pool_up"), plan.weight("attn_up"), z, o_ga)
    x2, fo = _mm_residual("mix_out", merged, plan.weight("o"), x1, _row(mod, 5), plan)
    h3 = _rms_mod_fwd("rms_mod_fwd3", x2, g_ffn2, _row(mod, 6), _row(mod, 7))
    gu2, act2 = _ffn_up("ffn2_up", h3, plan.weight("gu2"))
    x3, f2 = _mm_residual("ffn2_down", act2, plan.weight("down2"), x2, _row(half, 8))
    dx3, loss_acc = _loss_bwd(x3, target)

    df2, dgate8 = _gate_bwd("gate_bwd3", dx3, f2, _row(half, 8))
    dgu2 = _ffn_dact("ffn2_dact", df2, plan.weight("down2"), gu2)
    plan.partial("gu2", _ffn_dwgu("ffn2_dwgu", h3, dgu2))
    plan.partial("down2", _mm_tn("ffn2_dwd", act2, df2, (1408, 512), (1024,), plan))
    dh3 = _ffn_dh("ffn2_dh", dgu2, plan.weight("gu2"), plan)
    dx2, acc3 = _rms_mod_bwd("rms_mod_bwd3", dh3, x2, dx3, g_ffn2, _row(mod, 7))

    do, dgate5 = _gate_bwd("gate_bwd2", dx2, fo, _row(mod, 5))
    dgab, dyy = _merge_bwd(do, plan.weight("o"), z, o_ga, yy, plan)
    plan.partial("o", _mm_tn("mix_dwo", merged, do, (1024,), (1024,), plan))
    dmixed = _mm_up_t("pool_up_t", dyy, 0, plan.weight("pool_up"))
    dattn = _mm_up_t("attn_up_t", dyy, 1, plan.weight("attn_up"))
    plan.partial("pool_up", _mm_up_dw("pool_up_dw", mixed, dyy, 0))
    plan.partial("attn_up", _mm_up_dw("attn_up_dw", attn, dyy, 1))
    dpooled, dpm, dps = _pool_mix_bwd(pooled, plan.weight("pool_mix"), pool_scale, dmixed)
    plan.partial("pool_mix", dpm)
    du_pool = _pool_bwd(dpooled)
    dq, dkp, dkc, dvp, dvc, dl, dsink, dgain = _attn_bwd(z, o_q, o_k, qg2, kg2, sink_rows, bias, dattn, plan)
    dk, dv = _kv_combine(dkp, dkc, dvp, dvc)
    drb = _rel_bias_grad(dl.reshape(N_HEADS, BLK * 2 * BLK))
    dz = jnp.concatenate([du_pool, dq, dk, dv, dgab[0], dgab[1]], axis=1)
    plan.partial("w_in", _mm_tn("mix_dwin", h2, dz, (1024,), (1280, 256), plan))
    tkz = IN_W
    tnd = _pick(D, (512,))
    dh2 = _mm("mix_dh", (S // tmz, D // tnd, IN_W // tkz), NT, dz, pl.BlockSpec((tmz, tkz), lambda i, j, k: (i, k)),
              w_in, pl.BlockSpec((tnd, tkz), lambda i, j, k: (j, k)), [], [], [_sds((S, D), F32)],
              [pl.BlockSpec((tmz, tnd), lambda i, j, k: (i, j))], (tmz, tnd), _store, plan)[0]
    dx1, acc2 = _rms_mod_bwd("rms_mod_bwd2", dh2, x1, dx2, g_mix, _row(mod, 4))

    df1, dgate2 = _gate_bwd("gate_bwd1", dx1, f1, _row(half, 2))
    dgu1 = _ffn_dact("ffn1_dact", df1, plan.weight("down1"), gu1, plan)
    plan.partial("gu1", _ffn_dwgu("ffn1_dwgu", h1, dgu1, plan))
    plan.partial("down1", _mm_tn("ffn1_dwd", act1, df1, (1408, 512), (1024,), plan))
    dh1 = _ffn_dh("ffn1_dh", dgu1, plan.weight("gu1"), plan)
    grad_x, acc1 = _rms_mod_bwd("rms_mod_bwd1", dh1, x, dx1, g_ffn1, _row(mod, 1), plan)

    dmod = jnp.concatenate([_row(acc1, 0), _row(acc1, 1), 0.5 * _row(dgate2, 0),
                            _row(acc2, 0), _row(acc2, 1), _row(dgate5, 0),
                            _row(acc3, 0), _row(acc3, 1), 0.5 * _row(dgate8, 0)], axis=0)
    fold = lambda r: r[:, :HEAD_DIM] + r[:, HEAD_DIM:]
    small = dict(
        dmod=dmod, g_ffn1=_row(acc1, 2), g_mix=_row(acc2, 2), g_ffn2=_row(acc3, 2), pool_scale=_row(dps, 0),
        q_gain=fold(_row(dgain, 0)), k_gain=fold(_row(dgain, 1)),
        sinks=jnp.sum(dsink.reshape(N_HEADS, BLK), axis=1).reshape(1, N_HEADS), rel_bias=drb,
        loss=(0.5 / D) * jnp.sum(_row(loss_acc, 0)).reshape(1, 1))
    return grad_x, small


SMALL_ORDER = ("dmod", "g_ffn1", "g_mix", "g_ffn2", "pool_scale", "q_gain", "k_gain", "sinks", "rel_bias", "loss")


def _pack_small(vals):
    flat = jnp.concatenate([vals[k].reshape(-1) for k in SMALL_ORDER])
    n = flat.shape[0]
    rows = -(-n // (8 * LANES)) * 8
    return jnp.pad(flat, (0, rows * LANES - n)).reshape(rows, LANES)


def _unpack_small(packed, like):
    flat = packed.reshape(-1)
    out, off = {}, 0
    for k in SMALL_ORDER:
        n = int(np.prod(like[k].shape))
        out[k] = flat[off:off + n].reshape(like[k].shape)
        off += n
    return out


def kernel(x, c, w_ada, b_ada, g_ffn1, w_ffn1_gu, w_ffn1_down, g_mix, w_in, pool_mix, pool_scale, w_pool_up, q_gain, k_gain, sinks, rel_bias, w_attn_up, w_o, g_ffn2, w_ffn2_gu, w_ffn2_down, loss_target, m_w_ada, m_b_ada, m_g_ffn1, m_w_ffn1_gu, m_w_ffn1_down, m_g_mix, m_w_in, m_pool_mix, m_pool_scale, m_w_pool_up, m_q_gain, m_k_gain, m_sinks, m_rel_bias, m_w_attn_up, m_w_o, m_g_ffn2, m_w_ffn2_gu, m_w_ffn2_down, v_w_ada, v_b_ada, v_g_ffn1, v_w_ffn1_gu, v_w_ffn1_down, v_g_mix, v_w_in, v_pool_mix, v_pool_scale, v_w_pool_up, v_q_gain, v_k_gain, v_sinks, v_rel_bias, v_w_attn_up, v_w_o, v_g_ffn2, v_w_ffn2_gu, v_w_ffn2_down):
    S, D = x.shape[1], x.shape[2]
    gw = pool_mix.shape[3]
    r = 2 * lax.axis_index("x") + lax.axis_index("y")

    two_d = lambda a: a.reshape(-1, a.shape[-1])
    w_sh = dict(gu1=w_ffn1_gu, down1=w_ffn1_down, w_in=w_in, pool_mix=pool_mix, pool_up=w_pool_up, attn_up=w_attn_up, o=w_o,
                gu2=w_ffn2_gu, down2=w_ffn2_down)
    m_sh = dict(gu1=m_w_ffn1_gu, down1=m_w_ffn1_down, w_in=m_w_in, pool_mix=m_pool_mix, pool_up=m_w_pool_up, attn_up=m_w_attn_up,
                o=m_w_o, gu2=m_w_ffn2_gu, down2=m_w_ffn2_down)
    v_sh = dict(gu1=v_w_ffn1_gu, down1=v_w_ffn1_down, w_in=v_w_in, pool_mix=v_pool_mix, pool_up=v_w_pool_up, attn_up=v_w_attn_up,
                o=v_w_o, gu2=v_w_ffn2_gu, down2=v_w_ffn2_down)
    w2 = {k: two_d(w_sh[k]) for k in BIG}
    full = {k: _cast_into_slot("cast_" + k, w2[k]) for k in BIG}
    plan = _Plan(w2, {k: two_d(m_sh[k]) for k in BIG}, {k: two_d(v_sh[k]) for k in BIG}, full, D, gw)

    c_all = _gather_all("gather_c", jnp.broadcast_to(c, (8, D)))[:, 0, :]
    cols = w_ada.shape[2]
    b_sh = lax.dynamic_slice(b_ada, (0, r * cols), (1, cols))
    mod_cols = _mod_fwd(c_all, w_ada[0], b_sh)
    mod_all = _chip_exchange("mod_exchange", mod_cols)
    me = 4 * lax.axis_index("x") + 2 * lax.axis_index("y") + lax.axis_index("c")
    mod = lax.dynamic_slice(mod_all, (0, me, 0), (N_CHIPS, 1, cols)).reshape(9, D)

    grad_x, small = _local_step(x[0], loss_target[0], mod, g_ffn1, g_mix, g_ffn2, pool_scale, q_gain, k_gain,
                                sinks, rel_bias, plan)

    small_w = dict(dmod=b_ada, g_ffn1=g_ffn1, g_mix=g_mix, g_ffn2=g_ffn2, pool_scale=pool_scale, q_gain=q_gain, k_gain=k_gain,
                   sinks=sinks, rel_bias=rel_bias, loss=jnp.zeros((1, 1), F32))
    small_m = dict(dmod=m_b_ada, g_ffn1=m_g_ffn1, g_mix=m_g_mix, g_ffn2=m_g_ffn2, pool_scale=m_pool_scale, q_gain=m_q_gain,
                   k_gain=m_k_gain, sinks=m_sinks, rel_bias=m_rel_bias, loss=jnp.zeros((1, 1), F32))
    small_v = dict(dmod=v_b_ada, g_ffn1=v_g_ffn1, g_mix=v_g_mix, g_ffn2=v_g_ffn2, pool_scale=v_pool_scale, q_gain=v_q_gain,
                   k_gain=v_k_gain, sinks=v_sinks, rel_bias=v_rel_bias, loss=jnp.ones((1, 1), F32))
    small_all = _gather_all("gather_small", _pack_small(small))
    sg, sd, sm, sv = [_unpack_small(a, small_w) for a in
                      _small_finish(small_all, _pack_small(small_w), _pack_small(small_m), _pack_small(small_v))]
    loss = sg["loss"].reshape(())

    dmod_all = small_all.reshape(N_DEV, -1)[:, :9 * D]
    dmod_sh = lax.dynamic_slice(dmod_all, (0, r * cols), (N_DEV, cols))
    g_ada, d_ada, nm_ada, nv_ada = _wada_bwd(c_all, dmod_sh, w_ada[0], m_w_ada[0], v_w_ada[0], plan)

    big = [{k: plan.result[k][i].reshape(w_sh[k].shape) for k in BIG} for i in range(4)]

    def ordered(b, ada, sm_):
        return (ada[None], sm_["dmod"], sm_["g_ffn1"], b["gu1"], b["down1"], sm_["g_mix"], b["w_in"], b["pool_mix"],
                sm_["pool_scale"], b["pool_up"], sm_["q_gain"], sm_["k_gain"], sm_["sinks"], sm_["rel_bias"], b["attn_up"],
                b["o"], sm_["g_ffn2"], b["gu2"], b["down2"])

    return (loss, grad_x[None], *ordered(big[0], g_ada, sg), *ordered(big[1], d_ada, sd), *ordered(big[2], nm_ada, sm),
            *ordered(big[3], nv_ada, sv))
```

```python
import numpy as np
import jax
import jax.numpy as jnp
from jax import lax
from jax.experimental import pallas as pl
from jax.experimental.pallas import tpu as pltpu

BF = jnp.bfloat16
F32 = jnp.float32
MESH = pl.DeviceIdType.MESH

EPS = 1e-6
NEG_INF = -1e30
HEAD_DIM = 64
N_HEADS = 16
N_KV = 2
ATT_W = N_HEADS * HEAD_DIM
KV_W = N_KV * HEAD_DIM
BLK = 128
NUM_BUCKETS = 32
POOL_MAX_W = 16
N_CHIPS = 4
N_DEV = 8
LANES = 128
ADAM_LR, ADAM_B1, ADAM_B2, ADAM_EPS, ADAM_WD, ADAM_STEP = 0.001, 0.9, 0.999, 1e-08, 0.01, 10
VMEM_LIMIT = 52 * 1024 * 1024
ANY = pl.BlockSpec(memory_space=pl.ANY)


def _pick(dim, prefs):
    for p in prefs:
        if p <= dim and dim % p == 0:
            return p
    return dim


def _row_tile(rows, cap):
    return max(t for t in range(16, min(rows, cap) + 1, 16) if rows % t == 0)


def _sds(shape, dtype):
    return jax.ShapeDtypeStruct(tuple(shape), dtype)


def _place():
    return lax.axis_index("x"), lax.axis_index("y"), lax.axis_index("c")


def _other_chips(x, y):
    return [(1 - x, y), (x, 1 - y), (1 - x, 1 - y)]


def _chip_of(chip):
    return 2 * chip[0] + chip[1]


def _half_rows(ref, lead, cc, h):
    return ref.at[lead, pl.ds(pl.multiple_of(cc * h, 16), h), :]


class _Stage:
    def __init__(self, bufs, outs, alias, n_sem, start, wait):
        self.bufs, self.outs, self.alias, self.n_sem, self.start, self.wait = bufs, outs, alias, n_sem, start, wait


def _stage_plumbing(stages, n_in0, n_out0):
    bufs, outs, aliases, spans, scratch = [], [], {}, [], []
    for st in stages:
        i0, o0 = len(bufs), len(outs)
        bufs += list(st.bufs)
        outs += list(st.outs)
        for a, b in st.alias.items():
            aliases[n_in0 + i0 + a] = n_out0 + o0 + b
        spans.append((i0, len(bufs), o0, len(outs)))
        scratch += [pltpu.SemaphoreType.DMA((st.n_sem,)), pltpu.SemaphoreType.DMA((st.n_sem,))]

    def run(which, in_refs, out_refs, sem_refs):
        for s, st in enumerate(stages):
            i0, i1, o0, o1 = spans[s]
            getattr(st, which)(in_refs[i0:i1], out_refs[o0:o1], sem_refs[2 * s], sem_refs[2 * s + 1])

    def split(flat):
        return [list(flat[o0:o1]) for (_, _, o0, o1) in spans]

    return bufs, outs, aliases, scratch, run, split


def _run_stages(name, stages):
    bufs, outs, aliases, scratch, run, split = _stage_plumbing(stages, 0, 0)
    ni, no = len(bufs), len(outs)

    def body(*refs):
        ins, os_, sems = refs[:ni], refs[ni:ni + no], refs[ni + no:]
        run("start", ins, os_, sems)
        run("wait", ins, os_, sems)

    res = pl.pallas_call(body, in_specs=[ANY] * ni, out_specs=[ANY] * no, out_shape=outs, input_output_aliases=aliases,
                         scratch_shapes=scratch, name=name)(*bufs)
    return split(res)


def _call(name, body, grid, in_specs, out_specs, out_shape, args, scratch=(), sem=None, plan=None):
    stages = plan.stages(name) if plan is not None else []
    n_in, n_out, n_scr = len(args), len(out_shape), len(scratch)
    if not stages:
        return pl.pallas_call(body, grid=grid, in_specs=list(in_specs), out_specs=list(out_specs), out_shape=list(out_shape),
                              scratch_shapes=list(scratch), name=name,
                              compiler_params=pltpu.CompilerParams(dimension_semantics=sem, vmem_limit_bytes=VMEM_LIMIT))(*args)
    bufs, s_outs, aliases, s_scratch, run, split = _stage_plumbing(stages, n_in, n_out)
    nb, nso = len(bufs), len(s_outs)

    def hosted(*refs):
        ins = refs[:n_in]
        s_ins = refs[n_in:n_in + nb]
        outs = refs[n_in + nb:n_in + nb + n_out]
        s_os = refs[n_in + nb + n_out:n_in + nb + n_out + nso]
        scr = refs[n_in + nb + n_out + nso:n_in + nb + n_out + nso + n_scr]
        sems = refs[n_in + nb + n_out + nso + n_scr:]
        first = pl.program_id(0) == 0
        last = pl.program_id(0) == grid[0] - 1
        for d in range(1, len(grid)):
            first = first & (pl.program_id(d) == 0)
            last = last & (pl.program_id(d) == grid[d] - 1)

        @pl.when(first)
        def _():
            run("start", s_ins, s_os, sems)

        body(*ins, *outs, *scr)

        @pl.when(last)
        def _():
            run("wait", s_ins, s_os, sems)

    res = pl.pallas_call(
        hosted, grid=grid, in_specs=list(in_specs) + [ANY] * nb, out_specs=list(out_specs) + [ANY] * nso,
        out_shape=list(out_shape) + s_outs, input_output_aliases=aliases, scratch_shapes=list(scratch) + s_scratch, name=name,
        compiler_params=pltpu.CompilerParams(dimension_semantics=("arbitrary",) * len(grid), vmem_limit_bytes=VMEM_LIMIT))(*args, *bufs)
    plan.done(name, split(res[n_out:]))
    return list(res[:n_out])


def _gather_ici_stage(fulls):
    n = len(fulls)

    def copy(i, j, slot, ins, outs, send, recv):
        x, y, c = _place()
        chip = _other_chips(x, y)[j]
        h = fulls[i].shape[1] // 2
        return pltpu.make_async_remote_copy(_half_rows(ins[i], 2 * x + y, c, h), _half_rows(outs[i], slot(x, y, chip), c, h),
                                            send.at[3 * i + j], recv.at[3 * i + j], device_id=(*chip, c), device_id_type=MESH)

    mine = lambda x, y, chip: 2 * x + y
    theirs = lambda x, y, chip: _chip_of(chip)

    def start(ins, outs, send, recv):
        for i in range(n):
            for j in range(3):
                copy(i, j, mine, ins, outs, send, recv).start()

    def wait(ins, outs, send, recv):
        for i in range(n):
            for j in range(3):
                copy(i, j, theirs, ins, outs, send, recv).wait_recv()
        for i in range(n):
            for j in range(3):
                copy(i, j, mine, ins, outs, send, recv).wait_send()

    return _Stage(fulls, [_sds(f.shape, f.dtype) for f in fulls], {i: i for i in range(n)}, 3 * n, start, wait)


def _gather_d2d_stage(fulls):
    n = len(fulls)

    def copy(i, j, cc, ins, outs, send, recv):
        x, y, c = _place()
        rj = _chip_of(_other_chips(x, y)[j])
        h = fulls[i].shape[1] // 2
        half = cc(c)
        return pltpu.make_async_remote_copy(_half_rows(ins[i], rj, half, h), _half_rows(outs[i], rj, half, h),
                                            send.at[3 * i + j], recv.at[3 * i + j], device_id=(x, y, 1 - c), device_id_type=MESH)

    mine = lambda c: c
    theirs = lambda c: 1 - c

    def start(ins, outs, send, recv):
        for i in range(n):
            for j in range(3):
                copy(i, j, mine, ins, outs, send, recv).start()

    def wait(ins, outs, send, recv):
        for i in range(n):
            for j in range(3):
                copy(i, j, theirs, ins, outs, send, recv).wait_recv()
        for i in range(n):
            for j in range(3):
                copy(i, j, mine, ins, outs, send, recv).wait_send()

    return _Stage(fulls, [_sds(f.shape, f.dtype) for f in fulls], {i: i for i in range(n)}, 3 * n, start, wait)


def _split_stage(parts):
    n = len(parts)

    def copy(i, ins, outs, send, recv):
        x, y, c = _place()
        h = parts[i].shape[1] // 2
        return pltpu.make_async_remote_copy(_half_rows(ins[i], slice(None), 1 - c, h), outs[i], send.at[i], recv.at[i],
                                            device_id=(x, y, 1 - c), device_id_type=MESH)

    def start(ins, outs, send, recv):
        for i in range(n):
            copy(i, ins, outs, send, recv).start()

    def wait(ins, outs, send, recv):
        for i in range(n):
            copy(i, ins, outs, send, recv).wait_recv()
        for i in range(n):
            copy(i, ins, outs, send, recv).wait_send()

    return _Stage(parts, [_sds((N_CHIPS, p.shape[1] // 2, p.shape[2]), p.dtype) for p in parts], {}, n, start, wait)


def _owners_stage(sums):
    n = len(sums)

    def copy(i, j, mine, ins, outs, send, recv):
        x, y, c = _place()
        chip = _other_chips(x, y)[j]
        slot = (2 * x + y) if mine else _chip_of(chip)
        return pltpu.make_async_remote_copy(ins[i].at[_chip_of(chip)], outs[i].at[slot], send.at[3 * i + j], recv.at[3 * i + j],
                                            device_id=(*chip, c), device_id_type=MESH)

    def start(ins, outs, send, recv):
        for i in range(n):
            for j in range(3):
                copy(i, j, True, ins, outs, send, recv).start()

    def wait(ins, outs, send, recv):
        for i in range(n):
            for j in range(3):
                copy(i, j, False, ins, outs, send, recv).wait_recv()
        for i in range(n):
            for j in range(3):
                copy(i, j, True, ins, outs, send, recv).wait_send()

    return _Stage(sums, [_sds(s.shape, s.dtype) for s in sums], {}, 3 * n, start, wait)


def _join_stage(gs):
    n = len(gs)

    def copy(i, mine, ins, outs, send, recv):
        x, y, c = _place()
        slot = c if mine else 1 - c
        return pltpu.make_async_remote_copy(ins[i].at[slot], outs[i].at[slot], send.at[i], recv.at[i],
                                            device_id=(x, y, 1 - c), device_id_type=MESH)

    def start(ins, outs, send, recv):
        for i in range(n):
            copy(i, True, ins, outs, send, recv).start()

    def wait(ins, outs, send, recv):
        for i in range(n):
            copy(i, False, ins, outs, send, recv).wait_recv()
        for i in range(n):
            copy(i, True, ins, outs, send, recv).wait_send()

    return _Stage(gs, [_sds(g.shape, g.dtype) for g in gs], {i: i for i in range(n)}, n, start, wait)


def _chip_exchange(name, arr):
    def body(src, dst, send, recv, loc):
        x, y, c = _place()
        r = 2 * x + y
        chips = _other_chips(x, y)

        def cp(j, slot):
            return pltpu.make_async_remote_copy(src, dst.at[slot], send.at[j], recv.at[j], device_id=(*chips[j], c), device_id_type=MESH)

        mine = pltpu.make_async_copy(src, dst.at[r], loc)
        mine.start()
        for j in range(3):
            cp(j, r).start()
        for j in range(3):
            cp(j, _chip_of(chips[j])).wait_recv()
        for j in range(3):
            cp(j, r).wait_send()
        mine.wait()

    return pl.pallas_call(body, in_specs=[ANY], out_specs=ANY, out_shape=_sds((N_CHIPS, *arr.shape), arr.dtype),
                          scratch_shapes=[pltpu.SemaphoreType.DMA((3,)), pltpu.SemaphoreType.DMA((3,)), pltpu.SemaphoreType.DMA],
                          name=name)(arr)


def _gather_all(name, arr):
    def body(src, dst, send, recv, loc):
        x, y, c = _place()

        def cp(k, slot_of_me):
            px, py, pc = x ^ ((k >> 2) & 1), y ^ ((k >> 1) & 1), c ^ (k & 1)
            slot = (4 * x + 2 * y + c) if slot_of_me else (4 * px + 2 * py + pc)
            return pltpu.make_async_remote_copy(src, dst.at[slot], send.at[k - 1], recv.at[k - 1],
                                                device_id=(px, py, pc), device_id_type=MESH)

        mine = pltpu.make_async_copy(src, dst.at[4 * x + 2 * y + c], loc)
        mine.start()
        for k in range(1, N_DEV):
            cp(k, True).start()
        for k in range(1, N_DEV):
            cp(k, False).wait_recv()
        for k in range(1, N_DEV):
            cp(k, True).wait_send()
        mine.wait()

    return pl.pallas_call(body, in_specs=[ANY], out_specs=ANY, out_shape=_sds((N_DEV, *arr.shape), arr.dtype),
                          scratch_shapes=[pltpu.SemaphoreType.DMA((N_DEV - 1,)), pltpu.SemaphoreType.DMA((N_DEV - 1,)), pltpu.SemaphoreType.DMA],
                          name=name)(arr)


NN = (((1,), (0,)), ((), ()))
NT = (((1,), (1,)), ((), ()))
TN = (((0,), (0,)), ((), ()))


ALL = slice(None)


def _mm(name, grid, dims, a, a_spec, b, b_spec, extras, extra_specs, out_shapes, out_specs, acc_shape, epilogue, plan=None):
    n_k = grid[2]
    n_e = len(extras)
    n_o = len(out_shapes)

    def body(*refs):
        a_ref, b_ref = refs[0], refs[1]
        e_refs = refs[2:2 + n_e]
        o_refs = refs[2 + n_e:2 + n_e + n_o]
        p = lax.dot_general(a_ref[...].astype(BF), b_ref[...].astype(BF), dims, preferred_element_type=F32)
        if n_k == 1:
            epilogue(p, e_refs, o_refs, ALL)
        else:
            acc = refs[-1]
            k = pl.program_id(2)

            @pl.when(k == 0)
            def _():
                acc[...] = p

            @pl.when(k > 0)
            def _():
                acc[...] += p

            @pl.when(k == n_k - 1)
            def _():
                epilogue(acc[...], e_refs, o_refs, ALL)

    scratch = [] if n_k == 1 else [pltpu.VMEM(acc_shape, F32)]
    return _call(name, body, grid, [a_spec, b_spec, *extra_specs], out_specs, out_shapes, [a, b, *extras], scratch,
                 ("parallel", "parallel", "arbitrary"), plan)


def _store(p, e, o, rs):
    o[0][rs, :] = p.astype(o[0].dtype)


def _rms_mod_fwd(name, x, gain, shift, scale):
    S, D = x.shape
    ts = _pick(S, (512,))

    def body(x_ref, g_ref, sh_ref, sc_ref, h_ref):
        xv = x_ref[...]
        r = lax.rsqrt(jnp.mean(xv * xv, axis=-1, keepdims=True) + EPS)
        n = xv * r * g_ref[...]
        h_ref[...] = (n * (1.0 + sc_ref[...]) + sh_ref[...]).astype(BF)

    row = pl.BlockSpec((ts, D), lambda i: (i, 0))
    vec = pl.BlockSpec((1, D), lambda i: (0, 0))
    return _call(name, body, (S // ts,), [row, vec, vec, vec], [row], [_sds((S, D), BF)], [x, gain, shift, scale],
                 sem=("parallel",))[0]


def _acc_rows(acc_ref, first, part):
    @pl.when(first)
    def _():
        acc_ref[...] = part

    @pl.when(jnp.logical_not(first))
    def _():
        acc_ref[...] += part


def _rms_mod_bwd(name, dh, x, dres, gain, scale, f=None, coef=None, plan=None):
    S, D = x.shape
    ts = _pick(S, (256,))
    gated = f is not None

    def body(dh_ref, x_ref, dr_ref, g_ref, sc_ref, *rest):
        xv = x_ref[...]
        dhv = dh_ref[...]
        g = g_ref[...]
        r = lax.rsqrt(jnp.mean(xv * xv, axis=-1, keepdims=True) + EPS)
        xhat = xv * r
        dn = dhv * (1.0 + sc_ref[...])
        dxhat = dn * g
        dx = dr_ref[...] + r * (dxhat - xhat * jnp.mean(dxhat * xhat, axis=-1, keepdims=True))
        rows = [jnp.sum(dhv, axis=0, keepdims=True), jnp.sum(dhv * (xhat * g), axis=0, keepdims=True),
                jnp.sum(dn * xhat, axis=0, keepdims=True)]
        if gated:
            f_ref, c_ref, dx_ref, df_ref, acc_ref = rest
            df_ref[...] = (dx * c_ref[...]).astype(BF)
            rows.append(jnp.sum(dx * f_ref[...].astype(F32), axis=0, keepdims=True))
        else:
            dx_ref, acc_ref = rest
        dx_ref[...] = dx
        _acc_rows(acc_ref, pl.program_id(0) == 0, jnp.concatenate(rows + [jnp.zeros((8 - len(rows), D), F32)], axis=0))

    row = pl.BlockSpec((ts, D), lambda i: (i, 0))
    vec = pl.BlockSpec((1, D), lambda i: (0, 0))
    acc = pl.BlockSpec((8, D), lambda i: (0, 0))
    if gated:
        return _call(name, body, (S // ts,), [row, row, row, vec, vec, row, vec], [row, row, acc],
                     [_sds((S, D), F32), _sds((S, D), BF), _sds((8, D), F32)], [dh, x, dres, gain, scale, f, coef],
                     sem=("arbitrary",), plan=plan)
    return _call(name, body, (S // ts,), [row, row, row, vec, vec], [row, acc],
                 [_sds((S, D), F32), _sds((8, D), F32)], [dh, x, dres, gain, scale], sem=("arbitrary",), plan=plan)


def _loss_bwd(x3, target, f, coef):
    S, D = x3.shape
    ts = _pick(S, (512,))

    def body(x_ref, t_ref, f_ref, c_ref, dx_ref, df_ref, acc_ref):
        e = x_ref[...] - t_ref[...]
        dx = e * (1.0 / D)
        dx_ref[...] = dx
        df_ref[...] = (dx * c_ref[...]).astype(BF)
        part = jnp.concatenate([jnp.sum(e * e, axis=0, keepdims=True), jnp.sum(dx * f_ref[...].astype(F32), axis=0, keepdims=True),
                                jnp.zeros((6, D), F32)], axis=0)
        _acc_rows(acc_ref, pl.program_id(0) == 0, part)

    row = pl.BlockSpec((ts, D), lambda i: (i, 0))
    return _call("loss_bwd", body, (S // ts,), [row, row, row, pl.BlockSpec((1, D), lambda i: (0, 0))],
                 [row, row, pl.BlockSpec((8, D), lambda i: (0, 0))],
                 [_sds((S, D), F32), _sds((S, D), BF), _sds((8, D), F32)], [x3, target, f, coef], sem=("arbitrary",))


def _silu_parts(g):
    s = jax.nn.sigmoid(g)
    return s, g * s


def _ffn_up(name, h, wgu4, plan=None):
    S, D = h.shape
    SH = wgu4.shape[2]
    F = 2 * SH
    tm = _pick(S, (512,))
    tn = _pick(SH, (1408, 256))
    nts = SH // tn

    def body(h_ref, wg_ref, wu_ref, gu_ref, act_ref):
        hv = h_ref[...]
        g = jnp.dot(hv, wg_ref[...], preferred_element_type=F32)
        u = jnp.dot(hv, wu_ref[...], preferred_element_type=F32)
        gu_ref[0] = g.astype(BF)
        gu_ref[1] = u.astype(BF)
        act_ref[...] = (_silu_parts(g)[1] * u).astype(BF)

    return _call(name, body, (S // tm, F // tn),
                 [pl.BlockSpec((tm, D), lambda i, j: (i, 0)),
                  pl.BlockSpec((None, D, tn), lambda i, j: (j // nts, 0, j % nts)),
                  pl.BlockSpec((None, D, tn), lambda i, j: (2 + j // nts, 0, j % nts))],
                 [pl.BlockSpec((2, tm, tn), lambda i, j: (0, i, j)), pl.BlockSpec((tm, tn), lambda i, j: (i, j))],
                 [_sds((2, S, F), BF), _sds((S, F), BF)], [h, wgu4, wgu4], sem=("parallel", "parallel"), plan=plan)


def _mm_residual(name, a, w, x_in, coef, plan=None):
    S, K = a.shape
    D = w.shape[1]
    tm = _pick(S, (1024,))
    tn = _pick(D, (512,))
    tk = K

    def epi(p, e, o, rs):
        o[0][rs, :] = e[0][rs, :] + e[1][...] * p
        o[1][rs, :] = p.astype(BF)

    tile = pl.BlockSpec((tm, tn), lambda i, j, k: (i, j))
    return _mm(name, (S // tm, D // tn, K // tk), NN,
               a, pl.BlockSpec((tm, tk), lambda i, j, k: (i, k)),
               w, pl.BlockSpec((tk, tn), lambda i, j, k: (k, j)),
               [x_in, coef], [tile, pl.BlockSpec((1, tn), lambda i, j, k: (0, j))],
               [_sds((S, D), F32), _sds((S, D), BF)], [tile, tile], (tm, tn), epi, plan)


def _ffn_dact(name, df, wd, gu, plan=None):
    S, D = df.shape
    F = wd.shape[0]
    tm = _pick(S, (512,))
    tn = _pick(F, (1408, 256))

    def epi(p, e, o, rs):
        g = e[0][0, rs, :].astype(F32)
        u = e[0][1, rs, :].astype(F32)
        s, sg = _silu_parts(g)
        o[0][0, rs, :] = (p * u * (s * (1.0 + g * (1.0 - s)))).astype(BF)
        o[0][1, rs, :] = (p * sg).astype(BF)

    pair = pl.BlockSpec((2, tm, tn), lambda i, j, k: (0, i, j))
    return _mm(name, (S // tm, F // tn, 1), NT,
               df, pl.BlockSpec((tm, D), lambda i, j, k: (i, 0)),
               wd, pl.BlockSpec((tn, D), lambda i, j, k: (j, 0)),
               [gu], [pair], [_sds((2, S, F), BF)], [pair], None, epi, plan)[0]


def _ffn_dh(name, dgu, wgu4, plan=None):
    _, S, F = dgu.shape
    _, D, SH = wgu4.shape
    tm = _pick(S, (1024,))
    tn = _pick(D, (1024,))
    tk = SH
    nkp = F // tk
    nks = SH // tk
    return _mm(name, (S // tm, D // tn, 2 * nkp), NT,
               dgu, pl.BlockSpec((None, tm, tk), lambda i, j, k: (k // nkp, i, k % nkp)),
               wgu4, pl.BlockSpec((None, tn, tk), lambda i, j, k: (k // nks, j, k % nks)),
               [], [], [_sds((S, D), F32)], [pl.BlockSpec((tm, tn), lambda i, j, k: (i, j))], (tm, tn), _store, plan)[0]


def _ffn_dwgu(name, h, dgu, plan=None):
    _, S, F = dgu.shape
    D = h.shape[1]
    SH = F // 2
    tk1 = _pick(D, (1024,))
    tn = _pick(SH, (1408, 256))
    ts = _pick(S, (2048, 1024))
    npj = F // tn
    nsj = SH // tn
    return _mm(name, (D // tk1, 2 * npj, S // ts), TN,
               h, pl.BlockSpec((ts, tk1), lambda i, j, k: (k, i)),
               dgu, pl.BlockSpec((None, ts, tn), lambda i, j, k: (j // npj, k, j % npj)),
               [], [], [_sds((4, D, SH), BF)],
               [pl.BlockSpec((None, tk1, tn), lambda i, j, k: (j // nsj, i, j % nsj))], (tk1, tn), _store, plan)[0]


def _mm_tn(name, a, b, tk1_prefs, tn_prefs, plan=None):
    S, K1 = a.shape
    N = b.shape[1]
    tk1 = _pick(K1, tk1_prefs)
    tn = _pick(N, tn_prefs)
    ts = _pick(S, (2048, 1024))
    return _mm(name, (K1 // tk1, N // tn, S // ts), TN,
               a, pl.BlockSpec((ts, tk1), lambda i, j, k: (k, i)),
               b, pl.BlockSpec((ts, tn), lambda i, j, k: (k, j)),
               [], [], [_sds((K1, N), BF)], [pl.BlockSpec((tk1, tn), lambda i, j, k: (i, j))], (tk1, tn), _store, plan)[0]


def _pool_window(ext, w, back):
    n = ext.shape[0]
    s = ext
    for step in (1, 2, 4, 8):
        sh = pltpu.roll(s, (n - step) if back else step, axis=0)
        s = jnp.where(w > step, s + sh, s)
    return s


def _pool_fwd(z, PW):
    S = z.shape[0]
    tc = _pick(S, (1024,))
    bpg = (PW // 4) // LANES
    H = POOL_MAX_W

    def body(prev_ref, u_ref, o_ref):
        i = pl.program_id(0)
        j = pl.program_id(1)
        w = lax.shift_left(jnp.int32(2), j // bpg)
        u = u_ref[...]
        prev = jnp.where(i > 0, prev_ref[...], 0.0)
        s = _pool_window(jnp.concatenate([prev, u], axis=0), w, False)[H:]
        t = i * tc + lax.broadcasted_iota(jnp.int32, (tc, LANES), 0)
        cnt = jnp.minimum(t + 1, w).astype(F32)
        o_ref[...] = (s / cnt - u).astype(BF)

    r = tc // H
    return _call("pool_fwd", body, (S // tc, PW // LANES),
                 [pl.BlockSpec((H, LANES), lambda i, j: (jnp.maximum(i * r - 1, 0), j)),
                  pl.BlockSpec((tc, LANES), lambda i, j: (i, j))],
                 [pl.BlockSpec((tc, LANES), lambda i, j: (i, j))], [_sds((S, PW), BF)], [z, z], sem=("parallel", "parallel"))[0]


def _pool_bwd(dpooled):
    S, PW = dpooled.shape
    tc = _pick(S, (1024,))
    bpg = (PW // 4) // LANES
    H = POOL_MAX_W
    last = S // tc - 1

    def body(dp_ref, nxt_ref, o_ref):
        i = pl.program_id(0)
        j = pl.program_id(1)
        w = lax.shift_left(jnp.int32(2), j // bpg)
        dp = dp_ref[...]
        nxt = jnp.where(i < last, nxt_ref[...], 0.0)
        ext = jnp.concatenate([dp, nxt], axis=0)
        t = i * tc + lax.broadcasted_iota(jnp.int32, (tc + H, LANES), 0)
        cnt = jnp.minimum(t + 1, w).astype(F32)
        s = _pool_window(ext / cnt, w, True)[:tc]
        o_ref[...] = (s - dp).astype(BF)

    r = tc // H
    nh = S // H - 1
    return _call("pool_bwd", body, (S // tc, PW // LANES),
                 [pl.BlockSpec((tc, LANES), lambda i, j: (i, j)),
                  pl.BlockSpec((H, LANES), lambda i, j: (jnp.minimum((i + 1) * r, nh), j))],
                 [pl.BlockSpec((tc, LANES), lambda i, j: (i, j))], [_sds((S, PW), BF)], [dpooled, dpooled],
                 sem=("parallel", "parallel"))[0]


def _pool_mix(pooled, pm, scale):
    S, PW = pooled.shape
    gw = PW // 4
    ts = _pick(S, (1024,))

    def epi(p, e, o, rs):
        o[0][rs, :] = (p * e[0][...]).astype(BF)

    tile = pl.BlockSpec((ts, gw), lambda i, j, k: (i, j))
    return _mm("pool_mix", (S // ts, 4, 1), NN, pooled, tile,
               pm, pl.BlockSpec((None, gw, gw), lambda i, j, k: (j, 0, 0)),
               [scale], [pl.BlockSpec((1, gw), lambda i, j, k: (0, j))], [_sds((S, PW), BF)], [tile], None, epi)[0]


def _pool_mix_bwd(pooled, pm, scale, dmixed):
    S, PW = pooled.shape
    gw = PW // 4
    ts = _pick(S, (1024,))

    def body(p_ref, pm_ref, sc_ref, dm_ref, dp_ref, dpm_ref, dsc_ref):
        i = pl.program_id(1)
        p = p_ref[...]
        w = pm_ref[...]
        dm = dm_ref[...]
        pre = jnp.dot(p, w, preferred_element_type=F32)
        dmp = (dm * sc_ref[...]).astype(BF)
        dp_ref[...] = lax.dot_general(dmp, w, NT, preferred_element_type=F32)
        dw = lax.dot_general(p, dmp, TN, preferred_element_type=F32)
        ds = jnp.concatenate([jnp.sum(dm * pre, axis=0, keepdims=True), jnp.zeros((7, gw), F32)], axis=0)
        _acc_rows(dpm_ref, i == 0, dw)
        _acc_rows(dsc_ref, i == 0, ds)

    tile = pl.BlockSpec((ts, gw), lambda g, i: (i, g))
    return _call("pool_mix_bwd", body, (4, S // ts),
                 [tile, pl.BlockSpec((None, gw, gw), lambda g, i: (g, 0, 0)), pl.BlockSpec((1, gw), lambda g, i: (0, g)), tile],
                 [tile, pl.BlockSpec((None, gw, gw), lambda g, i: (g, 0, 0)), pl.BlockSpec((8, gw), lambda g, i: (0, g))],
                 [_sds((S, PW), F32), _sds((4, gw, gw), F32), _sds((8, PW), F32)], [pooled, pm, scale, dmixed],
                 sem=("parallel", "arbitrary"))


def _bucket_onehot():
    ql = np.arange(BLK)[:, None]
    j = np.arange(2 * BLK)[None, :]
    d = BLK + ql - j
    n = np.clip(d, 0, None)
    nf = np.maximum(n, 1).astype(np.float32)
    max_exact = NUM_BUCKETS // 2
    large = max_exact + (np.log(nf / max_exact) / np.log(BLK / max_exact) * (NUM_BUCKETS - max_exact)).astype(np.int32)
    large = np.minimum(large, NUM_BUCKETS - 1)
    bucket = np.where(n < max_exact, n, large).astype(np.int32)
    valid = (d >= 0) & (d < BLK)
    oh = (bucket[None] == np.arange(NUM_BUCKETS)[:, None, None]) & valid[None]
    return oh.reshape(NUM_BUCKETS, BLK * 2 * BLK)


def _three_bf16(v):
    hi = v.astype(BF)
    r1 = v - hi.astype(F32)
    mid = r1.astype(BF)
    lo = (r1 - mid.astype(F32)).astype(BF)
    return hi, mid, lo


def _bias_table(rel_bias):
    oh = jnp.asarray(_bucket_onehot(), BF)
    tn = 4096

    def body(rb_ref, oh_ref, o_ref):
        o = oh_ref[...]
        hi, mid, lo = _three_bf16(rb_ref[...])
        acc = lax.dot_general(hi, o, TN, preferred_element_type=F32)
        acc = acc + lax.dot_general(mid, o, TN, preferred_element_type=F32)
        acc = acc + lax.dot_general(lo, o, TN, preferred_element_type=F32)
        on_band = jnp.sum(o.astype(F32), axis=0, keepdims=True) > 0.5
        o_ref[...] = jnp.where(on_band, acc, NEG_INF)

    n = oh.shape[1]
    return _call("bias_table", body, (n // tn,),
                 [pl.BlockSpec((NUM_BUCKETS, N_HEADS), lambda i: (0, 0)), pl.BlockSpec((NUM_BUCKETS, tn), lambda i: (0, i))],
                 [pl.BlockSpec((N_HEADS, tn), lambda i: (0, i))], [_sds((N_HEADS, n), F32)], [rel_bias, oh], sem=("parallel",))[0]


def _rel_bias_grad(dl):
    oh = jnp.asarray(_bucket_onehot(), BF)
    n = oh.shape[1]
    tk = 4096

    def body(dl_ref, oh_ref, o_ref):
        o = oh_ref[...]
        hi, mid, lo = _three_bf16(dl_ref[...])
        acc = lax.dot_general(o, hi, NT, preferred_element_type=F32)
        acc = acc + lax.dot_general(o, mid, NT, preferred_element_type=F32)
        acc = acc + lax.dot_general(o, lo, NT, preferred_element_type=F32)
        _acc_rows(o_ref, pl.program_id(0) == 0, acc)

    return _call("rel_bias_grad", body, (n // tk,),
                 [pl.BlockSpec((N_HEADS, tk), lambda i: (0, i)), pl.BlockSpec((NUM_BUCKETS, tk), lambda i: (0, i))],
                 [pl.BlockSpec((NUM_BUCKETS, N_HEADS), lambda i: (0, 0))], [_sds((NUM_BUCKETS, N_HEADS), F32)], [dl, oh],
                 sem=("arbitrary",))[0]


def _lo_half(shape):
    return lax.broadcasted_iota(jnp.int32, shape, 1) < HEAD_DIM


def _half_sum(x, lo):
    s_lo = jnp.sum(jnp.where(lo, x, 0.0), axis=-1, keepdims=True)
    s_hi = jnp.sum(jnp.where(lo, 0.0, x), axis=-1, keepdims=True)
    return jnp.where(lo, s_lo, s_hi)


def _norm2(x, lo):
    r = lax.rsqrt(_half_sum(x * x, lo) * (1.0 / HEAD_DIM) + EPS)
    return x * r, r


def _norm2_bwd(dy, xhat, r, gain, lo):
    dxhat = dy * gain
    dx = r * (dxhat - xhat * (_half_sum(dxhat * xhat, lo) * (1.0 / HEAD_DIM)))
    return dx, dy * xhat


def _swap(x):
    return pltpu.roll(x, HEAD_DIM, axis=1)


def _attn_logits(n, kk, zq_ref, kn, qg, bias_ref, sink_ref, lo_k):
    lo_q = _lo_half((BLK, LANES))
    half_k = lo_k if kk == 0 else jnp.logical_not(lo_k)
    K = jnp.where(half_k, kn, 0.0).astype(BF)
    rows, qhats, qrs = [], [], []
    for jp in range(4):
        xq = zq_ref[:, jp * LANES:(jp + 1) * LANES]
        qhat, qr = _norm2(xq, lo_q)
        qn = qhat * qg * (HEAD_DIM ** -0.5)
        qs = _swap(qn)
        rows += [qn, qs] if kk == 0 else [qs, qn]
        qhats.append(qhat)
        qrs.append(qr)
    Q = jnp.concatenate(rows, axis=0).astype(BF)
    qk = lax.dot_general(Q, K, NT, preferred_element_type=F32)
    b = bias_ref[8 * kk:8 * kk + 8].reshape(8 * BLK, 2 * BLK)
    col = lax.broadcasted_iota(jnp.int32, qk.shape, 1)
    ok = (b > -1e29) & ((n > 0) | (col >= BLK))
    l = jnp.where(ok, qk + b, NEG_INF)
    sink = sink_ref[kk]
    m = jnp.maximum(jnp.max(l, axis=-1, keepdims=True), sink)
    e = jnp.exp(l - m)
    es = jnp.exp(sink - m)
    den = jnp.sum(e, axis=-1, keepdims=True) + es
    return Q, K, e / den, es / den, qhats, qrs


def _attn_specs(o_q, o_k):
    nq = o_q // 512
    nk = o_k // LANES
    prev = lambda n: (jnp.maximum(n - 1, 0), nk)
    prev_v = lambda n: (jnp.maximum(n - 1, 0), nk + 1)
    return [pl.BlockSpec((BLK, 512), lambda n: (n, nq)), pl.BlockSpec((BLK, 512), lambda n: (n, nq + 1)),
            pl.BlockSpec((BLK, LANES), prev), pl.BlockSpec((BLK, LANES), lambda n: (n, nk)),
            pl.BlockSpec((BLK, LANES), prev_v), pl.BlockSpec((BLK, LANES), lambda n: (n, nk + 1)),
            pl.BlockSpec((1, LANES), lambda n: (0, 0)), pl.BlockSpec((1, LANES), lambda n: (0, 0)),
            pl.BlockSpec((N_KV, 8 * BLK, 1), lambda n: (0, 0, 0)),
            pl.BlockSpec((N_HEADS, BLK, 2 * BLK), lambda n: (0, 0, 0))]


def _attn_fwd(z, o_q, o_k, qg2, kg2, sink_rows, bias, plan=None):
    S = z.shape[0]

    def body(zq0, zq1, zkp, zkc, zvp, zvc, qg_ref, kg_ref, sink_ref, bias_ref, o_ref):
        n = pl.program_id(0)
        lo_k = _lo_half((2 * BLK, LANES))
        lo_q = _lo_half((BLK, LANES))
        khat, _ = _norm2(jnp.concatenate([zkp[...], zkc[...]], axis=0), lo_k)
        kn = khat * kg_ref[...]
        vb = jnp.concatenate([zvp[...], zvc[...]], axis=0).astype(BF)
        for kk, zq in enumerate((zq0, zq1)):
            _, _, p, _, _, _ = _attn_logits(n, kk, zq, kn, qg_ref[...], bias_ref, sink_ref, lo_k)
            r = jnp.dot(p.astype(BF), vb, preferred_element_type=F32)
            for jp in range(4):
                ev = r[(2 * jp) * BLK:(2 * jp + 1) * BLK]
                od = r[(2 * jp + 1) * BLK:(2 * jp + 2) * BLK]
                pair = jnp.where(lo_q, ev, _swap(od)) if kk == 0 else jnp.where(lo_q, _swap(ev), od)
                c0 = (4 * kk + jp) * LANES
                o_ref[:, c0:c0 + LANES] = pair.astype(BF)

    return _call("attn_fwd", body, (S // BLK,), _attn_specs(o_q, o_k), [pl.BlockSpec((BLK, ATT_W), lambda n: (n, 0))],
                 [_sds((S, ATT_W), BF)], [z, z, z, z, z, z, qg2, kg2, sink_rows, bias], sem=("parallel",), plan=plan)[0]


def _attn_bwd(z, o_q, o_k, qg2, kg2, sink_rows, bias, dout, plan=None):
    S = z.shape[0]

    def body(zq0, zq1, zkp, zkc, zvp, zvc, qg_ref, kg_ref, sink_ref, bias_ref, do_ref,
             dq_ref, dkp_ref, dkc_ref, dvp_ref, dvc_ref, dl_ref, dsink_ref, dgain_ref):
        n = pl.program_id(0)
        lo_k = _lo_half((2 * BLK, LANES))
        lo_q = _lo_half((BLK, LANES))
        qg = qg_ref[...]
        kg = kg_ref[...]
        khat, kr = _norm2(jnp.concatenate([zkp[...], zkc[...]], axis=0), lo_k)
        kn = khat * kg
        vf = jnp.concatenate([zvp[...], zvc[...]], axis=0)

        @pl.when(n == 0)
        def _():
            dl_ref[...] = jnp.zeros_like(dl_ref)
            dsink_ref[...] = jnp.zeros_like(dsink_ref)
            dgain_ref[...] = jnp.zeros_like(dgain_ref)

        dkn = jnp.zeros((2 * BLK, LANES), F32)
        dvb = jnp.zeros((2 * BLK, LANES), F32)
        dqg = jnp.zeros((1, LANES), F32)
        for kk, zq in enumerate((zq0, zq1)):
            half_k = lo_k if kk == 0 else jnp.logical_not(lo_k)
            Q, K, p, ps, qhats, qrs = _attn_logits(n, kk, zq, kn, qg, bias_ref, sink_ref, lo_k)
            rows = []
            for jp in range(4):
                c0 = (4 * kk + jp) * LANES
                x = do_ref[:, c0:c0 + LANES]
                rows += [x, _swap(x)] if kk == 0 else [_swap(x), x]
            dO = jnp.concatenate(rows, axis=0).astype(BF)
            V = jnp.where(half_k, vf, 0.0).astype(BF)
            dP = lax.dot_general(dO, V, NT, preferred_element_type=F32)
            delta = jnp.sum(p * dP, axis=-1, keepdims=True)
            dS = p * (dP - delta)
            dsink_ref[kk] += -ps * delta
            dl_ref[8 * kk:8 * kk + 8] += dS.reshape(8, BLK, 2 * BLK)
            dSb = dS.astype(BF)
            dvb = dvb + jnp.where(half_k, lax.dot_general(p.astype(BF), dO, TN, preferred_element_type=F32), 0.0)
            dkn = dkn + jnp.where(half_k, lax.dot_general(dSb, Q, TN, preferred_element_type=F32), 0.0)
            dQ = jnp.dot(dSb, K, preferred_element_type=F32) * (HEAD_DIM ** -0.5)
            for jp in range(4):
                ev = dQ[(2 * jp) * BLK:(2 * jp + 1) * BLK]
                od = dQ[(2 * jp + 1) * BLK:(2 * jp + 2) * BLK]
                dy = (ev + _swap(od)) if kk == 0 else (_swap(ev) + od)
                dx, gq = _norm2_bwd(dy, qhats[jp], qrs[jp], qg, lo_q)
                dqg = dqg + jnp.sum(gq, axis=0, keepdims=True)
                c0 = (4 * kk + jp) * LANES
                dq_ref[:, c0:c0 + LANES] = dx.astype(BF)
        dk, gk = _norm2_bwd(dkn, khat, kr, kg, lo_k)
        dkp_ref[...] = dk[:BLK]
        dkc_ref[...] = dk[BLK:]
        dvp_ref[...] = dvb[:BLK]
        dvc_ref[...] = dvb[BLK:]
        dgain_ref[...] += jnp.concatenate([dqg, jnp.sum(gk, axis=0, keepdims=True), jnp.zeros((6, LANES), F32)], axis=0)

    blk = pl.BlockSpec((BLK, LANES), lambda n: (n, 0))
    wide = pl.BlockSpec((BLK, ATT_W), lambda n: (n, 0))
    return _call(
        "attn_bwd", body, (S // BLK,), _attn_specs(o_q, o_k) + [wide],
        [wide, blk, blk, blk, blk, pl.BlockSpec((N_HEADS, BLK, 2 * BLK), lambda n: (0, 0, 0)),
         pl.BlockSpec((N_KV, 8 * BLK, 1), lambda n: (0, 0, 0)), pl.BlockSpec((8, LANES), lambda n: (0, 0))],
        [_sds((S, ATT_W), BF), _sds((S, LANES), F32), _sds((S, LANES), F32), _sds((S, LANES), F32), _sds((S, LANES), F32),
         _sds((N_HEADS, BLK, 2 * BLK), F32), _sds((N_KV, 8 * BLK, 1), F32), _sds((8, LANES), F32)],
        [z, z, z, z, z, z, qg2, kg2, sink_rows, bias, dout], sem=("arbitrary",), plan=plan)


def _kv_combine(dkp, dkc, dvp, dvc):
    S = dkc.shape[0]
    last = S // BLK - 1

    def body(kp_ref, kc_ref, vp_ref, vc_ref, dk_ref, dv_ref):
        more = pl.program_id(0) < last
        dk_ref[...] = (kc_ref[...] + jnp.where(more, kp_ref[...], 0.0)).astype(BF)
        dv_ref[...] = (vc_ref[...] + jnp.where(more, vp_ref[...], 0.0)).astype(BF)

    cur = pl.BlockSpec((BLK, LANES), lambda n: (n, 0))
    nxt = pl.BlockSpec((BLK, LANES), lambda n: (jnp.minimum(n + 1, last), 0))
    return _call("kv_combine", body, (S // BLK,), [nxt, cur, nxt, cur], [cur, cur],
                 [_sds((S, LANES), BF), _sds((S, LANES), BF)], [dkp, dkc, dvp, dvc], sem=("parallel",))


def _merge_fwd(mixed, attn, wpu4, wau4, z, o_ga, plan=None):
    S, PW = mixed.shape
    _, _, CS = wpu4.shape
    D = 4 * CS
    tm = _pick(S, (1024,))
    tn = 256
    nsj = CS // tn
    na = o_ga // tn
    nb = (o_ga + D) // tn

    def body(m_ref, a_ref, wp_ref, wa_ref, ga_ref, gb_ref, mg_ref, yy_ref):
        yp = jnp.dot(m_ref[...], wp_ref[...], preferred_element_type=F32)
        ya = jnp.dot(a_ref[...], wa_ref[...], preferred_element_type=F32)
        mg_ref[...] = (jax.nn.sigmoid(ga_ref[...]) * yp + jax.nn.sigmoid(gb_ref[...]) * ya).astype(BF)
        yy_ref[0] = yp.astype(BF)
        yy_ref[1] = ya.astype(BF)

    return _call("merge_fwd", body, (S // tm, D // tn),
                 [pl.BlockSpec((tm, PW), lambda i, j: (i, 0)), pl.BlockSpec((tm, ATT_W), lambda i, j: (i, 0)),
                  pl.BlockSpec((None, PW, tn), lambda i, j: (j // nsj, 0, j % nsj)),
                  pl.BlockSpec((None, ATT_W, tn), lambda i, j: (j // nsj, 0, j % nsj)),
                  pl.BlockSpec((tm, tn), lambda i, j: (i, na + j)), pl.BlockSpec((tm, tn), lambda i, j: (i, nb + j))],
                 [pl.BlockSpec((tm, tn), lambda i, j: (i, j)), pl.BlockSpec((2, tm, tn), lambda i, j: (0, i, j))],
                 [_sds((S, D), BF), _sds((2, S, D), BF)], [mixed, attn, wpu4, wau4, z, z], sem=("parallel", "parallel"), plan=plan)


def _merge_bwd(do, wo, z, o_ga, yy, plan=None):
    S, D = do.shape
    tm = _pick(S, (1024,))
    tn = 256
    na = o_ga // tn
    nb = (o_ga + D) // tn

    def epi(p, e, o, rs):
        sa = jax.nn.sigmoid(e[0][rs, :])
        sb = jax.nn.sigmoid(e[1][rs, :])
        yp = e[2][0, rs, :].astype(F32)
        ya = e[2][1, rs, :].astype(F32)
        o[0][0, rs, :] = (p * yp * sa * (1.0 - sa)).astype(BF)
        o[0][1, rs, :] = (p * ya * sb * (1.0 - sb)).astype(BF)
        o[1][0, rs, :] = (p * sa).astype(BF)
        o[1][1, rs, :] = (p * sb).astype(BF)

    pair = pl.BlockSpec((2, tm, tn), lambda i, j, k: (0, i, j))
    return _mm("merge_bwd", (S // tm, D // tn, 1), NT,
               do, pl.BlockSpec((tm, D), lambda i, j, k: (i, 0)),
               wo, pl.BlockSpec((tn, D), lambda i, j, k: (j, 0)),
               [z, z, yy], [pl.BlockSpec((tm, tn), lambda i, j, k: (i, na + j)), pl.BlockSpec((tm, tn), lambda i, j, k: (i, nb + j)), pair],
               [_sds((2, S, D), BF), _sds((2, S, D), BF)], [pair, pair], None, epi, plan)


def _mm_up_t(name, dyy, which, w4):
    _, S, D = dyy.shape
    _, K, CS = w4.shape
    tm = _pick(S, (1024,))
    return _mm(name, (S // tm, 1, N_CHIPS), NT,
               dyy, pl.BlockSpec((None, tm, CS), lambda i, j, k: (which, i, k)),
               w4, pl.BlockSpec((None, K, CS), lambda i, j, k: (k, 0, 0)),
               [], [], [_sds((S, K), F32)], [pl.BlockSpec((tm, K), lambda i, j, k: (i, 0))], (tm, K), _store)[0]


def _mm_up_dw(name, a, dyy, which):
    _, S, D = dyy.shape
    K = a.shape[1]
    CS = D // N_CHIPS
    ts = _pick(S, (1024,))
    return _mm(name, (1, N_CHIPS, S // ts), TN,
               a, pl.BlockSpec((ts, K), lambda i, j, k: (k, 0)),
               dyy, pl.BlockSpec((None, ts, CS), lambda i, j, k: (which, k, j)),
               [], [], [_sds((N_CHIPS, K, CS), BF)], [pl.BlockSpec((None, K, CS), lambda i, j, k: (j, 0, 0))], (K, CS), _store)[0]


def _adamw(w, g, m, v):
    m = ADAM_B1 * m + (1.0 - ADAM_B1) * g
    v = ADAM_B2 * v + (1.0 - ADAM_B2) * (g * g)
    m_hat = m / (1.0 - ADAM_B1 ** ADAM_STEP)
    v_hat = v / (1.0 - ADAM_B2 ** ADAM_STEP)
    delta = -ADAM_LR * (m_hat / (jnp.sqrt(v_hat) + ADAM_EPS) + ADAM_WD * w)
    return delta, m, v


def _mod_fwd(c_all, w_ada, b_sh):
    D, cols = w_ada.shape
    tn = cols // 9

    def body(c_ref, w_ref, b_ref, o_ref):
        cv = c_ref[...]
        sc = (cv * jax.nn.sigmoid(cv)).astype(BF)
        o_ref[...] = jnp.dot(sc, w_ref[...].astype(BF), preferred_element_type=F32) + b_ref[...]

    return _call("mod_fwd", body, (9,),
                 [pl.BlockSpec((N_DEV, D), lambda j: (0, 0)), pl.BlockSpec((D, tn), lambda j: (0, j)), pl.BlockSpec((1, tn), lambda j: (0, j))],
                 [pl.BlockSpec((N_DEV, tn), lambda j: (0, j))], [_sds((N_DEV, cols), F32)], [c_all, w_ada, b_sh], sem=("parallel",))[0]


def _wada_bwd(c_all, dmod_sh, w, m, v, plan=None):
    D, cols = w.shape
    tn = cols // 18

    def body(c_ref, d_ref, w_ref, m_ref, v_ref, g_ref, dl_ref, nm_ref, nv_ref):
        cv = c_ref[...]
        sc = (cv * jax.nn.sigmoid(cv)).astype(BF)
        g = lax.dot_general(sc, d_ref[...].astype(BF), TN, preferred_element_type=F32)
        g_ref[...] = g
        dl_ref[...], nm_ref[...], nv_ref[...] = _adamw(w_ref[...], g, m_ref[...], v_ref[...])

    tile = pl.BlockSpec((D, tn), lambda j: (0, j))
    out = _sds((D, cols), F32)
    return _call("wada_bwd", body, (18,),
                 [pl.BlockSpec((N_DEV, D), lambda j: (0, 0)), pl.BlockSpec((N_DEV, tn), lambda j: (0, j)), tile, tile, tile],
                 [tile] * 4, [out] * 4, [c_all, dmod_sh, w, m, v], sem=("parallel",), plan=plan)


def _adam_2d(name, w, g, m, v):
    R, C = w.shape
    tr = _row_tile(R, 256)

    def body(w_ref, g_ref, m_ref, v_ref, dl_ref, nm_ref, nv_ref):
        dl_ref[...], nm_ref[...], nv_ref[...] = _adamw(w_ref[...], g_ref[...], m_ref[...], v_ref[...])

    tile = pl.BlockSpec((tr, C), lambda i: (i, 0))
    out = _sds((R, C), F32)
    return _call(name, body, (R // tr,), [tile] * 4, [tile] * 3, [out] * 3, [w, g, m, v], sem=("parallel",))


def _small_finish(parts, w, m, v):
    _, R, C = parts.shape

    def body(p_ref, w_ref, m_ref, v_ref, g_ref, dl_ref, nm_ref, nv_ref):
        g = p_ref[0]
        for d in range(1, N_DEV):
            g = g + p_ref[d]
        g_ref[...] = g
        dl_ref[...], nm_ref[...], nv_ref[...] = _adamw(w_ref[...], g, m_ref[...], v_ref[...])

    out = _sds((R, C), F32)
    return pl.pallas_call(body, out_shape=[out] * 4, name="small_finish",
                          compiler_params=pltpu.CompilerParams(vmem_limit_bytes=VMEM_LIMIT))(parts, w, m, v)


def _my_chip():
    return 2 * lax.axis_index("x") + lax.axis_index("y")


def _cast_into_slot(name, w):
    R, C = w.shape
    tr = _row_tile(R, 256)

    def body(w_ref, o_ref):
        o_ref[...] = w_ref[...].astype(BF)

    return _call(name, body, (R // tr,), [pl.BlockSpec((tr, C), lambda i: (i, 0))],
                 [pl.BlockSpec((None, tr, C), lambda i: (_my_chip(), i, 0))], [_sds((N_CHIPS, R, C), BF)], [w], sem=("parallel",))[0]


def _add_pair(name, p, q):
    _, H, C = q.shape
    tr = _row_tile(H, 512)
    nt = H // tr

    def body(p_ref, q_ref, o_ref):
        o_ref[...] = (p_ref[...].astype(F32) + q_ref[...].astype(F32)).astype(BF)

    tile = pl.BlockSpec((None, tr, C), lambda k, i: (k, i, 0))
    return _call(name, body, (N_CHIPS, nt), [pl.BlockSpec((None, tr, C), lambda k, i: (k, lax.axis_index("c") * nt + i, 0)), tile],
                 [tile], [_sds(q.shape, BF)], [p, q], sem=("parallel", "parallel"))[0]


def _sum_chips(name, u, t):
    _, H, C = u.shape
    tr = _row_tile(H, 256)

    def body(u_ref, t_ref, o_ref):
        r = _my_chip()
        own = t_ref[...].astype(F32)
        pick = lambda k: jnp.where(r == k, own, u_ref[k].astype(F32))
        o_ref[...] = ((pick(0) + pick(1)) + pick(2)) + pick(3)

    return _call(name, body, (H // tr,),
                 [pl.BlockSpec((N_CHIPS, tr, C), lambda i: (0, i, 0)), pl.BlockSpec((None, tr, C), lambda i: (_my_chip(), i, 0))],
                 [pl.BlockSpec((None, tr, C), lambda i: (lax.axis_index("c"), i, 0))], [_sds((2, H, C), F32)], [u, t],
                 sem=("parallel",))[0]


BIG = ("gu1", "down1", "w_in", "pool_mix", "pool_up", "attn_up", "o", "gu2", "down2")
MIX = ("o", "pool_up", "attn_up", "pool_mix")
EARLY = ("down1", "w_in", "pool_mix", "pool_up", "attn_up", "o")

SCHEDULE = {
    "+gather_gu1_ici": ([("ici", ("gu1",))], []),
    "+gather_gu1_d2d": ([("d2d", ("gu1",))], []),
    "ffn1_up": ([("ici", EARLY)], []),
    "+gather_early_d2d": ([("d2d", EARLY)], []),
    "ffn1_down": ([("ici", ("gu2",))], []),
    "mix_in": ([("ici", ("down2",)), ("d2d", ("gu2",))], []),
    "mix_out": ([("d2d", ("down2",))], []),
    "ffn2_dwd": ([("split", ("gu2",))], [("add", ("gu2",))]),
    "ffn2_dh": ([("owners", ("gu2",)), ("split", ("down2",))], [("add", ("down2",)), ("sum", ("gu2",))]),
    "merge_bwd": ([("owners", ("down2",)), ("join", ("gu2",))], [("sum", ("down2",)), ("adam", ("gu2",))]),
    "mix_dwo": ([("join", ("down2",))], [("adam", ("down2",))]),
    "attn_bwd": ([("split", MIX)], [("add", MIX)]),
    "mix_dwin": ([("owners", MIX)], [("sum", MIX)]),
    "mix_dh": ([("split", ("w_in",)), ("join", MIX)], [("add", ("w_in",)), ("adam", MIX)]),
    "ffn1_dact": ([("owners", ("w_in",))], [("sum", ("w_in",))]),
    "ffn1_dwgu": ([("join", ("w_in",))], [("adam", ("w_in",))]),
    "ffn1_dwd": ([("split", ("gu1",))], [("add", ("gu1",))]),
    "ffn1_dh": ([("owners", ("gu1",)), ("split", ("down1",))], [("add", ("down1",)), ("sum", ("gu1",))]),
    "rms_mod_bwd1": ([("owners", ("down1",)), ("join", ("gu1",))], [("sum", ("down1",)), ("adam", ("gu1",))]),
    "+join_down1": ([("join", ("down1",))], [("adam", ("down1",))]),
}


class _Plan:
    def __init__(self, w2, m2, v2, full, D, gw):
        self.w2, self.m2, self.v2, self.full, self.D, self.gw = w2, m2, v2, dict(full), D, gw
        self.part, self.got, self.sums, self.landed, self.g = {}, {}, {}, {}, {}
        self.result = {}
        self.pending = {}

    def _make(self, op, names):
        if op == "ici":
            return _gather_ici_stage([self.full[k] for k in names])
        if op == "d2d":
            return _gather_d2d_stage([self.full[k] for k in names])
        if op == "split":
            return _split_stage([self.part[k] for k in names])
        if op == "owners":
            return _owners_stage([self.sums[k] for k in names])
        return _join_stage([self.g[k] for k in names])

    def stages(self, name):
        ops = SCHEDULE.get(name, ([], []))[0]
        return [self._make(op, names) for op, names in ops]

    def done(self, name, outs):
        ops, local = SCHEDULE[name]
        for (op, names), res in zip(ops, outs):
            store = {"ici": self.full, "d2d": self.full, "split": self.got, "owners": self.landed, "join": self.g}[op]
            store.update(zip(names, res))
        for op, names in local:
            for k in names:
                if op == "add":
                    self.sums[k] = _add_pair("add_pair_" + k, self.part[k], self.got[k])
                elif op == "sum":
                    self.g[k] = _sum_chips("sum_chips_" + k, self.landed[k], self.sums[k])
                else:
                    g2 = self.g[k].reshape(self.w2[k].shape)
                    self.result[k] = (g2, *_adam_2d("adam_" + k, self.w2[k], g2, self.m2[k], self.v2[k]))

    def alone(self, name):
        self.done(name, _run_stages(name[1:], self.stages(name)))

    def weight(self, k):
        D, gw, f = self.D, self.gw, self.full[k]
        if k in ("down1", "down2", "o"):
            return f.reshape(-1, D)
        if k == "pool_mix":
            return f.reshape(N_CHIPS, 4, gw // N_CHIPS, gw).transpose(1, 0, 2, 3).reshape(4, gw, gw)
        if k == "w_in":
            return f.reshape(-1, D)
        return f

    def partial(self, k, p):
        D, gw = self.D, self.gw
        if k in ("down1", "down2", "o", "w_in"):
            p = p.reshape(N_CHIPS, p.shape[0] // N_CHIPS, p.shape[1])
        elif k == "pool_mix":
            p = p.astype(BF).reshape(4, N_CHIPS, gw // N_CHIPS, gw).transpose(1, 0, 2, 3).reshape(N_CHIPS, gw, gw)
        self.part[k] = p


class _NoComm:
    def __init__(self, weights):
        self.w, self.part = weights, {}

    def stages(self, name):
        return []

    def alone(self, name):
        pass

    def weight(self, k):
        return self.w[k]

    def partial(self, k, p):
        self.part[k] = p


def _row(a, i):
    return a[i:i + 1]


def _local_step(x, target, mod, g_ffn1, g_mix, g_ffn2, pool_scale, q_gain, k_gain, sinks, rel_bias, plan):
    S, D = x.shape
    half = 0.5 * mod
    tile2 = lambda g: jnp.concatenate([g, g], axis=1)
    qg2, kg2 = tile2(q_gain), tile2(k_gain)
    sink_rows = jnp.broadcast_to(sinks.reshape(N_KV, 8, 1, 1), (N_KV, 8, BLK, 1)).reshape(N_KV, 8 * BLK, 1)
    bias = _bias_table(rel_bias).reshape(N_HEADS, BLK, 2 * BLK)

    plan.alone("+gather_gu1_ici")
    plan.alone("+gather_gu1_d2d")
    h1 = _rms_mod_fwd("rms_mod_fwd1", x, g_ffn1, _row(mod, 0), _row(mod, 1))
    gu1, act1 = _ffn_up("ffn1_up", h1, plan.weight("gu1"), plan)
    plan.alone("+gather_early_d2d")
    x1, f1 = _mm_residual("ffn1_down", act1, plan.weight("down1"), x, _row(half, 2), plan)
    h2 = _rms_mod_fwd("rms_mod_fwd2", x1, g_mix, _row(mod, 3), _row(mod, 4))
    w_in_t = plan.weight("w_in")
    IN_W = w_in_t.shape[0]
    PW = plan.weight("pool_up").shape[1]
    o_q, o_k = PW, PW + ATT_W
    o_ga = o_k + 2 * KV_W
    tnz = _pick(IN_W, (1280, 256))
    tmz = _pick(S, (1024,))
    z = _mm("mix_in", (S // tmz, IN_W // tnz, 1), NT, h2, pl.BlockSpec((tmz, D), lambda i, j, k: (i, 0)),
            w_in_t, pl.BlockSpec((tnz, D), lambda i, j, k: (j, 0)), [], [], [_sds((S, IN_W), F32)],
            [pl.BlockSpec((tmz, tnz), lambda i, j, k: (i, j))], None, _store, plan)[0]
    pooled = _pool_fwd(z, PW)
    mixed = _pool_mix(pooled, plan.weight("pool_mix"), pool_scale)
    attn = _attn_fwd(z, o_q, o_k, qg2, kg2, sink_rows, bias)
    merged, yy = _merge_fwd(mixed, attn, plan.weight("pool_up"), plan.weight("attn_up"), z, o_ga)
    x2, fo = _mm_residual("mix_out", merged, plan.weight("o"), x1, _row(mod, 5), plan)
    h3 = _rms_mod_fwd("rms_mod_fwd3", x2, g_ffn2, _row(mod, 6), _row(mod, 7))
    gu2, act2 = _ffn_up("ffn2_up", h3, plan.weight("gu2"))
    x3, f2 = _mm_residual("ffn2_down", act2, plan.weight("down2"), x2, _row(half, 8))
    dx3, df2, loss_acc = _loss_bwd(x3, target, f2, _row(half, 8))

    dgu2 = _ffn_dact("ffn2_dact", df2, plan.weight("down2"), gu2)
    plan.partial("gu2", _ffn_dwgu("ffn2_dwgu", h3, dgu2))
    plan.partial("down2", _mm_tn("ffn2_dwd", act2, df2, (1408, 512), (1024,), plan))
    dh3 = _ffn_dh("ffn2_dh", dgu2, plan.weight("gu2"), plan)
    dx2, do, acc3 = _rms_mod_bwd("rms_mod_bwd3", dh3, x2, dx3, g_ffn2, _row(mod, 7), fo, _row(mod, 5))

    dgab, dyy = _merge_bwd(do, plan.weight("o"), z, o_ga, yy, plan)
    plan.partial("o", _mm_tn("mix_dwo", merged, do, (1024,), (1024,), plan))
    dmixed = _mm_up_t("pool_up_t", dyy, 0, plan.weight("pool_up"))
    dattn = _mm_up_t("attn_up_t", dyy, 1, plan.weight("attn_up"))
    plan.partial("pool_up", _mm_up_dw("pool_up_dw", mixed, dyy, 0))
    plan.partial("attn_up", _mm_up_dw("attn_up_dw", attn, dyy, 1))
    dpooled, dpm, dps = _pool_mix_bwd(pooled, plan.weight("pool_mix"), pool_scale, dmixed)
    plan.partial("pool_mix", dpm)
    du_pool = _pool_bwd(dpooled)
    dq, dkp, dkc, dvp, dvc, dl, dsink, dgain = _attn_bwd(z, o_q, o_k, qg2, kg2, sink_rows, bias, dattn, plan)
    dk, dv = _kv_combine(dkp, dkc, dvp, dvc)
    drb = _rel_bias_grad(dl.reshape(N_HEADS, BLK * 2 * BLK))
    dz = jnp.concatenate([du_pool, dq, dk, dv, dgab[0], dgab[1]], axis=1)
    plan.partial("w_in", _mm_tn("mix_dwin", dz, h2, (1280, 256), (1024,), plan))
    tnd = _pick(D, (512,))
    dh2 = _mm("mix_dh", (S // tmz, D // tnd, 1), NN, dz, pl.BlockSpec((tmz, IN_W), lambda i, j, k: (i, 0)),
              w_in_t, pl.BlockSpec((IN_W, tnd), lambda i, j, k: (0, j)), [], [], [_sds((S, D), F32)],
              [pl.BlockSpec((tmz, tnd), lambda i, j, k: (i, j))], None, _store, plan)[0]
    dx1, df1, acc2 = _rms_mod_bwd("rms_mod_bwd2", dh2, x1, dx2, g_mix, _row(mod, 4), f1, _row(half, 2))

    dgu1 =_ffn_dact("ffn1_dact", df1, plan.weight("down1"), gu1, plan)
    plan.partial("gu1", _ffn_dwgu("ffn1_dwgu", h1, dgu1, plan))
    plan.partial("down1", _mm_tn("ffn1_dwd", act1, df1, (1408, 512), (1024,), plan))
    dh1 = _ffn_dh("ffn1_dh", dgu1, plan.weight("gu1"), plan)
    grad_x, acc1 = _rms_mod_bwd("rms_mod_bwd1", dh1, x, dx1, g_ffn1, _row(mod, 1), plan=plan)

    dmod = jnp.concatenate([_row(acc1, 0), _row(acc1, 1), 0.5 * _row(acc2, 3),
                            _row(acc2, 0), _row(acc2, 1), _row(acc3, 3),
                            _row(acc3, 0), _row(acc3, 1), 0.5 * _row(loss_acc, 1)], axis=0)
    fold = lambda r: r[:, :HEAD_DIM] + r[:, HEAD_DIM:]
    small = dict(
        dmod=dmod, g_ffn1=_row(acc1, 2), g_mix=_row(acc2, 2), g_ffn2=_row(acc3, 2), pool_scale=_row(dps, 0),
        q_gain=fold(_row(dgain, 0)), k_gain=fold(_row(dgain, 1)),
        sinks=jnp.sum(dsink.reshape(N_HEADS, BLK), axis=1).reshape(1, N_HEADS), rel_bias=drb,
        loss=(0.5 / D) * jnp.sum(_row(loss_acc, 0)).reshape(1, 1))
    return grad_x, small


SMALL_ORDER = ("dmod", "g_ffn1", "g_mix", "g_ffn2", "pool_scale", "q_gain", "k_gain", "sinks", "rel_bias", "loss")


def _pack_small(vals):
    flat = jnp.concatenate([vals[k].reshape(-1) for k in SMALL_ORDER])
    n = flat.shape[0]
    rows = -(-n // (8 * LANES)) * 8
    return jnp.pad(flat, (0, rows * LANES - n)).reshape(rows, LANES)


def _unpack_small(packed, like):
    flat = packed.reshape(-1)
    out, off = {}, 0
    for k in SMALL_ORDER:
        n = int(np.prod(like[k].shape))
        out[k] = flat[off:off + n].reshape(like[k].shape)
        off += n
    return out


def kernel(x, c, w_ada, b_ada, g_ffn1, w_ffn1_gu, w_ffn1_down, g_mix, w_in, pool_mix, pool_scale, w_pool_up, q_gain, k_gain, sinks, rel_bias, w_attn_up, w_o, g_ffn2, w_ffn2_gu, w_ffn2_down, loss_target, m_w_ada, m_b_ada, m_g_ffn1, m_w_ffn1_gu, m_w_ffn1_down, m_g_mix, m_w_in, m_pool_mix, m_pool_scale, m_w_pool_up, m_q_gain, m_k_gain, m_sinks, m_rel_bias, m_w_attn_up, m_w_o, m_g_ffn2, m_w_ffn2_gu, m_w_ffn2_down, v_w_ada, v_b_ada, v_g_ffn1, v_w_ffn1_gu, v_w_ffn1_down, v_g_mix, v_w_in, v_pool_mix, v_pool_scale, v_w_pool_up, v_q_gain, v_k_gain, v_sinks, v_rel_bias, v_w_attn_up, v_w_o, v_g_ffn2, v_w_ffn2_gu, v_w_ffn2_down):
    S, D = x.shape[1], x.shape[2]
    gw = pool_mix.shape[3]
    r = 2 * lax.axis_index("x") + lax.axis_index("y")

    two_d = lambda a: a.reshape(-1, a.shape[-1])
    w_sh = dict(gu1=w_ffn1_gu, down1=w_ffn1_down, w_in=w_in, pool_mix=pool_mix, pool_up=w_pool_up, attn_up=w_attn_up, o=w_o,
                gu2=w_ffn2_gu, down2=w_ffn2_down)
    m_sh = dict(gu1=m_w_ffn1_gu, down1=m_w_ffn1_down, w_in=m_w_in, pool_mix=m_pool_mix, pool_up=m_w_pool_up, attn_up=m_w_attn_up,
                o=m_w_o, gu2=m_w_ffn2_gu, down2=m_w_ffn2_down)
    v_sh = dict(gu1=v_w_ffn1_gu, down1=v_w_ffn1_down, w_in=v_w_in, pool_mix=v_pool_mix, pool_up=v_w_pool_up, attn_up=v_w_attn_up,
                o=v_w_o, gu2=v_w_ffn2_gu, down2=v_w_ffn2_down)
    view = lambda k, a: two_d(a).T if k == "w_in" else two_d(a)
    unview = lambda k, a: (a.T if k == "w_in" else a).reshape(w_sh[k].shape)
    w2 = {k: view(k, w_sh[k]) for k in BIG}
    full = {k: _cast_into_slot("cast_" + k, w2[k]) for k in BIG}
    plan = _Plan(w2, {k: view(k, m_sh[k]) for k in BIG}, {k: view(k, v_sh[k]) for k in BIG}, full, D, gw)

    c_all = _gather_all("gather_c", jnp.broadcast_to(c, (8, D)))[:, 0, :]
    cols = w_ada.shape[2]
    b_sh = lax.dynamic_slice(b_ada, (0, r * cols), (1, cols))
    mod_cols = _mod_fwd(c_all, w_ada[0], b_sh)
    mod_all = _chip_exchange("mod_exchange", mod_cols)
    me = 4 * lax.axis_index("x") + 2 * lax.axis_index("y") + lax.axis_index("c")
    mod = lax.dynamic_slice(mod_all, (0, me, 0), (N_CHIPS, 1, cols)).reshape(9, D)

    grad_x, small = _local_step(x[0], loss_target[0], mod, g_ffn1, g_mix, g_ffn2, pool_scale, q_gain, k_gain,
                                sinks, rel_bias, plan)

    small_w = dict(dmod=b_ada, g_ffn1=g_ffn1, g_mix=g_mix, g_ffn2=g_ffn2, pool_scale=pool_scale, q_gain=q_gain, k_gain=k_gain,
                   sinks=sinks, rel_bias=rel_bias, loss=jnp.zeros((1, 1), F32))
    small_m = dict(dmod=m_b_ada, g_ffn1=m_g_ffn1, g_mix=m_g_mix, g_ffn2=m_g_ffn2, pool_scale=m_pool_scale, q_gain=m_q_gain,
                   k_gain=m_k_gain, sinks=m_sinks, rel_bias=m_rel_bias, loss=jnp.zeros((1, 1), F32))
    small_v = dict(dmod=v_b_ada, g_ffn1=v_g_ffn1, g_mix=v_g_mix, g_ffn2=v_g_ffn2, pool_scale=v_pool_scale, q_gain=v_q_gain,
                   k_gain=v_k_gain, sinks=v_sinks, rel_bias=v_rel_bias, loss=jnp.ones((1, 1), F32))
    small_all = _gather_all("gather_small", _pack_small(small))
    sg, sd, sm, sv = [_unpack_small(a, small_w) for a in
                      _small_finish(small_all, _pack_small(small_w), _pack_small(small_m), _pack_small(small_v))]
    loss = sg["loss"].reshape(())

    dmod_all = small_all.reshape(N_DEV, -1)[:, :9 * D]
    dmod_sh = lax.dynamic_slice(dmod_all, (0, r * cols), (N_DEV, cols))
    g_ada, d_ada, nm_ada, nv_ada = _wada_bwd(c_all, dmod_sh, w_ada[0], m_w_ada[0], v_w_ada[0])
    plan.alone("+join_down1")

    big = [{k: unview(k, plan.result[k][i]) for k in BIG} for i in range(4)]

    def ordered(b, ada, sm_):
        return (ada[None], sm_["dmod"], sm_["g_ffn1"], b["gu1"], b["down1"], sm_["g_mix"], b["w_in"], b["pool_mix"],
                sm_["pool_scale"], b["pool_up"], sm_["q_gain"], sm_["k_gain"], sm_["sinks"], sm_["rel_bias"], b["attn_up"],
                b["o"], sm_["g_ffn2"], b["gu2"], b["down2"])

    return (loss, grad_x[None], *ordered(big[0], g_ada, sg), *ordered(big[1], d_ada, sd), *ordered(big[2], nm_ada, sm),
            *ordered(big[3], nv_ada, sv))
```

```python
import numpy as np
import jax
import jax.numpy as jnp
from jax import lax
from jax.experimental import pallas as pl
from jax.experimental.pallas import tpu as pltpu

BF = jnp.bfloat16
F32 = jnp.float32
MESH = pl.DeviceIdType.MESH

EPS = 1e-6
NEG_INF = -1e30
HEAD_DIM = 64
N_HEADS = 16
N_KV = 2
ATT_W = N_HEADS * HEAD_DIM
KV_W = N_KV * HEAD_DIM
BLK = 128
NUM_BUCKETS = 32
POOL_MAX_W = 16
N_CHIPS = 4
N_DEV = 8
LANES = 128
ADAM_LR, ADAM_B1, ADAM_B2, ADAM_EPS, ADAM_WD, ADAM_STEP = 0.001, 0.9, 0.999, 1e-08, 0.01, 10
VMEM_LIMIT = 52 * 1024 * 1024
ANY = pl.BlockSpec(memory_space=pl.ANY)


def _pick(dim, prefs):
    for p in prefs:
        if p <= dim and dim % p == 0:
            return p
    return dim


def _row_tile(rows, cap):
    return max(t for t in range(16, min(rows, cap) + 1, 16) if rows % t == 0)


def _sds(shape, dtype):
    return jax.ShapeDtypeStruct(tuple(shape), dtype)


def _place():
    return lax.axis_index("x"), lax.axis_index("y"), lax.axis_index("c")


def _other_chips(x, y):
    return [(1 - x, y), (x, 1 - y), (1 - x, 1 - y)]


def _chip_of(chip):
    return 2 * chip[0] + chip[1]


def _half_rows(ref, lead, cc, h):
    return ref.at[lead, pl.ds(pl.multiple_of(cc * h, 16), h), :]


class _Stage:
    def __init__(self, bufs, outs, alias, n_sem, start, wait):
        self.bufs, self.outs, self.alias, self.n_sem, self.start, self.wait = bufs, outs, alias, n_sem, start, wait


def _stage_plumbing(stages, n_in0, n_out0):
    bufs, outs, aliases, spans, scratch = [], [], {}, [], []
    for st in stages:
        i0, o0 = len(bufs), len(outs)
        bufs += list(st.bufs)
        outs += list(st.outs)
        for a, b in st.alias.items():
            aliases[n_in0 + i0 + a] = n_out0 + o0 + b
        spans.append((i0, len(bufs), o0, len(outs)))
        scratch += [pltpu.SemaphoreType.DMA((st.n_sem,)), pltpu.SemaphoreType.DMA((st.n_sem,))]

    def run(which, in_refs, out_refs, sem_refs):
        for s, st in enumerate(stages):
            i0, i1, o0, o1 = spans[s]
            getattr(st, which)(in_refs[i0:i1], out_refs[o0:o1], sem_refs[2 * s], sem_refs[2 * s + 1])

    def split(flat):
        return [list(flat[o0:o1]) for (_, _, o0, o1) in spans]

    return bufs, outs, aliases, scratch, run, split


def _run_stages(name, stages):
    bufs, outs, aliases, scratch, run, split = _stage_plumbing(stages, 0, 0)
    ni, no = len(bufs), len(outs)

    def body(*refs):
        ins, os_, sems = refs[:ni], refs[ni:ni + no], refs[ni + no:]
        run("start", ins, os_, sems)
        run("wait", ins, os_, sems)

    res = pl.pallas_call(body, in_specs=[ANY] * ni, out_specs=[ANY] * no, out_shape=outs, input_output_aliases=aliases,
                         scratch_shapes=scratch, name=name)(*bufs)
    return split(res)


def _call(name, body, grid, in_specs, out_specs, out_shape, args, scratch=(), sem=None, plan=None):
    stages = plan.stages(name) if plan is not None else []
    n_in, n_out, n_scr = len(args), len(out_shape), len(scratch)
    if not stages:
        return pl.pallas_call(body, grid=grid, in_specs=list(in_specs), out_specs=list(out_specs), out_shape=list(out_shape),
                              scratch_shapes=list(scratch), name=name,
                              compiler_params=pltpu.CompilerParams(dimension_semantics=sem, vmem_limit_bytes=VMEM_LIMIT))(*args)
    bufs, s_outs, aliases, s_scratch, run, split = _stage_plumbing(stages, n_in, n_out)
    nb, nso = len(bufs), len(s_outs)

    def hosted(*refs):
        ins = refs[:n_in]
        s_ins = refs[n_in:n_in + nb]
        outs = refs[n_in + nb:n_in + nb + n_out]
        s_os = refs[n_in + nb + n_out:n_in + nb + n_out + nso]
        scr = refs[n_in + nb + n_out + nso:n_in + nb + n_out + nso + n_scr]
        sems = refs[n_in + nb + n_out + nso + n_scr:]
        first = pl.program_id(0) == 0
        last = pl.program_id(0) == grid[0] - 1
        for d in range(1, len(grid)):
            first = first & (pl.program_id(d) == 0)
            last = last & (pl.program_id(d) == grid[d] - 1)

        @pl.when(first)
        def _():
            run("start", s_ins, s_os, sems)

        body(*ins, *outs, *scr)

        @pl.when(last)
        def _():
            run("wait", s_ins, s_os, sems)

    res = pl.pallas_call(
        hosted, grid=grid, in_specs=list(in_specs) + [ANY] * nb, out_specs=list(out_specs) + [ANY] * nso,
        out_shape=list(out_shape) + s_outs, input_output_aliases=aliases, scratch_shapes=list(scratch) + s_scratch, name=name,
        compiler_params=pltpu.CompilerParams(dimension_semantics=("arbitrary",) * len(grid), vmem_limit_bytes=VMEM_LIMIT))(*args, *bufs)
    plan.done(name, split(res[n_out:]))
    return list(res[:n_out])


def _gather_ici_stage(fulls):
    n = len(fulls)

    def copy(i, j, slot, ins, outs, send, recv):
        x, y, c = _place()
        chip = _other_chips(x, y)[j]
        h = fulls[i].shape[1] // 2
        s = 3 * i + j
        return pltpu.make_async_remote_copy(_half_rows(ins[i], 2 * x + y, c, h), _half_rows(outs[i], slot(x, y, chip), c, h),
                                            send.at[s], recv.at[s], device_id=(*chip, c), device_id_type=MESH)

    mine = lambda x, y, chip: 2 * x + y
    theirs = lambda x, y, chip: _chip_of(chip)

    def start(ins, outs, send, recv):
        for i in range(n):
            for j in range(3):
                copy(i, j, mine, ins, outs, send, recv).start()

    def wait(ins, outs, send, recv):
        for i in range(n):
            for j in range(3):
                copy(i, j, theirs, ins, outs, send, recv).wait_recv()
        for i in range(n):
            for j in range(3):
                copy(i, j, mine, ins, outs, send, recv).wait_send()

    return _Stage(fulls, [_sds(f.shape, f.dtype) for f in fulls], {i: i for i in range(n)}, 3 * n, start, wait)


def _gather_d2d_stage(fulls):
    n = len(fulls)

    def copy(i, j, cc, ins, outs, send, recv):
        x, y, c = _place()
        rj = _chip_of(_other_chips(x, y)[j])
        h = fulls[i].shape[1] // 2
        half = cc(c)
        s = 3 * i + j
        return pltpu.make_async_remote_copy(_half_rows(ins[i], rj, half, h), _half_rows(outs[i], rj, half, h),
                                            send.at[s], recv.at[s], device_id=(x, y, 1 - c), device_id_type=MESH)

    mine = lambda c: c
    theirs = lambda c: 1 - c

    def start(ins, outs, send, recv):
        for i in range(n):
            for j in range(3):
                copy(i, j, mine, ins, outs, send, recv).start()

    def wait(ins, outs, send, recv):
        for i in range(n):
            for j in range(3):
                copy(i, j, theirs, ins, outs, send, recv).wait_recv()
        for i in range(n):
            for j in range(3):
                copy(i, j, mine, ins, outs, send, recv).wait_send()

    return _Stage(fulls, [_sds(f.shape, f.dtype) for f in fulls], {i: i for i in range(n)}, 3 * n, start, wait)


def _split_stage(parts):
    n = len(parts)

    def copy(i, ins, outs, send, recv):
        x, y, c = _place()
        h = parts[i].shape[1] // 2
        return pltpu.make_async_remote_copy(_half_rows(ins[i], slice(None), 1 - c, h), outs[i], send.at[i], recv.at[i],
                                            device_id=(x, y, 1 - c), device_id_type=MESH)

    def start(ins, outs, send, recv):
        for i in range(n):
            copy(i, ins, outs, send, recv).start()

    def wait(ins, outs, send, recv):
        for i in range(n):
            copy(i, ins, outs, send, recv).wait_recv()
        for i in range(n):
            copy(i, ins, outs, send, recv).wait_send()

    return _Stage(parts, [_sds((N_CHIPS, p.shape[1] // 2, p.shape[2]), p.dtype) for p in parts], {}, n, start, wait)


def _owners_stage(sums):
    n = len(sums)

    def copy(i, j, mine, ins, outs, send, recv):
        x, y, c = _place()
        chip = _other_chips(x, y)[j]
        slot = (2 * x + y) if mine else _chip_of(chip)
        return pltpu.make_async_remote_copy(ins[i].at[_chip_of(chip)], outs[i].at[slot], send.at[3 * i + j], recv.at[3 * i + j],
                                            device_id=(*chip, c), device_id_type=MESH)

    def start(ins, outs, send, recv):
        for i in range(n):
            for j in range(3):
                copy(i, j, True, ins, outs, send, recv).start()

    def wait(ins, outs, send, recv):
        for i in range(n):
            for j in range(3):
                copy(i, j, False, ins, outs, send, recv).wait_recv()
        for i in range(n):
            for j in range(3):
                copy(i, j, True, ins, outs, send, recv).wait_send()

    return _Stage(sums, [_sds(s.shape, s.dtype) for s in sums], {}, 3 * n, start, wait)


def _join_stage(gs):
    n = len(gs)

    def copy(i, mine, ins, outs, send, recv):
        x, y, c = _place()
        slot = c if mine else 1 - c
        return pltpu.make_async_remote_copy(ins[i].at[slot], outs[i].at[slot], send.at[i], recv.at[i],
                                            device_id=(x, y, 1 - c), device_id_type=MESH)

    def start(ins, outs, send, recv):
        for i in range(n):
            copy(i, True, ins, outs, send, recv).start()

    def wait(ins, outs, send, recv):
        for i in range(n):
            copy(i, False, ins, outs, send, recv).wait_recv()
        for i in range(n):
            copy(i, True, ins, outs, send, recv).wait_send()

    return _Stage(gs, [_sds(g.shape, g.dtype) for g in gs], {i: i for i in range(n)}, n, start, wait)


def _chip_exchange(name, arr):
    def body(src, dst, send, recv, loc):
        x, y, c = _place()
        r = 2 * x + y
        chips = _other_chips(x, y)

        def cp(j, slot):
            return pltpu.make_async_remote_copy(src, dst.at[slot], send.at[j], recv.at[j], device_id=(*chips[j], c), device_id_type=MESH)

        mine = pltpu.make_async_copy(src, dst.at[r], loc)
        mine.start()
        for j in range(3):
            cp(j, r).start()
        for j in range(3):
            cp(j, _chip_of(chips[j])).wait_recv()
        for j in range(3):
            cp(j, r).wait_send()
        mine.wait()

    return pl.pallas_call(body, in_specs=[ANY], out_specs=ANY, out_shape=_sds((N_CHIPS, *arr.shape), arr.dtype),
                          scratch_shapes=[pltpu.SemaphoreType.DMA((3,)), pltpu.SemaphoreType.DMA((3,)), pltpu.SemaphoreType.DMA],
                          name=name)(arr)


def _gather_all(name, arr):
    def body(src, dst, send, recv, loc):
        x, y, c = _place()

        def cp(k, slot_of_me):
            px, py, pc = x ^ ((k >> 2) & 1), y ^ ((k >> 1) & 1), c ^ (k & 1)
            slot = (4 * x + 2 * y + c) if slot_of_me else (4 * px + 2 * py + pc)
            return pltpu.make_async_remote_copy(src, dst.at[slot], send.at[k - 1], recv.at[k - 1],
                                                device_id=(px, py, pc), device_id_type=MESH)

        mine = pltpu.make_async_copy(src, dst.at[4 * x + 2 * y + c], loc)
        mine.start()
        for k in range(1, N_DEV):
            cp(k, True).start()
        for k in range(1, N_DEV):
            cp(k, False).wait_recv()
        for k in range(1, N_DEV):
            cp(k, True).wait_send()
        mine.wait()

    return pl.pallas_call(body, in_specs=[ANY], out_specs=ANY, out_shape=_sds((N_DEV, *arr.shape), arr.dtype),
                          scratch_shapes=[pltpu.SemaphoreType.DMA((N_DEV - 1,)), pltpu.SemaphoreType.DMA((N_DEV - 1,)), pltpu.SemaphoreType.DMA],
                          name=name)(arr)


NN = (((1,), (0,)), ((), ()))
NT = (((1,), (1,)), ((), ()))
TN = (((0,), (0,)), ((), ()))


ALL = slice(None)


def _mm(name, grid, dims, a, a_spec, b, b_spec, extras, extra_specs, out_shapes, out_specs, acc_shape, epilogue, plan=None):
    n_k = grid[2]
    n_e = len(extras)
    n_o = len(out_shapes)

    def body(*refs):
        a_ref, b_ref = refs[0], refs[1]
        e_refs = refs[2:2 + n_e]
        o_refs = refs[2 + n_e:2 + n_e + n_o]
        p = lax.dot_general(a_ref[...].astype(BF), b_ref[...].astype(BF), dims, preferred_element_type=F32)
        if n_k == 1:
            epilogue(p, e_refs, o_refs, ALL)
        else:
            acc = refs[-1]
            k = pl.program_id(2)

            @pl.when(k == 0)
            def _():
                acc[...] = p

            @pl.when(k > 0)
            def _():
                acc[...] += p

            @pl.when(k == n_k - 1)
            def _():
                epilogue(acc[...], e_refs, o_refs, ALL)

    scratch = [] if n_k == 1 else [pltpu.VMEM(acc_shape, F32)]
    return _call(name, body, grid, [a_spec, b_spec, *extra_specs], out_specs, out_shapes, [a, b, *extras], scratch,
                 ("parallel", "parallel", "arbitrary"), plan)


def _store(p, e, o, rs):
    o[0][rs, :] = p.astype(o[0].dtype)


def _rms_mod_fwd(name, x, gain, shift, scale, plan=None):
    S, D = x.shape
    ts = _pick(S, (512,))

    def body(x_ref, g_ref, sh_ref, sc_ref, h_ref):
        xv = x_ref[...]
        r = lax.rsqrt(jnp.mean(xv * xv, axis=-1, keepdims=True) + EPS)
        n = xv * r * g_ref[...]
        h_ref[...] = (n * (1.0 + sc_ref[...]) + sh_ref[...]).astype(BF)

    row = pl.BlockSpec((ts, D), lambda i: (i, 0))
    vec = pl.BlockSpec((1, D), lambda i: (0, 0))
    return _call(name, body, (S // ts,), [row, vec, vec, vec], [row], [_sds((S, D), BF)], [x, gain, shift, scale],
                 sem=("parallel",), plan=plan)[0]


def _acc_rows(acc_ref, first, part):
    @pl.when(first)
    def _():
        acc_ref[...] = part

    @pl.when(jnp.logical_not(first))
    def _():
        acc_ref[...] += part


def _rms_mod_bwd(name, dh, x, dres, gain, scale, f=None, coef=None, plan=None):
    S, D = x.shape
    ts = _pick(S, (256,))
    gated = f is not None

    def body(dh_ref, x_ref, dr_ref, g_ref, sc_ref, *rest):
        xv = x_ref[...]
        dhv = dh_ref[...]
        g = g_ref[...]
        r = lax.rsqrt(jnp.mean(xv * xv, axis=-1, keepdims=True) + EPS)
        xhat = xv * r
        dn = dhv * (1.0 + sc_ref[...])
        dxhat = dn * g
        dx = dr_ref[...] + r * (dxhat - xhat * jnp.mean(dxhat * xhat, axis=-1, keepdims=True))
        rows = [jnp.sum(dhv, axis=0, keepdims=True), jnp.sum(dhv * (xhat * g), axis=0, keepdims=True),
                jnp.sum(dn * xhat, axis=0, keepdims=True)]
        if gated:
            f_ref, c_ref, dx_ref, df_ref, acc_ref = rest
            df_ref[...] = (dx * c_ref[...]).astype(BF)
            rows.append(jnp.sum(dx * f_ref[...].astype(F32), axis=0, keepdims=True))
        else:
            dx_ref, acc_ref = rest
        dx_ref[...] = dx
        _acc_rows(acc_ref, pl.program_id(0) == 0, jnp.concatenate(rows + [jnp.zeros((8 - len(rows), D), F32)], axis=0))

    row = pl.BlockSpec((ts, D), lambda i: (i, 0))
    vec = pl.BlockSpec((1, D), lambda i: (0, 0))
    acc = pl.BlockSpec((8, D), lambda i: (0, 0))
    if gated:
        return _call(name, body, (S // ts,), [row, row, row, vec, vec, row, vec], [row, row, acc],
                     [_sds((S, D), F32), _sds((S, D), BF), _sds((8, D), F32)], [dh, x, dres, gain, scale, f, coef],
                     sem=("arbitrary",), plan=plan)
    return _call(name, body, (S // ts,), [row, row, row, vec, vec], [row, acc],
                 [_sds((S, D), F32), _sds((8, D), F32)], [dh, x, dres, gain, scale], sem=("arbitrary",), plan=plan)


def _loss_bwd(x3, target, f, coef):
    S, D = x3.shape
    ts = _pick(S, (512,))

    def body(x_ref, t_ref, f_ref, c_ref, dx_ref, df_ref, acc_ref):
        e = x_ref[...] - t_ref[...]
        dx = e * (1.0 / D)
        dx_ref[...] = dx
        df_ref[...] = (dx * c_ref[...]).astype(BF)
        part = jnp.concatenate([jnp.sum(e * e, axis=0, keepdims=True), jnp.sum(dx * f_ref[...].astype(F32), axis=0, keepdims=True),
                                jnp.zeros((6, D), F32)], axis=0)
        _acc_rows(acc_ref, pl.program_id(0) == 0, part)

    row = pl.BlockSpec((ts, D), lambda i: (i, 0))
    return _call("loss_bwd", body, (S // ts,), [row, row, row, pl.BlockSpec((1, D), lambda i: (0, 0))],
                 [row, row, pl.BlockSpec((8, D), lambda i: (0, 0))],
                 [_sds((S, D), F32), _sds((S, D), BF), _sds((8, D), F32)], [x3, target, f, coef], sem=("arbitrary",))


def _silu_parts(g):
    s = jax.nn.sigmoid(g)
    return s, g * s


def _ffn_up(name, h, wgu4, plan=None):
    S, D = h.shape
    SH = wgu4.shape[2]
    F = 2 * SH
    tm = _pick(S, (512,))
    tn = _pick(SH, (1408, 256))
    nts = SH // tn

    def body(h_ref, wg_ref, wu_ref, gu_ref, act_ref):
        hv = h_ref[...]
        g = jnp.dot(hv, wg_ref[...], preferred_element_type=F32)
        u = jnp.dot(hv, wu_ref[...], preferred_element_type=F32)
        gu_ref[0] = g.astype(BF)
        gu_ref[1] = u.astype(BF)
        act_ref[...] = (_silu_parts(g)[1] * u).astype(BF)

    return _call(name, body, (S // tm, F // tn),
                 [pl.BlockSpec((tm, D), lambda i, j: (i, 0)),
                  pl.BlockSpec((None, D, tn), lambda i, j: (j // nts, 0, j % nts)),
                  pl.BlockSpec((None, D, tn), lambda i, j: (2 + j // nts, 0, j % nts))],
                 [pl.BlockSpec((2, tm, tn), lambda i, j: (0, i, j)), pl.BlockSpec((tm, tn), lambda i, j: (i, j))],
                 [_sds((2, S, F), BF), _sds((S, F), BF)], [h, wgu4, wgu4], sem=("parallel", "parallel"), plan=plan)


def _mm_residual(name, a, w, x_in, coef, plan=None):
    S, K = a.shape
    D = w.shape[1]
    tm = _pick(S, (1024,))
    tn = _pick(D, (512,))
    tk = K

    def epi(p, e, o, rs):
        o[0][rs, :] = e[0][rs, :] + e[1][...] * p
        o[1][rs, :] = p.astype(BF)

    tile = pl.BlockSpec((tm, tn), lambda i, j, k: (i, j))
    return _mm(name, (S // tm, D // tn, K // tk), NN,
               a, pl.BlockSpec((tm, tk), lambda i, j, k: (i, k)),
               w, pl.BlockSpec((tk, tn), lambda i, j, k: (k, j)),
               [x_in, coef], [tile, pl.BlockSpec((1, tn), lambda i, j, k: (0, j))],
               [_sds((S, D), F32), _sds((S, D), BF)], [tile, tile], (tm, tn), epi, plan)


def _ffn_dact(name, df, wd, gu, plan=None):
    S, D = df.shape
    F = wd.shape[0]
    tm = _pick(S, (512,))
    tn = _pick(F, (1408, 256))

    def epi(p, e, o, rs):
        g = e[0][0, rs, :].astype(F32)
        u = e[0][1, rs, :].astype(F32)
        s, sg = _silu_parts(g)
        o[0][0, rs, :] = (p * u * (s * (1.0 + g * (1.0 - s)))).astype(BF)
        o[0][1, rs, :] = (p * sg).astype(BF)

    pair = pl.BlockSpec((2, tm, tn), lambda i, j, k: (0, i, j))
    return _mm(name, (S // tm, F // tn, 1), NT,
               df, pl.BlockSpec((tm, D), lambda i, j, k: (i, 0)),
               wd, pl.BlockSpec((tn, D), lambda i, j, k: (j, 0)),
               [gu], [pair], [_sds((2, S, F), BF)], [pair], None, epi, plan)[0]


def _ffn_dh(name, dgu, wgu4, plan=None):
    _, S, F = dgu.shape
    _, D, SH = wgu4.shape
    tm = _pick(S, (1024,))
    tn = _pick(D, (1024,))
    tk = SH
    nkp = F // tk
    nks = SH // tk
    return _mm(name, (S // tm, D // tn, 2 * nkp), NT,
               dgu, pl.BlockSpec((None, tm, tk), lambda i, j, k: (k // nkp, i, k % nkp)),
               wgu4, pl.BlockSpec((None, tn, tk), lambda i, j, k: (k // nks, j, k % nks)),
               [], [], [_sds((S, D), F32)], [pl.BlockSpec((tm, tn), lambda i, j, k: (i, j))], (tm, tn), _store, plan)[0]


def _ffn_dwgu(name, h, dgu, plan=None):
    _, S, F = dgu.shape
    D = h.shape[1]
    SH = F // 2
    tk1 = _pick(D, (1024,))
    tn = _pick(SH, (1408, 256))
    ts = _pick(S, (2048, 1024))
    npj = F // tn
    nsj = SH // tn
    return _mm(name, (D // tk1, 2 * npj, S // ts), TN,
               h, pl.BlockSpec((ts, tk1), lambda i, j, k: (k, i)),
               dgu, pl.BlockSpec((None, ts, tn), lambda i, j, k: (j // npj, k, j % npj)),
               [], [], [_sds((4, D, SH), BF)],
               [pl.BlockSpec((None, tk1, tn), lambda i, j, k: (j // nsj, i, j % nsj))], (tk1, tn), _store, plan)[0]


def _mm_tn(name, a, b, tk1_prefs, tn_prefs, plan=None):
    S, K1 = a.shape
    N = b.shape[1]
    tk1 = _pick(K1, tk1_prefs)
    tn = _pick(N, tn_prefs)
    ts = _pick(S, (2048, 1024))
    return _mm(name, (K1 // tk1, N // tn, S // ts), TN,
               a, pl.BlockSpec((ts, tk1), lambda i, j, k: (k, i)),
               b, pl.BlockSpec((ts, tn), lambda i, j, k: (k, j)),
               [], [], [_sds((K1, N), BF)], [pl.BlockSpec((tk1, tn), lambda i, j, k: (i, j))], (tk1, tn), _store, plan)[0]


def _pool_window(ext, w, back):
    n = ext.shape[0]
    s = ext
    for step in (1, 2, 4, 8):
        sh = pltpu.roll(s, (n - step) if back else step, axis=0)
        s = jnp.where(w > step, s + sh, s)
    return s


def _pool_fwd(z, PW):
    S = z.shape[0]
    tc = _pick(S, (1024,))
    bpg = (PW // 4) // LANES
    H = POOL_MAX_W

    def body(prev_ref, u_ref, o_ref):
        i = pl.program_id(0)
        j = pl.program_id(1)
        w = lax.shift_left(jnp.int32(2), j // bpg)
        u = u_ref[...]
        prev = jnp.where(i > 0, prev_ref[...], 0.0)
        s = _pool_window(jnp.concatenate([prev, u], axis=0), w, False)[H:]
        t = i * tc + lax.broadcasted_iota(jnp.int32, (tc, LANES), 0)
        cnt = jnp.minimum(t + 1, w).astype(F32)
        o_ref[...] = (s / cnt - u).astype(BF)

    r = tc // H
    return _call("pool_fwd", body, (S // tc, PW // LANES),
                 [pl.BlockSpec((H, LANES), lambda i, j: (jnp.maximum(i * r - 1, 0), j)),
                  pl.BlockSpec((tc, LANES), lambda i, j: (i, j))],
                 [pl.BlockSpec((tc, LANES), lambda i, j: (i, j))], [_sds((S, PW), BF)], [z, z], sem=("parallel", "parallel"))[0]


def _pool_bwd(dpooled):
    S, PW = dpooled.shape
    tc = _pick(S, (1024,))
    bpg = (PW // 4) // LANES
    H = POOL_MAX_W
    last = S // tc - 1

    def body(dp_ref, nxt_ref, o_ref):
        i = pl.program_id(0)
        j = pl.program_id(1)
        w = lax.shift_left(jnp.int32(2), j // bpg)
        dp = dp_ref[...]
        nxt = jnp.where(i < last, nxt_ref[...], 0.0)
        ext = jnp.concatenate([dp, nxt], axis=0)
        t = i * tc + lax.broadcasted_iota(jnp.int32, (tc + H, LANES), 0)
        cnt = jnp.minimum(t + 1, w).astype(F32)
        s = _pool_window(ext / cnt, w, True)[:tc]
        o_ref[...] = (s - dp).astype(BF)

    r = tc // H
    nh = S // H - 1
    return _call("pool_bwd", body, (S // tc, PW // LANES),
                 [pl.BlockSpec((tc, LANES), lambda i, j: (i, j)),
                  pl.BlockSpec((H, LANES), lambda i, j: (jnp.minimum((i + 1) * r, nh), j))],
                 [pl.BlockSpec((tc, LANES), lambda i, j: (i, j))], [_sds((S, PW), BF)], [dpooled, dpooled],
                 sem=("parallel", "parallel"))[0]


def _pool_mix(pooled, pm, scale):
    S, PW = pooled.shape
    gw = PW // 4
    ts = _pick(S, (1024,))

    def epi(p, e, o, rs):
        o[0][rs, :] = (p * e[0][...]).astype(BF)

    tile = pl.BlockSpec((ts, gw), lambda i, j, k: (i, j))
    return _mm("pool_mix", (S // ts, 4, 1), NN, pooled, tile,
               pm, pl.BlockSpec((None, gw, gw), lambda i, j, k: (j, 0, 0)),
               [scale], [pl.BlockSpec((1, gw), lambda i, j, k: (0, j))], [_sds((S, PW), BF)], [tile], None, epi)[0]


def _pool_mix_bwd(pooled, pm, scale, dmixed):
    S, PW = pooled.shape
    gw = PW // 4
    ts = _pick(S, (1024,))

    def body(p_ref, pm_ref, sc_ref, dm_ref, dp_ref, dpm_ref, dsc_ref):
        i = pl.program_id(1)
        p = p_ref[...]
        w = pm_ref[...]
        dm = dm_ref[...]
        pre = jnp.dot(p, w, preferred_element_type=F32)
        dmp = (dm * sc_ref[...]).astype(BF)
        dp_ref[...] = lax.dot_general(dmp, w, NT, preferred_element_type=F32)
        dw = lax.dot_general(p, dmp, TN, preferred_element_type=F32)
        ds = jnp.concatenate([jnp.sum(dm * pre, axis=0, keepdims=True), jnp.zeros((7, gw), F32)], axis=0)
        _acc_rows(dpm_ref, i == 0, dw)
        _acc_rows(dsc_ref, i == 0, ds)

    tile = pl.BlockSpec((ts, gw), lambda g, i: (i, g))
    return _call("pool_mix_bwd", body, (4, S // ts),
                 [tile, pl.BlockSpec((None, gw, gw), lambda g, i: (g, 0, 0)), pl.BlockSpec((1, gw), lambda g, i: (0, g)), tile],
                 [tile, pl.BlockSpec((None, gw, gw), lambda g, i: (g, 0, 0)), pl.BlockSpec((8, gw), lambda g, i: (0, g))],
                 [_sds((S, PW), F32), _sds((4, gw, gw), F32), _sds((8, PW), F32)], [pooled, pm, scale, dmixed],
                 sem=("parallel", "arbitrary"))


def _bucket_onehot():
    ql = np.arange(BLK)[:, None]
    j = np.arange(2 * BLK)[None, :]
    d = BLK + ql - j
    n = np.clip(d, 0, None)
    nf = np.maximum(n, 1).astype(np.float32)
    max_exact = NUM_BUCKETS // 2
    large = max_exact + (np.log(nf / max_exact) / np.log(BLK / max_exact) * (NUM_BUCKETS - max_exact)).astype(np.int32)
    large = np.minimum(large, NUM_BUCKETS - 1)
    bucket = np.where(n < max_exact, n, large).astype(np.int32)
    valid = (d >= 0) & (d < BLK)
    oh = (bucket[None] == np.arange(NUM_BUCKETS)[:, None, None]) & valid[None]
    return oh.reshape(NUM_BUCKETS, BLK * 2 * BLK)


def _three_bf16(v):
    hi = v.astype(BF)
    r1 = v - hi.astype(F32)
    mid = r1.astype(BF)
    lo = (r1 - mid.astype(F32)).astype(BF)
    return hi, mid, lo


def _bias_table(rel_bias):
    oh = jnp.asarray(_bucket_onehot(), BF)
    tn = 4096

    def body(rb_ref, oh_ref, o_ref):
        o = oh_ref[...]
        hi, mid, lo = _three_bf16(rb_ref[...])
        acc = lax.dot_general(hi, o, TN, preferred_element_type=F32)
        acc = acc + lax.dot_general(mid, o, TN, preferred_element_type=F32)
        acc = acc + lax.dot_general(lo, o, TN, preferred_element_type=F32)
        on_band = jnp.sum(o.astype(F32), axis=0, keepdims=True) > 0.5
        o_ref[...] = jnp.where(on_band, acc, NEG_INF)

    n = oh.shape[1]
    return _call("bias_table", body, (n // tn,),
                 [pl.BlockSpec((NUM_BUCKETS, N_HEADS), lambda i: (0, 0)), pl.BlockSpec((NUM_BUCKETS, tn), lambda i: (0, i))],
                 [pl.BlockSpec((N_HEADS, tn), lambda i: (0, i))], [_sds((N_HEADS, n), F32)], [rel_bias, oh], sem=("parallel",))[0]


def _rel_bias_grad(dl):
    oh = jnp.asarray(_bucket_onehot(), BF)
    n = oh.shape[1]
    tk = 4096

    def body(dl_ref, oh_ref, o_ref):
        o = oh_ref[...]
        hi, mid, lo = _three_bf16(dl_ref[...])
        acc = lax.dot_general(o, hi, NT, preferred_element_type=F32)
        acc = acc + lax.dot_general(o, mid, NT, preferred_element_type=F32)
        acc = acc + lax.dot_general(o, lo, NT, preferred_element_type=F32)
        _acc_rows(o_ref, pl.program_id(0) == 0, acc)

    return _call("rel_bias_grad", body, (n // tk,),
                 [pl.BlockSpec((N_HEADS, tk), lambda i: (0, i)), pl.BlockSpec((NUM_BUCKETS, tk), lambda i: (0, i))],
                 [pl.BlockSpec((NUM_BUCKETS, N_HEADS), lambda i: (0, 0))], [_sds((NUM_BUCKETS, N_HEADS), F32)], [dl, oh],
                 sem=("arbitrary",))[0]


def _lo_half(shape):
    return lax.broadcasted_iota(jnp.int32, shape, 1) < HEAD_DIM


def _half_sum(x, lo):
    s_lo = jnp.sum(jnp.where(lo, x, 0.0), axis=-1, keepdims=True)
    s_hi = jnp.sum(jnp.where(lo, 0.0, x), axis=-1, keepdims=True)
    return jnp.where(lo, s_lo, s_hi)


def _norm2(x, lo):
    r = lax.rsqrt(_half_sum(x * x, lo) * (1.0 / HEAD_DIM) + EPS)
    return x * r, r


def _norm2_bwd(dy, xhat, r, gain, lo):
    dxhat = dy * gain
    dx = r * (dxhat - xhat * (_half_sum(dxhat * xhat, lo) * (1.0 / HEAD_DIM)))
    return dx, dy * xhat


def _swap(x):
    return pltpu.roll(x, HEAD_DIM, axis=1)


def _attn_logits(n, kk, zq_ref, kn, qg, bias_ref, sink_ref, lo_k):
    lo_q = _lo_half((BLK, LANES))
    half_k = lo_k if kk == 0 else jnp.logical_not(lo_k)
    K = jnp.where(half_k, kn, 0.0).astype(BF)
    rows, qhats, qrs = [], [], []
    for jp in range(4):
        xq = zq_ref[:, jp * LANES:(jp + 1) * LANES]
        qhat, qr = _norm2(xq, lo_q)
        qn = qhat * qg * (HEAD_DIM ** -0.5)
        qs = _swap(qn)
        rows += [qn, qs] if kk == 0 else [qs, qn]
        qhats.append(qhat)
        qrs.append(qr)
    Q = jnp.concatenate(rows, axis=0).astype(BF)
    kq = lax.dot_general(K, Q, NT, preferred_element_type=F32)
    l = kq + bias_ref[kk]
    l = jnp.concatenate([jnp.where(n == 0, NEG_INF, l[:BLK]), l[BLK:]], axis=0)
    sink = sink_ref[kk]
    m = jnp.maximum(jnp.max(l, axis=0, keepdims=True), sink)
    e = jnp.exp(l - m)
    es = jnp.exp(sink - m)
    inv = 1.0 / (jnp.sum(e, axis=0, keepdims=True) + es)
    return Q, K, e * inv, es * inv, qhats, qrs


def _attn_specs(o_q, o_k):
    nq = o_q // 512
    nk = o_k // LANES
    prev = lambda n: (jnp.maximum(n - 1, 0), nk)
    prev_v = lambda n: (jnp.maximum(n - 1, 0), nk + 1)
    return [pl.BlockSpec((BLK, 512), lambda n: (n, nq)), pl.BlockSpec((BLK, 512), lambda n: (n, nq + 1)),
            pl.BlockSpec((BLK, LANES), prev), pl.BlockSpec((BLK, LANES), lambda n: (n, nk)),
            pl.BlockSpec((BLK, LANES), prev_v), pl.BlockSpec((BLK, LANES), lambda n: (n, nk + 1)),
            pl.BlockSpec((1, LANES), lambda n: (0, 0)), pl.BlockSpec((1, LANES), lambda n: (0, 0)),
            pl.BlockSpec((N_KV, 1, 8 * BLK), lambda n: (0, 0, 0)),
            pl.BlockSpec((N_KV, 2 * BLK, 8 * BLK), lambda n: (0, 0, 0))]


def _attn_fwd(z, o_q, o_k, qg2, kg2, sink_rows, bias, plan=None):
    S = z.shape[0]

    def body(zq0, zq1, zkp, zkc, zvp, zvc, qg_ref, kg_ref, sink_ref, bias_ref, o_ref):
        n = pl.program_id(0)
        lo_k = _lo_half((2 * BLK, LANES))
        lo_q = _lo_half((BLK, LANES))
        khat, _ = _norm2(jnp.concatenate([zkp[...], zkc[...]], axis=0), lo_k)
        kn = khat * kg_ref[...]
        vb = jnp.concatenate([zvp[...], zvc[...]], axis=0).astype(BF)
        for kk, zq in enumerate((zq0, zq1)):
            _, _, p, _, _, _ = _attn_logits(n, kk, zq, kn, qg_ref[...], bias_ref, sink_ref, lo_k)
            r = lax.dot_general(p.astype(BF), vb, TN, preferred_element_type=F32)
            for jp in range(4):
                ev = r[(2 * jp) * BLK:(2 * jp + 1) * BLK]
                od = r[(2 * jp + 1) * BLK:(2 * jp + 2) * BLK]
                pair = jnp.where(lo_q, ev, _swap(od)) if kk == 0 else jnp.where(lo_q, _swap(ev), od)
                c0 = (4 * kk + jp) * LANES
                o_ref[:, c0:c0 + LANES] = pair.astype(BF)

    return _call("attn_fwd", body, (S // BLK,), _attn_specs(o_q, o_k), [pl.BlockSpec((BLK, ATT_W), lambda n: (n, 0))],
                 [_sds((S, ATT_W), BF)], [z, z, z, z, z, z, qg2, kg2, sink_rows, bias], sem=("parallel",), plan=plan)[0]


def _attn_bwd(z, o_q, o_k, qg2, kg2, sink_rows, bias, dout, plan=None):
    S = z.shape[0]

    def body(zq0, zq1, zkp, zkc, zvp, zvc, qg_ref, kg_ref, sink_ref, bias_ref, do_ref,
             dq_ref, dkp_ref, dkc_ref, dvp_ref, dvc_ref, dl_ref, dsink_ref, dgain_ref):
        n = pl.program_id(0)
        lo_k = _lo_half((2 * BLK, LANES))
        lo_q = _lo_half((BLK, LANES))
        qg = qg_ref[...]
        kg = kg_ref[...]
        khat, kr = _norm2(jnp.concatenate([zkp[...], zkc[...]], axis=0), lo_k)
        kn = khat * kg
        vf = jnp.concatenate([zvp[...], zvc[...]], axis=0)

        @pl.when(n == 0)
        def _():
            dl_ref[...] = jnp.zeros_like(dl_ref)
            dsink_ref[...] = jnp.zeros_like(dsink_ref)
            dgain_ref[...] = jnp.zeros_like(dgain_ref)

        dkn = jnp.zeros((2 * BLK, LANES), F32)
        dvb = jnp.zeros((2 * BLK, LANES), F32)
        dqg = jnp.zeros((1, LANES), F32)
        for kk, zq in enumerate((zq0, zq1)):
            half_k = lo_k if kk == 0 else jnp.logical_not(lo_k)
            Q, K, p, ps, qhats, qrs = _attn_logits(n, kk, zq, kn, qg, bias_ref, sink_ref, lo_k)
            rows = []
            for jp in range(4):
                c0 = (4 * kk + jp) * LANES
                x = do_ref[:, c0:c0 + LANES]
                rows += [x, _swap(x)] if kk == 0 else [_swap(x), x]
            dO = jnp.concatenate(rows, axis=0).astype(BF)
            V = jnp.where(half_k, vf, 0.0).astype(BF)
            dP = lax.dot_general(V, dO, NT, preferred_element_type=F32)
            delta = jnp.sum(p * dP, axis=0, keepdims=True)
            dS = p * (dP - delta)
            dsink_ref[kk] += -ps * delta
            dl_ref[kk] += dS
            dSb = dS.astype(BF)
            dvb = dvb + jnp.where(half_k, jnp.dot(p.astype(BF), dO, preferred_element_type=F32), 0.0)
            dkn = dkn + jnp.where(half_k, jnp.dot(dSb, Q, preferred_element_type=F32), 0.0)
            dQ = lax.dot_general(dSb, K, TN, preferred_element_type=F32) * (HEAD_DIM ** -0.5)
            for jp in range(4):
                ev = dQ[(2 * jp) * BLK:(2 * jp + 1) * BLK]
                od = dQ[(2 * jp + 1) * BLK:(2 * jp + 2) * BLK]
                dy = (ev + _swap(od)) if kk == 0 else (_swap(ev) + od)
                dx, gq = _norm2_bwd(dy, qhats[jp], qrs[jp], qg, lo_q)
                dqg = dqg + jnp.sum(gq, axis=0, keepdims=True)
                c0 = (4 * kk + jp) * LANES
                dq_ref[:, c0:c0 + LANES] = dx.astype(BF)
        dk, gk = _norm2_bwd(dkn, khat, kr, kg, lo_k)
        dkp_ref[...] = dk[:BLK]
        dkc_ref[...] = dk[BLK:]
        dvp_ref[...] = dvb[:BLK]
        dvc_ref[...] = dvb[BLK:]
        dgain_ref[...] += jnp.concatenate([dqg, jnp.sum(gk, axis=0, keepdims=True), jnp.zeros((6, LANES), F32)], axis=0)

    blk = pl.BlockSpec((BLK, LANES), lambda n: (n, 0))
    wide = pl.BlockSpec((BLK, ATT_W), lambda n: (n, 0))
    return _call(
        "attn_bwd", body, (S // BLK,), _attn_specs(o_q, o_k) + [wide],
        [wide, blk, blk, blk, blk, pl.BlockSpec((N_KV, 2 * BLK, 8 * BLK), lambda n: (0, 0, 0)),
         pl.BlockSpec((N_KV, 1, 8 * BLK), lambda n: (0, 0, 0)), pl.BlockSpec((8, LANES), lambda n: (0, 0))],
        [_sds((S, ATT_W), BF), _sds((S, LANES), F32), _sds((S, LANES), F32), _sds((S, LANES), F32), _sds((S, LANES), F32),
         _sds((N_KV, 2 * BLK, 8 * BLK), F32), _sds((N_KV, 1, 8 * BLK), F32), _sds((8, LANES), F32)],
        [z, z, z, z, z, z, qg2, kg2, sink_rows, bias, dout], sem=("arbitrary",), plan=plan)


def _kv_combine(dkp, dkc, dvp, dvc):
    S = dkc.shape[0]
    last = S // BLK - 1

    def body(kp_ref, kc_ref, vp_ref, vc_ref, dk_ref, dv_ref):
        more = pl.program_id(0) < last
        dk_ref[...] = (kc_ref[...] + jnp.where(more, kp_ref[...], 0.0)).astype(BF)
        dv_ref[...] = (vc_ref[...] + jnp.where(more, vp_ref[...], 0.0)).astype(BF)

    cur = pl.BlockSpec((BLK, LANES), lambda n: (n, 0))
    nxt = pl.BlockSpec((BLK, LANES), lambda n: (jnp.minimum(n + 1, last), 0))
    return _call("kv_combine", body, (S // BLK,), [nxt, cur, nxt, cur], [cur, cur],
                 [_sds((S, LANES), BF), _sds((S, LANES), BF)], [dkp, dkc, dvp, dvc], sem=("parallel",))


def _merge_fwd(mixed, attn, wpu4, wau4, z, o_ga, plan=None):
    S, PW = mixed.shape
    _, _, CS = wpu4.shape
    D = 4 * CS
    tm = _pick(S, (1024,))
    tn = 256
    nsj = CS // tn
    na = o_ga // tn
    nb = (o_ga + D) // tn

    def body(m_ref, a_ref, wp_ref, wa_ref, ga_ref, gb_ref, mg_ref, yy_ref):
        yp = jnp.dot(m_ref[...], wp_ref[...], preferred_element_type=F32)
        ya = jnp.dot(a_ref[...], wa_ref[...], preferred_element_type=F32)
        mg_ref[...] = (jax.nn.sigmoid(ga_ref[...]) * yp + jax.nn.sigmoid(gb_ref[...]) * ya).astype(BF)
        yy_ref[0] = yp.astype(BF)
        yy_ref[1] = ya.astype(BF)

    return _call("merge_fwd", body, (S // tm, D // tn),
                 [pl.BlockSpec((tm, PW), lambda i, j: (i, 0)), pl.BlockSpec((tm, ATT_W), lambda i, j: (i, 0)),
                  pl.BlockSpec((None, PW, tn), lambda i, j: (j // nsj, 0, j % nsj)),
                  pl.BlockSpec((None, ATT_W, tn), lambda i, j: (j // nsj, 0, j % nsj)),
                  pl.BlockSpec((tm, tn), lambda i, j: (i, na + j)), pl.BlockSpec((tm, tn), lambda i, j: (i, nb + j))],
                 [pl.BlockSpec((tm, tn), lambda i, j: (i, j)), pl.BlockSpec((2, tm, tn), lambda i, j: (0, i, j))],
                 [_sds((S, D), BF), _sds((2, S, D), BF)], [mixed, attn, wpu4, wau4, z, z], sem=("parallel", "parallel"), plan=plan)


def _merge_bwd(do, wo, z, o_ga, yy, plan=None):
    S, D = do.shape
    tm = _pick(S, (1024,))
    tn = 256
    na = o_ga // tn
    nb = (o_ga + D) // tn

    def epi(p, e, o, rs):
        sa = jax.nn.sigmoid(e[0][rs, :])
        sb = jax.nn.sigmoid(e[1][rs, :])
        yp = e[2][0, rs, :].astype(F32)
        ya = e[2][1, rs, :].astype(F32)
        o[0][0, rs, :] = (p * yp * sa * (1.0 - sa)).astype(BF)
        o[0][1, rs, :] = (p * ya * sb * (1.0 - sb)).astype(BF)
        o[1][0, rs, :] = (p * sa).astype(BF)
        o[1][1, rs, :] = (p * sb).astype(BF)

    pair = pl.BlockSpec((2, tm, tn), lambda i, j, k: (0, i, j))
    return _mm("merge_bwd", (S // tm, D // tn, 1), NT,
               do, pl.BlockSpec((tm, D), lambda i, j, k: (i, 0)),
               wo, pl.BlockSpec((tn, D), lambda i, j, k: (j, 0)),
               [z, z, yy], [pl.BlockSpec((tm, tn), lambda i, j, k: (i, na + j)), pl.BlockSpec((tm, tn), lambda i, j, k: (i, nb + j)), pair],
               [_sds((2, S, D), BF), _sds((2, S, D), BF)], [pair, pair], None, epi, plan)


def _mm_up_t(name, dyy, which, w4):
    _, S, D = dyy.shape
    _, K, CS = w4.shape
    tm = _pick(S, (1024,))
    return _mm(name, (S // tm, 1, N_CHIPS), NT,
               dyy, pl.BlockSpec((None, tm, CS), lambda i, j, k: (which, i, k)),
               w4, pl.BlockSpec((None, K, CS), lambda i, j, k: (k, 0, 0)),
               [], [], [_sds((S, K), F32)], [pl.BlockSpec((tm, K), lambda i, j, k: (i, 0))], (tm, K), _store)[0]


def _mm_up_dw(name, a, dyy, which):
    _, S, D = dyy.shape
    K = a.shape[1]
    CS = D // N_CHIPS
    ts = _pick(S, (1024,))
    return _mm(name, (1, N_CHIPS, S // ts), TN,
               a, pl.BlockSpec((ts, K), lambda i, j, k: (k, 0)),
               dyy, pl.BlockSpec((None, ts, CS), lambda i, j, k: (which, k, j)),
               [], [], [_sds((N_CHIPS, K, CS), BF)], [pl.BlockSpec((None, K, CS), lambda i, j, k: (j, 0, 0))], (K, CS), _store)[0]


def _adamw(w, g, m, v):
    m = ADAM_B1 * m + (1.0 - ADAM_B1) * g
    v = ADAM_B2 * v + (1.0 - ADAM_B2) * (g * g)
    m_hat = m / (1.0 - ADAM_B1 ** ADAM_STEP)
    v_hat = v / (1.0 - ADAM_B2 ** ADAM_STEP)
    delta = -ADAM_LR * (m_hat / (jnp.sqrt(v_hat) + ADAM_EPS) + ADAM_WD * w)
    return delta, m, v


def _mod_fwd(c_all, w_ada, b_sh):
    D, cols = w_ada.shape
    tn = cols // 9

    def body(c_ref, w_ref, b_ref, o_ref):
        cv = c_ref[...]
        sc = (cv * jax.nn.sigmoid(cv)).astype(BF)
        o_ref[...] = jnp.dot(sc, w_ref[...].astype(BF), preferred_element_type=F32) + b_ref[...]

    return _call("mod_fwd", body, (9,),
                 [pl.BlockSpec((N_DEV, D), lambda j: (0, 0)), pl.BlockSpec((D, tn), lambda j: (0, j)), pl.BlockSpec((1, tn), lambda j: (0, j))],
                 [pl.BlockSpec((N_DEV, tn), lambda j: (0, j))], [_sds((N_DEV, cols), F32)], [c_all, w_ada, b_sh], sem=("parallel",))[0]


def _wada_bwd(c_all, dmod_sh, w, m, v, plan=None):
    D, cols = w.shape
    tn = cols // 18

    def body(c_ref, d_ref, w_ref, m_ref, v_ref, g_ref, dl_ref, nm_ref, nv_ref):
        cv = c_ref[...]
        sc = (cv * jax.nn.sigmoid(cv)).astype(BF)
        g = lax.dot_general(sc, d_ref[...].astype(BF), TN, preferred_element_type=F32)
        g_ref[...] = g
        dl_ref[...], nm_ref[...], nv_ref[...] = _adamw(w_ref[...], g, m_ref[...], v_ref[...])

    tile = pl.BlockSpec((D, tn), lambda j: (0, j))
    out = _sds((D, cols), F32)
    return _call("wada_bwd", body, (18,),
                 [pl.BlockSpec((N_DEV, D), lambda j: (0, 0)), pl.BlockSpec((N_DEV, tn), lambda j: (0, j)), tile, tile, tile],
                 [tile] * 4, [out] * 4, [c_all, dmod_sh, w, m, v], sem=("parallel",), plan=plan)


def _adam_2d(name, w, g, m, v):
    R, C = w.shape
    tr = _row_tile(R, 256)

    def body(w_ref, g_ref, m_ref, v_ref, dl_ref, nm_ref, nv_ref):
        dl_ref[...], nm_ref[...], nv_ref[...] = _adamw(w_ref[...], g_ref[...], m_ref[...], v_ref[...])

    tile = pl.BlockSpec((tr, C), lambda i: (i, 0))
    out = _sds((R, C), F32)
    return _call(name, body, (R // tr,), [tile] * 4, [tile] * 3, [out] * 3, [w, g, m, v], sem=("parallel",))


def _small_finish(parts, w, m, v):
    _, R, C = parts.shape

    def body(p_ref, w_ref, m_ref, v_ref, g_ref, dl_ref, nm_ref, nv_ref):
        g = p_ref[0]
        for d in range(1, N_DEV):
            g = g + p_ref[d]
        g_ref[...] = g
        dl_ref[...], nm_ref[...], nv_ref[...] = _adamw(w_ref[...], g, m_ref[...], v_ref[...])

    out = _sds((R, C), F32)
    return pl.pallas_call(body, out_shape=[out] * 4, name="small_finish",
                          compiler_params=pltpu.CompilerParams(vmem_limit_bytes=VMEM_LIMIT))(parts, w, m, v)


def _my_chip():
    return 2 * lax.axis_index("x") + lax.axis_index("y")


def _cast_into_slot(name, w):
    R, C = w.shape
    tr = _row_tile(R, 256)

    def body(w_ref, o_ref):
        o_ref[...] = w_ref[...].astype(BF)

    return _call(name, body, (R // tr,), [pl.BlockSpec((tr, C), lambda i: (i, 0))],
                 [pl.BlockSpec((None, tr, C), lambda i: (_my_chip(), i, 0))], [_sds((N_CHIPS, R, C), BF)], [w], sem=("parallel",))[0]


def _add_pair(name, p, q):
    _, H, C = q.shape
    tr = _row_tile(H, 512)
    nt = H // tr

    def body(p_ref, q_ref, o_ref):
        o_ref[...] = (p_ref[...].astype(F32) + q_ref[...].astype(F32)).astype(BF)

    tile = pl.BlockSpec((None, tr, C), lambda k, i: (k, i, 0))
    return _call(name, body, (N_CHIPS, nt), [pl.BlockSpec((None, tr, C), lambda k, i: (k, lax.axis_index("c") * nt + i, 0)), tile],
                 [tile], [_sds(q.shape, BF)], [p, q], sem=("parallel", "parallel"))[0]


def _sum_chips(name, u, t):
    _, H, C = u.shape
    tr = _row_tile(H, 256)

    def body(u_ref, t_ref, o_ref):
        r = _my_chip()
        own = t_ref[...].astype(F32)
        pick = lambda k: jnp.where(r == k, own, u_ref[k].astype(F32))
        o_ref[...] = ((pick(0) + pick(1)) + pick(2)) + pick(3)

    return _call(name, body, (H // tr,),
                 [pl.BlockSpec((N_CHIPS, tr, C), lambda i: (0, i, 0)), pl.BlockSpec((None, tr, C), lambda i: (_my_chip(), i, 0))],
                 [pl.BlockSpec((None, tr, C), lambda i: (lax.axis_index("c"), i, 0))], [_sds((2, H, C), F32)], [u, t],
                 sem=("parallel",))[0]


BIG = ("gu1", "down1", "w_in", "pool_mix", "pool_up", "attn_up", "o", "gu2", "down2")
MIX = ("o", "pool_up", "attn_up", "pool_mix")
EARLY = ("down1", "w_in", "pool_mix", "pool_up", "attn_up", "o")

SCHEDULE = {
    "+gather_gu1_ici": ([("ici", ("gu1",))], []),
    "+gather_gu1_d2d": ([("d2d", ("gu1",))], []),
    "ffn1_up": ([("ici", EARLY)], []),
    "+gather_early_d2d": ([("d2d", EARLY)], []),
    "ffn1_down": ([("ici", ("gu2",))], []),
    "mix_in": ([("ici", ("down2",)), ("d2d", ("gu2",))], []),
    "mix_out": ([("d2d", ("down2",))], []),
    "ffn2_dwd": ([("split", ("gu2",))], [("add", ("gu2",))]),
    "ffn2_dh": ([("owners", ("gu2",)), ("split", ("down2",))], [("add", ("down2",)), ("sum", ("gu2",))]),
    "merge_bwd": ([("owners", ("down2",)), ("join", ("gu2",))], [("sum", ("down2",)), ("adam", ("gu2",))]),
    "mix_dwo": ([("join", ("down2",))], [("adam", ("down2",))]),
    "attn_bwd": ([("split", MIX)], [("add", MIX)]),
    "mix_dwin": ([("owners", MIX)], [("sum", MIX)]),
    "mix_dh": ([("split", ("w_in",)), ("join", MIX)], [("add", ("w_in",)), ("adam", MIX)]),
    "ffn1_dact": ([("owners", ("w_in",))], [("sum", ("w_in",))]),
    "ffn1_dwgu": ([("join", ("w_in",))], [("adam", ("w_in",))]),
    "ffn1_dwd": ([("split", ("gu1",))], [("add", ("gu1",))]),
    "ffn1_dh": ([("owners", ("gu1",)), ("split", ("down1",))], [("add", ("down1",)), ("sum", ("gu1",))]),
    "rms_mod_bwd1": ([("owners", ("down1",)), ("join", ("gu1",))], [("sum", ("down1",)), ("adam", ("gu1",))]),
    "+join_down1": ([("join", ("down1",))], [("adam", ("down1",))]),
}


class _Plan:
    def __init__(self, w2, m2, v2, full, D, gw):
        self.w2, self.m2, self.v2, self.full, self.D, self.gw = w2, m2, v2, dict(full), D, gw
        self.part, self.got, self.sums, self.landed, self.g = {}, {}, {}, {}, {}
        self.result = {}
        self.pending = {}

    def _make(self, op, names):
        if op == "ici":
            return _gather_ici_stage([self.full[k] for k in names])
        if op == "d2d":
            return _gather_d2d_stage([self.full[k] for k in names])
        if op == "split":
            return _split_stage([self.part[k] for k in names])
        if op == "owners":
            return _owners_stage([self.sums[k] for k in names])
        return _join_stage([self.g[k] for k in names])

    def stages(self, name):
        ops = SCHEDULE.get(name, ([], []))[0]
        return [self._make(*op) for op in ops]

    def done(self, name, outs):
        ops, local = SCHEDULE[name]
        for op, res in zip(ops, outs):
            store = {"ici": self.full, "d2d": self.full, "split": self.got, "owners": self.landed, "join": self.g}[op[0]]
            store.update(zip(op[1], res))
        for op, names in local:
            for k in names:
                if op == "add":
                    self.sums[k] = _add_pair("add_pair_" + k, self.part[k], self.got[k])
                elif op == "sum":
                    self.g[k] = _sum_chips("sum_chips_" + k, self.landed[k], self.sums[k])
                else:
                    g2 = self.g[k].reshape(self.w2[k].shape)
                    self.result[k] = (g2, *_adam_2d("adam_" + k, self.w2[k], g2, self.m2[k], self.v2[k]))

    def alone(self, name):
        self.done(name, _run_stages(name[1:], self.stages(name)))

    def weight(self, k):
        D, gw, f = self.D, self.gw, self.full[k]
        if k in ("down1", "down2", "o"):
            return f.reshape(-1, D)
        if k == "pool_mix":
            return f.reshape(N_CHIPS, 4, gw // N_CHIPS, gw).transpose(1, 0, 2, 3).reshape(4, gw, gw)
        if k == "w_in":
            return f.reshape(-1, D)
        return f

    def partial(self, k, p):
        D, gw = self.D, self.gw
        if k in ("down1", "down2", "o", "w_in"):
            p = p.reshape(N_CHIPS, p.shape[0] // N_CHIPS, p.shape[1])
        elif k == "pool_mix":
            p = p.astype(BF).reshape(4, N_CHIPS, gw // N_CHIPS, gw).transpose(1, 0, 2, 3).reshape(N_CHIPS, gw, gw)
        self.part[k] = p


class _NoComm:
    def __init__(self, weights):
        self.w, self.part = weights, {}

    def stages(self, name):
        return []

    def alone(self, name):
        pass

    def weight(self, k):
        return self.w[k]

    def partial(self, k, p):
        self.part[k] = p


def _row(a, i):
    return a[i:i + 1]


def _local_step(x, target, mod, g_ffn1, g_mix, g_ffn2, pool_scale, q_gain, k_gain, sinks, rel_bias, plan):
    S, D = x.shape
    half = 0.5 * mod
    tile2 = lambda g: jnp.concatenate([g, g], axis=1)
    qg2, kg2 = tile2(q_gain), tile2(k_gain)
    sink_rows = jnp.broadcast_to(sinks.reshape(N_KV, 1, 8, 1), (N_KV, 1, 8, BLK)).reshape(N_KV, 1, 8 * BLK)
    bias = _bias_table(rel_bias).reshape(N_KV, 8, BLK, 2 * BLK).transpose(0, 3, 1, 2).reshape(N_KV, 2 * BLK, 8 * BLK)

    plan.alone("+gather_gu1_ici")
    plan.alone("+gather_gu1_d2d")
    h1 = _rms_mod_fwd("rms_mod_fwd1", x, g_ffn1, _row(mod, 0), _row(mod, 1))
    gu1, act1 = _ffn_up("ffn1_up", h1, plan.weight("gu1"), plan)
    plan.alone("+gather_early_d2d")
    x1, f1 = _mm_residual("ffn1_down", act1, plan.weight("down1"), x, _row(half, 2), plan)
    h2 = _rms_mod_fwd("rms_mod_fwd2", x1, g_mix, _row(mod, 3), _row(mod, 4))
    w_in_t = plan.weight("w_in")
    IN_W = w_in_t.shape[0]
    PW = plan.weight("pool_up").shape[1]
    o_q, o_k = PW, PW + ATT_W
    o_ga = o_k + 2 * KV_W
    tnz = _pick(IN_W, (1280, 256))
    tmz = _pick(S, (1024,))
    z = _mm("mix_in", (S // tmz, IN_W // tnz, 1), NT, h2, pl.BlockSpec((tmz, D), lambda i, j, k: (i, 0)),
            w_in_t, pl.BlockSpec((tnz, D), lambda i, j, k: (j, 0)), [], [], [_sds((S, IN_W), F32)],
            [pl.BlockSpec((tmz, tnz), lambda i, j, k: (i, j))], None, _store, plan)[0]
    pooled = _pool_fwd(z, PW)
    mixed = _pool_mix(pooled, plan.weight("pool_mix"), pool_scale)
    attn = _attn_fwd(z, o_q, o_k, qg2, kg2, sink_rows, bias)
    merged, yy = _merge_fwd(mixed, attn, plan.weight("pool_up"), plan.weight("attn_up"), z, o_ga)
    x2, fo = _mm_residual("mix_out", merged, plan.weight("o"), x1, _row(mod, 5), plan)
    h3 = _rms_mod_fwd("rms_mod_fwd3", x2, g_ffn2, _row(mod, 6), _row(mod, 7))
    gu2, act2 = _ffn_up("ffn2_up", h3, plan.weight("gu2"))
    x3, f2 = _mm_residual("ffn2_down", act2, plan.weight("down2"), x2, _row(half, 8))
    dx3, df2, loss_acc = _loss_bwd(x3, target, f2, _row(half, 8))

    dgu2 = _ffn_dact("ffn2_dact", df2, plan.weight("down2"), gu2)
    plan.partial("gu2", _ffn_dwgu("ffn2_dwgu", h3, dgu2))
    plan.partial("down2", _mm_tn("ffn2_dwd", act2, df2, (1408, 512), (1024,), plan))
    dh3 = _ffn_dh("ffn2_dh", dgu2, plan.weight("gu2"), plan)
    dx2, do, acc3 = _rms_mod_bwd("rms_mod_bwd3", dh3, x2, dx3, g_ffn2, _row(mod, 7), fo, _row(mod, 5))

    dgab, dyy = _merge_bwd(do, plan.weight("o"), z, o_ga, yy, plan)
    plan.partial("o", _mm_tn("mix_dwo", merged, do, (1024,), (1024,), plan))
    dmixed = _mm_up_t("pool_up_t", dyy, 0, plan.weight("pool_up"))
    dattn = _mm_up_t("attn_up_t", dyy, 1, plan.weight("attn_up"))
    plan.partial("pool_up", _mm_up_dw("pool_up_dw", mixed, dyy, 0))
    plan.partial("attn_up", _mm_up_dw("attn_up_dw", attn, dyy, 1))
    dpooled, dpm, dps = _pool_mix_bwd(pooled, plan.weight("pool_mix"), pool_scale, dmixed)
    plan.partial("pool_mix", dpm)
    du_pool = _pool_bwd(dpooled)
    dq, dkp, dkc, dvp, dvc, dl, dsink, dgain = _attn_bwd(z, o_q, o_k, qg2, kg2, sink_rows, bias, dattn, plan)
    dk, dv = _kv_combine(dkp, dkc, dvp, dvc)
    drb = _rel_bias_grad(dl.reshape(N_KV, 2 * BLK, 8, BLK).transpose(0, 2, 3, 1).reshape(N_HEADS, BLK * 2 * BLK))
    dz = jnp.concatenate([du_pool, dq, dk, dv, dgab[0], dgab[1]], axis=1)
    plan.partial("w_in", _mm_tn("mix_dwin", dz, h2, (1280, 256), (1024,), plan))
    tnd = _pick(D, (512,))
    dh2 = _mm("mix_dh", (S // tmz, D // tnd, 1), NN, dz, pl.BlockSpec((tmz, IN_W), lambda i, j, k: (i, 0)),
              w_in_t, pl.BlockSpec((IN_W, tnd), lambda i, j, k: (0, j)), [], [], [_sds((S, D), F32)],
              [pl.BlockSpec((tmz, tnd), lambda i, j, k: (i, j))], None, _store, plan)[0]
    dx1, df1, acc2 = _rms_mod_bwd("rms_mod_bwd2", dh2, x1, dx2, g_mix, _row(mod, 4), f1, _row(half, 2))

    dgu1 =_ffn_dact("ffn1_dact", df1, plan.weight("down1"), gu1, plan)
    plan.partial("gu1", _ffn_dwgu("ffn1_dwgu", h1, dgu1, plan))
    plan.partial("down1", _mm_tn("ffn1_dwd", act1, df1, (1408, 512), (1024,), plan))
    dh1 = _ffn_dh("ffn1_dh", dgu1, plan.weight("gu1"), plan)
    grad_x, acc1 = _rms_mod_bwd("rms_mod_bwd1", dh1, x, dx1, g_ffn1, _row(mod, 1), plan=plan)

    dmod = jnp.concatenate([_row(acc1, 0), _row(acc1, 1), 0.5 * _row(acc2, 3),
                            _row(acc2, 0), _row(acc2, 1), _row(acc3, 3),
                            _row(acc3, 0), _row(acc3, 1), 0.5 * _row(loss_acc, 1)], axis=0)
    fold = lambda r: r[:, :HEAD_DIM] + r[:, HEAD_DIM:]
    small = dict(
        dmod=dmod, g_ffn1=_row(acc1, 2), g_mix=_row(acc2, 2), g_ffn2=_row(acc3, 2), pool_scale=_row(dps, 0),
        q_gain=fold(_row(dgain, 0)), k_gain=fold(_row(dgain, 1)),
        sinks=jnp.sum(dsink.reshape(N_HEADS, BLK), axis=1).reshape(1, N_HEADS), rel_bias=drb,
        loss=(0.5 / D) * jnp.sum(_row(loss_acc, 0)).reshape(1, 1))
    return grad_x, small


SMALL_ORDER = ("dmod", "g_ffn1", "g_mix", "g_ffn2", "pool_scale", "q_gain", "k_gain", "sinks", "rel_bias", "loss")


def _pack_small(vals):
    flat = jnp.concatenate([vals[k].reshape(-1) for k in SMALL_ORDER])
    n = flat.shape[0]
    rows = -(-n // (8 * LANES)) * 8
    return jnp.pad(flat, (0, rows * LANES - n)).reshape(rows, LANES)


def _unpack_small(packed, like):
    flat = packed.reshape(-1)
    out, off = {}, 0
    for k in SMALL_ORDER:
        n = int(np.prod(like[k].shape))
        out[k] = flat[off:off + n].reshape(like[k].shape)
        off += n
    return out


def kernel(x, c, w_ada, b_ada, g_ffn1, w_ffn1_gu, w_ffn1_down, g_mix, w_in, pool_mix, pool_scale, w_pool_up, q_gain, k_gain, sinks, rel_bias, w_attn_up, w_o, g_ffn2, w_ffn2_gu, w_ffn2_down, loss_target, m_w_ada, m_b_ada, m_g_ffn1, m_w_ffn1_gu, m_w_ffn1_down, m_g_mix, m_w_in, m_pool_mix, m_pool_scale, m_w_pool_up, m_q_gain, m_k_gain, m_sinks, m_rel_bias, m_w_attn_up, m_w_o, m_g_ffn2, m_w_ffn2_gu, m_w_ffn2_down, v_w_ada, v_b_ada, v_g_ffn1, v_w_ffn1_gu, v_w_ffn1_down, v_g_mix, v_w_in, v_pool_mix, v_pool_scale, v_w_pool_up, v_q_gain, v_k_gain, v_sinks, v_rel_bias, v_w_attn_up, v_w_o, v_g_ffn2, v_w_ffn2_gu, v_w_ffn2_down):
    S, D = x.shape[1], x.shape[2]
    gw = pool_mix.shape[3]
    r = 2 * lax.axis_index("x") + lax.axis_index("y")

    two_d = lambda a: a.reshape(-1, a.shape[-1])
    w_sh = dict(gu1=w_ffn1_gu, down1=w_ffn1_down, w_in=w_in, pool_mix=pool_mix, pool_up=w_pool_up, attn_up=w_attn_up, o=w_o,
                gu2=w_ffn2_gu, down2=w_ffn2_down)
    m_sh = dict(gu1=m_w_ffn1_gu, down1=m_w_ffn1_down, w_in=m_w_in, pool_mix=m_pool_mix, pool_up=m_w_pool_up, attn_up=m_w_attn_up,
                o=m_w_o, gu2=m_w_ffn2_gu, down2=m_w_ffn2_down)
    v_sh = dict(gu1=v_w_ffn1_gu, down1=v_w_ffn1_down, w_in=v_w_in, pool_mix=v_pool_mix, pool_up=v_w_pool_up, attn_up=v_w_attn_up,
                o=v_w_o, gu2=v_w_ffn2_gu, down2=v_w_ffn2_down)
    view = lambda k, a: two_d(a).T if k == "w_in" else two_d(a)
    unview = lambda k, a: (a.T if k == "w_in" else a).reshape(w_sh[k].shape)
    w2 = {k: view(k, w_sh[k]) for k in BIG}
    full = {k: _cast_into_slot("cast_" + k, w2[k]) for k in BIG}
    plan = _Plan(w2, {k: view(k, m_sh[k]) for k in BIG}, {k: view(k, v_sh[k]) for k in BIG}, full, D, gw)

    c_all = _gather_all("gather_c", jnp.broadcast_to(c, (8, D)))[:, 0, :]
    cols = w_ada.shape[2]
    b_sh = lax.dynamic_slice(b_ada, (0, r * cols), (1, cols))
    mod_cols = _mod_fwd(c_all, w_ada[0], b_sh)
    mod_all = _chip_exchange("mod_exchange", mod_cols)
    me = 4 * lax.axis_index("x") + 2 * lax.axis_index("y") + lax.axis_index("c")
    mod = lax.dynamic_slice(mod_all, (0, me, 0), (N_CHIPS, 1, cols)).reshape(9, D)

    grad_x, small = _local_step(x[0], loss_target[0], mod, g_ffn1, g_mix, g_ffn2, pool_scale, q_gain, k_gain,
                                sinks, rel_bias, plan)

    small_w = dict(dmod=b_ada, g_ffn1=g_ffn1, g_mix=g_mix, g_ffn2=g_ffn2, pool_scale=pool_scale, q_gain=q_gain, k_gain=k_gain,
                   sinks=sinks, rel_bias=rel_bias, loss=jnp.zeros((1, 1), F32))
    small_m = dict(dmod=m_b_ada, g_ffn1=m_g_ffn1, g_mix=m_g_mix, g_ffn2=m_g_ffn2, pool_scale=m_pool_scale, q_gain=m_q_gain,
                   k_gain=m_k_gain, sinks=m_sinks, rel_bias=m_rel_bias, loss=jnp.zeros((1, 1), F32))
    small_v = dict(dmod=v_b_ada, g_ffn1=v_g_ffn1, g_mix=v_g_mix, g_ffn2=v_g_ffn2, pool_scale=v_pool_scale, q_gain=v_q_gain,
                   k_gain=v_k_gain, sinks=v_sinks, rel_bias=v_rel_bias, loss=jnp.ones((1, 1), F32))
    small_all = _gather_all("gather_small", _pack_small(small))
    sg, sd, sm, sv = [_unpack_small(a, small_w) for a in
                      _small_finish(small_all, _pack_small(small_w), _pack_small(small_m), _pack_small(small_v))]
    loss = sg["loss"].reshape(())

    dmod_all = small_all.reshape(N_DEV, -1)[:, :9 * D]
    dmod_sh = lax.dynamic_slice(dmod_all, (0, r * cols), (N_DEV, cols))
    g_ada, d_ada, nm_ada, nv_ada = _wada_bwd(c_all, dmod_sh, w_ada[0], m_w_ada[0], v_w_ada[0])
    plan.alone("+join_down1")

    big = [{k: unview(k, plan.result[k][i]) for k in BIG} for i in range(4)]

    def ordered(b, ada, sm_):
        return (ada[None], sm_["dmod"], sm_["g_ffn1"], b["gu1"], b["down1"], sm_["g_mix"], b["w_in"], b["pool_mix"],
                sm_["pool_scale"], b["pool_up"], sm_["q_gain"], sm_["k_gain"], sm_["sinks"], sm_["rel_bias"], b["attn_up"],
                b["o"], sm_["g_ffn2"], b["gu2"], b["down2"])

    return (loss, grad_x[None], *ordered(big[0], g_ada, sg), *ordered(big[1], d_ada, sd), *ordered(big[2], nm_ada, sm),
            *ordered(big[3], nv_ada, sv))
```

```python
import numpy as np
import jax
import jax.numpy as jnp
from jax import lax
from jax.experimental import pallas as pl
from jax.experimental.pallas import tpu as pltpu

BF = jnp.bfloat16
F32 = jnp.float32
MESH = pl.DeviceIdType.MESH

EPS = 1e-6
NEG_INF = -1e30
HEAD_DIM = 64
N_HEADS = 16
N_KV = 2
ATT_W = N_HEADS * HEAD_DIM
KV_W = N_KV * HEAD_DIM
BLK = 128
NUM_BUCKETS = 32
POOL_MAX_W = 16
N_CHIPS = 4
N_DEV = 8
LANES = 128
ADAM_LR, ADAM_B1, ADAM_B2, ADAM_EPS, ADAM_WD, ADAM_STEP = 0.001, 0.9, 0.999, 1e-08, 0.01, 10
VMEM_LIMIT = 52 * 1024 * 1024
ANY = pl.BlockSpec(memory_space=pl.ANY)


def _pick(dim, prefs):
    for p in prefs:
        if p <= dim and dim % p == 0:
            return p
    return dim


def _row_tile(rows, cap):
    return max(t for t in range(16, min(rows, cap) + 1, 16) if rows % t == 0)


def _sds(shape, dtype):
    return jax.ShapeDtypeStruct(tuple(shape), dtype)


def _place():
    return lax.axis_index("x"), lax.axis_index("y"), lax.axis_index("c")


def _other_chips(x, y):
    return [(1 - x, y), (x, 1 - y), (1 - x, 1 - y)]


def _chip_of(chip):
    return 2 * chip[0] + chip[1]


def _half_rows(ref, lead, cc, h):
    return ref.at[lead, pl.ds(pl.multiple_of(cc * h, 16), h), :]


class _Stage:
    def __init__(self, bufs, outs, alias, n_sem, start, wait):
        self.bufs, self.outs, self.alias, self.n_sem, self.start, self.wait = bufs, outs, alias, n_sem, start, wait


def _stage_plumbing(stages, n_in0, n_out0):
    bufs, outs, aliases, spans, scratch = [], [], {}, [], []
    for st in stages:
        i0, o0 = len(bufs), len(outs)
        bufs += list(st.bufs)
        outs += list(st.outs)
        for a, b in st.alias.items():
            aliases[n_in0 + i0 + a] = n_out0 + o0 + b
        spans.append((i0, len(bufs), o0, len(outs)))
        scratch += [pltpu.SemaphoreType.DMA((st.n_sem,)), pltpu.SemaphoreType.DMA((st.n_sem,))]

    def run(which, in_refs, out_refs, sem_refs):
        for s, st in enumerate(stages):
            i0, i1, o0, o1 = spans[s]
            getattr(st, which)(in_refs[i0:i1], out_refs[o0:o1], sem_refs[2 * s], sem_refs[2 * s + 1])

    def split(flat):
        return [list(flat[o0:o1]) for (_, _, o0, o1) in spans]

    return bufs, outs, aliases, scratch, run, split


def _run_stages(name, stages):
    bufs, outs, aliases, scratch, run, split = _stage_plumbing(stages, 0, 0)
    ni, no = len(bufs), len(outs)

    def body(*refs):
        ins, os_, sems = refs[:ni], refs[ni:ni + no], refs[ni + no:]
        run("start", ins, os_, sems)
        run("wait", ins, os_, sems)

    res = pl.pallas_call(body, in_specs=[ANY] * ni, out_specs=[ANY] * no, out_shape=outs, input_output_aliases=aliases,
                         scratch_shapes=scratch, name=name)(*bufs)
    return split(res)


def _call(name, body, grid, in_specs, out_specs, out_shape, args, scratch=(), sem=None, plan=None):
    stages = plan.stages(name) if plan is not None else []
    n_in, n_out, n_scr = len(args), len(out_shape), len(scratch)
    if not stages:
        return pl.pallas_call(body, grid=grid, in_specs=list(in_specs), out_specs=list(out_specs), out_shape=list(out_shape),
                              scratch_shapes=list(scratch), name=name,
                              compiler_params=pltpu.CompilerParams(dimension_semantics=sem, vmem_limit_bytes=VMEM_LIMIT))(*args)
    bufs, s_outs, aliases, s_scratch, run, split = _stage_plumbing(stages, n_in, n_out)
    nb, nso = len(bufs), len(s_outs)

    def hosted(*refs):
        ins = refs[:n_in]
        s_ins = refs[n_in:n_in + nb]
        outs = refs[n_in + nb:n_in + nb + n_out]
        s_os = refs[n_in + nb + n_out:n_in + nb + n_out + nso]
        scr = refs[n_in + nb + n_out + nso:n_in + nb + n_out + nso + n_scr]
        sems = refs[n_in + nb + n_out + nso + n_scr:]
        first = pl.program_id(0) == 0
        last = pl.program_id(0) == grid[0] - 1
        for d in range(1, len(grid)):
            first = first & (pl.program_id(d) == 0)
            last = last & (pl.program_id(d) == grid[d] - 1)

        @pl.when(first)
        def _():
            run("start", s_ins, s_os, sems)

        body(*ins, *outs, *scr)

        @pl.when(last)
        def _():
            run("wait", s_ins, s_os, sems)

    res = pl.pallas_call(
        hosted, grid=grid, in_specs=list(in_specs) + [ANY] * nb, out_specs=list(out_specs) + [ANY] * nso,
        out_shape=list(out_shape) + s_outs, input_output_aliases=aliases, scratch_shapes=list(scratch) + s_scratch, name=name,
        compiler_params=pltpu.CompilerParams(dimension_semantics=("arbitrary",) * len(grid), vmem_limit_bytes=VMEM_LIMIT))(*args, *bufs)
    plan.done(name, split(res[n_out:]))
    return list(res[:n_out])


def _gather_ici_stage(fulls):
    n = len(fulls)

    def copy(i, j, slot, ins, outs, send, recv):
        x, y, c = _place()
        chip = _other_chips(x, y)[j]
        h = fulls[i].shape[1] // 2
        s = 3 * i + j
        return pltpu.make_async_remote_copy(_half_rows(ins[i], 2 * x + y, c, h), _half_rows(outs[i], slot(x, y, chip), c, h),
                                            send.at[s], recv.at[s], device_id=(*chip, c), device_id_type=MESH)

    mine = lambda x, y, chip: 2 * x + y
    theirs = lambda x, y, chip: _chip_of(chip)

    def start(ins, outs, send, recv):
        for i in range(n):
            for j in range(3):
                copy(i, j, mine, ins, outs, send, recv).start()

    def wait(ins, outs, send, recv):
        for i in range(n):
            for j in range(3):
                copy(i, j, theirs, ins, outs, send, recv).wait_recv()
        for i in range(n):
            for j in range(3):
                copy(i, j, mine, ins, outs, send, recv).wait_send()

    return _Stage(fulls, [_sds(f.shape, f.dtype) for f in fulls], {i: i for i in range(n)}, 3 * n, start, wait)


def _gather_d2d_stage(fulls):
    n = len(fulls)

    def copy(i, j, cc, ins, outs, send, recv):
        x, y, c = _place()
        rj = _chip_of(_other_chips(x, y)[j])
        h = fulls[i].shape[1] // 2
        half = cc(c)
        s = 3 * i + j
        return pltpu.make_async_remote_copy(_half_rows(ins[i], rj, half, h), _half_rows(outs[i], rj, half, h),
                                            send.at[s], recv.at[s], device_id=(x, y, 1 - c), device_id_type=MESH)

    mine = lambda c: c
    theirs = lambda c: 1 - c

    def start(ins, outs, send, recv):
        for i in range(n):
            for j in range(3):
                copy(i, j, mine, ins, outs, send, recv).start()

    def wait(ins, outs, send, recv):
        for i in range(n):
            for j in range(3):
                copy(i, j, theirs, ins, outs, send, recv).wait_recv()
        for i in range(n):
            for j in range(3):
                copy(i, j, mine, ins, outs, send, recv).wait_send()

    return _Stage(fulls, [_sds(f.shape, f.dtype) for f in fulls], {i: i for i in range(n)}, 3 * n, start, wait)


def _split_stage(parts):
    n = len(parts)

    def copy(i, ins, outs, send, recv):
        x, y, c = _place()
        h = parts[i].shape[1] // 2
        return pltpu.make_async_remote_copy(_half_rows(ins[i], slice(None), 1 - c, h), outs[i], send.at[i], recv.at[i],
                                            device_id=(x, y, 1 - c), device_id_type=MESH)

    def start(ins, outs, send, recv):
        for i in range(n):
            copy(i, ins, outs, send, recv).start()

    def wait(ins, outs, send, recv):
        for i in range(n):
            copy(i, ins, outs, send, recv).wait_recv()
        for i in range(n):
            copy(i, ins, outs, send, recv).wait_send()

    return _Stage(parts, [_sds((N_CHIPS, p.shape[1] // 2, p.shape[2]), p.dtype) for p in parts], {}, n, start, wait)


def _owners_stage(sums):
    n = len(sums)

    def copy(i, j, mine, ins, outs, send, recv):
        x, y, c = _place()
        chip = _other_chips(x, y)[j]
        slot = (2 * x + y) if mine else _chip_of(chip)
        return pltpu.make_async_remote_copy(ins[i].at[_chip_of(chip)], outs[i].at[slot], send.at[3 * i + j], recv.at[3 * i + j],
                                            device_id=(*chip, c), device_id_type=MESH)

    def start(ins, outs, send, recv):
        for i in range(n):
            for j in range(3):
                copy(i, j, True, ins, outs, send, recv).start()

    def wait(ins, outs, send, recv):
        for i in range(n):
            for j in range(3):
                copy(i, j, False, ins, outs, send, recv).wait_recv()
        for i in range(n):
            for j in range(3):
                copy(i, j, True, ins, outs, send, recv).wait_send()

    return _Stage(sums, [_sds(s.shape, s.dtype) for s in sums], {}, 3 * n, start, wait)


def _join_stage(gs):
    n = len(gs)

    def copy(i, mine, ins, outs, send, recv):
        x, y, c = _place()
        slot = c if mine else 1 - c
        return pltpu.make_async_remote_copy(ins[i].at[slot], outs[i].at[slot], send.at[i], recv.at[i],
                                            device_id=(x, y, 1 - c), device_id_type=MESH)

    def start(ins, outs, send, recv):
        for i in range(n):
            copy(i, True, ins, outs, send, recv).start()

    def wait(ins, outs, send, recv):
        for i in range(n):
            copy(i, False, ins, outs, send, recv).wait_recv()
        for i in range(n):
            copy(i, True, ins, outs, send, recv).wait_send()

    return _Stage(gs, [_sds(g.shape, g.dtype) for g in gs], {i: i for i in range(n)}, n, start, wait)


def _chip_exchange(name, arr):
    def body(src, dst, send, recv, loc):
        x, y, c = _place()
        r = 2 * x + y
        chips = _other_chips(x, y)

        def cp(j, slot):
            return pltpu.make_async_remote_copy(src, dst.at[slot], send.at[j], recv.at[j], device_id=(*chips[j], c), device_id_type=MESH)

        mine = pltpu.make_async_copy(src, dst.at[r], loc)
        mine.start()
        for j in range(3):
            cp(j, r).start()
        for j in range(3):
            cp(j, _chip_of(chips[j])).wait_recv()
        for j in range(3):
            cp(j, r).wait_send()
        mine.wait()

    return pl.pallas_call(body, in_specs=[ANY], out_specs=ANY, out_shape=_sds((N_CHIPS, *arr.shape), arr.dtype),
                          scratch_shapes=[pltpu.SemaphoreType.DMA((3,)), pltpu.SemaphoreType.DMA((3,)), pltpu.SemaphoreType.DMA],
                          name=name)(arr)


def _gather_all(name, arr):
    def body(src, dst, send, recv, loc):
        x, y, c = _place()

        def cp(k, slot_of_me):
            px, py, pc = x ^ ((k >> 2) & 1), y ^ ((k >> 1) & 1), c ^ (k & 1)
            slot = (4 * x + 2 * y + c) if slot_of_me else (4 * px + 2 * py + pc)
            return pltpu.make_async_remote_copy(src, dst.at[slot], send.at[k - 1], recv.at[k - 1],
                                                device_id=(px, py, pc), device_id_type=MESH)

        mine = pltpu.make_async_copy(src, dst.at[4 * x + 2 * y + c], loc)
        mine.start()
        for k in range(1, N_DEV):
            cp(k, True).start()
        for k in range(1, N_DEV):
            cp(k, False).wait_recv()
        for k in range(1, N_DEV):
            cp(k, True).wait_send()
        mine.wait()

    return pl.pallas_call(body, in_specs=[ANY], out_specs=ANY, out_shape=_sds((N_DEV, *arr.shape), arr.dtype),
                          scratch_shapes=[pltpu.SemaphoreType.DMA((N_DEV - 1,)), pltpu.SemaphoreType.DMA((N_DEV - 1,)), pltpu.SemaphoreType.DMA],
                          name=name)(arr)


NN = (((1,), (0,)), ((), ()))
NT = (((1,), (1,)), ((), ()))
TN = (((0,), (0,)), ((), ()))


ALL = slice(None)


def _mm(name, grid, dims, a, a_spec, b, b_spec, extras, extra_specs, out_shapes, out_specs, acc_shape, epilogue, plan=None):
    n_k = grid[2]
    n_e = len(extras)
    n_o = len(out_shapes)

    def body(*refs):
        a_ref, b_ref = refs[0], refs[1]
        e_refs = refs[2:2 + n_e]
        o_refs = refs[2 + n_e:2 + n_e + n_o]
        p = lax.dot_general(a_ref[...].astype(BF), b_ref[...].astype(BF), dims, preferred_element_type=F32)
        if n_k == 1:
            epilogue(p, e_refs, o_refs, ALL)
        else:
            acc = refs[-1]
            k = pl.program_id(2)

            @pl.when(k == 0)
            def _():
                acc[...] = p

            @pl.when(k > 0)
            def _():
                acc[...] += p

            @pl.when(k == n_k - 1)
            def _():
                epilogue(acc[...], e_refs, o_refs, ALL)

    scratch = [] if n_k == 1 else [pltpu.VMEM(acc_shape, F32)]
    return _call(name, body, grid, [a_spec, b_spec, *extra_specs], out_specs, out_shapes, [a, b, *extras], scratch,
                 ("parallel", "parallel", "arbitrary"), plan)


def _store(p, e, o, rs):
    o[0][rs, :] = p.astype(o[0].dtype)


def _rms_mod_fwd(name, x, gain, shift, scale, plan=None):
    S, D = x.shape
    ts = _pick(S, (512,))

    def body(x_ref, g_ref, sh_ref, sc_ref, h_ref):
        xv = x_ref[...]
        r = lax.rsqrt(jnp.mean(xv * xv, axis=-1, keepdims=True) + EPS)
        n = xv * r * g_ref[...]
        h_ref[...] = (n * (1.0 + sc_ref[...]) + sh_ref[...]).astype(BF)

    row = pl.BlockSpec((ts, D), lambda i: (i, 0))
    vec = pl.BlockSpec((1, D), lambda i: (0, 0))
    return _call(name, body, (S // ts,), [row, vec, vec, vec], [row], [_sds((S, D), BF)], [x, gain, shift, scale],
                 sem=("parallel",), plan=plan)[0]


def _acc_rows(acc_ref, first, part):
    @pl.when(first)
    def _():
        acc_ref[...] = part

    @pl.when(jnp.logical_not(first))
    def _():
        acc_ref[...] += part


def _rms_mod_bwd(name, dh, x, dres, gain, scale, f=None, coef=None, plan=None):
    S, D = x.shape
    ts = _pick(S, (256,))
    gated = f is not None

    def body(dh_ref, x_ref, dr_ref, g_ref, sc_ref, *rest):
        xv = x_ref[...]
        dhv = dh_ref[...]
        g = g_ref[...]
        r = lax.rsqrt(jnp.mean(xv * xv, axis=-1, keepdims=True) + EPS)
        xhat = xv * r
        dn = dhv * (1.0 + sc_ref[...])
        dxhat = dn * g
        dx = dr_ref[...] + r * (dxhat - xhat * jnp.mean(dxhat * xhat, axis=-1, keepdims=True))
        rows = [jnp.sum(dhv, axis=0, keepdims=True), jnp.sum(dhv * (xhat * g), axis=0, keepdims=True),
                jnp.sum(dn * xhat, axis=0, keepdims=True)]
        if gated:
            f_ref, c_ref, dx_ref, df_ref, acc_ref = rest
            df_ref[...] = (dx * c_ref[...]).astype(BF)
            rows.append(jnp.sum(dx * f_ref[...].astype(F32), axis=0, keepdims=True))
        else:
            dx_ref, acc_ref = rest
        dx_ref[...] = dx
        _acc_rows(acc_ref, pl.program_id(0) == 0, jnp.concatenate(rows + [jnp.zeros((8 - len(rows), D), F32)], axis=0))

    row = pl.BlockSpec((ts, D), lambda i: (i, 0))
    vec = pl.BlockSpec((1, D), lambda i: (0, 0))
    acc = pl.BlockSpec((8, D), lambda i: (0, 0))
    if gated:
        return _call(name, body, (S // ts,), [row, row, row, vec, vec, row, vec], [row, row, acc],
                     [_sds((S, D), F32), _sds((S, D), BF), _sds((8, D), F32)], [dh, x, dres, gain, scale, f, coef],
                     sem=("arbitrary",), plan=plan)
    return _call(name, body, (S // ts,), [row, row, row, vec, vec], [row, acc],
                 [_sds((S, D), F32), _sds((8, D), F32)], [dh, x, dres, gain, scale], sem=("arbitrary",), plan=plan)


def _loss_bwd(x3, target, f, coef):
    S, D = x3.shape
    ts = _pick(S, (512,))

    def body(x_ref, t_ref, f_ref, c_ref, dx_ref, df_ref, acc_ref):
        e = x_ref[...] - t_ref[...]
        dx = e * (1.0 / D)
        dx_ref[...] = dx
        df_ref[...] = (dx * c_ref[...]).astype(BF)
        part = jnp.concatenate([jnp.sum(e * e, axis=0, keepdims=True), jnp.sum(dx * f_ref[...].astype(F32), axis=0, keepdims=True),
                                jnp.zeros((6, D), F32)], axis=0)
        _acc_rows(acc_ref, pl.program_id(0) == 0, part)

    row = pl.BlockSpec((ts, D), lambda i: (i, 0))
    return _call("loss_bwd", body, (S // ts,), [row, row, row, pl.BlockSpec((1, D), lambda i: (0, 0))],
                 [row, row, pl.BlockSpec((8, D), lambda i: (0, 0))],
                 [_sds((S, D), F32), _sds((S, D), BF), _sds((8, D), F32)], [x3, target, f, coef], sem=("arbitrary",))


def _silu_parts(g):
    s = jax.nn.sigmoid(g)
    return s, g * s


def _ffn_up(name, h, wgu4, plan=None):
    S, D = h.shape
    SH = wgu4.shape[2]
    F = 2 * SH
    tm = _pick(S, (512,))
    tn = _pick(SH, (1408, 256))
    nts = SH // tn

    def body(h_ref, wg_ref, wu_ref, gu_ref, act_ref):
        hv = h_ref[...]
        g = jnp.dot(hv, wg_ref[...], preferred_element_type=F32)
        u = jnp.dot(hv, wu_ref[...], preferred_element_type=F32)
        gu_ref[0] = g.astype(BF)
        gu_ref[1] = u.astype(BF)
        act_ref[...] = (_silu_parts(g)[1] * u).astype(BF)

    return _call(name, body, (S // tm, F // tn),
                 [pl.BlockSpec((tm, D), lambda i, j: (i, 0)),
                  pl.BlockSpec((None, D, tn), lambda i, j: (j // nts, 0, j % nts)),
                  pl.BlockSpec((None, D, tn), lambda i, j: (2 + j // nts, 0, j % nts))],
                 [pl.BlockSpec((2, tm, tn), lambda i, j: (0, i, j)), pl.BlockSpec((tm, tn), lambda i, j: (i, j))],
                 [_sds((2, S, F), BF), _sds((S, F), BF)], [h, wgu4, wgu4], sem=("parallel", "parallel"), plan=plan)


def _mm_residual(name, a, w, x_in, coef, plan=None):
    S, K = a.shape
    D = w.shape[1]
    tm = _pick(S, (1024,))
    tn = _pick(D, (512,))
    tk = K

    def epi(p, e, o, rs):
        o[0][rs, :] = e[0][rs, :] + e[1][...] * p
        o[1][rs, :] = p.astype(BF)

    tile = pl.BlockSpec((tm, tn), lambda i, j, k: (i, j))
    return _mm(name, (S // tm, D // tn, K // tk), NN,
               a, pl.BlockSpec((tm, tk), lambda i, j, k: (i, k)),
               w, pl.BlockSpec((tk, tn), lambda i, j, k: (k, j)),
               [x_in, coef], [tile, pl.BlockSpec((1, tn), lambda i, j, k: (0, j))],
               [_sds((S, D), F32), _sds((S, D), BF)], [tile, tile], (tm, tn), epi, plan)


def _ffn_dact(name, df, wd, gu, plan=None):
    S, D = df.shape
    F = wd.shape[0]
    tm = _pick(S, (512,))
    tn = _pick(F, (1408, 256))

    def epi(p, e, o, rs):
        g = e[0][0, rs, :].astype(F32)
        u = e[0][1, rs, :].astype(F32)
        s, sg = _silu_parts(g)
        o[0][0, rs, :] = (p * u * (s * (1.0 + g * (1.0 - s)))).astype(BF)
        o[0][1, rs, :] = (p * sg).astype(BF)

    pair = pl.BlockSpec((2, tm, tn), lambda i, j, k: (0, i, j))
    return _mm(name, (S // tm, F // tn, 1), NT,
               df, pl.BlockSpec((tm, D), lambda i, j, k: (i, 0)),
               wd, pl.BlockSpec((tn, D), lambda i, j, k: (j, 0)),
               [gu], [pair], [_sds((2, S, F), BF)], [pair], None, epi, plan)[0]


def _ffn_dh(name, dgu, wgu4, plan=None):
    _, S, F = dgu.shape
    _, D, SH = wgu4.shape
    tm = _pick(S, (1024,))
    tn = _pick(D, (1024,))
    tk = SH
    nkp = F // tk
    nks = SH // tk
    return _mm(name, (S // tm, D // tn, 2 * nkp), NT,
               dgu, pl.BlockSpec((None, tm, tk), lambda i, j, k: (k // nkp, i, k % nkp)),
               wgu4, pl.BlockSpec((None, tn, tk), lambda i, j, k: (k // nks, j, k % nks)),
               [], [], [_sds((S, D), F32)], [pl.BlockSpec((tm, tn), lambda i, j, k: (i, j))], (tm, tn), _store, plan)[0]


def _ffn_dwgu(name, h, dgu, plan=None):
    _, S, F = dgu.shape
    D = h.shape[1]
    SH = F // 2
    tk1 = _pick(D, (1024,))
    tn = _pick(SH, (1408, 256))
    ts = _pick(S, (2048, 1024))
    npj = F // tn
    nsj = SH // tn
    return _mm(name, (D // tk1, 2 * npj, S // ts), TN,
               h, pl.BlockSpec((ts, tk1), lambda i, j, k: (k, i)),
               dgu, pl.BlockSpec((None, ts, tn), lambda i, j, k: (j // npj, k, j % npj)),
               [], [], [_sds((4, D, SH), BF)],
               [pl.BlockSpec((None, tk1, tn), lambda i, j, k: (j // nsj, i, j % nsj))], (tk1, tn), _store, plan)[0]


def _mm_tn(name, a, b, tk1_prefs, tn_prefs, plan=None):
    S, K1 = a.shape
    N = b.shape[1]
    tk1 = _pick(K1, tk1_prefs)
    tn = _pick(N, tn_prefs)
    ts = _pick(S, (2048, 1024))
    return _mm(name, (K1 // tk1, N // tn, S // ts), TN,
               a, pl.BlockSpec((ts, tk1), lambda i, j, k: (k, i)),
               b, pl.BlockSpec((ts, tn), lambda i, j, k: (k, j)),
               [], [], [_sds((K1, N), BF)], [pl.BlockSpec((tk1, tn), lambda i, j, k: (i, j))], (tk1, tn), _store, plan)[0]


def _pool_window(ext, w, back):
    n = ext.shape[0]
    s = ext
    for step in (1, 2, 4, 8):
        sh = pltpu.roll(s, (n - step) if back else step, axis=0)
        s = jnp.where(w > step, s + sh, s)
    return s


def _pool_fwd(z, PW):
    S = z.shape[0]
    tc = _pick(S, (1024,))
    bpg = (PW // 4) // LANES
    H = POOL_MAX_W

    def body(prev_ref, u_ref, o_ref):
        i = pl.program_id(0)
        j = pl.program_id(1)
        w = lax.shift_left(jnp.int32(2), j // bpg)
        u = u_ref[...]
        prev = jnp.where(i > 0, prev_ref[...], 0.0)
        s = _pool_window(jnp.concatenate([prev, u], axis=0), w, False)[H:]
        t = i * tc + lax.broadcasted_iota(jnp.int32, (tc, LANES), 0)
        cnt = jnp.minimum(t + 1, w).astype(F32)
        o_ref[...] = (s / cnt - u).astype(BF)

    r = tc // H
    return _call("pool_fwd", body, (S // tc, PW // LANES),
                 [pl.BlockSpec((H, LANES), lambda i, j: (jnp.maximum(i * r - 1, 0), j)),
                  pl.BlockSpec((tc, LANES), lambda i, j: (i, j))],
                 [pl.BlockSpec((tc, LANES), lambda i, j: (i, j))], [_sds((S, PW), BF)], [z, z], sem=("parallel", "parallel"))[0]


def _pool_bwd(dpooled):
    S, PW = dpooled.shape
    tc = _pick(S, (1024,))
    bpg = (PW // 4) // LANES
    H = POOL_MAX_W
    last = S // tc - 1

    def body(dp_ref, nxt_ref, o_ref):
        i = pl.program_id(0)
        j = pl.program_id(1)
        w = lax.shift_left(jnp.int32(2), j // bpg)
        dp = dp_ref[...]
        nxt = jnp.where(i < last, nxt_ref[...], 0.0)
        ext = jnp.concatenate([dp, nxt], axis=0)
        t = i * tc + lax.broadcasted_iota(jnp.int32, (tc + H, LANES), 0)
        cnt = jnp.minimum(t + 1, w).astype(F32)
        s = _pool_window(ext / cnt, w, True)[:tc]
        o_ref[...] = (s - dp).astype(BF)

    r = tc // H
    nh = S // H - 1
    return _call("pool_bwd", body, (S // tc, PW // LANES),
                 [pl.BlockSpec((tc, LANES), lambda i, j: (i, j)),
                  pl.BlockSpec((H, LANES), lambda i, j: (jnp.minimum((i + 1) * r, nh), j))],
                 [pl.BlockSpec((tc, LANES), lambda i, j: (i, j))], [_sds((S, PW), BF)], [dpooled, dpooled],
                 sem=("parallel", "parallel"))[0]


def _pool_mix(pooled, pm, scale):
    S, PW = pooled.shape
    gw = PW // 4
    ts = _pick(S, (1024,))

    def epi(p, e, o, rs):
        o[0][rs, :] = (p * e[0][...]).astype(BF)

    tile = pl.BlockSpec((ts, gw), lambda i, j, k: (i, j))
    return _mm("pool_mix", (S // ts, 4, 1), NN, pooled, tile,
               pm, pl.BlockSpec((None, gw, gw), lambda i, j, k: (j, 0, 0)),
               [scale], [pl.BlockSpec((1, gw), lambda i, j, k: (0, j))], [_sds((S, PW), BF)], [tile], None, epi)[0]


def _pool_mix_bwd(pooled, pm, scale, dmixed):
    S, PW = pooled.shape
    gw = PW // 4
    ts = _pick(S, (1024,))

    def body(p_ref, pm_ref, sc_ref, dm_ref, dp_ref, dpm_ref, dsc_ref):
        i = pl.program_id(1)
        p = p_ref[...]
        w = pm_ref[...]
        dm = dm_ref[...]
        pre = jnp.dot(p, w, preferred_element_type=F32)
        dmp = (dm * sc_ref[...]).astype(BF)
        dp_ref[...] = lax.dot_general(dmp, w, NT, preferred_element_type=F32)
        dw = lax.dot_general(p, dmp, TN, preferred_element_type=F32)
        ds = jnp.concatenate([jnp.sum(dm * pre, axis=0, keepdims=True), jnp.zeros((7, gw), F32)], axis=0)
        _acc_rows(dpm_ref, i == 0, dw)
        _acc_rows(dsc_ref, i == 0, ds)

    tile = pl.BlockSpec((ts, gw), lambda g, i: (i, g))
    return _call("pool_mix_bwd", body, (4, S // ts),
                 [tile, pl.BlockSpec((None, gw, gw), lambda g, i: (g, 0, 0)), pl.BlockSpec((1, gw), lambda g, i: (0, g)), tile],
                 [tile, pl.BlockSpec((None, gw, gw), lambda g, i: (g, 0, 0)), pl.BlockSpec((8, gw), lambda g, i: (0, g))],
                 [_sds((S, PW), F32), _sds((4, gw, gw), F32), _sds((8, PW), F32)], [pooled, pm, scale, dmixed],
                 sem=("parallel", "arbitrary"))


def _bucket_onehot():
    ql = np.arange(BLK)[:, None]
    j = np.arange(2 * BLK)[None, :]
    d = BLK + ql - j
    n = np.clip(d, 0, None)
    nf = np.maximum(n, 1).astype(np.float32)
    max_exact = NUM_BUCKETS // 2
    large = max_exact + (np.log(nf / max_exact) / np.log(BLK / max_exact) * (NUM_BUCKETS - max_exact)).astype(np.int32)
    large = np.minimum(large, NUM_BUCKETS - 1)
    bucket = np.where(n < max_exact, n, large).astype(np.int32)
    valid = (d >= 0) & (d < BLK)
    oh = (bucket[None] == np.arange(NUM_BUCKETS)[:, None, None]) & valid[None]
    return oh.reshape(NUM_BUCKETS, BLK * 2 * BLK)


def _three_bf16(v):
    hi = v.astype(BF)
    r1 = v - hi.astype(F32)
    mid = r1.astype(BF)
    lo = (r1 - mid.astype(F32)).astype(BF)
    return hi, mid, lo


def _bias_table(rel_bias):
    oh = jnp.asarray(_bucket_onehot(), BF)
    tn = 4096

    def body(rb_ref, oh_ref, o_ref):
        o = oh_ref[...]
        hi, mid, lo = _three_bf16(rb_ref[...])
        acc = lax.dot_general(hi, o, TN, preferred_element_type=F32)
        acc = acc + lax.dot_general(mid, o, TN, preferred_element_type=F32)
        acc = acc + lax.dot_general(lo, o, TN, preferred_element_type=F32)
        on_band = jnp.sum(o.astype(F32), axis=0, keepdims=True) > 0.5
        o_ref[...] = jnp.where(on_band, acc, NEG_INF)

    n = oh.shape[1]
    return _call("bias_table", body, (n // tn,),
                 [pl.BlockSpec((NUM_BUCKETS, N_HEADS), lambda i: (0, 0)), pl.BlockSpec((NUM_BUCKETS, tn), lambda i: (0, i))],
                 [pl.BlockSpec((N_HEADS, tn), lambda i: (0, i))], [_sds((N_HEADS, n), F32)], [rel_bias, oh], sem=("parallel",))[0]


def _rel_bias_grad(dl):
    oh = jnp.asarray(_bucket_onehot(), BF)
    n = oh.shape[1]
    tk = 4096

    def body(dl_ref, oh_ref, o_ref):
        o = oh_ref[...]
        hi, mid, lo = _three_bf16(dl_ref[...])
        acc = lax.dot_general(o, hi, NT, preferred_element_type=F32)
        acc = acc + lax.dot_general(o, mid, NT, preferred_element_type=F32)
        acc = acc + lax.dot_general(o, lo, NT, preferred_element_type=F32)
        _acc_rows(o_ref, pl.program_id(0) == 0, acc)

    return _call("rel_bias_grad", body, (n // tk,),
                 [pl.BlockSpec((N_HEADS, tk), lambda i: (0, i)), pl.BlockSpec((NUM_BUCKETS, tk), lambda i: (0, i))],
                 [pl.BlockSpec((NUM_BUCKETS, N_HEADS), lambda i: (0, 0))], [_sds((NUM_BUCKETS, N_HEADS), F32)], [dl, oh],
                 sem=("arbitrary",))[0]


def _lo_half(shape):
    return lax.broadcasted_iota(jnp.int32, shape, 1) < HEAD_DIM


def _half_sum(x, lo):
    s_lo = jnp.sum(jnp.where(lo, x, 0.0), axis=-1, keepdims=True)
    s_hi = jnp.sum(jnp.where(lo, 0.0, x), axis=-1, keepdims=True)
    return jnp.where(lo, s_lo, s_hi)


def _norm2(x, lo):
    r = lax.rsqrt(_half_sum(x * x, lo) * (1.0 / HEAD_DIM) + EPS)
    return x * r, r


def _norm2_bwd(dy, xhat, r, gain, lo):
    dxhat = dy * gain
    dx = r * (dxhat - xhat * (_half_sum(dxhat * xhat, lo) * (1.0 / HEAD_DIM)))
    return dx, dy * xhat


def _swap(x):
    return pltpu.roll(x, HEAD_DIM, axis=1)


def _pair_rows(x, kk):
    return jnp.concatenate([x, _swap(x)] if kk == 0 else [_swap(x), x], axis=0)


def _attn_probs(n, kk, jp0, npairs, zq_ref, K, qg, bias_ref, sink_ref):
    lo_q = _lo_half((BLK, LANES))
    rows, qhats, qrs = [], [], []
    for jp in range(jp0, jp0 + npairs):
        qhat, qr = _norm2(zq_ref[:, jp * LANES:(jp + 1) * LANES], lo_q)
        rows.append(_pair_rows(qhat * qg * (HEAD_DIM ** -0.5), kk))
        qhats.append(qhat)
        qrs.append(qr)
    Q = jnp.concatenate(rows, axis=0).astype(BF)
    cols = slice(jp0 * 2 * BLK, (jp0 + npairs) * 2 * BLK)
    l = lax.dot_general(K, Q, NT, preferred_element_type=F32) + bias_ref[kk, :, cols]
    l = jnp.concatenate([jnp.where(n == 0, NEG_INF, l[:BLK]), l[BLK:]], axis=0)
    sink = sink_ref[kk, :, cols]
    m = jnp.maximum(jnp.max(l, axis=0, keepdims=True), sink)
    e = jnp.exp(l - m)
    es = jnp.exp(sink - m)
    inv = 1.0 / (jnp.sum(e, axis=0, keepdims=True) + es)
    return Q, e * inv, es * inv, qhats, qrs


def _attn_specs(o_q, o_k):
    nq = o_q // 512
    nk = o_k // LANES
    prev = lambda n: (jnp.maximum(n - 1, 0), nk)
    prev_v = lambda n: (jnp.maximum(n - 1, 0), nk + 1)
    return [pl.BlockSpec((BLK, 512), lambda n: (n, nq)), pl.BlockSpec((BLK, 512), lambda n: (n, nq + 1)),
            pl.BlockSpec((BLK, LANES), prev), pl.BlockSpec((BLK, LANES), lambda n: (n, nk)),
            pl.BlockSpec((BLK, LANES), prev_v), pl.BlockSpec((BLK, LANES), lambda n: (n, nk + 1)),
            pl.BlockSpec((1, LANES), lambda n: (0, 0)), pl.BlockSpec((1, LANES), lambda n: (0, 0)),
            pl.BlockSpec((N_KV, 1, 8 * BLK), lambda n: (0, 0, 0)),
            pl.BlockSpec((N_KV, 2 * BLK, 8 * BLK), lambda n: (0, 0, 0))]


def _attn_fwd(z, o_q, o_k, qg2, kg2, sink_rows, bias, plan=None):
    S = z.shape[0]

    def body(zq0, zq1, zkp, zkc, zvp, zvc, qg_ref, kg_ref, sink_ref, bias_ref, o_ref):
        n = pl.program_id(0)
        lo_k = _lo_half((2 * BLK, LANES))
        lo_q = _lo_half((BLK, LANES))
        khat, _ = _norm2(jnp.concatenate([zkp[...], zkc[...]], axis=0), lo_k)
        kn = khat * kg_ref[...]
        vb = jnp.concatenate([zvp[...], zvc[...]], axis=0).astype(BF)
        for kk, zq in enumerate((zq0, zq1)):
            K = jnp.where(lo_k if kk == 0 else jnp.logical_not(lo_k), kn, 0.0).astype(BF)
            for jp in range(4):
                _, p, _, _, _ = _attn_probs(n, kk, jp, 1, zq, K, qg_ref[...], bias_ref, sink_ref)
                r = lax.dot_general(p.astype(BF), vb, TN, preferred_element_type=F32)
                ev, od = r[:BLK], r[BLK:]
                pair = jnp.where(lo_q, ev, _swap(od)) if kk == 0 else jnp.where(lo_q, _swap(ev), od)
                c0 = (4 * kk + jp) * LANES
                o_ref[:, c0:c0 + LANES] = pair.astype(BF)

    return _call("attn_fwd", body, (S // BLK,), _attn_specs(o_q, o_k), [pl.BlockSpec((BLK, ATT_W), lambda n: (n, 0))],
                 [_sds((S, ATT_W), BF)], [z, z, z, z, z, z, qg2, kg2, sink_rows, bias], sem=("parallel",), plan=plan)[0]


def _attn_bwd(z, o_q, o_k, qg2, kg2, sink_rows, bias, dout, plan=None):
    S = z.shape[0]

    def body(zq0, zq1, zkp, zkc, zvp, zvc, qg_ref, kg_ref, sink_ref, bias_ref, do_ref,
             dq_ref, dkp_ref, dkc_ref, dvp_ref, dvc_ref, dl_ref, dsink_ref, dgain_ref):
        n = pl.program_id(0)
        lo_k = _lo_half((2 * BLK, LANES))
        lo_q = _lo_half((BLK, LANES))
        qg = qg_ref[...]
        kg = kg_ref[...]
        khat, kr = _norm2(jnp.concatenate([zkp[...], zkc[...]], axis=0), lo_k)
        kn = khat * kg
        vf = jnp.concatenate([zvp[...], zvc[...]], axis=0)

        @pl.when(n == 0)
        def _():
            dl_ref[...] = jnp.zeros_like(dl_ref)
            dsink_ref[...] = jnp.zeros_like(dsink_ref)
            dgain_ref[...] = jnp.zeros_like(dgain_ref)

        dkn = jnp.zeros((2 * BLK, LANES), F32)
        dvb = jnp.zeros((2 * BLK, LANES), F32)
        dqg = jnp.zeros((1, LANES), F32)
        for kk, zq in enumerate((zq0, zq1)):
            half_k = lo_k if kk == 0 else jnp.logical_not(lo_k)
            K = jnp.where(half_k, kn, 0.0).astype(BF)
            V = jnp.where(half_k, vf, 0.0).astype(BF)
            Q, p, ps, qhats, qrs = _attn_probs(n, kk, 0, 4, zq, K, qg, bias_ref, sink_ref)
            dO = jnp.concatenate([_pair_rows(do_ref[:, (4 * kk + jp) * LANES:(4 * kk + jp + 1) * LANES], kk) for jp in range(4)],
                                 axis=0).astype(BF)
            dP = lax.dot_general(V, dO, NT, preferred_element_type=F32)
            delta = jnp.sum(p * dP, axis=0, keepdims=True)
            dS = p * (dP - delta)
            dsink_ref[kk] += -ps * delta
            dl_ref[kk] += dS
            dSb = dS.astype(BF)
            dvb = dvb + jnp.where(half_k, jnp.dot(p.astype(BF), dO, preferred_element_type=F32), 0.0)
            dkn = dkn + jnp.where(half_k, jnp.dot(dSb, Q, preferred_element_type=F32), 0.0)
            dQ = lax.dot_general(dSb, K, TN, preferred_element_type=F32) * (HEAD_DIM ** -0.5)
            for jp in range(4):
                ev = dQ[(2 * jp) * BLK:(2 * jp + 1) * BLK]
                od = dQ[(2 * jp + 1) * BLK:(2 * jp + 2) * BLK]
                dy = (ev + _swap(od)) if kk == 0 else (_swap(ev) + od)
                dx, gq = _norm2_bwd(dy, qhats[jp], qrs[jp], qg, lo_q)
                dqg = dqg + jnp.sum(gq, axis=0, keepdims=True)
                c0 = (4 * kk + jp) * LANES
                dq_ref[:, c0:c0 + LANES] = dx.astype(BF)
        dk, gk = _norm2_bwd(dkn, khat, kr, kg, lo_k)
        dkp_ref[...] = dk[:BLK]
        dkc_ref[...] = dk[BLK:]
        dvp_ref[...] = dvb[:BLK]
        dvc_ref[...] = dvb[BLK:]
        dgain_ref[...] += jnp.concatenate([dqg, jnp.sum(gk, axis=0, keepdims=True), jnp.zeros((6, LANES), F32)], axis=0)

    blk = pl.BlockSpec((BLK, LANES), lambda n: (n, 0))
    wide = pl.BlockSpec((BLK, ATT_W), lambda n: (n, 0))
    return _call(
        "attn_bwd", body, (S // BLK,), _attn_specs(o_q, o_k) + [wide],
        [wide, blk, blk, blk, blk, pl.BlockSpec((N_KV, 2 * BLK, 8 * BLK), lambda n: (0, 0, 0)),
         pl.BlockSpec((N_KV, 1, 8 * BLK), lambda n: (0, 0, 0)), pl.BlockSpec((8, LANES), lambda n: (0, 0))],
        [_sds((S, ATT_W), BF), _sds((S, LANES), F32), _sds((S, LANES), F32), _sds((S, LANES), F32), _sds((S, LANES), F32),
         _sds((N_KV, 2 * BLK, 8 * BLK), F32), _sds((N_KV, 1, 8 * BLK), F32), _sds((8, LANES), F32)],
        [z, z, z, z, z, z, qg2, kg2, sink_rows, bias, dout], sem=("arbitrary",), plan=plan)


def _kv_combine(dkp, dkc, dvp, dvc):
    S = dkc.shape[0]
    last = S // BLK - 1

    def body(kp_ref, kc_ref, vp_ref, vc_ref, dk_ref, dv_ref):
        more = pl.program_id(0) < last
        dk_ref[...] = (kc_ref[...] + jnp.where(more, kp_ref[...], 0.0)).astype(BF)
        dv_ref[...] = (vc_ref[...] + jnp.where(more, vp_ref[...], 0.0)).astype(BF)

    cur = pl.BlockSpec((BLK, LANES), lambda n: (n, 0))
    nxt = pl.BlockSpec((BLK, LANES), lambda n: (jnp.minimum(n + 1, last), 0))
    return _call("kv_combine", body, (S // BLK,), [nxt, cur, nxt, cur], [cur, cur],
                 [_sds((S, LANES), BF), _sds((S, LANES), BF)], [dkp, dkc, dvp, dvc], sem=("parallel",))


def _merge_fwd(mixed, attn, wpu4, wau4, z, o_ga, plan=None):
    S, PW = mixed.shape
    _, _, CS = wpu4.shape
    D = 4 * CS
    tm = _pick(S, (1024,))
    tn = 256
    nsj = CS // tn
    na = o_ga // tn
    nb = (o_ga + D) // tn

    def body(m_ref, a_ref, wp_ref, wa_ref, ga_ref, gb_ref, mg_ref, yy_ref):
        yp = jnp.dot(m_ref[...], wp_ref[...], preferred_element_type=F32)
        ya = jnp.dot(a_ref[...], wa_ref[...], preferred_element_type=F32)
        mg_ref[...] = (jax.nn.sigmoid(ga_ref[...]) * yp + jax.nn.sigmoid(gb_ref[...]) * ya).astype(BF)
        yy_ref[0] = yp.astype(BF)
        yy_ref[1] = ya.astype(BF)

    return _call("merge_fwd", body, (S // tm, D // tn),
                 [pl.BlockSpec((tm, PW), lambda i, j: (i, 0)), pl.BlockSpec((tm, ATT_W), lambda i, j: (i, 0)),
                  pl.BlockSpec((None, PW, tn), lambda i, j: (j // nsj, 0, j % nsj)),
                  pl.BlockSpec((None, ATT_W, tn), lambda i, j: (j // nsj, 0, j % nsj)),
                  pl.BlockSpec((tm, tn), lambda i, j: (i, na + j)), pl.BlockSpec((tm, tn), lambda i, j: (i, nb + j))],
                 [pl.BlockSpec((tm, tn), lambda i, j: (i, j)), pl.BlockSpec((2, tm, tn), lambda i, j: (0, i, j))],
                 [_sds((S, D), BF), _sds((2, S, D), BF)], [mixed, attn, wpu4, wau4, z, z], sem=("parallel", "parallel"), plan=plan)


def _merge_bwd(do, wo, z, o_ga, yy, plan=None):
    S, D = do.shape
    tm = _pick(S, (1024,))
    tn = 256
    na = o_ga // tn
    nb = (o_ga + D) // tn

    def epi(p, e, o, rs):
        sa = jax.nn.sigmoid(e[0][rs, :])
        sb = jax.nn.sigmoid(e[1][rs, :])
        yp = e[2][0, rs, :].astype(F32)
        ya = e[2][1, rs, :].astype(F32)
        o[0][0, rs, :] = (p * yp * sa * (1.0 - sa)).astype(BF)
        o[0][1, rs, :] = (p * ya * sb * (1.0 - sb)).astype(BF)
        o[1][0, rs, :] = (p * sa).astype(BF)
        o[1][1, rs, :] = (p * sb).astype(BF)

    pair = pl.BlockSpec((2, tm, tn), lambda i, j, k: (0, i, j))
    return _mm("merge_bwd", (S // tm, D // tn, 1), NT,
               do, pl.BlockSpec((tm, D), lambda i, j, k: (i, 0)),
               wo, pl.BlockSpec((tn, D), lambda i, j, k: (j, 0)),
               [z, z, yy], [pl.BlockSpec((tm, tn), lambda i, j, k: (i, na + j)), pl.BlockSpec((tm, tn), lambda i, j, k: (i, nb + j)), pair],
               [_sds((2, S, D), BF), _sds((2, S, D), BF)], [pair, pair], None, epi, plan)


def _mm_up_t(name, dyy, which, w4):
    _, S, D = dyy.shape
    _, K, CS = w4.shape
    tm = _pick(S, (1024,))
    return _mm(name, (S // tm, 1, N_CHIPS), NT,
               dyy, pl.BlockSpec((None, tm, CS), lambda i, j, k: (which, i, k)),
               w4, pl.BlockSpec((None, K, CS), lambda i, j, k: (k, 0, 0)),
               [], [], [_sds((S, K), F32)], [pl.BlockSpec((tm, K), lambda i, j, k: (i, 0))], (tm, K), _store)[0]


def _mm_up_dw(name, a, dyy, which):
    _, S, D = dyy.shape
    K = a.shape[1]
    CS = D // N_CHIPS
    ts = _pick(S, (1024,))
    return _mm(name, (1, N_CHIPS, S // ts), TN,
               a, pl.BlockSpec((ts, K), lambda i, j, k: (k, 0)),
               dyy, pl.BlockSpec((None, ts, CS), lambda i, j, k: (which, k, j)),
               [], [], [_sds((N_CHIPS, K, CS), BF)], [pl.BlockSpec((None, K, CS), lambda i, j, k: (j, 0, 0))], (K, CS), _store)[0]


def _adamw(w, g, m, v):
    m = ADAM_B1 * m + (1.0 - ADAM_B1) * g
    v = ADAM_B2 * v + (1.0 - ADAM_B2) * (g * g)
    m_hat = m / (1.0 - ADAM_B1 ** ADAM_STEP)
    v_hat = v / (1.0 - ADAM_B2 ** ADAM_STEP)
    delta = -ADAM_LR * (m_hat / (jnp.sqrt(v_hat) + ADAM_EPS) + ADAM_WD * w)
    return delta, m, v


def _mod_fwd(c_all, w_ada, b_sh):
    D, cols = w_ada.shape
    tn = cols // 9

    def body(c_ref, w_ref, b_ref, o_ref):
        cv = c_ref[...]
        sc = (cv * jax.nn.sigmoid(cv)).astype(BF)
        o_ref[...] = jnp.dot(sc, w_ref[...].astype(BF), preferred_element_type=F32) + b_ref[...]

    return _call("mod_fwd", body, (9,),
                 [pl.BlockSpec((N_DEV, D), lambda j: (0, 0)), pl.BlockSpec((D, tn), lambda j: (0, j)), pl.BlockSpec((1, tn), lambda j: (0, j))],
                 [pl.BlockSpec((N_DEV, tn), lambda j: (0, j))], [_sds((N_DEV, cols), F32)], [c_all, w_ada, b_sh], sem=("parallel",))[0]


def _wada_bwd(c_all, dmod_sh, w, m, v, plan=None):
    D, cols = w.shape
    tn = cols // 18

    def body(c_ref, d_ref, w_ref, m_ref, v_ref, g_ref, dl_ref, nm_ref, nv_ref):
        cv = c_ref[...]
        sc = (cv * jax.nn.sigmoid(cv)).astype(BF)
        g = lax.dot_general(sc, d_ref[...].astype(BF), TN, preferred_element_type=F32)
        g_ref[...] = g
        dl_ref[...], nm_ref[...], nv_ref[...] = _adamw(w_ref[...], g, m_ref[...], v_ref[...])

    tile = pl.BlockSpec((D, tn), lambda j: (0, j))
    out = _sds((D, cols), F32)
    return _call("wada_bwd", body, (18,),
                 [pl.BlockSpec((N_DEV, D), lambda j: (0, 0)), pl.BlockSpec((N_DEV, tn), lambda j: (0, j)), tile, tile, tile],
                 [tile] * 4, [out] * 4, [c_all, dmod_sh, w, m, v], sem=("parallel",), plan=plan)


def _adam_2d(name, w, g, m, v):
    R, C = w.shape
    tr = _row_tile(R, 256)

    def body(w_ref, g_ref, m_ref, v_ref, dl_ref, nm_ref, nv_ref):
        dl_ref[...], nm_ref[...], nv_ref[...] = _adamw(w_ref[...], g_ref[...], m_ref[...], v_ref[...])

    tile = pl.BlockSpec((tr, C), lambda i: (i, 0))
    out = _sds((R, C), F32)
    return _call(name, body, (R // tr,), [tile] * 4, [tile] * 3, [out] * 3, [w, g, m, v], sem=("parallel",))


def _small_finish(parts, w, m, v):
    _, R, C = parts.shape

    def body(p_ref, w_ref, m_ref, v_ref, g_ref, dl_ref, nm_ref, nv_ref):
        g = p_ref[0]
        for d in range(1, N_DEV):
            g = g + p_ref[d]
        g_ref[...] = g
        dl_ref[...], nm_ref[...], nv_ref[...] = _adamw(w_ref[...], g, m_ref[...], v_ref[...])

    out = _sds((R, C), F32)
    return pl.pallas_call(body, out_shape=[out] * 4, name="small_finish",
                          compiler_params=pltpu.CompilerParams(vmem_limit_bytes=VMEM_LIMIT))(parts, w, m, v)


def _my_chip():
    return 2 * lax.axis_index("x") + lax.axis_index("y")


def _cast_into_slot(name, w):
    R, C = w.shape
    tr = _row_tile(R, 256)

    def body(w_ref, o_ref):
        o_ref[...] = w_ref[...].astype(BF)

    return _call(name, body, (R // tr,), [pl.BlockSpec((tr, C), lambda i: (i, 0))],
                 [pl.BlockSpec((None, tr, C), lambda i: (_my_chip(), i, 0))], [_sds((N_CHIPS, R, C), BF)], [w], sem=("parallel",))[0]


def _add_pair(name, p, q):
    _, H, C = q.shape
    tr = _row_tile(H, 512)
    nt = H // tr

    def body(p_ref, q_ref, o_ref):
        o_ref[...] = (p_ref[...].astype(F32) + q_ref[...].astype(F32)).astype(BF)

    tile = pl.BlockSpec((None, tr, C), lambda k, i: (k, i, 0))
    return _call(name, body, (N_CHIPS, nt), [pl.BlockSpec((None, tr, C), lambda k, i: (k, lax.axis_index("c") * nt + i, 0)), tile],
                 [tile], [_sds(q.shape, BF)], [p, q], sem=("parallel", "parallel"))[0]


def _sum_chips(name, u, t):
    _, H, C = u.shape
    tr = _row_tile(H, 256)

    def body(u_ref, t_ref, o_ref):
        r = _my_chip()
        own = t_ref[...].astype(F32)
        pick = lambda k: jnp.where(r == k, own, u_ref[k].astype(F32))
        o_ref[...] = ((pick(0) + pick(1)) + pick(2)) + pick(3)

    return _call(name, body, (H // tr,),
                 [pl.BlockSpec((N_CHIPS, tr, C), lambda i: (0, i, 0)), pl.BlockSpec((None, tr, C), lambda i: (_my_chip(), i, 0))],
                 [pl.BlockSpec((None, tr, C), lambda i: (lax.axis_index("c"), i, 0))], [_sds((2, H, C), F32)], [u, t],
                 sem=("parallel",))[0]


BIG = ("gu1", "down1", "w_in", "pool_mix", "pool_up", "attn_up", "o", "gu2", "down2")
MIX = ("o", "pool_up", "attn_up", "pool_mix")
EARLY = ("down1", "w_in", "pool_mix", "pool_up", "attn_up", "o")

SCHEDULE = {
    "rms_mod_fwd1": ([("ici", ("gu1",))], []),
    "+gather_gu1_d2d": ([("d2d", ("gu1",))], []),
    "ffn1_up": ([("ici", EARLY)], []),
    "+gather_early_d2d": ([("d2d", EARLY)], []),
    "ffn1_down": ([("ici", ("gu2",))], []),
    "mix_in": ([("ici", ("down2",)), ("d2d", ("gu2",))], []),
    "mix_out": ([("d2d", ("down2",))], []),
    "ffn2_dwd": ([("split", ("gu2",))], [("add", ("gu2",))]),
    "ffn2_dh": ([("owners", ("gu2",)), ("split", ("down2",))], [("add", ("down2",)), ("sum", ("gu2",))]),
    "merge_bwd": ([("owners", ("down2",)), ("join", ("gu2",))], [("sum", ("down2",)), ("adam", ("gu2",))]),
    "attn_bwd": ([("split", MIX), ("join", ("down2",))], [("add", MIX), ("adam", ("down2",))]),
    "mix_dwin": ([("owners", MIX)], [("sum", MIX)]),
    "mix_dh": ([("split", ("w_in",)), ("join", MIX)], [("add", ("w_in",)), ("adam", MIX)]),
    "ffn1_dact": ([("owners", ("w_in",))], [("sum", ("w_in",))]),
    "ffn1_dwd": ([("split", ("gu1",)), ("join", ("w_in",))], [("add", ("gu1",)), ("adam", ("w_in",))]),
    "ffn1_dh": ([("owners", ("gu1",)), ("split", ("down1",))], [("add", ("down1",)), ("sum", ("gu1",))]),
    "rms_mod_bwd1": ([("owners", ("down1",)), ("join", ("gu1",))], [("sum", ("down1",)), ("adam", ("gu1",))]),
    "+join_down1": ([("join", ("down1",))], [("adam", ("down1",))]),
}


class _Plan:
    def __init__(self, w2, m2, v2, full, D, gw):
        self.w2, self.m2, self.v2, self.full, self.D, self.gw = w2, m2, v2, dict(full), D, gw
        self.part, self.got, self.sums, self.landed, self.g = {}, {}, {}, {}, {}
        self.result = {}
        self.pending = {}

    def _make(self, op, names):
        if op == "ici":
            return _gather_ici_stage([self.full[k] for k in names])
        if op == "d2d":
            return _gather_d2d_stage([self.full[k] for k in names])
        if op == "split":
            return _split_stage([self.part[k] for k in names])
        if op == "owners":
            return _owners_stage([self.sums[k] for k in names])
        return _join_stage([self.g[k] for k in names])

    def stages(self, name):
        ops = SCHEDULE.get(name, ([], []))[0]
        return [self._make(*op) for op in ops]

    def done(self, name, outs):
        ops, local = SCHEDULE[name]
        for op, res in zip(ops, outs):
            store = {"ici": self.full, "d2d": self.full, "split": self.got, "owners": self.landed, "join": self.g}[op[0]]
            store.update(zip(op[1], res))
        for op, names in local:
            for k in names:
                if op == "add":
                    self.sums[k] = _add_pair("add_pair_" + k, self.part[k], self.got[k])
                elif op == "sum":
                    self.g[k] = _sum_chips("sum_chips_" + k, self.landed[k], self.sums[k])
                else:
                    g2 = self.g[k].reshape(self.w2[k].shape)
                    self.result[k] = (g2, *_adam_2d("adam_" + k, self.w2[k], g2, self.m2[k], self.v2[k]))

    def alone(self, name):
        self.done(name, _run_stages(name[1:], self.stages(name)))

    def weight(self, k):
        D, gw, f = self.D, self.gw, self.full[k]
        if k in ("down1", "down2", "o"):
            return f.reshape(-1, D)
        if k == "pool_mix":
            return f.reshape(N_CHIPS, 4, gw // N_CHIPS, gw).transpose(1, 0, 2, 3).reshape(4, gw, gw)
        if k == "w_in":
            return f.reshape(-1, D)
        return f

    def partial(self, k, p):
        D, gw = self.D, self.gw
        if k in ("down1", "down2", "o", "w_in"):
            p = p.reshape(N_CHIPS, p.shape[0] // N_CHIPS, p.shape[1])
        elif k == "pool_mix":
            p = p.astype(BF).reshape(4, N_CHIPS, gw // N_CHIPS, gw).transpose(1, 0, 2, 3).reshape(N_CHIPS, gw, gw)
        self.part[k] = p


class _NoComm:
    def __init__(self, weights):
        self.w, self.part = weights, {}

    def stages(self, name):
        return []

    def alone(self, name):
        pass

    def weight(self, k):
        return self.w[k]

    def partial(self, k, p):
        self.part[k] = p


def _row(a, i):
    return a[i:i + 1]


def _local_step(x, target, mod, g_ffn1, g_mix, g_ffn2, pool_scale, q_gain, k_gain, sinks, rel_bias, plan):
    S, D = x.shape
    half = 0.5 * mod
    tile2 = lambda g: jnp.concatenate([g, g], axis=1)
    qg2, kg2 = tile2(q_gain), tile2(k_gain)
    sink_rows = jnp.broadcast_to(sinks.reshape(N_KV, 1, 8, 1), (N_KV, 1, 8, BLK)).reshape(N_KV, 1, 8 * BLK)
    bias = _bias_table(rel_bias).reshape(N_KV, 8, BLK, 2 * BLK).transpose(0, 3, 1, 2).reshape(N_KV, 2 * BLK, 8 * BLK)

    h1 = _rms_mod_fwd("rms_mod_fwd1", x, g_ffn1, _row(mod, 0), _row(mod, 1), plan)
    plan.alone("+gather_gu1_d2d")
    gu1, act1 = _ffn_up("ffn1_up", h1, plan.weight("gu1"), plan)
    plan.alone("+gather_early_d2d")
    x1, f1 = _mm_residual("ffn1_down", act1, plan.weight("down1"), x, _row(half, 2), plan)
    h2 = _rms_mod_fwd("rms_mod_fwd2", x1, g_mix, _row(mod, 3), _row(mod, 4))
    w_in_t = plan.weight("w_in")
    IN_W = w_in_t.shape[0]
    PW = plan.weight("pool_up").shape[1]
    o_q, o_k = PW, PW + ATT_W
    o_ga = o_k + 2 * KV_W
    tnz = _pick(IN_W, (1280, 256))
    tmz = _pick(S, (1024,))
    z = _mm("mix_in", (S // tmz, IN_W // tnz, 1), NT, h2, pl.BlockSpec((tmz, D), lambda i, j, k: (i, 0)),
            w_in_t, pl.BlockSpec((tnz, D), lambda i, j, k: (j, 0)), [], [], [_sds((S, IN_W), F32)],
            [pl.BlockSpec((tmz, tnz), lambda i, j, k: (i, j))], None, _store, plan)[0]
    pooled = _pool_fwd(z, PW)
    mixed = _pool_mix(pooled, plan.weight("pool_mix"), pool_scale)
    attn = _attn_fwd(z, o_q, o_k, qg2, kg2, sink_rows, bias)
    merged, yy = _merge_fwd(mixed, attn, plan.weight("pool_up"), plan.weight("attn_up"), z, o_ga)
    x2, fo = _mm_residual("mix_out", merged, plan.weight("o"), x1, _row(mod, 5), plan)
    h3 = _rms_mod_fwd("rms_mod_fwd3", x2, g_ffn2, _row(mod, 6), _row(mod, 7))
    gu2, act2 = _ffn_up("ffn2_up", h3, plan.weight("gu2"))
    x3, f2 = _mm_residual("ffn2_down", act2, plan.weight("down2"), x2, _row(half, 8))
    dx3, df2, loss_acc = _loss_bwd(x3, target, f2, _row(half, 8))

    dgu2 = _ffn_dact("ffn2_dact", df2, plan.weight("down2"), gu2)
    plan.partial("gu2", _ffn_dwgu("ffn2_dwgu", h3, dgu2))
    plan.partial("down2", _mm_tn("ffn2_dwd", act2, df2, (1408, 512), (1024,), plan))
    dh3 = _ffn_dh("ffn2_dh", dgu2, plan.weight("gu2"), plan)
    dx2, do, acc3 = _rms_mod_bwd("rms_mod_bwd3", dh3, x2, dx3, g_ffn2, _row(mod, 7), fo, _row(mod, 5))

    dgab, dyy = _merge_bwd(do, plan.weight("o"), z, o_ga, yy, plan)
    plan.partial("o", _mm_tn("mix_dwo", merged, do, (1024,), (1024,), plan))
    dmixed = _mm_up_t("pool_up_t", dyy, 0, plan.weight("pool_up"))
    dattn = _mm_up_t("attn_up_t", dyy, 1, plan.weight("attn_up"))
    plan.partial("pool_up", _mm_up_dw("pool_up_dw", mixed, dyy, 0))
    plan.partial("attn_up", _mm_up_dw("attn_up_dw", attn, dyy, 1))
    dpooled, dpm, dps = _pool_mix_bwd(pooled, plan.weight("pool_mix"), pool_scale, dmixed)
    plan.partial("pool_mix", dpm)
    du_pool = _pool_bwd(dpooled)
    dq, dkp, dkc, dvp, dvc, dl, dsink, dgain = _attn_bwd(z, o_q, o_k, qg2, kg2, sink_rows, bias, dattn, plan)
    dk, dv = _kv_combine(dkp, dkc, dvp, dvc)
    drb = _rel_bias_grad(dl.reshape(N_KV, 2 * BLK, 8, BLK).transpose(0, 2, 3, 1).reshape(N_HEADS, BLK * 2 * BLK))
    dz = jnp.concatenate([du_pool, dq, dk, dv, dgab[0], dgab[1]], axis=1)
    plan.partial("w_in", _mm_tn("mix_dwin", dz, h2, (1280, 256), (1024,), plan))
    tnd = _pick(D, (512,))
    dh2 = _mm("mix_dh", (S // tmz, D // tnd, 1), NN, dz, pl.BlockSpec((tmz, IN_W), lambda i, j, k: (i, 0)),
              w_in_t, pl.BlockSpec((IN_W, tnd), lambda i, j, k: (0, j)), [], [], [_sds((S, D), F32)],
              [pl.BlockSpec((tmz, tnd), lambda i, j, k: (i, j))], None, _store, plan)[0]
    dx1, df1, acc2 = _rms_mod_bwd("rms_mod_bwd2", dh2, x1, dx2, g_mix, _row(mod, 4), f1, _row(half, 2))

    dgu1 =_ffn_dact("ffn1_dact", df1, plan.weight("down1"), gu1, plan)
    plan.partial("gu1", _ffn_dwgu("ffn1_dwgu", h1, dgu1, plan))
    plan.partial("down1", _mm_tn("ffn1_dwd", act1, df1, (1408, 512), (1024,), plan))
    dh1 = _ffn_dh("ffn1_dh", dgu1, plan.weight("gu1"), plan)
    grad_x, acc1 = _rms_mod_bwd("rms_mod_bwd1", dh1, x, dx1, g_ffn1, _row(mod, 1), plan=plan)

    dmod = jnp.concatenate([_row(acc1, 0), _row(acc1, 1), 0.5 * _row(acc2, 3),
                            _row(acc2, 0), _row(acc2, 1), _row(acc3, 3),
                            _row(acc3, 0), _row(acc3, 1), 0.5 * _row(loss_acc, 1)], axis=0)
    fold = lambda r: r[:, :HEAD_DIM] + r[:, HEAD_DIM:]
    small = dict(
        dmod=dmod, g_ffn1=_row(acc1, 2), g_mix=_row(acc2, 2), g_ffn2=_row(acc3, 2), pool_scale=_row(dps, 0),
        q_gain=fold(_row(dgain, 0)), k_gain=fold(_row(dgain, 1)),
        sinks=jnp.sum(dsink.reshape(N_HEADS, BLK), axis=1).reshape(1, N_HEADS), rel_bias=drb,
        loss=(0.5 / D) * jnp.sum(_row(loss_acc, 0)).reshape(1, 1))
    return grad_x, small


SMALL_ORDER = ("dmod", "g_ffn1", "g_mix", "g_ffn2", "pool_scale", "q_gain", "k_gain", "sinks", "rel_bias", "loss")


def _pack_small(vals):
    flat = jnp.concatenate([vals[k].reshape(-1) for k in SMALL_ORDER])
    n = flat.shape[0]
    rows = -(-n // (8 * LANES)) * 8
    return jnp.pad(flat, (0, rows * LANES - n)).reshape(rows, LANES)


def _unpack_small(packed, like):
    flat = packed.reshape(-1)
    out, off = {}, 0
    for k in SMALL_ORDER:
        n = int(np.prod(like[k].shape))
        out[k] = flat[off:off + n].reshape(like[k].shape)
        off += n
    return out


def kernel(x, c, w_ada, b_ada, g_ffn1, w_ffn1_gu, w_ffn1_down, g_mix, w_in, pool_mix, pool_scale, w_pool_up, q_gain, k_gain, sinks, rel_bias, w_attn_up, w_o, g_ffn2, w_ffn2_gu, w_ffn2_down, loss_target, m_w_ada, m_b_ada, m_g_ffn1, m_w_ffn1_gu, m_w_ffn1_down, m_g_mix, m_w_in, m_pool_mix, m_pool_scale, m_w_pool_up, m_q_gain, m_k_gain, m_sinks, m_rel_bias, m_w_attn_up, m_w_o, m_g_ffn2, m_w_ffn2_gu, m_w_ffn2_down, v_w_ada, v_b_ada, v_g_ffn1, v_w_ffn1_gu, v_w_ffn1_down, v_g_mix, v_w_in, v_pool_mix, v_pool_scale, v_w_pool_up, v_q_gain, v_k_gain, v_sinks, v_rel_bias, v_w_attn_up, v_w_o, v_g_ffn2, v_w_ffn2_gu, v_w_ffn2_down):
    S, D = x.shape[1], x.shape[2]
    gw = pool_mix.shape[3]
    r = 2 * lax.axis_index("x") + lax.axis_index("y")

    two_d = lambda a: a.reshape(-1, a.shape[-1])
    w_sh = dict(gu1=w_ffn1_gu, down1=w_ffn1_down, w_in=w_in, pool_mix=pool_mix, pool_up=w_pool_up, attn_up=w_attn_up, o=w_o,
                gu2=w_ffn2_gu, down2=w_ffn2_down)
    m_sh = dict(gu1=m_w_ffn1_gu, down1=m_w_ffn1_down, w_in=m_w_in, pool_mix=m_pool_mix, pool_up=m_w_pool_up, attn_up=m_w_attn_up,
                o=m_w_o, gu2=m_w_ffn2_gu, down2=m_w_ffn2_down)
    v_sh = dict(gu1=v_w_ffn1_gu, down1=v_w_ffn1_down, w_in=v_w_in, pool_mix=v_pool_mix, pool_up=v_w_pool_up, attn_up=v_w_attn_up,
                o=v_w_o, gu2=v_w_ffn2_gu, down2=v_w_ffn2_down)
    view = lambda k, a: two_d(a).T if k == "w_in" else two_d(a)
    unview = lambda k, a: (a.T if k == "w_in" else a).reshape(w_sh[k].shape)
    w2 = {k: view(k, w_sh[k]) for k in BIG}
    full = {k: _cast_into_slot("cast_" + k, w2[k]) for k in BIG}
    plan = _Plan(w2, {k: view(k, m_sh[k]) for k in BIG}, {k: view(k, v_sh[k]) for k in BIG}, full, D, gw)

    c_all = _gather_all("gather_c", jnp.broadcast_to(c, (8, D)))[:, 0, :]
    cols = w_ada.shape[2]
    b_sh = lax.dynamic_slice(b_ada, (0, r * cols), (1, cols))
    mod_cols = _mod_fwd(c_all, w_ada[0], b_sh)
    mod_all = _chip_exchange("mod_exchange", mod_cols)
    me = 4 * lax.axis_index("x") + 2 * lax.axis_index("y") + lax.axis_index("c")
    mod = lax.dynamic_slice(mod_all, (0, me, 0), (N_CHIPS, 1, cols)).reshape(9, D)

    grad_x, small = _local_step(x[0], loss_target[0], mod, g_ffn1, g_mix, g_ffn2, pool_scale, q_gain, k_gain,
                                sinks, rel_bias, plan)

    small_w = dict(dmod=b_ada, g_ffn1=g_ffn1, g_mix=g_mix, g_ffn2=g_ffn2, pool_scale=pool_scale, q_gain=q_gain, k_gain=k_gain,
                   sinks=sinks, rel_bias=rel_bias, loss=jnp.zeros((1, 1), F32))
    small_m = dict(dmod=m_b_ada, g_ffn1=m_g_ffn1, g_mix=m_g_mix, g_ffn2=m_g_ffn2, pool_scale=m_pool_scale, q_gain=m_q_gain,
                   k_gain=m_k_gain, sinks=m_sinks, rel_bias=m_rel_bias, loss=jnp.zeros((1, 1), F32))
    small_v = dict(dmod=v_b_ada, g_ffn1=v_g_ffn1, g_mix=v_g_mix, g_ffn2=v_g_ffn2, pool_scale=v_pool_scale, q_gain=v_q_gain,
                   k_gain=v_k_gain, sinks=v_sinks, rel_bias=v_rel_bias, loss=jnp.ones((1, 1), F32))
    small_all = _gather_all("gather_small", _pack_small(small))
    sg, sd, sm, sv = [_unpack_small(a, small_w) for a in
                      _small_finish(small_all, _pack_small(small_w), _pack_small(small_m), _pack_small(small_v))]
    loss = sg["loss"].reshape(())

    dmod_all = small_all.reshape(N_DEV, -1)[:, :9 * D]
    dmod_sh = lax.dynamic_slice(dmod_all, (0, r * cols), (N_DEV, cols))
    g_ada, d_ada, nm_ada, nv_ada = _wada_bwd(c_all, dmod_sh, w_ada[0], m_w_ada[0], v_w_ada[0])
    plan.alone("+join_down1")

    big = [{k: unview(k, plan.result[k][i]) for k in BIG} for i in range(4)]

    def ordered(b, ada, sm_):
        return (ada[None], sm_["dmod"], sm_["g_ffn1"], b["gu1"], b["down1"], sm_["g_mix"], b["w_in"], b["pool_mix"],
                sm_["pool_scale"], b["pool_up"], sm_["q_gain"], sm_["k_gain"], sm_["sinks"], sm_["rel_bias"], b["attn_up"],
                b["o"], sm_["g_ffn2"], b["gu2"], b["down2"])

    return (loss, grad_x[None], *ordered(big[0], g_ada, sg), *ordered(big[1], d_ada, sd), *ordered(big[2], nm_ada, sm),
            *ordered(big[3], nv_ada, sv))
```

```python
import numpy as np
import jax
import jax.numpy as jnp
from jax import lax
from jax.experimental import pallas as pl
from jax.experimental.pallas import tpu as pltpu

BF = jnp.bfloat16
F32 = jnp.float32
MESH = pl.DeviceIdType.MESH

EPS = 1e-6
NEG_INF = -1e30
HEAD_DIM = 64
N_HEADS = 16
N_KV = 2
ATT_W = N_HEADS * HEAD_DIM
KV_W = N_KV * HEAD_DIM
BLK = 128
NUM_BUCKETS = 32
POOL_MAX_W = 16
N_CHIPS = 4
N_DEV = 8
LANES = 128
ADAM_LR, ADAM_B1, ADAM_B2, ADAM_EPS, ADAM_WD, ADAM_STEP = 0.001, 0.9, 0.999, 1e-08, 0.01, 10
VMEM_LIMIT = 52 * 1024 * 1024
ANY = pl.BlockSpec(memory_space=pl.ANY)


def _pick(dim, prefs):
    for p in prefs:
        if p <= dim and dim % p == 0:
            return p
    return dim


def _row_tile(rows, cap):
    return max(t for t in range(16, min(rows, cap) + 1, 16) if rows % t == 0)


def _sds(shape, dtype):
    return jax.ShapeDtypeStruct(tuple(shape), dtype)


def _place():
    return lax.axis_index("x"), lax.axis_index("y"), lax.axis_index("c")


def _other_chips(x, y):
    return [(1 - x, y), (x, 1 - y), (1 - x, 1 - y)]


def _chip_of(chip):
    return 2 * chip[0] + chip[1]


def _half_rows(ref, lead, cc, h):
    return ref.at[lead, pl.ds(pl.multiple_of(cc * h, 16), h), :]


class _Stage:
    def __init__(self, bufs, outs, alias, n_sem, start, wait):
        self.bufs, self.outs, self.alias, self.n_sem, self.start, self.wait = bufs, outs, alias, n_sem, start, wait


def _stage_plumbing(stages, n_in0, n_out0):
    bufs, outs, aliases, spans, scratch = [], [], {}, [], []
    for st in stages:
        i0, o0 = len(bufs), len(outs)
        bufs += list(st.bufs)
        outs += list(st.outs)
        for a, b in st.alias.items():
            aliases[n_in0 + i0 + a] = n_out0 + o0 + b
        spans.append((i0, len(bufs), o0, len(outs)))
        scratch += [pltpu.SemaphoreType.DMA((st.n_sem,)), pltpu.SemaphoreType.DMA((st.n_sem,))]

    def run(which, in_refs, out_refs, sem_refs):
        for s, st in enumerate(stages):
            i0, i1, o0, o1 = spans[s]
            getattr(st, which)(in_refs[i0:i1], out_refs[o0:o1], sem_refs[2 * s], sem_refs[2 * s + 1])

    def split(flat):
        return [list(flat[o0:o1]) for (_, _, o0, o1) in spans]

    return bufs, outs, aliases, scratch, run, split


def _run_stages(name, stages):
    bufs, outs, aliases, scratch, run, split = _stage_plumbing(stages, 0, 0)
    ni, no = len(bufs), len(outs)

    def body(*refs):
        ins, os_, sems = refs[:ni], refs[ni:ni + no], refs[ni + no:]
        run("start", ins, os_, sems)
        run("wait", ins, os_, sems)

    res = pl.pallas_call(body, in_specs=[ANY] * ni, out_specs=[ANY] * no, out_shape=outs, input_output_aliases=aliases,
                         scratch_shapes=scratch, name=name)(*bufs)
    return split(res)


def _call(name, body, grid, in_specs, out_specs, out_shape, args, scratch=(), sem=None, plan=None):
    stages = plan.stages(name) if plan is not None else []
    n_in, n_out, n_scr = len(args), len(out_shape), len(scratch)
    if not stages:
        return pl.pallas_call(body, grid=grid, in_specs=list(in_specs), out_specs=list(out_specs), out_shape=list(out_shape),
                              scratch_shapes=list(scratch), name=name,
                              compiler_params=pltpu.CompilerParams(dimension_semantics=sem, vmem_limit_bytes=VMEM_LIMIT))(*args)
    bufs, s_outs, aliases, s_scratch, run, split = _stage_plumbing(stages, n_in, n_out)
    nb, nso = len(bufs), len(s_outs)

    def hosted(*refs):
        ins = refs[:n_in]
        s_ins = refs[n_in:n_in + nb]
        outs = refs[n_in + nb:n_in + nb + n_out]
        s_os = refs[n_in + nb + n_out:n_in + nb + n_out + nso]
        scr = refs[n_in + nb + n_out + nso:n_in + nb + n_out + nso + n_scr]
        sems = refs[n_in + nb + n_out + nso + n_scr:]
        first = pl.program_id(0) == 0
        last = pl.program_id(0) == grid[0] - 1
        for d in range(1, len(grid)):
            first = first & (pl.program_id(d) == 0)
            last = last & (pl.program_id(d) == grid[d] - 1)

        @pl.when(first)
        def _():
            run("start", s_ins, s_os, sems)

        body(*ins, *outs, *scr)

        @pl.when(last)
        def _():
            run("wait", s_ins, s_os, sems)

    res = pl.pallas_call(
        hosted, grid=grid, in_specs=list(in_specs) + [ANY] * nb, out_specs=list(out_specs) + [ANY] * nso,
        out_shape=list(out_shape) + s_outs, input_output_aliases=aliases, scratch_shapes=list(scratch) + s_scratch, name=name,
        compiler_params=pltpu.CompilerParams(dimension_semantics=("arbitrary",) * len(grid), vmem_limit_bytes=VMEM_LIMIT))(*args, *bufs)
    plan.done(name, split(res[n_out:]))
    return list(res[:n_out])


def _gather_ici_stage(fulls):
    n = len(fulls)

    def copy(i, j, slot, ins, outs, send, recv):
        x, y, c = _place()
        chip = _other_chips(x, y)[j]
        h = fulls[i].shape[1] // 2
        s = 3 * i + j
        return pltpu.make_async_remote_copy(_half_rows(ins[i], 2 * x + y, c, h), _half_rows(outs[i], slot(x, y, chip), c, h),
                                            send.at[s], recv.at[s], device_id=(*chip, c), device_id_type=MESH)

    mine = lambda x, y, chip: 2 * x + y
    theirs = lambda x, y, chip: _chip_of(chip)

    def start(ins, outs, send, recv):
        for i in range(n):
            for j in range(3):
                copy(i, j, mine, ins, outs, send, recv).start()

    def wait(ins, outs, send, recv):
        for i in range(n):
            for j in range(3):
                copy(i, j, theirs, ins, outs, send, recv).wait_recv()
        for i in range(n):
            for j in range(3):
                copy(i, j, mine, ins, outs, send, recv).wait_send()

    return _Stage(fulls, [_sds(f.shape, f.dtype) for f in fulls], {i: i for i in range(n)}, 3 * n, start, wait)


def _gather_d2d_stage(fulls):
    n = len(fulls)

    def copy(i, j, cc, ins, outs, send, recv):
        x, y, c = _place()
        rj = _chip_of(_other_chips(x, y)[j])
        h = fulls[i].shape[1] // 2
        half = cc(c)
        s = 3 * i + j
        return pltpu.make_async_remote_copy(_half_rows(ins[i], rj, half, h), _half_rows(outs[i], rj, half, h),
                                            send.at[s], recv.at[s], device_id=(x, y, 1 - c), device_id_type=MESH)

    mine = lambda c: c
    theirs = lambda c: 1 - c

    def start(ins, outs, send, recv):
        for i in range(n):
            for j in range(3):
                copy(i, j, mine, ins, outs, send, recv).start()

    def wait(ins, outs, send, recv):
        for i in range(n):
            for j in range(3):
                copy(i, j, theirs, ins, outs, send, recv).wait_recv()
        for i in range(n):
            for j in range(3):
                copy(i, j, mine, ins, outs, send, recv).wait_send()

    return _Stage(fulls, [_sds(f.shape, f.dtype) for f in fulls], {i: i for i in range(n)}, 3 * n, start, wait)


def _split_stage(parts):
    n = len(parts)

    def copy(i, ins, outs, send, recv):
        x, y, c = _place()
        h = parts[i].shape[1] // 2
        return pltpu.make_async_remote_copy(_half_rows(ins[i], slice(None), 1 - c, h), outs[i], send.at[i], recv.at[i],
                                            device_id=(x, y, 1 - c), device_id_type=MESH)

    def start(ins, outs, send, recv):
        for i in range(n):
            copy(i, ins, outs, send, recv).start()

    def wait(ins, outs, send, recv):
        for i in range(n):
            copy(i, ins, outs, send, recv).wait_recv()
        for i in range(n):
            copy(i, ins, outs, send, recv).wait_send()

    return _Stage(parts, [_sds((N_CHIPS, p.shape[1] // 2, p.shape[2]), p.dtype) for p in parts], {}, n, start, wait)


def _owners_stage(sums):
    n = len(sums)

    def copy(i, j, mine, ins, outs, send, recv):
        x, y, c = _place()
        chip = _other_chips(x, y)[j]
        slot = (2 * x + y) if mine else _chip_of(chip)
        return pltpu.make_async_remote_copy(ins[i].at[_chip_of(chip)], outs[i].at[slot], send.at[3 * i + j], recv.at[3 * i + j],
                                            device_id=(*chip, c), device_id_type=MESH)

    def start(ins, outs, send, recv):
        for i in range(n):
            for j in range(3):
                copy(i, j, True, ins, outs, send, recv).start()

    def wait(ins, outs, send, recv):
        for i in range(n):
            for j in range(3):
                copy(i, j, False, ins, outs, send, recv).wait_recv()
        for i in range(n):
            for j in range(3):
                copy(i, j, True, ins, outs, send, recv).wait_send()

    return _Stage(sums, [_sds(s.shape, s.dtype) for s in sums], {}, 3 * n, start, wait)


def _join_stage(gs):
    n = len(gs)

    def copy(i, mine, ins, outs, send, recv):
        x, y, c = _place()
        slot = c if mine else 1 - c
        return pltpu.make_async_remote_copy(ins[i].at[slot], outs[i].at[slot], send.at[i], recv.at[i],
                                            device_id=(x, y, 1 - c), device_id_type=MESH)

    def start(ins, outs, send, recv):
        for i in range(n):
            copy(i, True, ins, outs, send, recv).start()

    def wait(ins, outs, send, recv):
        for i in range(n):
            copy(i, False, ins, outs, send, recv).wait_recv()
        for i in range(n):
            copy(i, True, ins, outs, send, recv).wait_send()

    return _Stage(gs, [_sds(g.shape, g.dtype) for g in gs], {i: i for i in range(n)}, n, start, wait)


def _chip_exchange(name, arr):
    def body(src, dst, send, recv, loc):
        x, y, c = _place()
        r = 2 * x + y
        chips = _other_chips(x, y)

        def cp(j, slot):
            return pltpu.make_async_remote_copy(src, dst.at[slot], send.at[j], recv.at[j], device_id=(*chips[j], c), device_id_type=MESH)

        mine = pltpu.make_async_copy(src, dst.at[r], loc)
        mine.start()
        for j in range(3):
            cp(j, r).start()
        for j in range(3):
            cp(j, _chip_of(chips[j])).wait_recv()
        for j in range(3):
            cp(j, r).wait_send()
        mine.wait()

    return pl.pallas_call(body, in_specs=[ANY], out_specs=ANY, out_shape=_sds((N_CHIPS, *arr.shape), arr.dtype),
                          scratch_shapes=[pltpu.SemaphoreType.DMA((3,)), pltpu.SemaphoreType.DMA((3,)), pltpu.SemaphoreType.DMA],
                          name=name)(arr)


def _gather_all(name, arr):
    def body(src, dst, send, recv, loc):
        x, y, c = _place()

        def cp(k, slot_of_me):
            px, py, pc = x ^ ((k >> 2) & 1), y ^ ((k >> 1) & 1), c ^ (k & 1)
            slot = (4 * x + 2 * y + c) if slot_of_me else (4 * px + 2 * py + pc)
            return pltpu.make_async_remote_copy(src, dst.at[slot], send.at[k - 1], recv.at[k - 1],
                                                device_id=(px, py, pc), device_id_type=MESH)

        mine = pltpu.make_async_copy(src, dst.at[4 * x + 2 * y + c], loc)
        mine.start()
        for k in range(1, N_DEV):
            cp(k, True).start()
        for k in range(1, N_DEV):
            cp(k, False).wait_recv()
        for k in range(1, N_DEV):
            cp(k, True).wait_send()
        mine.wait()

    return pl.pallas_call(body, in_specs=[ANY], out_specs=ANY, out_shape=_sds((N_DEV, *arr.shape), arr.dtype),
                          scratch_shapes=[pltpu.SemaphoreType.DMA((N_DEV - 1,)), pltpu.SemaphoreType.DMA((N_DEV - 1,)), pltpu.SemaphoreType.DMA],
                          name=name)(arr)


NN = (((1,), (0,)), ((), ()))
NT = (((1,), (1,)), ((), ()))
TN = (((0,), (0,)), ((), ()))


ALL = slice(None)


def _mm(name, grid, dims, a, a_spec, b, b_spec, extras, extra_specs, out_shapes, out_specs, acc_shape, epilogue, plan=None):
    n_k = grid[2]
    n_e = len(extras)
    n_o = len(out_shapes)

    def body(*refs):
        a_ref, b_ref = refs[0], refs[1]
        e_refs = refs[2:2 + n_e]
        o_refs = refs[2 + n_e:2 + n_e + n_o]
        p = lax.dot_general(a_ref[...].astype(BF), b_ref[...].astype(BF), dims, preferred_element_type=F32)
        if n_k == 1:
            epilogue(p, e_refs, o_refs, ALL)
        else:
            acc = refs[-1]
            k = pl.program_id(2)

            @pl.when(k == 0)
            def _():
                acc[...] = p

            @pl.when(k > 0)
            def _():
                acc[...] += p

            @pl.when(k == n_k - 1)
            def _():
                epilogue(acc[...], e_refs, o_refs, ALL)

    scratch = [] if n_k == 1 else [pltpu.VMEM(acc_shape, F32)]
    return _call(name, body, grid, [a_spec, b_spec, *extra_specs], out_specs, out_shapes, [a, b, *extras], scratch,
                 ("parallel", "parallel", "arbitrary"), plan)


def _store(p, e, o, rs):
    o[0][rs, :] = p.astype(o[0].dtype)


def _rms_mod_fwd(name, x, gain, shift, scale, plan=None):
    S, D = x.shape
    ts = _pick(S, (512,))

    def body(x_ref, g_ref, sh_ref, sc_ref, h_ref):
        xv = x_ref[...]
        r = lax.rsqrt(jnp.mean(xv * xv, axis=-1, keepdims=True) + EPS)
        n = xv * r * g_ref[...]
        h_ref[...] = (n * (1.0 + sc_ref[...]) + sh_ref[...]).astype(BF)

    row = pl.BlockSpec((ts, D), lambda i: (i, 0))
    vec = pl.BlockSpec((1, D), lambda i: (0, 0))
    return _call(name, body, (S // ts,), [row, vec, vec, vec], [row], [_sds((S, D), BF)], [x, gain, shift, scale],
                 sem=("parallel",), plan=plan)[0]


def _acc_rows(acc_ref, first, part):
    @pl.when(first)
    def _():
        acc_ref[...] = part

    @pl.when(jnp.logical_not(first))
    def _():
        acc_ref[...] += part


def _rms_mod_bwd(name, dh, x, dres, gain, scale, f=None, coef=None, plan=None):
    S, D = x.shape
    ts = _pick(S, (256,))
    gated = f is not None

    def body(dh_ref, x_ref, dr_ref, g_ref, sc_ref, *rest):
        xv = x_ref[...]
        dhv = dh_ref[...]
        g = g_ref[...]
        r = lax.rsqrt(jnp.mean(xv * xv, axis=-1, keepdims=True) + EPS)
        xhat = xv * r
        dn = dhv * (1.0 + sc_ref[...])
        dxhat = dn * g
        dx = dr_ref[...] + r * (dxhat - xhat * jnp.mean(dxhat * xhat, axis=-1, keepdims=True))
        rows = [jnp.sum(dhv, axis=0, keepdims=True), jnp.sum(dhv * (xhat * g), axis=0, keepdims=True),
                jnp.sum(dn * xhat, axis=0, keepdims=True)]
        if gated:
            f_ref, c_ref, dx_ref, df_ref, acc_ref = rest
            df_ref[...] = (dx * c_ref[...]).astype(BF)
            rows.append(jnp.sum(dx * f_ref[...].astype(F32), axis=0, keepdims=True))
        else:
            dx_ref, acc_ref = rest
        dx_ref[...] = dx
        _acc_rows(acc_ref, pl.program_id(0) == 0, jnp.concatenate(rows + [jnp.zeros((8 - len(rows), D), F32)], axis=0))

    row = pl.BlockSpec((ts, D), lambda i: (i, 0))
    vec = pl.BlockSpec((1, D), lambda i: (0, 0))
    acc = pl.BlockSpec((8, D), lambda i: (0, 0))
    if gated:
        return _call(name, body, (S // ts,), [row, row, row, vec, vec, row, vec], [row, row, acc],
                     [_sds((S, D), F32), _sds((S, D), BF), _sds((8, D), F32)], [dh, x, dres, gain, scale, f, coef],
                     sem=("arbitrary",), plan=plan)
    return _call(name, body, (S // ts,), [row, row, row, vec, vec], [row, acc],
                 [_sds((S, D), F32), _sds((8, D), F32)], [dh, x, dres, gain, scale], sem=("arbitrary",), plan=plan)


def _loss_bwd(x3, target, f, coef):
    S, D = x3.shape
    ts = _pick(S, (512,))

    def body(x_ref, t_ref, f_ref, c_ref, dx_ref, df_ref, acc_ref):
        e = x_ref[...] - t_ref[...]
        dx = e * (1.0 / D)
        dx_ref[...] = dx
        df_ref[...] = (dx * c_ref[...]).astype(BF)
        part = jnp.concatenate([jnp.sum(e * e, axis=0, keepdims=True), jnp.sum(dx * f_ref[...].astype(F32), axis=0, keepdims=True),
                                jnp.zeros((6, D), F32)], axis=0)
        _acc_rows(acc_ref, pl.program_id(0) == 0, part)

    row = pl.BlockSpec((ts, D), lambda i: (i, 0))
    return _call("loss_bwd", body, (S // ts,), [row, row, row, pl.BlockSpec((1, D), lambda i: (0, 0))],
                 [row, row, pl.BlockSpec((8, D), lambda i: (0, 0))],
                 [_sds((S, D), F32), _sds((S, D), BF), _sds((8, D), F32)], [x3, target, f, coef], sem=("arbitrary",))


def _silu_parts(g):
    s = jax.nn.sigmoid(g)
    return s, g * s


def _ffn_up(name, h, wgu4, plan=None):
    S, D = h.shape
    SH = wgu4.shape[2]
    F = 2 * SH
    tm = _pick(S, (512,))
    tn = _pick(SH, (1408, 256))
    nts = SH // tn

    def body(h_ref, wg_ref, wu_ref, gu_ref, act_ref):
        hv = h_ref[...]
        g = jnp.dot(hv, wg_ref[...], preferred_element_type=F32)
        u = jnp.dot(hv, wu_ref[...], preferred_element_type=F32)
        gu_ref[0] = g.astype(BF)
        gu_ref[1] = u.astype(BF)
        act_ref[...] = (_silu_parts(g)[1] * u).astype(BF)

    return _call(name, body, (S // tm, F // tn),
                 [pl.BlockSpec((tm, D), lambda i, j: (i, 0)),
                  pl.BlockSpec((None, D, tn), lambda i, j: (j // nts, 0, j % nts)),
                  pl.BlockSpec((None, D, tn), lambda i, j: (2 + j // nts, 0, j % nts))],
                 [pl.BlockSpec((2, tm, tn), lambda i, j: (0, i, j)), pl.BlockSpec((tm, tn), lambda i, j: (i, j))],
                 [_sds((2, S, F), BF), _sds((S, F), BF)], [h, wgu4, wgu4], sem=("parallel", "parallel"), plan=plan)


def _mm_residual(name, a, w, x_in, coef, plan=None):
    S, K = a.shape
    D = w.shape[1]
    tm = _pick(S, (1024,))
    tn = _pick(D, (512,))
    tk = K

    def epi(p, e, o, rs):
        o[0][rs, :] = e[0][rs, :] + e[1][...] * p
        o[1][rs, :] = p.astype(BF)

    tile = pl.BlockSpec((tm, tn), lambda i, j, k: (i, j))
    return _mm(name, (S // tm, D // tn, K // tk), NN,
               a, pl.BlockSpec((tm, tk), lambda i, j, k: (i, k)),
               w, pl.BlockSpec((tk, tn), lambda i, j, k: (k, j)),
               [x_in, coef], [tile, pl.BlockSpec((1, tn), lambda i, j, k: (0, j))],
               [_sds((S, D), F32), _sds((S, D), BF)], [tile, tile], (tm, tn), epi, plan)


def _ffn_dact(name, df, wd, gu, plan=None):
    S, D = df.shape
    F = wd.shape[0]
    tm = _pick(S, (512,))
    tn = _pick(F, (1408, 256))

    def epi(p, e, o, rs):
        g = e[0][0, rs, :].astype(F32)
        u = e[0][1, rs, :].astype(F32)
        s, sg = _silu_parts(g)
        o[0][0, rs, :] = (p * u * (s * (1.0 + g * (1.0 - s)))).astype(BF)
        o[0][1, rs, :] = (p * sg).astype(BF)

    pair = pl.BlockSpec((2, tm, tn), lambda i, j, k: (0, i, j))
    return _mm(name, (S // tm, F // tn, 1), NT,
               df, pl.BlockSpec((tm, D), lambda i, j, k: (i, 0)),
               wd, pl.BlockSpec((tn, D), lambda i, j, k: (j, 0)),
               [gu], [pair], [_sds((2, S, F), BF)], [pair], None, epi, plan)[0]


def _ffn_dh(name, dgu, wgu4, plan=None):
    _, S, F = dgu.shape
    _, D, SH = wgu4.shape
    tm = _pick(S, (1024,))
    tn = _pick(D, (512,))

    def body(a_ref, b_ref, o_ref, acc):
        k = pl.program_id(2)
        p = lax.dot_general(a_ref[:, :SH], b_ref[0], NT, preferred_element_type=F32)
        p = p + lax.dot_general(a_ref[:, SH:], b_ref[1], NT, preferred_element_type=F32)

        @pl.when(k == 0)
        def _():
            acc[...] = p

        @pl.when(k == 1)
        def _():
            o_ref[...] = acc[...] + p

    return _call(name, body, (S // tm, D // tn, 2),
                 [pl.BlockSpec((None, tm, F), lambda i, j, k: (k, i, 0)), pl.BlockSpec((2, tn, SH), lambda i, j, k: (k, j, 0))],
                 [pl.BlockSpec((tm, tn), lambda i, j, k: (i, j))], [_sds((S, D), F32)], [dgu, wgu4],
                 [pltpu.VMEM((tm, tn), F32)], ("parallel", "parallel", "arbitrary"), plan)[0]


def _ffn_dwgu(name, h, dgu, plan=None):
    _, S, F = dgu.shape
    D = h.shape[1]
    SH = F // 2
    tk1 = _pick(D, (512,))
    tn = _pick(SH, (1408, 256))
    ts = _pick(S, (4096, 1024))
    npj = F // tn
    nsj = SH // tn
    return _mm(name, (D // tk1, 2 * npj, S // ts), TN,
               h, pl.BlockSpec((ts, tk1), lambda i, j, k: (k, i)),
               dgu, pl.BlockSpec((None, ts, tn), lambda i, j, k: (j // npj, k, j % npj)),
               [], [], [_sds((4, D, SH), BF)],
               [pl.BlockSpec((None, tk1, tn), lambda i, j, k: (j // nsj, i, j % nsj))], (tk1, tn), _store, plan)[0]


def _mm_tn(name, a, b, tk1_prefs, tn_prefs, plan=None):
    S, K1 = a.shape
    N = b.shape[1]
    tk1 = _pick(K1, tk1_prefs)
    tn = _pick(N, tn_prefs)
    ts = _pick(S, (4096, 1024))
    return _mm(name, (K1 // tk1, N // tn, S // ts), TN,
               a, pl.BlockSpec((ts, tk1), lambda i, j, k: (k, i)),
               b, pl.BlockSpec((ts, tn), lambda i, j, k: (k, j)),
               [], [], [_sds((K1, N), BF)], [pl.BlockSpec((tk1, tn), lambda i, j, k: (i, j))], (tk1, tn), _store, plan)[0]


def _pool_window(ext, w, back):
    n = ext.shape[0]
    s = ext
    for step in (1, 2, 4, 8):
        sh = pltpu.roll(s, (n - step) if back else step, axis=0)
        s = jnp.where(w > step, s + sh, s)
    return s


def _pool_fwd(z, PW):
    S = z.shape[0]
    tc = _pick(S, (1024,))
    bpg = (PW // 4) // LANES
    H = POOL_MAX_W

    def body(prev_ref, u_ref, o_ref):
        i = pl.program_id(0)
        j = pl.program_id(1)
        w = lax.shift_left(jnp.int32(2), j // bpg)
        u = u_ref[...]
        prev = jnp.where(i > 0, prev_ref[...], 0.0)
        s = _pool_window(jnp.concatenate([prev, u], axis=0), w, False)[H:]
        t = i * tc + lax.broadcasted_iota(jnp.int32, (tc, LANES), 0)
        cnt = jnp.minimum(t + 1, w).astype(F32)
        o_ref[...] = (s / cnt - u).astype(BF)

    r = tc // H
    return _call("pool_fwd", body, (S // tc, PW // LANES),
                 [pl.BlockSpec((H, LANES), lambda i, j: (jnp.maximum(i * r - 1, 0), j)),
                  pl.BlockSpec((tc, LANES), lambda i, j: (i, j))],
                 [pl.BlockSpec((tc, LANES), lambda i, j: (i, j))], [_sds((S, PW), BF)], [z, z], sem=("parallel", "parallel"))[0]


def _pool_bwd(dpooled):
    S, PW = dpooled.shape
    tc = _pick(S, (1024,))
    bpg = (PW // 4) // LANES
    H = POOL_MAX_W
    last = S // tc - 1

    def body(dp_ref, nxt_ref, o_ref):
        i = pl.program_id(0)
        j = pl.program_id(1)
        w = lax.shift_left(jnp.int32(2), j // bpg)
        dp = dp_ref[...]
        nxt = jnp.where(i < last, nxt_ref[...], 0.0)
        ext = jnp.concatenate([dp, nxt], axis=0)
        t = i * tc + lax.broadcasted_iota(jnp.int32, (tc + H, LANES), 0)
        cnt = jnp.minimum(t + 1, w).astype(F32)
        s = _pool_window(ext / cnt, w, True)[:tc]
        o_ref[...] = (s - dp).astype(BF)

    r = tc // H
    nh = S // H - 1
    return _call("pool_bwd", body, (S // tc, PW // LANES),
                 [pl.BlockSpec((tc, LANES), lambda i, j: (i, j)),
                  pl.BlockSpec((H, LANES), lambda i, j: (jnp.minimum((i + 1) * r, nh), j))],
                 [pl.BlockSpec((tc, LANES), lambda i, j: (i, j))], [_sds((S, PW), BF)], [dpooled, dpooled],
                 sem=("parallel", "parallel"))[0]


def _pool_mix(pooled, pm, scale):
    S, PW = pooled.shape
    gw = PW // 4
    ts = _pick(S, (1024,))

    def epi(p, e, o, rs):
        o[0][rs, :] = (p * e[0][...]).astype(BF)

    tile = pl.BlockSpec((ts, gw), lambda i, j, k: (i, j))
    return _mm("pool_mix", (S // ts, 4, 1), NN, pooled, tile,
               pm, pl.BlockSpec((None, gw, gw), lambda i, j, k: (j, 0, 0)),
               [scale], [pl.BlockSpec((1, gw), lambda i, j, k: (0, j))], [_sds((S, PW), BF)], [tile], None, epi)[0]


def _pool_mix_bwd(pooled, pm, scale, dmixed):
    S, PW = pooled.shape
    gw = PW // 4
    ts = _pick(S, (1024,))

    def body(p_ref, pm_ref, sc_ref, dm_ref, dp_ref, dpm_ref, dsc_ref):
        i = pl.program_id(1)
        p = p_ref[...]
        w = pm_ref[...]
        dm = dm_ref[...]
        pre = jnp.dot(p, w, preferred_element_type=F32)
        dmp = (dm * sc_ref[...]).astype(BF)
        dp_ref[...] = lax.dot_general(dmp, w, NT, preferred_element_type=F32)
        dw = lax.dot_general(p, dmp, TN, preferred_element_type=F32)
        ds = jnp.concatenate([jnp.sum(dm * pre, axis=0, keepdims=True), jnp.zeros((7, gw), F32)], axis=0)
        _acc_rows(dpm_ref, i == 0, dw)
        _acc_rows(dsc_ref, i == 0, ds)

    tile = pl.BlockSpec((ts, gw), lambda g, i: (i, g))
    return _call("pool_mix_bwd", body, (4, S // ts),
                 [tile, pl.BlockSpec((None, gw, gw), lambda g, i: (g, 0, 0)), pl.BlockSpec((1, gw), lambda g, i: (0, g)), tile],
                 [tile, pl.BlockSpec((None, gw, gw), lambda g, i: (g, 0, 0)), pl.BlockSpec((8, gw), lambda g, i: (0, g))],
                 [_sds((S, PW), F32), _sds((4, gw, gw), F32), _sds((8, PW), F32)], [pooled, pm, scale, dmixed],
                 sem=("parallel", "arbitrary"))


def _bucket_onehot():
    ql = np.arange(BLK)[:, None]
    j = np.arange(2 * BLK)[None, :]
    d = BLK + ql - j
    n = np.clip(d, 0, None)
    nf = np.maximum(n, 1).astype(np.float32)
    max_exact = NUM_BUCKETS // 2
    large = max_exact + (np.log(nf / max_exact) / np.log(BLK / max_exact) * (NUM_BUCKETS - max_exact)).astype(np.int32)
    large = np.minimum(large, NUM_BUCKETS - 1)
    bucket = np.where(n < max_exact, n, large).astype(np.int32)
    valid = (d >= 0) & (d < BLK)
    oh = (bucket[None] == np.arange(NUM_BUCKETS)[:, None, None]) & valid[None]
    return oh.reshape(NUM_BUCKETS, BLK * 2 * BLK)


def _three_bf16(v):
    hi = v.astype(BF)
    r1 = v - hi.astype(F32)
    mid = r1.astype(BF)
    lo = (r1 - mid.astype(F32)).astype(BF)
    return hi, mid, lo


def _bias_table(rel_bias):
    oh = jnp.asarray(_bucket_onehot(), BF)
    tn = 4096

    def body(rb_ref, oh_ref, o_ref):
        o = oh_ref[...]
        hi, mid, lo = _three_bf16(rb_ref[...])
        acc = lax.dot_general(hi, o, TN, preferred_element_type=F32)
        acc = acc + lax.dot_general(mid, o, TN, preferred_element_type=F32)
        acc = acc + lax.dot_general(lo, o, TN, preferred_element_type=F32)
        on_band = jnp.sum(o.astype(F32), axis=0, keepdims=True) > 0.5
        o_ref[...] = jnp.where(on_band, acc, NEG_INF)

    n = oh.shape[1]
    return _call("bias_table", body, (n // tn,),
                 [pl.BlockSpec((NUM_BUCKETS, N_HEADS), lambda i: (0, 0)), pl.BlockSpec((NUM_BUCKETS, tn), lambda i: (0, i))],
                 [pl.BlockSpec((N_HEADS, tn), lambda i: (0, i))], [_sds((N_HEADS, n), F32)], [rel_bias, oh], sem=("parallel",))[0]


def _rel_bias_grad(dl):
    oh = jnp.asarray(_bucket_onehot(), BF)
    n = oh.shape[1]
    tk = 4096

    def body(dl_ref, oh_ref, o_ref):
        o = oh_ref[...]
        hi, mid, lo = _three_bf16(dl_ref[...])
        acc = lax.dot_general(o, hi, NT, preferred_element_type=F32)
        acc = acc + lax.dot_general(o, mid, NT, preferred_element_type=F32)
        acc = acc + lax.dot_general(o, lo, NT, preferred_element_type=F32)
        _acc_rows(o_ref, pl.program_id(0) == 0, acc)

    return _call("rel_bias_grad", body, (n // tk,),
                 [pl.BlockSpec((N_HEADS, tk), lambda i: (0, i)), pl.BlockSpec((NUM_BUCKETS, tk), lambda i: (0, i))],
                 [pl.BlockSpec((NUM_BUCKETS, N_HEADS), lambda i: (0, 0))], [_sds((NUM_BUCKETS, N_HEADS), F32)], [dl, oh],
                 sem=("arbitrary",))[0]


def _lo_half(shape):
    return lax.broadcasted_iota(jnp.int32, shape, 1) < HEAD_DIM


def _half_sum(x, lo):
    s_lo = jnp.sum(jnp.where(lo, x, 0.0), axis=-1, keepdims=True)
    s_hi = jnp.sum(jnp.where(lo, 0.0, x), axis=-1, keepdims=True)
    return jnp.where(lo, s_lo, s_hi)


def _norm2(x, lo):
    r = lax.rsqrt(_half_sum(x * x, lo) * (1.0 / HEAD_DIM) + EPS)
    return x * r, r


def _norm2_bwd(dy, xhat, r, gain, lo):
    dxhat = dy * gain
    dx = r * (dxhat - xhat * (_half_sum(dxhat * xhat, lo) * (1.0 / HEAD_DIM)))
    return dx, dy * xhat


def _swap(x):
    return pltpu.roll(x, HEAD_DIM, axis=1)


def _pair_rows(x, kk):
    return jnp.concatenate([x, _swap(x)] if kk == 0 else [_swap(x), x], axis=0)


def _attn_probs(n, kk, jp0, npairs, zq_ref, K, qg, bias_ref, sink_ref):
    lo_q = _lo_half((BLK, LANES))
    rows, qhats, qrs = [], [], []
    for jp in range(jp0, jp0 + npairs):
        qhat, qr = _norm2(zq_ref[:, jp * LANES:(jp + 1) * LANES], lo_q)
        rows.append(_pair_rows(qhat * qg * (HEAD_DIM ** -0.5), kk))
        qhats.append(qhat)
        qrs.append(qr)
    Q = jnp.concatenate(rows, axis=0).astype(BF)
    cols = slice(jp0 * 2 * BLK, (jp0 + npairs) * 2 * BLK)
    l = lax.dot_general(K, Q, NT, preferred_element_type=F32) + bias_ref[kk, :, cols]
    l = jnp.concatenate([jnp.where(n == 0, NEG_INF, l[:BLK]), l[BLK:]], axis=0)
    sink = sink_ref[kk, :, cols]
    m = jnp.maximum(jnp.max(l, axis=0, keepdims=True), sink)
    e = jnp.exp(l - m)
    es = jnp.exp(sink - m)
    inv = 1.0 / (jnp.sum(e, axis=0, keepdims=True) + es)
    return Q, e * inv, es * inv, qhats, qrs


def _attn_specs(o_q, o_k):
    nq = o_q // 512
    nk = o_k // LANES
    prev = lambda n: (jnp.maximum(n - 1, 0), nk)
    prev_v = lambda n: (jnp.maximum(n - 1, 0), nk + 1)
    return [pl.BlockSpec((BLK, 512), lambda n: (n, nq)), pl.BlockSpec((BLK, 512), lambda n: (n, nq + 1)),
            pl.BlockSpec((BLK, LANES), prev), pl.BlockSpec((BLK, LANES), lambda n: (n, nk)),
            pl.BlockSpec((BLK, LANES), prev_v), pl.BlockSpec((BLK, LANES), lambda n: (n, nk + 1)),
            pl.BlockSpec((1, LANES), lambda n: (0, 0)), pl.BlockSpec((1, LANES), lambda n: (0, 0)),
            pl.BlockSpec((N_KV, 1, 8 * BLK), lambda n: (0, 0, 0)),
            pl.BlockSpec((N_KV, 2 * BLK, 8 * BLK), lambda n: (0, 0, 0))]


def _attn_fwd(z, o_q, o_k, qg2, kg2, sink_rows, bias, plan=None):
    S = z.shape[0]

    def body(zq0, zq1, zkp, zkc, zvp, zvc, qg_ref, kg_ref, sink_ref, bias_ref, o_ref):
        n = pl.program_id(0)
        lo_k = _lo_half((2 * BLK, LANES))
        lo_q = _lo_half((BLK, LANES))
        khat, _ = _norm2(jnp.concatenate([zkp[...], zkc[...]], axis=0), lo_k)
        kn = khat * kg_ref[...]
        vb = jnp.concatenate([zvp[...], zvc[...]], axis=0).astype(BF)
        for kk, zq in enumerate((zq0, zq1)):
            K = jnp.where(lo_k if kk == 0 else jnp.logical_not(lo_k), kn, 0.0).astype(BF)
            for jp in range(4):
                _, p, _, _, _ = _attn_probs(n, kk, jp, 1, zq, K, qg_ref[...], bias_ref, sink_ref)
                r = lax.dot_general(p.astype(BF), vb, TN, preferred_element_type=F32)
                ev, od = r[:BLK], r[BLK:]
                pair = jnp.where(lo_q, ev, _swap(od)) if kk == 0 else jnp.where(lo_q, _swap(ev), od)
                c0 = (4 * kk + jp) * LANES
                o_ref[:, c0:c0 + LANES] = pair.astype(BF)

    return _call("attn_fwd", body, (S // BLK,), _attn_specs(o_q, o_k), [pl.BlockSpec((BLK, ATT_W), lambda n: (n, 0))],
                 [_sds((S, ATT_W), BF)], [z, z, z, z, z, z, qg2, kg2, sink_rows, bias], sem=("parallel",), plan=plan)[0]


def _attn_bwd(z, o_q, o_k, qg2, kg2, sink_rows, bias, dout, plan=None):
    S = z.shape[0]

    def body(zq0, zq1, zkp, zkc, zvp, zvc, qg_ref, kg_ref, sink_ref, bias_ref, do_ref,
             dq_ref, dkp_ref, dkc_ref, dvp_ref, dvc_ref, dl_ref, dsink_ref, dgain_ref):
        n = pl.program_id(0)
        lo_k = _lo_half((2 * BLK, LANES))
        lo_q = _lo_half((BLK, LANES))
        qg = qg_ref[...]
        kg = kg_ref[...]
        khat, kr = _norm2(jnp.concatenate([zkp[...], zkc[...]], axis=0), lo_k)
        kn = khat * kg
        vf = jnp.concatenate([zvp[...], zvc[...]], axis=0)

        @pl.when(n == 0)
        def _():
            dl_ref[...] = jnp.zeros_like(dl_ref)
            dsink_ref[...] = jnp.zeros_like(dsink_ref)
            dgain_ref[...] = jnp.zeros_like(dgain_ref)

        dkn = jnp.zeros((2 * BLK, LANES), F32)
        dvb = jnp.zeros((2 * BLK, LANES), F32)
        dqg = jnp.zeros((1, LANES), F32)
        for kk, zq in enumerate((zq0, zq1)):
            half_k = lo_k if kk == 0 else jnp.logical_not(lo_k)
            K = jnp.where(half_k, kn, 0.0).astype(BF)
            V = jnp.where(half_k, vf, 0.0).astype(BF)
            Q, p, ps, qhats, qrs = _attn_probs(n, kk, 0, 4, zq, K, qg, bias_ref, sink_ref)
            dO = jnp.concatenate([_pair_rows(do_ref[:, (4 * kk + jp) * LANES:(4 * kk + jp + 1) * LANES], kk) for jp in range(4)],
                                 axis=0).astype(BF)
            dP = lax.dot_general(V, dO, NT, preferred_element_type=F32)
            delta = jnp.sum(p * dP, axis=0, keepdims=True)
            dS = p * (dP - delta)
            dsink_ref[kk] += -ps * delta
            dl_ref[kk] += dS
            dSb = dS.astype(BF)
            dvb = dvb + jnp.where(half_k, jnp.dot(p.astype(BF), dO, preferred_element_type=F32), 0.0)
            dkn = dkn + jnp.where(half_k, jnp.dot(dSb, Q, preferred_element_type=F32), 0.0)
            dQ = lax.dot_general(dSb, K, TN, preferred_element_type=F32) * (HEAD_DIM ** -0.5)
            for jp in range(4):
                ev = dQ[(2 * jp) * BLK:(2 * jp + 1) * BLK]
                od = dQ[(2 * jp + 1) * BLK:(2 * jp + 2) * BLK]
                dy = (ev + _swap(od)) if kk == 0 else (_swap(ev) + od)
                dx, gq = _norm2_bwd(dy, qhats[jp], qrs[jp], qg, lo_q)
                dqg = dqg + jnp.sum(gq, axis=0, keepdims=True)
                c0 = (4 * kk + jp) * LANES
                dq_ref[:, c0:c0 + LANES] = dx.astype(BF)
        dk, gk = _norm2_bwd(dkn, khat, kr, kg, lo_k)
        dkp_ref[...] = dk[:BLK]
        dkc_ref[...] = dk[BLK:]
        dvp_ref[...] = dvb[:BLK]
        dvc_ref[...] = dvb[BLK:]
        dgain_ref[...] += jnp.concatenate([dqg, jnp.sum(gk, axis=0, keepdims=True), jnp.zeros((6, LANES), F32)], axis=0)

    blk = pl.BlockSpec((BLK, LANES), lambda n: (n, 0))
    wide = pl.BlockSpec((BLK, ATT_W), lambda n: (n, 0))
    return _call(
        "attn_bwd", body, (S // BLK,), _attn_specs(o_q, o_k) + [wide],
        [wide, blk, blk, blk, blk, pl.BlockSpec((N_KV, 2 * BLK, 8 * BLK), lambda n: (0, 0, 0)),
         pl.BlockSpec((N_KV, 1, 8 * BLK), lambda n: (0, 0, 0)), pl.BlockSpec((8, LANES), lambda n: (0, 0))],
        [_sds((S, ATT_W), BF), _sds((S, LANES), F32), _sds((S, LANES), F32), _sds((S, LANES), F32), _sds((S, LANES), F32),
         _sds((N_KV, 2 * BLK, 8 * BLK), F32), _sds((N_KV, 1, 8 * BLK), F32), _sds((8, LANES), F32)],
        [z, z, z, z, z, z, qg2, kg2, sink_rows, bias, dout], sem=("arbitrary",), plan=plan)


def _kv_combine(dkp, dkc, dvp, dvc):
    S = dkc.shape[0]
    last = S // BLK - 1

    def body(kp_ref, kc_ref, vp_ref, vc_ref, dk_ref, dv_ref):
        more = pl.program_id(0) < last
        dk_ref[...] = (kc_ref[...] + jnp.where(more, kp_ref[...], 0.0)).astype(BF)
        dv_ref[...] = (vc_ref[...] + jnp.where(more, vp_ref[...], 0.0)).astype(BF)

    cur = pl.BlockSpec((BLK, LANES), lambda n: (n, 0))
    nxt = pl.BlockSpec((BLK, LANES), lambda n: (jnp.minimum(n + 1, last), 0))
    return _call("kv_combine", body, (S // BLK,), [nxt, cur, nxt, cur], [cur, cur],
                 [_sds((S, LANES), BF), _sds((S, LANES), BF)], [dkp, dkc, dvp, dvc], sem=("parallel",))


def _merge_fwd(mixed, attn, wpu4, wau4, z, o_ga, plan=None):
    S, PW = mixed.shape
    _, _, CS = wpu4.shape
    D = 4 * CS
    tm = _pick(S, (1024,))
    tn = 256
    nsj = CS // tn
    na = o_ga // tn
    nb = (o_ga + D) // tn

    def body(m_ref, a_ref, wp_ref, wa_ref, ga_ref, gb_ref, mg_ref, yy_ref):
        yp = jnp.dot(m_ref[...], wp_ref[...], preferred_element_type=F32)
        ya = jnp.dot(a_ref[...], wa_ref[...], preferred_element_type=F32)
        mg_ref[...] = (jax.nn.sigmoid(ga_ref[...]) * yp + jax.nn.sigmoid(gb_ref[...]) * ya).astype(BF)
        yy_ref[0] = yp.astype(BF)
        yy_ref[1] = ya.astype(BF)

    return _call("merge_fwd", body, (S // tm, D // tn),
                 [pl.BlockSpec((tm, PW), lambda i, j: (i, 0)), pl.BlockSpec((tm, ATT_W), lambda i, j: (i, 0)),
                  pl.BlockSpec((None, PW, tn), lambda i, j: (j // nsj, 0, j % nsj)),
                  pl.BlockSpec((None, ATT_W, tn), lambda i, j: (j // nsj, 0, j % nsj)),
                  pl.BlockSpec((tm, tn), lambda i, j: (i, na + j)), pl.BlockSpec((tm, tn), lambda i, j: (i, nb + j))],
                 [pl.BlockSpec((tm, tn), lambda i, j: (i, j)), pl.BlockSpec((2, tm, tn), lambda i, j: (0, i, j))],
                 [_sds((S, D), BF), _sds((2, S, D), BF)], [mixed, attn, wpu4, wau4, z, z], sem=("parallel", "parallel"), plan=plan)


def _merge_bwd(do, wo, z, o_ga, yy, plan=None):
    S, D = do.shape
    tm = _pick(S, (1024,))
    tn = 256
    na = o_ga // tn
    nb = (o_ga + D) // tn

    def epi(p, e, o, rs):
        sa = jax.nn.sigmoid(e[0][rs, :])
        sb = jax.nn.sigmoid(e[1][rs, :])
        yp = e[2][0, rs, :].astype(F32)
        ya = e[2][1, rs, :].astype(F32)
        o[0][0, rs, :] = (p * yp * sa * (1.0 - sa)).astype(BF)
        o[0][1, rs, :] = (p * ya * sb * (1.0 - sb)).astype(BF)
        o[1][0, rs, :] = (p * sa).astype(BF)
        o[1][1, rs, :] = (p * sb).astype(BF)

    pair = pl.BlockSpec((2, tm, tn), lambda i, j, k: (0, i, j))
    return _mm("merge_bwd", (S // tm, D // tn, 1), NT,
               do, pl.BlockSpec((tm, D), lambda i, j, k: (i, 0)),
               wo, pl.BlockSpec((tn, D), lambda i, j, k: (j, 0)),
               [z, z, yy], [pl.BlockSpec((tm, tn), lambda i, j, k: (i, na + j)), pl.BlockSpec((tm, tn), lambda i, j, k: (i, nb + j)), pair],
               [_sds((2, S, D), BF), _sds((2, S, D), BF)], [pair, pair], None, epi, plan)


def _mm_up_t(name, dyy, which, w4):
    _, S, D = dyy.shape
    _, K, CS = w4.shape
    tm = _pick(S, (1024,))
    return _mm(name, (S // tm, 1, N_CHIPS), NT,
               dyy, pl.BlockSpec((None, tm, CS), lambda i, j, k: (which, i, k)),
               w4, pl.BlockSpec((None, K, CS), lambda i, j, k: (k, 0, 0)),
               [], [], [_sds((S, K), F32)], [pl.BlockSpec((tm, K), lambda i, j, k: (i, 0))], (tm, K), _store)[0]


def _mm_up_dw(name, a, dyy, which):
    _, S, D = dyy.shape
    K = a.shape[1]
    CS = D // N_CHIPS
    ts = _pick(S, (1024,))
    return _mm(name, (1, N_CHIPS, S // ts), TN,
               a, pl.BlockSpec((ts, K), lambda i, j, k: (k, 0)),
               dyy, pl.BlockSpec((None, ts, CS), lambda i, j, k: (which, k, j)),
               [], [], [_sds((N_CHIPS, K, CS), BF)], [pl.BlockSpec((None, K, CS), lambda i, j, k: (j, 0, 0))], (K, CS), _store)[0]


def _adamw(w, g, m, v):
    m = ADAM_B1 * m + (1.0 - ADAM_B1) * g
    v = ADAM_B2 * v + (1.0 - ADAM_B2) * (g * g)
    m_hat = m / (1.0 - ADAM_B1 ** ADAM_STEP)
    v_hat = v / (1.0 - ADAM_B2 ** ADAM_STEP)
    delta = -ADAM_LR * (m_hat / (jnp.sqrt(v_hat) + ADAM_EPS) + ADAM_WD * w)
    return delta, m, v


def _mod_fwd(c_all, w_ada, b_sh):
    D, cols = w_ada.shape
    tn = cols // 9

    def body(c_ref, w_ref, b_ref, o_ref):
        cv = c_ref[...]
        sc = (cv * jax.nn.sigmoid(cv)).astype(BF)
        o_ref[...] = jnp.dot(sc, w_ref[...].astype(BF), preferred_element_type=F32) + b_ref[...]

    return _call("mod_fwd", body, (9,),
                 [pl.BlockSpec((N_DEV, D), lambda j: (0, 0)), pl.BlockSpec((D, tn), lambda j: (0, j)), pl.BlockSpec((1, tn), lambda j: (0, j))],
                 [pl.BlockSpec((N_DEV, tn), lambda j: (0, j))], [_sds((N_DEV, cols), F32)], [c_all, w_ada, b_sh], sem=("parallel",))[0]


def _wada_bwd(c_all, dmod_sh, w, m, v, plan=None):
    D, cols = w.shape
    tn = cols // 18

    def body(c_ref, d_ref, w_ref, m_ref, v_ref, g_ref, dl_ref, nm_ref, nv_ref):
        cv = c_ref[...]
        sc = (cv * jax.nn.sigmoid(cv)).astype(BF)
        g = lax.dot_general(sc, d_ref[...].astype(BF), TN, preferred_element_type=F32)
        g_ref[...] = g
        dl_ref[...], nm_ref[...], nv_ref[...] = _adamw(w_ref[...], g, m_ref[...], v_ref[...])

    tile = pl.BlockSpec((D, tn), lambda j: (0, j))
    out = _sds((D, cols), F32)
    return _call("wada_bwd", body, (18,),
                 [pl.BlockSpec((N_DEV, D), lambda j: (0, 0)), pl.BlockSpec((N_DEV, tn), lambda j: (0, j)), tile, tile, tile],
                 [tile] * 4, [out] * 4, [c_all, dmod_sh, w, m, v], sem=("parallel",), plan=plan)


def _adam_2d(name, w, g, m, v):
    R, C = w.shape
    tr = _row_tile(R, 256)

    def body(w_ref, g_ref, m_ref, v_ref, dl_ref, nm_ref, nv_ref):
        dl_ref[...], nm_ref[...], nv_ref[...] = _adamw(w_ref[...], g_ref[...], m_ref[...], v_ref[...])

    tile = pl.BlockSpec((tr, C), lambda i: (i, 0))
    out = _sds((R, C), F32)
    return _call(name, body, (R // tr,), [tile] * 4, [tile] * 3, [out] * 3, [w, g, m, v], sem=("parallel",))


def _small_finish(parts, w, m, v):
    _, R, C = parts.shape

    def body(p_ref, w_ref, m_ref, v_ref, g_ref, dl_ref, nm_ref, nv_ref):
        g = p_ref[0]
        for d in range(1, N_DEV):
            g = g + p_ref[d]
        g_ref[...] = g
        dl_ref[...], nm_ref[...], nv_ref[...] = _adamw(w_ref[...], g, m_ref[...], v_ref[...])

    out = _sds((R, C), F32)
    return pl.pallas_call(body, out_shape=[out] * 4, name="small_finish",
                          compiler_params=pltpu.CompilerParams(vmem_limit_bytes=VMEM_LIMIT))(parts, w, m, v)


def _my_chip():
    return 2 * lax.axis_index("x") + lax.axis_index("y")


def _cast_into_slot(name, w):
    R, C = w.shape
    tr = _row_tile(R, 256)

    def body(w_ref, o_ref):
        o_ref[...] = w_ref[...].astype(BF)

    return _call(name, body, (R // tr,), [pl.BlockSpec((tr, C), lambda i: (i, 0))],
                 [pl.BlockSpec((None, tr, C), lambda i: (_my_chip(), i, 0))], [_sds((N_CHIPS, R, C), BF)], [w], sem=("parallel",))[0]


def _add_pair(name, p, q):
    _, H, C = q.shape
    tr = _row_tile(H, 512)
    nt = H // tr

    def body(p_ref, q_ref, o_ref):
        o_ref[...] = (p_ref[...].astype(F32) + q_ref[...].astype(F32)).astype(BF)

    tile = pl.BlockSpec((None, tr, C), lambda k, i: (k, i, 0))
    return _call(name, body, (N_CHIPS, nt), [pl.BlockSpec((None, tr, C), lambda k, i: (k, lax.axis_index("c") * nt + i, 0)), tile],
                 [tile], [_sds(q.shape, BF)], [p, q], sem=("parallel", "parallel"))[0]


def _sum_chips(name, u, t):
    _, H, C = u.shape
    tr = _row_tile(H, 256)

    def body(u_ref, t_ref, o_ref):
        r = _my_chip()
        own = t_ref[...].astype(F32)
        pick = lambda k: jnp.where(r == k, own, u_ref[k].astype(F32))
        o_ref[...] = ((pick(0) + pick(1)) + pick(2)) + pick(3)

    return _call(name, body, (H // tr,),
                 [pl.BlockSpec((N_CHIPS, tr, C), lambda i: (0, i, 0)), pl.BlockSpec((None, tr, C), lambda i: (_my_chip(), i, 0))],
                 [pl.BlockSpec((None, tr, C), lambda i: (lax.axis_index("c"), i, 0))], [_sds((2, H, C), F32)], [u, t],
                 sem=("parallel",))[0]


BIG = ("gu1", "down1", "w_in", "pool_mix", "pool_up", "attn_up", "o", "gu2", "down2")
MIX = ("o", "pool_up", "attn_up", "pool_mix")
EARLY = ("down1", "w_in", "pool_mix", "pool_up", "attn_up", "o")

SCHEDULE = {
    "rms_mod_fwd1": ([("ici", ("gu1",))], []),
    "+gather_gu1_d2d": ([("d2d", ("gu1",))], []),
    "ffn1_up": ([("ici", EARLY)], []),
    "+gather_early_d2d": ([("d2d", EARLY)], []),
    "ffn1_down": ([("ici", ("gu2",))], []),
    "mix_in": ([("ici", ("down2",)), ("d2d", ("gu2",))], []),
    "mix_out": ([("d2d", ("down2",))], []),
    "ffn2_dwd": ([("split", ("gu2",))], [("add", ("gu2",))]),
    "ffn2_dh": ([("owners", ("gu2",)), ("split", ("down2",))], [("add", ("down2",)), ("sum", ("gu2",))]),
    "merge_bwd": ([("owners", ("down2",)), ("join", ("gu2",))], [("sum", ("down2",)), ("adam", ("gu2",))]),
    "attn_bwd": ([("split", MIX), ("join", ("down2",))], [("add", MIX), ("adam", ("down2",))]),
    "mix_dwin": ([("owners", MIX)], [("sum", MIX)]),
    "mix_dh": ([("split", ("w_in",)), ("join", MIX)], [("add", ("w_in",)), ("adam", MIX)]),
    "ffn1_dact": ([("owners", ("w_in",))], [("sum", ("w_in",))]),
    "ffn1_dwd": ([("split", ("gu1",)), ("join", ("w_in",))], [("add", ("gu1",)), ("adam", ("w_in",))]),
    "ffn1_dh": ([("owners", ("gu1",)), ("split", ("down1",))], [("add", ("down1",)), ("sum", ("gu1",))]),
    "rms_mod_bwd1": ([("owners", ("down1",)), ("join", ("gu1",))], [("sum", ("down1",)), ("adam", ("gu1",))]),
    "+join_down1": ([("join", ("down1",))], [("adam", ("down1",))]),
}


class _Plan:
    def __init__(self, w2, m2, v2, full, D, gw):
        self.w2, self.m2, self.v2, self.full, self.D, self.gw = w2, m2, v2, dict(full), D, gw
        self.part, self.got, self.sums, self.landed, self.g = {}, {}, {}, {}, {}
        self.result = {}
        self.pending = {}

    def _make(self, op, names):
        if op == "ici":
            return _gather_ici_stage([self.full[k] for k in names])
        if op == "d2d":
            return _gather_d2d_stage([self.full[k] for k in names])
        if op == "split":
            return _split_stage([self.part[k] for k in names])
        if op == "owners":
            return _owners_stage([self.sums[k] for k in names])
        return _join_stage([self.g[k] for k in names])

    def stages(self, name):
        ops = SCHEDULE.get(name, ([], []))[0]
        return [self._make(*op) for op in ops]

    def done(self, name, outs):
        ops, local = SCHEDULE[name]
        for op, res in zip(ops, outs):
            store = {"ici": self.full, "d2d": self.full, "split": self.got, "owners": self.landed, "join": self.g}[op[0]]
            store.update(zip(op[1], res))
        for op, names in local:
            for k in names:
                if op == "add":
                    self.sums[k] = _add_pair("add_pair_" + k, self.part[k], self.got[k])
                elif op == "sum":
                    self.g[k] = _sum_chips("sum_chips_" + k, self.landed[k], self.sums[k])
                else:
                    g2 = self.g[k].reshape(self.w2[k].shape)
                    self.result[k] = (g2, *_adam_2d("adam_" + k, self.w2[k], g2, self.m2[k], self.v2[k]))

    def alone(self, name):
        self.done(name, _run_stages(name[1:], self.stages(name)))

    def weight(self, k):
        D, gw, f = self.D, self.gw, self.full[k]
        if k in ("down1", "down2", "o"):
            return f.reshape(-1, D)
        if k == "pool_mix":
            return f.reshape(N_CHIPS, 4, gw // N_CHIPS, gw).transpose(1, 0, 2, 3).reshape(4, gw, gw)
        if k == "w_in":
            return f.reshape(-1, D)
        return f

    def partial(self, k, p):
        D, gw = self.D, self.gw
        if k in ("down1", "down2", "o", "w_in"):
            p = p.reshape(N_CHIPS, p.shape[0] // N_CHIPS, p.shape[1])
        elif k == "pool_mix":
            p = p.astype(BF).reshape(4, N_CHIPS, gw // N_CHIPS, gw).transpose(1, 0, 2, 3).reshape(N_CHIPS, gw, gw)
        self.part[k] = p


class _NoComm:
    def __init__(self, weights):
        self.w, self.part = weights, {}

    def stages(self, name):
        return []

    def alone(self, name):
        pass

    def weight(self, k):
        return self.w[k]

    def partial(self, k, p):
        self.part[k] = p


def _row(a, i):
    return a[i:i + 1]


def _local_step(x, target, mod, g_ffn1, g_mix, g_ffn2, pool_scale, q_gain, k_gain, sinks, rel_bias, plan):
    S, D = x.shape
    half = 0.5 * mod
    tile2 = lambda g: jnp.concatenate([g, g], axis=1)
    qg2, kg2 = tile2(q_gain), tile2(k_gain)
    sink_rows = jnp.broadcast_to(sinks.reshape(N_KV, 1, 8, 1), (N_KV, 1, 8, BLK)).reshape(N_KV, 1, 8 * BLK)
    bias = _bias_table(rel_bias).reshape(N_KV, 8, BLK, 2 * BLK).transpose(0, 3, 1, 2).reshape(N_KV, 2 * BLK, 8 * BLK)

    h1 = _rms_mod_fwd("rms_mod_fwd1", x, g_ffn1, _row(mod, 0), _row(mod, 1), plan)
    plan.alone("+gather_gu1_d2d")
    gu1, act1 = _ffn_up("ffn1_up", h1, plan.weight("gu1"), plan)
    plan.alone("+gather_early_d2d")
    x1, f1 = _mm_residual("ffn1_down", act1, plan.weight("down1"), x, _row(half, 2), plan)
    h2 = _rms_mod_fwd("rms_mod_fwd2", x1, g_mix, _row(mod, 3), _row(mod, 4))
    w_in_t = plan.weight("w_in")
    IN_W = w_in_t.shape[0]
    PW = plan.weight("pool_up").shape[1]
    o_q, o_k = PW, PW + ATT_W
    o_ga = o_k + 2 * KV_W
    tnz = _pick(IN_W, (1280, 256))
    tmz = _pick(S, (1024,))
    z = _mm("mix_in", (S // tmz, IN_W // tnz, 1), NT, h2, pl.BlockSpec((tmz, D), lambda i, j, k: (i, 0)),
            w_in_t, pl.BlockSpec((tnz, D), lambda i, j, k: (j, 0)), [], [], [_sds((S, IN_W), F32)],
            [pl.BlockSpec((tmz, tnz), lambda i, j, k: (i, j))], None, _store, plan)[0]
    pooled = _pool_fwd(z, PW)
    mixed = _pool_mix(pooled, plan.weight("pool_mix"), pool_scale)
    attn = _attn_fwd(z, o_q, o_k, qg2, kg2, sink_rows, bias)
    merged, yy = _merge_fwd(mixed, attn, plan.weight("pool_up"), plan.weight("attn_up"), z, o_ga)
    x2, fo = _mm_residual("mix_out", merged, plan.weight("o"), x1, _row(mod, 5), plan)
    h3 = _rms_mod_fwd("rms_mod_fwd3", x2, g_ffn2, _row(mod, 6), _row(mod, 7))
    gu2, act2 = _ffn_up("ffn2_up", h3, plan.weight("gu2"))
    x3, f2 = _mm_residual("ffn2_down", act2, plan.weight("down2"), x2, _row(half, 8))
    dx3, df2, loss_acc = _loss_bwd(x3, target, f2, _row(half, 8))

    dgu2 = _ffn_dact("ffn2_dact", df2, plan.weight("down2"), gu2)
    plan.partial("gu2", _ffn_dwgu("ffn2_dwgu", h3, dgu2))
    plan.partial("down2", _mm_tn("ffn2_dwd", act2, df2, (1408, 512), (512,), plan))
    dh3 = _ffn_dh("ffn2_dh", dgu2, plan.weight("gu2"), plan)
    dx2, do, acc3 = _rms_mod_bwd("rms_mod_bwd3", dh3, x2, dx3, g_ffn2, _row(mod, 7), fo, _row(mod, 5))

    dgab, dyy = _merge_bwd(do, plan.weight("o"), z, o_ga, yy, plan)
    plan.partial("o", _mm_tn("mix_dwo", merged, do, (1024,), (512,), plan))
    dmixed = _mm_up_t("pool_up_t", dyy, 0, plan.weight("pool_up"))
    dattn = _mm_up_t("attn_up_t", dyy, 1, plan.weight("attn_up"))
    plan.partial("pool_up", _mm_up_dw("pool_up_dw", mixed, dyy, 0))
    plan.partial("attn_up", _mm_up_dw("attn_up_dw", attn, dyy, 1))
    dpooled, dpm, dps = _pool_mix_bwd(pooled, plan.weight("pool_mix"), pool_scale, dmixed)
    plan.partial("pool_mix", dpm)
    du_pool = _pool_bwd(dpooled)
    dq, dkp, dkc, dvp, dvc, dl, dsink, dgain = _attn_bwd(z, o_q, o_k, qg2, kg2, sink_rows, bias, dattn, plan)
    dk, dv = _kv_combine(dkp, dkc, dvp, dvc)
    drb = _rel_bias_grad(dl.reshape(N_KV, 2 * BLK, 8, BLK).transpose(0, 2, 3, 1).reshape(N_HEADS, BLK * 2 * BLK))
    dz = jnp.concatenate([du_pool, dq, dk, dv, dgab[0], dgab[1]], axis=1)
    plan.partial("w_in", _mm_tn("mix_dwin", dz, h2, (1280, 256), (512,), plan))
    tnd = _pick(D, (512,))
    dh2 = _mm("mix_dh", (S // tmz, D // tnd, 1), NN, dz, pl.BlockSpec((tmz, IN_W), lambda i, j, k: (i, 0)),
              w_in_t, pl.BlockSpec((IN_W, tnd), lambda i, j, k: (0, j)), [], [], [_sds((S, D), F32)],
              [pl.BlockSpec((tmz, tnd), lambda i, j, k: (i, j))], None, _store, plan)[0]
    dx1, df1, acc2 = _rms_mod_bwd("rms_mod_bwd2", dh2, x1, dx2, g_mix, _row(mod, 4), f1, _row(half, 2))

    dgu1 =_ffn_dact("ffn1_dact", df1, plan.weight("down1"), gu1, plan)
    plan.partial("gu1", _ffn_dwgu("ffn1_dwgu", h1, dgu1, plan))
    plan.partial("down1", _mm_tn("ffn1_dwd", act1, df1, (1408, 512), (512,), plan))
    dh1 = _ffn_dh("ffn1_dh", dgu1, plan.weight("gu1"), plan)
    grad_x, acc1 = _rms_mod_bwd("rms_mod_bwd1", dh1, x, dx1, g_ffn1, _row(mod, 1), plan=plan)

    dmod = jnp.concatenate([_row(acc1, 0), _row(acc1, 1), 0.5 * _row(acc2, 3),
                            _row(acc2, 0), _row(acc2, 1), _row(acc3, 3),
                            _row(acc3, 0), _row(acc3, 1), 0.5 * _row(loss_acc, 1)], axis=0)
    fold = lambda r: r[:, :HEAD_DIM] + r[:, HEAD_DIM:]
    small = dict(
        dmod=dmod, g_ffn1=_row(acc1, 2), g_mix=_row(acc2, 2), g_ffn2=_row(acc3, 2), pool_scale=_row(dps, 0),
        q_gain=fold(_row(dgain, 0)), k_gain=fold(_row(dgain, 1)),
        sinks=jnp.sum(dsink.reshape(N_HEADS, BLK), axis=1).reshape(1, N_HEADS), rel_bias=drb,
        loss=(0.5 / D) * jnp.sum(_row(loss_acc, 0)).reshape(1, 1))
    return grad_x, small


SMALL_ORDER = ("dmod", "g_ffn1", "g_mix", "g_ffn2", "pool_scale", "q_gain", "k_gain", "sinks", "rel_bias", "loss")


def _pack_small(vals):
    flat = jnp.concatenate([vals[k].reshape(-1) for k in SMALL_ORDER])
    n = flat.shape[0]
    rows = -(-n // (8 * LANES)) * 8
    return jnp.pad(flat, (0, rows * LANES - n)).reshape(rows, LANES)


def _unpack_small(packed, like):
    flat = packed.reshape(-1)
    out, off = {}, 0
    for k in SMALL_ORDER:
        n = int(np.prod(like[k].shape))
        out[k] = flat[off:off + n].reshape(like[k].shape)
        off += n
    return out


def kernel(x, c, w_ada, b_ada, g_ffn1, w_ffn1_gu, w_ffn1_down, g_mix, w_in, pool_mix, pool_scale, w_pool_up, q_gain, k_gain, sinks, rel_bias, w_attn_up, w_o, g_ffn2, w_ffn2_gu, w_ffn2_down, loss_target, m_w_ada, m_b_ada, m_g_ffn1, m_w_ffn1_gu, m_w_ffn1_down, m_g_mix, m_w_in, m_pool_mix, m_pool_scale, m_w_pool_up, m_q_gain, m_k_gain, m_sinks, m_rel_bias, m_w_attn_up, m_w_o, m_g_ffn2, m_w_ffn2_gu, m_w_ffn2_down, v_w_ada, v_b_ada, v_g_ffn1, v_w_ffn1_gu, v_w_ffn1_down, v_g_mix, v_w_in, v_pool_mix, v_pool_scale, v_w_pool_up, v_q_gain, v_k_gain, v_sinks, v_rel_bias, v_w_attn_up, v_w_o, v_g_ffn2, v_w_ffn2_gu, v_w_ffn2_down):
    S, D = x.shape[1], x.shape[2]
    gw = pool_mix.shape[3]
    r = 2 * lax.axis_index("x") + lax.axis_index("y")

    two_d = lambda a: a.reshape(-1, a.shape[-1])
    w_sh = dict(gu1=w_ffn1_gu, down1=w_ffn1_down, w_in=w_in, pool_mix=pool_mix, pool_up=w_pool_up, attn_up=w_attn_up, o=w_o,
                gu2=w_ffn2_gu, down2=w_ffn2_down)
    m_sh = dict(gu1=m_w_ffn1_gu, down1=m_w_ffn1_down, w_in=m_w_in, pool_mix=m_pool_mix, pool_up=m_w_pool_up, attn_up=m_w_attn_up,
                o=m_w_o, gu2=m_w_ffn2_gu, down2=m_w_ffn2_down)
    v_sh = dict(gu1=v_w_ffn1_gu, down1=v_w_ffn1_down, w_in=v_w_in, pool_mix=v_pool_mix, pool_up=v_w_pool_up, attn_up=v_w_attn_up,
                o=v_w_o, gu2=v_w_ffn2_gu, down2=v_w_ffn2_down)
    view = lambda k, a: two_d(a).T if k == "w_in" else two_d(a)
    unview = lambda k, a: (a.T if k == "w_in" else a).reshape(w_sh[k].shape)
    w2 = {k: view(k, w_sh[k]) for k in BIG}
    full = {k: _cast_into_slot("cast_" + k, w2[k]) for k in BIG}
    plan = _Plan(w2, {k: view(k, m_sh[k]) for k in BIG}, {k: view(k, v_sh[k]) for k in BIG}, full, D, gw)

    c_all = _gather_all("gather_c", jnp.broadcast_to(c, (8, D)))[:, 0, :]
    cols = w_ada.shape[2]
    b_sh = lax.dynamic_slice(b_ada, (0, r * cols), (1, cols))
    mod_cols = _mod_fwd(c_all, w_ada[0], b_sh)
    mod_all = _chip_exchange("mod_exchange", mod_cols)
    me = 4 * lax.axis_index("x") + 2 * lax.axis_index("y") + lax.axis_index("c")
    mod = lax.dynamic_slice(mod_all, (0, me, 0), (N_CHIPS, 1, cols)).reshape(9, D)

    grad_x, small = _local_step(x[0], loss_target[0], mod, g_ffn1, g_mix, g_ffn2, pool_scale, q_gain, k_gain,
                                sinks, rel_bias, plan)

    small_w = dict(dmod=b_ada, g_ffn1=g_ffn1, g_mix=g_mix, g_ffn2=g_ffn2, pool_scale=pool_scale, q_gain=q_gain, k_gain=k_gain,
                   sinks=sinks, rel_bias=rel_bias, loss=jnp.zeros((1, 1), F32))
    small_m = dict(dmod=m_b_ada, g_ffn1=m_g_ffn1, g_mix=m_g_mix, g_ffn2=m_g_ffn2, pool_scale=m_pool_scale, q_gain=m_q_gain,
                   k_gain=m_k_gain, sinks=m_sinks, rel_bias=m_rel_bias, loss=jnp.zeros((1, 1), F32))
    small_v = dict(dmod=v_b_ada, g_ffn1=v_g_ffn1, g_mix=v_g_mix, g_ffn2=v_g_ffn2, pool_scale=v_pool_scale, q_gain=v_q_gain,
                   k_gain=v_k_gain, sinks=v_sinks, rel_bias=v_rel_bias, loss=jnp.ones((1, 1), F32))
    small_all = _gather_all("gather_small", _pack_small(small))
    sg, sd, sm, sv = [_unpack_small(a, small_w) for a in
                      _small_finish(small_all, _pack_small(small_w), _pack_small(small_m), _pack_small(small_v))]
    loss = sg["loss"].reshape(())

    dmod_all = small_all.reshape(N_DEV, -1)[:, :9 * D]
    dmod_sh = lax.dynamic_slice(dmod_all, (0, r * cols), (N_DEV, cols))
    g_ada, d_ada, nm_ada, nv_ada = _wada_bwd(c_all, dmod_sh, w_ada[0], m_w_ada[0], v_w_ada[0])
    plan.alone("+join_down1")

    big = [{k: unview(k, plan.result[k][i]) for k in BIG} for i in range(4)]

    def ordered(b, ada, sm_):
        return (ada[None], sm_["dmod"], sm_["g_ffn1"], b["gu1"], b["down1"], sm_["g_mix"], b["w_in"], b["pool_mix"],
                sm_["pool_scale"], b["pool_up"], sm_["q_gain"], sm_["k_gain"], sm_["sinks"], sm_["rel_bias"], b["attn_up"],
                b["o"], sm_["g_ffn2"], b["gu2"], b["down2"])

    return (loss, grad_x[None], *ordered(big[0], g_ada, sg), *ordered(big[1], d_ada, sd), *ordered(big[2], nm_ada, sm),
            *ordered(big[3], nv_ada, sv))
```

```python
import numpy as np
import jax
import jax.numpy as jnp
from jax import lax
from jax.experimental import pallas as pl
from jax.experimental.pallas import tpu as pltpu

BF = jnp.bfloat16
F32 = jnp.float32
MESH = pl.DeviceIdType.MESH

EPS = 1e-6
NEG_INF = -1e30
HEAD_DIM = 64
N_HEADS = 16
N_KV = 2
ATT_W = N_HEADS * HEAD_DIM
KV_W = N_KV * HEAD_DIM
BLK = 128
NUM_BUCKETS = 32
POOL_MAX_W = 16
N_CHIPS = 4
N_DEV = 8
LANES = 128
ADAM_LR, ADAM_B1, ADAM_B2, ADAM_EPS, ADAM_WD, ADAM_STEP = 0.001, 0.9, 0.999, 1e-08, 0.01, 10
VMEM_LIMIT = 52 * 1024 * 1024
ANY = pl.BlockSpec(memory_space=pl.ANY)


def _pick(dim, prefs):
    for p in prefs:
        if p <= dim and dim % p == 0:
            return p
    return dim


def _row_tile(rows, cap):
    return max(t for t in range(16, min(rows, cap) + 1, 16) if rows % t == 0)


def _sds(shape, dtype):
    return jax.ShapeDtypeStruct(tuple(shape), dtype)


def _place():
    return lax.axis_index("x"), lax.axis_index("y"), lax.axis_index("c")


def _other_chips(x, y):
    return [(1 - x, y), (x, 1 - y), (1 - x, 1 - y)]


def _chip_of(chip):
    return 2 * chip[0] + chip[1]


def _half_rows(ref, lead, cc, h):
    return ref.at[lead, pl.ds(pl.multiple_of(cc * h, 16), h), :]


class _Stage:
    def __init__(self, bufs, outs, alias, n_sem, start, wait):
        self.bufs, self.outs, self.alias, self.n_sem, self.start, self.wait = bufs, outs, alias, n_sem, start, wait


def _stage_plumbing(stages, n_in0, n_out0):
    bufs, outs, aliases, spans, scratch = [], [], {}, [], []
    for st in stages:
        i0, o0 = len(bufs), len(outs)
        bufs += list(st.bufs)
        outs += list(st.outs)
        for a, b in st.alias.items():
            aliases[n_in0 + i0 + a] = n_out0 + o0 + b
        spans.append((i0, len(bufs), o0, len(outs)))
        scratch += [pltpu.SemaphoreType.DMA((st.n_sem,)), pltpu.SemaphoreType.DMA((st.n_sem,))]

    def run(which, in_refs, out_refs, sem_refs):
        for s, st in enumerate(stages):
            i0, i1, o0, o1 = spans[s]
            getattr(st, which)(in_refs[i0:i1], out_refs[o0:o1], sem_refs[2 * s], sem_refs[2 * s + 1])

    def split(flat):
        return [list(flat[o0:o1]) for (_, _, o0, o1) in spans]

    return bufs, outs, aliases, scratch, run, split


def _run_stages(name, stages):
    bufs, outs, aliases, scratch, run, split = _stage_plumbing(stages, 0, 0)
    ni, no = len(bufs), len(outs)

    def body(*refs):
        ins, os_, sems = refs[:ni], refs[ni:ni + no], refs[ni + no:]
        run("start", ins, os_, sems)
        run("wait", ins, os_, sems)

    res = pl.pallas_call(body, in_specs=[ANY] * ni, out_specs=[ANY] * no, out_shape=outs, input_output_aliases=aliases,
                         scratch_shapes=scratch, name=name)(*bufs)
    return split(res)


def _call(name, body, grid, in_specs, out_specs, out_shape, args, scratch=(), sem=None, plan=None):
    stages = plan.stages(name) if plan is not None else []
    n_in, n_out, n_scr = len(args), len(out_shape), len(scratch)
    if not stages:
        return pl.pallas_call(body, grid=grid, in_specs=list(in_specs), out_specs=list(out_specs), out_shape=list(out_shape),
                              scratch_shapes=list(scratch), name=name,
                              compiler_params=pltpu.CompilerParams(dimension_semantics=sem, vmem_limit_bytes=VMEM_LIMIT))(*args)
    bufs, s_outs, aliases, s_scratch, run, split = _stage_plumbing(stages, n_in, n_out)
    nb, nso = len(bufs), len(s_outs)

    def hosted(*refs):
        ins = refs[:n_in]
        s_ins = refs[n_in:n_in + nb]
        outs = refs[n_in + nb:n_in + nb + n_out]
        s_os = refs[n_in + nb + n_out:n_in + nb + n_out + nso]
        scr = refs[n_in + nb + n_out + nso:n_in + nb + n_out + nso + n_scr]
        sems = refs[n_in + nb + n_out + nso + n_scr:]
        first = pl.program_id(0) == 0
        last = pl.program_id(0) == grid[0] - 1
        for d in range(1, len(grid)):
            first = first & (pl.program_id(d) == 0)
            last = last & (pl.program_id(d) == grid[d] - 1)

        @pl.when(first)
        def _():
            run("start", s_ins, s_os, sems)

        body(*ins, *outs, *scr)

        @pl.when(last)
        def _():
            run("wait", s_ins, s_os, sems)

    res = pl.pallas_call(
        hosted, grid=grid, in_specs=list(in_specs) + [ANY] * nb, out_specs=list(out_specs) + [ANY] * nso,
        out_shape=list(out_shape) + s_outs, input_output_aliases=aliases, scratch_shapes=list(scratch) + s_scratch, name=name,
        compiler_params=pltpu.CompilerParams(dimension_semantics=("arbitrary",) * len(grid), vmem_limit_bytes=VMEM_LIMIT))(*args, *bufs)
    plan.done(name, split(res[n_out:]))
    return list(res[:n_out])


def _gather_ici_stage(fulls):
    n = len(fulls)

    def copy(i, j, slot, ins, outs, send, recv):
        x, y, c = _place()
        chip = _other_chips(x, y)[j]
        h = fulls[i].shape[1] // 2
        s = 3 * i + j
        return pltpu.make_async_remote_copy(_half_rows(ins[i], 2 * x + y, c, h), _half_rows(outs[i], slot(x, y, chip), c, h),
                                            send.at[s], recv.at[s], device_id=(*chip, c), device_id_type=MESH)

    mine = lambda x, y, chip: 2 * x + y
    theirs = lambda x, y, chip: _chip_of(chip)

    def start(ins, outs, send, recv):
        for i in range(n):
            for j in range(3):
                copy(i, j, mine, ins, outs, send, recv).start()

    def wait(ins, outs, send, recv):
        for i in range(n):
            for j in range(3):
                copy(i, j, theirs, ins, outs, send, recv).wait_recv()
        for i in range(n):
            for j in range(3):
                copy(i, j, mine, ins, outs, send, recv).wait_send()

    return _Stage(fulls, [_sds(f.shape, f.dtype) for f in fulls], {i: i for i in range(n)}, 3 * n, start, wait)


def _gather_d2d_stage(fulls):
    n = len(fulls)

    def copy(i, j, cc, ins, outs, send, recv):
        x, y, c = _place()
        rj = _chip_of(_other_chips(x, y)[j])
        h = fulls[i].shape[1] // 2
        half = cc(c)
        s = 3 * i + j
        return pltpu.make_async_remote_copy(_half_rows(ins[i], rj, half, h), _half_rows(outs[i], rj, half, h),
                                            send.at[s], recv.at[s], device_id=(x, y, 1 - c), device_id_type=MESH)

    mine = lambda c: c
    theirs = lambda c: 1 - c

    def start(ins, outs, send, recv):
        for i in range(n):
            for j in range(3):
                copy(i, j, mine, ins, outs, send, recv).start()

    def wait(ins, outs, send, recv):
        for i in range(n):
            for j in range(3):
                copy(i, j, theirs, ins, outs, send, recv).wait_recv()
        for i in range(n):
            for j in range(3):
                copy(i, j, mine, ins, outs, send, recv).wait_send()

    return _Stage(fulls, [_sds(f.shape, f.dtype) for f in fulls], {i: i for i in range(n)}, 3 * n, start, wait)


def _split_stage(parts):
    n = len(parts)

    def copy(i, ins, outs, send, recv):
        x, y, c = _place()
        h = parts[i].shape[1] // 2
        return pltpu.make_async_remote_copy(_half_rows(ins[i], slice(None), 1 - c, h), outs[i], send.at[i], recv.at[i],
                                            device_id=(x, y, 1 - c), device_id_type=MESH)

    def start(ins, outs, send, recv):
        for i in range(n):
            copy(i, ins, outs, send, recv).start()

    def wait(ins, outs, send, recv):
        for i in range(n):
            copy(i, ins, outs, send, recv).wait_recv()
        for i in range(n):
            copy(i, ins, outs, send, recv).wait_send()

    return _Stage(parts, [_sds((N_CHIPS, p.shape[1] // 2, p.shape[2]), p.dtype) for p in parts], {}, n, start, wait)


def _owners_stage(sums):
    n = len(sums)

    def copy(i, j, mine, ins, outs, send, recv):
        x, y, c = _place()
        chip = _other_chips(x, y)[j]
        slot = (2 * x + y) if mine else _chip_of(chip)
        return pltpu.make_async_remote_copy(ins[i].at[_chip_of(chip)], outs[i].at[slot], send.at[3 * i + j], recv.at[3 * i + j],
                                            device_id=(*chip, c), device_id_type=MESH)

    def start(ins, outs, send, recv):
        for i in range(n):
            for j in range(3):
                copy(i, j, True, ins, outs, send, recv).start()

    def wait(ins, outs, send, recv):
        for i in range(n):
            for j in range(3):
                copy(i, j, False, ins, outs, send, recv).wait_recv()
        for i in range(n):
            for j in range(3):
                copy(i, j, True, ins, outs, send, recv).wait_send()

    return _Stage(sums, [_sds(s.shape, s.dtype) for s in sums], {}, 3 * n, start, wait)


def _join_stage(gs):
    n = len(gs)

    def copy(i, mine, ins, outs, send, recv):
        x, y, c = _place()
        slot = c if mine else 1 - c
        return pltpu.make_async_remote_copy(ins[i].at[slot], outs[i].at[slot], send.at[i], recv.at[i],
                                            device_id=(x, y, 1 - c), device_id_type=MESH)

    def start(ins, outs, send, recv):
        for i in range(n):
            copy(i, True, ins, outs, send, recv).start()

    def wait(ins, outs, send, recv):
        for i in range(n):
            copy(i, False, ins, outs, send, recv).wait_recv()
        for i in range(n):
            copy(i, True, ins, outs, send, recv).wait_send()

    return _Stage(gs, [_sds(g.shape, g.dtype) for g in gs], {i: i for i in range(n)}, n, start, wait)


def _gather_start(name, full):
    h = full.shape[1] // 2

    def body(f_ref, send, recv, f_out, token):
        x, y, c = _place()
        for chip in _other_chips(x, y):
            pltpu.make_async_remote_copy(_half_rows(f_ref, 2 * x + y, c, h), _half_rows(f_out, 2 * x + y, c, h), send, recv,
                                         device_id=(*chip, c), device_id_type=MESH).start()
        token[...] = jnp.zeros_like(token)

    hbm = pl.BlockSpec(memory_space=pltpu.HBM)
    sem = pl.BlockSpec(memory_space=pltpu.SEMAPHORE)
    return pl.pallas_call(
        body, name=name,
        out_shape=(pltpu.SemaphoreType.DMA(()), pltpu.SemaphoreType.DMA(()), pltpu.HBM(full.shape, full.dtype), _sds((8, LANES), F32)),
        in_specs=(hbm,), out_specs=(sem, sem, hbm, pl.BlockSpec(memory_space=pltpu.VMEM)), input_output_aliases={0: 2},
        compiler_params=pltpu.CompilerParams(has_side_effects=pltpu.SideEffectType.DATAFLOW_SIDE_EFFECTING),
    )(pltpu.with_memory_space_constraint(full, pltpu.HBM))


def _gather_finish(name, send, recv, full, after):
    h = full.shape[1] // 2

    def body(f_ref, send, recv, after_ref, f_out):
        x, y, c = _place()
        three = f_ref.at[pl.ds(0, 3), pl.ds(0, h), :]
        all_three = pltpu.make_async_remote_copy(three, three, send, recv, device_id=(x, y, 1 - c), device_id_type=MESH)
        all_three.wait_send()
        all_three.wait_recv()

    hbm = pl.BlockSpec(memory_space=pltpu.HBM)
    sem = pl.BlockSpec(memory_space=pltpu.SEMAPHORE)
    return pl.pallas_call(
        body, name=name, out_shape=pltpu.HBM(full.shape, full.dtype),
        in_specs=(hbm, sem, sem, pl.BlockSpec(memory_space=pl.ANY)), out_specs=hbm, input_output_aliases={0: 0},
        compiler_params=pltpu.CompilerParams(has_side_effects=pltpu.SideEffectType.DATAFLOW_SIDE_EFFECTING),
    )(full, send, recv, after)


def _chip_exchange(name, arr):
    def body(src, dst, send, recv, loc):
        x, y, c = _place()
        r = 2 * x + y
        chips = _other_chips(x, y)

        def cp(j, slot):
            return pltpu.make_async_remote_copy(src, dst.at[slot], send.at[j], recv.at[j], device_id=(*chips[j], c), device_id_type=MESH)

        mine = pltpu.make_async_copy(src, dst.at[r], loc)
        mine.start()
        for j in range(3):
            cp(j, r).start()
        for j in range(3):
            cp(j, _chip_of(chips[j])).wait_recv()
        for j in range(3):
            cp(j, r).wait_send()
        mine.wait()

    return pl.pallas_call(body, in_specs=[ANY], out_specs=ANY, out_shape=_sds((N_CHIPS, *arr.shape), arr.dtype),
                          scratch_shapes=[pltpu.SemaphoreType.DMA((3,)), pltpu.SemaphoreType.DMA((3,)), pltpu.SemaphoreType.DMA],
                          name=name)(arr)


def _gather_all(name, arr):
    def body(src, dst, send, recv, loc):
        x, y, c = _place()

        def cp(k, slot_of_me):
            px, py, pc = x ^ ((k >> 2) & 1), y ^ ((k >> 1) & 1), c ^ (k & 1)
            slot = (4 * x + 2 * y + c) if slot_of_me else (4 * px + 2 * py + pc)
            return pltpu.make_async_remote_copy(src, dst.at[slot], send.at[k - 1], recv.at[k - 1],
                                                device_id=(px, py, pc), device_id_type=MESH)

        mine = pltpu.make_async_copy(src, dst.at[4 * x + 2 * y + c], loc)
        mine.start()
        for k in range(1, N_DEV):
            cp(k, True).start()
        for k in range(1, N_DEV):
            cp(k, False).wait_recv()
        for k in range(1, N_DEV):
            cp(k, True).wait_send()
        mine.wait()

    return pl.pallas_call(body, in_specs=[ANY], out_specs=ANY, out_shape=_sds((N_DEV, *arr.shape), arr.dtype),
                          scratch_shapes=[pltpu.SemaphoreType.DMA((N_DEV - 1,)), pltpu.SemaphoreType.DMA((N_DEV - 1,)), pltpu.SemaphoreType.DMA],
                          name=name)(arr)


NN = (((1,), (0,)), ((), ()))
NT = (((1,), (1,)), ((), ()))
TN = (((0,), (0,)), ((), ()))


ALL = slice(None)


def _mm(name, grid, dims, a, a_spec, b, b_spec, extras, extra_specs, out_shapes, out_specs, acc_shape, epilogue, plan=None):
    n_k = grid[2]
    n_e = len(extras)
    n_o = len(out_shapes)

    def body(*refs):
        a_ref, b_ref = refs[0], refs[1]
        e_refs = refs[2:2 + n_e]
        o_refs = refs[2 + n_e:2 + n_e + n_o]
        p = lax.dot_general(a_ref[...].astype(BF), b_ref[...].astype(BF), dims, preferred_element_type=F32)
        if n_k == 1:
            epilogue(p, e_refs, o_refs, ALL)
        else:
            acc = refs[-1]
            k = pl.program_id(2)

            @pl.when(k == 0)
            def _():
                acc[...] = p

            @pl.when(k > 0)
            def _():
                acc[...] += p

            @pl.when(k == n_k - 1)
            def _():
                epilogue(acc[...], e_refs, o_refs, ALL)

    scratch = [] if n_k == 1 else [pltpu.VMEM(acc_shape, F32)]
    return _call(name, body, grid, [a_spec, b_spec, *extra_specs], out_specs, out_shapes, [a, b, *extras], scratch,
                 ("parallel", "parallel", "arbitrary"), plan)


def _store(p, e, o, rs):
    o[0][rs, :] = p.astype(o[0].dtype)


def _rms_mod_fwd(name, x, gain, shift, scale, plan=None):
    S, D = x.shape
    ts = _pick(S, (512,))

    def body(x_ref, g_ref, sh_ref, sc_ref, h_ref):
        xv = x_ref[...]
        r = lax.rsqrt(jnp.mean(xv * xv, axis=-1, keepdims=True) + EPS)
        n = xv * r * g_ref[...]
        h_ref[...] = (n * (1.0 + sc_ref[...]) + sh_ref[...]).astype(BF)

    row = pl.BlockSpec((ts, D), lambda i: (i, 0))
    vec = pl.BlockSpec((1, D), lambda i: (0, 0))
    return _call(name, body, (S // ts,), [row, vec, vec, vec], [row], [_sds((S, D), BF)], [x, gain, shift, scale],
                 sem=("parallel",), plan=plan)[0]


def _acc_rows(acc_ref, first, part):
    @pl.when(first)
    def _():
        acc_ref[...] = part

    @pl.when(jnp.logical_not(first))
    def _():
        acc_ref[...] += part


def _rms_mod_bwd(name, dh, x, dres, gain, scale, f=None, coef=None, plan=None):
    S, D = x.shape
    ts = _pick(S, (256,))
    gated = f is not None

    def body(dh_ref, x_ref, dr_ref, g_ref, sc_ref, *rest):
        xv = x_ref[...]
        dhv = dh_ref[...]
        g = g_ref[...]
        r = lax.rsqrt(jnp.mean(xv * xv, axis=-1, keepdims=True) + EPS)
        xhat = xv * r
        dn = dhv * (1.0 + sc_ref[...])
        dxhat = dn * g
        dx = dr_ref[...] + r * (dxhat - xhat * jnp.mean(dxhat * xhat, axis=-1, keepdims=True))
        rows = [jnp.sum(dhv, axis=0, keepdims=True), jnp.sum(dhv * (xhat * g), axis=0, keepdims=True),
                jnp.sum(dn * xhat, axis=0, keepdims=True)]
        if gated:
            f_ref, c_ref, dx_ref, df_ref, acc_ref = rest
            df_ref[...] = (dx * c_ref[...]).astype(BF)
            rows.append(jnp.sum(dx * f_ref[...].astype(F32), axis=0, keepdims=True))
        else:
            dx_ref, acc_ref = rest
        dx_ref[...] = dx
        _acc_rows(acc_ref, pl.program_id(0) == 0, jnp.concatenate(rows + [jnp.zeros((8 - len(rows), D), F32)], axis=0))

    row = pl.BlockSpec((ts, D), lambda i: (i, 0))
    vec = pl.BlockSpec((1, D), lambda i: (0, 0))
    acc = pl.BlockSpec((8, D), lambda i: (0, 0))
    if gated:
        return _call(name, body, (S // ts,), [row, row, row, vec, vec, row, vec], [row, row, acc],
                     [_sds((S, D), F32), _sds((S, D), BF), _sds((8, D), F32)], [dh, x, dres, gain, scale, f, coef],
                     sem=("arbitrary",), plan=plan)
    return _call(name, body, (S // ts,), [row, row, row, vec, vec], [row, acc],
                 [_sds((S, D), F32), _sds((8, D), F32)], [dh, x, dres, gain, scale], sem=("arbitrary",), plan=plan)


def _loss_bwd(x3, target, f, coef):
    S, D = x3.shape
    ts = _pick(S, (512,))

    def body(x_ref, t_ref, f_ref, c_ref, dx_ref, df_ref, acc_ref):
        e = x_ref[...] - t_ref[...]
        dx = e * (1.0 / D)
        dx_ref[...] = dx
        df_ref[...] = (dx * c_ref[...]).astype(BF)
        part = jnp.concatenate([jnp.sum(e * e, axis=0, keepdims=True), jnp.sum(dx * f_ref[...].astype(F32), axis=0, keepdims=True),
                                jnp.zeros((6, D), F32)], axis=0)
        _acc_rows(acc_ref, pl.program_id(0) == 0, part)

    row = pl.BlockSpec((ts, D), lambda i: (i, 0))
    return _call("loss_bwd", body, (S // ts,), [row, row, row, pl.BlockSpec((1, D), lambda i: (0, 0))],
                 [row, row, pl.BlockSpec((8, D), lambda i: (0, 0))],
                 [_sds((S, D), F32), _sds((S, D), BF), _sds((8, D), F32)], [x3, target, f, coef], sem=("arbitrary",))


def _silu_parts(g):
    s = jax.nn.sigmoid(g)
    return s, g * s


def _ffn_up(name, h, wgu4, plan=None):
    S, D = h.shape
    SH = wgu4.shape[2]
    F = 2 * SH
    tm = _pick(S, (512,))
    tn = _pick(SH, (1408, 256))
    nts = SH // tn

    def body(h_ref, wg_ref, wu_ref, gu_ref, act_ref):
        hv = h_ref[...]
        g = jnp.dot(hv, wg_ref[...], preferred_element_type=F32)
        u = jnp.dot(hv, wu_ref[...], preferred_element_type=F32)
        gu_ref[0] = g.astype(BF)
        gu_ref[1] = u.astype(BF)
        act_ref[...] = (_silu_parts(g)[1] * u).astype(BF)

    return _call(name, body, (S // tm, F // tn),
                 [pl.BlockSpec((tm, D), lambda i, j: (i, 0)),
                  pl.BlockSpec((None, D, tn), lambda i, j: (j // nts, 0, j % nts)),
                  pl.BlockSpec((None, D, tn), lambda i, j: (2 + j // nts, 0, j % nts))],
                 [pl.BlockSpec((2, tm, tn), lambda i, j: (0, i, j)), pl.BlockSpec((tm, tn), lambda i, j: (i, j))],
                 [_sds((2, S, F), BF), _sds((S, F), BF)], [h, wgu4, wgu4], sem=("parallel", "parallel"), plan=plan)


def _mm_residual(name, a, w, x_in, coef, plan=None):
    S, K = a.shape
    D = w.shape[1]
    tm = _pick(S, (1024,))
    tn = _pick(D, (512,))
    tk = K

    def epi(p, e, o, rs):
        o[0][rs, :] = e[0][rs, :] + e[1][...] * p
        o[1][rs, :] = p.astype(BF)

    tile = pl.BlockSpec((tm, tn), lambda i, j, k: (i, j))
    return _mm(name, (S // tm, D // tn, K // tk), NN,
               a, pl.BlockSpec((tm, tk), lambda i, j, k: (i, k)),
               w, pl.BlockSpec((tk, tn), lambda i, j, k: (k, j)),
               [x_in, coef], [tile, pl.BlockSpec((1, tn), lambda i, j, k: (0, j))],
               [_sds((S, D), F32), _sds((S, D), BF)], [tile, tile], (tm, tn), epi, plan)


def _ffn_dact(name, df, wd, gu, plan=None):
    S, D = df.shape
    F = wd.shape[0]
    tm = _pick(S, (512,))
    tn = _pick(F, (1408, 256))

    def epi(p, e, o, rs):
        g = e[0][0, rs, :].astype(F32)
        u = e[0][1, rs, :].astype(F32)
        s, sg = _silu_parts(g)
        o[0][0, rs, :] = (p * u * (s * (1.0 + g * (1.0 - s)))).astype(BF)
        o[0][1, rs, :] = (p * sg).astype(BF)

    pair = pl.BlockSpec((2, tm, tn), lambda i, j, k: (0, i, j))
    return _mm(name, (S // tm, F // tn, 1), NT,
               df, pl.BlockSpec((tm, D), lambda i, j, k: (i, 0)),
               wd, pl.BlockSpec((tn, D), lambda i, j, k: (j, 0)),
               [gu], [pair], [_sds((2, S, F), BF)], [pair], None, epi, plan)[0]


def _ffn_dh(name, dgu, wgu4, plan=None):
    _, S, F = dgu.shape
    _, D, SH = wgu4.shape
    tm = _pick(S, (1024,))
    tn = _pick(D, (512,))

    def body(a_ref, b_ref, o_ref, acc):
        k = pl.program_id(2)
        p = lax.dot_general(a_ref[:, :SH], b_ref[0], NT, preferred_element_type=F32)
        p = p + lax.dot_general(a_ref[:, SH:], b_ref[1], NT, preferred_element_type=F32)

        @pl.when(k == 0)
        def _():
            acc[...] = p

        @pl.when(k == 1)
        def _():
            o_ref[...] = acc[...] + p

    return _call(name, body, (S // tm, D // tn, 2),
                 [pl.BlockSpec((None, tm, F), lambda i, j, k: (k, i, 0)), pl.BlockSpec((2, tn, SH), lambda i, j, k: (k, j, 0))],
                 [pl.BlockSpec((tm, tn), lambda i, j, k: (i, j))], [_sds((S, D), F32)], [dgu, wgu4],
                 [pltpu.VMEM((tm, tn), F32)], ("parallel", "parallel", "arbitrary"), plan)[0]


def _ffn_dwgu(name, h, dgu, plan=None):
    _, S, F = dgu.shape
    D = h.shape[1]
    SH = F // 2
    tk1 = _pick(D, (512,))
    tn = _pick(SH, (1408, 256))
    ts = _pick(S, (4096, 1024))
    npj = F // tn
    nsj = SH // tn
    return _mm(name, (D // tk1, 2 * npj, S // ts), TN,
               h, pl.BlockSpec((ts, tk1), lambda i, j, k: (k, i)),
               dgu, pl.BlockSpec((None, ts, tn), lambda i, j, k: (j // npj, k, j % npj)),
               [], [], [_sds((4, D, SH), BF)],
               [pl.BlockSpec((None, tk1, tn), lambda i, j, k: (j // nsj, i, j % nsj))], (tk1, tn), _store, plan)[0]


def _mm_tn(name, a, b, tk1_prefs, tn_prefs, plan=None):
    S, K1 = a.shape
    N = b.shape[1]
    tk1 = _pick(K1, tk1_prefs)
    tn = _pick(N, tn_prefs)
    ts = _pick(S, (4096, 1024))
    return _mm(name, (K1 // tk1, N // tn, S // ts), TN,
               a, pl.BlockSpec((ts, tk1), lambda i, j, k: (k, i)),
               b, pl.BlockSpec((ts, tn), lambda i, j, k: (k, j)),
               [], [], [_sds((K1, N), BF)], [pl.BlockSpec((tk1, tn), lambda i, j, k: (i, j))], (tk1, tn), _store, plan)[0]


def _pool_window(ext, w, back):
    n = ext.shape[0]
    s = ext
    for step in (1, 2, 4, 8):
        sh = pltpu.roll(s, (n - step) if back else step, axis=0)
        s = jnp.where(w > step, s + sh, s)
    return s


def _pool_fwd(z, PW):
    S = z.shape[0]
    tc = _pick(S, (1024,))
    bpg = (PW // 4) // LANES
    H = POOL_MAX_W

    def body(prev_ref, u_ref, o_ref):
        i = pl.program_id(0)
        j = pl.program_id(1)
        w = lax.shift_left(jnp.int32(2), j // bpg)
        u = u_ref[...]
        prev = jnp.where(i > 0, prev_ref[...], 0.0)
        s = _pool_window(jnp.concatenate([prev, u], axis=0), w, False)[H:]
        t = i * tc + lax.broadcasted_iota(jnp.int32, (tc, LANES), 0)
        cnt = jnp.minimum(t + 1, w).astype(F32)
        o_ref[...] = (s / cnt - u).astype(BF)

    r = tc // H
    return _call("pool_fwd", body, (S // tc, PW // LANES),
                 [pl.BlockSpec((H, LANES), lambda i, j: (jnp.maximum(i * r - 1, 0), j)),
                  pl.BlockSpec((tc, LANES), lambda i, j: (i, j))],
                 [pl.BlockSpec((tc, LANES), lambda i, j: (i, j))], [_sds((S, PW), BF)], [z, z], sem=("parallel", "parallel"))[0]


def _pool_bwd(dpooled):
    S, PW = dpooled.shape
    tc = _pick(S, (1024,))
    bpg = (PW // 4) // LANES
    H = POOL_MAX_W
    last = S // tc - 1

    def body(dp_ref, nxt_ref, o_ref):
        i = pl.program_id(0)
        j = pl.program_id(1)
        w = lax.shift_left(jnp.int32(2), j // bpg)
        dp = dp_ref[...]
        nxt = jnp.where(i < last, nxt_ref[...], 0.0)
        ext = jnp.concatenate([dp, nxt], axis=0)
        t = i * tc + lax.broadcasted_iota(jnp.int32, (tc + H, LANES), 0)
        cnt = jnp.minimum(t + 1, w).astype(F32)
        s = _pool_window(ext / cnt, w, True)[:tc]
        o_ref[...] = (s - dp).astype(BF)

    r = tc // H
    nh = S // H - 1
    return _call("pool_bwd", body, (S // tc, PW // LANES),
                 [pl.BlockSpec((tc, LANES), lambda i, j: (i, j)),
                  pl.BlockSpec((H, LANES), lambda i, j: (jnp.minimum((i + 1) * r, nh), j))],
                 [pl.BlockSpec((tc, LANES), lambda i, j: (i, j))], [_sds((S, PW), BF)], [dpooled, dpooled],
                 sem=("parallel", "parallel"))[0]


def _pool_mix(pooled, pm, scale):
    S, PW = pooled.shape
    gw = PW // 4
    ts = _pick(S, (1024,))

    def epi(p, e, o, rs):
        o[0][rs, :] = (p * e[0][...]).astype(BF)

    tile = pl.BlockSpec((ts, gw), lambda i, j, k: (i, j))
    return _mm("pool_mix", (S // ts, 4, 1), NN, pooled, tile,
               pm, pl.BlockSpec((None, gw, gw), lambda i, j, k: (j, 0, 0)),
               [scale], [pl.BlockSpec((1, gw), lambda i, j, k: (0, j))], [_sds((S, PW), BF)], [tile], None, epi)[0]


def _pool_mix_bwd(pooled, pm, scale, dmixed):
    S, PW = pooled.shape
    gw = PW // 4
    ts = _pick(S, (1024,))

    def body(p_ref, pm_ref, sc_ref, dm_ref, dp_ref, dpm_ref, dsc_ref):
        i = pl.program_id(1)
        p = p_ref[...]
        w = pm_ref[...]
        dm = dm_ref[...]
        pre = jnp.dot(p, w, preferred_element_type=F32)
        dmp = (dm * sc_ref[...]).astype(BF)
        dp_ref[...] = lax.dot_general(dmp, w, NT, preferred_element_type=F32)
        dw = lax.dot_general(p, dmp, TN, preferred_element_type=F32)
        ds = jnp.concatenate([jnp.sum(dm * pre, axis=0, keepdims=True), jnp.zeros((7, gw), F32)], axis=0)
        _acc_rows(dpm_ref, i == 0, dw)
        _acc_rows(dsc_ref, i == 0, ds)

    tile = pl.BlockSpec((ts, gw), lambda g, i: (i, g))
    return _call("pool_mix_bwd", body, (4, S // ts),
                 [tile, pl.BlockSpec((None, gw, gw), lambda g, i: (g, 0, 0)), pl.BlockSpec((1, gw), lambda g, i: (0, g)), tile],
                 [tile, pl.BlockSpec((None, gw, gw), lambda g, i: (g, 0, 0)), pl.BlockSpec((8, gw), lambda g, i: (0, g))],
                 [_sds((S, PW), F32), _sds((4, gw, gw), F32), _sds((8, PW), F32)], [pooled, pm, scale, dmixed],
                 sem=("parallel", "arbitrary"))


def _bucket_onehot():
    ql = np.arange(BLK)[:, None]
    j = np.arange(2 * BLK)[None, :]
    d = BLK + ql - j
    n = np.clip(d, 0, None)
    nf = np.maximum(n, 1).astype(np.float32)
    max_exact = NUM_BUCKETS // 2
    large = max_exact + (np.log(nf / max_exact) / np.log(BLK / max_exact) * (NUM_BUCKETS - max_exact)).astype(np.int32)
    large = np.minimum(large, NUM_BUCKETS - 1)
    bucket = np.where(n < max_exact, n, large).astype(np.int32)
    valid = (d >= 0) & (d < BLK)
    oh = (bucket[None] == np.arange(NUM_BUCKETS)[:, None, None]) & valid[None]
    return oh.reshape(NUM_BUCKETS, BLK * 2 * BLK)


def _three_bf16(v):
    hi = v.astype(BF)
    r1 = v - hi.astype(F32)
    mid = r1.astype(BF)
    lo = (r1 - mid.astype(F32)).astype(BF)
    return hi, mid, lo


def _bias_table(rel_bias):
    oh = jnp.asarray(_bucket_onehot(), BF)
    tn = 4096

    def body(rb_ref, oh_ref, o_ref):
        o = oh_ref[...]
        hi, mid, lo = _three_bf16(rb_ref[...])
        acc = lax.dot_general(hi, o, TN, preferred_element_type=F32)
        acc = acc + lax.dot_general(mid, o, TN, preferred_element_type=F32)
        acc = acc + lax.dot_general(lo, o, TN, preferred_element_type=F32)
        on_band = jnp.sum(o.astype(F32), axis=0, keepdims=True) > 0.5
        o_ref[...] = jnp.where(on_band, acc, NEG_INF)

    n = oh.shape[1]
    return _call("bias_table", body, (n // tn,),
                 [pl.BlockSpec((NUM_BUCKETS, N_HEADS), lambda i: (0, 0)), pl.BlockSpec((NUM_BUCKETS, tn), lambda i: (0, i))],
                 [pl.BlockSpec((N_HEADS, tn), lambda i: (0, i))], [_sds((N_HEADS, n), F32)], [rel_bias, oh], sem=("parallel",))[0]


def _rel_bias_grad(dl):
    oh = jnp.asarray(_bucket_onehot(), BF)
    n = oh.shape[1]
    tk = 4096

    def body(dl_ref, oh_ref, o_ref):
        o = oh_ref[...]
        hi, mid, lo = _three_bf16(dl_ref[...])
        acc = lax.dot_general(o, hi, NT, preferred_element_type=F32)
        acc = acc + lax.dot_general(o, mid, NT, preferred_element_type=F32)
        acc = acc + lax.dot_general(o, lo, NT, preferred_element_type=F32)
        _acc_rows(o_ref, pl.program_id(0) == 0, acc)

    return _call("rel_bias_grad", body, (n // tk,),
                 [pl.BlockSpec((N_HEADS, tk), lambda i: (0, i)), pl.BlockSpec((NUM_BUCKETS, tk), lambda i: (0, i))],
                 [pl.BlockSpec((NUM_BUCKETS, N_HEADS), lambda i: (0, 0))], [_sds((NUM_BUCKETS, N_HEADS), F32)], [dl, oh],
                 sem=("arbitrary",))[0]


def _lo_half(shape):
    return lax.broadcasted_iota(jnp.int32, shape, 1) < HEAD_DIM


def _half_sum(x, lo):
    s_lo = jnp.sum(jnp.where(lo, x, 0.0), axis=-1, keepdims=True)
    s_hi = jnp.sum(jnp.where(lo, 0.0, x), axis=-1, keepdims=True)
    return jnp.where(lo, s_lo, s_hi)


def _norm2(x, lo):
    r = lax.rsqrt(_half_sum(x * x, lo) * (1.0 / HEAD_DIM) + EPS)
    return x * r, r


def _norm2_bwd(dy, xhat, r, gain, lo):
    dxhat = dy * gain
    dx = r * (dxhat - xhat * (_half_sum(dxhat * xhat, lo) * (1.0 / HEAD_DIM)))
    return dx, dy * xhat


def _swap(x):
    return pltpu.roll(x, HEAD_DIM, axis=1)


def _pair_rows(x, kk):
    return jnp.concatenate([x, _swap(x)] if kk == 0 else [_swap(x), x], axis=0)


def _attn_probs(n, kk, jp0, npairs, zq_ref, K, qg, bias_ref, sink_ref):
    lo_q = _lo_half((BLK, LANES))
    rows, qhats, qrs = [], [], []
    for jp in range(jp0, jp0 + npairs):
        qhat, qr = _norm2(zq_ref[:, jp * LANES:(jp + 1) * LANES], lo_q)
        rows.append(_pair_rows(qhat * qg * (HEAD_DIM ** -0.5), kk))
        qhats.append(qhat)
        qrs.append(qr)
    Q = jnp.concatenate(rows, axis=0).astype(BF)
    cols = slice(jp0 * 2 * BLK, (jp0 + npairs) * 2 * BLK)
    l = lax.dot_general(K, Q, NT, preferred_element_type=F32) + bias_ref[kk, :, cols]
    l = jnp.concatenate([jnp.where(n == 0, NEG_INF, l[:BLK]), l[BLK:]], axis=0)
    sink = sink_ref[kk, :, cols]
    m = jnp.maximum(jnp.max(l, axis=0, keepdims=True), sink)
    e = jnp.exp(l - m)
    es = jnp.exp(sink - m)
    inv = 1.0 / (jnp.sum(e, axis=0, keepdims=True) + es)
    return Q, e * inv, es * inv, qhats, qrs


def _attn_specs(o_q, o_k):
    nq = o_q // 512
    nk = o_k // LANES
    prev = lambda n: (jnp.maximum(n - 1, 0), nk)
    prev_v = lambda n: (jnp.maximum(n - 1, 0), nk + 1)
    return [pl.BlockSpec((BLK, 512), lambda n: (n, nq)), pl.BlockSpec((BLK, 512), lambda n: (n, nq + 1)),
            pl.BlockSpec((BLK, LANES), prev), pl.BlockSpec((BLK, LANES), lambda n: (n, nk)),
            pl.BlockSpec((BLK, LANES), prev_v), pl.BlockSpec((BLK, LANES), lambda n: (n, nk + 1)),
            pl.BlockSpec((1, LANES), lambda n: (0, 0)), pl.BlockSpec((1, LANES), lambda n: (0, 0)),
            pl.BlockSpec((N_KV, 1, 8 * BLK), lambda n: (0, 0, 0)),
            pl.BlockSpec((N_KV, 2 * BLK, 8 * BLK), lambda n: (0, 0, 0))]


def _attn_fwd(z, o_q, o_k, qg2, kg2, sink_rows, bias, plan=None):
    S = z.shape[0]

    def body(zq0, zq1, zkp, zkc, zvp, zvc, qg_ref, kg_ref, sink_ref, bias_ref, o_ref):
        n = pl.program_id(0)
        lo_k = _lo_half((2 * BLK, LANES))
        lo_q = _lo_half((BLK, LANES))
        khat, _ = _norm2(jnp.concatenate([zkp[...], zkc[...]], axis=0), lo_k)
        kn = khat * kg_ref[...]
        vb = jnp.concatenate([zvp[...], zvc[...]], axis=0).astype(BF)
        for kk, zq in enumerate((zq0, zq1)):
            K = jnp.where(lo_k if kk == 0 else jnp.logical_not(lo_k), kn, 0.0).astype(BF)
            for jp in range(4):
                _, p, _, _, _ = _attn_probs(n, kk, jp, 1, zq, K, qg_ref[...], bias_ref, sink_ref)
                r = lax.dot_general(p.astype(BF), vb, TN, preferred_element_type=F32)
                ev, od = r[:BLK], r[BLK:]
                pair = jnp.where(lo_q, ev, _swap(od)) if kk == 0 else jnp.where(lo_q, _swap(ev), od)
                c0 = (4 * kk + jp) * LANES
                o_ref[:, c0:c0 + LANES] = pair.astype(BF)

    return _call("attn_fwd", body, (S // BLK,), _attn_specs(o_q, o_k), [pl.BlockSpec((BLK, ATT_W), lambda n: (n, 0))],
                 [_sds((S, ATT_W), BF)], [z, z, z, z, z, z, qg2, kg2, sink_rows, bias], sem=("parallel",), plan=plan)[0]


def _attn_bwd(z, o_q, o_k, qg2, kg2, sink_rows, bias, dout, plan=None):
    S = z.shape[0]

    def body(zq0, zq1, zkp, zkc, zvp, zvc, qg_ref, kg_ref, sink_ref, bias_ref, do_ref,
             dq_ref, dkp_ref, dkc_ref, dvp_ref, dvc_ref, dl_ref, dsink_ref, dgain_ref):
        n = pl.program_id(0)
        lo_k = _lo_half((2 * BLK, LANES))
        lo_q = _lo_half((BLK, LANES))
        qg = qg_ref[...]
        kg = kg_ref[...]
        khat, kr = _norm2(jnp.concatenate([zkp[...], zkc[...]], axis=0), lo_k)
        kn = khat * kg
        vf = jnp.concatenate([zvp[...], zvc[...]], axis=0)

        @pl.when(n == 0)
        def _():
            dl_ref[...] = jnp.zeros_like(dl_ref)
            dsink_ref[...] = jnp.zeros_like(dsink_ref)
            dgain_ref[...] = jnp.zeros_like(dgain_ref)

        dkn = jnp.zeros((2 * BLK, LANES), F32)
        dvb = jnp.zeros((2 * BLK, LANES), F32)
        dqg = jnp.zeros((1, LANES), F32)
        for kk, zq in enumerate((zq0, zq1)):
            half_k = lo_k if kk == 0 else jnp.logical_not(lo_k)
            K = jnp.where(half_k, kn, 0.0).astype(BF)
            V = jnp.where(half_k, vf, 0.0).astype(BF)
            Q, p, ps, qhats, qrs = _attn_probs(n, kk, 0, 4, zq, K, qg, bias_ref, sink_ref)
            dO = jnp.concatenate([_pair_rows(do_ref[:, (4 * kk + jp) * LANES:(4 * kk + jp + 1) * LANES], kk) for jp in range(4)],
                                 axis=0).astype(BF)
            dP = lax.dot_general(V, dO, NT, preferred_element_type=F32)
            delta = jnp.sum(p * dP, axis=0, keepdims=True)
            dS = p * (dP - delta)
            dsink_ref[kk] += -ps * delta
            dl_ref[kk] += dS
            dSb = dS.astype(BF)
            dvb = dvb + jnp.where(half_k, jnp.dot(p.astype(BF), dO, preferred_element_type=F32), 0.0)
            dkn = dkn + jnp.where(half_k, jnp.dot(dSb, Q, preferred_element_type=F32), 0.0)
            dQ = lax.dot_general(dSb, K, TN, preferred_element_type=F32) * (HEAD_DIM ** -0.5)
            for jp in range(4):
                ev = dQ[(2 * jp) * BLK:(2 * jp + 1) * BLK]
                od = dQ[(2 * jp + 1) * BLK:(2 * jp + 2) * BLK]
                dy = (ev + _swap(od)) if kk == 0 else (_swap(ev) + od)
                dx, gq = _norm2_bwd(dy, qhats[jp], qrs[jp], qg, lo_q)
                dqg = dqg + jnp.sum(gq, axis=0, keepdims=True)
                c0 = (4 * kk + jp) * LANES
                dq_ref[:, c0:c0 + LANES] = dx.astype(BF)
        dk, gk = _norm2_bwd(dkn, khat, kr, kg, lo_k)
        dkp_ref[...] = dk[:BLK]
        dkc_ref[...] = dk[BLK:]
        dvp_ref[...] = dvb[:BLK]
        dvc_ref[...] = dvb[BLK:]
        dgain_ref[...] += jnp.concatenate([dqg, jnp.sum(gk, axis=0, keepdims=True), jnp.zeros((6, LANES), F32)], axis=0)

    blk = pl.BlockSpec((BLK, LANES), lambda n: (n, 0))
    wide = pl.BlockSpec((BLK, ATT_W), lambda n: (n, 0))
    return _call(
        "attn_bwd", body, (S // BLK,), _attn_specs(o_q, o_k) + [wide],
        [wide, blk, blk, blk, blk, pl.BlockSpec((N_KV, 2 * BLK, 8 * BLK), lambda n: (0, 0, 0)),
         pl.BlockSpec((N_KV, 1, 8 * BLK), lambda n: (0, 0, 0)), pl.BlockSpec((8, LANES), lambda n: (0, 0))],
        [_sds((S, ATT_W), BF), _sds((S, LANES), F32), _sds((S, LANES), F32), _sds((S, LANES), F32), _sds((S, LANES), F32),
         _sds((N_KV, 2 * BLK, 8 * BLK), F32), _sds((N_KV, 1, 8 * BLK), F32), _sds((8, LANES), F32)],
        [z, z, z, z, z, z, qg2, kg2, sink_rows, bias, dout], sem=("arbitrary",), plan=plan)


def _kv_combine(dkp, dkc, dvp, dvc):
    S = dkc.shape[0]
    last = S // BLK - 1

    def body(kp_ref, kc_ref, vp_ref, vc_ref, dk_ref, dv_ref):
        more = pl.program_id(0) < last
        dk_ref[...] = (kc_ref[...] + jnp.where(more, kp_ref[...], 0.0)).astype(BF)
        dv_ref[...] = (vc_ref[...] + jnp.where(more, vp_ref[...], 0.0)).astype(BF)

    cur = pl.BlockSpec((BLK, LANES), lambda n: (n, 0))
    nxt = pl.BlockSpec((BLK, LANES), lambda n: (jnp.minimum(n + 1, last), 0))
    return _call("kv_combine", body, (S // BLK,), [nxt, cur, nxt, cur], [cur, cur],
                 [_sds((S, LANES), BF), _sds((S, LANES), BF)], [dkp, dkc, dvp, dvc], sem=("parallel",))


def _merge_fwd(mixed, attn, wpu4, wau4, z, o_ga, plan=None):
    S, PW = mixed.shape
    _, _, CS = wpu4.shape
    D = 4 * CS
    tm = _pick(S, (1024,))
    tn = 256
    nsj = CS // tn
    na = o_ga // tn
    nb = (o_ga + D) // tn

    def body(m_ref, a_ref, wp_ref, wa_ref, ga_ref, gb_ref, mg_ref, yy_ref):
        yp = jnp.dot(m_ref[...], wp_ref[...], preferred_element_type=F32)
        ya = jnp.dot(a_ref[...], wa_ref[...], preferred_element_type=F32)
        mg_ref[...] = (jax.nn.sigmoid(ga_ref[...]) * yp + jax.nn.sigmoid(gb_ref[...]) * ya).astype(BF)
        yy_ref[0] = yp.astype(BF)
        yy_ref[1] = ya.astype(BF)

    return _call("merge_fwd", body, (S // tm, D // tn),
                 [pl.BlockSpec((tm, PW), lambda i, j: (i, 0)), pl.BlockSpec((tm, ATT_W), lambda i, j: (i, 0)),
                  pl.BlockSpec((None, PW, tn), lambda i, j: (j // nsj, 0, j % nsj)),
                  pl.BlockSpec((None, ATT_W, tn), lambda i, j: (j // nsj, 0, j % nsj)),
                  pl.BlockSpec((tm, tn), lambda i, j: (i, na + j)), pl.BlockSpec((tm, tn), lambda i, j: (i, nb + j))],
                 [pl.BlockSpec((tm, tn), lambda i, j: (i, j)), pl.BlockSpec((2, tm, tn), lambda i, j: (0, i, j))],
                 [_sds((S, D), BF), _sds((2, S, D), BF)], [mixed, attn, wpu4, wau4, z, z], sem=("parallel", "parallel"), plan=plan)


def _merge_bwd(do, wo, z, o_ga, yy, plan=None):
    S, D = do.shape
    tm = _pick(S, (1024,))
    tn = 256
    na = o_ga // tn
    nb = (o_ga + D) // tn

    def epi(p, e, o, rs):
        sa = jax.nn.sigmoid(e[0][rs, :])
        sb = jax.nn.sigmoid(e[1][rs, :])
        yp = e[2][0, rs, :].astype(F32)
        ya = e[2][1, rs, :].astype(F32)
        o[0][0, rs, :] = (p * yp * sa * (1.0 - sa)).astype(BF)
        o[0][1, rs, :] = (p * ya * sb * (1.0 - sb)).astype(BF)
        o[1][0, rs, :] = (p * sa).astype(BF)
        o[1][1, rs, :] = (p * sb).astype(BF)

    pair = pl.BlockSpec((2, tm, tn), lambda i, j, k: (0, i, j))
    return _mm("merge_bwd", (S // tm, D // tn, 1), NT,
               do, pl.BlockSpec((tm, D), lambda i, j, k: (i, 0)),
               wo, pl.BlockSpec((tn, D), lambda i, j, k: (j, 0)),
               [z, z, yy], [pl.BlockSpec((tm, tn), lambda i, j, k: (i, na + j)), pl.BlockSpec((tm, tn), lambda i, j, k: (i, nb + j)), pair],
               [_sds((2, S, D), BF), _sds((2, S, D), BF)], [pair, pair], None, epi, plan)


def _mm_up_t(name, dyy, which, w4):
    _, S, D = dyy.shape
    _, K, CS = w4.shape
    tm = _pick(S, (1024,))
    return _mm(name, (S // tm, 1, N_CHIPS), NT,
               dyy, pl.BlockSpec((None, tm, CS), lambda i, j, k: (which, i, k)),
               w4, pl.BlockSpec((None, K, CS), lambda i, j, k: (k, 0, 0)),
               [], [], [_sds((S, K), F32)], [pl.BlockSpec((tm, K), lambda i, j, k: (i, 0))], (tm, K), _store)[0]


def _mm_up_dw(name, a, dyy, which):
    _, S, D = dyy.shape
    K = a.shape[1]
    CS = D // N_CHIPS
    ts = _pick(S, (1024,))
    return _mm(name, (1, N_CHIPS, S // ts), TN,
               a, pl.BlockSpec((ts, K), lambda i, j, k: (k, 0)),
               dyy, pl.BlockSpec((None, ts, CS), lambda i, j, k: (which, k, j)),
               [], [], [_sds((N_CHIPS, K, CS), BF)], [pl.BlockSpec((None, K, CS), lambda i, j, k: (j, 0, 0))], (K, CS), _store)[0]


def _adamw(w, g, m, v):
    m = ADAM_B1 * m + (1.0 - ADAM_B1) * g
    v = ADAM_B2 * v + (1.0 - ADAM_B2) * (g * g)
    m_hat = m / (1.0 - ADAM_B1 ** ADAM_STEP)
    v_hat = v / (1.0 - ADAM_B2 ** ADAM_STEP)
    delta = -ADAM_LR * (m_hat / (jnp.sqrt(v_hat) + ADAM_EPS) + ADAM_WD * w)
    return delta, m, v


def _mod_fwd(c_all, w_ada, b_sh):
    D, cols = w_ada.shape
    tn = cols // 9

    def body(c_ref, w_ref, b_ref, o_ref):
        cv = c_ref[...]
        sc = (cv * jax.nn.sigmoid(cv)).astype(BF)
        o_ref[...] = jnp.dot(sc, w_ref[...].astype(BF), preferred_element_type=F32) + b_ref[...]

    return _call("mod_fwd", body, (9,),
                 [pl.BlockSpec((N_DEV, D), lambda j: (0, 0)), pl.BlockSpec((D, tn), lambda j: (0, j)), pl.BlockSpec((1, tn), lambda j: (0, j))],
                 [pl.BlockSpec((N_DEV, tn), lambda j: (0, j))], [_sds((N_DEV, cols), F32)], [c_all, w_ada, b_sh], sem=("parallel",))[0]


def _wada_bwd(c_all, dmod_sh, w, m, v, plan=None):
    D, cols = w.shape
    tn = cols // 18

    def body(c_ref, d_ref, w_ref, m_ref, v_ref, g_ref, dl_ref, nm_ref, nv_ref):
        cv = c_ref[...]
        sc = (cv * jax.nn.sigmoid(cv)).astype(BF)
        g = lax.dot_general(sc, d_ref[...].astype(BF), TN, preferred_element_type=F32)
        g_ref[...] = g
        dl_ref[...], nm_ref[...], nv_ref[...] = _adamw(w_ref[...], g, m_ref[...], v_ref[...])

    tile = pl.BlockSpec((D, tn), lambda j: (0, j))
    out = _sds((D, cols), F32)
    return _call("wada_bwd", body, (18,),
                 [pl.BlockSpec((N_DEV, D), lambda j: (0, 0)), pl.BlockSpec((N_DEV, tn), lambda j: (0, j)), tile, tile, tile],
                 [tile] * 4, [out] * 4, [c_all, dmod_sh, w, m, v], sem=("parallel",), plan=plan)


def _adam_2d(name, w, g, m, v):
    R, C = w.shape
    tr = _row_tile(R, 256)

    def body(w_ref, g_ref, m_ref, v_ref, dl_ref, nm_ref, nv_ref):
        dl_ref[...], nm_ref[...], nv_ref[...] = _adamw(w_ref[...], g_ref[...], m_ref[...], v_ref[...])

    tile = pl.BlockSpec((tr, C), lambda i: (i, 0))
    out = _sds((R, C), F32)
    return _call(name, body, (R // tr,), [tile] * 4, [tile] * 3, [out] * 3, [w, g, m, v], sem=("parallel",))


def _small_finish(parts, w, m, v):
    _, R, C = parts.shape

    def body(p_ref, w_ref, m_ref, v_ref, g_ref, dl_ref, nm_ref, nv_ref):
        g = p_ref[0]
        for d in range(1, N_DEV):
            g = g + p_ref[d]
        g_ref[...] = g
        dl_ref[...], nm_ref[...], nv_ref[...] = _adamw(w_ref[...], g, m_ref[...], v_ref[...])

    out = _sds((R, C), F32)
    return pl.pallas_call(body, out_shape=[out] * 4, name="small_finish",
                          compiler_params=pltpu.CompilerParams(vmem_limit_bytes=VMEM_LIMIT))(parts, w, m, v)


def _my_chip():
    return 2 * lax.axis_index("x") + lax.axis_index("y")


def _cast_into_slot(name, w):
    R, C = w.shape
    tr = _row_tile(R, 256)

    def body(w_ref, o_ref):
        o_ref[...] = w_ref[...].astype(BF)

    return _call(name, body, (R // tr,), [pl.BlockSpec((tr, C), lambda i: (i, 0))],
                 [pl.BlockSpec((None, tr, C), lambda i: (_my_chip(), i, 0))], [_sds((N_CHIPS, R, C), BF)], [w], sem=("parallel",))[0]


def _add_pair(name, p, q):
    _, H, C = q.shape
    tr = _row_tile(H, 512)
    nt = H // tr

    def body(p_ref, q_ref, o_ref):
        o_ref[...] = (p_ref[...].astype(F32) + q_ref[...].astype(F32)).astype(BF)

    tile = pl.BlockSpec((None, tr, C), lambda k, i: (k, i, 0))
    return _call(name, body, (N_CHIPS, nt), [pl.BlockSpec((None, tr, C), lambda k, i: (k, lax.axis_index("c") * nt + i, 0)), tile],
                 [tile], [_sds(q.shape, BF)], [p, q], sem=("parallel", "parallel"))[0]


def _sum_chips(name, u, t):
    _, H, C = u.shape
    tr = _row_tile(H, 256)

    def body(u_ref, t_ref, o_ref):
        r = _my_chip()
        own = t_ref[...].astype(F32)
        pick = lambda k: jnp.where(r == k, own, u_ref[k].astype(F32))
        o_ref[...] = ((pick(0) + pick(1)) + pick(2)) + pick(3)

    return _call(name, body, (H // tr,),
                 [pl.BlockSpec((N_CHIPS, tr, C), lambda i: (0, i, 0)), pl.BlockSpec((None, tr, C), lambda i: (_my_chip(), i, 0))],
                 [pl.BlockSpec((None, tr, C), lambda i: (lax.axis_index("c"), i, 0))], [_sds((2, H, C), F32)], [u, t],
                 sem=("parallel",))[0]


BIG = ("gu1", "down1", "w_in", "pool_mix", "pool_up", "attn_up", "o", "gu2", "down2")
MIX = ("o", "pool_up", "attn_up", "pool_mix")
EARLY = ("down1", "w_in", "pool_mix", "pool_up", "attn_up", "o")

SCHEDULE = {
    "+gather_gu1_d2d": ([("d2d", ("gu1",))], []),
    "ffn1_up": ([("ici", EARLY)], []),
    "+gather_early_d2d": ([("d2d", EARLY)], []),
    "ffn1_down": ([("ici", ("gu2",))], []),
    "mix_in": ([("ici", ("down2",)), ("d2d", ("gu2",))], []),
    "mix_out": ([("d2d", ("down2",))], []),
    "ffn2_dwd": ([("split", ("gu2",))], [("add", ("gu2",))]),
    "ffn2_dh": ([("owners", ("gu2",)), ("split", ("down2",))], [("add", ("down2",)), ("sum", ("gu2",))]),
    "merge_bwd": ([("owners", ("down2",)), ("join", ("gu2",))], [("sum", ("down2",)), ("adam", ("gu2",))]),
    "attn_bwd": ([("split", MIX), ("join", ("down2",))], [("add", MIX), ("adam", ("down2",))]),
    "mix_dwin": ([("owners", MIX)], [("sum", MIX)]),
    "mix_dh": ([("split", ("w_in",)), ("join", MIX)], [("add", ("w_in",)), ("adam", MIX)]),
    "ffn1_dact": ([("owners", ("w_in",))], [("sum", ("w_in",))]),
    "ffn1_dwd": ([("split", ("gu1",)), ("join", ("w_in",))], [("add", ("gu1",)), ("adam", ("w_in",))]),
    "ffn1_dh": ([("owners", ("gu1",)), ("split", ("down1",))], [("add", ("down1",)), ("sum", ("gu1",))]),
    "rms_mod_bwd1": ([("owners", ("down1",)), ("join", ("gu1",))], [("sum", ("down1",)), ("adam", ("gu1",))]),
    "+join_down1": ([("join", ("down1",))], [("adam", ("down1",))]),
}


class _Plan:
    def __init__(self, w2, m2, v2, full, D, gw):
        self.w2, self.m2, self.v2, self.full, self.D, self.gw = w2, m2, v2, dict(full), D, gw
        self.part, self.got, self.sums, self.landed, self.g = {}, {}, {}, {}, {}
        self.result = {}
        self.pending = {}

    def _make(self, op, names):
        if op == "ici":
            return _gather_ici_stage([self.full[k] for k in names])
        if op == "d2d":
            return _gather_d2d_stage([self.full[k] for k in names])
        if op == "split":
            return _split_stage([self.part[k] for k in names])
        if op == "owners":
            return _owners_stage([self.sums[k] for k in names])
        return _join_stage([self.g[k] for k in names])

    def stages(self, name):
        ops = SCHEDULE.get(name, ([], []))[0]
        return [self._make(*op) for op in ops]

    def done(self, name, outs):
        ops, local = SCHEDULE[name]
        for op, res in zip(ops, outs):
            store = {"ici": self.full, "d2d": self.full, "split": self.got, "owners": self.landed, "join": self.g}[op[0]]
            store.update(zip(op[1], res))
        for op, names in local:
            for k in names:
                if op == "add":
                    self.sums[k] = _add_pair("add_pair_" + k, self.part[k], self.got[k])
                elif op == "sum":
                    self.g[k] = _sum_chips("sum_chips_" + k, self.landed[k], self.sums[k])
                else:
                    g2 = self.g[k].reshape(self.w2[k].shape)
                    self.result[k] = (g2, *_adam_2d("adam_" + k, self.w2[k], g2, self.m2[k], self.v2[k]))

    def alone(self, name):
        self.done(name, _run_stages(name[1:], self.stages(name)))

    def depart(self, k):
        send, recv, thru, token = _gather_start("gather_" + k + "_start", self.full[k])
        self.pending[k] = (send, recv, thru)
        return token

    def arrive(self, k, after):
        send, recv, thru = self.pending.pop(k)
        self.full[k] = _gather_finish("gather_" + k + "_finish", send, recv, thru, after)

    def weight(self, k):
        D, gw, f = self.D, self.gw, self.full[k]
        if k in ("down1", "down2", "o"):
            return f.reshape(-1, D)
        if k == "pool_mix":
            return f.reshape(N_CHIPS, 4, gw // N_CHIPS, gw).transpose(1, 0, 2, 3).reshape(4, gw, gw)
        if k == "w_in":
            return f.reshape(-1, D)
        return f

    def partial(self, k, p):
        D, gw = self.D, self.gw
        if k in ("down1", "down2", "o", "w_in"):
            p = p.reshape(N_CHIPS, p.shape[0] // N_CHIPS, p.shape[1])
        elif k == "pool_mix":
            p = p.astype(BF).reshape(4, N_CHIPS, gw // N_CHIPS, gw).transpose(1, 0, 2, 3).reshape(N_CHIPS, gw, gw)
        self.part[k] = p


class _NoComm:
    def __init__(self, weights):
        self.w, self.part = weights, {}

    def stages(self, name):
        return []

    def alone(self, name):
        pass

    def arrive(self, k, after):
        pass

    def weight(self, k):
        return self.w[k]

    def partial(self, k, p):
        self.part[k] = p


def _row(a, i):
    return a[i:i + 1]


def _local_step(x, target, mod, g_ffn1, g_mix, g_ffn2, pool_scale, q_gain, k_gain, sinks, rel_bias, plan):
    S, D = x.shape
    half = 0.5 * mod
    tile2 = lambda g: jnp.concatenate([g, g], axis=1)
    qg2, kg2 = tile2(q_gain), tile2(k_gain)
    sink_rows = jnp.broadcast_to(sinks.reshape(N_KV, 1, 8, 1), (N_KV, 1, 8, BLK)).reshape(N_KV, 1, 8 * BLK)
    bias = _bias_table(rel_bias).reshape(N_KV, 8, BLK, 2 * BLK).transpose(0, 3, 1, 2).reshape(N_KV, 2 * BLK, 8 * BLK)

    h1 = _rms_mod_fwd("rms_mod_fwd1", x, g_ffn1, _row(mod, 0), _row(mod, 1))
    plan.arrive("gu1", h1)
    plan.alone("+gather_gu1_d2d")
    gu1, act1 = _ffn_up("ffn1_up", h1, plan.weight("gu1"), plan)
    plan.alone("+gather_early_d2d")
    x1, f1 = _mm_residual("ffn1_down", act1, plan.weight("down1"), x, _row(half, 2), plan)
    h2 = _rms_mod_fwd("rms_mod_fwd2", x1, g_mix, _row(mod, 3), _row(mod, 4))
    w_in_t = plan.weight("w_in")
    IN_W = w_in_t.shape[0]
    PW = plan.weight("pool_up").shape[1]
    o_q, o_k = PW, PW + ATT_W
    o_ga = o_k + 2 * KV_W
    tnz = _pick(IN_W, (1280, 256))
    tmz = _pick(S, (1024,))
    z = _mm("mix_in", (S // tmz, IN_W // tnz, 1), NT, h2, pl.BlockSpec((tmz, D), lambda i, j, k: (i, 0)),
            w_in_t, pl.BlockSpec((tnz, D), lambda i, j, k: (j, 0)), [], [], [_sds((S, IN_W), F32)],
            [pl.BlockSpec((tmz, tnz), lambda i, j, k: (i, j))], None, _store, plan)[0]
    pooled = _pool_fwd(z, PW)
    mixed = _pool_mix(pooled, plan.weight("pool_mix"), pool_scale)
    attn = _attn_fwd(z, o_q, o_k, qg2, kg2, sink_rows, bias)
    merged, yy = _merge_fwd(mixed, attn, plan.weight("pool_up"), plan.weight("attn_up"), z, o_ga)
    x2, fo = _mm_residual("mix_out", merged, plan.weight("o"), x1, _row(mod, 5), plan)
    h3 = _rms_mod_fwd("rms_mod_fwd3", x2, g_ffn2, _row(mod, 6), _row(mod, 7))
    gu2, act2 = _ffn_up("ffn2_up", h3, plan.weight("gu2"))
    x3, f2 = _mm_residual("ffn2_down", act2, plan.weight("down2"), x2, _row(half, 8))
    dx3, df2, loss_acc = _loss_bwd(x3, target, f2, _row(half, 8))

    dgu2 = _ffn_dact("ffn2_dact", df2, plan.weight("down2"), gu2)
    plan.partial("gu2", _ffn_dwgu("ffn2_dwgu", h3, dgu2))
    plan.partial("down2", _mm_tn("ffn2_dwd", act2, df2, (1408, 512), (512,), plan))
    dh3 = _ffn_dh("ffn2_dh", dgu2, plan.weight("gu2"), plan)
    dx2, do, acc3 = _rms_mod_bwd("rms_mod_bwd3", dh3, x2, dx3, g_ffn2, _row(mod, 7), fo, _row(mod, 5))

    dgab, dyy = _merge_bwd(do, plan.weight("o"), z, o_ga, yy, plan)
    plan.partial("o", _mm_tn("mix_dwo", merged, do, (1024,), (512,), plan))
    dmixed = _mm_up_t("pool_up_t", dyy, 0, plan.weight("pool_up"))
    dattn = _mm_up_t("attn_up_t", dyy, 1, plan.weight("attn_up"))
    plan.partial("pool_up", _mm_up_dw("pool_up_dw", mixed, dyy, 0))
    plan.partial("attn_up", _mm_up_dw("attn_up_dw", attn, dyy, 1))
    dpooled, dpm, dps = _pool_mix_bwd(pooled, plan.weight("pool_mix"), pool_scale, dmixed)
    plan.partial("pool_mix", dpm)
    du_pool = _pool_bwd(dpooled)
    dq, dkp, dkc, dvp, dvc, dl, dsink, dgain = _attn_bwd(z, o_q, o_k, qg2, kg2, sink_rows, bias, dattn, plan)
    dk, dv = _kv_combine(dkp, dkc, dvp, dvc)
    drb = _rel_bias_grad(dl.reshape(N_KV, 2 * BLK, 8, BLK).transpose(0, 2, 3, 1).reshape(N_HEADS, BLK * 2 * BLK))
    dz = jnp.concatenate([du_pool, dq, dk, dv, dgab[0], dgab[1]], axis=1)
    plan.partial("w_in", _mm_tn("mix_dwin", dz, h2, (1280, 256), (512,), plan))
    tnd = _pick(D, (512,))
    dh2 = _mm("mix_dh", (S // tmz, D // tnd, 1), NN, dz, pl.BlockSpec((tmz, IN_W), lambda i, j, k: (i, 0)),
              w_in_t, pl.BlockSpec((IN_W, tnd), lambda i, j, k: (0, j)), [], [], [_sds((S, D), F32)],
              [pl.BlockSpec((tmz, tnd), lambda i, j, k: (i, j))], None, _store, plan)[0]
    dx1, df1, acc2 = _rms_mod_bwd("rms_mod_bwd2", dh2, x1, dx2, g_mix, _row(mod, 4), f1, _row(half, 2))

    dgu1 =_ffn_dact("ffn1_dact", df1, plan.weight("down1"), gu1, plan)
    plan.partial("gu1", _ffn_dwgu("ffn1_dwgu", h1, dgu1, plan))
    plan.partial("down1", _mm_tn("ffn1_dwd", act1, df1, (1408, 512), (512,), plan))
    dh1 = _ffn_dh("ffn1_dh", dgu1, plan.weight("gu1"), plan)
    grad_x, acc1 = _rms_mod_bwd("rms_mod_bwd1", dh1, x, dx1, g_ffn1, _row(mod, 1), plan=plan)

    dmod = jnp.concatenate([_row(acc1, 0), _row(acc1, 1), 0.5 * _row(acc2, 3),
                            _row(acc2, 0), _row(acc2, 1), _row(acc3, 3),
                            _row(acc3, 0), _row(acc3, 1), 0.5 * _row(loss_acc, 1)], axis=0)
    fold = lambda r: r[:, :HEAD_DIM] + r[:, HEAD_DIM:]
    small = dict(
        dmod=dmod, g_ffn1=_row(acc1, 2), g_mix=_row(acc2, 2), g_ffn2=_row(acc3, 2), pool_scale=_row(dps, 0),
        q_gain=fold(_row(dgain, 0)), k_gain=fold(_row(dgain, 1)),
        sinks=jnp.sum(dsink.reshape(N_HEADS, BLK), axis=1).reshape(1, N_HEADS), rel_bias=drb,
        loss=(0.5 / D) * jnp.sum(_row(loss_acc, 0)).reshape(1, 1))
    return grad_x, small


SMALL_ORDER = ("dmod", "g_ffn1", "g_mix", "g_ffn2", "pool_scale", "q_gain", "k_gain", "sinks", "rel_bias", "loss")


def _pack_small(vals):
    flat = jnp.concatenate([vals[k].reshape(-1) for k in SMALL_ORDER])
    n = flat.shape[0]
    rows = -(-n // (8 * LANES)) * 8
    return jnp.pad(flat, (0, rows * LANES - n)).reshape(rows, LANES)


def _unpack_small(packed, like):
    flat = packed.reshape(-1)
    out, off = {}, 0
    for k in SMALL_ORDER:
        n = int(np.prod(like[k].shape))
        out[k] = flat[off:off + n].reshape(like[k].shape)
        off += n
    return out


def kernel(x, c, w_ada, b_ada, g_ffn1, w_ffn1_gu, w_ffn1_down, g_mix, w_in, pool_mix, pool_scale, w_pool_up, q_gain, k_gain, sinks, rel_bias, w_attn_up, w_o, g_ffn2, w_ffn2_gu, w_ffn2_down, loss_target, m_w_ada, m_b_ada, m_g_ffn1, m_w_ffn1_gu, m_w_ffn1_down, m_g_mix, m_w_in, m_pool_mix, m_pool_scale, m_w_pool_up, m_q_gain, m_k_gain, m_sinks, m_rel_bias, m_w_attn_up, m_w_o, m_g_ffn2, m_w_ffn2_gu, m_w_ffn2_down, v_w_ada, v_b_ada, v_g_ffn1, v_w_ffn1_gu, v_w_ffn1_down, v_g_mix, v_w_in, v_pool_mix, v_pool_scale, v_w_pool_up, v_q_gain, v_k_gain, v_sinks, v_rel_bias, v_w_attn_up, v_w_o, v_g_ffn2, v_w_ffn2_gu, v_w_ffn2_down):
    S, D = x.shape[1], x.shape[2]
    gw = pool_mix.shape[3]
    r = 2 * lax.axis_index("x") + lax.axis_index("y")

    two_d = lambda a: a.reshape(-1, a.shape[-1])
    w_sh = dict(gu1=w_ffn1_gu, down1=w_ffn1_down, w_in=w_in, pool_mix=pool_mix, pool_up=w_pool_up, attn_up=w_attn_up, o=w_o,
                gu2=w_ffn2_gu, down2=w_ffn2_down)
    m_sh = dict(gu1=m_w_ffn1_gu, down1=m_w_ffn1_down, w_in=m_w_in, pool_mix=m_pool_mix, pool_up=m_w_pool_up, attn_up=m_w_attn_up,
                o=m_w_o, gu2=m_w_ffn2_gu, down2=m_w_ffn2_down)
    v_sh = dict(gu1=v_w_ffn1_gu, down1=v_w_ffn1_down, w_in=v_w_in, pool_mix=v_pool_mix, pool_up=v_w_pool_up, attn_up=v_w_attn_up,
                o=v_w_o, gu2=v_w_ffn2_gu, down2=v_w_ffn2_down)
    view = lambda k, a: two_d(a).T if k == "w_in" else two_d(a)
    unview = lambda k, a: (a.T if k == "w_in" else a).reshape(w_sh[k].shape)
    w2 = {k: view(k, w_sh[k]) for k in BIG}
    plan = _Plan(w2, {k: view(k, m_sh[k]) for k in BIG}, {k: view(k, v_sh[k]) for k in BIG},
                 {"gu1": _cast_into_slot("cast_gu1", w2["gu1"])}, D, gw)
    token = plan.depart("gu1")
    plan.full.update({k: _cast_into_slot("cast_" + k, w2[k]) for k in BIG[1:]})

    c_all = _gather_all("gather_c", jnp.broadcast_to(c + token[0, 0], (8, D)))[:, 0, :]
    cols = w_ada.shape[2]
    b_sh = lax.dynamic_slice(b_ada, (0, r * cols), (1, cols))
    mod_cols = _mod_fwd(c_all, w_ada[0], b_sh)
    mod_all = _chip_exchange("mod_exchange", mod_cols)
    me = 4 * lax.axis_index("x") + 2 * lax.axis_index("y") + lax.axis_index("c")
    mod = lax.dynamic_slice(mod_all, (0, me, 0), (N_CHIPS, 1, cols)).reshape(9, D)

    grad_x, small = _local_step(x[0], loss_target[0], mod, g_ffn1, g_mix, g_ffn2, pool_scale, q_gain, k_gain,
                                sinks, rel_bias, plan)

    small_w = dict(dmod=b_ada, g_ffn1=g_ffn1, g_mix=g_mix, g_ffn2=g_ffn2, pool_scale=pool_scale, q_gain=q_gain, k_gain=k_gain,
                   sinks=sinks, rel_bias=rel_bias, loss=jnp.zeros((1, 1), F32))
    small_m = dict(dmod=m_b_ada, g_ffn1=m_g_ffn1, g_mix=m_g_mix, g_ffn2=m_g_ffn2, pool_scale=m_pool_scale, q_gain=m_q_gain,
                   k_gain=m_k_gain, sinks=m_sinks, rel_bias=m_rel_bias, loss=jnp.zeros((1, 1), F32))
    small_v = dict(dmod=v_b_ada, g_ffn1=v_g_ffn1, g_mix=v_g_mix, g_ffn2=v_g_ffn2, pool_scale=v_pool_scale, q_gain=v_q_gain,
                   k_gain=v_k_gain, sinks=v_sinks, rel_bias=v_rel_bias, loss=jnp.ones((1, 1), F32))
    small_all = _gather_all("gather_small", _pack_small(small))
    sg, sd, sm, sv = [_unpack_small(a, small_w) for a in
                      _small_finish(small_all, _pack_small(small_w), _pack_small(small_m), _pack_small(small_v))]
    loss = sg["loss"].reshape(())

    dmod_all = small_all.reshape(N_DEV, -1)[:, :9 * D]
    dmod_sh = lax.dynamic_slice(dmod_all, (0, r * cols), (N_DEV, cols))
    g_ada, d_ada, nm_ada, nv_ada = _wada_bwd(c_all, dmod_sh, w_ada[0], m_w_ada[0], v_w_ada[0])
    plan.alone("+join_down1")

    big = [{k: unview(k, plan.result[k][i]) for k in BIG} for i in range(4)]

    def ordered(b, ada, sm_):
        return (ada[None], sm_["dmod"], sm_["g_ffn1"], b["gu1"], b["down1"], sm_["g_mix"], b["w_in"], b["pool_mix"],
                sm_["pool_scale"], b["pool_up"], sm_["q_gain"], sm_["k_gain"], sm_["sinks"], sm_["rel_bias"], b["attn_up"],
                b["o"], sm_["g_ffn2"], b["gu2"], b["down2"])

    return (loss, grad_x[None], *ordered(big[0], g_ada, sg), *ordered(big[1], d_ada, sd), *ordered(big[2], nm_ada, sm),
            *ordered(big[3], nv_ada, sv))
```

```python
import numpy as np
import jax
import jax.numpy as jnp
from jax import lax
from jax.experimental import pallas as pl
from jax.experimental.pallas import tpu as pltpu

BF = jnp.bfloat16
F32 = jnp.float32
MESH = pl.DeviceIdType.MESH

EPS = 1e-6
NEG_INF = -1e30
HEAD_DIM = 64
N_HEADS = 16
N_KV = 2
ATT_W = N_HEADS * HEAD_DIM
KV_W = N_KV * HEAD_DIM
BLK = 128
NUM_BUCKETS = 32
POOL_MAX_W = 16
N_CHIPS = 4
N_DEV = 8
LANES = 128
ADAM_LR, ADAM_B1, ADAM_B2, ADAM_EPS, ADAM_WD, ADAM_STEP = 0.001, 0.9, 0.999, 1e-08, 0.01, 10
VMEM_LIMIT = 52 * 1024 * 1024
ANY = pl.BlockSpec(memory_space=pl.ANY)


def _pick(dim, prefs):
    for p in prefs:
        if p <= dim and dim % p == 0:
            return p
    return dim


def _row_tile(rows, cap):
    return max(t for t in range(16, min(rows, cap) + 1, 16) if rows % t == 0)


def _sds(shape, dtype):
    return jax.ShapeDtypeStruct(tuple(shape), dtype)


def _place():
    return lax.axis_index("x"), lax.axis_index("y"), lax.axis_index("c")


def _other_chips(x, y):
    return [(1 - x, y), (x, 1 - y), (1 - x, 1 - y)]


def _chip_of(chip):
    return 2 * chip[0] + chip[1]


def _half_rows(ref, lead, cc, h):
    return ref.at[lead, pl.ds(pl.multiple_of(cc * h, 16), h), :]


class _Stage:
    def __init__(self, bufs, outs, alias, n_sem, start, wait):
        self.bufs, self.outs, self.alias, self.n_sem, self.start, self.wait = bufs, outs, alias, n_sem, start, wait


def _stage_plumbing(stages, n_in0, n_out0):
    bufs, outs, aliases, spans, scratch = [], [], {}, [], []
    for st in stages:
        i0, o0 = len(bufs), len(outs)
        bufs += list(st.bufs)
        outs += list(st.outs)
        for a, b in st.alias.items():
            aliases[n_in0 + i0 + a] = n_out0 + o0 + b
        spans.append((i0, len(bufs), o0, len(outs)))
        scratch += [pltpu.SemaphoreType.DMA((st.n_sem,)), pltpu.SemaphoreType.DMA((st.n_sem,))]

    def run(which, in_refs, out_refs, sem_refs):
        for s, st in enumerate(stages):
            i0, i1, o0, o1 = spans[s]
            getattr(st, which)(in_refs[i0:i1], out_refs[o0:o1], sem_refs[2 * s], sem_refs[2 * s + 1])

    def split(flat):
        return [list(flat[o0:o1]) for (_, _, o0, o1) in spans]

    return bufs, outs, aliases, scratch, run, split


def _run_stages(name, stages):
    bufs, outs, aliases, scratch, run, split = _stage_plumbing(stages, 0, 0)
    ni, no = len(bufs), len(outs)

    def body(*refs):
        ins, os_, sems = refs[:ni], refs[ni:ni + no], refs[ni + no:]
        run("start", ins, os_, sems)
        run("wait", ins, os_, sems)

    res = pl.pallas_call(body, in_specs=[ANY] * ni, out_specs=[ANY] * no, out_shape=outs, input_output_aliases=aliases,
                         scratch_shapes=scratch, name=name)(*bufs)
    return split(res)


def _call(name, body, grid, in_specs, out_specs, out_shape, args, scratch=(), sem=None, plan=None):
    stages = plan.stages(name) if plan is not None else []
    n_in, n_out, n_scr = len(args), len(out_shape), len(scratch)
    if not stages:
        return pl.pallas_call(body, grid=grid, in_specs=list(in_specs), out_specs=list(out_specs), out_shape=list(out_shape),
                              scratch_shapes=list(scratch), name=name,
                              compiler_params=pltpu.CompilerParams(dimension_semantics=sem, vmem_limit_bytes=VMEM_LIMIT))(*args)
    bufs, s_outs, aliases, s_scratch, run, split = _stage_plumbing(stages, n_in, n_out)
    nb, nso = len(bufs), len(s_outs)

    def hosted(*refs):
        ins = refs[:n_in]
        s_ins = refs[n_in:n_in + nb]
        outs = refs[n_in + nb:n_in + nb + n_out]
        s_os = refs[n_in + nb + n_out:n_in + nb + n_out + nso]
        scr = refs[n_in + nb + n_out + nso:n_in + nb + n_out + nso + n_scr]
        sems = refs[n_in + nb + n_out + nso + n_scr:]
        first = pl.program_id(0) == 0
        last = pl.program_id(0) == grid[0] - 1
        for d in range(1, len(grid)):
            first = first & (pl.program_id(d) == 0)
            last = last & (pl.program_id(d) == grid[d] - 1)

        @pl.when(first)
        def _():
            run("start", s_ins, s_os, sems)

        body(*ins, *outs, *scr)

        @pl.when(last)
        def _():
            run("wait", s_ins, s_os, sems)

    res = pl.pallas_call(
        hosted, grid=grid, in_specs=list(in_specs) + [ANY] * nb, out_specs=list(out_specs) + [ANY] * nso,
        out_shape=list(out_shape) + s_outs, input_output_aliases=aliases, scratch_shapes=list(scratch) + s_scratch, name=name,
        compiler_params=pltpu.CompilerParams(dimension_semantics=("arbitrary",) * len(grid), vmem_limit_bytes=VMEM_LIMIT))(*args, *bufs)
    plan.done(name, split(res[n_out:]))
    return list(res[:n_out])


def _gather_ici_stage(fulls):
    n = len(fulls)

    def copy(i, j, slot, ins, outs, send, recv):
        x, y, c = _place()
        chip = _other_chips(x, y)[j]
        h = fulls[i].shape[1] // 2
        s = 3 * i + j
        return pltpu.make_async_remote_copy(_half_rows(ins[i], 2 * x + y, c, h), _half_rows(outs[i], slot(x, y, chip), c, h),
                                            send.at[s], recv.at[s], device_id=(*chip, c), device_id_type=MESH)

    mine = lambda x, y, chip: 2 * x + y
    theirs = lambda x, y, chip: _chip_of(chip)

    def start(ins, outs, send, recv):
        for i in range(n):
            for j in range(3):
                copy(i, j, mine, ins, outs, send, recv).start()

    def wait(ins, outs, send, recv):
        for i in range(n):
            for j in range(3):
                copy(i, j, theirs, ins, outs, send, recv).wait_recv()
        for i in range(n):
            for j in range(3):
                copy(i, j, mine, ins, outs, send, recv).wait_send()

    return _Stage(fulls, [_sds(f.shape, f.dtype) for f in fulls], {i: i for i in range(n)}, 3 * n, start, wait)


def _gather_d2d_stage(fulls):
    n = len(fulls)

    def copy(i, j, cc, ins, outs, send, recv):
        x, y, c = _place()
        rj = _chip_of(_other_chips(x, y)[j])
        h = fulls[i].shape[1] // 2
        half = cc(c)
        s = 3 * i + j
        return pltpu.make_async_remote_copy(_half_rows(ins[i], rj, half, h), _half_rows(outs[i], rj, half, h),
                                            send.at[s], recv.at[s], device_id=(x, y, 1 - c), device_id_type=MESH)

    mine = lambda c: c
    theirs = lambda c: 1 - c

    def start(ins, outs, send, recv):
        for i in range(n):
            for j in range(3):
                copy(i, j, mine, ins, outs, send, recv).start()

    def wait(ins, outs, send, recv):
        for i in range(n):
            for j in range(3):
                copy(i, j, theirs, ins, outs, send, recv).wait_recv()
        for i in range(n):
            for j in range(3):
                copy(i, j, mine, ins, outs, send, recv).wait_send()

    return _Stage(fulls, [_sds(f.shape, f.dtype) for f in fulls], {i: i for i in range(n)}, 3 * n, start, wait)


def _split_stage(parts):
    n = len(parts)

    def copy(i, ins, outs, send, recv):
        x, y, c = _place()
        h = parts[i].shape[1] // 2
        return pltpu.make_async_remote_copy(_half_rows(ins[i], slice(None), 1 - c, h), outs[i], send.at[i], recv.at[i],
                                            device_id=(x, y, 1 - c), device_id_type=MESH)

    def start(ins, outs, send, recv):
        for i in range(n):
            copy(i, ins, outs, send, recv).start()

    def wait(ins, outs, send, recv):
        for i in range(n):
            copy(i, ins, outs, send, recv).wait_recv()
        for i in range(n):
            copy(i, ins, outs, send, recv).wait_send()

    return _Stage(parts, [_sds((N_CHIPS, p.shape[1] // 2, p.shape[2]), p.dtype) for p in parts], {}, n, start, wait)


def _owners_stage(sums):
    n = len(sums)

    def copy(i, j, mine, ins, outs, send, recv):
        x, y, c = _place()
        chip = _other_chips(x, y)[j]
        slot = (2 * x + y) if mine else _chip_of(chip)
        return pltpu.make_async_remote_copy(ins[i].at[_chip_of(chip)], outs[i].at[slot], send.at[3 * i + j], recv.at[3 * i + j],
                                            device_id=(*chip, c), device_id_type=MESH)

    def start(ins, outs, send, recv):
        for i in range(n):
            for j in range(3):
                copy(i, j, True, ins, outs, send, recv).start()

    def wait(ins, outs, send, recv):
        for i in range(n):
            for j in range(3):
                copy(i, j, False, ins, outs, send, recv).wait_recv()
        for i in range(n):
            for j in range(3):
                copy(i, j, True, ins, outs, send, recv).wait_send()

    return _Stage(sums, [_sds(s.shape, s.dtype) for s in sums], {}, 3 * n, start, wait)


def _join_stage(gs):
    n = len(gs)

    def copy(i, mine, ins, outs, send, recv):
        x, y, c = _place()
        slot = c if mine else 1 - c
        return pltpu.make_async_remote_copy(ins[i].at[slot], outs[i].at[slot], send.at[i], recv.at[i],
                                            device_id=(x, y, 1 - c), device_id_type=MESH)

    def start(ins, outs, send, recv):
        for i in range(n):
            copy(i, True, ins, outs, send, recv).start()

    def wait(ins, outs, send, recv):
        for i in range(n):
            copy(i, False, ins, outs, send, recv).wait_recv()
        for i in range(n):
            copy(i, True, ins, outs, send, recv).wait_send()

    return _Stage(gs, [_sds(g.shape, g.dtype) for g in gs], {i: i for i in range(n)}, n, start, wait)


def _gather_start(name, full, after):
    h = full.shape[1] // 2

    def body(f_ref, after_ref, send, recv, f_out, token):
        x, y, c = _place()
        for chip in _other_chips(x, y):
            pltpu.make_async_remote_copy(_half_rows(f_ref, 2 * x + y, c, h), _half_rows(f_out, 2 * x + y, c, h), send, recv,
                                         device_id=(*chip, c), device_id_type=MESH).start()
        token[...] = jnp.zeros_like(token)

    hbm = pl.BlockSpec(memory_space=pltpu.HBM)
    sem = pl.BlockSpec(memory_space=pltpu.SEMAPHORE)
    return pl.pallas_call(
        body, name=name,
        out_shape=(pltpu.SemaphoreType.DMA(()), pltpu.SemaphoreType.DMA(()), pltpu.HBM(full.shape, full.dtype), _sds((8, LANES), F32)),
        in_specs=(hbm, ANY), out_specs=(sem, sem, hbm, pl.BlockSpec(memory_space=pltpu.VMEM)), input_output_aliases={0: 2},
        compiler_params=pltpu.CompilerParams(has_side_effects=pltpu.SideEffectType.DATAFLOW_SIDE_EFFECTING),
    )(pltpu.with_memory_space_constraint(full, pltpu.HBM), after)


def _gather_finish(name, send, recv, full, after):
    h = full.shape[1] // 2

    def body(f_ref, send, recv, after_ref, f_out):
        x, y, c = _place()
        three = f_ref.at[pl.ds(0, 3), pl.ds(0, h), :]
        all_three = pltpu.make_async_remote_copy(three, three, send, recv, device_id=(x, y, 1 - c), device_id_type=MESH)
        all_three.wait_send()
        all_three.wait_recv()

    hbm = pl.BlockSpec(memory_space=pltpu.HBM)
    sem = pl.BlockSpec(memory_space=pltpu.SEMAPHORE)
    return pl.pallas_call(
        body, name=name, out_shape=pltpu.HBM(full.shape, full.dtype),
        in_specs=(hbm, sem, sem, pl.BlockSpec(memory_space=pl.ANY)), out_specs=hbm, input_output_aliases={0: 0},
        compiler_params=pltpu.CompilerParams(has_side_effects=pltpu.SideEffectType.DATAFLOW_SIDE_EFFECTING),
    )(full, send, recv, after)


def _chip_exchange(name, arr):
    def body(src, dst, send, recv, loc):
        x, y, c = _place()
        r = 2 * x + y
        chips = _other_chips(x, y)

        def cp(j, slot):
            return pltpu.make_async_remote_copy(src, dst.at[slot], send.at[j], recv.at[j], device_id=(*chips[j], c), device_id_type=MESH)

        mine = pltpu.make_async_copy(src, dst.at[r], loc)
        mine.start()
        for j in range(3):
            cp(j, r).start()
        for j in range(3):
            cp(j, _chip_of(chips[j])).wait_recv()
        for j in range(3):
            cp(j, r).wait_send()
        mine.wait()

    return pl.pallas_call(body, in_specs=[ANY], out_specs=ANY, out_shape=_sds((N_CHIPS, *arr.shape), arr.dtype),
                          scratch_shapes=[pltpu.SemaphoreType.DMA((3,)), pltpu.SemaphoreType.DMA((3,)), pltpu.SemaphoreType.DMA],
                          name=name)(arr)


def _gather_all(name, arr):
    def body(src, dst, send, recv, loc):
        x, y, c = _place()

        def cp(k, slot_of_me):
            px, py, pc = x ^ ((k >> 2) & 1), y ^ ((k >> 1) & 1), c ^ (k & 1)
            slot = (4 * x + 2 * y + c) if slot_of_me else (4 * px + 2 * py + pc)
            return pltpu.make_async_remote_copy(src, dst.at[slot], send.at[k - 1], recv.at[k - 1],
                                                device_id=(px, py, pc), device_id_type=MESH)

        mine = pltpu.make_async_copy(src, dst.at[4 * x + 2 * y + c], loc)
        mine.start()
        for k in range(1, N_DEV):
            cp(k, True).start()
        for k in range(1, N_DEV):
            cp(k, False).wait_recv()
        for k in range(1, N_DEV):
            cp(k, True).wait_send()
        mine.wait()

    return pl.pallas_call(body, in_specs=[ANY], out_specs=ANY, out_shape=_sds((N_DEV, *arr.shape), arr.dtype),
                          scratch_shapes=[pltpu.SemaphoreType.DMA((N_DEV - 1,)), pltpu.SemaphoreType.DMA((N_DEV - 1,)), pltpu.SemaphoreType.DMA],
                          name=name)(arr)


NN = (((1,), (0,)), ((), ()))
NT = (((1,), (1,)), ((), ()))
TN = (((0,), (0,)), ((), ()))


ALL = slice(None)


def _mm(name, grid, dims, a, a_spec, b, b_spec, extras, extra_specs, out_shapes, out_specs, acc_shape, epilogue, plan=None):
    n_k = grid[2]
    n_e = len(extras)
    n_o = len(out_shapes)

    def body(*refs):
        a_ref, b_ref = refs[0], refs[1]
        e_refs = refs[2:2 + n_e]
        o_refs = refs[2 + n_e:2 + n_e + n_o]
        p = lax.dot_general(a_ref[...].astype(BF), b_ref[...].astype(BF), dims, preferred_element_type=F32)
        if n_k == 1:
            epilogue(p, e_refs, o_refs, ALL)
        else:
            acc = refs[-1]
            k = pl.program_id(2)

            @pl.when(k == 0)
            def _():
                acc[...] = p

            @pl.when(k > 0)
            def _():
                acc[...] += p

            @pl.when(k == n_k - 1)
            def _():
                epilogue(acc[...], e_refs, o_refs, ALL)

    scratch = [] if n_k == 1 else [pltpu.VMEM(acc_shape, F32)]
    return _call(name, body, grid, [a_spec, b_spec, *extra_specs], out_specs, out_shapes, [a, b, *extras], scratch,
                 ("parallel", "parallel", "arbitrary"), plan)


def _store(p, e, o, rs):
    o[0][rs, :] = p.astype(o[0].dtype)


def _rms_mod_fwd(name, x, gain, shift, scale, plan=None):
    S, D = x.shape
    ts = _pick(S, (512,))

    def body(x_ref, g_ref, sh_ref, sc_ref, h_ref):
        xv = x_ref[...]
        r = lax.rsqrt(jnp.mean(xv * xv, axis=-1, keepdims=True) + EPS)
        n = xv * r * g_ref[...]
        h_ref[...] = (n * (1.0 + sc_ref[...]) + sh_ref[...]).astype(BF)

    row = pl.BlockSpec((ts, D), lambda i: (i, 0))
    vec = pl.BlockSpec((1, D), lambda i: (0, 0))
    return _call(name, body, (S // ts,), [row, vec, vec, vec], [row], [_sds((S, D), BF)], [x, gain, shift, scale],
                 sem=("parallel",), plan=plan)[0]


def _acc_rows(acc_ref, first, part):
    @pl.when(first)
    def _():
        acc_ref[...] = part

    @pl.when(jnp.logical_not(first))
    def _():
        acc_ref[...] += part


def _rms_mod_bwd(name, dh, x, dres, gain, scale, f=None, coef=None, plan=None):
    S, D = x.shape
    ts = _pick(S, (256,))
    gated = f is not None

    def body(dh_ref, x_ref, dr_ref, g_ref, sc_ref, *rest):
        xv = x_ref[...]
        dhv = dh_ref[...]
        g = g_ref[...]
        r = lax.rsqrt(jnp.mean(xv * xv, axis=-1, keepdims=True) + EPS)
        xhat = xv * r
        dn = dhv * (1.0 + sc_ref[...])
        dxhat = dn * g
        dx = dr_ref[...] + r * (dxhat - xhat * jnp.mean(dxhat * xhat, axis=-1, keepdims=True))
        rows = [jnp.sum(dhv, axis=0, keepdims=True), jnp.sum(dhv * (xhat * g), axis=0, keepdims=True),
                jnp.sum(dn * xhat, axis=0, keepdims=True)]
        if gated:
            f_ref, c_ref, dx_ref, df_ref, acc_ref = rest
            df_ref[...] = (dx * c_ref[...]).astype(BF)
            rows.append(jnp.sum(dx * f_ref[...].astype(F32), axis=0, keepdims=True))
        else:
            dx_ref, acc_ref = rest
        dx_ref[...] = dx
        _acc_rows(acc_ref, pl.program_id(0) == 0, jnp.concatenate(rows + [jnp.zeros((8 - len(rows), D), F32)], axis=0))

    row = pl.BlockSpec((ts, D), lambda i: (i, 0))
    vec = pl.BlockSpec((1, D), lambda i: (0, 0))
    acc = pl.BlockSpec((8, D), lambda i: (0, 0))
    if gated:
        return _call(name, body, (S // ts,), [row, row, row, vec, vec, row, vec], [row, row, acc],
                     [_sds((S, D), F32), _sds((S, D), BF), _sds((8, D), F32)], [dh, x, dres, gain, scale, f, coef],
                     sem=("arbitrary",), plan=plan)
    return _call(name, body, (S // ts,), [row, row, row, vec, vec], [row, acc],
                 [_sds((S, D), F32), _sds((8, D), F32)], [dh, x, dres, gain, scale], sem=("arbitrary",), plan=plan)


def _loss_bwd(x3, target, f, coef):
    S, D = x3.shape
    ts = _pick(S, (512,))

    def body(x_ref, t_ref, f_ref, c_ref, dx_ref, df_ref, acc_ref):
        e = x_ref[...] - t_ref[...]
        dx = e * (1.0 / D)
        dx_ref[...] = dx
        df_ref[...] = (dx * c_ref[...]).astype(BF)
        part = jnp.concatenate([jnp.sum(e * e, axis=0, keepdims=True), jnp.sum(dx * f_ref[...].astype(F32), axis=0, keepdims=True),
                                jnp.zeros((6, D), F32)], axis=0)
        _acc_rows(acc_ref, pl.program_id(0) == 0, part)

    row = pl.BlockSpec((ts, D), lambda i: (i, 0))
    return _call("loss_bwd", body, (S // ts,), [row, row, row, pl.BlockSpec((1, D), lambda i: (0, 0))],
                 [row, row, pl.BlockSpec((8, D), lambda i: (0, 0))],
                 [_sds((S, D), F32), _sds((S, D), BF), _sds((8, D), F32)], [x3, target, f, coef], sem=("arbitrary",))


def _silu_parts(g):
    s = jax.nn.sigmoid(g)
    return s, g * s


def _ffn_up(name, h, wgu4, plan=None):
    S, D = h.shape
    SH = wgu4.shape[2]
    F = 2 * SH
    tm = _pick(S, (512,))
    tn = _pick(SH, (1408, 256))
    nts = SH // tn

    def body(h_ref, wg_ref, wu_ref, gu_ref, act_ref):
        hv = h_ref[...]
        g = jnp.dot(hv, wg_ref[...], preferred_element_type=F32)
        u = jnp.dot(hv, wu_ref[...], preferred_element_type=F32)
        gu_ref[0] = g.astype(BF)
        gu_ref[1] = u.astype(BF)
        act_ref[...] = (_silu_parts(g)[1] * u).astype(BF)

    return _call(name, body, (S // tm, F // tn),
                 [pl.BlockSpec((tm, D), lambda i, j: (i, 0)),
                  pl.BlockSpec((None, D, tn), lambda i, j: (j // nts, 0, j % nts)),
                  pl.BlockSpec((None, D, tn), lambda i, j: (2 + j // nts, 0, j % nts))],
                 [pl.BlockSpec((2, tm, tn), lambda i, j: (0, i, j)), pl.BlockSpec((tm, tn), lambda i, j: (i, j))],
                 [_sds((2, S, F), BF), _sds((S, F), BF)], [h, wgu4, wgu4], sem=("parallel", "parallel"), plan=plan)


def _mm_residual(name, a, w, x_in, coef, plan=None):
    S, K = a.shape
    D = w.shape[1]
    tm = _pick(S, (1024,))
    tn = _pick(D, (512,))
    tk = K

    def epi(p, e, o, rs):
        o[0][rs, :] = e[0][rs, :] + e[1][...] * p
        o[1][rs, :] = p.astype(BF)

    tile = pl.BlockSpec((tm, tn), lambda i, j, k: (i, j))
    return _mm(name, (S // tm, D // tn, K // tk), NN,
               a, pl.BlockSpec((tm, tk), lambda i, j, k: (i, k)),
               w, pl.BlockSpec((tk, tn), lambda i, j, k: (k, j)),
               [x_in, coef], [tile, pl.BlockSpec((1, tn), lambda i, j, k: (0, j))],
               [_sds((S, D), F32), _sds((S, D), BF)], [tile, tile], (tm, tn), epi, plan)


def _ffn_dact(name, df, wd, gu, plan=None):
    S, D = df.shape
    F = wd.shape[0]
    tm = _pick(S, (512,))
    tn = _pick(F, (1408, 256))

    def epi(p, e, o, rs):
        g = e[0][0, rs, :].astype(F32)
        u = e[0][1, rs, :].astype(F32)
        s, sg = _silu_parts(g)
        o[0][0, rs, :] = (p * u * (s * (1.0 + g * (1.0 - s)))).astype(BF)
        o[0][1, rs, :] = (p * sg).astype(BF)

    pair = pl.BlockSpec((2, tm, tn), lambda i, j, k: (0, i, j))
    return _mm(name, (S // tm, F // tn, 1), NT,
               df, pl.BlockSpec((tm, D), lambda i, j, k: (i, 0)),
               wd, pl.BlockSpec((tn, D), lambda i, j, k: (j, 0)),
               [gu], [pair], [_sds((2, S, F), BF)], [pair], None, epi, plan)[0]


def _ffn_dh(name, dgu, wgu4, plan=None):
    _, S, F = dgu.shape
    _, D, SH = wgu4.shape
    tm = _pick(S, (1024,))
    tn = _pick(D, (512,))

    def body(a_ref, b_ref, o_ref, acc):
        k = pl.program_id(2)
        p = lax.dot_general(a_ref[:, :SH], b_ref[0], NT, preferred_element_type=F32)
        p = p + lax.dot_general(a_ref[:, SH:], b_ref[1], NT, preferred_element_type=F32)

        @pl.when(k == 0)
        def _():
            acc[...] = p

        @pl.when(k == 1)
        def _():
            o_ref[...] = acc[...] + p

    return _call(name, body, (S // tm, D // tn, 2),
                 [pl.BlockSpec((None, tm, F), lambda i, j, k: (k, i, 0)), pl.BlockSpec((2, tn, SH), lambda i, j, k: (k, j, 0))],
                 [pl.BlockSpec((tm, tn), lambda i, j, k: (i, j))], [_sds((S, D), F32)], [dgu, wgu4],
                 [pltpu.VMEM((tm, tn), F32)], ("parallel", "parallel", "arbitrary"), plan)[0]


def _ffn_dwgu(name, h, dgu, plan=None):
    _, S, F = dgu.shape
    D = h.shape[1]
    SH = F // 2
    tk1 = _pick(D, (512,))
    tn = _pick(SH, (1408, 256))
    ts = _pick(S, (4096, 1024))
    npj = F // tn
    nsj = SH // tn
    return _mm(name, (D // tk1, 2 * npj, S // ts), TN,
               h, pl.BlockSpec((ts, tk1), lambda i, j, k: (k, i)),
               dgu, pl.BlockSpec((None, ts, tn), lambda i, j, k: (j // npj, k, j % npj)),
               [], [], [_sds((4, D, SH), BF)],
               [pl.BlockSpec((None, tk1, tn), lambda i, j, k: (j // nsj, i, j % nsj))], (tk1, tn), _store, plan)[0]


def _mm_tn(name, a, b, tk1_prefs, tn_prefs, plan=None):
    S, K1 = a.shape
    N = b.shape[1]
    tk1 = _pick(K1, tk1_prefs)
    tn = _pick(N, tn_prefs)
    ts = _pick(S, (4096, 1024))
    return _mm(name, (K1 // tk1, N // tn, S // ts), TN,
               a, pl.BlockSpec((ts, tk1), lambda i, j, k: (k, i)),
               b, pl.BlockSpec((ts, tn), lambda i, j, k: (k, j)),
               [], [], [_sds((K1, N), BF)], [pl.BlockSpec((tk1, tn), lambda i, j, k: (i, j))], (tk1, tn), _store, plan)[0]


def _pool_window(ext, w, back):
    n = ext.shape[0]
    s = ext
    for step in (1, 2, 4, 8):
        sh = pltpu.roll(s, (n - step) if back else step, axis=0)
        s = jnp.where(w > step, s + sh, s)
    return s


def _pool_fwd(z, PW):
    S = z.shape[0]
    tc = _pick(S, (1024,))
    bpg = (PW // 4) // LANES
    H = POOL_MAX_W

    def body(prev_ref, u_ref, o_ref):
        i = pl.program_id(0)
        j = pl.program_id(1)
        w = lax.shift_left(jnp.int32(2), j // bpg)
        u = u_ref[...]
        prev = jnp.where(i > 0, prev_ref[...], 0.0)
        s = _pool_window(jnp.concatenate([prev, u], axis=0), w, False)[H:]
        t = i * tc + lax.broadcasted_iota(jnp.int32, (tc, LANES), 0)
        cnt = jnp.minimum(t + 1, w).astype(F32)
        o_ref[...] = (s / cnt - u).astype(BF)

    r = tc // H
    return _call("pool_fwd", body, (S // tc, PW // LANES),
                 [pl.BlockSpec((H, LANES), lambda i, j: (jnp.maximum(i * r - 1, 0), j)),
                  pl.BlockSpec((tc, LANES), lambda i, j: (i, j))],
                 [pl.BlockSpec((tc, LANES), lambda i, j: (i, j))], [_sds((S, PW), BF)], [z, z], sem=("parallel", "parallel"))[0]


def _pool_bwd(dpooled):
    S, PW = dpooled.shape
    tc = _pick(S, (1024,))
    bpg = (PW // 4) // LANES
    H = POOL_MAX_W
    last = S // tc - 1

    def body(dp_ref, nxt_ref, o_ref):
        i = pl.program_id(0)
        j = pl.program_id(1)
        w = lax.shift_left(jnp.int32(2), j // bpg)
        dp = dp_ref[...]
        nxt = jnp.where(i < last, nxt_ref[...], 0.0)
        ext = jnp.concatenate([dp, nxt], axis=0)
        t = i * tc + lax.broadcasted_iota(jnp.int32, (tc + H, LANES), 0)
        cnt = jnp.minimum(t + 1, w).astype(F32)
        s = _pool_window(ext / cnt, w, True)[:tc]
        o_ref[...] = (s - dp).astype(BF)

    r = tc // H
    nh = S // H - 1
    return _call("pool_bwd", body, (S // tc, PW // LANES),
                 [pl.BlockSpec((tc, LANES), lambda i, j: (i, j)),
                  pl.BlockSpec((H, LANES), lambda i, j: (jnp.minimum((i + 1) * r, nh), j))],
                 [pl.BlockSpec((tc, LANES), lambda i, j: (i, j))], [_sds((S, PW), BF)], [dpooled, dpooled],
                 sem=("parallel", "parallel"))[0]


def _pool_mix(pooled, pm, scale):
    S, PW = pooled.shape
    gw = PW // 4
    ts = _pick(S, (1024,))

    def epi(p, e, o, rs):
        o[0][rs, :] = (p * e[0][...]).astype(BF)

    tile = pl.BlockSpec((ts, gw), lambda i, j, k: (i, j))
    return _mm("pool_mix", (S // ts, 4, 1), NN, pooled, tile,
               pm, pl.BlockSpec((None, gw, gw), lambda i, j, k: (j, 0, 0)),
               [scale], [pl.BlockSpec((1, gw), lambda i, j, k: (0, j))], [_sds((S, PW), BF)], [tile], None, epi)[0]


def _pool_mix_bwd(pooled, pm, scale, dmixed):
    S, PW = pooled.shape
    gw = PW // 4
    ts = _pick(S, (1024,))

    def body(p_ref, pm_ref, sc_ref, dm_ref, dp_ref, dpm_ref, dsc_ref):
        i = pl.program_id(1)
        p = p_ref[...]
        w = pm_ref[...]
        dm = dm_ref[...]
        pre = jnp.dot(p, w, preferred_element_type=F32)
        dmp = (dm * sc_ref[...]).astype(BF)
        dp_ref[...] = lax.dot_general(dmp, w, NT, preferred_element_type=F32)
        dw = lax.dot_general(p, dmp, TN, preferred_element_type=F32)
        ds = jnp.concatenate([jnp.sum(dm * pre, axis=0, keepdims=True), jnp.zeros((7, gw), F32)], axis=0)
        _acc_rows(dpm_ref, i == 0, dw)
        _acc_rows(dsc_ref, i == 0, ds)

    tile = pl.BlockSpec((ts, gw), lambda g, i: (i, g))
    return _call("pool_mix_bwd", body, (4, S // ts),
                 [tile, pl.BlockSpec((None, gw, gw), lambda g, i: (g, 0, 0)), pl.BlockSpec((1, gw), lambda g, i: (0, g)), tile],
                 [tile, pl.BlockSpec((None, gw, gw), lambda g, i: (g, 0, 0)), pl.BlockSpec((8, gw), lambda g, i: (0, g))],
                 [_sds((S, PW), F32), _sds((4, gw, gw), F32), _sds((8, PW), F32)], [pooled, pm, scale, dmixed],
                 sem=("parallel", "arbitrary"))


def _bucket_onehot():
    ql = np.arange(BLK)[:, None]
    j = np.arange(2 * BLK)[None, :]
    d = BLK + ql - j
    n = np.clip(d, 0, None)
    nf = np.maximum(n, 1).astype(np.float32)
    max_exact = NUM_BUCKETS // 2
    large = max_exact + (np.log(nf / max_exact) / np.log(BLK / max_exact) * (NUM_BUCKETS - max_exact)).astype(np.int32)
    large = np.minimum(large, NUM_BUCKETS - 1)
    bucket = np.where(n < max_exact, n, large).astype(np.int32)
    valid = (d >= 0) & (d < BLK)
    oh = (bucket[None] == np.arange(NUM_BUCKETS)[:, None, None]) & valid[None]
    return oh.reshape(NUM_BUCKETS, BLK * 2 * BLK)


def _three_bf16(v):
    hi = v.astype(BF)
    r1 = v - hi.astype(F32)
    mid = r1.astype(BF)
    lo = (r1 - mid.astype(F32)).astype(BF)
    return hi, mid, lo


def _bias_table(rel_bias):
    oh = jnp.asarray(_bucket_onehot(), BF)
    tn = 4096

    def body(rb_ref, oh_ref, o_ref):
        o = oh_ref[...]
        hi, mid, lo = _three_bf16(rb_ref[...])
        acc = lax.dot_general(hi, o, TN, preferred_element_type=F32)
        acc = acc + lax.dot_general(mid, o, TN, preferred_element_type=F32)
        acc = acc + lax.dot_general(lo, o, TN, preferred_element_type=F32)
        on_band = jnp.sum(o.astype(F32), axis=0, keepdims=True) > 0.5
        o_ref[...] = jnp.where(on_band, acc, NEG_INF)

    n = oh.shape[1]
    return _call("bias_table", body, (n // tn,),
                 [pl.BlockSpec((NUM_BUCKETS, N_HEADS), lambda i: (0, 0)), pl.BlockSpec((NUM_BUCKETS, tn), lambda i: (0, i))],
                 [pl.BlockSpec((N_HEADS, tn), lambda i: (0, i))], [_sds((N_HEADS, n), F32)], [rel_bias, oh], sem=("parallel",))[0]


def _rel_bias_grad(dl):
    oh = jnp.asarray(_bucket_onehot(), BF)
    n = oh.shape[1]
    tk = 4096

    def body(dl_ref, oh_ref, o_ref):
        o = oh_ref[...]
        hi, mid, lo = _three_bf16(dl_ref[...])
        acc = lax.dot_general(o, hi, NT, preferred_element_type=F32)
        acc = acc + lax.dot_general(o, mid, NT, preferred_element_type=F32)
        acc = acc + lax.dot_general(o, lo, NT, preferred_element_type=F32)
        _acc_rows(o_ref, pl.program_id(0) == 0, acc)

    return _call("rel_bias_grad", body, (n // tk,),
                 [pl.BlockSpec((N_HEADS, tk), lambda i: (0, i)), pl.BlockSpec((NUM_BUCKETS, tk), lambda i: (0, i))],
                 [pl.BlockSpec((NUM_BUCKETS, N_HEADS), lambda i: (0, 0))], [_sds((NUM_BUCKETS, N_HEADS), F32)], [dl, oh],
                 sem=("arbitrary",))[0]


def _lo_half(shape):
    return lax.broadcasted_iota(jnp.int32, shape, 1) < HEAD_DIM


def _half_sum(x, lo):
    s_lo = jnp.sum(jnp.where(lo, x, 0.0), axis=-1, keepdims=True)
    s_hi = jnp.sum(jnp.where(lo, 0.0, x), axis=-1, keepdims=True)
    return jnp.where(lo, s_lo, s_hi)


def _norm2(x, lo):
    r = lax.rsqrt(_half_sum(x * x, lo) * (1.0 / HEAD_DIM) + EPS)
    return x * r, r


def _norm2_bwd(dy, xhat, r, gain, lo):
    dxhat = dy * gain
    dx = r * (dxhat - xhat * (_half_sum(dxhat * xhat, lo) * (1.0 / HEAD_DIM)))
    return dx, dy * xhat


def _swap(x):
    return pltpu.roll(x, HEAD_DIM, axis=1)


def _pair_rows(x, kk):
    return jnp.concatenate([x, _swap(x)] if kk == 0 else [_swap(x), x], axis=0)


def _attn_probs(n, kk, jp0, npairs, zq_ref, K, qg, bias_ref, sink_ref):
    lo_q = _lo_half((BLK, LANES))
    rows, qhats, qrs = [], [], []
    for jp in range(jp0, jp0 + npairs):
        qhat, qr = _norm2(zq_ref[:, jp * LANES:(jp + 1) * LANES], lo_q)
        rows.append(_pair_rows(qhat * qg * (HEAD_DIM ** -0.5), kk))
        qhats.append(qhat)
        qrs.append(qr)
    Q = jnp.concatenate(rows, axis=0).astype(BF)
    cols = slice(jp0 * 2 * BLK, (jp0 + npairs) * 2 * BLK)
    l = lax.dot_general(K, Q, NT, preferred_element_type=F32) + bias_ref[kk, :, cols]
    l = jnp.concatenate([jnp.where(n == 0, NEG_INF, l[:BLK]), l[BLK:]], axis=0)
    sink = sink_ref[kk, :, cols]
    m = jnp.maximum(jnp.max(l, axis=0, keepdims=True), sink)
    e = jnp.exp(l - m)
    es = jnp.exp(sink - m)
    inv = 1.0 / (jnp.sum(e, axis=0, keepdims=True) + es)
    return Q, e * inv, es * inv, qhats, qrs


def _attn_specs(o_q, o_k):
    nq = o_q // 512
    nk = o_k // LANES
    prev = lambda n: (jnp.maximum(n - 1, 0), nk)
    prev_v = lambda n: (jnp.maximum(n - 1, 0), nk + 1)
    return [pl.BlockSpec((BLK, 512), lambda n: (n, nq)), pl.BlockSpec((BLK, 512), lambda n: (n, nq + 1)),
            pl.BlockSpec((BLK, LANES), prev), pl.BlockSpec((BLK, LANES), lambda n: (n, nk)),
            pl.BlockSpec((BLK, LANES), prev_v), pl.BlockSpec((BLK, LANES), lambda n: (n, nk + 1)),
            pl.BlockSpec((1, LANES), lambda n: (0, 0)), pl.BlockSpec((1, LANES), lambda n: (0, 0)),
            pl.BlockSpec((N_KV, 1, 8 * BLK), lambda n: (0, 0, 0)),
            pl.BlockSpec((N_KV, 2 * BLK, 8 * BLK), lambda n: (0, 0, 0))]


def _attn_fwd(z, o_q, o_k, qg2, kg2, sink_rows, bias, plan=None):
    S = z.shape[0]

    def body(zq0, zq1, zkp, zkc, zvp, zvc, qg_ref, kg_ref, sink_ref, bias_ref, o_ref):
        n = pl.program_id(0)
        lo_k = _lo_half((2 * BLK, LANES))
        lo_q = _lo_half((BLK, LANES))
        khat, _ = _norm2(jnp.concatenate([zkp[...], zkc[...]], axis=0), lo_k)
        kn = khat * kg_ref[...]
        vb = jnp.concatenate([zvp[...], zvc[...]], axis=0).astype(BF)
        for kk, zq in enumerate((zq0, zq1)):
            K = jnp.where(lo_k if kk == 0 else jnp.logical_not(lo_k), kn, 0.0).astype(BF)
            for jp in range(4):
                _, p, _, _, _ = _attn_probs(n, kk, jp, 1, zq, K, qg_ref[...], bias_ref, sink_ref)
                r = lax.dot_general(p.astype(BF), vb, TN, preferred_element_type=F32)
                ev, od = r[:BLK], r[BLK:]
                pair = jnp.where(lo_q, ev, _swap(od)) if kk == 0 else jnp.where(lo_q, _swap(ev), od)
                c0 = (4 * kk + jp) * LANES
                o_ref[:, c0:c0 + LANES] = pair.astype(BF)

    return _call("attn_fwd", body, (S // BLK,), _attn_specs(o_q, o_k), [pl.BlockSpec((BLK, ATT_W), lambda n: (n, 0))],
                 [_sds((S, ATT_W), BF)], [z, z, z, z, z, z, qg2, kg2, sink_rows, bias], sem=("parallel",), plan=plan)[0]


def _attn_bwd(z, o_q, o_k, qg2, kg2, sink_rows, bias, dout, plan=None):
    S = z.shape[0]

    def body(zq0, zq1, zkp, zkc, zvp, zvc, qg_ref, kg_ref, sink_ref, bias_ref, do_ref,
             dq_ref, dkp_ref, dkc_ref, dvp_ref, dvc_ref, dl_ref, dsink_ref, dgain_ref):
        n = pl.program_id(0)
        lo_k = _lo_half((2 * BLK, LANES))
        lo_q = _lo_half((BLK, LANES))
        qg = qg_ref[...]
        kg = kg_ref[...]
        khat, kr = _norm2(jnp.concatenate([zkp[...], zkc[...]], axis=0), lo_k)
        kn = khat * kg
        vf = jnp.concatenate([zvp[...], zvc[...]], axis=0)

        @pl.when(n == 0)
        def _():
            dl_ref[...] = jnp.zeros_like(dl_ref)
            dsink_ref[...] = jnp.zeros_like(dsink_ref)
            dgain_ref[...] = jnp.zeros_like(dgain_ref)

        dkn = jnp.zeros((2 * BLK, LANES), F32)
        dvb = jnp.zeros((2 * BLK, LANES), F32)
        dqg = jnp.zeros((1, LANES), F32)
        for kk, zq in enumerate((zq0, zq1)):
            half_k = lo_k if kk == 0 else jnp.logical_not(lo_k)
            K = jnp.where(half_k, kn, 0.0).astype(BF)
            V = jnp.where(half_k, vf, 0.0).astype(BF)
            Q, p, ps, qhats, qrs = _attn_probs(n, kk, 0, 4, zq, K, qg, bias_ref, sink_ref)
            dO = jnp.concatenate([_pair_rows(do_ref[:, (4 * kk + jp) * LANES:(4 * kk + jp + 1) * LANES], kk) for jp in range(4)],
                                 axis=0).astype(BF)
            dP = lax.dot_general(V, dO, NT, preferred_element_type=F32)
            delta = jnp.sum(p * dP, axis=0, keepdims=True)
            dS = p * (dP - delta)
            dsink_ref[kk] += -ps * delta
            dl_ref[kk] += dS
            dSb = dS.astype(BF)
            dvb = dvb + jnp.where(half_k, jnp.dot(p.astype(BF), dO, preferred_element_type=F32), 0.0)
            dkn = dkn + jnp.where(half_k, jnp.dot(dSb, Q, preferred_element_type=F32), 0.0)
            dQ = lax.dot_general(dSb, K, TN, preferred_element_type=F32) * (HEAD_DIM ** -0.5)
            for jp in range(4):
                ev = dQ[(2 * jp) * BLK:(2 * jp + 1) * BLK]
                od = dQ[(2 * jp + 1) * BLK:(2 * jp + 2) * BLK]
                dy = (ev + _swap(od)) if kk == 0 else (_swap(ev) + od)
                dx, gq = _norm2_bwd(dy, qhats[jp], qrs[jp], qg, lo_q)
                dqg = dqg + jnp.sum(gq, axis=0, keepdims=True)
                c0 = (4 * kk + jp) * LANES
                dq_ref[:, c0:c0 + LANES] = dx.astype(BF)
        dk, gk = _norm2_bwd(dkn, khat, kr, kg, lo_k)
        dkp_ref[...] = dk[:BLK]
        dkc_ref[...] = dk[BLK:]
        dvp_ref[...] = dvb[:BLK]
        dvc_ref[...] = dvb[BLK:]
        dgain_ref[...] += jnp.concatenate([dqg, jnp.sum(gk, axis=0, keepdims=True), jnp.zeros((6, LANES), F32)], axis=0)

    blk = pl.BlockSpec((BLK, LANES), lambda n: (n, 0))
    wide = pl.BlockSpec((BLK, ATT_W), lambda n: (n, 0))
    return _call(
        "attn_bwd", body, (S // BLK,), _attn_specs(o_q, o_k) + [wide],
        [wide, blk, blk, blk, blk, pl.BlockSpec((N_KV, 2 * BLK, 8 * BLK), lambda n: (0, 0, 0)),
         pl.BlockSpec((N_KV, 1, 8 * BLK), lambda n: (0, 0, 0)), pl.BlockSpec((8, LANES), lambda n: (0, 0))],
        [_sds((S, ATT_W), BF), _sds((S, LANES), F32), _sds((S, LANES), F32), _sds((S, LANES), F32), _sds((S, LANES), F32),
         _sds((N_KV, 2 * BLK, 8 * BLK), F32), _sds((N_KV, 1, 8 * BLK), F32), _sds((8, LANES), F32)],
        [z, z, z, z, z, z, qg2, kg2, sink_rows, bias, dout], sem=("arbitrary",), plan=plan)


def _kv_combine(dkp, dkc, dvp, dvc):
    S = dkc.shape[0]
    last = S // BLK - 1

    def body(kp_ref, kc_ref, vp_ref, vc_ref, dk_ref, dv_ref):
        more = pl.program_id(0) < last
        dk_ref[...] = (kc_ref[...] + jnp.where(more, kp_ref[...], 0.0)).astype(BF)
        dv_ref[...] = (vc_ref[...] + jnp.where(more, vp_ref[...], 0.0)).astype(BF)

    cur = pl.BlockSpec((BLK, LANES), lambda n: (n, 0))
    nxt = pl.BlockSpec((BLK, LANES), lambda n: (jnp.minimum(n + 1, last), 0))
    return _call("kv_combine", body, (S // BLK,), [nxt, cur, nxt, cur], [cur, cur],
                 [_sds((S, LANES), BF), _sds((S, LANES), BF)], [dkp, dkc, dvp, dvc], sem=("parallel",))


def _merge_fwd(mixed, attn, wpu4, wau4, z, o_ga, plan=None):
    S, PW = mixed.shape
    _, _, CS = wpu4.shape
    D = 4 * CS
    tm = _pick(S, (1024,))
    tn = 256
    nsj = CS // tn
    na = o_ga // tn
    nb = (o_ga + D) // tn

    def body(m_ref, a_ref, wp_ref, wa_ref, ga_ref, gb_ref, mg_ref, yy_ref):
        yp = jnp.dot(m_ref[...], wp_ref[...], preferred_element_type=F32)
        ya = jnp.dot(a_ref[...], wa_ref[...], preferred_element_type=F32)
        mg_ref[...] = (jax.nn.sigmoid(ga_ref[...]) * yp + jax.nn.sigmoid(gb_ref[...]) * ya).astype(BF)
        yy_ref[0] = yp.astype(BF)
        yy_ref[1] = ya.astype(BF)

    return _call("merge_fwd", body, (S // tm, D // tn),
                 [pl.BlockSpec((tm, PW), lambda i, j: (i, 0)), pl.BlockSpec((tm, ATT_W), lambda i, j: (i, 0)),
                  pl.BlockSpec((None, PW, tn), lambda i, j: (j // nsj, 0, j % nsj)),
                  pl.BlockSpec((None, ATT_W, tn), lambda i, j: (j // nsj, 0, j % nsj)),
                  pl.BlockSpec((tm, tn), lambda i, j: (i, na + j)), pl.BlockSpec((tm, tn), lambda i, j: (i, nb + j))],
                 [pl.BlockSpec((tm, tn), lambda i, j: (i, j)), pl.BlockSpec((2, tm, tn), lambda i, j: (0, i, j))],
                 [_sds((S, D), BF), _sds((2, S, D), BF)], [mixed, attn, wpu4, wau4, z, z], sem=("parallel", "parallel"), plan=plan)


def _merge_bwd(do, wo, z, o_ga, yy, plan=None):
    S, D = do.shape
    tm = _pick(S, (1024,))
    tn = 256
    na = o_ga // tn
    nb = (o_ga + D) // tn

    def epi(p, e, o, rs):
        sa = jax.nn.sigmoid(e[0][rs, :])
        sb = jax.nn.sigmoid(e[1][rs, :])
        yp = e[2][0, rs, :].astype(F32)
        ya = e[2][1, rs, :].astype(F32)
        o[0][0, rs, :] = (p * yp * sa * (1.0 - sa)).astype(BF)
        o[0][1, rs, :] = (p * ya * sb * (1.0 - sb)).astype(BF)
        o[1][0, rs, :] = (p * sa).astype(BF)
        o[1][1, rs, :] = (p * sb).astype(BF)

    pair = pl.BlockSpec((2, tm, tn), lambda i, j, k: (0, i, j))
    return _mm("merge_bwd", (S // tm, D // tn, 1), NT,
               do, pl.BlockSpec((tm, D), lambda i, j, k: (i, 0)),
               wo, pl.BlockSpec((tn, D), lambda i, j, k: (j, 0)),
               [z, z, yy], [pl.BlockSpec((tm, tn), lambda i, j, k: (i, na + j)), pl.BlockSpec((tm, tn), lambda i, j, k: (i, nb + j)), pair],
               [_sds((2, S, D), BF), _sds((2, S, D), BF)], [pair, pair], None, epi, plan)


def _mm_up_t(name, dyy, which, w4):
    _, S, D = dyy.shape
    _, K, CS = w4.shape
    tm = _pick(S, (1024,))
    return _mm(name, (S // tm, 1, N_CHIPS), NT,
               dyy, pl.BlockSpec((None, tm, CS), lambda i, j, k: (which, i, k)),
               w4, pl.BlockSpec((None, K, CS), lambda i, j, k: (k, 0, 0)),
               [], [], [_sds((S, K), F32)], [pl.BlockSpec((tm, K), lambda i, j, k: (i, 0))], (tm, K), _store)[0]


def _mm_up_dw(name, a, dyy, which):
    _, S, D = dyy.shape
    K = a.shape[1]
    CS = D // N_CHIPS
    ts = _pick(S, (1024,))
    return _mm(name, (1, N_CHIPS, S // ts), TN,
               a, pl.BlockSpec((ts, K), lambda i, j, k: (k, 0)),
               dyy, pl.BlockSpec((None, ts, CS), lambda i, j, k: (which, k, j)),
               [], [], [_sds((N_CHIPS, K, CS), BF)], [pl.BlockSpec((None, K, CS), lambda i, j, k: (j, 0, 0))], (K, CS), _store)[0]


def _adamw(w, g, m, v):
    m = ADAM_B1 * m + (1.0 - ADAM_B1) * g
    v = ADAM_B2 * v + (1.0 - ADAM_B2) * (g * g)
    m_hat = m / (1.0 - ADAM_B1 ** ADAM_STEP)
    v_hat = v / (1.0 - ADAM_B2 ** ADAM_STEP)
    delta = -ADAM_LR * (m_hat / (jnp.sqrt(v_hat) + ADAM_EPS) + ADAM_WD * w)
    return delta, m, v


def _mod_fwd(c_all, w_ada, b_sh):
    D, cols = w_ada.shape
    tn = cols // 9

    def body(c_ref, w_ref, b_ref, o_ref):
        cv = c_ref[...]
        sc = (cv * jax.nn.sigmoid(cv)).astype(BF)
        o_ref[...] = jnp.dot(sc, w_ref[...].astype(BF), preferred_element_type=F32) + b_ref[...]

    return _call("mod_fwd", body, (9,),
                 [pl.BlockSpec((N_DEV, D), lambda j: (0, 0)), pl.BlockSpec((D, tn), lambda j: (0, j)), pl.BlockSpec((1, tn), lambda j: (0, j))],
                 [pl.BlockSpec((N_DEV, tn), lambda j: (0, j))], [_sds((N_DEV, cols), F32)], [c_all, w_ada, b_sh], sem=("parallel",))[0]


def _wada_bwd(c_all, dmod_sh, w, m, v, plan=None):
    D, cols = w.shape
    tn = cols // 18

    def body(c_ref, d_ref, w_ref, m_ref, v_ref, g_ref, dl_ref, nm_ref, nv_ref):
        cv = c_ref[...]
        sc = (cv * jax.nn.sigmoid(cv)).astype(BF)
        g = lax.dot_general(sc, d_ref[...].astype(BF), TN, preferred_element_type=F32)
        g_ref[...] = g
        dl_ref[...], nm_ref[...], nv_ref[...] = _adamw(w_ref[...], g, m_ref[...], v_ref[...])

    tile = pl.BlockSpec((D, tn), lambda j: (0, j))
    out = _sds((D, cols), F32)
    return _call("wada_bwd", body, (18,),
                 [pl.BlockSpec((N_DEV, D), lambda j: (0, 0)), pl.BlockSpec((N_DEV, tn), lambda j: (0, j)), tile, tile, tile],
                 [tile] * 4, [out] * 4, [c_all, dmod_sh, w, m, v], sem=("parallel",), plan=plan)


def _adam_2d(name, w, g, m, v):
    R, C = w.shape
    tr = _row_tile(R, 256)

    def body(w_ref, g_ref, m_ref, v_ref, dl_ref, nm_ref, nv_ref):
        dl_ref[...], nm_ref[...], nv_ref[...] = _adamw(w_ref[...], g_ref[...], m_ref[...], v_ref[...])

    tile = pl.BlockSpec((tr, C), lambda i: (i, 0))
    out = _sds((R, C), F32)
    return _call(name, body, (R // tr,), [tile] * 4, [tile] * 3, [out] * 3, [w, g, m, v], sem=("parallel",))


def _small_finish(parts, w, m, v):
    _, R, C = parts.shape

    def body(p_ref, w_ref, m_ref, v_ref, g_ref, dl_ref, nm_ref, nv_ref):
        g = p_ref[0]
        for d in range(1, N_DEV):
            g = g + p_ref[d]
        g_ref[...] = g
        dl_ref[...], nm_ref[...], nv_ref[...] = _adamw(w_ref[...], g, m_ref[...], v_ref[...])

    out = _sds((R, C), F32)
    return pl.pallas_call(body, out_shape=[out] * 4, name="small_finish",
                          compiler_params=pltpu.CompilerParams(vmem_limit_bytes=VMEM_LIMIT))(parts, w, m, v)


def _my_chip():
    return 2 * lax.axis_index("x") + lax.axis_index("y")


def _cast_into_slot(name, w, after):
    R, C = w.shape
    tr = _row_tile(R, 256)

    def body(w_ref, after_ref, o_ref):
        o_ref[...] = w_ref[...].astype(BF)

    return _call(name, body, (R // tr,), [pl.BlockSpec((tr, C), lambda i: (i, 0)), ANY],
                 [pl.BlockSpec((None, tr, C), lambda i: (_my_chip(), i, 0))], [_sds((N_CHIPS, R, C), BF)], [w, after],
                 sem=("parallel",))[0]


def _add_pair(name, p, q):
    _, H, C = q.shape
    tr = _row_tile(H, 512)
    nt = H // tr

    def body(p_ref, q_ref, o_ref):
        o_ref[...] = (p_ref[...].astype(F32) + q_ref[...].astype(F32)).astype(BF)

    tile = pl.BlockSpec((None, tr, C), lambda k, i: (k, i, 0))
    return _call(name, body, (N_CHIPS, nt), [pl.BlockSpec((None, tr, C), lambda k, i: (k, lax.axis_index("c") * nt + i, 0)), tile],
                 [tile], [_sds(q.shape, BF)], [p, q], sem=("parallel", "parallel"))[0]


def _sum_chips(name, u, t):
    _, H, C = u.shape
    tr = _row_tile(H, 256)

    def body(u_ref, t_ref, o_ref):
        r = _my_chip()
        own = t_ref[...].astype(F32)
        pick = lambda k: jnp.where(r == k, own, u_ref[k].astype(F32))
        o_ref[...] = ((pick(0) + pick(1)) + pick(2)) + pick(3)

    return _call(name, body, (H // tr,),
                 [pl.BlockSpec((N_CHIPS, tr, C), lambda i: (0, i, 0)), pl.BlockSpec((None, tr, C), lambda i: (_my_chip(), i, 0))],
                 [pl.BlockSpec((None, tr, C), lambda i: (lax.axis_index("c"), i, 0))], [_sds((2, H, C), F32)], [u, t],
                 sem=("parallel",))[0]


BIG = ("gu1", "down1", "w_in", "pool_mix", "pool_up", "attn_up", "o", "gu2", "down2")
MIX = ("o", "pool_up", "attn_up", "pool_mix")
EARLY = ("down1", "w_in", "pool_mix", "pool_up", "attn_up", "o")

SCHEDULE = {
    "+gather_gu1_d2d": ([("d2d", ("gu1",))], []),
    "ffn1_up": ([("ici", EARLY)], []),
    "+gather_early_d2d": ([("d2d", EARLY)], []),
    "ffn1_down": ([("ici", ("gu2",))], []),
    "mix_in": ([("ici", ("down2",)), ("d2d", ("gu2",))], []),
    "mix_out": ([("d2d", ("down2",))], []),
    "ffn2_dwd": ([("split", ("gu2",))], [("add", ("gu2",))]),
    "ffn2_dh": ([("owners", ("gu2",)), ("split", ("down2",))], [("add", ("down2",)), ("sum", ("gu2",))]),
    "merge_bwd": ([("owners", ("down2",)), ("join", ("gu2",))], [("sum", ("down2",)), ("adam", ("gu2",))]),
    "attn_bwd": ([("split", MIX), ("join", ("down2",))], [("add", MIX), ("adam", ("down2",))]),
    "mix_dwin": ([("owners", MIX)], [("sum", MIX)]),
    "mix_dh": ([("split", ("w_in",)), ("join", MIX)], [("add", ("w_in",)), ("adam", MIX)]),
    "ffn1_dact": ([("owners", ("w_in",))], [("sum", ("w_in",))]),
    "ffn1_dwd": ([("split", ("gu1",)), ("join", ("w_in",))], [("add", ("gu1",)), ("adam", ("w_in",))]),
    "ffn1_dh": ([("owners", ("gu1",)), ("split", ("down1",))], [("add", ("down1",)), ("sum", ("gu1",))]),
    "rms_mod_bwd1": ([("owners", ("down1",)), ("join", ("gu1",))], [("sum", ("down1",)), ("adam", ("gu1",))]),
    "+join_down1": ([("join", ("down1",))], [("adam", ("down1",))]),
}


class _Plan:
    def __init__(self, w2, m2, v2, full, D, gw):
        self.w2, self.m2, self.v2, self.full, self.D, self.gw = w2, m2, v2, dict(full), D, gw
        self.part, self.got, self.sums, self.landed, self.g = {}, {}, {}, {}, {}
        self.result = {}
        self.pending = {}

    def _make(self, op, names):
        if op == "ici":
            return _gather_ici_stage([self.full[k] for k in names])
        if op == "d2d":
            return _gather_d2d_stage([self.full[k] for k in names])
        if op == "split":
            return _split_stage([self.part[k] for k in names])
        if op == "owners":
            return _owners_stage([self.sums[k] for k in names])
        return _join_stage([self.g[k] for k in names])

    def stages(self, name):
        ops = SCHEDULE.get(name, ([], []))[0]
        return [self._make(*op) for op in ops]

    def done(self, name, outs):
        ops, local = SCHEDULE[name]
        for op, res in zip(ops, outs):
            store = {"ici": self.full, "d2d": self.full, "split": self.got, "owners": self.landed, "join": self.g}[op[0]]
            store.update(zip(op[1], res))
        for op, names in local:
            for k in names:
                if op == "add":
                    self.sums[k] = _add_pair("add_pair_" + k, self.part[k], self.got[k])
                elif op == "sum":
                    self.g[k] = _sum_chips("sum_chips_" + k, self.landed[k], self.sums[k])
                else:
                    g2 = self.g[k].reshape(self.w2[k].shape)
                    self.result[k] = (g2, *_adam_2d("adam_" + k, self.w2[k], g2, self.m2[k], self.v2[k]))

    def alone(self, name):
        self.done(name, _run_stages(name[1:], self.stages(name)))

    def depart(self, k, after):
        send, recv, thru, token = _gather_start("gather_" + k + "_start", self.full[k], after)
        self.pending[k] = (send, recv, thru)
        return token

    def arrive(self, k, after):
        send, recv, thru = self.pending.pop(k)
        self.full[k] = _gather_finish("gather_" + k + "_finish", send, recv, thru, after)

    def weight(self, k):
        D, gw, f = self.D, self.gw, self.full[k]
        if k in ("down1", "down2", "o"):
            return f.reshape(-1, D)
        if k == "pool_mix":
            return f.reshape(N_CHIPS, 4, gw // N_CHIPS, gw).transpose(1, 0, 2, 3).reshape(4, gw, gw)
        if k == "w_in":
            return f.reshape(-1, D)
        return f

    def partial(self, k, p):
        D, gw = self.D, self.gw
        if k in ("down1", "down2", "o", "w_in"):
            p = p.reshape(N_CHIPS, p.shape[0] // N_CHIPS, p.shape[1])
        elif k == "pool_mix":
            p = p.astype(BF).reshape(4, N_CHIPS, gw // N_CHIPS, gw).transpose(1, 0, 2, 3).reshape(N_CHIPS, gw, gw)
        self.part[k] = p


class _NoComm:
    def __init__(self, weights):
        self.w, self.part = weights, {}

    def stages(self, name):
        return []

    def alone(self, name):
        pass

    def arrive(self, k, after):
        pass

    def weight(self, k):
        return self.w[k]

    def partial(self, k, p):
        self.part[k] = p


def _row(a, i):
    return a[i:i + 1]


def _local_step(x, target, mod, g_ffn1, g_mix, g_ffn2, pool_scale, q_gain, k_gain, sinks, rel_bias, plan):
    S, D = x.shape
    half = 0.5 * mod
    tile2 = lambda g: jnp.concatenate([g, g], axis=1)
    qg2, kg2 = tile2(q_gain), tile2(k_gain)
    sink_rows = jnp.broadcast_to(sinks.reshape(N_KV, 1, 8, 1), (N_KV, 1, 8, BLK)).reshape(N_KV, 1, 8 * BLK)
    bias = _bias_table(rel_bias).reshape(N_KV, 8, BLK, 2 * BLK).transpose(0, 3, 1, 2).reshape(N_KV, 2 * BLK, 8 * BLK)

    h1 = _rms_mod_fwd("rms_mod_fwd1", x, g_ffn1, _row(mod, 0), _row(mod, 1))
    plan.arrive("gu1", h1)
    plan.alone("+gather_gu1_d2d")
    gu1, act1 = _ffn_up("ffn1_up", h1, plan.weight("gu1"), plan)
    plan.alone("+gather_early_d2d")
    x1, f1 = _mm_residual("ffn1_down", act1, plan.weight("down1"), x, _row(half, 2), plan)
    h2 = _rms_mod_fwd("rms_mod_fwd2", x1, g_mix, _row(mod, 3), _row(mod, 4))
    w_in_t = plan.weight("w_in")
    IN_W = w_in_t.shape[0]
    PW = plan.weight("pool_up").shape[1]
    o_q, o_k = PW, PW + ATT_W
    o_ga = o_k + 2 * KV_W
    tnz = _pick(IN_W, (1280, 256))
    tmz = _pick(S, (1024,))
    z = _mm("mix_in", (S // tmz, IN_W // tnz, 1), NT, h2, pl.BlockSpec((tmz, D), lambda i, j, k: (i, 0)),
            w_in_t, pl.BlockSpec((tnz, D), lambda i, j, k: (j, 0)), [], [], [_sds((S, IN_W), F32)],
            [pl.BlockSpec((tmz, tnz), lambda i, j, k: (i, j))], None, _store, plan)[0]
    pooled = _pool_fwd(z, PW)
    mixed = _pool_mix(pooled, plan.weight("pool_mix"), pool_scale)
    attn = _attn_fwd(z, o_q, o_k, qg2, kg2, sink_rows, bias)
    merged, yy = _merge_fwd(mixed, attn, plan.weight("pool_up"), plan.weight("attn_up"), z, o_ga)
    x2, fo = _mm_residual("mix_out", merged, plan.weight("o"), x1, _row(mod, 5), plan)
    h3 = _rms_mod_fwd("rms_mod_fwd3", x2, g_ffn2, _row(mod, 6), _row(mod, 7))
    gu2, act2 = _ffn_up("ffn2_up", h3, plan.weight("gu2"))
    x3, f2 = _mm_residual("ffn2_down", act2, plan.weight("down2"), x2, _row(half, 8))
    dx3, df2, loss_acc = _loss_bwd(x3, target, f2, _row(half, 8))

    dgu2 = _ffn_dact("ffn2_dact", df2, plan.weight("down2"), gu2)
    plan.partial("gu2", _ffn_dwgu("ffn2_dwgu", h3, dgu2))
    plan.partial("down2", _mm_tn("ffn2_dwd", act2, df2, (1408, 512), (512,), plan))
    dh3 = _ffn_dh("ffn2_dh", dgu2, plan.weight("gu2"), plan)
    dx2, do, acc3 = _rms_mod_bwd("rms_mod_bwd3", dh3, x2, dx3, g_ffn2, _row(mod, 7), fo, _row(mod, 5))

    dgab, dyy = _merge_bwd(do, plan.weight("o"), z, o_ga, yy, plan)
    plan.partial("o", _mm_tn("mix_dwo", merged, do, (1024,), (512,), plan))
    dmixed = _mm_up_t("pool_up_t", dyy, 0, plan.weight("pool_up"))
    dattn = _mm_up_t("attn_up_t", dyy, 1, plan.weight("attn_up"))
    plan.partial("pool_up", _mm_up_dw("pool_up_dw", mixed, dyy, 0))
    plan.partial("attn_up", _mm_up_dw("attn_up_dw", attn, dyy, 1))
    dpooled, dpm, dps = _pool_mix_bwd(pooled, plan.weight("pool_mix"), pool_scale, dmixed)
    plan.partial("pool_mix", dpm)
    du_pool = _pool_bwd(dpooled)
    dq, dkp, dkc, dvp, dvc, dl, dsink, dgain = _attn_bwd(z, o_q, o_k, qg2, kg2, sink_rows, bias, dattn, plan)
    dk, dv = _kv_combine(dkp, dkc, dvp, dvc)
    drb = _rel_bias_grad(dl.reshape(N_KV, 2 * BLK, 8, BLK).transpose(0, 2, 3, 1).reshape(N_HEADS, BLK * 2 * BLK))
    dz = jnp.concatenate([du_pool, dq, dk, dv, dgab[0], dgab[1]], axis=1)
    plan.partial("w_in", _mm_tn("mix_dwin", dz, h2, (1280, 256), (512,), plan))
    tnd = _pick(D, (512,))
    dh2 = _mm("mix_dh", (S // tmz, D // tnd, 1), NN, dz, pl.BlockSpec((tmz, IN_W), lambda i, j, k: (i, 0)),
              w_in_t, pl.BlockSpec((IN_W, tnd), lambda i, j, k: (0, j)), [], [], [_sds((S, D), F32)],
              [pl.BlockSpec((tmz, tnd), lambda i, j, k: (i, j))], None, _store, plan)[0]
    dx1, df1, acc2 = _rms_mod_bwd("rms_mod_bwd2", dh2, x1, dx2, g_mix, _row(mod, 4), f1, _row(half, 2))

    dgu1 =_ffn_dact("ffn1_dact", df1, plan.weight("down1"), gu1, plan)
    plan.partial("gu1", _ffn_dwgu("ffn1_dwgu", h1, dgu1, plan))
    plan.partial("down1", _mm_tn("ffn1_dwd", act1, df1, (1408, 512), (512,), plan))
    dh1 = _ffn_dh("ffn1_dh", dgu1, plan.weight("gu1"), plan)
    grad_x, acc1 = _rms_mod_bwd("rms_mod_bwd1", dh1, x, dx1, g_ffn1, _row(mod, 1), plan=plan)

    dmod = jnp.concatenate([_row(acc1, 0), _row(acc1, 1), 0.5 * _row(acc2, 3),
                            _row(acc2, 0), _row(acc2, 1), _row(acc3, 3),
                            _row(acc3, 0), _row(acc3, 1), 0.5 * _row(loss_acc, 1)], axis=0)
    fold = lambda r: r[:, :HEAD_DIM] + r[:, HEAD_DIM:]
    small = dict(
        dmod=dmod, g_ffn1=_row(acc1, 2), g_mix=_row(acc2, 2), g_ffn2=_row(acc3, 2), pool_scale=_row(dps, 0),
        q_gain=fold(_row(dgain, 0)), k_gain=fold(_row(dgain, 1)),
        sinks=jnp.sum(dsink.reshape(N_HEADS, BLK), axis=1).reshape(1, N_HEADS), rel_bias=drb,
        loss=(0.5 / D) * jnp.sum(_row(loss_acc, 0)).reshape(1, 1))
    return grad_x, small


SMALL_ORDER = ("dmod", "g_ffn1", "g_mix", "g_ffn2", "pool_scale", "q_gain", "k_gain", "sinks", "rel_bias", "loss")


def _pack_small(vals):
    flat = jnp.concatenate([vals[k].reshape(-1) for k in SMALL_ORDER])
    n = flat.shape[0]
    rows = -(-n // (8 * LANES)) * 8
    return jnp.pad(flat, (0, rows * LANES - n)).reshape(rows, LANES)


def _unpack_small(packed, like):
    flat = packed.reshape(-1)
    out, off = {}, 0
    for k in SMALL_ORDER:
        n = int(np.prod(like[k].shape))
        out[k] = flat[off:off + n].reshape(like[k].shape)
        off += n
    return out


def kernel(x, c, w_ada, b_ada, g_ffn1, w_ffn1_gu, w_ffn1_down, g_mix, w_in, pool_mix, pool_scale, w_pool_up, q_gain, k_gain, sinks, rel_bias, w_attn_up, w_o, g_ffn2, w_ffn2_gu, w_ffn2_down, loss_target, m_w_ada, m_b_ada, m_g_ffn1, m_w_ffn1_gu, m_w_ffn1_down, m_g_mix, m_w_in, m_pool_mix, m_pool_scale, m_w_pool_up, m_q_gain, m_k_gain, m_sinks, m_rel_bias, m_w_attn_up, m_w_o, m_g_ffn2, m_w_ffn2_gu, m_w_ffn2_down, v_w_ada, v_b_ada, v_g_ffn1, v_w_ffn1_gu, v_w_ffn1_down, v_g_mix, v_w_in, v_pool_mix, v_pool_scale, v_w_pool_up, v_q_gain, v_k_gain, v_sinks, v_rel_bias, v_w_attn_up, v_w_o, v_g_ffn2, v_w_ffn2_gu, v_w_ffn2_down):
    S, D = x.shape[1], x.shape[2]
    gw = pool_mix.shape[3]
    r = 2 * lax.axis_index("x") + lax.axis_index("y")

    two_d = lambda a: a.reshape(-1, a.shape[-1])
    w_sh = dict(gu1=w_ffn1_gu, down1=w_ffn1_down, w_in=w_in, pool_mix=pool_mix, pool_up=w_pool_up, attn_up=w_attn_up, o=w_o,
                gu2=w_ffn2_gu, down2=w_ffn2_down)
    m_sh = dict(gu1=m_w_ffn1_gu, down1=m_w_ffn1_down, w_in=m_w_in, pool_mix=m_pool_mix, pool_up=m_w_pool_up, attn_up=m_w_attn_up,
                o=m_w_o, gu2=m_w_ffn2_gu, down2=m_w_ffn2_down)
    v_sh = dict(gu1=v_w_ffn1_gu, down1=v_w_ffn1_down, w_in=v_w_in, pool_mix=v_pool_mix, pool_up=v_w_pool_up, attn_up=v_w_attn_up,
                o=v_w_o, gu2=v_w_ffn2_gu, down2=v_w_ffn2_down)
    view = lambda k, a: two_d(a).T if k == "w_in" else two_d(a)
    unview = lambda k, a: (a.T if k == "w_in" else a).reshape(w_sh[k].shape)
    w2 = {k: view(k, w_sh[k]) for k in BIG}
    plan = _Plan(w2, {k: view(k, m_sh[k]) for k in BIG}, {k: view(k, v_sh[k]) for k in BIG},
                 {"gu1": _cast_into_slot("cast_gu1", w2["gu1"], c)}, D, gw)

    c_all = _gather_all("gather_c", jnp.broadcast_to(c, (8, D)))[:, 0, :]
    cols = w_ada.shape[2]
    b_sh = lax.dynamic_slice(b_ada, (0, r * cols), (1, cols))
    mod_cols = _mod_fwd(c_all, w_ada[0], b_sh)
    mod_all = _chip_exchange("mod_exchange", mod_cols)

    token = plan.depart("gu1", mod_all)
    plan.full.update({k: _cast_into_slot("cast_" + k, w2[k], token) for k in BIG[1:]})
    me = 4 * lax.axis_index("x") + 2 * lax.axis_index("y") + lax.axis_index("c")
    mod = lax.dynamic_slice(mod_all, (0, me, 0), (N_CHIPS, 1, cols)).reshape(9, D) + token[0, 0]

    grad_x, small = _local_step(x[0], loss_target[0], mod, g_ffn1, g_mix, g_ffn2, pool_scale, q_gain, k_gain,
                                sinks, rel_bias, plan)

    small_w = dict(dmod=b_ada, g_ffn1=g_ffn1, g_mix=g_mix, g_ffn2=g_ffn2, pool_scale=pool_scale, q_gain=q_gain, k_gain=k_gain,
                   sinks=sinks, rel_bias=rel_bias, loss=jnp.zeros((1, 1), F32))
    small_m = dict(dmod=m_b_ada, g_ffn1=m_g_ffn1, g_mix=m_g_mix, g_ffn2=m_g_ffn2, pool_scale=m_pool_scale, q_gain=m_q_gain,
                   k_gain=m_k_gain, sinks=m_sinks, rel_bias=m_rel_bias, loss=jnp.zeros((1, 1), F32))
    small_v = dict(dmod=v_b_ada, g_ffn1=v_g_ffn1, g_mix=v_g_mix, g_ffn2=v_g_ffn2, pool_scale=v_pool_scale, q_gain=v_q_gain,
                   k_gain=v_k_gain, sinks=v_sinks, rel_bias=v_rel_bias, loss=jnp.ones((1, 1), F32))
    small_all = _gather_all("gather_small", _pack_small(small))
    sg, sd, sm, sv = [_unpack_small(a, small_w) for a in
                      _small_finish(small_all, _pack_small(small_w), _pack_small(small_m), _pack_small(small_v))]
    loss = sg["loss"].reshape(())

    dmod_all = small_all.reshape(N_DEV, -1)[:, :9 * D]
    dmod_sh = lax.dynamic_slice(dmod_all, (0, r * cols), (N_DEV, cols))
    g_ada, d_ada, nm_ada, nv_ada = _wada_bwd(c_all, dmod_sh, w_ada[0], m_w_ada[0], v_w_ada[0])
    plan.alone("+join_down1")

    big = [{k: unview(k, plan.result[k][i]) for k in BIG} for i in range(4)]

    def ordered(b, ada, sm_):
        return (ada[None], sm_["dmod"], sm_["g_ffn1"], b["gu1"], b["down1"], sm_["g_mix"], b["w_in"], b["pool_mix"],
                sm_["pool_scale"], b["pool_up"], sm_["q_gain"], sm_["k_gain"], sm_["sinks"], sm_["rel_bias"], b["attn_up"],
                b["o"], sm_["g_ffn2"], b["gu2"], b["down2"])

    return (loss, grad_x[None], *ordered(big[0], g_ada, sg), *ordered(big[1], d_ada, sd), *ordered(big[2], nm_ada, sm),
            *ordered(big[3], nv_ada, sv))
```

```python
import numpy as np
import jax
import jax.numpy as jnp
from jax import lax
from jax.experimental import pallas as pl
from jax.experimental.pallas import tpu as pltpu

BF = jnp.bfloat16
F32 = jnp.float32
MESH = pl.DeviceIdType.MESH

EPS = 1e-6
NEG_INF = -1e30
HEAD_DIM = 64
N_HEADS = 16
N_KV = 2
ATT_W = N_HEADS * HEAD_DIM
KV_W = N_KV * HEAD_DIM
BLK = 128
NUM_BUCKETS = 32
POOL_MAX_W = 16
N_CHIPS = 4
N_DEV = 8
LANES = 128
ADAM_LR, ADAM_B1, ADAM_B2, ADAM_EPS, ADAM_WD, ADAM_STEP = 0.001, 0.9, 0.999, 1e-08, 0.01, 10
VMEM_LIMIT = 52 * 1024 * 1024
ANY = pl.BlockSpec(memory_space=pl.ANY)


def _pick(dim, prefs):
    for p in prefs:
        if p <= dim and dim % p == 0:
            return p
    return dim


def _row_tile(rows, cap):
    return max(t for t in range(16, min(rows, cap) + 1, 16) if rows % t == 0)


def _sds(shape, dtype):
    return jax.ShapeDtypeStruct(tuple(shape), dtype)


def _place():
    return lax.axis_index("x"), lax.axis_index("y"), lax.axis_index("c")


def _other_chips(x, y):
    return [(1 - x, y), (x, 1 - y), (1 - x, 1 - y)]


def _chip_of(chip):
    return 2 * chip[0] + chip[1]


def _half_rows(ref, lead, cc, h):
    return ref.at[lead, pl.ds(pl.multiple_of(cc * h, 16), h), :]


class _Stage:
    def __init__(self, bufs, outs, alias, n_sem, start, wait):
        self.bufs, self.outs, self.alias, self.n_sem, self.start, self.wait = bufs, outs, alias, n_sem, start, wait


def _stage_plumbing(stages, n_in0, n_out0):
    bufs, outs, aliases, spans, scratch = [], [], {}, [], []
    for st in stages:
        i0, o0 = len(bufs), len(outs)
        bufs += list(st.bufs)
        outs += list(st.outs)
        for a, b in st.alias.items():
            aliases[n_in0 + i0 + a] = n_out0 + o0 + b
        spans.append((i0, len(bufs), o0, len(outs)))
        scratch += [pltpu.SemaphoreType.DMA((st.n_sem,)), pltpu.SemaphoreType.DMA((st.n_sem,))]

    def run(which, in_refs, out_refs, sem_refs):
        for s, st in enumerate(stages):
            i0, i1, o0, o1 = spans[s]
            getattr(st, which)(in_refs[i0:i1], out_refs[o0:o1], sem_refs[2 * s], sem_refs[2 * s + 1])

    def split(flat):
        return [list(flat[o0:o1]) for (_, _, o0, o1) in spans]

    return bufs, outs, aliases, scratch, run, split


def _run_stages(name, stages):
    bufs, outs, aliases, scratch, run, split = _stage_plumbing(stages, 0, 0)
    ni, no = len(bufs), len(outs)

    def body(*refs):
        ins, os_, sems = refs[:ni], refs[ni:ni + no], refs[ni + no:]
        run("start", ins, os_, sems)
        run("wait", ins, os_, sems)

    res = pl.pallas_call(body, in_specs=[ANY] * ni, out_specs=[ANY] * no, out_shape=outs, input_output_aliases=aliases,
                         scratch_shapes=scratch, name=name)(*bufs)
    return split(res)


def _call(name, body, grid, in_specs, out_specs, out_shape, args, scratch=(), sem=None, plan=None):
    stages = plan.stages(name) if plan is not None else []
    n_in, n_out, n_scr = len(args), len(out_shape), len(scratch)
    if not stages:
        return pl.pallas_call(body, grid=grid, in_specs=list(in_specs), out_specs=list(out_specs), out_shape=list(out_shape),
                              scratch_shapes=list(scratch), name=name,
                              compiler_params=pltpu.CompilerParams(dimension_semantics=sem, vmem_limit_bytes=VMEM_LIMIT))(*args)
    bufs, s_outs, aliases, s_scratch, run, split = _stage_plumbing(stages, n_in, n_out)
    nb, nso = len(bufs), len(s_outs)

    def hosted(*refs):
        ins = refs[:n_in]
        s_ins = refs[n_in:n_in + nb]
        outs = refs[n_in + nb:n_in + nb + n_out]
        s_os = refs[n_in + nb + n_out:n_in + nb + n_out + nso]
        scr = refs[n_in + nb + n_out + nso:n_in + nb + n_out + nso + n_scr]
        sems = refs[n_in + nb + n_out + nso + n_scr:]
        first = pl.program_id(0) == 0
        last = pl.program_id(0) == grid[0] - 1
        for d in range(1, len(grid)):
            first = first & (pl.program_id(d) == 0)
            last = last & (pl.program_id(d) == grid[d] - 1)

        @pl.when(first)
        def _():
            run("start", s_ins, s_os, sems)

        body(*ins, *outs, *scr)

        @pl.when(last)
        def _():
            run("wait", s_ins, s_os, sems)

    res = pl.pallas_call(
        hosted, grid=grid, in_specs=list(in_specs) + [ANY] * nb, out_specs=list(out_specs) + [ANY] * nso,
        out_shape=list(out_shape) + s_outs, input_output_aliases=aliases, scratch_shapes=list(scratch) + s_scratch, name=name,
        compiler_params=pltpu.CompilerParams(dimension_semantics=("arbitrary",) * len(grid), vmem_limit_bytes=VMEM_LIMIT))(*args, *bufs)
    plan.done(name, split(res[n_out:]))
    return list(res[:n_out])


def _gather_ici_stage(fulls):
    n = len(fulls)

    def copy(i, j, slot, ins, outs, send, recv):
        x, y, c = _place()
        chip = _other_chips(x, y)[j]
        h = fulls[i].shape[1] // 2
        s = 3 * i + j
        return pltpu.make_async_remote_copy(_half_rows(ins[i], 2 * x + y, c, h), _half_rows(outs[i], slot(x, y, chip), c, h),
                                            send.at[s], recv.at[s], device_id=(*chip, c), device_id_type=MESH)

    mine = lambda x, y, chip: 2 * x + y
    theirs = lambda x, y, chip: _chip_of(chip)

    def start(ins, outs, send, recv):
        for i in range(n):
            for j in range(3):
                copy(i, j, mine, ins, outs, send, recv).start()

    def wait(ins, outs, send, recv):
        for i in range(n):
            for j in range(3):
                copy(i, j, theirs, ins, outs, send, recv).wait_recv()
        for i in range(n):
            for j in range(3):
                copy(i, j, mine, ins, outs, send, recv).wait_send()

    return _Stage(fulls, [_sds(f.shape, f.dtype) for f in fulls], {i: i for i in range(n)}, 3 * n, start, wait)


def _gather_d2d_stage(fulls):
    n = len(fulls)

    def copy(i, j, cc, ins, outs, send, recv):
        x, y, c = _place()
        rj = _chip_of(_other_chips(x, y)[j])
        h = fulls[i].shape[1] // 2
        half = cc(c)
        s = 3 * i + j
        return pltpu.make_async_remote_copy(_half_rows(ins[i], rj, half, h), _half_rows(outs[i], rj, half, h),
                                            send.at[s], recv.at[s], device_id=(x, y, 1 - c), device_id_type=MESH)

    mine = lambda c: c
    theirs = lambda c: 1 - c

    def start(ins, outs, send, recv):
        for i in range(n):
            for j in range(3):
                copy(i, j, mine, ins, outs, send, recv).start()

    def wait(ins, outs, send, recv):
        for i in range(n):
            for j in range(3):
                copy(i, j, theirs, ins, outs, send, recv).wait_recv()
        for i in range(n):
            for j in range(3):
                copy(i, j, mine, ins, outs, send, recv).wait_send()

    return _Stage(fulls, [_sds(f.shape, f.dtype) for f in fulls], {i: i for i in range(n)}, 3 * n, start, wait)


def _split_stage(parts):
    n = len(parts)

    def copy(i, ins, outs, send, recv):
        x, y, c = _place()
        h = parts[i].shape[1] // 2
        return pltpu.make_async_remote_copy(_half_rows(ins[i], slice(None), 1 - c, h), outs[i], send.at[i], recv.at[i],
                                            device_id=(x, y, 1 - c), device_id_type=MESH)

    def start(ins, outs, send, recv):
        for i in range(n):
            copy(i, ins, outs, send, recv).start()

    def wait(ins, outs, send, recv):
        for i in range(n):
            copy(i, ins, outs, send, recv).wait_recv()
        for i in range(n):
            copy(i, ins, outs, send, recv).wait_send()

    return _Stage(parts, [_sds((N_CHIPS, p.shape[1] // 2, p.shape[2]), p.dtype) for p in parts], {}, n, start, wait)


def _owners_stage(sums):
    n = len(sums)

    def copy(i, j, mine, ins, outs, send, recv):
        x, y, c = _place()
        chip = _other_chips(x, y)[j]
        slot = (2 * x + y) if mine else _chip_of(chip)
        return pltpu.make_async_remote_copy(ins[i].at[_chip_of(chip)], outs[i].at[slot], send.at[3 * i + j], recv.at[3 * i + j],
                                            device_id=(*chip, c), device_id_type=MESH)

    def start(ins, outs, send, recv):
        for i in range(n):
            for j in range(3):
                copy(i, j, True, ins, outs, send, recv).start()

    def wait(ins, outs, send, recv):
        for i in range(n):
            for j in range(3):
                copy(i, j, False, ins, outs, send, recv).wait_recv()
        for i in range(n):
            for j in range(3):
                copy(i, j, True, ins, outs, send, recv).wait_send()

    return _Stage(sums, [_sds(s.shape, s.dtype) for s in sums], {}, 3 * n, start, wait)


def _join_stage(gs):
    n = len(gs)

    def copy(i, mine, ins, outs, send, recv):
        x, y, c = _place()
        slot = c if mine else 1 - c
        return pltpu.make_async_remote_copy(ins[i].at[slot], outs[i].at[slot], send.at[i], recv.at[i],
                                            device_id=(x, y, 1 - c), device_id_type=MESH)

    def start(ins, outs, send, recv):
        for i in range(n):
            copy(i, True, ins, outs, send, recv).start()

    def wait(ins, outs, send, recv):
        for i in range(n):
            copy(i, False, ins, outs, send, recv).wait_recv()
        for i in range(n):
            copy(i, True, ins, outs, send, recv).wait_send()

    return _Stage(gs, [_sds(g.shape, g.dtype) for g in gs], {i: i for i in range(n)}, n, start, wait)


def _chip_exchange(name, arr):
    def body(src, dst, send, recv, loc):
        x, y, c = _place()
        r = 2 * x + y
        chips = _other_chips(x, y)

        def cp(j, slot):
            return pltpu.make_async_remote_copy(src, dst.at[slot], send.at[j], recv.at[j], device_id=(*chips[j], c), device_id_type=MESH)

        mine = pltpu.make_async_copy(src, dst.at[r], loc)
        mine.start()
        for j in range(3):
            cp(j, r).start()
        for j in range(3):
            cp(j, _chip_of(chips[j])).wait_recv()
        for j in range(3):
            cp(j, r).wait_send()
        mine.wait()

    return pl.pallas_call(body, in_specs=[ANY], out_specs=ANY, out_shape=_sds((N_CHIPS, *arr.shape), arr.dtype),
                          scratch_shapes=[pltpu.SemaphoreType.DMA((3,)), pltpu.SemaphoreType.DMA((3,)), pltpu.SemaphoreType.DMA],
                          name=name)(arr)


def _gather_all(name, arr):
    def body(src, dst, send, recv, loc):
        x, y, c = _place()

        def cp(k, slot_of_me):
            px, py, pc = x ^ ((k >> 2) & 1), y ^ ((k >> 1) & 1), c ^ (k & 1)
            slot = (4 * x + 2 * y + c) if slot_of_me else (4 * px + 2 * py + pc)
            return pltpu.make_async_remote_copy(src, dst.at[slot], send.at[k - 1], recv.at[k - 1],
                                                device_id=(px, py, pc), device_id_type=MESH)

        mine = pltpu.make_async_copy(src, dst.at[4 * x + 2 * y + c], loc)
        mine.start()
        for k in range(1, N_DEV):
            cp(k, True).start()
        for k in range(1, N_DEV):
            cp(k, False).wait_recv()
        for k in range(1, N_DEV):
            cp(k, True).wait_send()
        mine.wait()

    return pl.pallas_call(body, in_specs=[ANY], out_specs=ANY, out_shape=_sds((N_DEV, *arr.shape), arr.dtype),
                          scratch_shapes=[pltpu.SemaphoreType.DMA((N_DEV - 1,)), pltpu.SemaphoreType.DMA((N_DEV - 1,)), pltpu.SemaphoreType.DMA],
                          name=name)(arr)


NN = (((1,), (0,)), ((), ()))
NT = (((1,), (1,)), ((), ()))
TN = (((0,), (0,)), ((), ()))


ALL = slice(None)


def _mm(name, grid, dims, a, a_spec, b, b_spec, extras, extra_specs, out_shapes, out_specs, acc_shape, epilogue, plan=None):
    n_k = grid[2]
    n_e = len(extras)
    n_o = len(out_shapes)

    def body(*refs):
        a_ref, b_ref = refs[0], refs[1]
        e_refs = refs[2:2 + n_e]
        o_refs = refs[2 + n_e:2 + n_e + n_o]
        p = lax.dot_general(a_ref[...].astype(BF), b_ref[...].astype(BF), dims, preferred_element_type=F32)
        if n_k == 1:
            epilogue(p, e_refs, o_refs, ALL)
        else:
            acc = refs[-1]
            k = pl.program_id(2)

            @pl.when(k == 0)
            def _():
                acc[...] = p

            @pl.when(k > 0)
            def _():
                acc[...] += p

            @pl.when(k == n_k - 1)
            def _():
                epilogue(acc[...], e_refs, o_refs, ALL)

    scratch = [] if n_k == 1 else [pltpu.VMEM(acc_shape, F32)]
    return _call(name, body, grid, [a_spec, b_spec, *extra_specs], out_specs, out_shapes, [a, b, *extras], scratch,
                 ("parallel", "parallel", "arbitrary"), plan)


def _store(p, e, o, rs):
    o[0][rs, :] = p.astype(o[0].dtype)


def _rms_mod_fwd(name, x, gain, shift, scale, plan=None):
    S, D = x.shape
    ts = _pick(S, (512,))

    def body(x_ref, g_ref, sh_ref, sc_ref, h_ref):
        xv = x_ref[...]
        r = lax.rsqrt(jnp.mean(xv * xv, axis=-1, keepdims=True) + EPS)
        n = xv * r * g_ref[...]
        h_ref[...] = (n * (1.0 + sc_ref[...]) + sh_ref[...]).astype(BF)

    row = pl.BlockSpec((ts, D), lambda i: (i, 0))
    vec = pl.BlockSpec((1, D), lambda i: (0, 0))
    return _call(name, body, (S // ts,), [row, vec, vec, vec], [row], [_sds((S, D), BF)], [x, gain, shift, scale],
                 sem=("parallel",), plan=plan)[0]


def _acc_rows(acc_ref, first, part):
    @pl.when(first)
    def _():
        acc_ref[...] = part

    @pl.when(jnp.logical_not(first))
    def _():
        acc_ref[...] += part


def _rms_mod_bwd(name, dh, x, dres, gain, scale, f=None, coef=None, plan=None):
    S, D = x.shape
    ts = _pick(S, (256,))
    gated = f is not None

    def body(dh_ref, x_ref, dr_ref, g_ref, sc_ref, *rest):
        xv = x_ref[...]
        dhv = dh_ref[...]
        g = g_ref[...]
        r = lax.rsqrt(jnp.mean(xv * xv, axis=-1, keepdims=True) + EPS)
        xhat = xv * r
        dn = dhv * (1.0 + sc_ref[...])
        dxhat = dn * g
        dx = dr_ref[...] + r * (dxhat - xhat * jnp.mean(dxhat * xhat, axis=-1, keepdims=True))
        rows = [jnp.sum(dhv, axis=0, keepdims=True), jnp.sum(dhv * (xhat * g), axis=0, keepdims=True),
                jnp.sum(dn * xhat, axis=0, keepdims=True)]
        if gated:
            f_ref, c_ref, dx_ref, df_ref, acc_ref = rest
            df_ref[...] = (dx * c_ref[...]).astype(BF)
            rows.append(jnp.sum(dx * f_ref[...].astype(F32), axis=0, keepdims=True))
        else:
            dx_ref, acc_ref = rest
        dx_ref[...] = dx
        _acc_rows(acc_ref, pl.program_id(0) == 0, jnp.concatenate(rows + [jnp.zeros((8 - len(rows), D), F32)], axis=0))

    row = pl.BlockSpec((ts, D), lambda i: (i, 0))
    vec = pl.BlockSpec((1, D), lambda i: (0, 0))
    acc = pl.BlockSpec((8, D), lambda i: (0, 0))
    if gated:
        return _call(name, body, (S // ts,), [row, row, row, vec, vec, row, vec], [row, row, acc],
                     [_sds((S, D), F32), _sds((S, D), BF), _sds((8, D), F32)], [dh, x, dres, gain, scale, f, coef],
                     sem=("arbitrary",), plan=plan)
    return _call(name, body, (S // ts,), [row, row, row, vec, vec], [row, acc],
                 [_sds((S, D), F32), _sds((8, D), F32)], [dh, x, dres, gain, scale], sem=("arbitrary",), plan=plan)


def _loss_bwd(x3, target, f, coef):
    S, D = x3.shape
    ts = _pick(S, (512,))

    def body(x_ref, t_ref, f_ref, c_ref, dx_ref, df_ref, acc_ref):
        e = x_ref[...] - t_ref[...]
        dx = e * (1.0 / D)
        dx_ref[...] = dx
        df_ref[...] = (dx * c_ref[...]).astype(BF)
        part = jnp.concatenate([jnp.sum(e * e, axis=0, keepdims=True), jnp.sum(dx * f_ref[...].astype(F32), axis=0, keepdims=True),
                                jnp.zeros((6, D), F32)], axis=0)
        _acc_rows(acc_ref, pl.program_id(0) == 0, part)

    row = pl.BlockSpec((ts, D), lambda i: (i, 0))
    return _call("loss_bwd", body, (S // ts,), [row, row, row, pl.BlockSpec((1, D), lambda i: (0, 0))],
                 [row, row, pl.BlockSpec((8, D), lambda i: (0, 0))],
                 [_sds((S, D), F32), _sds((S, D), BF), _sds((8, D), F32)], [x3, target, f, coef], sem=("arbitrary",))


def _silu_parts(g):
    s = jax.nn.sigmoid(g)
    return s, g * s


def _ffn_up(name, h, wgu4, plan=None):
    S, D = h.shape
    SH = wgu4.shape[2]
    F = 2 * SH
    tm = _pick(S, (512,))
    tn = _pick(SH, (1408, 256))
    nts = SH // tn

    def body(h_ref, wg_ref, wu_ref, gu_ref, act_ref):
        hv = h_ref[...]
        g = jnp.dot(hv, wg_ref[...], preferred_element_type=F32)
        u = jnp.dot(hv, wu_ref[...], preferred_element_type=F32)
        gu_ref[0] = g.astype(BF)
        gu_ref[1] = u.astype(BF)
        act_ref[...] = (_silu_parts(g)[1] * u).astype(BF)

    return _call(name, body, (S // tm, F // tn),
                 [pl.BlockSpec((tm, D), lambda i, j: (i, 0)),
                  pl.BlockSpec((None, D, tn), lambda i, j: (j // nts, 0, j % nts)),
                  pl.BlockSpec((None, D, tn), lambda i, j: (2 + j // nts, 0, j % nts))],
                 [pl.BlockSpec((2, tm, tn), lambda i, j: (0, i, j)), pl.BlockSpec((tm, tn), lambda i, j: (i, j))],
                 [_sds((2, S, F), BF), _sds((S, F), BF)], [h, wgu4, wgu4], sem=("parallel", "parallel"), plan=plan)


def _mm_residual(name, a, w, x_in, coef, plan=None):
    S, K = a.shape
    D = w.shape[1]
    tm = _pick(S, (1024,))
    tn = _pick(D, (512,))
    tk = K

    def epi(p, e, o, rs):
        o[0][rs, :] = e[0][rs, :] + e[1][...] * p
        o[1][rs, :] = p.astype(BF)

    tile = pl.BlockSpec((tm, tn), lambda i, j, k: (i, j))
    return _mm(name, (S // tm, D // tn, K // tk), NN,
               a, pl.BlockSpec((tm, tk), lambda i, j, k: (i, k)),
               w, pl.BlockSpec((tk, tn), lambda i, j, k: (k, j)),
               [x_in, coef], [tile, pl.BlockSpec((1, tn), lambda i, j, k: (0, j))],
               [_sds((S, D), F32), _sds((S, D), BF)], [tile, tile], (tm, tn), epi, plan)


def _ffn_dact(name, df, wd, gu, plan=None):
    S, D = df.shape
    F = wd.shape[0]
    tm = _pick(S, (512,))
    tn = _pick(F, (1408, 256))

    def epi(p, e, o, rs):
        g = e[0][0, rs, :].astype(F32)
        u = e[0][1, rs, :].astype(F32)
        s, sg = _silu_parts(g)
        o[0][0, rs, :] = (p * u * (s * (1.0 + g * (1.0 - s)))).astype(BF)
        o[0][1, rs, :] = (p * sg).astype(BF)

    pair = pl.BlockSpec((2, tm, tn), lambda i, j, k: (0, i, j))
    return _mm(name, (S // tm, F // tn, 1), NT,
               df, pl.BlockSpec((tm, D), lambda i, j, k: (i, 0)),
               wd, pl.BlockSpec((tn, D), lambda i, j, k: (j, 0)),
               [gu], [pair], [_sds((2, S, F), BF)], [pair], None, epi, plan)[0]


def _ffn_dh(name, dgu, wgu4, plan=None):
    _, S, F = dgu.shape
    _, D, SH = wgu4.shape
    tm = _pick(S, (1024,))
    tn = _pick(D, (512,))

    def body(a_ref, b_ref, o_ref, acc):
        k = pl.program_id(2)
        p = lax.dot_general(a_ref[:, :SH], b_ref[0], NT, preferred_element_type=F32)
        p = p + lax.dot_general(a_ref[:, SH:], b_ref[1], NT, preferred_element_type=F32)

        @pl.when(k == 0)
        def _():
            acc[...] = p

        @pl.when(k == 1)
        def _():
            o_ref[...] = acc[...] + p

    return _call(name, body, (S // tm, D // tn, 2),
                 [pl.BlockSpec((None, tm, F), lambda i, j, k: (k, i, 0)), pl.BlockSpec((2, tn, SH), lambda i, j, k: (k, j, 0))],
                 [pl.BlockSpec((tm, tn), lambda i, j, k: (i, j))], [_sds((S, D), F32)], [dgu, wgu4],
                 [pltpu.VMEM((tm, tn), F32)], ("parallel", "parallel", "arbitrary"), plan)[0]


def _ffn_dwgu(name, h, dgu, plan=None):
    _, S, F = dgu.shape
    D = h.shape[1]
    SH = F // 2
    tk1 = _pick(D, (512,))
    tn = _pick(SH, (1408, 256))
    ts = _pick(S, (4096, 1024))
    npj = F // tn
    nsj = SH // tn
    return _mm(name, (D // tk1, 2 * npj, S // ts), TN,
               h, pl.BlockSpec((ts, tk1), lambda i, j, k: (k, i)),
               dgu, pl.BlockSpec((None, ts, tn), lambda i, j, k: (j // npj, k, j % npj)),
               [], [], [_sds((4, D, SH), BF)],
               [pl.BlockSpec((None, tk1, tn), lambda i, j, k: (j // nsj, i, j % nsj))], (tk1, tn), _store, plan)[0]


def _mm_tn(name, a, b, tk1_prefs, tn_prefs, plan=None):
    S, K1 = a.shape
    N = b.shape[1]
    tk1 = _pick(K1, tk1_prefs)
    tn = _pick(N, tn_prefs)
    ts = _pick(S, (4096, 1024))
    return _mm(name, (K1 // tk1, N // tn, S // ts), TN,
               a, pl.BlockSpec((ts, tk1), lambda i, j, k: (k, i)),
               b, pl.BlockSpec((ts, tn), lambda i, j, k: (k, j)),
               [], [], [_sds((K1, N), BF)], [pl.BlockSpec((tk1, tn), lambda i, j, k: (i, j))], (tk1, tn), _store, plan)[0]


def _pool_window(ext, w, back):
    n = ext.shape[0]
    s = ext
    for step in (1, 2, 4, 8):
        sh = pltpu.roll(s, (n - step) if back else step, axis=0)
        s = jnp.where(w > step, s + sh, s)
    return s


def _pool_fwd(z, PW):
    S = z.shape[0]
    tc = _pick(S, (1024,))
    bpg = (PW // 4) // LANES
    H = POOL_MAX_W

    def body(prev_ref, u_ref, o_ref):
        i = pl.program_id(0)
        j = pl.program_id(1)
        w = lax.shift_left(jnp.int32(2), j // bpg)
        u = u_ref[...]
        prev = jnp.where(i > 0, prev_ref[...], 0.0)
        s = _pool_window(jnp.concatenate([prev, u], axis=0), w, False)[H:]
        t = i * tc + lax.broadcasted_iota(jnp.int32, (tc, LANES), 0)
        cnt = jnp.minimum(t + 1, w).astype(F32)
        o_ref[...] = (s / cnt - u).astype(BF)

    r = tc // H
    return _call("pool_fwd", body, (S // tc, PW // LANES),
                 [pl.BlockSpec((H, LANES), lambda i, j: (jnp.maximum(i * r - 1, 0), j)),
                  pl.BlockSpec((tc, LANES), lambda i, j: (i, j))],
                 [pl.BlockSpec((tc, LANES), lambda i, j: (i, j))], [_sds((S, PW), BF)], [z, z], sem=("parallel", "parallel"))[0]


def _pool_bwd(dpooled):
    S, PW = dpooled.shape
    tc = _pick(S, (1024,))
    bpg = (PW // 4) // LANES
    H = POOL_MAX_W
    last = S // tc - 1

    def body(dp_ref, nxt_ref, o_ref):
        i = pl.program_id(0)
        j = pl.program_id(1)
        w = lax.shift_left(jnp.int32(2), j // bpg)
        dp = dp_ref[...]
        nxt = jnp.where(i < last, nxt_ref[...], 0.0)
        ext = jnp.concatenate([dp, nxt], axis=0)
        t = i * tc + lax.broadcasted_iota(jnp.int32, (tc + H, LANES), 0)
        cnt = jnp.minimum(t + 1, w).astype(F32)
        s = _pool_window(ext / cnt, w, True)[:tc]
        o_ref[...] = (s - dp).astype(BF)

    r = tc // H
    nh = S // H - 1
    return _call("pool_bwd", body, (S // tc, PW // LANES),
                 [pl.BlockSpec((tc, LANES), lambda i, j: (i, j)),
                  pl.BlockSpec((H, LANES), lambda i, j: (jnp.minimum((i + 1) * r, nh), j))],
                 [pl.BlockSpec((tc, LANES), lambda i, j: (i, j))], [_sds((S, PW), BF)], [dpooled, dpooled],
                 sem=("parallel", "parallel"))[0]


def _pool_mix(pooled, pm, scale):
    S, PW = pooled.shape
    gw = PW // 4
    ts = _pick(S, (1024,))

    def epi(p, e, o, rs):
        o[0][rs, :] = (p * e[0][...]).astype(BF)

    tile = pl.BlockSpec((ts, gw), lambda i, j, k: (i, j))
    return _mm("pool_mix", (S // ts, 4, 1), NN, pooled, tile,
               pm, pl.BlockSpec((None, gw, gw), lambda i, j, k: (j, 0, 0)),
               [scale], [pl.BlockSpec((1, gw), lambda i, j, k: (0, j))], [_sds((S, PW), BF)], [tile], None, epi)[0]


def _pool_mix_bwd(pooled, pm, scale, dmixed):
    S, PW = pooled.shape
    gw = PW // 4
    ts = _pick(S, (1024,))

    def body(p_ref, pm_ref, sc_ref, dm_ref, dp_ref, dpm_ref, dsc_ref):
        i = pl.program_id(1)
        p = p_ref[...]
        w = pm_ref[...]
        dm = dm_ref[...]
        pre = jnp.dot(p, w, preferred_element_type=F32)
        dmp = (dm * sc_ref[...]).astype(BF)
        dp_ref[...] = lax.dot_general(dmp, w, NT, preferred_element_type=F32)
        dw = lax.dot_general(p, dmp, TN, preferred_element_type=F32)
        ds = jnp.concatenate([jnp.sum(dm * pre, axis=0, keepdims=True), jnp.zeros((7, gw), F32)], axis=0)
        _acc_rows(dpm_ref, i == 0, dw)
        _acc_rows(dsc_ref, i == 0, ds)

    tile = pl.BlockSpec((ts, gw), lambda g, i: (i, g))
    return _call("pool_mix_bwd", body, (4, S // ts),
                 [tile, pl.BlockSpec((None, gw, gw), lambda g, i: (g, 0, 0)), pl.BlockSpec((1, gw), lambda g, i: (0, g)), tile],
                 [tile, pl.BlockSpec((None, gw, gw), lambda g, i: (g, 0, 0)), pl.BlockSpec((8, gw), lambda g, i: (0, g))],
                 [_sds((S, PW), F32), _sds((4, gw, gw), F32), _sds((8, PW), F32)], [pooled, pm, scale, dmixed],
                 sem=("parallel", "arbitrary"))


def _bucket_onehot():
    ql = np.arange(BLK)[:, None]
    j = np.arange(2 * BLK)[None, :]
    d = BLK + ql - j
    n = np.clip(d, 0, None)
    nf = np.maximum(n, 1).astype(np.float32)
    max_exact = NUM_BUCKETS // 2
    large = max_exact + (np.log(nf / max_exact) / np.log(BLK / max_exact) * (NUM_BUCKETS - max_exact)).astype(np.int32)
    large = np.minimum(large, NUM_BUCKETS - 1)
    bucket = np.where(n < max_exact, n, large).astype(np.int32)
    valid = (d >= 0) & (d < BLK)
    oh = (bucket[None] == np.arange(NUM_BUCKETS)[:, None, None]) & valid[None]
    return oh.reshape(NUM_BUCKETS, BLK * 2 * BLK)


def _three_bf16(v):
    hi = v.astype(BF)
    r1 = v - hi.astype(F32)
    mid = r1.astype(BF)
    lo = (r1 - mid.astype(F32)).astype(BF)
    return hi, mid, lo


def _bias_table(rel_bias):
    oh = jnp.asarray(_bucket_onehot(), BF)
    tn = 4096

    def body(rb_ref, oh_ref, o_ref):
        o = oh_ref[...]
        hi, mid, lo = _three_bf16(rb_ref[...])
        acc = lax.dot_general(hi, o, TN, preferred_element_type=F32)
        acc = acc + lax.dot_general(mid, o, TN, preferred_element_type=F32)
        acc = acc + lax.dot_general(lo, o, TN, preferred_element_type=F32)
        on_band = jnp.sum(o.astype(F32), axis=0, keepdims=True) > 0.5
        o_ref[...] = jnp.where(on_band, acc, NEG_INF)

    n = oh.shape[1]
    return _call("bias_table", body, (n // tn,),
                 [pl.BlockSpec((NUM_BUCKETS, N_HEADS), lambda i: (0, 0)), pl.BlockSpec((NUM_BUCKETS, tn), lambda i: (0, i))],
                 [pl.BlockSpec((N_HEADS, tn), lambda i: (0, i))], [_sds((N_HEADS, n), F32)], [rel_bias, oh], sem=("parallel",))[0]


def _rel_bias_grad(dl):
    oh = jnp.asarray(_bucket_onehot(), BF)
    n = oh.shape[1]
    tk = 4096

    def body(dl_ref, oh_ref, o_ref):
        o = oh_ref[...]
        hi, mid, lo = _three_bf16(dl_ref[...])
        acc = lax.dot_general(o, hi, NT, preferred_element_type=F32)
        acc = acc + lax.dot_general(o, mid, NT, preferred_element_type=F32)
        acc = acc + lax.dot_general(o, lo, NT, preferred_element_type=F32)
        _acc_rows(o_ref, pl.program_id(0) == 0, acc)

    return _call("rel_bias_grad", body, (n // tk,),
                 [pl.BlockSpec((N_HEADS, tk), lambda i: (0, i)), pl.BlockSpec((NUM_BUCKETS, tk), lambda i: (0, i))],
                 [pl.BlockSpec((NUM_BUCKETS, N_HEADS), lambda i: (0, 0))], [_sds((NUM_BUCKETS, N_HEADS), F32)], [dl, oh],
                 sem=("arbitrary",))[0]


def _lo_half(shape):
    return lax.broadcasted_iota(jnp.int32, shape, 1) < HEAD_DIM


def _half_sum(x, lo):
    s_lo = jnp.sum(jnp.where(lo, x, 0.0), axis=-1, keepdims=True)
    s_hi = jnp.sum(jnp.where(lo, 0.0, x), axis=-1, keepdims=True)
    return jnp.where(lo, s_lo, s_hi)


def _norm2(x, lo):
    r = lax.rsqrt(_half_sum(x * x, lo) * (1.0 / HEAD_DIM) + EPS)
    return x * r, r


def _norm2_bwd(dy, xhat, r, gain, lo):
    dxhat = dy * gain
    dx = r * (dxhat - xhat * (_half_sum(dxhat * xhat, lo) * (1.0 / HEAD_DIM)))
    return dx, dy * xhat


def _swap(x):
    return pltpu.roll(x, HEAD_DIM, axis=1)


def _pair_rows(x, kk):
    return jnp.concatenate([x, _swap(x)] if kk == 0 else [_swap(x), x], axis=0)


def _attn_probs(n, kk, jp0, npairs, zq_ref, K, qg, bias_ref, sink_ref):
    lo_q = _lo_half((BLK, LANES))
    rows, qhats, qrs = [], [], []
    for jp in range(jp0, jp0 + npairs):
        qhat, qr = _norm2(zq_ref[:, jp * LANES:(jp + 1) * LANES], lo_q)
        rows.append(_pair_rows(qhat * qg * (HEAD_DIM ** -0.5), kk))
        qhats.append(qhat)
        qrs.append(qr)
    Q = jnp.concatenate(rows, axis=0).astype(BF)
    cols = slice(jp0 * 2 * BLK, (jp0 + npairs) * 2 * BLK)
    l = lax.dot_general(K, Q, NT, preferred_element_type=F32) + bias_ref[kk, :, cols]
    l = jnp.concatenate([jnp.where(n == 0, NEG_INF, l[:BLK]), l[BLK:]], axis=0)
    sink = sink_ref[kk, :, cols]
    m = jnp.maximum(jnp.max(l, axis=0, keepdims=True), sink)
    e = jnp.exp(l - m)
    es = jnp.exp(sink - m)
    inv = 1.0 / (jnp.sum(e, axis=0, keepdims=True) + es)
    return Q, e * inv, es * inv, qhats, qrs


def _attn_specs(o_q, o_k):
    nq = o_q // 512
    nk = o_k // LANES
    prev = lambda n: (jnp.maximum(n - 1, 0), nk)
    prev_v = lambda n: (jnp.maximum(n - 1, 0), nk + 1)
    return [pl.BlockSpec((BLK, 512), lambda n: (n, nq)), pl.BlockSpec((BLK, 512), lambda n: (n, nq + 1)),
            pl.BlockSpec((BLK, LANES), prev), pl.BlockSpec((BLK, LANES), lambda n: (n, nk)),
            pl.BlockSpec((BLK, LANES), prev_v), pl.BlockSpec((BLK, LANES), lambda n: (n, nk + 1)),
            pl.BlockSpec((1, LANES), lambda n: (0, 0)), pl.BlockSpec((1, LANES), lambda n: (0, 0)),
            pl.BlockSpec((N_KV, 1, 8 * BLK), lambda n: (0, 0, 0)),
            pl.BlockSpec((N_KV, 2 * BLK, 8 * BLK), lambda n: (0, 0, 0))]


def _attn_fwd(z, o_q, o_k, qg2, kg2, sink_rows, bias, plan=None):
    S = z.shape[0]

    def body(zq0, zq1, zkp, zkc, zvp, zvc, qg_ref, kg_ref, sink_ref, bias_ref, o_ref):
        n = pl.program_id(0)
        lo_k = _lo_half((2 * BLK, LANES))
        lo_q = _lo_half((BLK, LANES))
        khat, _ = _norm2(jnp.concatenate([zkp[...], zkc[...]], axis=0), lo_k)
        kn = khat * kg_ref[...]
        vb = jnp.concatenate([zvp[...], zvc[...]], axis=0).astype(BF)
        for kk, zq in enumerate((zq0, zq1)):
            K = jnp.where(lo_k if kk == 0 else jnp.logical_not(lo_k), kn, 0.0).astype(BF)
            for jp in range(4):
                _, p, _, _, _ = _attn_probs(n, kk, jp, 1, zq, K, qg_ref[...], bias_ref, sink_ref)
                r = lax.dot_general(p.astype(BF), vb, TN, preferred_element_type=F32)
                ev, od = r[:BLK], r[BLK:]
                pair = jnp.where(lo_q, ev, _swap(od)) if kk == 0 else jnp.where(lo_q, _swap(ev), od)
                c0 = (4 * kk + jp) * LANES
                o_ref[:, c0:c0 + LANES] = pair.astype(BF)

    return _call("attn_fwd", body, (S // BLK,), _attn_specs(o_q, o_k), [pl.BlockSpec((BLK, ATT_W), lambda n: (n, 0))],
                 [_sds((S, ATT_W), BF)], [z, z, z, z, z, z, qg2, kg2, sink_rows, bias], sem=("parallel",), plan=plan)[0]


def _attn_bwd(z, o_q, o_k, qg2, kg2, sink_rows, bias, dout, plan=None):
    S = z.shape[0]

    def body(zq0, zq1, zkp, zkc, zvp, zvc, qg_ref, kg_ref, sink_ref, bias_ref, do_ref,
             dq_ref, dkp_ref, dkc_ref, dvp_ref, dvc_ref, dl_ref, dsink_ref, dgain_ref):
        n = pl.program_id(0)
        lo_k = _lo_half((2 * BLK, LANES))
        lo_q = _lo_half((BLK, LANES))
        qg = qg_ref[...]
        kg = kg_ref[...]
        khat, kr = _norm2(jnp.concatenate([zkp[...], zkc[...]], axis=0), lo_k)
        kn = khat * kg
        vf = jnp.concatenate([zvp[...], zvc[...]], axis=0)

        @pl.when(n == 0)
        def _():
            dl_ref[...] = jnp.zeros_like(dl_ref)
            dsink_ref[...] = jnp.zeros_like(dsink_ref)
            dgain_ref[...] = jnp.zeros_like(dgain_ref)

        dkn = jnp.zeros((2 * BLK, LANES), F32)
        dvb = jnp.zeros((2 * BLK, LANES), F32)
        dqg = jnp.zeros((1, LANES), F32)
        for kk, zq in enumerate((zq0, zq1)):
            half_k = lo_k if kk == 0 else jnp.logical_not(lo_k)
            K = jnp.where(half_k, kn, 0.0).astype(BF)
            V = jnp.where(half_k, vf, 0.0).astype(BF)
            Q, p, ps, qhats, qrs = _attn_probs(n, kk, 0, 4, zq, K, qg, bias_ref, sink_ref)
            dO = jnp.concatenate([_pair_rows(do_ref[:, (4 * kk + jp) * LANES:(4 * kk + jp + 1) * LANES], kk) for jp in range(4)],
                                 axis=0).astype(BF)
            dP = lax.dot_general(V, dO, NT, preferred_element_type=F32)
            delta = jnp.sum(p * dP, axis=0, keepdims=True)
            dS = p * (dP - delta)
            dsink_ref[kk] += -ps * delta
            dl_ref[kk] += dS
            dSb = dS.astype(BF)
            dvb = dvb + jnp.where(half_k, jnp.dot(p.astype(BF), dO, preferred_element_type=F32), 0.0)
            dkn = dkn + jnp.where(half_k, jnp.dot(dSb, Q, preferred_element_type=F32), 0.0)
            dQ = lax.dot_general(dSb, K, TN, preferred_element_type=F32) * (HEAD_DIM ** -0.5)
            for jp in range(4):
                ev = dQ[(2 * jp) * BLK:(2 * jp + 1) * BLK]
                od = dQ[(2 * jp + 1) * BLK:(2 * jp + 2) * BLK]
                dy = (ev + _swap(od)) if kk == 0 else (_swap(ev) + od)
                dx, gq = _norm2_bwd(dy, qhats[jp], qrs[jp], qg, lo_q)
                dqg = dqg + jnp.sum(gq, axis=0, keepdims=True)
                c0 = (4 * kk + jp) * LANES
                dq_ref[:, c0:c0 + LANES] = dx.astype(BF)
        dk, gk = _norm2_bwd(dkn, khat, kr, kg, lo_k)
        dkp_ref[...] = dk[:BLK]
        dkc_ref[...] = dk[BLK:]
        dvp_ref[...] = dvb[:BLK]
        dvc_ref[...] = dvb[BLK:]
        dgain_ref[...] += jnp.concatenate([dqg, jnp.sum(gk, axis=0, keepdims=True), jnp.zeros((6, LANES), F32)], axis=0)

    blk = pl.BlockSpec((BLK, LANES), lambda n: (n, 0))
    wide = pl.BlockSpec((BLK, ATT_W), lambda n: (n, 0))
    return _call(
        "attn_bwd", body, (S // BLK,), _attn_specs(o_q, o_k) + [wide],
        [wide, blk, blk, blk, blk, pl.BlockSpec((N_KV, 2 * BLK, 8 * BLK), lambda n: (0, 0, 0)),
         pl.BlockSpec((N_KV, 1, 8 * BLK), lambda n: (0, 0, 0)), pl.BlockSpec((8, LANES), lambda n: (0, 0))],
        [_sds((S, ATT_W), BF), _sds((S, LANES), F32), _sds((S, LANES), F32), _sds((S, LANES), F32), _sds((S, LANES), F32),
         _sds((N_KV, 2 * BLK, 8 * BLK), F32), _sds((N_KV, 1, 8 * BLK), F32), _sds((8, LANES), F32)],
        [z, z, z, z, z, z, qg2, kg2, sink_rows, bias, dout], sem=("arbitrary",), plan=plan)


def _kv_combine(dkp, dkc, dvp, dvc):
    S = dkc.shape[0]
    last = S // BLK - 1

    def body(kp_ref, kc_ref, vp_ref, vc_ref, dk_ref, dv_ref):
        more = pl.program_id(0) < last
        dk_ref[...] = (kc_ref[...] + jnp.where(more, kp_ref[...], 0.0)).astype(BF)
        dv_ref[...] = (vc_ref[...] + jnp.where(more, vp_ref[...], 0.0)).astype(BF)

    cur = pl.BlockSpec((BLK, LANES), lambda n: (n, 0))
    nxt = pl.BlockSpec((BLK, LANES), lambda n: (jnp.minimum(n + 1, last), 0))
    return _call("kv_combine", body, (S // BLK,), [nxt, cur, nxt, cur], [cur, cur],
                 [_sds((S, LANES), BF), _sds((S, LANES), BF)], [dkp, dkc, dvp, dvc], sem=("parallel",))


def _merge_fwd(mixed, attn, wpu4, wau4, z, o_ga, plan=None):
    S, PW = mixed.shape
    _, _, CS = wpu4.shape
    D = 4 * CS
    tm = _pick(S, (1024,))
    tn = 256
    nsj = CS // tn
    na = o_ga // tn
    nb = (o_ga + D) // tn

    def body(m_ref, a_ref, wp_ref, wa_ref, ga_ref, gb_ref, mg_ref, yy_ref):
        yp = jnp.dot(m_ref[...], wp_ref[...], preferred_element_type=F32)
        ya = jnp.dot(a_ref[...], wa_ref[...], preferred_element_type=F32)
        mg_ref[...] = (jax.nn.sigmoid(ga_ref[...]) * yp + jax.nn.sigmoid(gb_ref[...]) * ya).astype(BF)
        yy_ref[0] = yp.astype(BF)
        yy_ref[1] = ya.astype(BF)

    return _call("merge_fwd", body, (S // tm, D // tn),
                 [pl.BlockSpec((tm, PW), lambda i, j: (i, 0)), pl.BlockSpec((tm, ATT_W), lambda i, j: (i, 0)),
                  pl.BlockSpec((None, PW, tn), lambda i, j: (j // nsj, 0, j % nsj)),
                  pl.BlockSpec((None, ATT_W, tn), lambda i, j: (j // nsj, 0, j % nsj)),
                  pl.BlockSpec((tm, tn), lambda i, j: (i, na + j)), pl.BlockSpec((tm, tn), lambda i, j: (i, nb + j))],
                 [pl.BlockSpec((tm, tn), lambda i, j: (i, j)), pl.BlockSpec((2, tm, tn), lambda i, j: (0, i, j))],
                 [_sds((S, D), BF), _sds((2, S, D), BF)], [mixed, attn, wpu4, wau4, z, z], sem=("parallel", "parallel"), plan=plan)


def _merge_bwd(do, wo, z, o_ga, yy, plan=None):
    S, D = do.shape
    tm = _pick(S, (1024,))
    tn = 256
    na = o_ga // tn
    nb = (o_ga + D) // tn

    def epi(p, e, o, rs):
        sa = jax.nn.sigmoid(e[0][rs, :])
        sb = jax.nn.sigmoid(e[1][rs, :])
        yp = e[2][0, rs, :].astype(F32)
        ya = e[2][1, rs, :].astype(F32)
        o[0][0, rs, :] = (p * yp * sa * (1.0 - sa)).astype(BF)
        o[0][1, rs, :] = (p * ya * sb * (1.0 - sb)).astype(BF)
        o[1][0, rs, :] = (p * sa).astype(BF)
        o[1][1, rs, :] = (p * sb).astype(BF)

    pair = pl.BlockSpec((2, tm, tn), lambda i, j, k: (0, i, j))
    return _mm("merge_bwd", (S // tm, D // tn, 1), NT,
               do, pl.BlockSpec((tm, D), lambda i, j, k: (i, 0)),
               wo, pl.BlockSpec((tn, D), lambda i, j, k: (j, 0)),
               [z, z, yy], [pl.BlockSpec((tm, tn), lambda i, j, k: (i, na + j)), pl.BlockSpec((tm, tn), lambda i, j, k: (i, nb + j)), pair],
               [_sds((2, S, D), BF), _sds((2, S, D), BF)], [pair, pair], None, epi, plan)


def _mm_up_t(name, dyy, which, w4):
    _, S, D = dyy.shape
    _, K, CS = w4.shape
    tm = _pick(S, (1024,))
    return _mm(name, (S // tm, 1, N_CHIPS), NT,
               dyy, pl.BlockSpec((None, tm, CS), lambda i, j, k: (which, i, k)),
               w4, pl.BlockSpec((None, K, CS), lambda i, j, k: (k, 0, 0)),
               [], [], [_sds((S, K), F32)], [pl.BlockSpec((tm, K), lambda i, j, k: (i, 0))], (tm, K), _store)[0]


def _mm_up_dw(name, a, dyy, which):
    _, S, D = dyy.shape
    K = a.shape[1]
    CS = D // N_CHIPS
    ts = _pick(S, (1024,))
    return _mm(name, (1, N_CHIPS, S // ts), TN,
               a, pl.BlockSpec((ts, K), lambda i, j, k: (k, 0)),
               dyy, pl.BlockSpec((None, ts, CS), lambda i, j, k: (which, k, j)),
               [], [], [_sds((N_CHIPS, K, CS), BF)], [pl.BlockSpec((None, K, CS), lambda i, j, k: (j, 0, 0))], (K, CS), _store)[0]


def _adamw(w, g, m, v):
    m = ADAM_B1 * m + (1.0 - ADAM_B1) * g
    v = ADAM_B2 * v + (1.0 - ADAM_B2) * (g * g)
    m_hat = m / (1.0 - ADAM_B1 ** ADAM_STEP)
    v_hat = v / (1.0 - ADAM_B2 ** ADAM_STEP)
    delta = -ADAM_LR * (m_hat / (jnp.sqrt(v_hat) + ADAM_EPS) + ADAM_WD * w)
    return delta, m, v


def _mod_fwd(c_all, w_ada, b_sh):
    D, cols = w_ada.shape
    tn = cols // 9

    def body(c_ref, w_ref, b_ref, o_ref):
        cv = c_ref[...]
        sc = (cv * jax.nn.sigmoid(cv)).astype(BF)
        o_ref[...] = jnp.dot(sc, w_ref[...].astype(BF), preferred_element_type=F32) + b_ref[...]

    return _call("mod_fwd", body, (9,),
                 [pl.BlockSpec((N_DEV, D), lambda j: (0, 0)), pl.BlockSpec((D, tn), lambda j: (0, j)), pl.BlockSpec((1, tn), lambda j: (0, j))],
                 [pl.BlockSpec((N_DEV, tn), lambda j: (0, j))], [_sds((N_DEV, cols), F32)], [c_all, w_ada, b_sh], sem=("parallel",))[0]


def _wada_bwd(c_all, dmod_sh, w, m, v, plan=None):
    D, cols = w.shape
    tn = cols // 18

    def body(c_ref, d_ref, w_ref, m_ref, v_ref, g_ref, dl_ref, nm_ref, nv_ref):
        cv = c_ref[...]
        sc = (cv * jax.nn.sigmoid(cv)).astype(BF)
        g = lax.dot_general(sc, d_ref[...].astype(BF), TN, preferred_element_type=F32)
        g_ref[...] = g
        dl_ref[...], nm_ref[...], nv_ref[...] = _adamw(w_ref[...], g, m_ref[...], v_ref[...])

    tile = pl.BlockSpec((D, tn), lambda j: (0, j))
    out = _sds((D, cols), F32)
    return _call("wada_bwd", body, (18,),
                 [pl.BlockSpec((N_DEV, D), lambda j: (0, 0)), pl.BlockSpec((N_DEV, tn), lambda j: (0, j)), tile, tile, tile],
                 [tile] * 4, [out] * 4, [c_all, dmod_sh, w, m, v], sem=("parallel",), plan=plan)


def _adam_2d(name, w, g, m, v):
    R, C = w.shape
    tr = _row_tile(R, 256)

    def body(w_ref, g_ref, m_ref, v_ref, dl_ref, nm_ref, nv_ref):
        dl_ref[...], nm_ref[...], nv_ref[...] = _adamw(w_ref[...], g_ref[...], m_ref[...], v_ref[...])

    tile = pl.BlockSpec((tr, C), lambda i: (i, 0))
    out = _sds((R, C), F32)
    return _call(name, body, (R // tr,), [tile] * 4, [tile] * 3, [out] * 3, [w, g, m, v], sem=("parallel",))


def _small_finish(parts, w, m, v):
    _, R, C = parts.shape

    def body(p_ref, w_ref, m_ref, v_ref, g_ref, dl_ref, nm_ref, nv_ref):
        g = p_ref[0]
        for d in range(1, N_DEV):
            g = g + p_ref[d]
        g_ref[...] = g
        dl_ref[...], nm_ref[...], nv_ref[...] = _adamw(w_ref[...], g, m_ref[...], v_ref[...])

    out = _sds((R, C), F32)
    return pl.pallas_call(body, out_shape=[out] * 4, name="small_finish",
                          compiler_params=pltpu.CompilerParams(vmem_limit_bytes=VMEM_LIMIT))(parts, w, m, v)


def _my_chip():
    return 2 * lax.axis_index("x") + lax.axis_index("y")


def _cast_into_slot(name, w):
    R, C = w.shape
    tr = _row_tile(R, 256)

    def body(w_ref, o_ref):
        o_ref[...] = w_ref[...].astype(BF)

    return _call(name, body, (R // tr,), [pl.BlockSpec((tr, C), lambda i: (i, 0))],
                 [pl.BlockSpec((None, tr, C), lambda i: (_my_chip(), i, 0))], [_sds((N_CHIPS, R, C), BF)], [w], sem=("parallel",))[0]


def _add_pair(name, p, q):
    _, H, C = q.shape
    tr = _row_tile(H, 512)
    nt = H // tr

    def body(p_ref, q_ref, o_ref):
        o_ref[...] = (p_ref[...].astype(F32) + q_ref[...].astype(F32)).astype(BF)

    tile = pl.BlockSpec((None, tr, C), lambda k, i: (k, i, 0))
    return _call(name, body, (N_CHIPS, nt), [pl.BlockSpec((None, tr, C), lambda k, i: (k, lax.axis_index("c") * nt + i, 0)), tile],
                 [tile], [_sds(q.shape, BF)], [p, q], sem=("parallel", "parallel"))[0]


def _sum_chips(name, u, t):
    _, H, C = u.shape
    tr = _row_tile(H, 256)

    def body(u_ref, t_ref, o_ref):
        r = _my_chip()
        own = t_ref[...].astype(F32)
        pick = lambda k: jnp.where(r == k, own, u_ref[k].astype(F32))
        o_ref[...] = ((pick(0) + pick(1)) + pick(2)) + pick(3)

    return _call(name, body, (H // tr,),
                 [pl.BlockSpec((N_CHIPS, tr, C), lambda i: (0, i, 0)), pl.BlockSpec((None, tr, C), lambda i: (_my_chip(), i, 0))],
                 [pl.BlockSpec((None, tr, C), lambda i: (lax.axis_index("c"), i, 0))], [_sds((2, H, C), F32)], [u, t],
                 sem=("parallel",))[0]


BIG = ("gu1", "down1", "w_in", "pool_mix", "pool_up", "attn_up", "o", "gu2", "down2")
MIX = ("o", "pool_up", "attn_up", "pool_mix")
EARLY = ("down1", "w_in", "pool_mix", "pool_up", "attn_up", "o")

SCHEDULE = {
    "rms_mod_fwd1": ([("ici", ("gu1",))], []),
    "+gather_gu1_d2d": ([("d2d", ("gu1",))], []),
    "ffn1_up": ([("ici", EARLY)], []),
    "+gather_early_d2d": ([("d2d", EARLY[:1])], []),
    "ffn1_down": ([("ici", ("gu2",)), ("d2d", EARLY[1:])], []),
    "mix_in": ([("ici", ("down2",)), ("d2d", ("gu2",))], []),
    "mix_out": ([("d2d", ("down2",))], []),
    "ffn2_dwd": ([("split", ("gu2",))], [("add", ("gu2",))]),
    "ffn2_dh": ([("owners", ("gu2",)), ("split", ("down2",))], [("add", ("down2",)), ("sum", ("gu2",))]),
    "merge_bwd": ([("owners", ("down2",)), ("join", ("gu2",))], [("sum", ("down2",)), ("adam", ("gu2",))]),
    "attn_bwd": ([("split", MIX), ("join", ("down2",))], [("add", MIX), ("adam", ("down2",))]),
    "mix_dwin": ([("owners", MIX)], [("sum", MIX)]),
    "mix_dh": ([("split", ("w_in",)), ("join", MIX)], [("add", ("w_in",)), ("adam", MIX)]),
    "ffn1_dact": ([("owners", ("w_in",))], [("sum", ("w_in",))]),
    "ffn1_dwd": ([("split", ("gu1",)), ("join", ("w_in",))], [("add", ("gu1",)), ("adam", ("w_in",))]),
    "ffn1_dh": ([("owners", ("gu1",)), ("split", ("down1",))], [("add", ("down1",)), ("sum", ("gu1",))]),
    "rms_mod_bwd1": ([("owners", ("down1",)), ("join", ("gu1",))], [("sum", ("down1",)), ("adam", ("gu1",))]),
    "+join_down1": ([("join", ("down1",))], [("adam", ("down1",))]),
}


class _Plan:
    def __init__(self, w2, m2, v2, full, D, gw):
        self.w2, self.m2, self.v2, self.full, self.D, self.gw = w2, m2, v2, dict(full), D, gw
        self.part, self.got, self.sums, self.landed, self.g = {}, {}, {}, {}, {}
        self.result = {}
        self.pending = {}

    def _make(self, op, names):
        if op == "ici":
            return _gather_ici_stage([self.full[k] for k in names])
        if op == "d2d":
            return _gather_d2d_stage([self.full[k] for k in names])
        if op == "split":
            return _split_stage([self.part[k] for k in names])
        if op == "owners":
            return _owners_stage([self.sums[k] for k in names])
        return _join_stage([self.g[k] for k in names])

    def stages(self, name):
        ops = SCHEDULE.get(name, ([], []))[0]
        return [self._make(*op) for op in ops]

    def done(self, name, outs):
        ops, local = SCHEDULE[name]
        for op, res in zip(ops, outs):
            store = {"ici": self.full, "d2d": self.full, "split": self.got, "owners": self.landed, "join": self.g}[op[0]]
            store.update(zip(op[1], res))
        for op, names in local:
            for k in names:
                if op == "add":
                    self.sums[k] = _add_pair("add_pair_" + k, self.part[k], self.got[k])
                elif op == "sum":
                    self.g[k] = _sum_chips("sum_chips_" + k, self.landed[k], self.sums[k])
                else:
                    g2 = self.g[k].reshape(self.w2[k].shape)
                    self.result[k] = (g2, *_adam_2d("adam_" + k, self.w2[k], g2, self.m2[k], self.v2[k]))

    def alone(self, name):
        self.done(name, _run_stages(name[1:], self.stages(name)))

    def weight(self, k):
        D, gw, f = self.D, self.gw, self.full[k]
        if k in ("down1", "down2", "o"):
            return f.reshape(-1, D)
        if k == "pool_mix":
            return f.reshape(N_CHIPS, 4, gw // N_CHIPS, gw).transpose(1, 0, 2, 3).reshape(4, gw, gw)
        if k == "w_in":
            return f.reshape(-1, D)
        return f

    def partial(self, k, p):
        D, gw = self.D, self.gw
        if k in ("down1", "down2", "o", "w_in"):
            p = p.reshape(N_CHIPS, p.shape[0] // N_CHIPS, p.shape[1])
        elif k == "pool_mix":
            p = p.astype(BF).reshape(4, N_CHIPS, gw // N_CHIPS, gw).transpose(1, 0, 2, 3).reshape(N_CHIPS, gw, gw)
        self.part[k] = p


class _NoComm:
    def __init__(self, weights):
        self.w, self.part = weights, {}

    def stages(self, name):
        return []

    def alone(self, name):
        pass

    def weight(self, k):
        return self.w[k]

    def partial(self, k, p):
        self.part[k] = p


def _row(a, i):
    return a[i:i + 1]


def _local_step(x, target, mod, g_ffn1, g_mix, g_ffn2, pool_scale, q_gain, k_gain, sinks, rel_bias, plan):
    S, D = x.shape
    half = 0.5 * mod
    tile2 = lambda g: jnp.concatenate([g, g], axis=1)
    qg2, kg2 = tile2(q_gain), tile2(k_gain)
    sink_rows = jnp.broadcast_to(sinks.reshape(N_KV, 1, 8, 1), (N_KV, 1, 8, BLK)).reshape(N_KV, 1, 8 * BLK)
    bias = _bias_table(rel_bias).reshape(N_KV, 8, BLK, 2 * BLK).transpose(0, 3, 1, 2).reshape(N_KV, 2 * BLK, 8 * BLK)

    h1 = _rms_mod_fwd("rms_mod_fwd1", x, g_ffn1, _row(mod, 0), _row(mod, 1), plan)
    plan.alone("+gather_gu1_d2d")
    gu1, act1 = _ffn_up("ffn1_up", h1, plan.weight("gu1"), plan)
    plan.alone("+gather_early_d2d")
    x1, f1 = _mm_residual("ffn1_down", act1, plan.weight("down1"), x, _row(half, 2), plan)
    h2 = _rms_mod_fwd("rms_mod_fwd2", x1, g_mix, _row(mod, 3), _row(mod, 4))
    w_in_t = plan.weight("w_in")
    IN_W = w_in_t.shape[0]
    PW = plan.weight("pool_up").shape[1]
    o_q, o_k = PW, PW + ATT_W
    o_ga = o_k + 2 * KV_W
    tnz = _pick(IN_W, (1280, 256))
    tmz = _pick(S, (1024,))
    z = _mm("mix_in", (S // tmz, IN_W // tnz, 1), NT, h2, pl.BlockSpec((tmz, D), lambda i, j, k: (i, 0)),
            w_in_t, pl.BlockSpec((tnz, D), lambda i, j, k: (j, 0)), [], [], [_sds((S, IN_W), F32)],
            [pl.BlockSpec((tmz, tnz), lambda i, j, k: (i, j))], None, _store, plan)[0]
    pooled = _pool_fwd(z, PW)
    mixed = _pool_mix(pooled, plan.weight("pool_mix"), pool_scale)
    attn = _attn_fwd(z, o_q, o_k, qg2, kg2, sink_rows, bias)
    merged, yy = _merge_fwd(mixed, attn, plan.weight("pool_up"), plan.weight("attn_up"), z, o_ga)
    x2, fo = _mm_residual("mix_out", merged, plan.weight("o"), x1, _row(mod, 5), plan)
    h3 = _rms_mod_fwd("rms_mod_fwd3", x2, g_ffn2, _row(mod, 6), _row(mod, 7))
    gu2, act2 = _ffn_up("ffn2_up", h3, plan.weight("gu2"))
    x3, f2 = _mm_residual("ffn2_down", act2, plan.weight("down2"), x2, _row(half, 8))
    dx3, df2, loss_acc = _loss_bwd(x3, target, f2, _row(half, 8))

    dgu2 = _ffn_dact("ffn2_dact", df2, plan.weight("down2"), gu2)
    plan.partial("gu2", _ffn_dwgu("ffn2_dwgu", h3, dgu2))
    plan.partial("down2", _mm_tn("ffn2_dwd", act2, df2, (1408, 512), (512,), plan))
    dh3 = _ffn_dh("ffn2_dh", dgu2, plan.weight("gu2"), plan)
    dx2, do, acc3 = _rms_mod_bwd("rms_mod_bwd3", dh3, x2, dx3, g_ffn2, _row(mod, 7), fo, _row(mod, 5))

    dgab, dyy = _merge_bwd(do, plan.weight("o"), z, o_ga, yy, plan)
    plan.partial("o", _mm_tn("mix_dwo", merged, do, (1024,), (512,), plan))
    dmixed = _mm_up_t("pool_up_t", dyy, 0, plan.weight("pool_up"))
    dattn = _mm_up_t("attn_up_t", dyy, 1, plan.weight("attn_up"))
    plan.partial("pool_up", _mm_up_dw("pool_up_dw", mixed, dyy, 0))
    plan.partial("attn_up", _mm_up_dw("attn_up_dw", attn, dyy, 1))
    dpooled, dpm, dps = _pool_mix_bwd(pooled, plan.weight("pool_mix"), pool_scale, dmixed)
    plan.partial("pool_mix", dpm)
    du_pool = _pool_bwd(dpooled)
    dq, dkp, dkc, dvp, dvc, dl, dsink, dgain = _attn_bwd(z, o_q, o_k, qg2, kg2, sink_rows, bias, dattn, plan)
    dk, dv = _kv_combine(dkp, dkc, dvp, dvc)
    drb = _rel_bias_grad(dl.reshape(N_KV, 2 * BLK, 8, BLK).transpose(0, 2, 3, 1).reshape(N_HEADS, BLK * 2 * BLK))
    dz = jnp.concatenate([du_pool, dq, dk, dv, dgab[0], dgab[1]], axis=1)
    plan.partial("w_in", _mm_tn("mix_dwin", dz, h2, (1280, 256), (512,), plan))
    tnd = _pick(D, (512,))
    dh2 = _mm("mix_dh", (S // tmz, D // tnd, 1), NN, dz, pl.BlockSpec((tmz, IN_W), lambda i, j, k: (i, 0)),
              w_in_t, pl.BlockSpec((IN_W, tnd), lambda i, j, k: (0, j)), [], [], [_sds((S, D), F32)],
              [pl.BlockSpec((tmz, tnd), lambda i, j, k: (i, j))], None, _store, plan)[0]
    dx1, df1, acc2 = _rms_mod_bwd("rms_mod_bwd2", dh2, x1, dx2, g_mix, _row(mod, 4), f1, _row(half, 2))

    dgu1 =_ffn_dact("ffn1_dact", df1, plan.weight("down1"), gu1, plan)
    plan.partial("gu1", _ffn_dwgu("ffn1_dwgu", h1, dgu1, plan))
    plan.partial("down1", _mm_tn("ffn1_dwd", act1, df1, (1408, 512), (512,), plan))
    dh1 = _ffn_dh("ffn1_dh", dgu1, plan.weight("gu1"), plan)
    grad_x, acc1 = _rms_mod_bwd("rms_mod_bwd1", dh1, x, dx1, g_ffn1, _row(mod, 1), plan=plan)

    dmod = jnp.concatenate([_row(acc1, 0), _row(acc1, 1), 0.5 * _row(acc2, 3),
                            _row(acc2, 0), _row(acc2, 1), _row(acc3, 3),
                            _row(acc3, 0), _row(acc3, 1), 0.5 * _row(loss_acc, 1)], axis=0)
    fold = lambda r: r[:, :HEAD_DIM] + r[:, HEAD_DIM:]
    small = dict(
        dmod=dmod, g_ffn1=_row(acc1, 2), g_mix=_row(acc2, 2), g_ffn2=_row(acc3, 2), pool_scale=_row(dps, 0),
        q_gain=fold(_row(dgain, 0)), k_gain=fold(_row(dgain, 1)),
        sinks=jnp.sum(dsink.reshape(N_HEADS, BLK), axis=1).reshape(1, N_HEADS), rel_bias=drb,
        loss=(0.5 / D) * jnp.sum(_row(loss_acc, 0)).reshape(1, 1))
    return grad_x, small


SMALL_ORDER = ("dmod", "g_ffn1", "g_mix", "g_ffn2", "pool_scale", "q_gain", "k_gain", "sinks", "rel_bias", "loss")


def _pack_small(vals):
    flat = jnp.concatenate([vals[k].reshape(-1) for k in SMALL_ORDER])
    n = flat.shape[0]
    rows = -(-n // (8 * LANES)) * 8
    return jnp.pad(flat, (0, rows * LANES - n)).reshape(rows, LANES)


def _unpack_small(packed, like):
    flat = packed.reshape(-1)
    out, off = {}, 0
    for k in SMALL_ORDER:
        n = int(np.prod(like[k].shape))
        out[k] = flat[off:off + n].reshape(like[k].shape)
        off += n
    return out


def kernel(x, c, w_ada, b_ada, g_ffn1, w_ffn1_gu, w_ffn1_down, g_mix, w_in, pool_mix, pool_scale, w_pool_up, q_gain, k_gain, sinks, rel_bias, w_attn_up, w_o, g_ffn2, w_ffn2_gu, w_ffn2_down, loss_target, m_w_ada, m_b_ada, m_g_ffn1, m_w_ffn1_gu, m_w_ffn1_down, m_g_mix, m_w_in, m_pool_mix, m_pool_scale, m_w_pool_up, m_q_gain, m_k_gain, m_sinks, m_rel_bias, m_w_attn_up, m_w_o, m_g_ffn2, m_w_ffn2_gu, m_w_ffn2_down, v_w_ada, v_b_ada, v_g_ffn1, v_w_ffn1_gu, v_w_ffn1_down, v_g_mix, v_w_in, v_pool_mix, v_pool_scale, v_w_pool_up, v_q_gain, v_k_gain, v_sinks, v_rel_bias, v_w_attn_up, v_w_o, v_g_ffn2, v_w_ffn2_gu, v_w_ffn2_down):
    S, D = x.shape[1], x.shape[2]
    gw = pool_mix.shape[3]
    r = 2 * lax.axis_index("x") + lax.axis_index("y")

    two_d = lambda a: a.reshape(-1, a.shape[-1])
    w_sh = dict(gu1=w_ffn1_gu, down1=w_ffn1_down, w_in=w_in, pool_mix=pool_mix, pool_up=w_pool_up, attn_up=w_attn_up, o=w_o,
                gu2=w_ffn2_gu, down2=w_ffn2_down)
    m_sh = dict(gu1=m_w_ffn1_gu, down1=m_w_ffn1_down, w_in=m_w_in, pool_mix=m_pool_mix, pool_up=m_w_pool_up, attn_up=m_w_attn_up,
                o=m_w_o, gu2=m_w_ffn2_gu, down2=m_w_ffn2_down)
    v_sh = dict(gu1=v_w_ffn1_gu, down1=v_w_ffn1_down, w_in=v_w_in, pool_mix=v_pool_mix, pool_up=v_w_pool_up, attn_up=v_w_attn_up,
                o=v_w_o, gu2=v_w_ffn2_gu, down2=v_w_ffn2_down)
    view = lambda k, a: two_d(a).T if k == "w_in" else two_d(a)
    unview = lambda k, a: (a.T if k == "w_in" else a).reshape(w_sh[k].shape)
    w2 = {k: view(k, w_sh[k]) for k in BIG}
    full = {k: _cast_into_slot("cast_" + k, w2[k]) for k in BIG}
    plan = _Plan(w2, {k: view(k, m_sh[k]) for k in BIG}, {k: view(k, v_sh[k]) for k in BIG}, full, D, gw)

    c_all = _gather_all("gather_c", jnp.broadcast_to(c, (8, D)))[:, 0, :]
    cols = w_ada.shape[2]
    b_sh = lax.dynamic_slice(b_ada, (0, r * cols), (1, cols))
    mod_cols = _mod_fwd(c_all, w_ada[0], b_sh)
    mod_all = _chip_exchange("mod_exchange", mod_cols)
    me = 4 * lax.axis_index("x") + 2 * lax.axis_index("y") + lax.axis_index("c")
    mod = lax.dynamic_slice(mod_all, (0, me, 0), (N_CHIPS, 1, cols)).reshape(9, D)

    grad_x, small = _local_step(x[0], loss_target[0], mod, g_ffn1, g_mix, g_ffn2, pool_scale, q_gain, k_gain,
                                sinks, rel_bias, plan)

    small_w = dict(dmod=b_ada, g_ffn1=g_ffn1, g_mix=g_mix, g_ffn2=g_ffn2, pool_scale=pool_scale, q_gain=q_gain, k_gain=k_gain,
                   sinks=sinks, rel_bias=rel_bias, loss=jnp.zeros((1, 1), F32))
    small_m = dict(dmod=m_b_ada, g_ffn1=m_g_ffn1, g_mix=m_g_mix, g_ffn2=m_g_ffn2, pool_scale=m_pool_scale, q_gain=m_q_gain,
                   k_gain=m_k_gain, sinks=m_sinks, rel_bias=m_rel_bias, loss=jnp.zeros((1, 1), F32))
    small_v = dict(dmod=v_b_ada, g_ffn1=v_g_ffn1, g_mix=v_g_mix, g_ffn2=v_g_ffn2, pool_scale=v_pool_scale, q_gain=v_q_gain,
                   k_gain=v_k_gain, sinks=v_sinks, rel_bias=v_rel_bias, loss=jnp.ones((1, 1), F32))
    small_all = _gather_all("gather_small", _pack_small(small))
    sg, sd, sm, sv = [_unpack_small(a, small_w) for a in
                      _small_finish(small_all, _pack_small(small_w), _pack_small(small_m), _pack_small(small_v))]
    loss = sg["loss"].reshape(())

    dmod_all = small_all.reshape(N_DEV, -1)[:, :9 * D]
    dmod_sh = lax.dynamic_slice(dmod_all, (0, r * cols), (N_DEV, cols))
    g_ada, d_ada, nm_ada, nv_ada = _wada_bwd(c_all, dmod_sh, w_ada[0], m_w_ada[0], v_w_ada[0])
    plan.alone("+join_down1")

    big = [{k: unview(k, plan.result[k][i]) for k in BIG} for i in range(4)]

    def ordered(b, ada, sm_):
        return (ada[None], sm_["dmod"], sm_["g_ffn1"], b["gu1"], b["down1"], sm_["g_mix"], b["w_in"], b["pool_mix"],
                sm_["pool_scale"], b["pool_up"], sm_["q_gain"], sm_["k_gain"], sm_["sinks"], sm_["rel_bias"], b["attn_up"],
                b["o"], sm_["g_ffn2"], b["gu2"], b["down2"])

    return (loss, grad_x[None], *ordered(big[0], g_ada, sg), *ordered(big[1], d_ada, sd), *ordered(big[2], nm_ada, sm),
            *ordered(big[3], nv_ada, sv))
```

```python
import numpy as np
import jax
import jax.numpy as jnp
from jax import lax
from jax.experimental import pallas as pl
from jax.experimental.pallas import tpu as pltpu

BF = jnp.bfloat16
F32 = jnp.float32
MESH = pl.DeviceIdType.MESH

EPS = 1e-6
NEG_INF = -1e30
HEAD_DIM = 64
N_HEADS = 16
N_KV = 2
ATT_W = N_HEADS * HEAD_DIM
KV_W = N_KV * HEAD_DIM
BLK = 128
NUM_BUCKETS = 32
POOL_MAX_W = 16
N_CHIPS = 4
N_DEV = 8
LANES = 128
ADAM_LR, ADAM_B1, ADAM_B2, ADAM_EPS, ADAM_WD, ADAM_STEP = 0.001, 0.9, 0.999, 1e-08, 0.01, 10
VMEM_LIMIT = 52 * 1024 * 1024
ANY = pl.BlockSpec(memory_space=pl.ANY)


def _pick(dim, prefs):
    for p in prefs:
        if p <= dim and dim % p == 0:
            return p
    return dim


def _row_tile(rows, cap):
    return max(t for t in range(16, min(rows, cap) + 1, 16) if rows % t == 0)


def _sds(shape, dtype):
    return jax.ShapeDtypeStruct(tuple(shape), dtype)


def _place():
    return lax.axis_index("x"), lax.axis_index("y"), lax.axis_index("c")


def _other_chips(x, y):
    return [(1 - x, y), (x, 1 - y), (1 - x, 1 - y)]


def _chip_of(chip):
    return 2 * chip[0] + chip[1]


def _half_rows(ref, lead, cc, h):
    return ref.at[lead, pl.ds(pl.multiple_of(cc * h, 16), h), :]


class _Stage:
    def __init__(self, bufs, outs, alias, n_sem, start, wait):
        self.bufs, self.outs, self.alias, self.n_sem, self.start, self.wait = bufs, outs, alias, n_sem, start, wait


def _stage_plumbing(stages, n_in0, n_out0):
    bufs, outs, aliases, spans, scratch = [], [], {}, [], []
    for st in stages:
        i0, o0 = len(bufs), len(outs)
        bufs += list(st.bufs)
        outs += list(st.outs)
        for a, b in st.alias.items():
            aliases[n_in0 + i0 + a] = n_out0 + o0 + b
        spans.append((i0, len(bufs), o0, len(outs)))
        scratch += [pltpu.SemaphoreType.DMA((st.n_sem,)), pltpu.SemaphoreType.DMA((st.n_sem,))]

    def run(which, in_refs, out_refs, sem_refs):
        for s, st in enumerate(stages):
            i0, i1, o0, o1 = spans[s]
            getattr(st, which)(in_refs[i0:i1], out_refs[o0:o1], sem_refs[2 * s], sem_refs[2 * s + 1])

    def split(flat):
        return [list(flat[o0:o1]) for (_, _, o0, o1) in spans]

    return bufs, outs, aliases, scratch, run, split


def _run_stages(name, stages):
    bufs, outs, aliases, scratch, run, split = _stage_plumbing(stages, 0, 0)
    ni, no = len(bufs), len(outs)

    def body(*refs):
        ins, os_, sems = refs[:ni], refs[ni:ni + no], refs[ni + no:]
        run("start", ins, os_, sems)
        run("wait", ins, os_, sems)

    res = pl.pallas_call(body, in_specs=[ANY] * ni, out_specs=[ANY] * no, out_shape=outs, input_output_aliases=aliases,
                         scratch_shapes=scratch, name=name)(*bufs)
    return split(res)


def _call(name, body, grid, in_specs, out_specs, out_shape, args, scratch=(), sem=None, plan=None):
    stages = plan.stages(name) if plan is not None else []
    n_in, n_out, n_scr = len(args), len(out_shape), len(scratch)
    if not stages:
        return pl.pallas_call(body, grid=grid, in_specs=list(in_specs), out_specs=list(out_specs), out_shape=list(out_shape),
                              scratch_shapes=list(scratch), name=name,
                              compiler_params=pltpu.CompilerParams(dimension_semantics=sem, vmem_limit_bytes=VMEM_LIMIT))(*args)
    bufs, s_outs, aliases, s_scratch, run, split = _stage_plumbing(stages, n_in, n_out)
    nb, nso = len(bufs), len(s_outs)

    def hosted(*refs):
        ins = refs[:n_in]
        s_ins = refs[n_in:n_in + nb]
        outs = refs[n_in + nb:n_in + nb + n_out]
        s_os = refs[n_in + nb + n_out:n_in + nb + n_out + nso]
        scr = refs[n_in + nb + n_out + nso:n_in + nb + n_out + nso + n_scr]
        sems = refs[n_in + nb + n_out + nso + n_scr:]
        first = pl.program_id(0) == 0
        last = pl.program_id(0) == grid[0] - 1
        for d in range(1, len(grid)):
            first = first & (pl.program_id(d) == 0)
            last = last & (pl.program_id(d) == grid[d] - 1)

        @pl.when(first)
        def _():
            run("start", s_ins, s_os, sems)

        body(*ins, *outs, *scr)

        @pl.when(last)
        def _():
            run("wait", s_ins, s_os, sems)

    res = pl.pallas_call(
        hosted, grid=grid, in_specs=list(in_specs) + [ANY] * nb, out_specs=list(out_specs) + [ANY] * nso,
        out_shape=list(out_shape) + s_outs, input_output_aliases=aliases, scratch_shapes=list(scratch) + s_scratch, name=name,
        compiler_params=pltpu.CompilerParams(dimension_semantics=("arbitrary",) * len(grid), vmem_limit_bytes=VMEM_LIMIT))(*args, *bufs)
    plan.done(name, split(res[n_out:]))
    return list(res[:n_out])


def _gather_ici_stage(fulls):
    n = len(fulls)

    def copy(i, j, slot, ins, outs, send, recv):
        x, y, c = _place()
        chip = _other_chips(x, y)[j]
        h = fulls[i].shape[1] // 2
        s = 3 * i + j
        return pltpu.make_async_remote_copy(_half_rows(ins[i], 2 * x + y, c, h), _half_rows(outs[i], slot(x, y, chip), c, h),
                                            send.at[s], recv.at[s], device_id=(*chip, c), device_id_type=MESH)

    mine = lambda x, y, chip: 2 * x + y
    theirs = lambda x, y, chip: _chip_of(chip)

    def start(ins, outs, send, recv):
        for i in range(n):
            for j in range(3):
                copy(i, j, mine, ins, outs, send, recv).start()

    def wait(ins, outs, send, recv):
        for i in range(n):
            for j in range(3):
                copy(i, j, theirs, ins, outs, send, recv).wait_recv()
        for i in range(n):
            for j in range(3):
                copy(i, j, mine, ins, outs, send, recv).wait_send()

    return _Stage(fulls, [_sds(f.shape, f.dtype) for f in fulls], {i: i for i in range(n)}, 3 * n, start, wait)


def _gather_d2d_stage(fulls):
    n = len(fulls)

    def copy(i, j, cc, ins, outs, send, recv):
        x, y, c = _place()
        rj = _chip_of(_other_chips(x, y)[j])
        h = fulls[i].shape[1] // 2
        half = cc(c)
        s = 3 * i + j
        return pltpu.make_async_remote_copy(_half_rows(ins[i], rj, half, h), _half_rows(outs[i], rj, half, h),
                                            send.at[s], recv.at[s], device_id=(x, y, 1 - c), device_id_type=MESH)

    mine = lambda c: c
    theirs = lambda c: 1 - c

    def start(ins, outs, send, recv):
        for i in range(n):
            for j in range(3):
                copy(i, j, mine, ins, outs, send, recv).start()

    def wait(ins, outs, send, recv):
        for i in range(n):
            for j in range(3):
                copy(i, j, theirs, ins, outs, send, recv).wait_recv()
        for i in range(n):
            for j in range(3):
                copy(i, j, mine, ins, outs, send, recv).wait_send()

    return _Stage(fulls, [_sds(f.shape, f.dtype) for f in fulls], {i: i for i in range(n)}, 3 * n, start, wait)


def _split_stage(parts):
    n = len(parts)

    def copy(i, ins, outs, send, recv):
        x, y, c = _place()
        h = parts[i].shape[1] // 2
        return pltpu.make_async_remote_copy(_half_rows(ins[i], slice(None), 1 - c, h), outs[i], send.at[i], recv.at[i],
                                            device_id=(x, y, 1 - c), device_id_type=MESH)

    def start(ins, outs, send, recv):
        for i in range(n):
            copy(i, ins, outs, send, recv).start()

    def wait(ins, outs, send, recv):
        for i in range(n):
            copy(i, ins, outs, send, recv).wait_recv()
        for i in range(n):
            copy(i, ins, outs, send, recv).wait_send()

    return _Stage(parts, [_sds((N_CHIPS, p.shape[1] // 2, p.shape[2]), p.dtype) for p in parts], {}, n, start, wait)


def _owners_stage(sums):
    n = len(sums)

    def copy(i, j, mine, ins, outs, send, recv):
        x, y, c = _place()
        chip = _other_chips(x, y)[j]
        slot = (2 * x + y) if mine else _chip_of(chip)
        return pltpu.make_async_remote_copy(ins[i].at[_chip_of(chip)], outs[i].at[slot], send.at[3 * i + j], recv.at[3 * i + j],
                                            device_id=(*chip, c), device_id_type=MESH)

    def start(ins, outs, send, recv):
        for i in range(n):
            for j in range(3):
                copy(i, j, True, ins, outs, send, recv).start()

    def wait(ins, outs, send, recv):
        for i in range(n):
            for j in range(3):
                copy(i, j, False, ins, outs, send, recv).wait_recv()
        for i in range(n):
            for j in range(3):
                copy(i, j, True, ins, outs, send, recv).wait_send()

    return _Stage(sums, [_sds(s.shape, s.dtype) for s in sums], {}, 3 * n, start, wait)


def _join_stage(gs):
    n = len(gs)

    def copy(i, mine, ins, outs, send, recv):
        x, y, c = _place()
        slot = c if mine else 1 - c
        return pltpu.make_async_remote_copy(ins[i].at[slot], outs[i].at[slot], send.at[i], recv.at[i],
                                            device_id=(x, y, 1 - c), device_id_type=MESH)

    def start(ins, outs, send, recv):
        for i in range(n):
            copy(i, True, ins, outs, send, recv).start()

    def wait(ins, outs, send, recv):
        for i in range(n):
            copy(i, False, ins, outs, send, recv).wait_recv()
        for i in range(n):
            copy(i, True, ins, outs, send, recv).wait_send()

    return _Stage(gs, [_sds(g.shape, g.dtype) for g in gs], {i: i for i in range(n)}, n, start, wait)


def _chip_exchange(name, arr):
    def body(src, dst, send, recv, loc):
        x, y, c = _place()
        r = 2 * x + y
        chips = _other_chips(x, y)

        def cp(j, slot):
            return pltpu.make_async_remote_copy(src, dst.at[slot], send.at[j], recv.at[j], device_id=(*chips[j], c), device_id_type=MESH)

        mine = pltpu.make_async_copy(src, dst.at[r], loc)
        mine.start()
        for j in range(3):
            cp(j, r).start()
        for j in range(3):
            cp(j, _chip_of(chips[j])).wait_recv()
        for j in range(3):
            cp(j, r).wait_send()
        mine.wait()

    return pl.pallas_call(body, in_specs=[ANY], out_specs=ANY, out_shape=_sds((N_CHIPS, *arr.shape), arr.dtype),
                          scratch_shapes=[pltpu.SemaphoreType.DMA((3,)), pltpu.SemaphoreType.DMA((3,)), pltpu.SemaphoreType.DMA],
                          name=name)(arr)


def _gather_all(name, arr):
    def body(src, dst, send, recv, loc):
        x, y, c = _place()

        def cp(k, slot_of_me):
            px, py, pc = x ^ ((k >> 2) & 1), y ^ ((k >> 1) & 1), c ^ (k & 1)
            slot = (4 * x + 2 * y + c) if slot_of_me else (4 * px + 2 * py + pc)
            return pltpu.make_async_remote_copy(src, dst.at[slot], send.at[k - 1], recv.at[k - 1],
                                                device_id=(px, py, pc), device_id_type=MESH)

        mine = pltpu.make_async_copy(src, dst.at[4 * x + 2 * y + c], loc)
        mine.start()
        for k in range(1, N_DEV):
            cp(k, True).start()
        for k in range(1, N_DEV):
            cp(k, False).wait_recv()
        for k in range(1, N_DEV):
            cp(k, True).wait_send()
        mine.wait()

    return pl.pallas_call(body, in_specs=[ANY], out_specs=ANY, out_shape=_sds((N_DEV, *arr.shape), arr.dtype),
                          scratch_shapes=[pltpu.SemaphoreType.DMA((N_DEV - 1,)), pltpu.SemaphoreType.DMA((N_DEV - 1,)), pltpu.SemaphoreType.DMA],
                          name=name)(arr)


NN = (((1,), (0,)), ((), ()))
NT = (((1,), (1,)), ((), ()))
TN = (((0,), (0,)), ((), ()))


ALL = slice(None)


def _mm(name, grid, dims, a, a_spec, b, b_spec, extras, extra_specs, out_shapes, out_specs, acc_shape, epilogue, plan=None):
    n_k = grid[2]
    n_e = len(extras)
    n_o = len(out_shapes)

    def body(*refs):
        a_ref, b_ref = refs[0], refs[1]
        e_refs = refs[2:2 + n_e]
        o_refs = refs[2 + n_e:2 + n_e + n_o]
        p = lax.dot_general(a_ref[...].astype(BF), b_ref[...].astype(BF), dims, preferred_element_type=F32)
        if n_k == 1:
            epilogue(p, e_refs, o_refs, ALL)
        else:
            acc = refs[-1]
            k = pl.program_id(2)

            @pl.when(k == 0)
            def _():
                acc[...] = p

            @pl.when(k > 0)
            def _():
                acc[...] += p

            @pl.when(k == n_k - 1)
            def _():
                epilogue(acc[...], e_refs, o_refs, ALL)

    scratch = [] if n_k == 1 else [pltpu.VMEM(acc_shape, F32)]
    return _call(name, body, grid, [a_spec, b_spec, *extra_specs], out_specs, out_shapes, [a, b, *extras], scratch,
                 ("parallel", "parallel", "arbitrary"), plan)


def _store(p, e, o, rs):
    o[0][rs, :] = p.astype(o[0].dtype)


def _rms_mod_fwd(name, x, gain, shift, scale, plan=None):
    S, D = x.shape
    ts = _pick(S, (512,))

    def body(x_ref, g_ref, sh_ref, sc_ref, h_ref):
        xv = x_ref[...]
        r = lax.rsqrt(jnp.mean(xv * xv, axis=-1, keepdims=True) + EPS)
        n = xv * r * g_ref[...]
        h_ref[...] = (n * (1.0 + sc_ref[...]) + sh_ref[...]).astype(BF)

    row = pl.BlockSpec((ts, D), lambda i: (i, 0))
    vec = pl.BlockSpec((1, D), lambda i: (0, 0))
    return _call(name, body, (S // ts,), [row, vec, vec, vec], [row], [_sds((S, D), BF)], [x, gain, shift, scale],
                 sem=("parallel",), plan=plan)[0]


def _acc_rows(acc_ref, first, part):
    @pl.when(first)
    def _():
        acc_ref[...] = part

    @pl.when(jnp.logical_not(first))
    def _():
        acc_ref[...] += part


def _rms_mod_bwd(name, dh, x, dres, gain, scale, f=None, coef=None, plan=None):
    S, D = x.shape
    ts = _pick(S, (256,))
    gated = f is not None

    def body(dh_ref, x_ref, dr_ref, g_ref, sc_ref, *rest):
        xv = x_ref[...]
        dhv = dh_ref[...]
        g = g_ref[...]
        r = lax.rsqrt(jnp.mean(xv * xv, axis=-1, keepdims=True) + EPS)
        xhat = xv * r
        dn = dhv * (1.0 + sc_ref[...])
        dxhat = dn * g
        dx = dr_ref[...] + r * (dxhat - xhat * jnp.mean(dxhat * xhat, axis=-1, keepdims=True))
        rows = [jnp.sum(dhv, axis=0, keepdims=True), jnp.sum(dhv * (xhat * g), axis=0, keepdims=True),
                jnp.sum(dn * xhat, axis=0, keepdims=True)]
        if gated:
            f_ref, c_ref, dx_ref, df_ref, acc_ref = rest
            df_ref[...] = (dx * c_ref[...]).astype(BF)
            rows.append(jnp.sum(dx * f_ref[...].astype(F32), axis=0, keepdims=True))
        else:
            dx_ref, acc_ref = rest
        dx_ref[...] = dx
        _acc_rows(acc_ref, pl.program_id(0) == 0, jnp.concatenate(rows + [jnp.zeros((8 - len(rows), D), F32)], axis=0))

    row = pl.BlockSpec((ts, D), lambda i: (i, 0))
    vec = pl.BlockSpec((1, D), lambda i: (0, 0))
    acc = pl.BlockSpec((8, D), lambda i: (0, 0))
    if gated:
        return _call(name, body, (S // ts,), [row, row, row, vec, vec, row, vec], [row, row, acc],
                     [_sds((S, D), F32), _sds((S, D), BF), _sds((8, D), F32)], [dh, x, dres, gain, scale, f, coef],
                     sem=("arbitrary",), plan=plan)
    return _call(name, body, (S // ts,), [row, row, row, vec, vec], [row, acc],
                 [_sds((S, D), F32), _sds((8, D), F32)], [dh, x, dres, gain, scale], sem=("arbitrary",), plan=plan)


def _loss_bwd(x3, target, f, coef):
    S, D = x3.shape
    ts = _pick(S, (512,))

    def body(x_ref, t_ref, f_ref, c_ref, dx_ref, df_ref, acc_ref):
        e = x_ref[...] - t_ref[...]
        dx = e * (1.0 / D)
        dx_ref[...] = dx
        df_ref[...] = (dx * c_ref[...]).astype(BF)
        part = jnp.concatenate([jnp.sum(e * e, axis=0, keepdims=True), jnp.sum(dx * f_ref[...].astype(F32), axis=0, keepdims=True),
                                jnp.zeros((6, D), F32)], axis=0)
        _acc_rows(acc_ref, pl.program_id(0) == 0, part)

    row = pl.BlockSpec((ts, D), lambda i: (i, 0))
    return _call("loss_bwd", body, (S // ts,), [row, row, row, pl.BlockSpec((1, D), lambda i: (0, 0))],
                 [row, row, pl.BlockSpec((8, D), lambda i: (0, 0))],
                 [_sds((S, D), F32), _sds((S, D), BF), _sds((8, D), F32)], [x3, target, f, coef], sem=("arbitrary",))


def _silu_parts(g):
    s = jax.nn.sigmoid(g)
    return s, g * s


def _ffn_up(name, h, wgu4, plan=None):
    S, D = h.shape
    SH = wgu4.shape[2]
    F = 2 * SH
    tm = _pick(S, (512,))
    tn = _pick(SH, (1408, 256))
    nts = SH // tn

    def body(h_ref, wg_ref, wu_ref, gu_ref, act_ref):
        hv = h_ref[...]
        g = jnp.dot(hv, wg_ref[...], preferred_element_type=F32)
        u = jnp.dot(hv, wu_ref[...], preferred_element_type=F32)
        gu_ref[0] = g.astype(BF)
        gu_ref[1] = u.astype(BF)
        act_ref[...] = (_silu_parts(g)[1] * u).astype(BF)

    return _call(name, body, (S // tm, F // tn),
                 [pl.BlockSpec((tm, D), lambda i, j: (i, 0)),
                  pl.BlockSpec((None, D, tn), lambda i, j: (j // nts, 0, j % nts)),
                  pl.BlockSpec((None, D, tn), lambda i, j: (2 + j // nts, 0, j % nts))],
                 [pl.BlockSpec((2, tm, tn), lambda i, j: (0, i, j)), pl.BlockSpec((tm, tn), lambda i, j: (i, j))],
                 [_sds((2, S, F), BF), _sds((S, F), BF)], [h, wgu4, wgu4], sem=("parallel", "parallel"), plan=plan)


def _mm_residual(name, a, w, x_in, coef, plan=None):
    S, K = a.shape
    D = w.shape[1]
    tm = _pick(S, (1024,))
    tn = _pick(D, (512,))
    tk = K

    def epi(p, e, o, rs):
        o[0][rs, :] = e[0][rs, :] + e[1][...] * p
        o[1][rs, :] = p.astype(BF)

    tile = pl.BlockSpec((tm, tn), lambda i, j, k: (i, j))
    return _mm(name, (S // tm, D // tn, K // tk), NN,
               a, pl.BlockSpec((tm, tk), lambda i, j, k: (i, k)),
               w, pl.BlockSpec((tk, tn), lambda i, j, k: (k, j)),
               [x_in, coef], [tile, pl.BlockSpec((1, tn), lambda i, j, k: (0, j))],
               [_sds((S, D), F32), _sds((S, D), BF)], [tile, tile], (tm, tn), epi, plan)


def _ffn_dact(name, df, wd, gu, plan=None):
    S, D = df.shape
    F = wd.shape[0]
    tm = _pick(S, (512,))
    tn = _pick(F, (1408, 256))

    def epi(p, e, o, rs):
        g = e[0][0, rs, :].astype(F32)
        u = e[0][1, rs, :].astype(F32)
        s, sg = _silu_parts(g)
        o[0][0, rs, :] = (p * u * (s * (1.0 + g * (1.0 - s)))).astype(BF)
        o[0][1, rs, :] = (p * sg).astype(BF)

    pair = pl.BlockSpec((2, tm, tn), lambda i, j, k: (0, i, j))
    return _mm(name, (S // tm, F // tn, 1), NT,
               df, pl.BlockSpec((tm, D), lambda i, j, k: (i, 0)),
               wd, pl.BlockSpec((tn, D), lambda i, j, k: (j, 0)),
               [gu], [pair], [_sds((2, S, F), BF)], [pair], None, epi, plan)[0]


def _ffn_dh(name, dgu, wgu4, plan=None):
    _, S, F = dgu.shape
    _, D, SH = wgu4.shape
    tm = _pick(S, (1024,))
    tn = _pick(D, (512,))

    def body(a_ref, b_ref, o_ref, acc):
        k = pl.program_id(2)
        p = lax.dot_general(a_ref[:, :SH], b_ref[0], NT, preferred_element_type=F32)
        p = p + lax.dot_general(a_ref[:, SH:], b_ref[1], NT, preferred_element_type=F32)

        @pl.when(k == 0)
        def _():
            acc[...] = p

        @pl.when(k == 1)
        def _():
            o_ref[...] = acc[...] + p

    return _call(name, body, (S // tm, D // tn, 2),
                 [pl.BlockSpec((None, tm, F), lambda i, j, k: (k, i, 0)), pl.BlockSpec((2, tn, SH), lambda i, j, k: (k, j, 0))],
                 [pl.BlockSpec((tm, tn), lambda i, j, k: (i, j))], [_sds((S, D), F32)], [dgu, wgu4],
                 [pltpu.VMEM((tm, tn), F32)], ("parallel", "parallel", "arbitrary"), plan)[0]


def _ffn_dwgu(name, h, dgu, plan=None):
    _, S, F = dgu.shape
    D = h.shape[1]
    SH = F // 2
    tk1 = _pick(D, (512,))
    tn = _pick(SH, (1408, 256))
    ts = _pick(S, (4096, 1024))
    npj = F // tn
    nsj = SH // tn
    return _mm(name, (D // tk1, 2 * npj, S // ts), TN,
               h, pl.BlockSpec((ts, tk1), lambda i, j, k: (k, i)),
               dgu, pl.BlockSpec((None, ts, tn), lambda i, j, k: (j // npj, k, j % npj)),
               [], [], [_sds((4, D, SH), BF)],
               [pl.BlockSpec((None, tk1, tn), lambda i, j, k: (j // nsj, i, j % nsj))], (tk1, tn), _store, plan)[0]


def _mm_tn(name, a, b, tk1_prefs, tn_prefs, plan=None):
    S, K1 = a.shape
    N = b.shape[1]
    tk1 = _pick(K1, tk1_prefs)
    tn = _pick(N, tn_prefs)
    ts = _pick(S, (4096, 1024))
    return _mm(name, (K1 // tk1, N // tn, S // ts), TN,
               a, pl.BlockSpec((ts, tk1), lambda i, j, k: (k, i)),
               b, pl.BlockSpec((ts, tn), lambda i, j, k: (k, j)),
               [], [], [_sds((K1, N), BF)], [pl.BlockSpec((tk1, tn), lambda i, j, k: (i, j))], (tk1, tn), _store, plan)[0]


def _pool_window(ext, w, back):
    n = ext.shape[0]
    s = ext
    for step in (1, 2, 4, 8):
        sh = pltpu.roll(s, (n - step) if back else step, axis=0)
        s = jnp.where(w > step, s + sh, s)
    return s


def _pool_fwd(z, PW):
    S = z.shape[0]
    tc = _pick(S, (1024,))
    bpg = (PW // 4) // LANES
    H = POOL_MAX_W

    def body(prev_ref, u_ref, o_ref):
        i = pl.program_id(0)
        j = pl.program_id(1)
        w = lax.shift_left(jnp.int32(2), j // bpg)
        u = u_ref[...]
        prev = jnp.where(i > 0, prev_ref[...], 0.0)
        s = _pool_window(jnp.concatenate([prev, u], axis=0), w, False)[H:]
        t = i * tc + lax.broadcasted_iota(jnp.int32, (tc, LANES), 0)
        cnt = jnp.minimum(t + 1, w).astype(F32)
        o_ref[...] = (s / cnt - u).astype(BF)

    r = tc // H
    return _call("pool_fwd", body, (S // tc, PW // LANES),
                 [pl.BlockSpec((H, LANES), lambda i, j: (jnp.maximum(i * r - 1, 0), j)),
                  pl.BlockSpec((tc, LANES), lambda i, j: (i, j))],
                 [pl.BlockSpec((tc, LANES), lambda i, j: (i, j))], [_sds((S, PW), BF)], [z, z], sem=("parallel", "parallel"))[0]


def _pool_bwd(dpooled):
    S, PW = dpooled.shape
    tc = _pick(S, (1024,))
    bpg = (PW // 4) // LANES
    H = POOL_MAX_W
    last = S // tc - 1

    def body(dp_ref, nxt_ref, o_ref):
        i = pl.program_id(0)
        j = pl.program_id(1)
        w = lax.shift_left(jnp.int32(2), j // bpg)
        dp = dp_ref[...]
        nxt = jnp.where(i < last, nxt_ref[...], 0.0)
        ext = jnp.concatenate([dp, nxt], axis=0)
        t = i * tc + lax.broadcasted_iota(jnp.int32, (tc + H, LANES), 0)
        cnt = jnp.minimum(t + 1, w).astype(F32)
        s = _pool_window(ext / cnt, w, True)[:tc]
        o_ref[...] = (s - dp).astype(BF)

    r = tc // H
    nh = S // H - 1
    return _call("pool_bwd", body, (S // tc, PW // LANES),
                 [pl.BlockSpec((tc, LANES), lambda i, j: (i, j)),
                  pl.BlockSpec((H, LANES), lambda i, j: (jnp.minimum((i + 1) * r, nh), j))],
                 [pl.BlockSpec((tc, LANES), lambda i, j: (i, j))], [_sds((S, PW), BF)], [dpooled, dpooled],
                 sem=("parallel", "parallel"))[0]


def _pool_mix(pooled, pm, scale):
    S, PW = pooled.shape
    gw = PW // 4
    ts = _pick(S, (1024,))

    def epi(p, e, o, rs):
        o[0][rs, :] = (p * e[0][...]).astype(BF)

    tile = pl.BlockSpec((ts, gw), lambda i, j, k: (i, j))
    return _mm("pool_mix", (S // ts, 4, 1), NN, pooled, tile,
               pm, pl.BlockSpec((None, gw, gw), lambda i, j, k: (j, 0, 0)),
               [scale], [pl.BlockSpec((1, gw), lambda i, j, k: (0, j))], [_sds((S, PW), BF)], [tile], None, epi)[0]


def _pool_mix_bwd(pooled, pm, scale, dmixed):
    S, PW = pooled.shape
    gw = PW // 4
    ts = _pick(S, (1024,))

    def body(p_ref, pm_ref, sc_ref, dm_ref, dp_ref, dpm_ref, dsc_ref):
        i = pl.program_id(1)
        p = p_ref[...]
        w = pm_ref[...]
        dm = dm_ref[...]
        pre = jnp.dot(p, w, preferred_element_type=F32)
        dmp = (dm * sc_ref[...]).astype(BF)
        dp_ref[...] = lax.dot_general(dmp, w, NT, preferred_element_type=F32)
        dw = lax.dot_general(p, dmp, TN, preferred_element_type=F32)
        ds = jnp.concatenate([jnp.sum(dm * pre, axis=0, keepdims=True), jnp.zeros((7, gw), F32)], axis=0)
        _acc_rows(dpm_ref, i == 0, dw)
        _acc_rows(dsc_ref, i == 0, ds)

    tile = pl.BlockSpec((ts, gw), lambda g, i: (i, g))
    return _call("pool_mix_bwd", body, (4, S // ts),
                 [tile, pl.BlockSpec((None, gw, gw), lambda g, i: (g, 0, 0)), pl.BlockSpec((1, gw), lambda g, i: (0, g)), tile],
                 [tile, pl.BlockSpec((None, gw, gw), lambda g, i: (g, 0, 0)), pl.BlockSpec((8, gw), lambda g, i: (0, g))],
                 [_sds((S, PW), F32), _sds((4, gw, gw), F32), _sds((8, PW), F32)], [pooled, pm, scale, dmixed],
                 sem=("parallel", "arbitrary"))


def _bucket_onehot():
    ql = np.arange(BLK)[:, None]
    j = np.arange(2 * BLK)[None, :]
    d = BLK + ql - j
    n = np.clip(d, 0, None)
    nf = np.maximum(n, 1).astype(np.float32)
    max_exact = NUM_BUCKETS // 2
    large = max_exact + (np.log(nf / max_exact) / np.log(BLK / max_exact) * (NUM_BUCKETS - max_exact)).astype(np.int32)
    large = np.minimum(large, NUM_BUCKETS - 1)
    bucket = np.where(n < max_exact, n, large).astype(np.int32)
    valid = (d >= 0) & (d < BLK)
    oh = (bucket[None] == np.arange(NUM_BUCKETS)[:, None, None]) & valid[None]
    return oh.reshape(NUM_BUCKETS, BLK * 2 * BLK)


def _three_bf16(v):
    hi = v.astype(BF)
    r1 = v - hi.astype(F32)
    mid = r1.astype(BF)
    lo = (r1 - mid.astype(F32)).astype(BF)
    return hi, mid, lo


def _bias_table(rel_bias):
    oh = jnp.asarray(_bucket_onehot(), BF)
    tn = 4096

    def body(rb_ref, oh_ref, o_ref):
        o = oh_ref[...]
        hi, mid, lo = _three_bf16(rb_ref[...])
        acc = lax.dot_general(hi, o, TN, preferred_element_type=F32)
        acc = acc + lax.dot_general(mid, o, TN, preferred_element_type=F32)
        acc = acc + lax.dot_general(lo, o, TN, preferred_element_type=F32)
        on_band = jnp.sum(o.astype(F32), axis=0, keepdims=True) > 0.5
        o_ref[...] = jnp.where(on_band, acc, NEG_INF)

    n = oh.shape[1]
    return _call("bias_table", body, (n // tn,),
                 [pl.BlockSpec((NUM_BUCKETS, N_HEADS), lambda i: (0, 0)), pl.BlockSpec((NUM_BUCKETS, tn), lambda i: (0, i))],
                 [pl.BlockSpec((N_HEADS, tn), lambda i: (0, i))], [_sds((N_HEADS, n), F32)], [rel_bias, oh], sem=("parallel",))[0]


def _rel_bias_grad(dl):
    oh = jnp.asarray(_bucket_onehot(), BF)
    n = oh.shape[1]
    tk = 4096

    def body(dl_ref, oh_ref, o_ref):
        o = oh_ref[...]
        hi, mid, lo = _three_bf16(dl_ref[...])
        acc = lax.dot_general(o, hi, NT, preferred_element_type=F32)
        acc = acc + lax.dot_general(o, mid, NT, preferred_element_type=F32)
        acc = acc + lax.dot_general(o, lo, NT, preferred_element_type=F32)
        _acc_rows(o_ref, pl.program_id(0) == 0, acc)

    return _call("rel_bias_grad", body, (n // tk,),
                 [pl.BlockSpec((N_HEADS, tk), lambda i: (0, i)), pl.BlockSpec((NUM_BUCKETS, tk), lambda i: (0, i))],
                 [pl.BlockSpec((NUM_BUCKETS, N_HEADS), lambda i: (0, 0))], [_sds((NUM_BUCKETS, N_HEADS), F32)], [dl, oh],
                 sem=("arbitrary",))[0]


def _lo_half(shape):
    return lax.broadcasted_iota(jnp.int32, shape, 1) < HEAD_DIM


def _half_sum(x, lo):
    s_lo = jnp.sum(jnp.where(lo, x, 0.0), axis=-1, keepdims=True)
    s_hi = jnp.sum(jnp.where(lo, 0.0, x), axis=-1, keepdims=True)
    return jnp.where(lo, s_lo, s_hi)


def _norm2(x, lo):
    r = lax.rsqrt(_half_sum(x * x, lo) * (1.0 / HEAD_DIM) + EPS)
    return x * r, r


def _norm2_bwd(dy, xhat, r, gain, lo):
    dxhat = dy * gain
    dx = r * (dxhat - xhat * (_half_sum(dxhat * xhat, lo) * (1.0 / HEAD_DIM)))
    return dx, dy * xhat


def _swap(x):
    return pltpu.roll(x, HEAD_DIM, axis=1)


def _pair_rows(x, kk):
    return jnp.concatenate([x, _swap(x)] if kk == 0 else [_swap(x), x], axis=0)


def _attn_probs(n, kk, jp0, npairs, zq_ref, K, qg, bias_ref, sink_ref):
    lo_q = _lo_half((BLK, LANES))
    rows, qhats, qrs = [], [], []
    for jp in range(jp0, jp0 + npairs):
        qhat, qr = _norm2(zq_ref[:, jp * LANES:(jp + 1) * LANES], lo_q)
        rows.append(_pair_rows(qhat * qg * (HEAD_DIM ** -0.5), kk))
        qhats.append(qhat)
        qrs.append(qr)
    Q = jnp.concatenate(rows, axis=0).astype(BF)
    cols = slice(jp0 * 2 * BLK, (jp0 + npairs) * 2 * BLK)
    l = lax.dot_general(K, Q, NT, preferred_element_type=F32) + bias_ref[kk, :, cols]
    l = jnp.concatenate([jnp.where(n == 0, NEG_INF, l[:BLK]), l[BLK:]], axis=0)
    sink = sink_ref[kk, :, cols]
    m = jnp.maximum(jnp.max(l, axis=0, keepdims=True), sink)
    e = jnp.exp(l - m)
    es = jnp.exp(sink - m)
    inv = 1.0 / (jnp.sum(e, axis=0, keepdims=True) + es)
    return Q, e * inv, es * inv, qhats, qrs


def _attn_specs(o_q, o_k):
    nq = o_q // 512
    nk = o_k // LANES
    prev = lambda n: (jnp.maximum(n - 1, 0), nk)
    prev_v = lambda n: (jnp.maximum(n - 1, 0), nk + 1)
    return [pl.BlockSpec((BLK, 512), lambda n: (n, nq)), pl.BlockSpec((BLK, 512), lambda n: (n, nq + 1)),
            pl.BlockSpec((BLK, LANES), prev), pl.BlockSpec((BLK, LANES), lambda n: (n, nk)),
            pl.BlockSpec((BLK, LANES), prev_v), pl.BlockSpec((BLK, LANES), lambda n: (n, nk + 1)),
            pl.BlockSpec((1, LANES), lambda n: (0, 0)), pl.BlockSpec((1, LANES), lambda n: (0, 0)),
            pl.BlockSpec((N_KV, 1, 8 * BLK), lambda n: (0, 0, 0)),
            pl.BlockSpec((N_KV, 2 * BLK, 8 * BLK), lambda n: (0, 0, 0))]


def _attn_fwd(z, o_q, o_k, qg2, kg2, sink_rows, bias, plan=None):
    S = z.shape[0]

    def body(zq0, zq1, zkp, zkc, zvp, zvc, qg_ref, kg_ref, sink_ref, bias_ref, o_ref):
        n = pl.program_id(0)
        lo_k = _lo_half((2 * BLK, LANES))
        lo_q = _lo_half((BLK, LANES))
        khat, _ = _norm2(jnp.concatenate([zkp[...], zkc[...]], axis=0), lo_k)
        kn = khat * kg_ref[...]
        vb = jnp.concatenate([zvp[...], zvc[...]], axis=0).astype(BF)
        for kk, zq in enumerate((zq0, zq1)):
            K = jnp.where(lo_k if kk == 0 else jnp.logical_not(lo_k), kn, 0.0).astype(BF)
            for jp in range(4):
                _, p, _, _, _ = _attn_probs(n, kk, jp, 1, zq, K, qg_ref[...], bias_ref, sink_ref)
                r = lax.dot_general(p.astype(BF), vb, TN, preferred_element_type=F32)
                ev, od = r[:BLK], r[BLK:]
                pair = jnp.where(lo_q, ev, _swap(od)) if kk == 0 else jnp.where(lo_q, _swap(ev), od)
                c0 = (4 * kk + jp) * LANES
                o_ref[:, c0:c0 + LANES] = pair.astype(BF)

    return _call("attn_fwd", body, (S // BLK,), _attn_specs(o_q, o_k), [pl.BlockSpec((BLK, ATT_W), lambda n: (n, 0))],
                 [_sds((S, ATT_W), BF)], [z, z, z, z, z, z, qg2, kg2, sink_rows, bias], sem=("parallel",), plan=plan)[0]


def _attn_bwd(z, o_q, o_k, qg2, kg2, sink_rows, bias, dout, plan=None):
    S = z.shape[0]

    def body(zq0, zq1, zkp, zkc, zvp, zvc, qg_ref, kg_ref, sink_ref, bias_ref, do_ref,
             dq_ref, dkp_ref, dkc_ref, dvp_ref, dvc_ref, dl_ref, dsink_ref, dgain_ref):
        n = pl.program_id(0)
        lo_k = _lo_half((2 * BLK, LANES))
        lo_q = _lo_half((BLK, LANES))
        qg = qg_ref[...]
        kg = kg_ref[...]
        khat, kr = _norm2(jnp.concatenate([zkp[...], zkc[...]], axis=0), lo_k)
        kn = khat * kg
        vf = jnp.concatenate([zvp[...], zvc[...]], axis=0)

        @pl.when(n == 0)
        def _():
            dl_ref[...] = jnp.zeros_like(dl_ref)
            dsink_ref[...] = jnp.zeros_like(dsink_ref)
            dgain_ref[...] = jnp.zeros_like(dgain_ref)

        dkn = jnp.zeros((2 * BLK, LANES), F32)
        dvb = jnp.zeros((2 * BLK, LANES), F32)
        dqg = jnp.zeros((1, LANES), F32)
        for kk, zq in enumerate((zq0, zq1)):
            half_k = lo_k if kk == 0 else jnp.logical_not(lo_k)
            K = jnp.where(half_k, kn, 0.0).astype(BF)
            V = jnp.where(half_k, vf, 0.0).astype(BF)
            Q, p, ps, qhats, qrs = _attn_probs(n, kk, 0, 4, zq, K, qg, bias_ref, sink_ref)
            dO = jnp.concatenate([_pair_rows(do_ref[:, (4 * kk + jp) * LANES:(4 * kk + jp + 1) * LANES], kk) for jp in range(4)],
                                 axis=0).astype(BF)
            dP = lax.dot_general(V, dO, NT, preferred_element_type=F32)
            delta = jnp.sum(p * dP, axis=0, keepdims=True)
            dS = p * (dP - delta)
            dsink_ref[kk] += -ps * delta
            dl_ref[kk] += dS
            dSb = dS.astype(BF)
            dvb = dvb + jnp.where(half_k, jnp.dot(p.astype(BF), dO, preferred_element_type=F32), 0.0)
            dkn = dkn + jnp.where(half_k, jnp.dot(dSb, Q, preferred_element_type=F32), 0.0)
            dQ = lax.dot_general(dSb, K, TN, preferred_element_type=F32) * (HEAD_DIM ** -0.5)
            for jp in range(4):
                ev = dQ[(2 * jp) * BLK:(2 * jp + 1) * BLK]
                od = dQ[(2 * jp + 1) * BLK:(2 * jp + 2) * BLK]
                dy = (ev + _swap(od)) if kk == 0 else (_swap(ev) + od)
                dx, gq = _norm2_bwd(dy, qhats[jp], qrs[jp], qg, lo_q)
                dqg = dqg + jnp.sum(gq, axis=0, keepdims=True)
                c0 = (4 * kk + jp) * LANES
                dq_ref[:, c0:c0 + LANES] = dx.astype(BF)
        dk, gk = _norm2_bwd(dkn, khat, kr, kg, lo_k)
        dkp_ref[...] = dk[:BLK]
        dkc_ref[...] = dk[BLK:]
        dvp_ref[...] = dvb[:BLK]
        dvc_ref[...] = dvb[BLK:]
        dgain_ref[...] += jnp.concatenate([dqg, jnp.sum(gk, axis=0, keepdims=True), jnp.zeros((6, LANES), F32)], axis=0)

    blk = pl.BlockSpec((BLK, LANES), lambda n: (n, 0))
    wide = pl.BlockSpec((BLK, ATT_W), lambda n: (n, 0))
    return _call(
        "attn_bwd", body, (S // BLK,), _attn_specs(o_q, o_k) + [wide],
        [wide, blk, blk, blk, blk, pl.BlockSpec((N_KV, 2 * BLK, 8 * BLK), lambda n: (0, 0, 0)),
         pl.BlockSpec((N_KV, 1, 8 * BLK), lambda n: (0, 0, 0)), pl.BlockSpec((8, LANES), lambda n: (0, 0))],
        [_sds((S, ATT_W), BF), _sds((S, LANES), F32), _sds((S, LANES), F32), _sds((S, LANES), F32), _sds((S, LANES), F32),
         _sds((N_KV, 2 * BLK, 8 * BLK), F32), _sds((N_KV, 1, 8 * BLK), F32), _sds((8, LANES), F32)],
        [z, z, z, z, z, z, qg2, kg2, sink_rows, bias, dout], sem=("arbitrary",), plan=plan)


def _kv_combine(dkp, dkc, dvp, dvc):
    S = dkc.shape[0]
    last = S // BLK - 1

    def body(kp_ref, kc_ref, vp_ref, vc_ref, dk_ref, dv_ref):
        more = pl.program_id(0) < last
        dk_ref[...] = (kc_ref[...] + jnp.where(more, kp_ref[...], 0.0)).astype(BF)
        dv_ref[...] = (vc_ref[...] + jnp.where(more, vp_ref[...], 0.0)).astype(BF)

    cur = pl.BlockSpec((BLK, LANES), lambda n: (n, 0))
    nxt = pl.BlockSpec((BLK, LANES), lambda n: (jnp.minimum(n + 1, last), 0))
    return _call("kv_combine", body, (S // BLK,), [nxt, cur, nxt, cur], [cur, cur],
                 [_sds((S, LANES), BF), _sds((S, LANES), BF)], [dkp, dkc, dvp, dvc], sem=("parallel",))


def _merge_fwd(mixed, attn, wpu4, wau4, z, o_ga, plan=None):
    S, PW = mixed.shape
    _, _, CS = wpu4.shape
    D = 4 * CS
    tm = _pick(S, (1024,))
    tn = 256
    nsj = CS // tn
    na = o_ga // tn
    nb = (o_ga + D) // tn

    def body(m_ref, a_ref, wp_ref, wa_ref, ga_ref, gb_ref, mg_ref, yy_ref):
        yp = jnp.dot(m_ref[...], wp_ref[...], preferred_element_type=F32)
        ya = jnp.dot(a_ref[...], wa_ref[...], preferred_element_type=F32)
        mg_ref[...] = (jax.nn.sigmoid(ga_ref[...]) * yp + jax.nn.sigmoid(gb_ref[...]) * ya).astype(BF)
        yy_ref[0] = yp.astype(BF)
        yy_ref[1] = ya.astype(BF)

    return _call("merge_fwd", body, (S // tm, D // tn),
                 [pl.BlockSpec((tm, PW), lambda i, j: (i, 0)), pl.BlockSpec((tm, ATT_W), lambda i, j: (i, 0)),
                  pl.BlockSpec((None, PW, tn), lambda i, j: (j // nsj, 0, j % nsj)),
                  pl.BlockSpec((None, ATT_W, tn), lambda i, j: (j // nsj, 0, j % nsj)),
                  pl.BlockSpec((tm, tn), lambda i, j: (i, na + j)), pl.BlockSpec((tm, tn), lambda i, j: (i, nb + j))],
                 [pl.BlockSpec((tm, tn), lambda i, j: (i, j)), pl.BlockSpec((2, tm, tn), lambda i, j: (0, i, j))],
                 [_sds((S, D), BF), _sds((2, S, D), BF)], [mixed, attn, wpu4, wau4, z, z], sem=("parallel", "parallel"), plan=plan)


def _merge_bwd(do, wo, z, o_ga, yy, plan=None):
    S, D = do.shape
    tm = _pick(S, (1024,))
    tn = 256
    na = o_ga // tn
    nb = (o_ga + D) // tn

    def epi(p, e, o, rs):
        sa = jax.nn.sigmoid(e[0][rs, :])
        sb = jax.nn.sigmoid(e[1][rs, :])
        yp = e[2][0, rs, :].astype(F32)
        ya = e[2][1, rs, :].astype(F32)
        o[0][0, rs, :] = (p * yp * sa * (1.0 - sa)).astype(BF)
        o[0][1, rs, :] = (p * ya * sb * (1.0 - sb)).astype(BF)
        o[1][0, rs, :] = (p * sa).astype(BF)
        o[1][1, rs, :] = (p * sb).astype(BF)

    pair = pl.BlockSpec((2, tm, tn), lambda i, j, k: (0, i, j))
    return _mm("merge_bwd", (S // tm, D // tn, 1), NT,
               do, pl.BlockSpec((tm, D), lambda i, j, k: (i, 0)),
               wo, pl.BlockSpec((tn, D), lambda i, j, k: (j, 0)),
               [z, z, yy], [pl.BlockSpec((tm, tn), lambda i, j, k: (i, na + j)), pl.BlockSpec((tm, tn), lambda i, j, k: (i, nb + j)), pair],
               [_sds((2, S, D), BF), _sds((2, S, D), BF)], [pair, pair], None, epi, plan)


def _mm_up_t(name, dyy, which, w4):
    _, S, D = dyy.shape
    _, K, CS = w4.shape
    tm = _pick(S, (1024,))

    def body(a_ref, b_ref, o_ref):
        p = lax.dot_general(a_ref[:, :CS], b_ref[0], NT, preferred_element_type=F32)
        for k in range(1, N_CHIPS):
            p = p + lax.dot_general(a_ref[:, k * CS:(k + 1) * CS], b_ref[k], NT, preferred_element_type=F32)
        o_ref[...] = p

    return _call(name, body, (S // tm,),
                 [pl.BlockSpec((None, tm, D), lambda i: (which, i, 0)), pl.BlockSpec((N_CHIPS, K, CS), lambda i: (0, 0, 0))],
                 [pl.BlockSpec((tm, K), lambda i: (i, 0))], [_sds((S, K), F32)], [dyy, w4], sem=("parallel",))[0]


def _mm_up_dw(name, a, dyy, which):
    _, S, D = dyy.shape
    K = a.shape[1]
    CS = D // N_CHIPS
    ts = _pick(S, (4096, 1024))
    return _mm(name, (1, N_CHIPS, S // ts), TN,
               a, pl.BlockSpec((ts, K), lambda i, j, k: (k, 0)),
               dyy, pl.BlockSpec((None, ts, CS), lambda i, j, k: (which, k, j)),
               [], [], [_sds((N_CHIPS, K, CS), BF)], [pl.BlockSpec((None, K, CS), lambda i, j, k: (j, 0, 0))], (K, CS), _store)[0]


def _adamw(w, g, m, v):
    m = ADAM_B1 * m + (1.0 - ADAM_B1) * g
    v = ADAM_B2 * v + (1.0 - ADAM_B2) * (g * g)
    m_hat = m / (1.0 - ADAM_B1 ** ADAM_STEP)
    v_hat = v / (1.0 - ADAM_B2 ** ADAM_STEP)
    delta = -ADAM_LR * (m_hat / (jnp.sqrt(v_hat) + ADAM_EPS) + ADAM_WD * w)
    return delta, m, v


def _mod_fwd(c_all, w_ada, b_sh):
    D, cols = w_ada.shape
    tn = cols // 9

    def body(c_ref, w_ref, b_ref, o_ref):
        cv = c_ref[...]
        sc = (cv * jax.nn.sigmoid(cv)).astype(BF)
        o_ref[...] = jnp.dot(sc, w_ref[...].astype(BF), preferred_element_type=F32) + b_ref[...]

    return _call("mod_fwd", body, (9,),
                 [pl.BlockSpec((N_DEV, D), lambda j: (0, 0)), pl.BlockSpec((D, tn), lambda j: (0, j)), pl.BlockSpec((1, tn), lambda j: (0, j))],
                 [pl.BlockSpec((N_DEV, tn), lambda j: (0, j))], [_sds((N_DEV, cols), F32)], [c_all, w_ada, b_sh], sem=("parallel",))[0]


def _wada_bwd(c_all, dmod_sh, w, m, v, plan=None):
    D, cols = w.shape
    tn = cols // 18

    def body(c_ref, d_ref, w_ref, m_ref, v_ref, g_ref, dl_ref, nm_ref, nv_ref):
        cv = c_ref[...]
        sc = (cv * jax.nn.sigmoid(cv)).astype(BF)
        g = lax.dot_general(sc, d_ref[...].astype(BF), TN, preferred_element_type=F32)
        g_ref[...] = g
        dl_ref[...], nm_ref[...], nv_ref[...] = _adamw(w_ref[...], g, m_ref[...], v_ref[...])

    tile = pl.BlockSpec((D, tn), lambda j: (0, j))
    out = _sds((D, cols), F32)
    return _call("wada_bwd", body, (18,),
                 [pl.BlockSpec((N_DEV, D), lambda j: (0, 0)), pl.BlockSpec((N_DEV, tn), lambda j: (0, j)), tile, tile, tile],
                 [tile] * 4, [out] * 4, [c_all, dmod_sh, w, m, v], sem=("parallel",), plan=plan)


def _adam_2d(name, w, g, m, v):
    R, C = w.shape
    tr = _row_tile(R, 256)

    def body(w_ref, g_ref, m_ref, v_ref, dl_ref, nm_ref, nv_ref):
        dl_ref[...], nm_ref[...], nv_ref[...] = _adamw(w_ref[...], g_ref[...], m_ref[...], v_ref[...])

    tile = pl.BlockSpec((tr, C), lambda i: (i, 0))
    out = _sds((R, C), F32)
    return _call(name, body, (R // tr,), [tile] * 4, [tile] * 3, [out] * 3, [w, g, m, v], sem=("parallel",))


def _small_finish(parts, w, m, v):
    _, R, C = parts.shape

    def body(p_ref, w_ref, m_ref, v_ref, g_ref, dl_ref, nm_ref, nv_ref):
        g = p_ref[0]
        for d in range(1, N_DEV):
            g = g + p_ref[d]
        g_ref[...] = g
        dl_ref[...], nm_ref[...], nv_ref[...] = _adamw(w_ref[...], g, m_ref[...], v_ref[...])

    out = _sds((R, C), F32)
    return pl.pallas_call(body, out_shape=[out] * 4, name="small_finish",
                          compiler_params=pltpu.CompilerParams(vmem_limit_bytes=VMEM_LIMIT))(parts, w, m, v)


def _my_chip():
    return 2 * lax.axis_index("x") + lax.axis_index("y")


def _cast_into_slot(name, w):
    R, C = w.shape
    tr = _row_tile(R, 256)

    def body(w_ref, o_ref):
        o_ref[...] = w_ref[...].astype(BF)

    return _call(name, body, (R // tr,), [pl.BlockSpec((tr, C), lambda i: (i, 0))],
                 [pl.BlockSpec((None, tr, C), lambda i: (_my_chip(), i, 0))], [_sds((N_CHIPS, R, C), BF)], [w], sem=("parallel",))[0]


def _add_pair(name, p, q):
    _, H, C = q.shape
    tr = _row_tile(H, 512)
    nt = H // tr

    def body(p_ref, q_ref, o_ref):
        o_ref[...] = (p_ref[...].astype(F32) + q_ref[...].astype(F32)).astype(BF)

    tile = pl.BlockSpec((None, tr, C), lambda k, i: (k, i, 0))
    return _call(name, body, (N_CHIPS, nt), [pl.BlockSpec((None, tr, C), lambda k, i: (k, lax.axis_index("c") * nt + i, 0)), tile],
                 [tile], [_sds(q.shape, BF)], [p, q], sem=("parallel", "parallel"))[0]


def _sum_chips(name, u, t):
    _, H, C = u.shape
    tr = _row_tile(H, 256)

    def body(u_ref, t_ref, o_ref):
        r = _my_chip()
        own = t_ref[...].astype(F32)
        pick = lambda k: jnp.where(r == k, own, u_ref[k].astype(F32))
        o_ref[...] = ((pick(0) + pick(1)) + pick(2)) + pick(3)

    return _call(name, body, (H // tr,),
                 [pl.BlockSpec((N_CHIPS, tr, C), lambda i: (0, i, 0)), pl.BlockSpec((None, tr, C), lambda i: (_my_chip(), i, 0))],
                 [pl.BlockSpec((None, tr, C), lambda i: (lax.axis_index("c"), i, 0))], [_sds((2, H, C), F32)], [u, t],
                 sem=("parallel",))[0]


BIG = ("gu1", "down1", "w_in", "pool_mix", "pool_up", "attn_up", "o", "gu2", "down2")
MIX = ("o", "pool_up", "attn_up", "pool_mix")
EARLY = ("down1", "w_in", "pool_mix", "pool_up", "attn_up", "o")

SCHEDULE = {
    "rms_mod_fwd1": ([("ici", ("gu1",))], []),
    "+gather_gu1_d2d": ([("d2d", ("gu1",))], []),
    "ffn1_up": ([("ici", EARLY)], []),
    "+gather_early_d2d": ([("d2d", EARLY[:1])], []),
    "ffn1_down": ([("ici", ("gu2",)), ("d2d", EARLY[1:])], []),
    "mix_in": ([("ici", ("down2",)), ("d2d", ("gu2",))], []),
    "mix_out": ([("d2d", ("down2",))], []),
    "ffn2_dwd": ([("split", ("gu2",))], [("add", ("gu2",))]),
    "ffn2_dh": ([("owners", ("gu2",)), ("split", ("down2",))], [("add", ("down2",)), ("sum", ("gu2",))]),
    "merge_bwd": ([("owners", ("down2",)), ("join", ("gu2",))], [("sum", ("down2",)), ("adam", ("gu2",))]),
    "attn_bwd": ([("split", MIX), ("join", ("down2",))], [("add", MIX), ("adam", ("down2",))]),
    "mix_dwin": ([("owners", MIX)], [("sum", MIX)]),
    "mix_dh": ([("split", ("w_in",)), ("join", MIX)], [("add", ("w_in",)), ("adam", MIX)]),
    "ffn1_dact": ([("owners", ("w_in",))], [("sum", ("w_in",))]),
    "ffn1_dwd": ([("split", ("gu1",)), ("join", ("w_in",))], [("add", ("gu1",)), ("adam", ("w_in",))]),
    "ffn1_dh": ([("owners", ("gu1",)), ("split", ("down1",))], [("add", ("down1",)), ("sum", ("gu1",))]),
    "rms_mod_bwd1": ([("owners", ("down1",)), ("join", ("gu1",))], [("sum", ("down1",)), ("adam", ("gu1",))]),
    "+join_down1": ([("join", ("down1",))], [("adam", ("down1",))]),
}


class _Plan:
    def __init__(self, w2, m2, v2, full, D, gw):
        self.w2, self.m2, self.v2, self.full, self.D, self.gw = w2, m2, v2, dict(full), D, gw
        self.part, self.got, self.sums, self.landed, self.g = {}, {}, {}, {}, {}
        self.result = {}
        self.pending = {}

    def _make(self, op, names):
        if op == "ici":
            return _gather_ici_stage([self.full[k] for k in names])
        if op == "d2d":
            return _gather_d2d_stage([self.full[k] for k in names])
        if op == "split":
            return _split_stage([self.part[k] for k in names])
        if op == "owners":
            return _owners_stage([self.sums[k] for k in names])
        return _join_stage([self.g[k] for k in names])

    def stages(self, name):
        ops = SCHEDULE.get(name, ([], []))[0]
        return [self._make(*op) for op in ops]

    def done(self, name, outs):
        ops, local = SCHEDULE[name]
        for op, res in zip(ops, outs):
            store = {"ici": self.full, "d2d": self.full, "split": self.got, "owners": self.landed, "join": self.g}[op[0]]
            store.update(zip(op[1], res))
        for op, names in local:
            for k in names:
                if op == "add":
                    self.sums[k] = _add_pair("add_pair_" + k, self.part[k], self.got[k])
                elif op == "sum":
                    self.g[k] = _sum_chips("sum_chips_" + k, self.landed[k], self.sums[k])
                else:
                    g2 = self.g[k].reshape(self.w2[k].shape)
                    self.result[k] = (g2, *_adam_2d("adam_" + k, self.w2[k], g2, self.m2[k], self.v2[k]))

    def alone(self, name):
        self.done(name, _run_stages(name[1:], self.stages(name)))

    def weight(self, k):
        D, gw, f = self.D, self.gw, self.full[k]
        if k in ("down1", "down2", "o"):
            return f.reshape(-1, D)
        if k == "pool_mix":
            return f.reshape(N_CHIPS, 4, gw // N_CHIPS, gw).transpose(1, 0, 2, 3).reshape(4, gw, gw)
        if k == "w_in":
            return f.reshape(-1, D)
        return f

    def partial(self, k, p):
        D, gw = self.D, self.gw
        if k in ("down1", "down2", "o", "w_in"):
            p = p.reshape(N_CHIPS, p.shape[0] // N_CHIPS, p.shape[1])
        elif k == "pool_mix":
            p = p.astype(BF).reshape(4, N_CHIPS, gw // N_CHIPS, gw).transpose(1, 0, 2, 3).reshape(N_CHIPS, gw, gw)
        self.part[k] = p


class _NoComm:
    def __init__(self, weights):
        self.w, self.part = weights, {}

    def stages(self, name):
        return []

    def alone(self, name):
        pass

    def weight(self, k):
        return self.w[k]

    def partial(self, k, p):
        self.part[k] = p


def _row(a, i):
    return a[i:i + 1]


def _local_step(x, target, mod, g_ffn1, g_mix, g_ffn2, pool_scale, q_gain, k_gain, sinks, rel_bias, plan):
    S, D = x.shape
    half = 0.5 * mod
    tile2 = lambda g: jnp.concatenate([g, g], axis=1)
    qg2, kg2 = tile2(q_gain), tile2(k_gain)
    sink_rows = jnp.broadcast_to(sinks.reshape(N_KV, 1, 8, 1), (N_KV, 1, 8, BLK)).reshape(N_KV, 1, 8 * BLK)
    bias = _bias_table(rel_bias).reshape(N_KV, 8, BLK, 2 * BLK).transpose(0, 3, 1, 2).reshape(N_KV, 2 * BLK, 8 * BLK)

    h1 = _rms_mod_fwd("rms_mod_fwd1", x, g_ffn1, _row(mod, 0), _row(mod, 1), plan)
    plan.alone("+gather_gu1_d2d")
    gu1, act1 = _ffn_up("ffn1_up", h1, plan.weight("gu1"), plan)
    plan.alone("+gather_early_d2d")
    x1, f1 = _mm_residual("ffn1_down", act1, plan.weight("down1"), x, _row(half, 2), plan)
    h2 = _rms_mod_fwd("rms_mod_fwd2", x1, g_mix, _row(mod, 3), _row(mod, 4))
    w_in_t = plan.weight("w_in")
    IN_W = w_in_t.shape[0]
    PW = plan.weight("pool_up").shape[1]
    o_q, o_k = PW, PW + ATT_W
    o_ga = o_k + 2 * KV_W
    tnz = _pick(IN_W, (1280, 256))
    tmz = _pick(S, (1024,))
    z = _mm("mix_in", (S // tmz, IN_W // tnz, 1), NT, h2, pl.BlockSpec((tmz, D), lambda i, j, k: (i, 0)),
            w_in_t, pl.BlockSpec((tnz, D), lambda i, j, k: (j, 0)), [], [], [_sds((S, IN_W), F32)],
            [pl.BlockSpec((tmz, tnz), lambda i, j, k: (i, j))], None, _store, plan)[0]
    pooled = _pool_fwd(z, PW)
    mixed = _pool_mix(pooled, plan.weight("pool_mix"), pool_scale)
    attn = _attn_fwd(z, o_q, o_k, qg2, kg2, sink_rows, bias)
    merged, yy = _merge_fwd(mixed, attn, plan.weight("pool_up"), plan.weight("attn_up"), z, o_ga)
    x2, fo = _mm_residual("mix_out", merged, plan.weight("o"), x1, _row(mod, 5), plan)
    h3 = _rms_mod_fwd("rms_mod_fwd3", x2, g_ffn2, _row(mod, 6), _row(mod, 7))
    gu2, act2 = _ffn_up("ffn2_up", h3, plan.weight("gu2"))
    x3, f2 = _mm_residual("ffn2_down", act2, plan.weight("down2"), x2, _row(half, 8))
    dx3, df2, loss_acc = _loss_bwd(x3, target, f2, _row(half, 8))

    dgu2 = _ffn_dact("ffn2_dact", df2, plan.weight("down2"), gu2)
    plan.partial("gu2", _ffn_dwgu("ffn2_dwgu", h3, dgu2))
    plan.partial("down2", _mm_tn("ffn2_dwd", act2, df2, (1408, 512), (512,), plan))
    dh3 = _ffn_dh("ffn2_dh", dgu2, plan.weight("gu2"), plan)
    dx2, do, acc3 = _rms_mod_bwd("rms_mod_bwd3", dh3, x2, dx3, g_ffn2, _row(mod, 7), fo, _row(mod, 5))

    dgab, dyy = _merge_bwd(do, plan.weight("o"), z, o_ga, yy, plan)
    plan.partial("o", _mm_tn("mix_dwo", merged, do, (1024,), (512,), plan))
    dmixed = _mm_up_t("pool_up_t", dyy, 0, plan.weight("pool_up"))
    dattn = _mm_up_t("attn_up_t", dyy, 1, plan.weight("attn_up"))
    plan.partial("pool_up", _mm_up_dw("pool_up_dw", mixed, dyy, 0))
    plan.partial("attn_up", _mm_up_dw("attn_up_dw", attn, dyy, 1))
    dpooled, dpm, dps = _pool_mix_bwd(pooled, plan.weight("pool_mix"), pool_scale, dmixed)
    plan.partial("pool_mix", dpm)
    du_pool = _pool_bwd(dpooled)
    dq, dkp, dkc, dvp, dvc, dl, dsink, dgain = _attn_bwd(z, o_q, o_k, qg2, kg2, sink_rows, bias, dattn, plan)
    dk, dv = _kv_combine(dkp, dkc, dvp, dvc)
    drb = _rel_bias_grad(dl.reshape(N_KV, 2 * BLK, 8, BLK).transpose(0, 2, 3, 1).reshape(N_HEADS, BLK * 2 * BLK))
    dz = jnp.concatenate([du_pool, dq, dk, dv, dgab[0], dgab[1]], axis=1)
    plan.partial("w_in", _mm_tn("mix_dwin", dz, h2, (1280, 256), (512,), plan))
    tnd = _pick(D, (512,))
    dh2 = _mm("mix_dh", (S // tmz, D // tnd, 1), NN, dz, pl.BlockSpec((tmz, IN_W), lambda i, j, k: (i, 0)),
              w_in_t, pl.BlockSpec((IN_W, tnd), lambda i, j, k: (0, j)), [], [], [_sds((S, D), F32)],
              [pl.BlockSpec((tmz, tnd), lambda i, j, k: (i, j))], None, _store, plan)[0]
    dx1, df1, acc2 = _rms_mod_bwd("rms_mod_bwd2", dh2, x1, dx2, g_mix, _row(mod, 4), f1, _row(half, 2))

    dgu1 =_ffn_dact("ffn1_dact", df1, plan.weight("down1"), gu1, plan)
    plan.partial("gu1", _ffn_dwgu("ffn1_dwgu", h1, dgu1, plan))
    plan.partial("down1", _mm_tn("ffn1_dwd", act1, df1, (1408, 512), (512,), plan))
    dh1 = _ffn_dh("ffn1_dh", dgu1, plan.weight("gu1"), plan)
    grad_x, acc1 = _rms_mod_bwd("rms_mod_bwd1", dh1, x, dx1, g_ffn1, _row(mod, 1), plan=plan)

    dmod = jnp.concatenate([_row(acc1, 0), _row(acc1, 1), 0.5 * _row(acc2, 3),
                            _row(acc2, 0), _row(acc2, 1), _row(acc3, 3),
                            _row(acc3, 0), _row(acc3, 1), 0.5 * _row(loss_acc, 1)], axis=0)
    fold = lambda r: r[:, :HEAD_DIM] + r[:, HEAD_DIM:]
    small = dict(
        dmod=dmod, g_ffn1=_row(acc1, 2), g_mix=_row(acc2, 2), g_ffn2=_row(acc3, 2), pool_scale=_row(dps, 0),
        q_gain=fold(_row(dgain, 0)), k_gain=fold(_row(dgain, 1)),
        sinks=jnp.sum(dsink.reshape(N_HEADS, BLK), axis=1).reshape(1, N_HEADS), rel_bias=drb,
        loss=(0.5 / D) * jnp.sum(_row(loss_acc, 0)).reshape(1, 1))
    return grad_x, small


SMALL_ORDER = ("dmod", "g_ffn1", "g_mix", "g_ffn2", "pool_scale", "q_gain", "k_gain", "sinks", "rel_bias", "loss")


def _pack_small(vals):
    flat = jnp.concatenate([vals[k].reshape(-1) for k in SMALL_ORDER])
    n = flat.shape[0]
    rows = -(-n // (8 * LANES)) * 8
    return jnp.pad(flat, (0, rows * LANES - n)).reshape(rows, LANES)


def _unpack_small(packed, like):
    flat = packed.reshape(-1)
    out, off = {}, 0
    for k in SMALL_ORDER:
        n = int(np.prod(like[k].shape))
        out[k] = flat[off:off + n].reshape(like[k].shape)
        off += n
    return out


def kernel(x, c, w_ada, b_ada, g_ffn1, w_ffn1_gu, w_ffn1_down, g_mix, w_in, pool_mix, pool_scale, w_pool_up, q_gain, k_gain, sinks, rel_bias, w_attn_up, w_o, g_ffn2, w_ffn2_gu, w_ffn2_down, loss_target, m_w_ada, m_b_ada, m_g_ffn1, m_w_ffn1_gu, m_w_ffn1_down, m_g_mix, m_w_in, m_pool_mix, m_pool_scale, m_w_pool_up, m_q_gain, m_k_gain, m_sinks, m_rel_bias, m_w_attn_up, m_w_o, m_g_ffn2, m_w_ffn2_gu, m_w_ffn2_down, v_w_ada, v_b_ada, v_g_ffn1, v_w_ffn1_gu, v_w_ffn1_down, v_g_mix, v_w_in, v_pool_mix, v_pool_scale, v_w_pool_up, v_q_gain, v_k_gain, v_sinks, v_rel_bias, v_w_attn_up, v_w_o, v_g_ffn2, v_w_ffn2_gu, v_w_ffn2_down):
    S, D = x.shape[1], x.shape[2]
    gw = pool_mix.shape[3]
    r = 2 * lax.axis_index("x") + lax.axis_index("y")

    two_d = lambda a: a.reshape(-1, a.shape[-1])
    w_sh = dict(gu1=w_ffn1_gu, down1=w_ffn1_down, w_in=w_in, pool_mix=pool_mix, pool_up=w_pool_up, attn_up=w_attn_up, o=w_o,
                gu2=w_ffn2_gu, down2=w_ffn2_down)
    m_sh = dict(gu1=m_w_ffn1_gu, down1=m_w_ffn1_down, w_in=m_w_in, pool_mix=m_pool_mix, pool_up=m_w_pool_up, attn_up=m_w_attn_up,
                o=m_w_o, gu2=m_w_ffn2_gu, down2=m_w_ffn2_down)
    v_sh = dict(gu1=v_w_ffn1_gu, down1=v_w_ffn1_down, w_in=v_w_in, pool_mix=v_pool_mix, pool_up=v_w_pool_up, attn_up=v_w_attn_up,
                o=v_w_o, gu2=v_w_ffn2_gu, down2=v_w_ffn2_down)
    view = lambda k, a: two_d(a).T if k == "w_in" else two_d(a)
    unview = lambda k, a: (a.T if k == "w_in" else a).reshape(w_sh[k].shape)
    w2 = {k: view(k, w_sh[k]) for k in BIG}
    full = {k: _cast_into_slot("cast_" + k, w2[k]) for k in BIG}
    plan = _Plan(w2, {k: view(k, m_sh[k]) for k in BIG}, {k: view(k, v_sh[k]) for k in BIG}, full, D, gw)

    c_all = _gather_all("gather_c", jnp.broadcast_to(c, (8, D)))[:, 0, :]
    cols = w_ada.shape[2]
    b_sh = lax.dynamic_slice(b_ada, (0, r * cols), (1, cols))
    mod_cols = _mod_fwd(c_all, w_ada[0], b_sh)
    mod_all = _chip_exchange("mod_exchange", mod_cols)
    me = 4 * lax.axis_index("x") + 2 * lax.axis_index("y") + lax.axis_index("c")
    mod = lax.dynamic_slice(mod_all, (0, me, 0), (N_CHIPS, 1, cols)).reshape(9, D)

    grad_x, small = _local_step(x[0], loss_target[0], mod, g_ffn1, g_mix, g_ffn2, pool_scale, q_gain, k_gain,
                                sinks, rel_bias, plan)

    small_w = dict(dmod=b_ada, g_ffn1=g_ffn1, g_mix=g_mix, g_ffn2=g_ffn2, pool_scale=pool_scale, q_gain=q_gain, k_gain=k_gain,
                   sinks=sinks, rel_bias=rel_bias, loss=jnp.zeros((1, 1), F32))
    small_m = dict(dmod=m_b_ada, g_ffn1=m_g_ffn1, g_mix=m_g_mix, g_ffn2=m_g_ffn2, pool_scale=m_pool_scale, q_gain=m_q_gain,
                   k_gain=m_k_gain, sinks=m_sinks, rel_bias=m_rel_bias, loss=jnp.zeros((1, 1), F32))
    small_v = dict(dmod=v_b_ada, g_ffn1=v_g_ffn1, g_mix=v_g_mix, g_ffn2=v_g_ffn2, pool_scale=v_pool_scale, q_gain=v_q_gain,
                   k_gain=v_k_gain, sinks=v_sinks, rel_bias=v_rel_bias, loss=jnp.ones((1, 1), F32))
    small_all = _gather_all("gather_small", _pack_small(small))
    sg, sd, sm, sv = [_unpack_small(a, small_w) for a in
                      _small_finish(small_all, _pack_small(small_w), _pack_small(small_m), _pack_small(small_v))]
    loss = sg["loss"].reshape(())

    dmod_all = small_all.reshape(N_DEV, -1)[:, :9 * D]
    dmod_sh = lax.dynamic_slice(dmod_all, (0, r * cols), (N_DEV, cols))
    g_ada, d_ada, nm_ada, nv_ada = _wada_bwd(c_all, dmod_sh, w_ada[0], m_w_ada[0], v_w_ada[0])
    plan.alone("+join_down1")

    big = [{k: unview(k, plan.result[k][i]) for k in BIG} for i in range(4)]

    def ordered(b, ada, sm_):
        return (ada[None], sm_["dmod"], sm_["g_ffn1"], b["gu1"], b["down1"], sm_["g_mix"], b["w_in"], b["pool_mix"],
                sm_["pool_scale"], b["pool_up"], sm_["q_gain"], sm_["k_gain"], sm_["sinks"], sm_["rel_bias"], b["attn_up"],
                b["o"], sm_["g_ffn2"], b["gu2"], b["down2"])

    return (loss, grad_x[None], *ordered(big[0], g_ada, sg), *ordered(big[1], d_ada, sd), *ordered(big[2], nm_ada, sm),
            *ordered(big[3], nv_ada, sv))
```

```python
import numpy as np
import jax
import jax.numpy as jnp
from jax import lax
from jax.experimental import pallas as pl
from jax.experimental.pallas import tpu as pltpu

BF = jnp.bfloat16
F32 = jnp.float32
MESH = pl.DeviceIdType.MESH

EPS = 1e-6
NEG_INF = -1e30
HEAD_DIM = 64
N_HEADS = 16
N_KV = 2
ATT_W = N_HEADS * HEAD_DIM
KV_W = N_KV * HEAD_DIM
BLK = 128
NUM_BUCKETS = 32
POOL_MAX_W = 16
N_CHIPS = 4
N_DEV = 8
LANES = 128
ADAM_LR, ADAM_B1, ADAM_B2, ADAM_EPS, ADAM_WD, ADAM_STEP = 0.001, 0.9, 0.999, 1e-08, 0.01, 10
VMEM_LIMIT = 52 * 1024 * 1024
ANY = pl.BlockSpec(memory_space=pl.ANY)


def _pick(dim, prefs):
    for p in prefs:
        if p <= dim and dim % p == 0:
            return p
    return dim


def _row_tile(rows, cap):
    return max(t for t in range(16, min(rows, cap) + 1, 16) if rows % t == 0)


def _sds(shape, dtype):
    return jax.ShapeDtypeStruct(tuple(shape), dtype)


def _place():
    return lax.axis_index("x"), lax.axis_index("y"), lax.axis_index("c")


def _other_chips(x, y):
    return [(1 - x, y), (x, 1 - y), (1 - x, 1 - y)]


def _chip_of(chip):
    return 2 * chip[0] + chip[1]


def _half_rows(ref, lead, cc, h):
    return ref.at[lead, pl.ds(pl.multiple_of(cc * h, 16), h), :]


class _Stage:
    def __init__(self, bufs, outs, alias, n_sem, start, wait):
        self.bufs, self.outs, self.alias, self.n_sem, self.start, self.wait = bufs, outs, alias, n_sem, start, wait


def _stage_plumbing(stages, n_in0, n_out0):
    bufs, outs, aliases, spans, scratch = [], [], {}, [], []
    for st in stages:
        i0, o0 = len(bufs), len(outs)
        bufs += list(st.bufs)
        outs += list(st.outs)
        for a, b in st.alias.items():
            aliases[n_in0 + i0 + a] = n_out0 + o0 + b
        spans.append((i0, len(bufs), o0, len(outs)))
        scratch += [pltpu.SemaphoreType.DMA((st.n_sem,)), pltpu.SemaphoreType.DMA((st.n_sem,))]

    def run(which, in_refs, out_refs, sem_refs):
        for s, st in enumerate(stages):
            i0, i1, o0, o1 = spans[s]
            getattr(st, which)(in_refs[i0:i1], out_refs[o0:o1], sem_refs[2 * s], sem_refs[2 * s + 1])

    def split(flat):
        return [list(flat[o0:o1]) for (_, _, o0, o1) in spans]

    return bufs, outs, aliases, scratch, run, split


def _run_stages(name, stages):
    bufs, outs, aliases, scratch, run, split = _stage_plumbing(stages, 0, 0)
    ni, no = len(bufs), len(outs)

    def body(*refs):
        ins, os_, sems = refs[:ni], refs[ni:ni + no], refs[ni + no:]
        run("start", ins, os_, sems)
        run("wait", ins, os_, sems)

    res = pl.pallas_call(body, in_specs=[ANY] * ni, out_specs=[ANY] * no, out_shape=outs, input_output_aliases=aliases,
                         scratch_shapes=scratch, name=name)(*bufs)
    return split(res)


def _call(name, body, grid, in_specs, out_specs, out_shape, args, scratch=(), sem=None, plan=None):
    stages = plan.stages(name) if plan is not None else []
    n_in, n_out, n_scr = len(args), len(out_shape), len(scratch)
    if not stages:
        return pl.pallas_call(body, grid=grid, in_specs=list(in_specs), out_specs=list(out_specs), out_shape=list(out_shape),
                              scratch_shapes=list(scratch), name=name,
                              compiler_params=pltpu.CompilerParams(dimension_semantics=sem, vmem_limit_bytes=VMEM_LIMIT))(*args)
    bufs, s_outs, aliases, s_scratch, run, split = _stage_plumbing(stages, n_in, n_out)
    nb, nso = len(bufs), len(s_outs)

    def hosted(*refs):
        ins = refs[:n_in]
        s_ins = refs[n_in:n_in + nb]
        outs = refs[n_in + nb:n_in + nb + n_out]
        s_os = refs[n_in + nb + n_out:n_in + nb + n_out + nso]
        scr = refs[n_in + nb + n_out + nso:n_in + nb + n_out + nso + n_scr]
        sems = refs[n_in + nb + n_out + nso + n_scr:]
        first = pl.program_id(0) == 0
        last = pl.program_id(0) == grid[0] - 1
        for d in range(1, len(grid)):
            first = first & (pl.program_id(d) == 0)
            last = last & (pl.program_id(d) == grid[d] - 1)

        @pl.when(first)
        def _():
            run("start", s_ins, s_os, sems)

        body(*ins, *outs, *scr)

        @pl.when(last)
        def _():
            run("wait", s_ins, s_os, sems)

    res = pl.pallas_call(
        hosted, grid=grid, in_specs=list(in_specs) + [ANY] * nb, out_specs=list(out_specs) + [ANY] * nso,
        out_shape=list(out_shape) + s_outs, input_output_aliases=aliases, scratch_shapes=list(scratch) + s_scratch, name=name,
        compiler_params=pltpu.CompilerParams(dimension_semantics=("arbitrary",) * len(grid), vmem_limit_bytes=VMEM_LIMIT))(*args, *bufs)
    plan.done(name, split(res[n_out:]))
    return list(res[:n_out])


def _gather_ici_stage(fulls):
    n = len(fulls)

    def copy(i, j, slot, ins, outs, send, recv):
        x, y, c = _place()
        chip = _other_chips(x, y)[j]
        h = fulls[i].shape[1] // 2
        s = 3 * i + j
        return pltpu.make_async_remote_copy(_half_rows(ins[i], 2 * x + y, c, h), _half_rows(outs[i], slot(x, y, chip), c, h),
                                            send.at[s], recv.at[s], device_id=(*chip, c), device_id_type=MESH)

    mine = lambda x, y, chip: 2 * x + y
    theirs = lambda x, y, chip: _chip_of(chip)

    def start(ins, outs, send, recv):
        for i in range(n):
            for j in range(3):
                copy(i, j, mine, ins, outs, send, recv).start()

    def wait(ins, outs, send, recv):
        for i in range(n):
            for j in range(3):
                copy(i, j, theirs, ins, outs, send, recv).wait_recv()
        for i in range(n):
            for j in range(3):
                copy(i, j, mine, ins, outs, send, recv).wait_send()

    return _Stage(fulls, [_sds(f.shape, f.dtype) for f in fulls], {i: i for i in range(n)}, 3 * n, start, wait)


def _gather_d2d_stage(fulls):
    n = len(fulls)

    def copy(i, j, cc, ins, outs, send, recv):
        x, y, c = _place()
        rj = _chip_of(_other_chips(x, y)[j])
        h = fulls[i].shape[1] // 2
        half = cc(c)
        s = 3 * i + j
        return pltpu.make_async_remote_copy(_half_rows(ins[i], rj, half, h), _half_rows(outs[i], rj, half, h),
                                            send.at[s], recv.at[s], device_id=(x, y, 1 - c), device_id_type=MESH)

    mine = lambda c: c
    theirs = lambda c: 1 - c

    def start(ins, outs, send, recv):
        for i in range(n):
            for j in range(3):
                copy(i, j, mine, ins, outs, send, recv).start()

    def wait(ins, outs, send, recv):
        for i in range(n):
            for j in range(3):
                copy(i, j, theirs, ins, outs, send, recv).wait_recv()
        for i in range(n):
            for j in range(3):
                copy(i, j, mine, ins, outs, send, recv).wait_send()

    return _Stage(fulls, [_sds(f.shape, f.dtype) for f in fulls], {i: i for i in range(n)}, 3 * n, start, wait)


def _split_stage(parts):
    n = len(parts)

    def copy(i, ins, outs, send, recv):
        x, y, c = _place()
        h = parts[i].shape[1] // 2
        return pltpu.make_async_remote_copy(_half_rows(ins[i], slice(None), 1 - c, h), outs[i], send.at[i], recv.at[i],
                                            device_id=(x, y, 1 - c), device_id_type=MESH)

    def start(ins, outs, send, recv):
        for i in range(n):
            copy(i, ins, outs, send, recv).start()

    def wait(ins, outs, send, recv):
        for i in range(n):
            copy(i, ins, outs, send, recv).wait_recv()
        for i in range(n):
            copy(i, ins, outs, send, recv).wait_send()

    return _Stage(parts, [_sds((N_CHIPS, p.shape[1] // 2, p.shape[2]), p.dtype) for p in parts], {}, n, start, wait)


def _owners_stage(sums):
    n = len(sums)

    def copy(i, j, mine, ins, outs, send, recv):
        x, y, c = _place()
        chip = _other_chips(x, y)[j]
        slot = (2 * x + y) if mine else _chip_of(chip)
        return pltpu.make_async_remote_copy(ins[i].at[_chip_of(chip)], outs[i].at[slot], send.at[3 * i + j], recv.at[3 * i + j],
                                            device_id=(*chip, c), device_id_type=MESH)

    def start(ins, outs, send, recv):
        for i in range(n):
            for j in range(3):
                copy(i, j, True, ins, outs, send, recv).start()

    def wait(ins, outs, send, recv):
        for i in range(n):
            for j in range(3):
                copy(i, j, False, ins, outs, send, recv).wait_recv()
        for i in range(n):
            for j in range(3):
                copy(i, j, True, ins, outs, send, recv).wait_send()

    return _Stage(sums, [_sds(s.shape, s.dtype) for s in sums], {}, 3 * n, start, wait)


def _join_stage(gs):
    n = len(gs)

    def copy(i, mine, ins, outs, send, recv):
        x, y, c = _place()
        slot = c if mine else 1 - c
        return pltpu.make_async_remote_copy(ins[i].at[slot], outs[i].at[slot], send.at[i], recv.at[i],
                                            device_id=(x, y, 1 - c), device_id_type=MESH)

    def start(ins, outs, send, recv):
        for i in range(n):
            copy(i, True, ins, outs, send, recv).start()

    def wait(ins, outs, send, recv):
        for i in range(n):
            copy(i, False, ins, outs, send, recv).wait_recv()
        for i in range(n):
            copy(i, True, ins, outs, send, recv).wait_send()

    return _Stage(gs, [_sds(g.shape, g.dtype) for g in gs], {i: i for i in range(n)}, n, start, wait)


def _chip_exchange(name, arr):
    def body(src, dst, send, recv, loc):
        x, y, c = _place()
        r = 2 * x + y
        chips = _other_chips(x, y)

        def cp(j, slot):
            return pltpu.make_async_remote_copy(src, dst.at[slot], send.at[j], recv.at[j], device_id=(*chips[j], c), device_id_type=MESH)

        mine = pltpu.make_async_copy(src, dst.at[r], loc)
        mine.start()
        for j in range(3):
            cp(j, r).start()
        for j in range(3):
            cp(j, _chip_of(chips[j])).wait_recv()
        for j in range(3):
            cp(j, r).wait_send()
        mine.wait()

    return pl.pallas_call(body, in_specs=[ANY], out_specs=ANY, out_shape=_sds((N_CHIPS, *arr.shape), arr.dtype),
                          scratch_shapes=[pltpu.SemaphoreType.DMA((3,)), pltpu.SemaphoreType.DMA((3,)), pltpu.SemaphoreType.DMA],
                          name=name)(arr)


def _gather_all(name, arr):
    def body(src, dst, send, recv, loc):
        x, y, c = _place()

        def cp(k, slot_of_me):
            px, py, pc = x ^ ((k >> 2) & 1), y ^ ((k >> 1) & 1), c ^ (k & 1)
            slot = (4 * x + 2 * y + c) if slot_of_me else (4 * px + 2 * py + pc)
            return pltpu.make_async_remote_copy(src, dst.at[slot], send.at[k - 1], recv.at[k - 1],
                                                device_id=(px, py, pc), device_id_type=MESH)

        mine = pltpu.make_async_copy(src, dst.at[4 * x + 2 * y + c], loc)
        mine.start()
        for k in range(1, N_DEV):
            cp(k, True).start()
        for k in range(1, N_DEV):
            cp(k, False).wait_recv()
        for k in range(1, N_DEV):
            cp(k, True).wait_send()
        mine.wait()

    return pl.pallas_call(body, in_specs=[ANY], out_specs=ANY, out_shape=_sds((N_DEV, *arr.shape), arr.dtype),
                          scratch_shapes=[pltpu.SemaphoreType.DMA((N_DEV - 1,)), pltpu.SemaphoreType.DMA((N_DEV - 1,)), pltpu.SemaphoreType.DMA],
                          name=name)(arr)


NN = (((1,), (0,)), ((), ()))
NT = (((1,), (1,)), ((), ()))
TN = (((0,), (0,)), ((), ()))


ALL = slice(None)


def _mm(name, grid, dims, a, a_spec, b, b_spec, extras, extra_specs, out_shapes, out_specs, acc_shape, epilogue, plan=None):
    n_k = grid[2]
    n_e = len(extras)
    n_o = len(out_shapes)

    def body(*refs):
        a_ref, b_ref = refs[0], refs[1]
        e_refs = refs[2:2 + n_e]
        o_refs = refs[2 + n_e:2 + n_e + n_o]
        p = lax.dot_general(a_ref[...].astype(BF), b_ref[...].astype(BF), dims, preferred_element_type=F32)
        if n_k == 1:
            epilogue(p, e_refs, o_refs, ALL)
        else:
            acc = refs[-1]
            k = pl.program_id(2)

            @pl.when(k == 0)
            def _():
                acc[...] = p

            @pl.when(k > 0)
            def _():
                acc[...] += p

            @pl.when(k == n_k - 1)
            def _():
                epilogue(acc[...], e_refs, o_refs, ALL)

    scratch = [] if n_k == 1 else [pltpu.VMEM(acc_shape, F32)]
    return _call(name, body, grid, [a_spec, b_spec, *extra_specs], out_specs, out_shapes, [a, b, *extras], scratch,
                 ("parallel", "parallel", "arbitrary"), plan)


def _store(p, e, o, rs):
    o[0][rs, :] = p.astype(o[0].dtype)


def _rms_mod_fwd(name, x, gain, shift, scale, plan=None):
    S, D = x.shape
    ts = _pick(S, (512,))

    def body(x_ref, g_ref, sh_ref, sc_ref, h_ref):
        xv = x_ref[...]
        r = lax.rsqrt(jnp.mean(xv * xv, axis=-1, keepdims=True) + EPS)
        n = xv * r * g_ref[...]
        h_ref[...] = (n * (1.0 + sc_ref[...]) + sh_ref[...]).astype(BF)

    row = pl.BlockSpec((ts, D), lambda i: (i, 0))
    vec = pl.BlockSpec((1, D), lambda i: (0, 0))
    return _call(name, body, (S // ts,), [row, vec, vec, vec], [row], [_sds((S, D), BF)], [x, gain, shift, scale],
                 sem=("parallel",), plan=plan)[0]


def _acc_rows(acc_ref, first, part):
    @pl.when(first)
    def _():
        acc_ref[...] = part

    @pl.when(jnp.logical_not(first))
    def _():
        acc_ref[...] += part


def _rms_mod_bwd(name, dh, x, dres, gain, scale, f=None, coef=None, plan=None):
    S, D = x.shape
    ts = _pick(S, (256,))
    gated = f is not None

    def body(dh_ref, x_ref, dr_ref, g_ref, sc_ref, *rest):
        xv = x_ref[...]
        dhv = dh_ref[...]
        g = g_ref[...]
        r = lax.rsqrt(jnp.mean(xv * xv, axis=-1, keepdims=True) + EPS)
        xhat = xv * r
        dn = dhv * (1.0 + sc_ref[...])
        dxhat = dn * g
        dx = dr_ref[...] + r * (dxhat - xhat * jnp.mean(dxhat * xhat, axis=-1, keepdims=True))
        rows = [jnp.sum(dhv, axis=0, keepdims=True), jnp.sum(dhv * (xhat * g), axis=0, keepdims=True),
                jnp.sum(dn * xhat, axis=0, keepdims=True)]
        if gated:
            f_ref, c_ref, dx_ref, df_ref, acc_ref = rest
            df_ref[...] = (dx * c_ref[...]).astype(BF)
            rows.append(jnp.sum(dx * f_ref[...].astype(F32), axis=0, keepdims=True))
        else:
            dx_ref, acc_ref = rest
        dx_ref[...] = dx
        _acc_rows(acc_ref, pl.program_id(0) == 0, jnp.concatenate(rows + [jnp.zeros((8 - len(rows), D), F32)], axis=0))

    row = pl.BlockSpec((ts, D), lambda i: (i, 0))
    vec = pl.BlockSpec((1, D), lambda i: (0, 0))
    acc = pl.BlockSpec((8, D), lambda i: (0, 0))
    if gated:
        return _call(name, body, (S // ts,), [row, row, row, vec, vec, row, vec], [row, row, acc],
                     [_sds((S, D), F32), _sds((S, D), BF), _sds((8, D), F32)], [dh, x, dres, gain, scale, f, coef],
                     sem=("arbitrary",), plan=plan)
    return _call(name, body, (S // ts,), [row, row, row, vec, vec], [row, acc],
                 [_sds((S, D), F32), _sds((8, D), F32)], [dh, x, dres, gain, scale], sem=("arbitrary",), plan=plan)


def _loss_bwd(x3, target, f, coef):
    S, D = x3.shape
    ts = _pick(S, (512,))

    def body(x_ref, t_ref, f_ref, c_ref, dx_ref, df_ref, acc_ref):
        e = x_ref[...] - t_ref[...]
        dx = e * (1.0 / D)
        dx_ref[...] = dx
        df_ref[...] = (dx * c_ref[...]).astype(BF)
        part = jnp.concatenate([jnp.sum(e * e, axis=0, keepdims=True), jnp.sum(dx * f_ref[...].astype(F32), axis=0, keepdims=True),
                                jnp.zeros((6, D), F32)], axis=0)
        _acc_rows(acc_ref, pl.program_id(0) == 0, part)

    row = pl.BlockSpec((ts, D), lambda i: (i, 0))
    return _call("loss_bwd", body, (S // ts,), [row, row, row, pl.BlockSpec((1, D), lambda i: (0, 0))],
                 [row, row, pl.BlockSpec((8, D), lambda i: (0, 0))],
                 [_sds((S, D), F32), _sds((S, D), BF), _sds((8, D), F32)], [x3, target, f, coef], sem=("arbitrary",))


def _silu_parts(g):
    s = jax.nn.sigmoid(g)
    return s, g * s


def _ffn_up(name, h, wgu4, plan=None):
    S, D = h.shape
    SH = wgu4.shape[2]
    F = 2 * SH
    tm = _pick(S, (512,))
    tn = _pick(SH, (1408, 256))
    nts = SH // tn

    def body(h_ref, wg_ref, wu_ref, gu_ref, act_ref):
        hv = h_ref[...]
        g = jnp.dot(hv, wg_ref[...], preferred_element_type=F32)
        u = jnp.dot(hv, wu_ref[...], preferred_element_type=F32)
        gu_ref[0] = g.astype(BF)
        gu_ref[1] = u.astype(BF)
        act_ref[...] = (_silu_parts(g)[1] * u).astype(BF)

    return _call(name, body, (S // tm, F // tn),
                 [pl.BlockSpec((tm, D), lambda i, j: (i, 0)),
                  pl.BlockSpec((None, D, tn), lambda i, j: (j // nts, 0, j % nts)),
                  pl.BlockSpec((None, D, tn), lambda i, j: (2 + j // nts, 0, j % nts))],
                 [pl.BlockSpec((2, tm, tn), lambda i, j: (0, i, j)), pl.BlockSpec((tm, tn), lambda i, j: (i, j))],
                 [_sds((2, S, F), BF), _sds((S, F), BF)], [h, wgu4, wgu4], sem=("parallel", "parallel"), plan=plan)


def _mm_residual(name, a, w, x_in, coef, plan=None):
    S, K = a.shape
    D = w.shape[1]
    tm = _pick(S, (1024,))
    tn = _pick(D, (512,))
    tk = K

    def epi(p, e, o, rs):
        o[0][rs, :] = e[0][rs, :] + e[1][...] * p
        o[1][rs, :] = p.astype(BF)

    tile = pl.BlockSpec((tm, tn), lambda i, j, k: (i, j))
    return _mm(name, (S // tm, D // tn, K // tk), NN,
               a, pl.BlockSpec((tm, tk), lambda i, j, k: (i, k)),
               w, pl.BlockSpec((tk, tn), lambda i, j, k: (k, j)),
               [x_in, coef], [tile, pl.BlockSpec((1, tn), lambda i, j, k: (0, j))],
               [_sds((S, D), F32), _sds((S, D), BF)], [tile, tile], (tm, tn), epi, plan)


def _ffn_dact(name, df, wd, gu, plan=None):
    S, D = df.shape
    F = wd.shape[0]
    tm = _pick(S, (512,))
    tn = _pick(F, (1408, 256))

    def epi(p, e, o, rs):
        g = e[0][0, rs, :].astype(F32)
        u = e[0][1, rs, :].astype(F32)
        s, sg = _silu_parts(g)
        o[0][0, rs, :] = (p * u * (s * (1.0 + g * (1.0 - s)))).astype(BF)
        o[0][1, rs, :] = (p * sg).astype(BF)

    pair = pl.BlockSpec((2, tm, tn), lambda i, j, k: (0, i, j))
    return _mm(name, (S // tm, F // tn, 1), NT,
               df, pl.BlockSpec((tm, D), lambda i, j, k: (i, 0)),
               wd, pl.BlockSpec((tn, D), lambda i, j, k: (j, 0)),
               [gu], [pair], [_sds((2, S, F), BF)], [pair], None, epi, plan)[0]


def _ffn_dh(name, dgu, wgu4, plan=None):
    _, S, F = dgu.shape
    _, D, SH = wgu4.shape
    tm = _pick(S, (1024,))
    tn = _pick(D, (512,))

    def body(a_ref, b_ref, o_ref, acc):
        k = pl.program_id(2)
        p = lax.dot_general(a_ref[:, :SH], b_ref[0], NT, preferred_element_type=F32)
        p = p + lax.dot_general(a_ref[:, SH:], b_ref[1], NT, preferred_element_type=F32)

        @pl.when(k == 0)
        def _():
            acc[...] = p

        @pl.when(k == 1)
        def _():
            o_ref[...] = acc[...] + p

    return _call(name, body, (S // tm, D // tn, 2),
                 [pl.BlockSpec((None, tm, F), lambda i, j, k: (k, i, 0)), pl.BlockSpec((2, tn, SH), lambda i, j, k: (k, j, 0))],
                 [pl.BlockSpec((tm, tn), lambda i, j, k: (i, j))], [_sds((S, D), F32)], [dgu, wgu4],
                 [pltpu.VMEM((tm, tn), F32)], ("parallel", "parallel", "arbitrary"), plan)[0]


def _ffn_dwgu(name, h, dgu, plan=None):
    _, S, F = dgu.shape
    D = h.shape[1]
    SH = F // 2
    tk1 = _pick(D, (512,))
    tn = _pick(SH, (1408, 256))
    ts = _pick(S, (4096, 1024))
    npj = F // tn
    nsj = SH // tn
    return _mm(name, (D // tk1, 2 * npj, S // ts), TN,
               h, pl.BlockSpec((ts, tk1), lambda i, j, k: (k, i)),
               dgu, pl.BlockSpec((None, ts, tn), lambda i, j, k: (j // npj, k, j % npj)),
               [], [], [_sds((4, D, SH), BF)],
               [pl.BlockSpec((None, tk1, tn), lambda i, j, k: (j // nsj, i, j % nsj))], (tk1, tn), _store, plan)[0]


def _mm_tn(name, a, b, tk1_prefs, tn_prefs, plan=None):
    S, K1 = a.shape
    N = b.shape[1]
    tk1 = _pick(K1, tk1_prefs)
    tn = _pick(N, tn_prefs)
    ts = _pick(S, (4096, 1024))
    return _mm(name, (K1 // tk1, N // tn, S // ts), TN,
               a, pl.BlockSpec((ts, tk1), lambda i, j, k: (k, i)),
               b, pl.BlockSpec((ts, tn), lambda i, j, k: (k, j)),
               [], [], [_sds((K1, N), BF)], [pl.BlockSpec((tk1, tn), lambda i, j, k: (i, j))], (tk1, tn), _store, plan)[0]


def _pool_window(ext, w, back):
    n = ext.shape[0]
    s = ext
    for step in (1, 2, 4, 8):
        sh = pltpu.roll(s, (n - step) if back else step, axis=0)
        s = jnp.where(w > step, s + sh, s)
    return s


def _pool_fwd(z, PW):
    S = z.shape[0]
    tc = _pick(S, (1024,))
    bpg = (PW // 4) // LANES
    H = POOL_MAX_W

    def body(prev_ref, u_ref, o_ref):
        i = pl.program_id(0)
        j = pl.program_id(1)
        w = lax.shift_left(jnp.int32(2), j // bpg)
        u = u_ref[...]
        prev = jnp.where(i > 0, prev_ref[...], 0.0)
        s = _pool_window(jnp.concatenate([prev, u], axis=0), w, False)[H:]
        t = i * tc + lax.broadcasted_iota(jnp.int32, (tc, LANES), 0)
        cnt = jnp.minimum(t + 1, w).astype(F32)
        o_ref[...] = (s / cnt - u).astype(BF)

    r = tc // H
    return _call("pool_fwd", body, (S // tc, PW // LANES),
                 [pl.BlockSpec((H, LANES), lambda i, j: (jnp.maximum(i * r - 1, 0), j)),
                  pl.BlockSpec((tc, LANES), lambda i, j: (i, j))],
                 [pl.BlockSpec((tc, LANES), lambda i, j: (i, j))], [_sds((S, PW), BF)], [z, z], sem=("parallel", "parallel"))[0]


def _pool_bwd(dpooled):
    S, PW = dpooled.shape
    tc = _pick(S, (1024,))
    bpg = (PW // 4) // LANES
    H = POOL_MAX_W
    last = S // tc - 1

    def body(dp_ref, nxt_ref, o_ref):
        i = pl.program_id(0)
        j = pl.program_id(1)
        w = lax.shift_left(jnp.int32(2), j // bpg)
        dp = dp_ref[...]
        nxt = jnp.where(i < last, nxt_ref[...], 0.0)
        ext = jnp.concatenate([dp, nxt], axis=0)
        t = i * tc + lax.broadcasted_iota(jnp.int32, (tc + H, LANES), 0)
        cnt = jnp.minimum(t + 1, w).astype(F32)
        s = _pool_window(ext / cnt, w, True)[:tc]
        o_ref[...] = (s - dp).astype(BF)

    r = tc // H
    nh = S // H - 1
    return _call("pool_bwd", body, (S // tc, PW // LANES),
                 [pl.BlockSpec((tc, LANES), lambda i, j: (i, j)),
                  pl.BlockSpec((H, LANES), lambda i, j: (jnp.minimum((i + 1) * r, nh), j))],
                 [pl.BlockSpec((tc, LANES), lambda i, j: (i, j))], [_sds((S, PW), BF)], [dpooled, dpooled],
                 sem=("parallel", "parallel"))[0]


def _pool_mix(pooled, pm, scale):
    S, PW = pooled.shape
    gw = PW // 4
    ts = _pick(S, (1024,))

    def epi(p, e, o, rs):
        o[0][rs, :] = (p * e[0][...]).astype(BF)

    tile = pl.BlockSpec((ts, gw), lambda i, j, k: (i, j))
    return _mm("pool_mix", (S // ts, 4, 1), NN, pooled, tile,
               pm, pl.BlockSpec((None, gw, gw), lambda i, j, k: (j, 0, 0)),
               [scale], [pl.BlockSpec((1, gw), lambda i, j, k: (0, j))], [_sds((S, PW), BF)], [tile], None, epi)[0]


def _pool_mix_bwd(pooled, pm, scale, dmixed):
    S, PW = pooled.shape
    gw = PW // 4
    ts = _pick(S, (1024,))

    def body(p_ref, pm_ref, sc_ref, dm_ref, dp_ref, dpm_ref, dsc_ref):
        i = pl.program_id(1)
        p = p_ref[...]
        w = pm_ref[...]
        dm = dm_ref[...]
        pre = jnp.dot(p, w, preferred_element_type=F32)
        dmp = (dm * sc_ref[...]).astype(BF)
        dp_ref[...] = lax.dot_general(dmp, w, NT, preferred_element_type=F32)
        dw = lax.dot_general(p, dmp, TN, preferred_element_type=F32)
        ds = jnp.concatenate([jnp.sum(dm * pre, axis=0, keepdims=True), jnp.zeros((7, gw), F32)], axis=0)
        _acc_rows(dpm_ref, i == 0, dw)
        _acc_rows(dsc_ref, i == 0, ds)

    tile = pl.BlockSpec((ts, gw), lambda g, i: (i, g))
    return _call("pool_mix_bwd", body, (4, S // ts),
                 [tile, pl.BlockSpec((None, gw, gw), lambda g, i: (g, 0, 0)), pl.BlockSpec((1, gw), lambda g, i: (0, g)), tile],
                 [tile, pl.BlockSpec((None, gw, gw), lambda g, i: (g, 0, 0)), pl.BlockSpec((8, gw), lambda g, i: (0, g))],
                 [_sds((S, PW), F32), _sds((4, gw, gw), F32), _sds((8, PW), F32)], [pooled, pm, scale, dmixed],
                 sem=("parallel", "arbitrary"))


def _bucket_onehot():
    ql = np.arange(BLK)[:, None]
    j = np.arange(2 * BLK)[None, :]
    d = BLK + ql - j
    n = np.clip(d, 0, None)
    nf = np.maximum(n, 1).astype(np.float32)
    max_exact = NUM_BUCKETS // 2
    large = max_exact + (np.log(nf / max_exact) / np.log(BLK / max_exact) * (NUM_BUCKETS - max_exact)).astype(np.int32)
    large = np.minimum(large, NUM_BUCKETS - 1)
    bucket = np.where(n < max_exact, n, large).astype(np.int32)
    valid = (d >= 0) & (d < BLK)
    oh = (bucket[None] == np.arange(NUM_BUCKETS)[:, None, None]) & valid[None]
    return oh.reshape(NUM_BUCKETS, BLK * 2 * BLK)


def _three_bf16(v):
    hi = v.astype(BF)
    r1 = v - hi.astype(F32)
    mid = r1.astype(BF)
    lo = (r1 - mid.astype(F32)).astype(BF)
    return hi, mid, lo


def _bias_table(rel_bias):
    oh = jnp.asarray(_bucket_onehot(), BF)
    tn = 4096

    def body(rb_ref, oh_ref, o_ref):
        o = oh_ref[...]
        hi, mid, lo = _three_bf16(rb_ref[...])
        acc = lax.dot_general(hi, o, TN, preferred_element_type=F32)
        acc = acc + lax.dot_general(mid, o, TN, preferred_element_type=F32)
        acc = acc + lax.dot_general(lo, o, TN, preferred_element_type=F32)
        on_band = jnp.sum(o.astype(F32), axis=0, keepdims=True) > 0.5
        o_ref[...] = jnp.where(on_band, acc, NEG_INF)

    n = oh.shape[1]
    return _call("bias_table", body, (n // tn,),
                 [pl.BlockSpec((NUM_BUCKETS, N_HEADS), lambda i: (0, 0)), pl.BlockSpec((NUM_BUCKETS, tn), lambda i: (0, i))],
                 [pl.BlockSpec((N_HEADS, tn), lambda i: (0, i))], [_sds((N_HEADS, n), F32)], [rel_bias, oh], sem=("parallel",))[0]


def _rel_bias_grad(dl):
    oh = jnp.asarray(_bucket_onehot(), BF)
    n = oh.shape[1]
    tk = 4096

    def body(dl_ref, oh_ref, o_ref):
        o = oh_ref[...]
        hi, mid, lo = _three_bf16(dl_ref[...])
        acc = lax.dot_general(o, hi, NT, preferred_element_type=F32)
        acc = acc + lax.dot_general(o, mid, NT, preferred_element_type=F32)
        acc = acc + lax.dot_general(o, lo, NT, preferred_element_type=F32)
        _acc_rows(o_ref, pl.program_id(0) == 0, acc)

    return _call("rel_bias_grad", body, (n // tk,),
                 [pl.BlockSpec((N_HEADS, tk), lambda i: (0, i)), pl.BlockSpec((NUM_BUCKETS, tk), lambda i: (0, i))],
                 [pl.BlockSpec((NUM_BUCKETS, N_HEADS), lambda i: (0, 0))], [_sds((NUM_BUCKETS, N_HEADS), F32)], [dl, oh],
                 sem=("arbitrary",))[0]


def _lo_half(shape):
    return lax.broadcasted_iota(jnp.int32, shape, 1) < HEAD_DIM


def _half_sum(x, lo):
    s_lo = jnp.sum(jnp.where(lo, x, 0.0), axis=-1, keepdims=True)
    s_hi = jnp.sum(jnp.where(lo, 0.0, x), axis=-1, keepdims=True)
    return jnp.where(lo, s_lo, s_hi)


def _norm2(x, lo):
    r = lax.rsqrt(_half_sum(x * x, lo) * (1.0 / HEAD_DIM) + EPS)
    return x * r, r


def _norm2_bwd(dy, xhat, r, gain, lo):
    dxhat = dy * gain
    dx = r * (dxhat - xhat * (_half_sum(dxhat * xhat, lo) * (1.0 / HEAD_DIM)))
    return dx, dy * xhat


def _swap(x):
    return pltpu.roll(x, HEAD_DIM, axis=1)


def _pair_rows(x, kk):
    return jnp.concatenate([x, _swap(x)] if kk == 0 else [_swap(x), x], axis=0)


def _attn_probs(n, kk, jp0, npairs, zq_ref, K, qg, bias_ref, sink_ref):
    lo_q = _lo_half((BLK, LANES))
    rows, qhats, qrs = [], [], []
    for jp in range(jp0, jp0 + npairs):
        qhat, qr = _norm2(zq_ref[:, jp * LANES:(jp + 1) * LANES], lo_q)
        rows.append(_pair_rows(qhat * qg * (HEAD_DIM ** -0.5), kk))
        qhats.append(qhat)
        qrs.append(qr)
    Q = jnp.concatenate(rows, axis=0).astype(BF)
    cols = slice(jp0 * 2 * BLK, (jp0 + npairs) * 2 * BLK)
    l = lax.dot_general(K, Q, NT, preferred_element_type=F32) + bias_ref[kk, :, cols]
    l = jnp.concatenate([jnp.where(n == 0, NEG_INF, l[:BLK]), l[BLK:]], axis=0)
    sink = sink_ref[kk, :, cols]
    m = jnp.maximum(jnp.max(l, axis=0, keepdims=True), sink)
    e = jnp.exp(l - m)
    es = jnp.exp(sink - m)
    inv = 1.0 / (jnp.sum(e, axis=0, keepdims=True) + es)
    return Q, e * inv, es * inv, qhats, qrs


def _attn_specs(o_q, o_k):
    nq = o_q // 512
    nk = o_k // LANES
    prev = lambda n: (jnp.maximum(n - 1, 0), nk)
    prev_v = lambda n: (jnp.maximum(n - 1, 0), nk + 1)
    return [pl.BlockSpec((BLK, 512), lambda n: (n, nq)), pl.BlockSpec((BLK, 512), lambda n: (n, nq + 1)),
            pl.BlockSpec((BLK, LANES), prev), pl.BlockSpec((BLK, LANES), lambda n: (n, nk)),
            pl.BlockSpec((BLK, LANES), prev_v), pl.BlockSpec((BLK, LANES), lambda n: (n, nk + 1)),
            pl.BlockSpec((1, LANES), lambda n: (0, 0)), pl.BlockSpec((1, LANES), lambda n: (0, 0)),
            pl.BlockSpec((N_KV, 1, 8 * BLK), lambda n: (0, 0, 0)),
            pl.BlockSpec((N_KV, 2 * BLK, 8 * BLK), lambda n: (0, 0, 0))]


def _attn_fwd(z, o_q, o_k, qg2, kg2, sink_rows, bias, plan=None):
    S = z.shape[0]

    def body(zq0, zq1, zkp, zkc, zvp, zvc, qg_ref, kg_ref, sink_ref, bias_ref, o_ref):
        n = pl.program_id(0)
        lo_k = _lo_half((2 * BLK, LANES))
        lo_q = _lo_half((BLK, LANES))
        khat, _ = _norm2(jnp.concatenate([zkp[...], zkc[...]], axis=0), lo_k)
        kn = khat * kg_ref[...]
        vb = jnp.concatenate([zvp[...], zvc[...]], axis=0).astype(BF)
        for kk, zq in enumerate((zq0, zq1)):
            K = jnp.where(lo_k if kk == 0 else jnp.logical_not(lo_k), kn, 0.0).astype(BF)
            for jp in range(4):
                _, p, _, _, _ = _attn_probs(n, kk, jp, 1, zq, K, qg_ref[...], bias_ref, sink_ref)
                r = lax.dot_general(p.astype(BF), vb, TN, preferred_element_type=F32)
                ev, od = r[:BLK], r[BLK:]
                pair = jnp.where(lo_q, ev, _swap(od)) if kk == 0 else jnp.where(lo_q, _swap(ev), od)
                c0 = (4 * kk + jp) * LANES
                o_ref[:, c0:c0 + LANES] = pair.astype(BF)

    return _call("attn_fwd", body, (S // BLK,), _attn_specs(o_q, o_k), [pl.BlockSpec((BLK, ATT_W), lambda n: (n, 0))],
                 [_sds((S, ATT_W), BF)], [z, z, z, z, z, z, qg2, kg2, sink_rows, bias], sem=("parallel",), plan=plan)[0]


def _attn_bwd(z, o_q, o_k, qg2, kg2, sink_rows, bias, dout, plan=None):
    S = z.shape[0]

    def body(zq0, zq1, zkp, zkc, zvp, zvc, qg_ref, kg_ref, sink_ref, bias_ref, do_ref,
             dq_ref, dkp_ref, dkc_ref, dvp_ref, dvc_ref, dl_ref, dsink_ref, dgain_ref):
        n = pl.program_id(0)
        lo_k = _lo_half((2 * BLK, LANES))
        lo_q = _lo_half((BLK, LANES))
        qg = qg_ref[...]
        kg = kg_ref[...]
        khat, kr = _norm2(jnp.concatenate([zkp[...], zkc[...]], axis=0), lo_k)
        kn = khat * kg
        vf = jnp.concatenate([zvp[...], zvc[...]], axis=0)

        @pl.when(n == 0)
        def _():
            dl_ref[...] = jnp.zeros_like(dl_ref)
            dsink_ref[...] = jnp.zeros_like(dsink_ref)
            dgain_ref[...] = jnp.zeros_like(dgain_ref)

        dkn = jnp.zeros((2 * BLK, LANES), F32)
        dvb = jnp.zeros((2 * BLK, LANES), F32)
        dqg = jnp.zeros((1, LANES), F32)
        for kk, zq in enumerate((zq0, zq1)):
            half_k = lo_k if kk == 0 else jnp.logical_not(lo_k)
            K = jnp.where(half_k, kn, 0.0).astype(BF)
            V = jnp.where(half_k, vf, 0.0).astype(BF)
            for jp0 in (0, 2):
                Q, p, ps, qhats, qrs = _attn_probs(n, kk, jp0, 2, zq, K, qg, bias_ref, sink_ref)
                cols = slice(jp0 * 2 * BLK, (jp0 + 2) * 2 * BLK)
                dO = jnp.concatenate([_pair_rows(do_ref[:, (4 * kk + jp) * LANES:(4 * kk + jp + 1) * LANES], kk)
                                      for jp in (jp0, jp0 + 1)], axis=0).astype(BF)
                dP = lax.dot_general(V, dO, NT, preferred_element_type=F32)
                delta = jnp.sum(p * dP, axis=0, keepdims=True)
                dS = p * (dP - delta)
                dsink_ref[kk, :, cols] += -ps * delta
                dl_ref[kk, :, cols] += dS
                dSb = dS.astype(BF)
                dvb = dvb + jnp.where(half_k, jnp.dot(p.astype(BF), dO, preferred_element_type=F32), 0.0)
                dkn = dkn + jnp.where(half_k, jnp.dot(dSb, Q, preferred_element_type=F32), 0.0)
                dQ = lax.dot_general(dSb, K, TN, preferred_element_type=F32) * (HEAD_DIM ** -0.5)
                for j in range(2):
                    ev = dQ[(2 * j) * BLK:(2 * j + 1) * BLK]
                    od = dQ[(2 * j + 1) * BLK:(2 * j + 2) * BLK]
                    dy = (ev + _swap(od)) if kk == 0 else (_swap(ev) + od)
                    dx, gq = _norm2_bwd(dy, qhats[j], qrs[j], qg, lo_q)
                    dqg = dqg + jnp.sum(gq, axis=0, keepdims=True)
                    c0 = (4 * kk + jp0 + j) * LANES
                    dq_ref[:, c0:c0 + LANES] = dx.astype(BF)
        dk, gk = _norm2_bwd(dkn, khat, kr, kg, lo_k)
        dkp_ref[...] = dk[:BLK]
        dkc_ref[...] = dk[BLK:]
        dvp_ref[...] = dvb[:BLK]
        dvc_ref[...] = dvb[BLK:]
        dgain_ref[...] += jnp.concatenate([dqg, jnp.sum(gk, axis=0, keepdims=True), jnp.zeros((6, LANES), F32)], axis=0)

    blk = pl.BlockSpec((BLK, LANES), lambda n: (n, 0))
    wide = pl.BlockSpec((BLK, ATT_W), lambda n: (n, 0))
    return _call(
        "attn_bwd", body, (S // BLK,), _attn_specs(o_q, o_k) + [wide],
        [wide, blk, blk, blk, blk, pl.BlockSpec((N_KV, 2 * BLK, 8 * BLK), lambda n: (0, 0, 0)),
         pl.BlockSpec((N_KV, 1, 8 * BLK), lambda n: (0, 0, 0)), pl.BlockSpec((8, LANES), lambda n: (0, 0))],
        [_sds((S, ATT_W), BF), _sds((S, LANES), F32), _sds((S, LANES), F32), _sds((S, LANES), F32), _sds((S, LANES), F32),
         _sds((N_KV, 2 * BLK, 8 * BLK), F32), _sds((N_KV, 1, 8 * BLK), F32), _sds((8, LANES), F32)],
        [z, z, z, z, z, z, qg2, kg2, sink_rows, bias, dout], sem=("arbitrary",), plan=plan)


def _kv_combine(dkp, dkc, dvp, dvc):
    S = dkc.shape[0]
    last = S // BLK - 1

    def body(kp_ref, kc_ref, vp_ref, vc_ref, dk_ref, dv_ref):
        more = pl.program_id(0) < last
        dk_ref[...] = (kc_ref[...] + jnp.where(more, kp_ref[...], 0.0)).astype(BF)
        dv_ref[...] = (vc_ref[...] + jnp.where(more, vp_ref[...], 0.0)).astype(BF)

    cur = pl.BlockSpec((BLK, LANES), lambda n: (n, 0))
    nxt = pl.BlockSpec((BLK, LANES), lambda n: (jnp.minimum(n + 1, last), 0))
    return _call("kv_combine", body, (S // BLK,), [nxt, cur, nxt, cur], [cur, cur],
                 [_sds((S, LANES), BF), _sds((S, LANES), BF)], [dkp, dkc, dvp, dvc], sem=("parallel",))


def _merge_fwd(mixed, attn, wpu4, wau4, z, o_ga, plan=None):
    S, PW = mixed.shape
    _, _, CS = wpu4.shape
    D = 4 * CS
    tm = _pick(S, (1024,))
    tn = 256
    nsj = CS // tn
    na = o_ga // tn
    nb = (o_ga + D) // tn

    def body(m_ref, a_ref, wp_ref, wa_ref, ga_ref, gb_ref, mg_ref, yy_ref):
        yp = jnp.dot(m_ref[...], wp_ref[...], preferred_element_type=F32)
        ya = jnp.dot(a_ref[...], wa_ref[...], preferred_element_type=F32)
        mg_ref[...] = (jax.nn.sigmoid(ga_ref[...]) * yp + jax.nn.sigmoid(gb_ref[...]) * ya).astype(BF)
        yy_ref[0] = yp.astype(BF)
        yy_ref[1] = ya.astype(BF)

    return _call("merge_fwd", body, (S // tm, D // tn),
                 [pl.BlockSpec((tm, PW), lambda i, j: (i, 0)), pl.BlockSpec((tm, ATT_W), lambda i, j: (i, 0)),
                  pl.BlockSpec((None, PW, tn), lambda i, j: (j // nsj, 0, j % nsj)),
                  pl.BlockSpec((None, ATT_W, tn), lambda i, j: (j // nsj, 0, j % nsj)),
                  pl.BlockSpec((tm, tn), lambda i, j: (i, na + j)), pl.BlockSpec((tm, tn), lambda i, j: (i, nb + j))],
                 [pl.BlockSpec((tm, tn), lambda i, j: (i, j)), pl.BlockSpec((2, tm, tn), lambda i, j: (0, i, j))],
                 [_sds((S, D), BF), _sds((2, S, D), BF)], [mixed, attn, wpu4, wau4, z, z], sem=("parallel", "parallel"), plan=plan)


def _merge_bwd(do, wo, z, o_ga, yy, plan=None):
    S, D = do.shape
    tm = _pick(S, (1024,))
    tn = 256
    na = o_ga // tn
    nb = (o_ga + D) // tn

    def epi(p, e, o, rs):
        sa = jax.nn.sigmoid(e[0][rs, :])
        sb = jax.nn.sigmoid(e[1][rs, :])
        yp = e[2][0, rs, :].astype(F32)
        ya = e[2][1, rs, :].astype(F32)
        o[0][0, rs, :] = (p * yp * sa * (1.0 - sa)).astype(BF)
        o[0][1, rs, :] = (p * ya * sb * (1.0 - sb)).astype(BF)
        o[1][0, rs, :] = (p * sa).astype(BF)
        o[1][1, rs, :] = (p * sb).astype(BF)

    pair = pl.BlockSpec((2, tm, tn), lambda i, j, k: (0, i, j))
    return _mm("merge_bwd", (S // tm, D // tn, 1), NT,
               do, pl.BlockSpec((tm, D), lambda i, j, k: (i, 0)),
               wo, pl.BlockSpec((tn, D), lambda i, j, k: (j, 0)),
               [z, z, yy], [pl.BlockSpec((tm, tn), lambda i, j, k: (i, na + j)), pl.BlockSpec((tm, tn), lambda i, j, k: (i, nb + j)), pair],
               [_sds((2, S, D), BF), _sds((2, S, D), BF)], [pair, pair], None, epi, plan)


def _mm_up_t(name, dyy, which, w4):
    _, S, D = dyy.shape
    _, K, CS = w4.shape
    tm = _pick(S, (1024,))

    def body(a_ref, b_ref, o_ref):
        p = lax.dot_general(a_ref[:, :CS], b_ref[0], NT, preferred_element_type=F32)
        for k in range(1, N_CHIPS):
            p = p + lax.dot_general(a_ref[:, k * CS:(k + 1) * CS], b_ref[k], NT, preferred_element_type=F32)
        o_ref[...] = p

    return _call(name, body, (S // tm,),
                 [pl.BlockSpec((None, tm, D), lambda i: (which, i, 0)), pl.BlockSpec((N_CHIPS, K, CS), lambda i: (0, 0, 0))],
                 [pl.BlockSpec((tm, K), lambda i: (i, 0))], [_sds((S, K), F32)], [dyy, w4], sem=("parallel",))[0]


def _mm_up_dw(name, a, dyy, which):
    _, S, D = dyy.shape
    K = a.shape[1]
    CS = D // N_CHIPS
    ts = _pick(S, (4096, 1024))
    return _mm(name, (1, N_CHIPS, S // ts), TN,
               a, pl.BlockSpec((ts, K), lambda i, j, k: (k, 0)),
               dyy, pl.BlockSpec((None, ts, CS), lambda i, j, k: (which, k, j)),
               [], [], [_sds((N_CHIPS, K, CS), BF)], [pl.BlockSpec((None, K, CS), lambda i, j, k: (j, 0, 0))], (K, CS), _store)[0]


def _adamw(w, g, m, v):
    m = ADAM_B1 * m + (1.0 - ADAM_B1) * g
    v = ADAM_B2 * v + (1.0 - ADAM_B2) * (g * g)
    m_hat = m / (1.0 - ADAM_B1 ** ADAM_STEP)
    v_hat = v / (1.0 - ADAM_B2 ** ADAM_STEP)
    delta = -ADAM_LR * (m_hat / (jnp.sqrt(v_hat) + ADAM_EPS) + ADAM_WD * w)
    return delta, m, v


def _mod_fwd(c_all, w_ada, b_sh):
    D, cols = w_ada.shape
    tn = cols // 9

    def body(c_ref, w_ref, b_ref, o_ref):
        cv = c_ref[...]
        sc = (cv * jax.nn.sigmoid(cv)).astype(BF)
        o_ref[...] = jnp.dot(sc, w_ref[...].astype(BF), preferred_element_type=F32) + b_ref[...]

    return _call("mod_fwd", body, (9,),
                 [pl.BlockSpec((N_DEV, D), lambda j: (0, 0)), pl.BlockSpec((D, tn), lambda j: (0, j)), pl.BlockSpec((1, tn), lambda j: (0, j))],
                 [pl.BlockSpec((N_DEV, tn), lambda j: (0, j))], [_sds((N_DEV, cols), F32)], [c_all, w_ada, b_sh], sem=("parallel",))[0]


def _wada_bwd(c_all, dmod_sh, w, m, v, plan=None):
    D, cols = w.shape
    tn = cols // 18

    def body(c_ref, d_ref, w_ref, m_ref, v_ref, g_ref, dl_ref, nm_ref, nv_ref):
        cv = c_ref[...]
        sc = (cv * jax.nn.sigmoid(cv)).astype(BF)
        g = lax.dot_general(sc, d_ref[...].astype(BF), TN, preferred_element_type=F32)
        g_ref[...] = g
        dl_ref[...], nm_ref[...], nv_ref[...] = _adamw(w_ref[...], g, m_ref[...], v_ref[...])

    tile = pl.BlockSpec((D, tn), lambda j: (0, j))
    out = _sds((D, cols), F32)
    return _call("wada_bwd", body, (18,),
                 [pl.BlockSpec((N_DEV, D), lambda j: (0, 0)), pl.BlockSpec((N_DEV, tn), lambda j: (0, j)), tile, tile, tile],
                 [tile] * 4, [out] * 4, [c_all, dmod_sh, w, m, v], sem=("parallel",), plan=plan)


def _adam_2d(name, w, g, m, v):
    R, C = w.shape
    tr = _row_tile(R, 256)

    def body(w_ref, g_ref, m_ref, v_ref, dl_ref, nm_ref, nv_ref):
        dl_ref[...], nm_ref[...], nv_ref[...] = _adamw(w_ref[...], g_ref[...], m_ref[...], v_ref[...])

    tile = pl.BlockSpec((tr, C), lambda i: (i, 0))
    out = _sds((R, C), F32)
    return _call(name, body, (R // tr,), [tile] * 4, [tile] * 3, [out] * 3, [w, g, m, v], sem=("parallel",))


def _small_finish(parts, w, m, v):
    _, R, C = parts.shape

    def body(p_ref, w_ref, m_ref, v_ref, g_ref, dl_ref, nm_ref, nv_ref):
        g = p_ref[0]
        for d in range(1, N_DEV):
            g = g + p_ref[d]
        g_ref[...] = g
        dl_ref[...], nm_ref[...], nv_ref[...] = _adamw(w_ref[...], g, m_ref[...], v_ref[...])

    out = _sds((R, C), F32)
    return pl.pallas_call(body, out_shape=[out] * 4, name="small_finish",
                          compiler_params=pltpu.CompilerParams(vmem_limit_bytes=VMEM_LIMIT))(parts, w, m, v)


def _my_chip():
    return 2 * lax.axis_index("x") + lax.axis_index("y")


def _cast_into_slot(name, w):
    R, C = w.shape
    tr = _row_tile(R, 256)

    def body(w_ref, o_ref):
        o_ref[...] = w_ref[...].astype(BF)

    return _call(name, body, (R // tr,), [pl.BlockSpec((tr, C), lambda i: (i, 0))],
                 [pl.BlockSpec((None, tr, C), lambda i: (_my_chip(), i, 0))], [_sds((N_CHIPS, R, C), BF)], [w], sem=("parallel",))[0]


def _add_pair(name, p, q):
    _, H, C = q.shape
    tr = _row_tile(H, 512)
    nt = H // tr

    def body(p_ref, q_ref, o_ref):
        o_ref[...] = (p_ref[...].astype(F32) + q_ref[...].astype(F32)).astype(BF)

    tile = pl.BlockSpec((None, tr, C), lambda k, i: (k, i, 0))
    return _call(name, body, (N_CHIPS, nt), [pl.BlockSpec((None, tr, C), lambda k, i: (k, lax.axis_index("c") * nt + i, 0)), tile],
                 [tile], [_sds(q.shape, BF)], [p, q], sem=("parallel", "parallel"))[0]


def _sum_chips(name, u, t):
    _, H, C = u.shape
    tr = _row_tile(H, 256)

    def body(u_ref, t_ref, o_ref):
        r = _my_chip()
        own = t_ref[...].astype(F32)
        pick = lambda k: jnp.where(r == k, own, u_ref[k].astype(F32))
        o_ref[...] = ((pick(0) + pick(1)) + pick(2)) + pick(3)

    return _call(name, body, (H // tr,),
                 [pl.BlockSpec((N_CHIPS, tr, C), lambda i: (0, i, 0)), pl.BlockSpec((None, tr, C), lambda i: (_my_chip(), i, 0))],
                 [pl.BlockSpec((None, tr, C), lambda i: (lax.axis_index("c"), i, 0))], [_sds((2, H, C), F32)], [u, t],
                 sem=("parallel",))[0]


BIG = ("gu1", "down1", "w_in", "pool_mix", "pool_up", "attn_up", "o", "gu2", "down2")
MIX = ("o", "pool_up", "attn_up", "pool_mix")
EARLY = ("down1", "w_in", "pool_mix", "pool_up", "attn_up", "o")

SCHEDULE = {
    "rms_mod_fwd1": ([("ici", ("gu1",))], []),
    "+gather_gu1_d2d": ([("d2d", ("gu1",))], []),
    "ffn1_up": ([("ici", EARLY)], []),
    "+gather_early_d2d": ([("d2d", EARLY[:1])], []),
    "ffn1_down": ([("ici", ("gu2",)), ("d2d", EARLY[1:])], []),
    "mix_in": ([("ici", ("down2",)), ("d2d", ("gu2",))], []),
    "mix_out": ([("d2d", ("down2",))], []),
    "ffn2_dwd": ([("split", ("gu2",))], [("add", ("gu2",))]),
    "ffn2_dh": ([("owners", ("gu2",)), ("split", ("down2",))], [("add", ("down2",)), ("sum", ("gu2",))]),
    "merge_bwd": ([("owners", ("down2",)), ("join", ("gu2",))], [("sum", ("down2",)), ("adam", ("gu2",))]),
    "attn_bwd": ([("split", MIX), ("join", ("down2",))], [("add", MIX), ("adam", ("down2",))]),
    "mix_dwin": ([("owners", MIX)], [("sum", MIX)]),
    "mix_dh": ([("split", ("w_in",)), ("join", MIX)], [("add", ("w_in",)), ("adam", MIX)]),
    "ffn1_dact": ([("owners", ("w_in",))], [("sum", ("w_in",))]),
    "ffn1_dwd": ([("split", ("gu1",)), ("join", ("w_in",))], [("add", ("gu1",)), ("adam", ("w_in",))]),
    "ffn1_dh": ([("owners", ("gu1",)), ("split", ("down1",))], [("add", ("down1",)), ("sum", ("gu1",))]),
    "rms_mod_bwd1": ([("owners", ("down1",)), ("join", ("gu1",))], [("sum", ("down1",)), ("adam", ("gu1",))]),
    "+join_down1": ([("join", ("down1",))], [("adam", ("down1",))]),
}


class _Plan:
    def __init__(self, w2, m2, v2, full, D, gw):
        self.w2, self.m2, self.v2, self.full, self.D, self.gw = w2, m2, v2, dict(full), D, gw
        self.part, self.got, self.sums, self.landed, self.g = {}, {}, {}, {}, {}
        self.result = {}
        self.pending = {}

    def _make(self, op, names):
        if op == "ici":
            return _gather_ici_stage([self.full[k] for k in names])
        if op == "d2d":
            return _gather_d2d_stage([self.full[k] for k in names])
        if op == "split":
            return _split_stage([self.part[k] for k in names])
        if op == "owners":
            return _owners_stage([self.sums[k] for k in names])
        return _join_stage([self.g[k] for k in names])

    def stages(self, name):
        ops = SCHEDULE.get(name, ([], []))[0]
        return [self._make(*op) for op in ops]

    def done(self, name, outs):
        ops, local = SCHEDULE[name]
        for op, res in zip(ops, outs):
            store = {"ici": self.full, "d2d": self.full, "split": self.got, "owners": self.landed, "join": self.g}[op[0]]
            store.update(zip(op[1], res))
        for op, names in local:
            for k in names:
                if op == "add":
                    self.sums[k] = _add_pair("add_pair_" + k, self.part[k], self.got[k])
                elif op == "sum":
                    self.g[k] = _sum_chips("sum_chips_" + k, self.landed[k], self.sums[k])
                else:
                    g2 = self.g[k].reshape(self.w2[k].shape)
                    self.result[k] = (g2, *_adam_2d("adam_" + k, self.w2[k], g2, self.m2[k], self.v2[k]))

    def alone(self, name):
        self.done(name, _run_stages(name[1:], self.stages(name)))

    def weight(self, k):
        D, gw, f = self.D, self.gw, self.full[k]
        if k in ("down1", "down2", "o"):
            return f.reshape(-1, D)
        if k == "pool_mix":
            return f.reshape(N_CHIPS, 4, gw // N_CHIPS, gw).transpose(1, 0, 2, 3).reshape(4, gw, gw)
        if k == "w_in":
            return f.reshape(-1, D)
        return f

    def partial(self, k, p):
        D, gw = self.D, self.gw
        if k in ("down1", "down2", "o", "w_in"):
            p = p.reshape(N_CHIPS, p.shape[0] // N_CHIPS, p.shape[1])
        elif k == "pool_mix":
            p = p.astype(BF).reshape(4, N_CHIPS, gw // N_CHIPS, gw).transpose(1, 0, 2, 3).reshape(N_CHIPS, gw, gw)
        self.part[k] = p


class _NoComm:
    def __init__(self, weights):
        self.w, self.part = weights, {}

    def stages(self, name):
        return []

    def alone(self, name):
        pass

    def weight(self, k):
        return self.w[k]

    def partial(self, k, p):
        self.part[k] = p


def _row(a, i):
    return a[i:i + 1]


def _local_step(x, target, mod, g_ffn1, g_mix, g_ffn2, pool_scale, q_gain, k_gain, sinks, rel_bias, plan):
    S, D = x.shape
    half = 0.5 * mod
    tile2 = lambda g: jnp.concatenate([g, g], axis=1)
    qg2, kg2 = tile2(q_gain), tile2(k_gain)
    sink_rows = jnp.broadcast_to(sinks.reshape(N_KV, 1, 8, 1), (N_KV, 1, 8, BLK)).reshape(N_KV, 1, 8 * BLK)
    bias = _bias_table(rel_bias).reshape(N_KV, 8, BLK, 2 * BLK).transpose(0, 3, 1, 2).reshape(N_KV, 2 * BLK, 8 * BLK)

    h1 = _rms_mod_fwd("rms_mod_fwd1", x, g_ffn1, _row(mod, 0), _row(mod, 1), plan)
    plan.alone("+gather_gu1_d2d")
    gu1, act1 = _ffn_up("ffn1_up", h1, plan.weight("gu1"), plan)
    plan.alone("+gather_early_d2d")
    x1, f1 = _mm_residual("ffn1_down", act1, plan.weight("down1"), x, _row(half, 2), plan)
    h2 = _rms_mod_fwd("rms_mod_fwd2", x1, g_mix, _row(mod, 3), _row(mod, 4))
    w_in_t = plan.weight("w_in")
    IN_W = w_in_t.shape[0]
    PW = plan.weight("pool_up").shape[1]
    o_q, o_k = PW, PW + ATT_W
    o_ga = o_k + 2 * KV_W
    tnz = _pick(IN_W, (1280, 256))
    tmz = _pick(S, (1024,))
    z = _mm("mix_in", (S // tmz, IN_W // tnz, 1), NT, h2, pl.BlockSpec((tmz, D), lambda i, j, k: (i, 0)),
            w_in_t, pl.BlockSpec((tnz, D), lambda i, j, k: (j, 0)), [], [], [_sds((S, IN_W), F32)],
            [pl.BlockSpec((tmz, tnz), lambda i, j, k: (i, j))], None, _store, plan)[0]
    pooled = _pool_fwd(z, PW)
    mixed = _pool_mix(pooled, plan.weight("pool_mix"), pool_scale)
    attn = _attn_fwd(z, o_q, o_k, qg2, kg2, sink_rows, bias)
    merged, yy = _merge_fwd(mixed, attn, plan.weight("pool_up"), plan.weight("attn_up"), z, o_ga)
    x2, fo = _mm_residual("mix_out", merged, plan.weight("o"), x1, _row(mod, 5), plan)
    h3 = _rms_mod_fwd("rms_mod_fwd3", x2, g_ffn2, _row(mod, 6), _row(mod, 7))
    gu2, act2 = _ffn_up("ffn2_up", h3, plan.weight("gu2"))
    x3, f2 = _mm_residual("ffn2_down", act2, plan.weight("down2"), x2, _row(half, 8))
    dx3, df2, loss_acc = _loss_bwd(x3, target, f2, _row(half, 8))

    dgu2 = _ffn_dact("ffn2_dact", df2, plan.weight("down2"), gu2)
    plan.partial("gu2", _ffn_dwgu("ffn2_dwgu", h3, dgu2))
    plan.partial("down2", _mm_tn("ffn2_dwd", act2, df2, (1408, 512), (512,), plan))
    dh3 = _ffn_dh("ffn2_dh", dgu2, plan.weight("gu2"), plan)
    dx2, do, acc3 = _rms_mod_bwd("rms_mod_bwd3", dh3, x2, dx3, g_ffn2, _row(mod, 7), fo, _row(mod, 5))

    dgab, dyy = _merge_bwd(do, plan.weight("o"), z, o_ga, yy, plan)
    plan.partial("o", _mm_tn("mix_dwo", merged, do, (1024,), (512,), plan))
    dmixed = _mm_up_t("pool_up_t", dyy, 0, plan.weight("pool_up"))
    dattn = _mm_up_t("attn_up_t", dyy, 1, plan.weight("attn_up"))
    plan.partial("pool_up", _mm_up_dw("pool_up_dw", mixed, dyy, 0))
    plan.partial("attn_up", _mm_up_dw("attn_up_dw", attn, dyy, 1))
    dpooled, dpm, dps = _pool_mix_bwd(pooled, plan.weight("pool_mix"), pool_scale, dmixed)
    plan.partial("pool_mix", dpm)
    du_pool = _pool_bwd(dpooled)
    dq, dkp, dkc, dvp, dvc, dl, dsink, dgain = _attn_bwd(z, o_q, o_k, qg2, kg2, sink_rows, bias, dattn, plan)
    dk, dv = _kv_combine(dkp, dkc, dvp, dvc)
    drb = _rel_bias_grad(dl.reshape(N_KV, 2 * BLK, 8, BLK).transpose(0, 2, 3, 1).reshape(N_HEADS, BLK * 2 * BLK))
    dz = jnp.concatenate([du_pool, dq, dk, dv, dgab[0], dgab[1]], axis=1)
    plan.partial("w_in", _mm_tn("mix_dwin", dz, h2, (1280, 256), (512,), plan))
    tnd = _pick(D, (512,))
    dh2 = _mm("mix_dh", (S // tmz, D // tnd, 1), NN, dz, pl.BlockSpec((tmz, IN_W), lambda i, j, k: (i, 0)),
              w_in_t, pl.BlockSpec((IN_W, tnd), lambda i, j, k: (0, j)), [], [], [_sds((S, D), F32)],
              [pl.BlockSpec((tmz, tnd), lambda i, j, k: (i, j))], None, _store, plan)[0]
    dx1, df1, acc2 = _rms_mod_bwd("rms_mod_bwd2", dh2, x1, dx2, g_mix, _row(mod, 4), f1, _row(half, 2))

    dgu1 =_ffn_dact("ffn1_dact", df1, plan.weight("down1"), gu1, plan)
    plan.partial("gu1", _ffn_dwgu("ffn1_dwgu", h1, dgu1, plan))
    plan.partial("down1", _mm_tn("ffn1_dwd", act1, df1, (1408, 512), (512,), plan))
    dh1 = _ffn_dh("ffn1_dh", dgu1, plan.weight("gu1"), plan)
    grad_x, acc1 = _rms_mod_bwd("rms_mod_bwd1", dh1, x, dx1, g_ffn1, _row(mod, 1), plan=plan)

    dmod = jnp.concatenate([_row(acc1, 0), _row(acc1, 1), 0.5 * _row(acc2, 3),
                            _row(acc2, 0), _row(acc2, 1), _row(acc3, 3),
                            _row(acc3, 0), _row(acc3, 1), 0.5 * _row(loss_acc, 1)], axis=0)
    fold = lambda r: r[:, :HEAD_DIM] + r[:, HEAD_DIM:]
    small = dict(
        dmod=dmod, g_ffn1=_row(acc1, 2), g_mix=_row(acc2, 2), g_ffn2=_row(acc3, 2), pool_scale=_row(dps, 0),
        q_gain=fold(_row(dgain, 0)), k_gain=fold(_row(dgain, 1)),
        sinks=jnp.sum(dsink.reshape(N_HEADS, BLK), axis=1).reshape(1, N_HEADS), rel_bias=drb,
        loss=(0.5 / D) * jnp.sum(_row(loss_acc, 0)).reshape(1, 1))
    return grad_x, small


SMALL_ORDER = ("dmod", "g_ffn1", "g_mix", "g_ffn2", "pool_scale", "q_gain", "k_gain", "sinks", "rel_bias", "loss")


def _pack_small(vals):
    flat = jnp.concatenate([vals[k].reshape(-1) for k in SMALL_ORDER])
    n = flat.shape[0]
    rows = -(-n // (8 * LANES)) * 8
    return jnp.pad(flat, (0, rows * LANES - n)).reshape(rows, LANES)


def _unpack_small(packed, like):
    flat = packed.reshape(-1)
    out, off = {}, 0
    for k in SMALL_ORDER:
        n = int(np.prod(like[k].shape))
        out[k] = flat[off:off + n].reshape(like[k].shape)
        off += n
    return out


def kernel(x, c, w_ada, b_ada, g_ffn1, w_ffn1_gu, w_ffn1_down, g_mix, w_in, pool_mix, pool_scale, w_pool_up, q_gain, k_gain, sinks, rel_bias, w_attn_up, w_o, g_ffn2, w_ffn2_gu, w_ffn2_down, loss_target, m_w_ada, m_b_ada, m_g_ffn1, m_w_ffn1_gu, m_w_ffn1_down, m_g_mix, m_w_in, m_pool_mix, m_pool_scale, m_w_pool_up, m_q_gain, m_k_gain, m_sinks, m_rel_bias, m_w_attn_up, m_w_o, m_g_ffn2, m_w_ffn2_gu, m_w_ffn2_down, v_w_ada, v_b_ada, v_g_ffn1, v_w_ffn1_gu, v_w_ffn1_down, v_g_mix, v_w_in, v_pool_mix, v_pool_scale, v_w_pool_up, v_q_gain, v_k_gain, v_sinks, v_rel_bias, v_w_attn_up, v_w_o, v_g_ffn2, v_w_ffn2_gu, v_w_ffn2_down):
    S, D = x.shape[1], x.shape[2]
    gw = pool_mix.shape[3]
    r = 2 * lax.axis_index("x") + lax.axis_index("y")

    two_d = lambda a: a.reshape(-1, a.shape[-1])
    w_sh = dict(gu1=w_ffn1_gu, down1=w_ffn1_down, w_in=w_in, pool_mix=pool_mix, pool_up=w_pool_up, attn_up=w_attn_up, o=w_o,
                gu2=w_ffn2_gu, down2=w_ffn2_down)
    m_sh = dict(gu1=m_w_ffn1_gu, down1=m_w_ffn1_down, w_in=m_w_in, pool_mix=m_pool_mix, pool_up=m_w_pool_up, attn_up=m_w_attn_up,
                o=m_w_o, gu2=m_w_ffn2_gu, down2=m_w_ffn2_down)
    v_sh = dict(gu1=v_w_ffn1_gu, down1=v_w_ffn1_down, w_in=v_w_in, pool_mix=v_pool_mix, pool_up=v_w_pool_up, attn_up=v_w_attn_up,
                o=v_w_o, gu2=v_w_ffn2_gu, down2=v_w_ffn2_down)
    view = lambda k, a: two_d(a).T if k == "w_in" else two_d(a)
    unview = lambda k, a: (a.T if k == "w_in" else a).reshape(w_sh[k].shape)
    w2 = {k: view(k, w_sh[k]) for k in BIG}
    full = {k: _cast_into_slot("cast_" + k, w2[k]) for k in BIG}
    plan = _Plan(w2, {k: view(k, m_sh[k]) for k in BIG}, {k: view(k, v_sh[k]) for k in BIG}, full, D, gw)

    c_all = _gather_all("gather_c", jnp.broadcast_to(c, (8, D)))[:, 0, :]
    cols = w_ada.shape[2]
    b_sh = lax.dynamic_slice(b_ada, (0, r * cols), (1, cols))
    mod_cols = _mod_fwd(c_all, w_ada[0], b_sh)
    mod_all = _chip_exchange("mod_exchange", mod_cols)
    me = 4 * lax.axis_index("x") + 2 * lax.axis_index("y") + lax.axis_index("c")
    mod = lax.dynamic_slice(mod_all, (0, me, 0), (N_CHIPS, 1, cols)).reshape(9, D)

    grad_x, small = _local_step(x[0], loss_target[0], mod, g_ffn1, g_mix, g_ffn2, pool_scale, q_gain, k_gain,
                                sinks, rel_bias, plan)

    small_w = dict(dmod=b_ada, g_ffn1=g_ffn1, g_mix=g_mix, g_ffn2=g_ffn2, pool_scale=pool_scale, q_gain=q_gain, k_gain=k_gain,
                   sinks=sinks, rel_bias=rel_bias, loss=jnp.zeros((1, 1), F32))
    small_m = dict(dmod=m_b_ada, g_ffn1=m_g_ffn1, g_mix=m_g_mix, g_ffn2=m_g_ffn2, pool_scale=m_pool_scale, q_gain=m_q_gain,
                   k_gain=m_k_gain, sinks=m_sinks, rel_bias=m_rel_bias, loss=jnp.zeros((1, 1), F32))
    small_v = dict(dmod=v_b_ada, g_ffn1=v_g_ffn1, g_mix=v_g_mix, g_ffn2=v_g_ffn2, pool_scale=v_pool_scale, q_gain=v_q_gain,
                   k_gain=v_k_gain, sinks=v_sinks, rel_bias=v_rel_bias, loss=jnp.ones((1, 1), F32))
    small_all = _gather_all("gather_small", _pack_small(small))
    sg, sd, sm, sv = [_unpack_small(a, small_w) for a in
                      _small_finish(small_all, _pack_small(small_w), _pack_small(small_m), _pack_small(small_v))]
    loss = sg["loss"].reshape(())

    dmod_all = small_all.reshape(N_DEV, -1)[:, :9 * D]
    dmod_sh = lax.dynamic_slice(dmod_all, (0, r * cols), (N_DEV, cols))
    g_ada, d_ada, nm_ada, nv_ada = _wada_bwd(c_all, dmod_sh, w_ada[0], m_w_ada[0], v_w_ada[0])
    plan.alone("+join_down1")

    big = [{k: unview(k, plan.result[k][i]) for k in BIG} for i in range(4)]

    def ordered(b, ada, sm_):
        return (ada[None], sm_["dmod"], sm_["g_ffn1"], b["gu1"], b["down1"], sm_["g_mix"], b["w_in"], b["pool_mix"],
                sm_["pool_scale"], b["pool_up"], sm_["q_gain"], sm_["k_gain"], sm_["sinks"], sm_["rel_bias"], b["attn_up"],
                b["o"], sm_["g_ffn2"], b["gu2"], b["down2"])

    return (loss, grad_x[None], *ordered(big[0], g_ada, sg), *ordered(big[1], d_ada, sd), *ordered(big[2], nm_ada, sm),
            *ordered(big[3], nv_ada, sv))
```
